```python
import jax, jax.numpy as jnp
from jax import lax
import numpy as np

D_MODEL = 2048
BATCH = 8
SEQ = 8192
DEPTH = 1

N_META = 16
BLOCK = 128
PAD_LEN = BLOCK - N_META
MLA_HEADS = 16
Q_LORA_RANK = 1536
KV_LORA_RANK = 512
QK_NOPE_DIM = 128
QK_ROPE_DIM = 64
QK_HEAD_DIM = QK_NOPE_DIM + QK_ROPE_DIM
V_HEAD_DIM = 128
MLA_WIDTH = MLA_HEADS * V_HEAD_DIM
ROPE_THETA = 10000.0
HGRN_HEADS = 16
HGRN_EXPAND = 128
HGRN_KEY_WIDTH = HGRN_HEADS * HGRN_EXPAND
HGRN_V_DIM = D_MODEL // HGRN_HEADS
HGRN_V_WIDTH = HGRN_HEADS * HGRN_V_DIM
D_FF = 5632
CONV_WIDTH = 3
NORM_EPS = 1e-6
IN_WIDTHS = (Q_LORA_RANK, KV_LORA_RANK, QK_ROPE_DIM,
             HGRN_KEY_WIDTH, HGRN_KEY_WIDTH, HGRN_V_WIDTH,
             HGRN_V_WIDTH,
             D_MODEL, D_MODEL)
IN_COLS = sum(IN_WIDTHS)

kernel_name = "hybrid_mla_hgrn2_convffn_block"


def _rms_norm(x, g):
    xf = x.astype(jnp.float32)
    y = xf * lax.rsqrt(jnp.mean(xf * xf, axis=-1, keepdims=True) + NORM_EPS)
    return (y * g.astype(jnp.float32)).astype(x.dtype)


def _rope_tables(pos):
    inv = 1.0 / (ROPE_THETA ** (jnp.arange(0, QK_ROPE_DIM, 2, dtype=jnp.float32) / QK_ROPE_DIM))
    ang = pos.astype(jnp.float32)[..., None] * inv
    ang = jnp.concatenate([ang, ang], axis=-1)
    return jnp.cos(ang), jnp.sin(ang)


def _apply_rope(x, cos, sin):
    xf = x.astype(jnp.float32)
    x1, x2 = jnp.split(xf, 2, axis=-1)
    rot = jnp.concatenate([-x2, x1], axis=-1)
    return (xf * cos + rot * sin).astype(x.dtype)


def _mla_attention(q_nope, q_rope, k_nope, k_rope, v, valid):
    B, L, H, _ = q_nope.shape
    n_blocks = L // BLOCK
    scale = QK_HEAD_DIM ** -0.5
    key_idx = jnp.arange(L)

    def one_block(i):
        start = i * BLOCK
        qn = lax.dynamic_slice_in_dim(q_nope, start, BLOCK, axis=1)
        qr = lax.dynamic_slice_in_dim(q_rope, start, BLOCK, axis=1)
        s = (jnp.einsum('bqhd,bkhd->bhqk', qn, k_nope, preferred_element_type=jnp.float32)
             + jnp.einsum('bqhr,bkr->bhqk', qr, k_rope, preferred_element_type=jnp.float32)) * scale
        q_idx = start + jnp.arange(BLOCK)
        causal = key_idx[None, :] <= q_idx[:, None]
        visible = valid[None, :] | (key_idx[None, :] == q_idx[:, None])
        s = jnp.where(causal & visible, s, -jnp.inf)
        p = jax.nn.softmax(s, axis=-1)
        return jnp.einsum('bhqk,bkhd->bqhd', p.astype(v.dtype), v)

    out = lax.map(one_block, jnp.arange(n_blocks))
    return jnp.transpose(out, (1, 0, 2, 3, 4)).reshape(B, L, H * v.shape[-1])


def _hgrn2_chunked(q, k, v, log_f):
    B, L, H, DK = q.shape
    DV = v.shape[-1]
    n_chunks = L // BLOCK

    def to_chunks(t):
        return jnp.transpose(t.reshape(B, n_chunks, BLOCK, H, t.shape[-1]), (1, 0, 3, 2, 4))

    causal = jnp.tril(jnp.ones((BLOCK, BLOCK), dtype=bool))

    def step(S, inp):
        qc, kc, vc, gc = inp
        b = jnp.cumsum(gc, axis=2)
        diff = b[:, :, :, None, :] - b[:, :, None, :, :]
        decay = jnp.exp(jnp.where(causal[:, :, None], diff, -jnp.inf))
        scores = jnp.einsum('bhtd,bhtsd,bhsd->bhts', qc, decay, kc)
        o = (jnp.einsum('bhts,bhse->bhte', scores, vc)
             + jnp.einsum('bhtd,bhde->bhte', qc * jnp.exp(b), S))
        b_last = b[:, :, -1:, :]
        S = (jnp.exp(b_last[:, :, 0, :])[..., None] * S
             + jnp.einsum('bhsd,bhse->bhde', kc * jnp.exp(b_last - b), vc))
        return S, o

    S0 = jnp.zeros((B, H, DK, DV), jnp.float32)
    _, o = lax.scan(step, S0, (to_chunks(q), to_chunks(k), to_chunks(v), to_chunks(log_f)))
    return jnp.transpose(o, (1, 0, 3, 2, 4)).reshape(B, L, H, DV)


def _fwd_setup_inputs(seed: int = 0) -> dict:
    key = jax.random.key(seed)
    ks = jax.random.split(key, 24)
    f32 = jnp.float32

    def w(k, shape, fan_in):
        return jax.random.normal(k, shape, f32) * (fan_in ** -0.5)

    def gain(k, shape):
        return 1.0 + 0.02 * jax.random.normal(k, shape, f32)

    return {
        "x": jax.random.normal(ks[0], (BATCH, SEQ, D_MODEL), f32),
        "positions": jnp.broadcast_to(jnp.arange(SEQ, dtype=jnp.int32)[None, :], (BATCH, SEQ)),
        "meta_tokens": jax.random.normal(ks[1], (N_META, D_MODEL), f32),
        "w_in": w(ks[2], (DEPTH, D_MODEL, IN_COLS), D_MODEL),
        "w_q_up": w(ks[3], (DEPTH, Q_LORA_RANK, MLA_HEADS * QK_HEAD_DIM), Q_LORA_RANK),
        "w_kv_up": w(ks[4], (DEPTH, KV_LORA_RANK, MLA_HEADS * (QK_NOPE_DIM + V_HEAD_DIM)), KV_LORA_RANK),
        "w_branch_mla": w(ks[5], (DEPTH, MLA_WIDTH, D_MODEL), MLA_WIDTH),
        "w_branch_hgrn": w(ks[6], (DEPTH, HGRN_V_WIDTH, D_MODEL), HGRN_V_WIDTH),
        "w_out": w(ks[7], (DEPTH, D_MODEL, D_MODEL), D_MODEL),
        "w_ffn_in": w(ks[8], (DEPTH, D_MODEL, 2 * D_FF), D_MODEL),
        "w_ffn_out": w(ks[9], (DEPTH, D_FF, D_MODEL), D_FF),
        "conv_w": w(ks[10], (DEPTH, CONV_WIDTH, D_FF), CONV_WIDTH),
        "conv_b": 0.01 * jax.random.normal(ks[11], (DEPTH, D_FF), f32),
        "g_mix_norm": gain(ks[12], (DEPTH, D_MODEL)),
        "g_q_norm": gain(ks[13], (DEPTH, Q_LORA_RANK)),
        "g_kv_norm": gain(ks[14], (DEPTH, KV_LORA_RANK)),
        "g_hgrn_norm": gain(ks[15], (DEPTH, HGRN_V_DIM)),
        "g_ffn_norm": gain(ks[16], (DEPTH, D_MODEL)),
        "g_final_norm": gain(ks[17], (D_MODEL,)),
        "lb_raw": 1.0 + 0.1 * jax.random.normal(ks[18], (DEPTH + 1, HGRN_KEY_WIDTH), f32),
    }


def _fwd_reference(x, positions, meta_tokens, w_in, w_q_up, w_kv_up, w_branch_mla, w_branch_hgrn,
              w_out, w_ffn_in, w_ffn_out, conv_w, conv_b, g_mix_norm, g_q_norm, g_kv_norm,
              g_hgrn_norm, g_ffn_norm, g_final_norm, lb_raw):
    B, S, D = x.shape
    dt = x.dtype
    prefix = PAD_LEN + N_META
    L = prefix + S

    h = jnp.concatenate([jnp.zeros((B, PAD_LEN, D), dt),
                         jnp.broadcast_to(meta_tokens.astype(dt)[None], (B, N_META, D)),
                         x], axis=1)
    valid = jnp.arange(L) >= PAD_LEN
    pos = jnp.concatenate([jnp.zeros((B, PAD_LEN), jnp.int32),
                           jnp.broadcast_to(jnp.arange(N_META, dtype=jnp.int32)[None], (B, N_META)),
                           positions.astype(jnp.int32) + N_META], axis=1)
    cos, sin = _rope_tables(pos)
    split_pts = np.cumsum(IN_WIDTHS)[:-1].tolist()
    lb_all = jnp.cumsum(jax.nn.softmax(lb_raw.astype(jnp.float32), axis=0), axis=0)

    for layer in range(DEPTH):
        u = _rms_norm(h, g_mix_norm[layer])
        proj = u @ w_in[layer]
        q_lat, kv_lat, k_rope, hq, hf, hi, hg, gate_a, gate_b = jnp.split(proj, split_pts, axis=-1)

        q = (_rms_norm(q_lat, g_q_norm[layer]) @ w_q_up[layer]).reshape(B, L, MLA_HEADS, QK_HEAD_DIM)
        q_nope, q_rope = q[..., :QK_NOPE_DIM], q[..., QK_NOPE_DIM:]
        kv = (_rms_norm(kv_lat, g_kv_norm[layer]) @ w_kv_up[layer]).reshape(
            B, L, MLA_HEADS, QK_NOPE_DIM + V_HEAD_DIM)
        k_nope, v_mla = kv[..., :QK_NOPE_DIM], kv[..., QK_NOPE_DIM:]
        q_rope = _apply_rope(q_rope, cos[:, :, None, :], sin[:, :, None, :])
        k_rope = _apply_rope(k_rope, cos, sin)
        o_mla = _mla_attention(q_nope, q_rope, k_nope, k_rope, v_mla, valid)

        lb = lb_all[layer]
        f = lb + (1.0 - lb) * jax.nn.sigmoid(hf.astype(jnp.float32))
        vmask = valid[None, :, None]
        log_f = jnp.where(vmask, jnp.log(f), 0.0)
        k_in = jnp.where(vmask, 1.0 - f, 0.0)
        q_h = jax.nn.silu(hq.astype(jnp.float32))
        o_h = _hgrn2_chunked(q_h.reshape(B, L, HGRN_HEADS, HGRN_EXPAND),
                             k_in.reshape(B, L, HGRN_HEADS, HGRN_EXPAND),
                             hi.astype(jnp.float32).reshape(B, L, HGRN_HEADS, HGRN_V_DIM),
                             log_f.reshape(B, L, HGRN_HEADS, HGRN_EXPAND))
        o_h = _rms_norm(o_h, g_hgrn_norm[layer]) * jax.nn.silu(
            hg.astype(jnp.float32).reshape(B, L, HGRN_HEADS, HGRN_V_DIM))
        o_hgrn = o_h.reshape(B, L, HGRN_V_WIDTH).astype(dt)

        merged = (jax.nn.sigmoid(gate_a) * (o_mla @ w_branch_mla[layer])
                  + jax.nn.sigmoid(gate_b) * (o_hgrn @ w_branch_hgrn[layer]))
        h = h + merged @ w_out[layer]

        u = _rms_norm(h, g_ffn_norm[layer])
        gate, up = jnp.split(u @ w_ffn_in[layer], 2, axis=-1)
        gate = jnp.where(vmask, gate, 0.0)
        gp = jnp.pad(gate, ((0, 0), (CONV_WIDTH - 1, 0), (0, 0)))
        cw = conv_w[layer]
        conv = (cw[0] * gp[:, :-2] + cw[1] * gp[:, 1:-1] + cw[2] * gp[:, 2:]) + conv_b[layer]
        h = h + (jax.nn.silu(conv) * up) @ w_ffn_out[layer]

    out = _rms_norm(h, g_final_norm)
    return out[:, prefix:, :]


import jax as _jax
import jax.numpy as _jnp

TWIN_FORMAT = 'train_step'
FWD_PARAMS = ['x', 'positions', 'meta_tokens', 'w_in', 'w_q_up', 'w_kv_up', 'w_branch_mla', 'w_branch_hgrn', 'w_out', 'w_ffn_in', 'w_ffn_out', 'conv_w', 'conv_b', 'g_mix_norm', 'g_q_norm', 'g_kv_norm', 'g_hgrn_norm', 'g_ffn_norm', 'g_final_norm', 'lb_raw']
TWIN_WEIGHTS = ['meta_tokens', 'w_in', 'w_q_up', 'w_kv_up', 'w_branch_mla', 'w_branch_hgrn', 'w_out', 'w_ffn_in', 'w_ffn_out', 'conv_w', 'conv_b', 'g_mix_norm', 'g_q_norm', 'g_kv_norm', 'g_hgrn_norm', 'g_ffn_norm', 'g_final_norm', 'lb_raw']
TWIN_DIFF_INPUT = 'x'
TWIN_INPUTS = ['x', 'positions', 'meta_tokens', 'w_in', 'w_q_up', 'w_kv_up', 'w_branch_mla', 'w_branch_hgrn', 'w_out', 'w_ffn_in', 'w_ffn_out', 'conv_w', 'conv_b', 'g_mix_norm', 'g_q_norm', 'g_kv_norm', 'g_hgrn_norm', 'g_ffn_norm', 'g_final_norm', 'lb_raw', 'loss_target', 'm_meta_tokens', 'm_w_in', 'm_w_q_up', 'm_w_kv_up', 'm_w_branch_mla', 'm_w_branch_hgrn', 'm_w_out', 'm_w_ffn_in', 'm_w_ffn_out', 'm_conv_w', 'm_conv_b', 'm_g_mix_norm', 'm_g_q_norm', 'm_g_kv_norm', 'm_g_hgrn_norm', 'm_g_ffn_norm', 'm_g_final_norm', 'm_lb_raw', 'v_meta_tokens', 'v_w_in', 'v_w_q_up', 'v_w_kv_up', 'v_w_branch_mla', 'v_w_branch_hgrn', 'v_w_out', 'v_w_ffn_in', 'v_w_ffn_out', 'v_conv_w', 'v_conv_b', 'v_g_mix_norm', 'v_g_q_norm', 'v_g_kv_norm', 'v_g_hgrn_norm', 'v_g_ffn_norm', 'v_g_final_norm', 'v_lb_raw']
TWIN_OUTPUTS = ['loss', 'grad_x', 'grad_meta_tokens', 'grad_w_in', 'grad_w_q_up', 'grad_w_kv_up', 'grad_w_branch_mla', 'grad_w_branch_hgrn', 'grad_w_out', 'grad_w_ffn_in', 'grad_w_ffn_out', 'grad_conv_w', 'grad_conv_b', 'grad_g_mix_norm', 'grad_g_q_norm', 'grad_g_kv_norm', 'grad_g_hgrn_norm', 'grad_g_ffn_norm', 'grad_g_final_norm', 'grad_lb_raw', 'delta_meta_tokens', 'delta_w_in', 'delta_w_q_up', 'delta_w_kv_up', 'delta_w_branch_mla', 'delta_w_branch_hgrn', 'delta_w_out', 'delta_w_ffn_in', 'delta_w_ffn_out', 'delta_conv_w', 'delta_conv_b', 'delta_g_mix_norm', 'delta_g_q_norm', 'delta_g_kv_norm', 'delta_g_hgrn_norm', 'delta_g_ffn_norm', 'delta_g_final_norm', 'delta_lb_raw', 'new_m_meta_tokens', 'new_m_w_in', 'new_m_w_q_up', 'new_m_w_kv_up', 'new_m_w_branch_mla', 'new_m_w_branch_hgrn', 'new_m_w_out', 'new_m_w_ffn_in', 'new_m_w_ffn_out', 'new_m_conv_w', 'new_m_conv_b', 'new_m_g_mix_norm', 'new_m_g_q_norm', 'new_m_g_kv_norm', 'new_m_g_hgrn_norm', 'new_m_g_ffn_norm', 'new_m_g_final_norm', 'new_m_lb_raw', 'new_v_meta_tokens', 'new_v_w_in', 'new_v_w_q_up', 'new_v_w_kv_up', 'new_v_w_branch_mla', 'new_v_w_branch_hgrn', 'new_v_w_out', 'new_v_w_ffn_in', 'new_v_w_ffn_out', 'new_v_conv_w', 'new_v_conv_b', 'new_v_g_mix_norm', 'new_v_g_q_norm', 'new_v_g_kv_norm', 'new_v_g_hgrn_norm', 'new_v_g_ffn_norm', 'new_v_g_final_norm', 'new_v_lb_raw']
TWIN_LEAF_KINDS = {'loss': 'loss', 'grad_x': 'grad_x', 'grad_meta_tokens': 'grad_w', 'grad_w_in': 'grad_w', 'grad_w_q_up': 'grad_w', 'grad_w_kv_up': 'grad_w', 'grad_w_branch_mla': 'grad_w', 'grad_w_branch_hgrn': 'grad_w', 'grad_w_out': 'grad_w', 'grad_w_ffn_in': 'grad_w', 'grad_w_ffn_out': 'grad_w', 'grad_conv_w': 'grad_w', 'grad_conv_b': 'grad_w', 'grad_g_mix_norm': 'grad_w', 'grad_g_q_norm': 'grad_w', 'grad_g_kv_norm': 'grad_w', 'grad_g_hgrn_norm': 'grad_w', 'grad_g_ffn_norm': 'grad_w', 'grad_g_final_norm': 'grad_w', 'grad_lb_raw': 'grad_w', 'delta_meta_tokens': 'delta_w', 'delta_w_in': 'delta_w', 'delta_w_q_up': 'delta_w', 'delta_w_kv_up': 'delta_w', 'delta_w_branch_mla': 'delta_w', 'delta_w_branch_hgrn': 'delta_w', 'delta_w_out': 'delta_w', 'delta_w_ffn_in': 'delta_w', 'delta_w_ffn_out': 'delta_w', 'delta_conv_w': 'delta_w', 'delta_conv_b': 'delta_w', 'delta_g_mix_norm': 'delta_w', 'delta_g_q_norm': 'delta_w', 'delta_g_kv_norm': 'delta_w', 'delta_g_hgrn_norm': 'delta_w', 'delta_g_ffn_norm': 'delta_w', 'delta_g_final_norm': 'delta_w', 'delta_lb_raw': 'delta_w', 'new_m_meta_tokens': 'new_m', 'new_m_w_in': 'new_m', 'new_m_w_q_up': 'new_m', 'new_m_w_kv_up': 'new_m', 'new_m_w_branch_mla': 'new_m', 'new_m_w_branch_hgrn': 'new_m', 'new_m_w_out': 'new_m', 'new_m_w_ffn_in': 'new_m', 'new_m_w_ffn_out': 'new_m', 'new_m_conv_w': 'new_m', 'new_m_conv_b': 'new_m', 'new_m_g_mix_norm': 'new_m', 'new_m_g_q_norm': 'new_m', 'new_m_g_kv_norm': 'new_m', 'new_m_g_hgrn_norm': 'new_m', 'new_m_g_ffn_norm': 'new_m', 'new_m_g_final_norm': 'new_m', 'new_m_lb_raw': 'new_m', 'new_v_meta_tokens': 'new_v', 'new_v_w_in': 'new_v', 'new_v_w_q_up': 'new_v', 'new_v_w_kv_up': 'new_v', 'new_v_w_branch_mla': 'new_v', 'new_v_w_branch_hgrn': 'new_v', 'new_v_w_out': 'new_v', 'new_v_w_ffn_in': 'new_v', 'new_v_w_ffn_out': 'new_v', 'new_v_conv_w': 'new_v', 'new_v_conv_b': 'new_v', 'new_v_g_mix_norm': 'new_v', 'new_v_g_q_norm': 'new_v', 'new_v_g_kv_norm': 'new_v', 'new_v_g_hgrn_norm': 'new_v', 'new_v_g_ffn_norm': 'new_v', 'new_v_g_final_norm': 'new_v', 'new_v_lb_raw': 'new_v'}


def _forward(args):
    return _fwd_reference(*[args[k] for k in FWD_PARAMS])


def _output_shape():
    def fwd():
        inp = _fwd_setup_inputs(0)
        return _fwd_reference(*[inp[k] for k in FWD_PARAMS])
    out = _jax.eval_shape(fwd)
    return out.shape, out.dtype

N_MICROBATCH = 1
ADAM_LR = 0.001
ADAM_B1 = 0.9
ADAM_B2 = 0.999
ADAM_EPS = 1e-08
ADAM_WD = 0.01
ADAM_STEP = 10
PER_EXAMPLE_BATCH_AXIS = {'x': 0, 'positions': 0, 'loss_target': 0}
SHARED_INPUTS = []
_WEIGHT_DTYPES = {'meta_tokens': _jnp.float32, 'w_in': _jnp.float32, 'w_q_up': _jnp.float32, 'w_kv_up': _jnp.float32, 'w_branch_mla': _jnp.float32, 'w_branch_hgrn': _jnp.float32, 'w_out': _jnp.float32, 'w_ffn_in': _jnp.float32, 'w_ffn_out': _jnp.float32, 'conv_w': _jnp.float32, 'conv_b': _jnp.float32, 'g_mix_norm': _jnp.float32, 'g_q_norm': _jnp.float32, 'g_kv_norm': _jnp.float32, 'g_hgrn_norm': _jnp.float32, 'g_ffn_norm': _jnp.float32, 'g_final_norm': _jnp.float32, 'lb_raw': _jnp.float32}
MOMENT_SCALE = {'meta_tokens': 2.525388e-03, 'w_in': 2.564447e-02, 'w_q_up': 9.302085e-03, 'w_kv_up': 1.090807e-02, 'w_branch_mla': 1.217393e-02, 'w_branch_hgrn': 4.344545e-02, 'w_out': 4.481821e-02, 'w_ffn_in': 3.864389e-02, 'w_ffn_out': 6.299280e-02, 'conv_w': 3.956224e-02, 'conv_b': 3.795683e-02, 'g_mix_norm': 6.915006e-02, 'g_q_norm': 1.311435e-02, 'g_kv_norm': 3.180984e-02, 'g_hgrn_norm': 1.920104e-01, 'g_ffn_norm': 9.316482e-02, 'g_final_norm': 3.196720e+01, 'lb_raw': 4.223065e-03}


def _to_microbatches(a, axis):
    t = _jnp.moveaxis(a, axis, 0)
    t = t.reshape((N_MICROBATCH, t.shape[0] // N_MICROBATCH) + t.shape[1:])
    return _jnp.moveaxis(t, 1, axis + 1)


def setup_inputs(seed: int = 0) -> dict:
    inp = _fwd_setup_inputs(seed)
    key = _jax.random.fold_in(_jax.random.key(seed), 7919)
    shape, _ = _output_shape()
    out = dict(inp)
    out["loss_target"] = _jax.random.normal(_jax.random.fold_in(key, 0), shape, _jnp.float32)
    for i, name in enumerate(TWIN_WEIGHTS):
        w = inp[name].astype(_jnp.float32)
        if MOMENT_SCALE is None:
            s = _jnp.sqrt(_jnp.mean(_jnp.square(w)) + 1e-30)
        else:
            s = MOMENT_SCALE[name]
        km, kv = _jax.random.split(_jax.random.fold_in(key, i + 1))
        out[name] = w
        out["m_" + name] = s * _jax.random.normal(km, w.shape, _jnp.float32)
        out["v_" + name] = (s * s) * _jax.random.uniform(kv, w.shape, _jnp.float32, 0.5, 1.5)
    if N_MICROBATCH > 1:
        for name, axis in PER_EXAMPLE_BATCH_AXIS.items():
            out[name] = _to_microbatches(out[name], axis)
    return {'x': out['x'], 'positions': out['positions'], 'meta_tokens': out['meta_tokens'], 'w_in': out['w_in'], 'w_q_up': out['w_q_up'], 'w_kv_up': out['w_kv_up'], 'w_branch_mla': out['w_branch_mla'], 'w_branch_hgrn': out['w_branch_hgrn'], 'w_out': out['w_out'], 'w_ffn_in': out['w_ffn_in'], 'w_ffn_out': out['w_ffn_out'], 'conv_w': out['conv_w'], 'conv_b': out['conv_b'], 'g_mix_norm': out['g_mix_norm'], 'g_q_norm': out['g_q_norm'], 'g_kv_norm': out['g_kv_norm'], 'g_hgrn_norm': out['g_hgrn_norm'], 'g_ffn_norm': out['g_ffn_norm'], 'g_final_norm': out['g_final_norm'], 'lb_raw': out['lb_raw'], 'loss_target': out['loss_target'], 'm_meta_tokens': out['m_meta_tokens'], 'm_w_in': out['m_w_in'], 'm_w_q_up': out['m_w_q_up'], 'm_w_kv_up': out['m_w_kv_up'], 'm_w_branch_mla': out['m_w_branch_mla'], 'm_w_branch_hgrn': out['m_w_branch_hgrn'], 'm_w_out': out['m_w_out'], 'm_w_ffn_in': out['m_w_ffn_in'], 'm_w_ffn_out': out['m_w_ffn_out'], 'm_conv_w': out['m_conv_w'], 'm_conv_b': out['m_conv_b'], 'm_g_mix_norm': out['m_g_mix_norm'], 'm_g_q_norm': out['m_g_q_norm'], 'm_g_kv_norm': out['m_g_kv_norm'], 'm_g_hgrn_norm': out['m_g_hgrn_norm'], 'm_g_ffn_norm': out['m_g_ffn_norm'], 'm_g_final_norm': out['m_g_final_norm'], 'm_lb_raw': out['m_lb_raw'], 'v_meta_tokens': out['v_meta_tokens'], 'v_w_in': out['v_w_in'], 'v_w_q_up': out['v_w_q_up'], 'v_w_kv_up': out['v_w_kv_up'], 'v_w_branch_mla': out['v_w_branch_mla'], 'v_w_branch_hgrn': out['v_w_branch_hgrn'], 'v_w_out': out['v_w_out'], 'v_w_ffn_in': out['v_w_ffn_in'], 'v_w_ffn_out': out['v_w_ffn_out'], 'v_conv_w': out['v_conv_w'], 'v_conv_b': out['v_conv_b'], 'v_g_mix_norm': out['v_g_mix_norm'], 'v_g_q_norm': out['v_g_q_norm'], 'v_g_kv_norm': out['v_g_kv_norm'], 'v_g_hgrn_norm': out['v_g_hgrn_norm'], 'v_g_ffn_norm': out['v_g_ffn_norm'], 'v_g_final_norm': out['v_g_final_norm'], 'v_lb_raw': out['v_lb_raw']}


def _loss(weights, diff, rest, loss_target):
    with _jax.named_scope("forward"):
        args = {**rest, TWIN_DIFF_INPUT: diff, **{k: w.astype(_WEIGHT_DTYPES[k]) for k, w in weights.items()}}
        y = _forward(args)
    with _jax.named_scope("loss_head"):
        err = _jnp.square(y.astype(_jnp.float32) - loss_target)
        return 0.5 * _jnp.sum(_jnp.mean(err, axis=-1)) if err.ndim else 0.5 * err


def _adamw(w, g, m, v):
    m = ADAM_B1 * m + (1.0 - ADAM_B1) * g
    v = ADAM_B2 * v + (1.0 - ADAM_B2) * _jnp.square(g)
    m_hat = m / (1.0 - ADAM_B1 ** ADAM_STEP)
    v_hat = v / (1.0 - ADAM_B2 ** ADAM_STEP)
    delta = -ADAM_LR * (m_hat / (_jnp.sqrt(v_hat) + ADAM_EPS) + ADAM_WD * w)
    return delta, m, v


def reference(x, positions, meta_tokens, w_in, w_q_up, w_kv_up, w_branch_mla, w_branch_hgrn, w_out, w_ffn_in, w_ffn_out, conv_w, conv_b, g_mix_norm, g_q_norm, g_kv_norm, g_hgrn_norm, g_ffn_norm, g_final_norm, lb_raw, loss_target, m_meta_tokens, m_w_in, m_w_q_up, m_w_kv_up, m_w_branch_mla, m_w_branch_hgrn, m_w_out, m_w_ffn_in, m_w_ffn_out, m_conv_w, m_conv_b, m_g_mix_norm, m_g_q_norm, m_g_kv_norm, m_g_hgrn_norm, m_g_ffn_norm, m_g_final_norm, m_lb_raw, v_meta_tokens, v_w_in, v_w_q_up, v_w_kv_up, v_w_branch_mla, v_w_branch_hgrn, v_w_out, v_w_ffn_in, v_w_ffn_out, v_conv_w, v_conv_b, v_g_mix_norm, v_g_q_norm, v_g_kv_norm, v_g_hgrn_norm, v_g_ffn_norm, v_g_final_norm, v_lb_raw):
    given = dict(x=x, positions=positions, meta_tokens=meta_tokens, w_in=w_in, w_q_up=w_q_up, w_kv_up=w_kv_up, w_branch_mla=w_branch_mla, w_branch_hgrn=w_branch_hgrn, w_out=w_out, w_ffn_in=w_ffn_in, w_ffn_out=w_ffn_out, conv_w=conv_w, conv_b=conv_b, g_mix_norm=g_mix_norm, g_q_norm=g_q_norm, g_kv_norm=g_kv_norm, g_hgrn_norm=g_hgrn_norm, g_ffn_norm=g_ffn_norm, g_final_norm=g_final_norm, lb_raw=lb_raw, loss_target=loss_target, m_meta_tokens=m_meta_tokens, m_w_in=m_w_in, m_w_q_up=m_w_q_up, m_w_kv_up=m_w_kv_up, m_w_branch_mla=m_w_branch_mla, m_w_branch_hgrn=m_w_branch_hgrn, m_w_out=m_w_out, m_w_ffn_in=m_w_ffn_in, m_w_ffn_out=m_w_ffn_out, m_conv_w=m_conv_w, m_conv_b=m_conv_b, m_g_mix_norm=m_g_mix_norm, m_g_q_norm=m_g_q_norm, m_g_kv_norm=m_g_kv_norm, m_g_hgrn_norm=m_g_hgrn_norm, m_g_ffn_norm=m_g_ffn_norm, m_g_final_norm=m_g_final_norm, m_lb_raw=m_lb_raw, v_meta_tokens=v_meta_tokens, v_w_in=v_w_in, v_w_q_up=v_w_q_up, v_w_kv_up=v_w_kv_up, v_w_branch_mla=v_w_branch_mla, v_w_branch_hgrn=v_w_branch_hgrn, v_w_out=v_w_out, v_w_ffn_in=v_w_ffn_in, v_w_ffn_out=v_w_ffn_out, v_conv_w=v_conv_w, v_conv_b=v_conv_b, v_g_mix_norm=v_g_mix_norm, v_g_q_norm=v_g_q_norm, v_g_kv_norm=v_g_kv_norm, v_g_hgrn_norm=v_g_hgrn_norm, v_g_ffn_norm=v_g_ffn_norm, v_g_final_norm=v_g_final_norm, v_lb_raw=v_lb_raw)
    weights = {n: given[n] for n in TWIN_WEIGHTS}
    shared = {n: given[n] for n in SHARED_INPUTS}
    per_example = {n: given[n] for n in ['x', 'positions']}
    grad_fn = _jax.value_and_grad(_loss, argnums=(0, 1))

    def one_microbatch(ex, loss_target):
        ex = dict(ex)
        diff = ex.pop(TWIN_DIFF_INPUT)
        return grad_fn(weights, diff, {**shared, **ex}, loss_target)

    if N_MICROBATCH == 1:
        loss, (grad_w, grad_x) = one_microbatch(per_example, given["loss_target"])
    else:
        def body(carry, xs):
            loss_sum, grad_sum = carry
            l_k, (gw_k, gx_k) = one_microbatch(xs[0], xs[1])
            with _jax.named_scope("update"):
                return (loss_sum + l_k, _jax.tree.map(_jnp.add, grad_sum, gw_k)), gx_k

        init = (_jnp.zeros((), _jnp.float32), _jax.tree.map(_jnp.zeros_like, weights))
        (loss, grad_w), grad_x = _jax.lax.scan(body, init, (per_example, given["loss_target"]))
    with _jax.named_scope("update"):
        delta_w, new_m, new_v = {}, {}, {}
        for n in TWIN_WEIGHTS:
            delta_w[n], new_m[n], new_v[n] = _adamw(weights[n], grad_w[n], given["m_" + n], given["v_" + n])
    return (loss, grad_x, *[grad_w[n] for n in TWIN_WEIGHTS], *[delta_w[n] for n in TWIN_WEIGHTS],
            *[new_m[n] for n in TWIN_WEIGHTS], *[new_v[n] for n in TWIN_WEIGHTS])
```

```python
import functools

import jax
import jax.numpy as jnp
import numpy as np
from jax import lax
from jax.experimental import pallas as pl
from jax.experimental.pallas import tpu as pltpu

F32 = jnp.float32
BF16 = jnp.bfloat16

D_MODEL = 2048
N_META = 16
BLOCK = 128
PAD_LEN = BLOCK - N_META
HEADS = 16
Q_LORA = 1536
KV_LORA = 512
ROPE = 64
NOPE = 128
VDIM = 128
D_FF = 5632
NORM_EPS = 1e-6
ROPE_THETA = 10000.0
ATTN_SCALE = (NOPE + ROPE) ** -0.5
ADAM_LR = 0.001
ADAM_B1 = 0.9
ADAM_B2 = 0.999
ADAM_EPS = 1e-08
ADAM_WD = 0.01
ADAM_STEP = 10
N_DEV = 8

LANE = 128
SEG_Q_LAT = 0
SEG_KV_LAT = Q_LORA
SEG_HQ = 2048
SEG_HF = SEG_HQ + D_MODEL
SEG_HI = SEG_HF + D_MODEL
SEG_HG = SEG_HI + D_MODEL
SEG_GA = SEG_HG + D_MODEL
SEG_GB = SEG_GA + D_MODEL
SEG_KR = SEG_GB + D_MODEL
KR_W = 256
PROJ_W = SEG_KR + KR_W
QHEAD_W = 256

V7X_VMEM_BYTES = 64 * 1024 * 1024
VMEM_LIMIT = V7X_VMEM_BYTES * 7 // 8
NEG_BIG = -1e30
SUB = 16


def _tile(n, target, mult):
    best = None
    for t in range(mult, min(n, target) + 1, mult):
        if n % t == 0:
            best = t
    return n if best is None else best


def _params(sem):
    return pltpu.CompilerParams(dimension_semantics=sem, vmem_limit_bytes=VMEM_LIMIT)


def _sigmoid(x):
    return 1.0 / (1.0 + jnp.exp(-x))


_DIMS = {"nn": (((1,), (0,)), ((), ())), "nt": (((1,), (1,)), ((), ())), "tn": (((0,), (0,)), ((), ()))}


def _matmul(a, b, mode, out_dtype, name):
    if mode == "nn":
        (m, k), (_, n) = a.shape, b.shape
    elif mode == "nt":
        (m, k), (n, _) = a.shape, b.shape
    else:
        (k, m), (_, n) = a.shape, b.shape
    tm = _tile(m, 1040, 8) if mode != "tn" else _tile(m, 1024, LANE)
    tn = _tile(n, 1024, LANE)
    tk = _tile(k, 2048, LANE) if mode != "tn" else _tile(k, 1040, 8)
    nk = k // tk
    if mode == "nn":
        a_spec = pl.BlockSpec((tm, tk), lambda i, j, kk: (i, kk))
        b_spec = pl.BlockSpec((tk, tn), lambda i, j, kk: (kk, j))
    elif mode == "nt":
        a_spec = pl.BlockSpec((tm, tk), lambda i, j, kk: (i, kk))
        b_spec = pl.BlockSpec((tn, tk), lambda i, j, kk: (j, kk))
    else:
        a_spec = pl.BlockSpec((tk, tm), lambda i, j, kk: (kk, i))
        b_spec = pl.BlockSpec((tk, tn), lambda i, j, kk: (kk, j))
    dims = _DIMS[mode]

    def body(a_ref, b_ref, o_ref, acc_ref):
        kk = pl.program_id(2)
        part = lax.dot_general(a_ref[...], b_ref[...], dims, preferred_element_type=F32)

        @pl.when(kk == 0)
        def _():
            acc_ref[...] = part

        @pl.when(kk > 0)
        def _():
            acc_ref[...] += part

        @pl.when(kk == nk - 1)
        def _():
            o_ref[...] = acc_ref[...].astype(o_ref.dtype)

    return pl.pallas_call(
        body,
        name=name,
        out_shape=jax.ShapeDtypeStruct((m, n), out_dtype),
        grid=(m // tm, n // tn, nk),
        in_specs=[a_spec, b_spec],
        out_specs=pl.BlockSpec((tm, tn), lambda i, j, kk: (i, j)),
        scratch_shapes=[pltpu.VMEM((tm, tn), F32)],
        compiler_params=_params(("parallel", "parallel", "arbitrary")),
    )(a, b)


ROW_WINDOW_BYTES = 12 * 1024 * 1024


def _rowwise(name, fn, ins, outs, rows, tm):
    per_row = sum(s[2] * s[1].dtype.itemsize for s in ins if s[0] == "row")
    per_row += sum(s[1] * jnp.dtype(s[2]).itemsize for s in outs if s[0] == "row")
    if per_row:
        tm = _tile(rows, min(tm, max(8, ROW_WINDOW_BYTES // (2 * per_row))), 8)
    n_in = len(ins)
    in_specs, args = [], []
    for spec in ins:
        if spec[0] == "row":
            _, arr, w, cb = spec
            in_specs.append(pl.BlockSpec((tm, w), functools.partial(lambda i, cb: (i, cb), cb=cb)))
        else:
            arr = spec[1]
            in_specs.append(pl.BlockSpec(arr.shape, lambda i: (0, 0)))
        args.append(arr)
    out_shape, out_specs = [], []
    for spec in outs:
        if spec[0] == "row":
            out_shape.append(jax.ShapeDtypeStruct((rows, spec[1]), spec[2]))
            out_specs.append(pl.BlockSpec((tm, spec[1]), lambda i: (i, 0)))
        else:
            out_shape.append(jax.ShapeDtypeStruct(spec[1], F32))
            out_specs.append(pl.BlockSpec(spec[1], lambda i: (0, 0)))
    has_acc = any(s[0] == "acc" for s in outs)

    def body(*refs):
        i = pl.program_id(0)
        res = fn(i, tm, *[r[...] for r in refs[:n_in]])
        for spec, ref, val in zip(outs, refs[n_in:], res):
            if spec[0] == "row":
                ref[...] = val.astype(ref.dtype)
            else:
                @pl.when(i == 0)
                def _(ref=ref, val=val):
                    ref[...] = val

                @pl.when(i > 0)
                def _(ref=ref, val=val):
                    ref[...] += val

    return pl.pallas_call(
        body,
        name=name,
        out_shape=out_shape,
        grid=(rows // tm,),
        in_specs=in_specs,
        out_specs=out_specs,
        compiler_params=_params(("arbitrary" if has_acc else "parallel",)),
    )(*args)


def _row_ids(i, tm, shape):
    return i * tm + lax.broadcasted_iota(jnp.int32, shape, 0)


def _rms_fwd(x, g):
    r = lax.rsqrt(jnp.mean(x * x, axis=-1, keepdims=True) + NORM_EPS)
    return x * r * g


def _rms_bwd(x, g, dy):
    r = lax.rsqrt(jnp.mean(x * x, axis=-1, keepdims=True) + NORM_EPS)
    xhat = x * r
    dxhat = dy * g
    dx = r * (dxhat - xhat * jnp.mean(dxhat * xhat, axis=-1, keepdims=True))
    return dx, jnp.sum(dy * xhat, axis=0, keepdims=True)


def _silu(x):
    return x * _sigmoid(x)


def _dsilu(x):
    s = _sigmoid(x)
    return s * (1.0 + x * (1.0 - s))


def _rot_src(x):
    lane = lax.broadcasted_iota(jnp.int32, x.shape, 1)
    return jnp.where(lane < ROPE // 2, pltpu.roll(x, LANE - ROPE // 2, 1), pltpu.roll(x, ROPE // 2, 1))


def _rope_fwd_call(q_raw, kv, proj, cos_t, sin_t, rows, tm):
    def fn(i, tm_, q, kvv, kr, c, s):
        kr_rot = kr[:, :LANE]
        kr_rot = kr_rot * c + _rot_src(kr_rot) * s
        qs, ks, vs = [], [], []
        for h in range(HEADS):
            qn = q[:, h * QHEAD_W:h * QHEAD_W + NOPE]
            qr = q[:, h * QHEAD_W + NOPE:(h + 1) * QHEAD_W]
            qs += [qn, qr * c + _rot_src(qr) * s]
            ks += [kvv[:, h * 2 * NOPE:h * 2 * NOPE + NOPE], kr_rot]
            vs += [kvv[:, h * 2 * NOPE + NOPE:(h + 1) * 2 * NOPE]]
        return jnp.concatenate(qs, axis=1), jnp.concatenate(ks, axis=1), jnp.concatenate(vs, axis=1)

    return _rowwise(
        "rope_fwd", fn,
        [("row", q_raw, HEADS * QHEAD_W, 0), ("row", kv, HEADS * 2 * NOPE, 0), ("row", proj, KR_W, SEG_KR // KR_W),
         ("row", cos_t, LANE, 0), ("row", sin_t, LANE, 0)],
        [("row", HEADS * QHEAD_W, BF16), ("row", HEADS * QHEAD_W, BF16), ("row", HEADS * VDIM, BF16)],
        rows, tm)


def _rope_bwd_call(dq_att, dk_att, dv, cos_t, sin_t, rows, tm):
    def fn(i, tm_, dq, dk, dvv, c, s):
        qs, kvs = [], []
        dkr = jnp.zeros((dq.shape[0], LANE), F32)
        for h in range(HEADS):
            dqr = dq[:, h * QHEAD_W + NOPE:(h + 1) * QHEAD_W]
            qs += [dq[:, h * QHEAD_W:h * QHEAD_W + NOPE], dqr * c - _rot_src(dqr) * s]
            kvs += [dk[:, h * QHEAD_W:h * QHEAD_W + NOPE], dvv[:, h * VDIM:(h + 1) * VDIM]]
            dkr = dkr + dk[:, h * QHEAD_W + NOPE:(h + 1) * QHEAD_W]
        dkr = dkr * c - _rot_src(dkr) * s
        return (jnp.concatenate(qs, axis=1), jnp.concatenate(kvs, axis=1),
                jnp.concatenate([dkr, jnp.zeros_like(dkr)], axis=1))

    return _rowwise(
        "rope_bwd", fn,
        [("row", dq_att, HEADS * QHEAD_W, 0), ("row", dk_att, HEADS * QHEAD_W, 0), ("row", dv, HEADS * VDIM, 0),
         ("row", cos_t, LANE, 0), ("row", sin_t, LANE, 0)],
        [("row", HEADS * QHEAD_W, BF16), ("row", HEADS * 2 * NOPE, BF16), ("row", KR_W, BF16)],
        rows, tm)


def _attn_mask(qi, kj, t):
    rows = qi * t + lax.broadcasted_iota(jnp.int32, (t, t), 0)
    cols = kj * t + lax.broadcasted_iota(jnp.int32, (t, t), 1)
    return (cols <= rows) & ((cols >= PAD_LEN) | (cols == rows))


_NT = _DIMS["nt"]
_TN = _DIMS["tn"]


def _attn_fwd(q_att, k_att, v, rows):
    t = _tile(rows, 640, LANE)
    nb = rows // t

    def body(q_ref, k_ref, v_ref, o32_ref, obf_ref, lse_ref, m_sc, l_sc, acc_sc):
        qi, kj = pl.program_id(1), pl.program_id(2)

        @pl.when(kj == 0)
        def _():
            m_sc[...] = jnp.full_like(m_sc, NEG_BIG)
            l_sc[...] = jnp.zeros_like(l_sc)
            acc_sc[...] = jnp.zeros_like(acc_sc)

        @pl.when(kj <= qi)
        def _():
            s = lax.dot_general(q_ref[...], k_ref[...], _NT, preferred_element_type=F32) * ATTN_SCALE
            s = jnp.where(_attn_mask(qi, kj, t), s, NEG_BIG)
            m_prev = m_sc[...]
            m_new = jnp.maximum(m_prev, jnp.max(s, axis=1, keepdims=True))
            alpha = jnp.exp(m_prev - m_new)
            p = jnp.exp(s - m_new)
            l_sc[...] = alpha * l_sc[...] + jnp.sum(p, axis=1, keepdims=True)
            acc_sc[...] = alpha * acc_sc[...] + jnp.dot(p.astype(BF16), v_ref[...], preferred_element_type=F32)
            m_sc[...] = m_new

        @pl.when(kj == qi)
        def _():
            o = acc_sc[...] / l_sc[...]
            o32_ref[...] = o
            obf_ref[...] = o.astype(BF16)
            lse_ref[0] = m_sc[...] + jnp.log(l_sc[...])

    kmap = lambda h, qi, kj: (jnp.minimum(kj, qi), h)
    return pl.pallas_call(
        body,
        name="attn_fwd",
        out_shape=[jax.ShapeDtypeStruct((rows, HEADS * VDIM), F32), jax.ShapeDtypeStruct((rows, HEADS * VDIM), BF16),
                   jax.ShapeDtypeStruct((HEADS, rows, 1), F32)],
        grid=(HEADS, nb, nb),
        in_specs=[pl.BlockSpec((t, QHEAD_W), lambda h, qi, kj: (qi, h)),
                  pl.BlockSpec((t, QHEAD_W), kmap),
                  pl.BlockSpec((t, VDIM), kmap)],
        out_specs=[pl.BlockSpec((t, VDIM), lambda h, qi, kj: (qi, h)),
                   pl.BlockSpec((t, VDIM), lambda h, qi, kj: (qi, h)),
                   pl.BlockSpec((1, t, 1), lambda h, qi, kj: (h, qi, 0))],
        scratch_shapes=[pltpu.VMEM((t, 1), F32), pltpu.VMEM((t, 1), F32), pltpu.VMEM((t, VDIM), F32)],
        compiler_params=_params(("parallel", "parallel", "arbitrary")),
    )(q_att, k_att, v)


def _attn_probs(q, k, v, do, o, lse, qi, kj, t):
    s = lax.dot_general(q, k, _NT, preferred_element_type=F32) * ATTN_SCALE
    p = jnp.where(_attn_mask(qi, kj, t), jnp.exp(s - lse), 0.0)
    do_bf = do.astype(BF16)
    dp = lax.dot_general(do_bf, v, _NT, preferred_element_type=F32)
    delta = jnp.sum(do * o, axis=1, keepdims=True)
    ds = p * (dp - delta) * ATTN_SCALE
    return p, ds, do_bf


def _attn_bwd_dq(q_att, k_att, v, do, o32, lse, rows):
    t = _tile(rows, 640, LANE)
    nb = rows // t

    def body(q_ref, k_ref, v_ref, do_ref, o_ref, lse_ref, dq_ref, acc_sc):
        qi, kj = pl.program_id(1), pl.program_id(2)

        @pl.when(kj == 0)
        def _():
            acc_sc[...] = jnp.zeros_like(acc_sc)

        @pl.when(kj <= qi)
        def _():
            _, ds, _ = _attn_probs(q_ref[...], k_ref[...], v_ref[...], do_ref[...], o_ref[...], lse_ref[0], qi, kj, t)
            acc_sc[...] += jnp.dot(ds.astype(BF16), k_ref[...], preferred_element_type=F32)

        @pl.when(kj == qi)
        def _():
            dq_ref[...] = acc_sc[...]

    kmap = lambda h, qi, kj: (jnp.minimum(kj, qi), h)
    qmap = lambda h, qi, kj: (qi, h)
    return pl.pallas_call(
        body,
        name="attn_bwd_dq",
        out_shape=jax.ShapeDtypeStruct((rows, HEADS * QHEAD_W), F32),
        grid=(HEADS, nb, nb),
        in_specs=[pl.BlockSpec((t, QHEAD_W), qmap), pl.BlockSpec((t, QHEAD_W), kmap), pl.BlockSpec((t, VDIM), kmap),
                  pl.BlockSpec((t, VDIM), qmap), pl.BlockSpec((t, VDIM), qmap),
                  pl.BlockSpec((1, t, 1), lambda h, qi, kj: (h, qi, 0))],
        out_specs=pl.BlockSpec((t, QHEAD_W), qmap),
        scratch_shapes=[pltpu.VMEM((t, QHEAD_W), F32)],
        compiler_params=_params(("parallel", "parallel", "arbitrary")),
    )(q_att, k_att, v, do, o32, lse)


def _attn_bwd_dkv(q_att, k_att, v, do, o32, lse, rows):
    t = _tile(rows, 640, LANE)
    nb = rows // t

    def body(q_ref, k_ref, v_ref, do_ref, o_ref, lse_ref, dk_ref, dv_ref, dk_sc, dv_sc):
        kj, qi = pl.program_id(1), pl.program_id(2)

        @pl.when(qi == 0)
        def _():
            dk_sc[...] = jnp.zeros_like(dk_sc)
            dv_sc[...] = jnp.zeros_like(dv_sc)

        @pl.when(qi >= kj)
        def _():
            p, ds, do_bf = _attn_probs(q_ref[...], k_ref[...], v_ref[...], do_ref[...], o_ref[...], lse_ref[0], qi, kj, t)
            dv_sc[...] += lax.dot_general(p.astype(BF16), do_bf, _TN, preferred_element_type=F32)
            dk_sc[...] += lax.dot_general(ds.astype(BF16), q_ref[...], _TN, preferred_element_type=F32)

        @pl.when(qi == nb - 1)
        def _():
            dk_ref[...] = dk_sc[...]
            dv_ref[...] = dv_sc[...]

    kmap = lambda h, kj, qi: (kj, h)
    qmap = lambda h, kj, qi: (jnp.maximum(qi, kj), h)
    return pl.pallas_call(
        body,
        name="attn_bwd_dkv",
        out_shape=[jax.ShapeDtypeStruct((rows, HEADS * QHEAD_W), F32), jax.ShapeDtypeStruct((rows, HEADS * VDIM), F32)],
        grid=(HEADS, nb, nb),
        in_specs=[pl.BlockSpec((t, QHEAD_W), qmap), pl.BlockSpec((t, QHEAD_W), kmap), pl.BlockSpec((t, VDIM), kmap),
                  pl.BlockSpec((t, VDIM), qmap), pl.BlockSpec((t, VDIM), qmap),
                  pl.BlockSpec((1, t, 1), lambda h, kj, qi: (h, jnp.maximum(qi, kj), 0))],
        out_specs=[pl.BlockSpec((t, QHEAD_W), kmap), pl.BlockSpec((t, VDIM), kmap)],
        scratch_shapes=[pltpu.VMEM((t, QHEAD_W), F32), pltpu.VMEM((t, VDIM), F32)],
        compiler_params=_params(("parallel", "parallel", "arbitrary")),
    )(q_att, k_att, v, do, o32, lse)


C = BLOCK


def _hgrn_prep(hq, hf, hi, lb, c):
    rows = c * C + lax.broadcasted_iota(jnp.int32, (C, C), 0)
    valid = rows >= PAD_LEN
    sg = _sigmoid(hf)
    f = lb + (1.0 - lb) * sg
    g = jnp.where(valid, jnp.log(f), 0.0)
    k = jnp.where(valid, 1.0 - f, 0.0)
    q = _silu(hq)
    r = lax.broadcasted_iota(jnp.int32, (C, C), 0)
    cc = lax.broadcasted_iota(jnp.int32, (C, C), 1)
    tri = jnp.where(cc <= r, 1.0, 0.0).astype(F32)
    b = jnp.dot(tri, g, precision=lax.Precision.HIGHEST, preferred_element_type=F32)
    return q, k, hi, b, f, sg, valid


def _last_row_as_col(b_t):
    lane = lax.broadcasted_iota(jnp.int32, b_t.shape, 1)
    return jnp.sum(jnp.where(lane == C - 1, b_t, 0.0), axis=1, keepdims=True)


def _k_scaled(k, b, bs):
    return (k * jnp.exp(jnp.minimum(bs - b, 0.0))).astype(BF16)


def _hgrn_fwd(proj, lb, rows):
    nc = rows // C

    def body(hq_ref, hf_ref, hi_ref, lb_ref, o_ref, a_ref, s_ref, s_sc, b_sc):
        c = pl.program_id(1)

        @pl.when(c == 0)
        def _():
            s_sc[...] = jnp.zeros_like(s_sc)

        q, k, v, b, _, _, _ = _hgrn_prep(hq_ref[...], hf_ref[...], hi_ref[...], lb_ref[...], c)
        b_sc[...] = b
        s0 = s_sc[...]
        s_ref[0, 0] = s0
        v_bf = v.astype(BF16)
        r16 = lax.broadcasted_iota(jnp.int32, (SUB, C), 0)
        c16 = lax.broadcasted_iota(jnp.int32, (SUB, C), 1)
        slabs = [jnp.zeros((SUB, C), F32)]
        for i in range(1, C // SUB):
            bs = b_sc[SUB * i - 1:SUB * i, :]
            qs = (q[SUB * i:SUB * (i + 1)] * jnp.exp(b[SUB * i:SUB * (i + 1)] - bs)).astype(BF16)
            a_i = lax.dot_general(qs, _k_scaled(k, b, bs), _NT, preferred_element_type=F32)
            slabs.append(jnp.where(c16 <= r16 + (SUB * i - SUB), a_i, 0.0))
        a_off = jnp.concatenate(slabs, axis=0)
        q_t, k_t, b_t = q.T, k.T, b.T
        sub = lax.broadcasted_iota(jnp.int32, (C, C), 0)
        lane = lax.broadcasted_iota(jnp.int32, (C, C), 1)
        lane1 = lax.broadcasted_iota(jnp.int32, (1, C), 1)
        at_band = jnp.zeros((C, C), F32)
        for dl in range(SUB):
            k_s = pltpu.roll(k_t, dl, 1) if dl else k_t
            b_s = pltpu.roll(b_t, dl, 1) if dl else b_t
            e = jnp.exp(jnp.minimum(b_t - b_s, 0.0))
            band = jnp.sum(q_t * k_s * e, axis=0, keepdims=True)
            band = jnp.where(lane1 >= dl, band, 0.0)
            at_band = at_band + jnp.where(sub == lane - dl, jnp.broadcast_to(band, (C, C)), 0.0)
        a = (a_off + at_band.T).astype(BF16)
        a_ref[0] = a
        qe = (q * jnp.exp(b)).astype(BF16)
        o_ref[...] = (jnp.dot(a, v_bf, preferred_element_type=F32)
                      + jnp.dot(qe, s0.astype(BF16), preferred_element_type=F32))
        b_last = b_sc[C - 1:C, :]
        kd = (k * jnp.exp(b_last - b)).astype(BF16)
        s_sc[...] = (jnp.exp(_last_row_as_col(b_t)) * s0
                     + lax.dot_general(kd, v_bf, _TN, preferred_element_type=F32))

    seg = lambda base: (lambda h, c: (c, base // C + h))
    return pl.pallas_call(
        body,
        name="hgrn_fwd",
        out_shape=[jax.ShapeDtypeStruct((rows, D_MODEL), F32), jax.ShapeDtypeStruct((HEADS, rows, C), BF16),
                   jax.ShapeDtypeStruct((HEADS, nc, C, C), F32)],
        grid=(HEADS, nc),
        in_specs=[pl.BlockSpec((C, C), seg(SEG_HQ)), pl.BlockSpec((C, C), seg(SEG_HF)), pl.BlockSpec((C, C), seg(SEG_HI)),
                  pl.BlockSpec((1, C), lambda h, c: (0, h))],
        out_specs=[pl.BlockSpec((C, C), lambda h, c: (c, h)), pl.BlockSpec((1, C, C), lambda h, c: (h, c, 0)),
                   pl.BlockSpec((1, 1, C, C), lambda h, c: (h, c, 0, 0))],
        scratch_shapes=[pltpu.VMEM((C, C), F32), pltpu.VMEM((C, C), F32)],
        compiler_params=_params(("parallel", "arbitrary")),
    )(proj, proj, proj, lb)


def _hgrn_bwd(proj, lb, a_mat, s_states, do_h, rows):
    nc = rows // C

    def body(hq_ref, hf_ref, hi_ref, lb_ref, a_ref, s_ref, do_ref, dhq_ref, dhf_ref, dhi_ref, dlb_ref, ds_sc, b_sc):
        step = pl.program_id(1)
        c = nc - 1 - step

        @pl.when(step == 0)
        def _():
            ds_sc[...] = jnp.zeros_like(ds_sc)
            dlb_ref[...] = jnp.zeros_like(dlb_ref)

        hq, hf = hq_ref[...], hf_ref[...]
        lb_row = lb_ref[...]
        q, k, v, b, f, sg, valid = _hgrn_prep(hq, hf, hi_ref[...], lb_row, c)
        b_sc[...] = b
        s0 = s_ref[0, 0]
        ds1 = ds_sc[...]
        s0_bf, ds1_bf = s0.astype(BF16), ds1.astype(BF16)
        do = do_ref[...]
        do_bf, v_bf = do.astype(BF16), v.astype(BF16)
        b_last = b_sc[C - 1:C, :]
        e_last = jnp.exp(b_last - b)
        eb = jnp.exp(b)
        sub = lax.broadcasted_iota(jnp.int32, (C, C), 0)
        lane = lax.broadcasted_iota(jnp.int32, (C, C), 1)
        r16 = lax.broadcasted_iota(jnp.int32, (SUB, C), 0)
        c16 = lax.broadcasted_iota(jnp.int32, (SUB, C), 1)

        dv = (lax.dot_general(a_ref[0], do_bf, _TN, preferred_element_type=F32)
              + jnp.dot((k * e_last).astype(BF16), ds1_bf, preferred_element_type=F32))
        da = jnp.where(lane <= sub, lax.dot_general(do_bf, v_bf, _NT, preferred_element_type=F32), 0.0)
        da_t = jnp.where(sub <= lane, lax.dot_general(v_bf, do_bf, _NT, preferred_element_type=F32), 0.0)

        dq_slabs = [jnp.zeros((SUB, C), F32)]
        for i in range(1, C // SUB):
            bs = b_sc[SUB * i - 1:SUB * i, :]
            da_i = jnp.where(c16 <= r16 + (SUB * i - SUB), da[SUB * i:SUB * (i + 1)], 0.0).astype(BF16)
            dq_slabs.append(jnp.exp(b[SUB * i:SUB * (i + 1)] - bs)
                            * jnp.dot(da_i, _k_scaled(k, b, bs), preferred_element_type=F32))
        dk_slabs = []
        for j in range(C // SUB - 1):
            be = b_sc[SUB * j + SUB - 1:SUB * (j + 1), :]
            qe_j = (q * jnp.exp(jnp.minimum(b - be, 0.0))).astype(BF16)
            da_j = jnp.where(c16 >= r16 + (SUB * j + SUB), da_t[SUB * j:SUB * (j + 1)], 0.0).astype(BF16)
            dk_slabs.append(jnp.exp(be - b[SUB * j:SUB * (j + 1)]) * jnp.dot(da_j, qe_j, preferred_element_type=F32))
        dk_slabs.append(jnp.zeros((SUB, C), F32))

        q_t, k_t, b_t = q.T, k.T, b.T
        lane1 = lax.broadcasted_iota(jnp.int32, (1, C), 1)
        dq_t = jnp.zeros((C, C), F32)
        dk_t = jnp.zeros((C, C), F32)
        for dl in range(SUB):
            k_s = pltpu.roll(k_t, dl, 1) if dl else k_t
            b_s = pltpu.roll(b_t, dl, 1) if dl else b_t
            e = jnp.exp(jnp.minimum(b_t - b_s, 0.0))
            dband = jnp.sum(jnp.where(sub == lane - dl, da_t, 0.0), axis=0, keepdims=True)
            w = jnp.where(lane1 >= dl, dband, 0.0) * e
            dq_t = dq_t + w * k_s
            back = w * q_t
            dk_t = dk_t + (pltpu.roll(back, C - dl, 1) if dl else back)

        dq = eb * lax.dot_general(do_bf, s0_bf, _NT, preferred_element_type=F32) + jnp.concatenate(dq_slabs, axis=0) + dq_t.T
        dk_inter = e_last * lax.dot_general(v_bf, ds1_bf, _NT, preferred_element_type=F32)
        dk = dk_inter + jnp.concatenate(dk_slabs, axis=0) + dk_t.T

        extra = (jnp.exp(b_last) * jnp.sum((s0 * ds1).T, axis=0, keepdims=True)
                 + jnp.sum(k * dk_inter, axis=0, keepdims=True))
        db = q * dq - k * dk + jnp.where(sub == C - 1, jnp.broadcast_to(extra, (C, C)), 0.0)
        tri_t = jnp.where(lane >= sub, 1.0, 0.0).astype(F32)
        dg = jnp.dot(tri_t, db, precision=lax.Precision.HIGHEST, preferred_element_type=F32)
        ds_sc[...] = (jnp.exp(_last_row_as_col(b_t)) * ds1
                      + lax.dot_general((q * eb).astype(BF16), do_bf, _TN, preferred_element_type=F32))

        df = jnp.where(valid, dg / f - dk, 0.0)
        dhf_ref[...] = (df * (1.0 - lb_row) * sg * (1.0 - sg)).astype(BF16)
        dlb_ref[...] += jnp.sum(df * (1.0 - sg), axis=0, keepdims=True)
        dhq_ref[...] = (dq * _dsilu(hq)).astype(BF16)
        dhi_ref[...] = dv.astype(BF16)

    seg = lambda base: (lambda h, s: (nc - 1 - s, base // C + h))
    rmap = lambda h, s: (nc - 1 - s, h)
    return pl.pallas_call(
        body,
        name="hgrn_bwd",
        out_shape=[jax.ShapeDtypeStruct((rows, D_MODEL), BF16)] * 3 + [jax.ShapeDtypeStruct((1, D_MODEL), F32)],
        grid=(HEADS, nc),
        in_specs=[pl.BlockSpec((C, C), seg(SEG_HQ)), pl.BlockSpec((C, C), seg(SEG_HF)), pl.BlockSpec((C, C), seg(SEG_HI)),
                  pl.BlockSpec((1, C), lambda h, s: (0, h)),
                  pl.BlockSpec((1, C, C), lambda h, s: (h, nc - 1 - s, 0)),
                  pl.BlockSpec((1, 1, C, C), lambda h, s: (h, nc - 1 - s, 0, 0)),
                  pl.BlockSpec((C, C), rmap)],
        out_specs=[pl.BlockSpec((C, C), rmap)] * 3 + [pl.BlockSpec((1, C), lambda h, s: (0, h))],
        scratch_shapes=[pltpu.VMEM((C, C), F32), pltpu.VMEM((C, C), F32)],
        compiler_params=_params(("parallel", "arbitrary")),
    )(proj, proj, proj, lb, a_mat, s_states, do_h)


CONV_TC = 512
HALO = 8


def _conv_taps(i, tm, g_ref, pg_ref):
    shape = g_ref.shape
    r = lax.broadcasted_iota(jnp.int32, shape, 0)
    g = jnp.where(i * tm + r >= PAD_LEN, g_ref[...], 0.0)
    p1 = jnp.where(i * tm - 1 >= PAD_LEN, pg_ref[HALO - 1:HALO, :], 0.0)
    p2 = jnp.where(i * tm - 2 >= PAD_LEN, pg_ref[HALO - 2:HALO - 1, :], 0.0)
    s1 = jnp.where(r == 0, p1, pltpu.roll(g, 1, 0))
    s2 = jnp.where(r == 0, p2, jnp.where(r == 1, p1, pltpu.roll(g, 2, 0)))
    return g, s1, s2


def _conv_specs(tm, tc, ncb, order):
    gate = pl.BlockSpec((tm, tc), lambda *ids: order(ids))
    halo = pl.BlockSpec((HALO, tc), lambda *ids: (jnp.maximum(order(ids)[0] * (tm // HALO) - 1, 0), order(ids)[1]))
    up = pl.BlockSpec((tm, tc), lambda *ids: (order(ids)[0], ncb + order(ids)[1]))
    return gate, halo, up


def _conv_fwd(ffn, conv_w, conv_b, rows, tm):
    tc = CONV_TC
    ncb = D_FF // tc

    def body(g_ref, pg_ref, up_ref, cw_ref, cb_ref, act_ref):
        i = pl.program_id(0)
        g, s1, s2 = _conv_taps(i, tm, g_ref, pg_ref)
        conv = (cw_ref[0:1, :] * s2 + cw_ref[1:2, :] * s1 + cw_ref[2:3, :] * g) + cb_ref[...]
        act_ref[...] = (_silu(conv) * up_ref[...]).astype(BF16)

    gate, halo, up = _conv_specs(tm, tc, ncb, lambda ids: (ids[0], ids[1]))
    return pl.pallas_call(
        body,
        name="conv_fwd",
        out_shape=jax.ShapeDtypeStruct((rows, D_FF), BF16),
        grid=(rows // tm, ncb),
        in_specs=[gate, halo, up, pl.BlockSpec((3, tc), lambda i, j: (0, j)), pl.BlockSpec((1, tc), lambda i, j: (0, j))],
        out_specs=pl.BlockSpec((tm, tc), lambda i, j: (i, j)),
        compiler_params=_params(("parallel", "parallel")),
    )(ffn, ffn, ffn, conv_w, conv_b)


def _conv_bwd_a(ffn, dact, conv_w, conv_b, rows, tm):
    tc = CONV_TC
    ncb = D_FF // tc

    def body(g_ref, pg_ref, up_ref, da_ref, cw_ref, cb_ref, dc_ref, dffn_ref, w0_ref, w1_ref, w2_ref, db_ref):
        i = pl.program_id(1)
        g, s1, s2 = _conv_taps(i, tm, g_ref, pg_ref)
        conv = (cw_ref[0:1, :] * s2 + cw_ref[1:2, :] * s1 + cw_ref[2:3, :] * g) + cb_ref[...]
        da = da_ref[...]
        dffn_ref[...] = (da * _silu(conv)).astype(BF16)
        dc = da * up_ref[...] * _dsilu(conv)
        dc_ref[...] = dc
        sums = [jnp.sum(dc * s2, axis=0, keepdims=True), jnp.sum(dc * s1, axis=0, keepdims=True),
                jnp.sum(dc * g, axis=0, keepdims=True), jnp.sum(dc, axis=0, keepdims=True)]
        for ref, val in zip((w0_ref, w1_ref, w2_ref, db_ref), sums):
            @pl.when(i == 0)
            def _(ref=ref, val=val):
                ref[...] = val

            @pl.when(i > 0)
            def _(ref=ref, val=val):
                ref[...] += val

    gate, halo, up = _conv_specs(tm, tc, ncb, lambda ids: (ids[1], ids[0]))
    col = pl.BlockSpec((1, tc), lambda j, i: (0, j))
    return pl.pallas_call(
        body,
        name="conv_bwd_a",
        out_shape=[jax.ShapeDtypeStruct((rows, D_FF), F32), jax.ShapeDtypeStruct((rows, 2 * D_FF), BF16)]
        + [jax.ShapeDtypeStruct((1, D_FF), F32)] * 4,
        grid=(ncb, rows // tm),
        in_specs=[gate, halo, up, pl.BlockSpec((tm, tc), lambda j, i: (i, j)),
                  pl.BlockSpec((3, tc), lambda j, i: (0, j)), col],
        out_specs=[pl.BlockSpec((tm, tc), lambda j, i: (i, j)), pl.BlockSpec((tm, tc), lambda j, i: (i, ncb + j)),
                   col, col, col, col],
        compiler_params=_params(("parallel", "arbitrary")),
    )(ffn, ffn, ffn, dact, conv_w, conv_b)


def _conv_bwd_b(dconv, conv_w, dffn, rows, tm):
    tc = CONV_TC
    ncb = D_FF // tc
    nrb = rows // tm

    def body(dc_ref, nx_ref, cw_ref, dffn_in, out_ref):
        del dffn_in
        i = pl.program_id(0)
        dc = dc_ref[...]
        r = lax.broadcasted_iota(jnp.int32, dc.shape, 0)
        last = i == nrb - 1
        x1 = jnp.where(last, 0.0, nx_ref[0:1, :])
        x2 = jnp.where(last, 0.0, nx_ref[1:2, :])
        n1 = jnp.where(r == tm - 1, x1, pltpu.roll(dc, tm - 1, 0))
        n2 = jnp.where(r == tm - 1, x2, jnp.where(r == tm - 2, x1, pltpu.roll(dc, tm - 2, 0)))
        dg = cw_ref[2:3, :] * dc + cw_ref[1:2, :] * n1 + cw_ref[0:1, :] * n2
        out_ref[...] = jnp.where(i * tm + r >= PAD_LEN, dg, 0.0).astype(BF16)

    return pl.pallas_call(
        body,
        name="conv_bwd_b",
        out_shape=jax.ShapeDtypeStruct((rows, 2 * D_FF), BF16),
        grid=(nrb, ncb),
        in_specs=[pl.BlockSpec((tm, tc), lambda i, j: (i, j)),
                  pl.BlockSpec((HALO, tc), lambda i, j: (jnp.minimum((i + 1) * (tm // HALO), rows // HALO - 1), j)),
                  pl.BlockSpec((3, tc), lambda i, j: (0, j)),
                  pl.BlockSpec(memory_space=pl.ANY)],
        out_specs=pl.BlockSpec((tm, tc), lambda i, j: (i, j)),
        input_output_aliases={3: 0},
        compiler_params=_params(("parallel", "parallel")),
    )(dconv, dconv, conv_w, dffn)


def _final_call(h1, y, target, g_final, rows):
    tm = BLOCK

    def fn(i, tm_, h1v, yv, tgt, g):
        h2 = h1v + yv
        out = _rms_fwd(h2, g)
        err = jnp.where(i > 0, out - tgt, 0.0)
        loss = 0.5 * jnp.sum(jnp.mean(err * err, axis=-1, keepdims=True), axis=0, keepdims=True)
        dx, dg = _rms_bwd(h2, g, err * (1.0 / D_MODEL))
        return dx, dx, jnp.broadcast_to(loss, (1, LANE)), dg

    n_in = 4
    in_specs = [pl.BlockSpec((tm, D_MODEL), lambda i: (i, 0)), pl.BlockSpec((tm, D_MODEL), lambda i: (i, 0)),
                pl.BlockSpec((tm, D_MODEL), lambda i: (jnp.maximum(i - 1, 0), 0)),
                pl.BlockSpec((1, D_MODEL), lambda i: (0, 0))]

    def body(*refs):
        i = pl.program_id(0)
        dx, dx2, loss, dg = fn(i, tm, *[r[...] for r in refs[:n_in]])
        refs[4][...] = dx
        refs[5][...] = dx2.astype(BF16)
        for ref, val in ((refs[6], loss), (refs[7], dg)):
            @pl.when(i == 0)
            def _(ref=ref, val=val):
                ref[...] = val

            @pl.when(i > 0)
            def _(ref=ref, val=val):
                ref[...] += val

    return pl.pallas_call(
        body,
        name="final_loss",
        out_shape=[jax.ShapeDtypeStruct((rows, D_MODEL), F32), jax.ShapeDtypeStruct((rows, D_MODEL), BF16),
                   jax.ShapeDtypeStruct((1, LANE), F32), jax.ShapeDtypeStruct((1, D_MODEL), F32)],
        grid=(rows // tm,),
        in_specs=in_specs,
        out_specs=[pl.BlockSpec((tm, D_MODEL), lambda i: (i, 0)), pl.BlockSpec((tm, D_MODEL), lambda i: (i, 0)),
                   pl.BlockSpec((1, LANE), lambda i: (0, 0)), pl.BlockSpec((1, D_MODEL), lambda i: (0, 0))],
        compiler_params=_params(("arbitrary",)),
    )(h1, y, target, g_final)


def _heads_map(fn, *slabs):
    outs = [fn(*[s[:, h * LANE:(h + 1) * LANE] for s in slabs]) for h in range(HEADS)]
    if isinstance(outs[0], tuple):
        return tuple(jnp.concatenate([o[k] for o in outs], axis=1) for k in range(len(outs[0])))
    return jnp.concatenate(outs, axis=1)


def _local_step(x, positions, target, w, p):
    s_len = x.shape[0]
    rows = s_len + BLOCK
    tm = _tile(rows, 640, 8)
    row = lambda arr, width, cb=0: ("row", arr, width, cb)

    h0 = jnp.concatenate([jnp.zeros((PAD_LEN, D_MODEL), F32), w["meta_tokens"], x], axis=0)
    pos = jnp.concatenate([jnp.zeros((PAD_LEN,), jnp.int32), jnp.arange(N_META, dtype=jnp.int32),
                           positions.astype(jnp.int32) + N_META])
    inv = 1.0 / (ROPE_THETA ** (jnp.arange(0, ROPE, 2, dtype=F32) / ROPE))
    ang = pos.astype(F32)[:, None] * inv
    zero = jnp.zeros((rows, LANE - ROPE), F32)
    cos_t = jnp.concatenate([jnp.cos(ang), jnp.cos(ang), zero], axis=1)
    sin_t = jnp.concatenate([-jnp.sin(ang), jnp.sin(ang), zero], axis=1)
    lb_r0, lb_r1 = p["lb_raw"][0:1], p["lb_raw"][1:2]

    def lb_fn(i, tm_, r0, r1):
        m = jnp.maximum(r0, r1)
        e0, e1 = jnp.exp(r0 - m), jnp.exp(r1 - m)
        return (e0 / (e0 + e1),)

    (lb,) = _rowwise("lb_fwd", lb_fn, [("bc", lb_r0), ("bc", lb_r1)], [("acc", (1, D_MODEL))], 1, 1)

    (u1,) = _rowwise("mix_norm", lambda i, t, h, g: (_rms_fwd(h, g),),
                     [row(h0, D_MODEL), ("bc", p["g_mix_norm"])], [("row", D_MODEL, BF16)], rows, tm)
    proj = _matmul(u1, w["w_in"], "nn", F32, "mm_proj")
    qn, kvn = _rowwise(
        "latent_norm", lambda i, t, ql, kl, gq, gk: (_rms_fwd(ql, gq), _rms_fwd(kl, gk)),
        [row(proj, Q_LORA, 0), row(proj, KV_LORA, SEG_KV_LAT // KV_LORA), ("bc", p["g_q_norm"]), ("bc", p["g_kv_norm"])],
        [("row", Q_LORA, BF16), ("row", KV_LORA, BF16)], rows, tm)
    q_raw = _matmul(qn, w["w_q_up"], "nn", F32, "mm_q_up")
    kv = _matmul(kvn, w["w_kv_up"], "nn", F32, "mm_kv_up")
    q_att, k_att, v_att = _rope_fwd_call(q_raw, kv, proj, cos_t, sin_t, rows, tm)
    o32, o_bf, lse = _attn_fwd(q_att, k_att, v_att, rows)
    o_h, a_mat, s_states = _hgrn_fwd(proj, lb, rows)

    def hgrn_post(i, t, oh, hg, g):
        return (_heads_map(lambda a, b: _rms_fwd(a, g) * _silu(b), oh, hg),)

    (o_hgrn,) = _rowwise("hgrn_post", hgrn_post,
                         [row(o_h, D_MODEL), row(proj, D_MODEL, SEG_HG // D_MODEL), ("bc", p["g_hgrn_norm"])],
                         [("row", D_MODEL, BF16)], rows, tm)
    br_a = _matmul(o_bf, w["w_branch_mla"], "nn", F32, "mm_branch_mla")
    br_b = _matmul(o_hgrn, w["w_branch_hgrn"], "nn", F32, "mm_branch_hgrn")
    (merged,) = _rowwise(
        "merge", lambda i, t, a, b, ga, gb: (_sigmoid(ga) * a + _sigmoid(gb) * b,),
        [row(br_a, D_MODEL), row(br_b, D_MODEL), row(proj, D_MODEL, SEG_GA // D_MODEL), row(proj, D_MODEL, SEG_GB // D_MODEL)],
        [("row", D_MODEL, BF16)], rows, tm)
    mix_out = _matmul(merged, w["w_out"], "nn", F32, "mm_out")

    def ffn_norm(i, t, h, mo, g):
        h1v = h + mo
        return h1v, _rms_fwd(h1v, g)

    h1, u2 = _rowwise("ffn_norm", ffn_norm, [row(h0, D_MODEL), row(mix_out, D_MODEL), ("bc", p["g_ffn_norm"])],
                      [("row", D_MODEL, F32), ("row", D_MODEL, BF16)], rows, tm)
    ffn = _matmul(u2, w["w_ffn_in"], "nn", F32, "mm_ffn_in")
    act = _conv_fwd(ffn, w["conv_w"], p["conv_b"], rows, tm)
    y = _matmul(act, w["w_ffn_out"], "nn", F32, "mm_ffn_out")
    dh2, dh2_bf, loss_acc, dg_final = _final_call(h1, y, target, p["g_final_norm"].reshape(1, D_MODEL), rows)

    grads = {"g_final_norm": dg_final.reshape(D_MODEL)}
    dact = _matmul(dh2_bf, w["w_ffn_out"], "nt", F32, "mm_d_act")
    grads["w_ffn_out"] = _matmul(act, dh2_bf, "tn", F32, "mm_dw_ffn_out")
    dconv, dffn, dcw0, dcw1, dcw2, dcb = _conv_bwd_a(ffn, dact, w["conv_w"], p["conv_b"], rows, tm)
    dffn = _conv_bwd_b(dconv, w["conv_w"], dffn, rows, tm)
    grads["conv_w"] = jnp.concatenate([dcw0, dcw1, dcw2], axis=0)
    grads["conv_b"] = dcb
    du2 = _matmul(dffn, w["w_ffn_in"], "nt", F32, "mm_d_u2")
    grads["w_ffn_in"] = _matmul(u2, dffn, "tn", F32, "mm_dw_ffn_in")

    def ffn_norm_bwd(i, t, h, du, dh, g):
        dx, dg = _rms_bwd(h, g, du)
        dh1v = dh + dx
        return dh1v, dh1v, dg

    dh1, dh1_bf, grads["g_ffn_norm"] = _rowwise(
        "ffn_norm_bwd", ffn_norm_bwd, [row(h1, D_MODEL), row(du2, D_MODEL), row(dh2, D_MODEL), ("bc", p["g_ffn_norm"])],
        [("row", D_MODEL, F32), ("row", D_MODEL, BF16), ("acc", (1, D_MODEL))], rows, tm)
    dmerged = _matmul(dh1_bf, w["w_out"], "nt", F32, "mm_d_merged")
    grads["w_out"] = _matmul(merged, dh1_bf, "tn", F32, "mm_dw_out")

    def merge_bwd(i, t, dm, a, b, ga, gb):
        sa, sb = _sigmoid(ga), _sigmoid(gb)
        return dm * sa, dm * sb, dm * a * sa * (1.0 - sa), dm * b * sb * (1.0 - sb)

    da_bf, db_bf, dga, dgb = _rowwise(
        "merge_bwd", merge_bwd,
        [row(dmerged, D_MODEL), row(br_a, D_MODEL), row(br_b, D_MODEL),
         row(proj, D_MODEL, SEG_GA // D_MODEL), row(proj, D_MODEL, SEG_GB // D_MODEL)],
        [("row", D_MODEL, BF16)] * 4, rows, tm)
    do_mla = _matmul(da_bf, w["w_branch_mla"], "nt", F32, "mm_d_o_mla")
    grads["w_branch_mla"] = _matmul(o_bf, da_bf, "tn", F32, "mm_dw_branch_mla")
    do_hgrn = _matmul(db_bf, w["w_branch_hgrn"], "nt", F32, "mm_d_o_hgrn")
    grads["w_branch_hgrn"] = _matmul(o_hgrn, db_bf, "tn", F32, "mm_dw_branch_hgrn")

    def hgrn_post_bwd(i, t, dy, oh, hg, g):
        def one(dyh, ohh, hgh):
            dx, dg = _rms_bwd(ohh, g, dyh * _silu(hgh))
            return dx, dyh * _rms_fwd(ohh, g) * _dsilu(hgh), dg

        dx, dhg, dg = _heads_map(one, dy, oh, hg)
        dg_sum = dg[:, 0:LANE]
        for h in range(1, HEADS):
            dg_sum = dg_sum + dg[:, h * LANE:(h + 1) * LANE]
        return dx, dhg, dg_sum

    do_h, dhg, grads["g_hgrn_norm"] = _rowwise(
        "hgrn_post_bwd", hgrn_post_bwd,
        [row(do_hgrn, D_MODEL), row(o_h, D_MODEL), row(proj, D_MODEL, SEG_HG // D_MODEL), ("bc", p["g_hgrn_norm"])],
        [("row", D_MODEL, F32), ("row", D_MODEL, BF16), ("acc", (1, LANE))], rows, tm)
    dhq, dhf, dhi, dlb = _hgrn_bwd(proj, lb, a_mat, s_states, do_h, rows)

    def lb_bwd(i, tm_, d, l):
        t = d * l * (1.0 - l)
        return t, -t

    dlb0, dlb1 = _rowwise("lb_bwd", lb_bwd, [("bc", dlb), ("bc", lb)], [("acc", (1, D_MODEL))] * 2, 1, 1)
    grads["lb_raw"] = jnp.concatenate([dlb0, dlb1], axis=0)

    dq_att = _attn_bwd_dq(q_att, k_att, v_att, do_mla, o32, lse, rows)
    dk_att, dv_att = _attn_bwd_dkv(q_att, k_att, v_att, do_mla, o32, lse, rows)
    dq_full, dkv, dkr = _rope_bwd_call(dq_att, dk_att, dv_att, cos_t, sin_t, rows, tm)
    dqn = _matmul(dq_full, w["w_q_up"], "nt", F32, "mm_d_qn")
    grads["w_q_up"] = _matmul(qn, dq_full, "tn", F32, "mm_dw_q_up")
    dkvn = _matmul(dkv, w["w_kv_up"], "nt", F32, "mm_d_kvn")
    grads["w_kv_up"] = _matmul(kvn, dkv, "tn", F32, "mm_dw_kv_up")

    def latent_norm_bwd(i, t, ql, kl, dq, dk, gq, gk):
        dql, dgq = _rms_bwd(ql, gq, dq)
        dkl, dgk = _rms_bwd(kl, gk, dk)
        return dql, dkl, dgq, dgk

    dq_lat, dkv_lat, grads["g_q_norm"], grads["g_kv_norm"] = _rowwise(
        "latent_norm_bwd", latent_norm_bwd,
        [row(proj, Q_LORA, 0), row(proj, KV_LORA, SEG_KV_LAT // KV_LORA), row(dqn, Q_LORA), row(dkvn, KV_LORA),
         ("bc", p["g_q_norm"]), ("bc", p["g_kv_norm"])],
        [("row", Q_LORA, BF16), ("row", KV_LORA, BF16), ("acc", (1, Q_LORA)), ("acc", (1, KV_LORA))], rows, tm)
    dproj = jnp.concatenate([dq_lat, dkv_lat, dhq, dhf, dhi, dhg, dga, dgb, dkr], axis=1)
    du1 = _matmul(dproj, w["w_in"], "nt", F32, "mm_d_u1")
    grads["w_in"] = _matmul(u1, dproj, "tn", F32, "mm_dw_in")

    def mix_norm_bwd(i, t, h, du, dh, g):
        dx, dg = _rms_bwd(h, g, du)
        return dh + dx, dg

    dh0, grads["g_mix_norm"] = _rowwise(
        "mix_norm_bwd", mix_norm_bwd, [row(h0, D_MODEL), row(du1, D_MODEL), row(dh1, D_MODEL), ("bc", p["g_mix_norm"])],
        [("row", D_MODEL, F32), ("acc", (1, D_MODEL))], rows, tm)
    grads["meta_tokens"] = dh0[PAD_LEN:BLOCK]
    return loss_acc[0, 0], dh0[BLOCK:], grads


K_ROPE_AT = Q_LORA + KV_LORA
COL_SHARDED = ("w_in", "w_q_up", "w_kv_up", "w_ffn_in", "conv_w", "meta_tokens")
ROW_SHARDED = ("w_branch_mla", "w_branch_hgrn", "w_out", "w_ffn_out")
BIG = ("w_in", "w_q_up", "w_kv_up", "w_branch_mla", "w_branch_hgrn", "w_out", "w_ffn_in", "w_ffn_out")
SMALL = ("conv_b", "g_mix_norm", "g_q_norm", "g_kv_norm", "g_hgrn_norm", "g_ffn_norm", "g_final_norm", "lb_raw")


def _unshard(name, stacked):
    if name in COL_SHARDED:
        return jnp.transpose(stacked, (1, 0, 2)).reshape(stacked.shape[1], N_DEV * stacked.shape[2])
    return stacked.reshape(N_DEV * stacked.shape[1], stacked.shape[2])


def _reshard(name, full):
    if name in COL_SHARDED:
        r, c = full.shape
        return jnp.transpose(full.reshape(r, N_DEV, c // N_DEV), (1, 0, 2))
    return full.reshape(N_DEV, full.shape[0] // N_DEV, full.shape[1])


def _to_kernel_layout(full):
    out = dict(full)
    w_in = full["w_in"]
    pad = jnp.zeros((D_MODEL, KR_W - ROPE), w_in.dtype)
    out["w_in"] = jnp.concatenate([w_in[:, :K_ROPE_AT], w_in[:, K_ROPE_AT + ROPE:], w_in[:, K_ROPE_AT:K_ROPE_AT + ROPE], pad], axis=1)
    wq = full["w_q_up"].reshape(Q_LORA, HEADS, NOPE + ROPE)
    out["w_q_up"] = jnp.pad(wq, ((0, 0), (0, 0), (0, QHEAD_W - NOPE - ROPE))).reshape(Q_LORA, HEADS * QHEAD_W)
    return out


def _from_kernel_layout(grads):
    out = dict(grads)
    g = grads["w_in"]
    out["w_in"] = jnp.concatenate([g[:, :K_ROPE_AT], g[:, SEG_KR:SEG_KR + ROPE], g[:, K_ROPE_AT:SEG_KR]], axis=1)
    out["w_q_up"] = grads["w_q_up"].reshape(Q_LORA, HEADS, QHEAD_W)[:, :, :NOPE + ROPE].reshape(Q_LORA, HEADS * (NOPE + ROPE))
    return out


MESH_ID = pl.DeviceIdType.MESH
ANY = pl.BlockSpec(memory_space=pl.ANY)


def _slot(dev):
    return 4 * dev[0] + 2 * dev[1] + dev[2]


def _all_gather(shards):
    n = len(shards)

    def body(*refs):
        ins, outs = refs[:n], refs[n:2 * n]
        send_sems, recv_sems, local_sems = refs[2 * n:]
        x, y, c = lax.axis_index("x"), lax.axis_index("y"), lax.axis_index("c")
        me, sibling = (x, y, c), (x, y, 1 - c)
        chips = [(1 - x, y), (x, 1 - y), (1 - x, 1 - y)]

        def copy(a, k, block, to, src=None):
            dst = outs[a].at[_slot(block)]
            return pltpu.make_async_remote_copy(
                src_ref=dst if src is None else src, dst_ref=dst, send_sem=send_sems.at[a, k],
                recv_sem=recv_sems.at[a, k], device_id=to, device_id_type=MESH_ID)

        mine = [pltpu.make_async_copy(ins[a], outs[a].at[_slot(me)], local_sems.at[a]) for a in range(n)]
        for cp in mine:
            cp.start()
        first = []
        for a in range(n):
            first.append(copy(a, 0, me, sibling, src=ins[a]))
            first += [copy(a, 1 + j, me, (*chip, c), src=ins[a]) for j, chip in enumerate(chips)]
        for cp in first:
            cp.start()
        passed = []
        for a in range(n):
            for j, chip in enumerate(chips):
                copy(a, 1 + j, (*chip, c), me).wait_recv()
                fwd = copy(a, 4 + j, (*chip, c), sibling)
                fwd.start()
                passed.append(fwd)
        for a in range(n):
            copy(a, 0, sibling, me).wait_recv()
            for j, chip in enumerate(chips):
                copy(a, 4 + j, (*chip, 1 - c), me).wait_recv()
        for cp in first + passed:
            cp.wait_send()
        for cp in mine:
            cp.wait()

    return pl.pallas_call(
        body,
        name="gather_weights",
        out_shape=[jax.ShapeDtypeStruct((N_DEV,) + s.shape, s.dtype) for s in shards],
        in_specs=[ANY] * n,
        out_specs=[ANY] * n,
        scratch_shapes=[pltpu.SemaphoreType.DMA((n, 7)), pltpu.SemaphoreType.DMA((n, 7)), pltpu.SemaphoreType.DMA((n,))],
    )(*shards)


def _exchange(blocked, replicated):
    nb, n = len(blocked), len(blocked) + len(replicated)
    arrays = list(blocked) + list(replicated)

    def body(*refs):
        ins, outs = refs[:n], refs[n:2 * n]
        send_sems, recv_sems, local_sems = refs[2 * n:]
        x, y, c = lax.axis_index("x"), lax.axis_index("y"), lax.axis_index("c")
        me = (x, y, c)
        peers = [(x, y, 1 - c), (1 - x, y, c), (x, 1 - y, c), (1 - x, 1 - y, c),
                 (1 - x, y, 1 - c), (x, 1 - y, 1 - c), (1 - x, 1 - y, 1 - c)]

        def src_of(a, dev):
            return ins[a].at[_slot(dev)] if a < nb else ins[a]

        def copy(a, k, frm, to):
            return pltpu.make_async_remote_copy(
                src_ref=src_of(a, to), dst_ref=outs[a].at[_slot(frm)], send_sem=send_sems.at[a, k],
                recv_sem=recv_sems.at[a, k], device_id=to, device_id_type=MESH_ID)

        mine = [pltpu.make_async_copy(src_of(a, me), outs[a].at[_slot(me)], local_sems.at[a]) for a in range(n)]
        for cp in mine:
            cp.start()
        sends = [copy(a, k, me, peer) for a in range(n) for k, peer in enumerate(peers)]
        for cp in sends:
            cp.start()
        for a in range(n):
            for k, peer in enumerate(peers):
                copy(a, k, peer, me).wait_recv()
        for cp in sends:
            cp.wait_send()
        for cp in mine:
            cp.wait()

    return pl.pallas_call(
        body,
        name="exchange_grads",
        out_shape=[jax.ShapeDtypeStruct(s.shape, s.dtype) for s in blocked]
        + [jax.ShapeDtypeStruct((N_DEV,) + s.shape, s.dtype) for s in replicated],
        in_specs=[ANY] * n,
        out_specs=[ANY] * n,
        scratch_shapes=[pltpu.SemaphoreType.DMA((n, 7)), pltpu.SemaphoreType.DMA((n, 7)), pltpu.SemaphoreType.DMA((n,))],
    )(*arrays)


ADAMW_BLOCK_ELEMS = 256 * 1024


def _adamw(name, parts, w, m, v):
    r, c = w.shape
    tr = _tile(r, max(8, ADAMW_BLOCK_ELEMS // c), 8)

    def body(p_ref, w_ref, m_ref, v_ref, g_ref, d_ref, nm_ref, nv_ref):
        g = p_ref[0]
        for s in range(1, N_DEV):
            g = g + p_ref[s]
        m_new = ADAM_B1 * m_ref[...] + (1.0 - ADAM_B1) * g
        v_new = ADAM_B2 * v_ref[...] + (1.0 - ADAM_B2) * (g * g)
        m_hat = m_new / (1.0 - ADAM_B1 ** ADAM_STEP)
        v_hat = v_new / (1.0 - ADAM_B2 ** ADAM_STEP)
        g_ref[...] = g
        d_ref[...] = -ADAM_LR * (m_hat / (jnp.sqrt(v_hat) + ADAM_EPS) + ADAM_WD * w_ref[...])
        nm_ref[...] = m_new
        nv_ref[...] = v_new

    blk = pl.BlockSpec((tr, c), lambda i: (i, 0))
    return pl.pallas_call(
        body,
        name="adamw_" + name,
        out_shape=[jax.ShapeDtypeStruct((r, c), F32)] * 4,
        grid=(r // tr,),
        in_specs=[pl.BlockSpec((N_DEV, tr, c), lambda i: (0, i, 0)), blk, blk, blk],
        out_specs=[blk] * 4,
        compiler_params=_params(("parallel",)),
    )(parts, w, m, v)


def _pack_small(vals):
    return jnp.concatenate([v.reshape(-1, LANE) for v in vals], axis=0)


def _unpack_small(packed, like):
    out, at = [], 0
    for ref in like:
        n = ref.size // LANE
        out.append(packed[at:at + n].reshape(ref.shape))
        at += n
    return out


def kernel(x, positions, meta_tokens, w_in, w_q_up, w_kv_up, w_branch_mla, w_branch_hgrn, w_out, w_ffn_in, w_ffn_out, conv_w, conv_b, g_mix_norm, g_q_norm, g_kv_norm, g_hgrn_norm, g_ffn_norm, g_final_norm, lb_raw, loss_target, m_meta_tokens, m_w_in, m_w_q_up, m_w_kv_up, m_w_branch_mla, m_w_branch_hgrn, m_w_out, m_w_ffn_in, m_w_ffn_out, m_conv_w, m_conv_b, m_g_mix_norm, m_g_q_norm, m_g_kv_norm, m_g_hgrn_norm, m_g_ffn_norm, m_g_final_norm, m_lb_raw, v_meta_tokens, v_w_in, v_w_q_up, v_w_kv_up, v_w_branch_mla, v_w_branch_hgrn, v_w_out, v_w_ffn_in, v_w_ffn_out, v_conv_w, v_conv_b, v_g_mix_norm, v_g_q_norm, v_g_kv_norm, v_g_hgrn_norm, v_g_ffn_norm, v_g_final_norm, v_lb_raw):
    names = BIG + ("conv_w", "meta_tokens") + SMALL
    local = dict(zip(
        ("meta_tokens", "w_in", "w_q_up", "w_kv_up", "w_branch_mla", "w_branch_hgrn", "w_out", "w_ffn_in", "w_ffn_out",
         "conv_w", "conv_b", "g_mix_norm", "g_q_norm", "g_kv_norm", "g_hgrn_norm", "g_ffn_norm", "g_final_norm", "lb_raw"),
        (meta_tokens, w_in, w_q_up, w_kv_up, w_branch_mla, w_branch_hgrn, w_out, w_ffn_in, w_ffn_out,
         conv_w, conv_b, g_mix_norm, g_q_norm, g_kv_norm, g_hgrn_norm, g_ffn_norm, g_final_norm, lb_raw)))
    mom_m = dict(zip(local, (m_meta_tokens, m_w_in, m_w_q_up, m_w_kv_up, m_w_branch_mla, m_w_branch_hgrn, m_w_out, m_w_ffn_in,
                             m_w_ffn_out, m_conv_w, m_conv_b, m_g_mix_norm, m_g_q_norm, m_g_kv_norm, m_g_hgrn_norm,
                             m_g_ffn_norm, m_g_final_norm, m_lb_raw)))
    mom_v = dict(zip(local, (v_meta_tokens, v_w_in, v_w_q_up, v_w_kv_up, v_w_branch_mla, v_w_branch_hgrn, v_w_out, v_w_ffn_in,
                             v_w_ffn_out, v_conv_w, v_conv_b, v_g_mix_norm, v_g_q_norm, v_g_kv_norm, v_g_hgrn_norm,
                             v_g_ffn_norm, v_g_final_norm, v_lb_raw)))
    sharded = BIG + ("conv_w", "meta_tokens")

    def shard2d(name, arr):
        return arr.reshape(arr.shape[-2:]) if name != "meta_tokens" else arr

    shards = [shard2d(n, local[n]).astype(BF16) for n in BIG] + [shard2d(n, local[n]) for n in ("conv_w", "meta_tokens")]
    gathered = _all_gather(shards)
    full = _to_kernel_layout({n: _unshard(n, g) for n, g in zip(sharded, gathered)})
    small = {n: local[n] for n in SMALL}

    loss, grad_x, grads = _local_step(x[0], positions[0], loss_target[0], full, small)
    grads = _from_kernel_layout(grads)

    small_like = [local[n] for n in SMALL]
    blocked = [_reshard(n, grads[n]) for n in sharded]
    parts = _exchange(blocked, [_pack_small([grads[n] for n in SMALL])])
    out = {}
    for n, part in zip(sharded, parts[:-1]):
        res = _adamw(n, part, shard2d(n, local[n]), shard2d(n, mom_m[n]), shard2d(n, mom_v[n]))
        out[n] = [r.reshape(local[n].shape) for r in res]
    res = _adamw("replicated", parts[-1], _pack_small(small_like), _pack_small([mom_m[n] for n in SMALL]),
                 _pack_small([mom_v[n] for n in SMALL]))
    for k, n in enumerate(SMALL):
        out[n] = [_unpack_small(r, small_like)[k] for r in res]

    loss = lax.psum(loss, ("x", "y", "c"))
    order = tuple(local)
    return (loss, grad_x[None], *[out[n][0] for n in order], *[out[n][1] for n in order],
            *[out[n][2] for n in order], *[out[n][3] for n in order])
```

```python
import functools

import jax
import jax.numpy as jnp
import numpy as np
from jax import lax
from jax.experimental import pallas as pl
from jax.experimental.pallas import tpu as pltpu

F32 = jnp.float32
BF16 = jnp.bfloat16

D_MODEL = 2048
N_META = 16
BLOCK = 128
PAD_LEN = BLOCK - N_META
HEADS = 16
Q_LORA = 1536
KV_LORA = 512
ROPE = 64
NOPE = 128
VDIM = 128
D_FF = 5632
NORM_EPS = 1e-6
ROPE_THETA = 10000.0
ATTN_SCALE = (NOPE + ROPE) ** -0.5
ADAM_LR = 0.001
ADAM_B1 = 0.9
ADAM_B2 = 0.999
ADAM_EPS = 1e-08
ADAM_WD = 0.01
ADAM_STEP = 10
N_DEV = 8

LANE = 128
SEG_Q_LAT = 0
SEG_KV_LAT = Q_LORA
SEG_HQ = 2048
SEG_HF = SEG_HQ + D_MODEL
SEG_HI = SEG_HF + D_MODEL
SEG_HG = SEG_HI + D_MODEL
SEG_GA = SEG_HG + D_MODEL
SEG_GB = SEG_GA + D_MODEL
SEG_KR = SEG_GB + D_MODEL
KR_W = 256
PROJ_W = SEG_KR + KR_W
QHEAD_W = 256

V7X_VMEM_BYTES = 64 * 1024 * 1024
VMEM_LIMIT = V7X_VMEM_BYTES * 7 // 8
NEG_BIG = -1e30
SUB = 16


def _tile(n, target, mult):
    best = None
    for t in range(mult, min(n, target) + 1, mult):
        if n % t == 0:
            best = t
    return n if best is None else best


def _params(sem):
    return pltpu.CompilerParams(dimension_semantics=sem, vmem_limit_bytes=VMEM_LIMIT)


def _sigmoid(x):
    return 1.0 / (1.0 + jnp.exp(-x))


_DIMS = {"nn": (((1,), (0,)), ((), ())), "nt": (((1,), (1,)), ((), ())), "tn": (((0,), (0,)), ((), ()))}


def _matmul(a, b, mode, out_dtype, name):
    if mode == "nn":
        (m, k), (_, n) = a.shape, b.shape
    elif mode == "nt":
        (m, k), (n, _) = a.shape, b.shape
    else:
        (k, m), (_, n) = a.shape, b.shape
    tm = _tile(m, 1040, 8) if mode != "tn" else _tile(m, 1024, LANE)
    tn = _tile(n, 1024, LANE)
    tk = _tile(k, 2816, LANE) if mode != "tn" else _tile(k, 2080, 8)
    nk = k // tk
    if mode == "nn":
        a_spec = pl.BlockSpec((tm, tk), lambda i, j, kk: (i, kk))
        b_spec = pl.BlockSpec((tk, tn), lambda i, j, kk: (kk, j))
    elif mode == "nt":
        a_spec = pl.BlockSpec((tm, tk), lambda i, j, kk: (i, kk))
        b_spec = pl.BlockSpec((tn, tk), lambda i, j, kk: (j, kk))
    else:
        a_spec = pl.BlockSpec((tk, tm), lambda i, j, kk: (kk, i))
        b_spec = pl.BlockSpec((tk, tn), lambda i, j, kk: (kk, j))
    dims = _DIMS[mode]

    def body(a_ref, b_ref, o_ref, *acc):
        part = lax.dot_general(a_ref[...], b_ref[...], dims, preferred_element_type=F32)
        if nk == 1:
            o_ref[...] = part.astype(o_ref.dtype)
            return
        acc_ref, kk = acc[0], pl.program_id(2)

        @pl.when(kk == 0)
        def _():
            acc_ref[...] = part

        @pl.when((kk > 0) & (kk < nk - 1))
        def _():
            acc_ref[...] += part

        @pl.when(kk == nk - 1)
        def _():
            o_ref[...] = (acc_ref[...] + part).astype(o_ref.dtype)

    return pl.pallas_call(
        body,
        name=name,
        out_shape=jax.ShapeDtypeStruct((m, n), out_dtype),
        grid=(m // tm, n // tn, nk),
        in_specs=[a_spec, b_spec],
        out_specs=pl.BlockSpec((tm, tn), lambda i, j, kk: (i, j)),
        scratch_shapes=[pltpu.VMEM((tm, tn), F32)] if nk > 1 else [],
        compiler_params=_params(("parallel", "parallel", "arbitrary")),
    )(a, b)


ROW_WINDOW_BYTES = 12 * 1024 * 1024


def _rowwise(name, fn, ins, outs, rows, tm):
    per_row = sum(s[2] * s[1].dtype.itemsize for s in ins if s[0] == "row")
    per_row += sum(s[1] * jnp.dtype(s[2]).itemsize for s in outs if s[0] == "row")
    if per_row:
        tm = _tile(rows, min(tm, max(8, ROW_WINDOW_BYTES // (2 * per_row))), 8)
    n_in = len(ins)
    in_specs, args = [], []
    for spec in ins:
        if spec[0] == "row":
            _, arr, w, cb = spec
            in_specs.append(pl.BlockSpec((tm, w), functools.partial(lambda i, cb: (i, cb), cb=cb)))
        else:
            arr = spec[1]
            in_specs.append(pl.BlockSpec(arr.shape, lambda i: (0, 0)))
        args.append(arr)
    out_shape, out_specs = [], []
    for spec in outs:
        if spec[0] == "row":
            out_shape.append(jax.ShapeDtypeStruct((rows, spec[1]), spec[2]))
            out_specs.append(pl.BlockSpec((tm, spec[1]), lambda i: (i, 0)))
        else:
            out_shape.append(jax.ShapeDtypeStruct(spec[1], F32))
            out_specs.append(pl.BlockSpec(spec[1], lambda i: (0, 0)))
    has_acc = any(s[0] == "acc" for s in outs)

    def body(*refs):
        i = pl.program_id(0)
        res = fn(i, tm, *[r[...] for r in refs[:n_in]])
        for spec, ref, val in zip(outs, refs[n_in:], res):
            if spec[0] == "row":
                ref[...] = val.astype(ref.dtype)
            else:
                @pl.when(i == 0)
                def _(ref=ref, val=val):
                    ref[...] = val

                @pl.when(i > 0)
                def _(ref=ref, val=val):
                    ref[...] += val

    return pl.pallas_call(
        body,
        name=name,
        out_shape=out_shape,
        grid=(rows // tm,),
        in_specs=in_specs,
        out_specs=out_specs,
        compiler_params=_params(("arbitrary" if has_acc else "parallel",)),
    )(*args)


def _row_ids(i, tm, shape):
    return i * tm + lax.broadcasted_iota(jnp.int32, shape, 0)


def _rms_fwd(x, g):
    r = lax.rsqrt(jnp.mean(x * x, axis=-1, keepdims=True) + NORM_EPS)
    return x * r * g


def _rms_bwd(x, g, dy):
    r = lax.rsqrt(jnp.mean(x * x, axis=-1, keepdims=True) + NORM_EPS)
    xhat = x * r
    dxhat = dy * g
    dx = r * (dxhat - xhat * jnp.mean(dxhat * xhat, axis=-1, keepdims=True))
    return dx, jnp.sum(dy * xhat, axis=0, keepdims=True)


def _silu(x):
    return x * _sigmoid(x)


def _dsilu(x):
    s = _sigmoid(x)
    return s * (1.0 + x * (1.0 - s))


def _rot_src(x):
    lane = lax.broadcasted_iota(jnp.int32, x.shape, 1)
    return jnp.where(lane < ROPE // 2, pltpu.roll(x, LANE - ROPE // 2, 1), pltpu.roll(x, ROPE // 2, 1))


def _rope_fwd_call(q_raw, kv, proj, cos_t, sin_t, rows, tm):
    def fn(i, tm_, q, kvv, kr, c, s):
        kr_rot = kr[:, :LANE]
        kr_rot = kr_rot * c + _rot_src(kr_rot) * s
        qs, ks, vs = [], [], []
        for h in range(HEADS):
            qn = q[:, h * QHEAD_W:h * QHEAD_W + NOPE]
            qr = q[:, h * QHEAD_W + NOPE:(h + 1) * QHEAD_W]
            qs += [qn, qr * c + _rot_src(qr) * s]
            ks += [kvv[:, h * 2 * NOPE:h * 2 * NOPE + NOPE], kr_rot]
            vs += [kvv[:, h * 2 * NOPE + NOPE:(h + 1) * 2 * NOPE]]
        return jnp.concatenate(qs, axis=1), jnp.concatenate(ks, axis=1), jnp.concatenate(vs, axis=1)

    return _rowwise(
        "rope_fwd", fn,
        [("row", q_raw, HEADS * QHEAD_W, 0), ("row", kv, HEADS * 2 * NOPE, 0), ("row", proj, KR_W, SEG_KR // KR_W),
         ("row", cos_t, LANE, 0), ("row", sin_t, LANE, 0)],
        [("row", HEADS * QHEAD_W, BF16), ("row", HEADS * QHEAD_W, BF16), ("row", HEADS * VDIM, BF16)],
        rows, tm)


def _rope_bwd_call(dq_att, dk_att, dv, cos_t, sin_t, rows, tm):
    def fn(i, tm_, dq, dk, dvv, c, s):
        qs, kvs = [], []
        dkr = jnp.zeros((dq.shape[0], LANE), F32)
        for h in range(HEADS):
            dqr = dq[:, h * QHEAD_W + NOPE:(h + 1) * QHEAD_W]
            qs += [dq[:, h * QHEAD_W:h * QHEAD_W + NOPE], dqr * c - _rot_src(dqr) * s]
            kvs += [dk[:, h * QHEAD_W:h * QHEAD_W + NOPE], dvv[:, h * VDIM:(h + 1) * VDIM]]
            dkr = dkr + dk[:, h * QHEAD_W + NOPE:(h + 1) * QHEAD_W]
        dkr = dkr * c - _rot_src(dkr) * s
        return (jnp.concatenate(qs, axis=1), jnp.concatenate(kvs, axis=1),
                jnp.concatenate([dkr, jnp.zeros_like(dkr)], axis=1))

    return _rowwise(
        "rope_bwd", fn,
        [("row", dq_att, HEADS * QHEAD_W, 0), ("row", dk_att, HEADS * QHEAD_W, 0), ("row", dv, HEADS * VDIM, 0),
         ("row", cos_t, LANE, 0), ("row", sin_t, LANE, 0)],
        [("row", HEADS * QHEAD_W, BF16), ("row", HEADS * 2 * NOPE, BF16), ("row", KR_W, BF16)],
        rows, tm)


def _attn_mask(q_blk, k_blk, t, keys_on_rows=False):
    qa, ka = (1, 0) if keys_on_rows else (0, 1)
    qs = q_blk * t + lax.broadcasted_iota(jnp.int32, (t, t), qa)
    ks = k_blk * t + lax.broadcasted_iota(jnp.int32, (t, t), ka)
    return (ks <= qs) & ((ks >= PAD_LEN) | (ks == qs))


_NT = _DIMS["nt"]
_TN = _DIMS["tn"]
LOG2E = 1.4426950408889634
SCORE_TO_LOG2 = ATTN_SCALE * LOG2E


def _causal_pairs(nb, by_key):
    if by_key:
        pairs = [(qi, kj) for kj in range(nb) for qi in range(kj, nb)]
    else:
        pairs = [(qi, kj) for qi in range(nb) for kj in range(qi + 1)]
    return (jnp.asarray(np.array([p[0] for p in pairs], np.int32)), jnp.asarray(np.array([p[1] for p in pairs], np.int32)))


def _attn_fwd(q_att, k_att, v, rows):
    t = _tile(rows, 640, LANE)
    nb = rows // t

    def body(qt_ref, kt_ref, q_ref, k_ref, v_ref, o32_ref, obf_ref, lse_ref, m_sc, l_sc, acc_sc):
        qi, kj = qt_ref[pl.program_id(1)], kt_ref[pl.program_id(1)]

        @pl.when(kj == 0)
        def _():
            m_sc[...] = jnp.full_like(m_sc, NEG_BIG)
            l_sc[...] = jnp.zeros_like(l_sc)
            acc_sc[...] = jnp.zeros_like(acc_sc)

        def step(masked):
            s = lax.dot_general(q_ref[...], k_ref[...], _NT, preferred_element_type=F32) * SCORE_TO_LOG2
            if masked:
                s = jnp.where(_attn_mask(qi, kj, t), s, NEG_BIG)
            m_prev = m_sc[...]
            m_new = jnp.maximum(m_prev, jnp.max(s, axis=1, keepdims=True))
            alpha = jnp.exp2(m_prev - m_new)
            p = jnp.exp2(s - jnp.tile(m_new, (1, t // LANE)))
            l_sc[...] = alpha * l_sc[...] + jnp.sum(p, axis=1, keepdims=True)
            acc_sc[...] = alpha * acc_sc[...] + jnp.dot(p.astype(BF16), v_ref[...], preferred_element_type=F32)
            m_sc[...] = m_new

        pl.when((kj == qi) | (kj == 0))(functools.partial(step, True))
        pl.when((kj < qi) & (kj > 0))(functools.partial(step, False))

        @pl.when(kj == qi)
        def _():
            o = acc_sc[...] / l_sc[...]
            o32_ref[...] = o
            obf_ref[...] = o.astype(BF16)
            lse_ref[0] = m_sc[:, 0:1] + jnp.log2(l_sc[:, 0:1])

    qt, kt = _causal_pairs(nb, by_key=False)
    qmap = lambda h, p, qt_ref, kt_ref: (qt_ref[p], h)
    kmap = lambda h, p, qt_ref, kt_ref: (kt_ref[p], h)
    return pl.pallas_call(
        body,
        name="attn_fwd",
        out_shape=[jax.ShapeDtypeStruct((rows, HEADS * VDIM), F32), jax.ShapeDtypeStruct((rows, HEADS * VDIM), BF16),
                   jax.ShapeDtypeStruct((HEADS, rows, 1), F32)],
        grid_spec=pltpu.PrefetchScalarGridSpec(
            num_scalar_prefetch=2,
            grid=(HEADS, len(qt)),
            in_specs=[pl.BlockSpec((t, QHEAD_W), qmap), pl.BlockSpec((t, QHEAD_W), kmap), pl.BlockSpec((t, VDIM), kmap)],
            out_specs=[pl.BlockSpec((t, VDIM), qmap), pl.BlockSpec((t, VDIM), qmap),
                       pl.BlockSpec((1, t, 1), lambda h, p, qt_ref, kt_ref: (h, qt_ref[p], 0))],
            scratch_shapes=[pltpu.VMEM((t, LANE), F32), pltpu.VMEM((t, LANE), F32), pltpu.VMEM((t, VDIM), F32)]),
        compiler_params=_params(("parallel", "arbitrary")),
    )(qt, kt, q_att, k_att, v)


def _attn_bwd_dq(q_att, k_att, v, do, o32, lse, rows):
    t = _tile(rows, 640, LANE)
    nb = rows // t

    def body(qt_ref, kt_ref, q_ref, k_ref, v_ref, do_ref, o_ref, lse_ref, dq_ref, delta_ref, acc_sc, dl_sc, lse_sc):
        qi, kj = qt_ref[pl.program_id(1)], kt_ref[pl.program_id(1)]

        @pl.when(kj == 0)
        def _():
            acc_sc[...] = jnp.zeros_like(acc_sc)
            delta = jnp.sum(do_ref[...].astype(F32) * o_ref[...], axis=1, keepdims=True)
            delta_ref[0] = delta
            dl_sc[...] = jnp.broadcast_to(delta, (t, LANE))
            lse_sc[...] = jnp.broadcast_to(lse_ref[0], (t, LANE))

        def step(masked):
            s = lax.dot_general(q_ref[...], k_ref[...], _NT, preferred_element_type=F32) * SCORE_TO_LOG2
            p = jnp.exp2(s - jnp.tile(lse_sc[...], (1, t // LANE)))
            if masked:
                p = jnp.where(_attn_mask(qi, kj, t), p, 0.0)
            dp = lax.dot_general(do_ref[...], v_ref[...], _NT, preferred_element_type=F32)
            ds = p * (dp - jnp.tile(dl_sc[...], (1, t // LANE)))
            acc_sc[...] += jnp.dot(ds.astype(BF16), k_ref[...], preferred_element_type=F32)

        pl.when((kj == qi) | (kj == 0))(functools.partial(step, True))
        pl.when((kj < qi) & (kj > 0))(functools.partial(step, False))

        @pl.when(kj == qi)
        def _():
            dq_ref[...] = acc_sc[...] * ATTN_SCALE

    qt, kt = _causal_pairs(nb, by_key=False)
    qmap = lambda h, p, qt_ref, kt_ref: (qt_ref[p], h)
    kmap = lambda h, p, qt_ref, kt_ref: (kt_ref[p], h)
    stat = pl.BlockSpec((1, t, 1), lambda h, p, qt_ref, kt_ref: (h, qt_ref[p], 0))
    return pl.pallas_call(
        body,
        name="attn_bwd_dq",
        out_shape=[jax.ShapeDtypeStruct((rows, HEADS * QHEAD_W), F32), jax.ShapeDtypeStruct((HEADS, rows, 1), F32)],
        grid_spec=pltpu.PrefetchScalarGridSpec(
            num_scalar_prefetch=2,
            grid=(HEADS, len(qt)),
            in_specs=[pl.BlockSpec((t, QHEAD_W), qmap), pl.BlockSpec((t, QHEAD_W), kmap), pl.BlockSpec((t, VDIM), kmap),
                      pl.BlockSpec((t, VDIM), qmap), pl.BlockSpec((t, VDIM), qmap), stat],
            out_specs=[pl.BlockSpec((t, QHEAD_W), qmap), stat],
            scratch_shapes=[pltpu.VMEM((t, QHEAD_W), F32), pltpu.VMEM((t, LANE), F32), pltpu.VMEM((t, LANE), F32)]),
        compiler_params=_params(("parallel", "arbitrary")),
    )(qt, kt, q_att, k_att, v, do, o32, lse)


def _attn_bwd_dkv(q_att, k_att, v, do, lse_row, delta_row, rows):
    t = _tile(rows, 640, LANE)
    nb = rows // t

    def body(qt_ref, kt_ref, q_ref, k_ref, v_ref, do_ref, lse_ref, delta_ref, dk_ref, dv_ref, dk_sc, dv_sc):
        qi, kj = qt_ref[pl.program_id(1)], kt_ref[pl.program_id(1)]

        @pl.when(qi == kj)
        def _():
            dk_sc[...] = jnp.zeros_like(dk_sc)
            dv_sc[...] = jnp.zeros_like(dv_sc)

        def step(masked):
            st = lax.dot_general(k_ref[...], q_ref[...], _NT, preferred_element_type=F32) * SCORE_TO_LOG2
            pt = jnp.exp2(st - lse_ref[0])
            if masked:
                pt = jnp.where(_attn_mask(qi, kj, t, keys_on_rows=True), pt, 0.0)
            dv_sc[...] += jnp.dot(pt.astype(BF16), do_ref[...], preferred_element_type=F32)
            dpt = lax.dot_general(v_ref[...], do_ref[...], _NT, preferred_element_type=F32)
            dst = pt * (dpt - delta_ref[0])
            dk_sc[...] += jnp.dot(dst.astype(BF16), q_ref[...], preferred_element_type=F32)

        pl.when((qi == kj) | (kj == 0))(functools.partial(step, True))
        pl.when((qi > kj) & (kj > 0))(functools.partial(step, False))

        @pl.when(qi == nb - 1)
        def _():
            dk_ref[...] = dk_sc[...] * ATTN_SCALE
            dv_ref[...] = dv_sc[...]

    qt, kt = _causal_pairs(nb, by_key=True)
    qmap = lambda h, p, qt_ref, kt_ref: (qt_ref[p], h)
    kmap = lambda h, p, qt_ref, kt_ref: (kt_ref[p], h)
    stat = pl.BlockSpec((1, 1, t), lambda h, p, qt_ref, kt_ref: (h, 0, qt_ref[p]))
    return pl.pallas_call(
        body,
        name="attn_bwd_dkv",
        out_shape=[jax.ShapeDtypeStruct((rows, HEADS * QHEAD_W), F32), jax.ShapeDtypeStruct((rows, HEADS * VDIM), F32)],
        grid_spec=pltpu.PrefetchScalarGridSpec(
            num_scalar_prefetch=2,
            grid=(HEADS, len(qt)),
            in_specs=[pl.BlockSpec((t, QHEAD_W), qmap), pl.BlockSpec((t, QHEAD_W), kmap), pl.BlockSpec((t, VDIM), kmap),
                      pl.BlockSpec((t, VDIM), qmap), stat, stat],
            out_specs=[pl.BlockSpec((t, QHEAD_W), kmap), pl.BlockSpec((t, VDIM), kmap)],
            scratch_shapes=[pltpu.VMEM((t, QHEAD_W), F32), pltpu.VMEM((t, VDIM), F32)]),
        compiler_params=_params(("parallel", "arbitrary")),
    )(qt, kt, q_att, k_att, v, do, lse_row, delta_row)


C = BLOCK


def _hgrn_prep(hq, hf, hi, lb, c):
    rows = c * C + lax.broadcasted_iota(jnp.int32, (C, C), 0)
    valid = rows >= PAD_LEN
    sg = _sigmoid(hf)
    f = lb + (1.0 - lb) * sg
    g = jnp.where(valid, jnp.log(f), 0.0)
    k = jnp.where(valid, 1.0 - f, 0.0)
    q = _silu(hq)
    r = lax.broadcasted_iota(jnp.int32, (C, C), 0)
    cc = lax.broadcasted_iota(jnp.int32, (C, C), 1)
    tri = jnp.where(cc <= r, 1.0, 0.0).astype(F32)
    b = jnp.dot(tri, g, precision=lax.Precision.HIGHEST, preferred_element_type=F32)
    return q, k, hi, b, f, sg, valid


def _last_row_as_col(b_t):
    lane = lax.broadcasted_iota(jnp.int32, b_t.shape, 1)
    return jnp.sum(jnp.where(lane == C - 1, b_t, 0.0), axis=1, keepdims=True)


def _k_scaled(k, b, bs):
    return (k * jnp.exp(jnp.minimum(bs - b, 0.0))).astype(BF16)


def _hgrn_fwd(proj, lb, rows):
    nc = rows // C

    def body(hq_ref, hf_ref, hi_ref, lb_ref, o_ref, a_ref, s_ref, s_sc, b_sc):
        c = pl.program_id(1)

        @pl.when(c == 0)
        def _():
            s_sc[...] = jnp.zeros_like(s_sc)

        q, k, v, b, _, _, _ = _hgrn_prep(hq_ref[...], hf_ref[...], hi_ref[...], lb_ref[...], c)
        b_sc[...] = b
        s0 = s_sc[...]
        s_ref[0, 0] = s0
        v_bf = v.astype(BF16)
        r16 = lax.broadcasted_iota(jnp.int32, (SUB, C), 0)
        c16 = lax.broadcasted_iota(jnp.int32, (SUB, C), 1)
        slabs = [jnp.zeros((SUB, C), F32)]
        for i in range(1, C // SUB):
            bs = b_sc[SUB * i - 1:SUB * i, :]
            qs = (q[SUB * i:SUB * (i + 1)] * jnp.exp(b[SUB * i:SUB * (i + 1)] - bs)).astype(BF16)
            a_i = lax.dot_general(qs, _k_scaled(k, b, bs), _NT, preferred_element_type=F32)
            slabs.append(jnp.where(c16 <= r16 + (SUB * i - SUB), a_i, 0.0))
        a_off = jnp.concatenate(slabs, axis=0)
        q_t, k_t, b_t = q.T, k.T, b.T
        sub = lax.broadcasted_iota(jnp.int32, (C, C), 0)
        lane = lax.broadcasted_iota(jnp.int32, (C, C), 1)
        lane1 = lax.broadcasted_iota(jnp.int32, (1, C), 1)
        at_band = jnp.zeros((C, C), F32)
        for dl in range(SUB):
            k_s = pltpu.roll(k_t, dl, 1) if dl else k_t
            b_s = pltpu.roll(b_t, dl, 1) if dl else b_t
            e = jnp.exp(jnp.minimum(b_t - b_s, 0.0))
            band = jnp.sum(q_t * k_s * e, axis=0, keepdims=True)
            band = jnp.where(lane1 >= dl, band, 0.0)
            at_band = at_band + jnp.where(sub == lane - dl, jnp.broadcast_to(band, (C, C)), 0.0)
        a = (a_off + at_band.T).astype(BF16)
        a_ref[0] = a
        qe = (q * jnp.exp(b)).astype(BF16)
        o_ref[...] = (jnp.dot(a, v_bf, preferred_element_type=F32)
                      + jnp.dot(qe, s0.astype(BF16), preferred_element_type=F32))
        b_last = b_sc[C - 1:C, :]
        kd = (k * jnp.exp(b_last - b)).astype(BF16)
        s_sc[...] = (jnp.exp(_last_row_as_col(b_t)) * s0
                     + lax.dot_general(kd, v_bf, _TN, preferred_element_type=F32))

    seg = lambda base: (lambda h, c: (c, base // C + h))
    return pl.pallas_call(
        body,
        name="hgrn_fwd",
        out_shape=[jax.ShapeDtypeStruct((rows, D_MODEL), F32), jax.ShapeDtypeStruct((HEADS, rows, C), BF16),
                   jax.ShapeDtypeStruct((HEADS, nc, C, C), F32)],
        grid=(HEADS, nc),
        in_specs=[pl.BlockSpec((C, C), seg(SEG_HQ)), pl.BlockSpec((C, C), seg(SEG_HF)), pl.BlockSpec((C, C), seg(SEG_HI)),
                  pl.BlockSpec((1, C), lambda h, c: (0, h))],
        out_specs=[pl.BlockSpec((C, C), lambda h, c: (c, h)), pl.BlockSpec((1, C, C), lambda h, c: (h, c, 0)),
                   pl.BlockSpec((1, 1, C, C), lambda h, c: (h, c, 0, 0))],
        scratch_shapes=[pltpu.VMEM((C, C), F32), pltpu.VMEM((C, C), F32)],
        compiler_params=_params(("parallel", "arbitrary")),
    )(proj, proj, proj, lb)


def _hgrn_bwd(proj, lb, a_mat, s_states, do_h, rows):
    nc = rows // C

    def body(hq_ref, hf_ref, hi_ref, lb_ref, a_ref, s_ref, do_ref, dhq_ref, dhf_ref, dhi_ref, dlb_ref, ds_sc, b_sc):
        step = pl.program_id(1)
        c = nc - 1 - step

        @pl.when(step == 0)
        def _():
            ds_sc[...] = jnp.zeros_like(ds_sc)
            dlb_ref[...] = jnp.zeros_like(dlb_ref)

        hq, hf = hq_ref[...], hf_ref[...]
        lb_row = lb_ref[...]
        q, k, v, b, f, sg, valid = _hgrn_prep(hq, hf, hi_ref[...], lb_row, c)
        b_sc[...] = b
        s0 = s_ref[0, 0]
        ds1 = ds_sc[...]
        s0_bf, ds1_bf = s0.astype(BF16), ds1.astype(BF16)
        do = do_ref[...]
        do_bf, v_bf = do.astype(BF16), v.astype(BF16)
        b_last = b_sc[C - 1:C, :]
        e_last = jnp.exp(b_last - b)
        eb = jnp.exp(b)
        sub = lax.broadcasted_iota(jnp.int32, (C, C), 0)
        lane = lax.broadcasted_iota(jnp.int32, (C, C), 1)
        r16 = lax.broadcasted_iota(jnp.int32, (SUB, C), 0)
        c16 = lax.broadcasted_iota(jnp.int32, (SUB, C), 1)

        dv = (lax.dot_general(a_ref[0], do_bf, _TN, preferred_element_type=F32)
              + jnp.dot((k * e_last).astype(BF16), ds1_bf, preferred_element_type=F32))
        da = jnp.where(lane <= sub, lax.dot_general(do_bf, v_bf, _NT, preferred_element_type=F32), 0.0)
        da_t = jnp.where(sub <= lane, lax.dot_general(v_bf, do_bf, _NT, preferred_element_type=F32), 0.0)

        dq_slabs = [jnp.zeros((SUB, C), F32)]
        for i in range(1, C // SUB):
            bs = b_sc[SUB * i - 1:SUB * i, :]
            da_i = jnp.where(c16 <= r16 + (SUB * i - SUB), da[SUB * i:SUB * (i + 1)], 0.0).astype(BF16)
            dq_slabs.append(jnp.exp(b[SUB * i:SUB * (i + 1)] - bs)
                            * jnp.dot(da_i, _k_scaled(k, b, bs), preferred_element_type=F32))
        dk_slabs = []
        for j in range(C // SUB - 1):
            be = b_sc[SUB * j + SUB - 1:SUB * (j + 1), :]
            qe_j = (q * jnp.exp(jnp.minimum(b - be, 0.0))).astype(BF16)
            da_j = jnp.where(c16 >= r16 + (SUB * j + SUB), da_t[SUB * j:SUB * (j + 1)], 0.0).astype(BF16)
            dk_slabs.append(jnp.exp(be - b[SUB * j:SUB * (j + 1)]) * jnp.dot(da_j, qe_j, preferred_element_type=F32))
        dk_slabs.append(jnp.zeros((SUB, C), F32))

        q_t, k_t, b_t = q.T, k.T, b.T
        lane1 = lax.broadcasted_iota(jnp.int32, (1, C), 1)
        dq_t = jnp.zeros((C, C), F32)
        dk_t = jnp.zeros((C, C), F32)
        for dl in range(SUB):
            k_s = pltpu.roll(k_t, dl, 1) if dl else k_t
            b_s = pltpu.roll(b_t, dl, 1) if dl else b_t
            e = jnp.exp(jnp.minimum(b_t - b_s, 0.0))
            dband = jnp.sum(jnp.where(sub == lane - dl, da_t, 0.0), axis=0, keepdims=True)
            w = jnp.where(lane1 >= dl, dband, 0.0) * e
            dq_t = dq_t + w * k_s
            back = w * q_t
            dk_t = dk_t + (pltpu.roll(back, C - dl, 1) if dl else back)

        dq = eb * lax.dot_general(do_bf, s0_bf, _NT, preferred_element_type=F32) + jnp.concatenate(dq_slabs, axis=0) + dq_t.T
        dk_inter = e_last * lax.dot_general(v_bf, ds1_bf, _NT, preferred_element_type=F32)
        dk = dk_inter + jnp.concatenate(dk_slabs, axis=0) + dk_t.T

        extra = (jnp.exp(b_last) * jnp.sum((s0 * ds1).T, axis=0, keepdims=True)
                 + jnp.sum(k * dk_inter, axis=0, keepdims=True))
        db = q * dq - k * dk + jnp.where(sub == C - 1, jnp.broadcast_to(extra, (C, C)), 0.0)
        tri_t = jnp.where(lane >= sub, 1.0, 0.0).astype(F32)
        dg = jnp.dot(tri_t, db, precision=lax.Precision.HIGHEST, preferred_element_type=F32)
        ds_sc[...] = (jnp.exp(_last_row_as_col(b_t)) * ds1
                      + lax.dot_general((q * eb).astype(BF16), do_bf, _TN, preferred_element_type=F32))

        df = jnp.where(valid, dg / f - dk, 0.0)
        dhf_ref[...] = (df * (1.0 - lb_row) * sg * (1.0 - sg)).astype(BF16)
        dlb_ref[...] += jnp.sum(df * (1.0 - sg), axis=0, keepdims=True)
        dhq_ref[...] = (dq * _dsilu(hq)).astype(BF16)
        dhi_ref[...] = dv.astype(BF16)

    seg = lambda base: (lambda h, s: (nc - 1 - s, base // C + h))
    rmap = lambda h, s: (nc - 1 - s, h)
    return pl.pallas_call(
        body,
        name="hgrn_bwd",
        out_shape=[jax.ShapeDtypeStruct((rows, D_MODEL), BF16)] * 3 + [jax.ShapeDtypeStruct((1, D_MODEL), F32)],
        grid=(HEADS, nc),
        in_specs=[pl.BlockSpec((C, C), seg(SEG_HQ)), pl.BlockSpec((C, C), seg(SEG_HF)), pl.BlockSpec((C, C), seg(SEG_HI)),
                  pl.BlockSpec((1, C), lambda h, s: (0, h)),
                  pl.BlockSpec((1, C, C), lambda h, s: (h, nc - 1 - s, 0)),
                  pl.BlockSpec((1, 1, C, C), lambda h, s: (h, nc - 1 - s, 0, 0)),
                  pl.BlockSpec((C, C), rmap)],
        out_specs=[pl.BlockSpec((C, C), rmap)] * 3 + [pl.BlockSpec((1, C), lambda h, s: (0, h))],
        scratch_shapes=[pltpu.VMEM((C, C), F32), pltpu.VMEM((C, C), F32)],
        compiler_params=_params(("parallel", "arbitrary")),
    )(proj, proj, proj, lb, a_mat, s_states, do_h)


CONV_TC = 512
HALO = 8


def _conv_taps(i, tm, g_ref, pg_ref):
    shape = g_ref.shape
    r = lax.broadcasted_iota(jnp.int32, shape, 0)
    g = jnp.where(i * tm + r >= PAD_LEN, g_ref[...], 0.0)
    p1 = jnp.where(i * tm - 1 >= PAD_LEN, pg_ref[HALO - 1:HALO, :], 0.0)
    p2 = jnp.where(i * tm - 2 >= PAD_LEN, pg_ref[HALO - 2:HALO - 1, :], 0.0)
    s1 = jnp.where(r == 0, p1, pltpu.roll(g, 1, 0))
    s2 = jnp.where(r == 0, p2, jnp.where(r == 1, p1, pltpu.roll(g, 2, 0)))
    return g, s1, s2


def _conv_specs(tm, tc, ncb, order):
    gate = pl.BlockSpec((tm, tc), lambda *ids: order(ids))
    halo = pl.BlockSpec((HALO, tc), lambda *ids: (jnp.maximum(order(ids)[0] * (tm // HALO) - 1, 0), order(ids)[1]))
    up = pl.BlockSpec((tm, tc), lambda *ids: (order(ids)[0], ncb + order(ids)[1]))
    return gate, halo, up


def _conv_fwd(ffn, conv_w, conv_b, rows, tm):
    tc = CONV_TC
    ncb = D_FF // tc

    def body(g_ref, pg_ref, up_ref, cw_ref, cb_ref, act_ref):
        i = pl.program_id(0)
        g, s1, s2 = _conv_taps(i, tm, g_ref, pg_ref)
        conv = (cw_ref[0:1, :] * s2 + cw_ref[1:2, :] * s1 + cw_ref[2:3, :] * g) + cb_ref[...]
        act_ref[...] = (_silu(conv) * up_ref[...]).astype(BF16)

    gate, halo, up = _conv_specs(tm, tc, ncb, lambda ids: (ids[0], ids[1]))
    return pl.pallas_call(
        body,
        name="conv_fwd",
        out_shape=jax.ShapeDtypeStruct((rows, D_FF), BF16),
        grid=(rows // tm, ncb),
        in_specs=[gate, halo, up, pl.BlockSpec((3, tc), lambda i, j: (0, j)), pl.BlockSpec((1, tc), lambda i, j: (0, j))],
        out_specs=pl.BlockSpec((tm, tc), lambda i, j: (i, j)),
        compiler_params=_params(("parallel", "parallel")),
    )(ffn, ffn, ffn, conv_w, conv_b)


def _conv_bwd_a(ffn, dact, conv_w, conv_b, rows, tm):
    tc = CONV_TC
    ncb = D_FF // tc

    def body(g_ref, pg_ref, up_ref, da_ref, cw_ref, cb_ref, dc_ref, dffn_ref, w0_ref, w1_ref, w2_ref, db_ref):
        i = pl.program_id(1)
        g, s1, s2 = _conv_taps(i, tm, g_ref, pg_ref)
        conv = (cw_ref[0:1, :] * s2 + cw_ref[1:2, :] * s1 + cw_ref[2:3, :] * g) + cb_ref[...]
        da = da_ref[...]
        dffn_ref[...] = (da * _silu(conv)).astype(BF16)
        dc = da * up_ref[...] * _dsilu(conv)
        dc_ref[...] = dc
        sums = [jnp.sum(dc * s2, axis=0, keepdims=True), jnp.sum(dc * s1, axis=0, keepdims=True),
                jnp.sum(dc * g, axis=0, keepdims=True), jnp.sum(dc, axis=0, keepdims=True)]
        for ref, val in zip((w0_ref, w1_ref, w2_ref, db_ref), sums):
            @pl.when(i == 0)
            def _(ref=ref, val=val):
                ref[...] = val

            @pl.when(i > 0)
            def _(ref=ref, val=val):
                ref[...] += val

    gate, halo, up = _conv_specs(tm, tc, ncb, lambda ids: (ids[1], ids[0]))
    col = pl.BlockSpec((1, tc), lambda j, i: (0, j))
    return pl.pallas_call(
        body,
        name="conv_bwd_a",
        out_shape=[jax.ShapeDtypeStruct((rows, D_FF), F32), jax.ShapeDtypeStruct((rows, 2 * D_FF), BF16)]
        + [jax.ShapeDtypeStruct((1, D_FF), F32)] * 4,
        grid=(ncb, rows // tm),
        in_specs=[gate, halo, up, pl.BlockSpec((tm, tc), lambda j, i: (i, j)),
                  pl.BlockSpec((3, tc), lambda j, i: (0, j)), col],
        out_specs=[pl.BlockSpec((tm, tc), lambda j, i: (i, j)), pl.BlockSpec((tm, tc), lambda j, i: (i, ncb + j)),
                   col, col, col, col],
        compiler_params=_params(("parallel", "arbitrary")),
    )(ffn, ffn, ffn, dact, conv_w, conv_b)


def _conv_bwd_b(dconv, conv_w, dffn, rows, tm):
    tc = CONV_TC
    ncb = D_FF // tc
    nrb = rows // tm

    def body(dc_ref, nx_ref, cw_ref, dffn_in, out_ref):
        del dffn_in
        i = pl.program_id(0)
        dc = dc_ref[...]
        r = lax.broadcasted_iota(jnp.int32, dc.shape, 0)
        last = i == nrb - 1
        x1 = jnp.where(last, 0.0, nx_ref[0:1, :])
        x2 = jnp.where(last, 0.0, nx_ref[1:2, :])
        n1 = jnp.where(r == tm - 1, x1, pltpu.roll(dc, tm - 1, 0))
        n2 = jnp.where(r == tm - 1, x2, jnp.where(r == tm - 2, x1, pltpu.roll(dc, tm - 2, 0)))
        dg = cw_ref[2:3, :] * dc + cw_ref[1:2, :] * n1 + cw_ref[0:1, :] * n2
        out_ref[...] = jnp.where(i * tm + r >= PAD_LEN, dg, 0.0).astype(BF16)

    return pl.pallas_call(
        body,
        name="conv_bwd_b",
        out_shape=jax.ShapeDtypeStruct((rows, 2 * D_FF), BF16),
        grid=(nrb, ncb),
        in_specs=[pl.BlockSpec((tm, tc), lambda i, j: (i, j)),
                  pl.BlockSpec((HALO, tc), lambda i, j: (jnp.minimum((i + 1) * (tm // HALO), rows // HALO - 1), j)),
                  pl.BlockSpec((3, tc), lambda i, j: (0, j)),
                  pl.BlockSpec(memory_space=pl.ANY)],
        out_specs=pl.BlockSpec((tm, tc), lambda i, j: (i, j)),
        input_output_aliases={3: 0},
        compiler_params=_params(("parallel", "parallel")),
    )(dconv, dconv, conv_w, dffn)


def _final_call(h1, y, target, g_final, rows):
    tm = BLOCK

    def fn(i, tm_, h1v, yv, tgt, g):
        h2 = h1v + yv
        out = _rms_fwd(h2, g)
        err = jnp.where(i > 0, out - tgt, 0.0)
        loss = 0.5 * jnp.sum(jnp.mean(err * err, axis=-1, keepdims=True), axis=0, keepdims=True)
        dx, dg = _rms_bwd(h2, g, err * (1.0 / D_MODEL))
        return dx, dx, jnp.broadcast_to(loss, (1, LANE)), dg

    n_in = 4
    in_specs = [pl.BlockSpec((tm, D_MODEL), lambda i: (i, 0)), pl.BlockSpec((tm, D_MODEL), lambda i: (i, 0)),
                pl.BlockSpec((tm, D_MODEL), lambda i: (jnp.maximum(i - 1, 0), 0)),
                pl.BlockSpec((1, D_MODEL), lambda i: (0, 0))]

    def body(*refs):
        i = pl.program_id(0)
        dx, dx2, loss, dg = fn(i, tm, *[r[...] for r in refs[:n_in]])
        refs[4][...] = dx
        refs[5][...] = dx2.astype(BF16)
        for ref, val in ((refs[6], loss), (refs[7], dg)):
            @pl.when(i == 0)
            def _(ref=ref, val=val):
                ref[...] = val

            @pl.when(i > 0)
            def _(ref=ref, val=val):
                ref[...] += val

    return pl.pallas_call(
        body,
        name="final_loss",
        out_shape=[jax.ShapeDtypeStruct((rows, D_MODEL), F32), jax.ShapeDtypeStruct((rows, D_MODEL), BF16),
                   jax.ShapeDtypeStruct((1, LANE), F32), jax.ShapeDtypeStruct((1, D_MODEL), F32)],
        grid=(rows // tm,),
        in_specs=in_specs,
        out_specs=[pl.BlockSpec((tm, D_MODEL), lambda i: (i, 0)), pl.BlockSpec((tm, D_MODEL), lambda i: (i, 0)),
                   pl.BlockSpec((1, LANE), lambda i: (0, 0)), pl.BlockSpec((1, D_MODEL), lambda i: (0, 0))],
        compiler_params=_params(("arbitrary",)),
    )(h1, y, target, g_final)


def _heads_map(fn, *slabs):
    outs = [fn(*[s[:, h * LANE:(h + 1) * LANE] for s in slabs]) for h in range(HEADS)]
    if isinstance(outs[0], tuple):
        return tuple(jnp.concatenate([o[k] for o in outs], axis=1) for k in range(len(outs[0])))
    return jnp.concatenate(outs, axis=1)


def _local_step(x, positions, target, w, p):
    s_len = x.shape[0]
    rows = s_len + BLOCK
    tm = _tile(rows, 640, 8)
    row = lambda arr, width, cb=0: ("row", arr, width, cb)

    h0 = jnp.concatenate([jnp.zeros((PAD_LEN, D_MODEL), F32), w["meta_tokens"], x], axis=0)
    pos = jnp.concatenate([jnp.zeros((PAD_LEN,), jnp.int32), jnp.arange(N_META, dtype=jnp.int32),
                           positions.astype(jnp.int32) + N_META])
    inv = 1.0 / (ROPE_THETA ** (jnp.arange(0, ROPE, 2, dtype=F32) / ROPE))
    ang = pos.astype(F32)[:, None] * inv
    zero = jnp.zeros((rows, LANE - ROPE), F32)
    cos_t = jnp.concatenate([jnp.cos(ang), jnp.cos(ang), zero], axis=1)
    sin_t = jnp.concatenate([-jnp.sin(ang), jnp.sin(ang), zero], axis=1)
    lb_r0, lb_r1 = p["lb_raw"][0:1], p["lb_raw"][1:2]

    def lb_fn(i, tm_, r0, r1):
        m = jnp.maximum(r0, r1)
        e0, e1 = jnp.exp(r0 - m), jnp.exp(r1 - m)
        return (e0 / (e0 + e1),)

    (lb,) = _rowwise("lb_fwd", lb_fn, [("bc", lb_r0), ("bc", lb_r1)], [("acc", (1, D_MODEL))], 1, 1)

    (u1,) = _rowwise("mix_norm", lambda i, t, h, g: (_rms_fwd(h, g),),
                     [row(h0, D_MODEL), ("bc", p["g_mix_norm"])], [("row", D_MODEL, BF16)], rows, tm)
    proj = _matmul(u1, w["w_in"], "nn", F32, "mm_proj")
    qn, kvn = _rowwise(
        "latent_norm", lambda i, t, ql, kl, gq, gk: (_rms_fwd(ql, gq), _rms_fwd(kl, gk)),
        [row(proj, Q_LORA, 0), row(proj, KV_LORA, SEG_KV_LAT // KV_LORA), ("bc", p["g_q_norm"]), ("bc", p["g_kv_norm"])],
        [("row", Q_LORA, BF16), ("row", KV_LORA, BF16)], rows, tm)
    q_raw = _matmul(qn, w["w_q_up"], "nn", F32, "mm_q_up")
    kv = _matmul(kvn, w["w_kv_up"], "nn", F32, "mm_kv_up")
    q_att, k_att, v_att = _rope_fwd_call(q_raw, kv, proj, cos_t, sin_t, rows, tm)
    o32, o_bf, lse = _attn_fwd(q_att, k_att, v_att, rows)
    o_h, a_mat, s_states = _hgrn_fwd(proj, lb, rows)

    def hgrn_post(i, t, oh, hg, g):
        return (_heads_map(lambda a, b: _rms_fwd(a, g) * _silu(b), oh, hg),)

    (o_hgrn,) = _rowwise("hgrn_post", hgrn_post,
                         [row(o_h, D_MODEL), row(proj, D_MODEL, SEG_HG // D_MODEL), ("bc", p["g_hgrn_norm"])],
                         [("row", D_MODEL, BF16)], rows, tm)
    br_a = _matmul(o_bf, w["w_branch_mla"], "nn", F32, "mm_branch_mla")
    br_b = _matmul(o_hgrn, w["w_branch_hgrn"], "nn", F32, "mm_branch_hgrn")
    (merged,) = _rowwise(
        "merge", lambda i, t, a, b, ga, gb: (_sigmoid(ga) * a + _sigmoid(gb) * b,),
        [row(br_a, D_MODEL), row(br_b, D_MODEL), row(proj, D_MODEL, SEG_GA // D_MODEL), row(proj, D_MODEL, SEG_GB // D_MODEL)],
        [("row", D_MODEL, BF16)], rows, tm)
    mix_out = _matmul(merged, w["w_out"], "nn", F32, "mm_out")

    def ffn_norm(i, t, h, mo, g):
        h1v = h + mo
        return h1v, _rms_fwd(h1v, g)

    h1, u2 = _rowwise("ffn_norm", ffn_norm, [row(h0, D_MODEL), row(mix_out, D_MODEL), ("bc", p["g_ffn_norm"])],
                      [("row", D_MODEL, F32), ("row", D_MODEL, BF16)], rows, tm)
    ffn = _matmul(u2, w["w_ffn_in"], "nn", F32, "mm_ffn_in")
    act = _conv_fwd(ffn, w["conv_w"], p["conv_b"], rows, tm)
    y = _matmul(act, w["w_ffn_out"], "nn", F32, "mm_ffn_out")
    dh2, dh2_bf, loss_acc, dg_final = _final_call(h1, y, target, p["g_final_norm"].reshape(1, D_MODEL), rows)

    grads = {"g_final_norm": dg_final.reshape(D_MODEL)}
    dact = _matmul(dh2_bf, w["w_ffn_out"], "nt", F32, "mm_d_act")
    grads["w_ffn_out"] = _matmul(act, dh2_bf, "tn", BF16, "mm_dw_ffn_out")
    dconv, dffn, dcw0, dcw1, dcw2, dcb = _conv_bwd_a(ffn, dact, w["conv_w"], p["conv_b"], rows, tm)
    dffn = _conv_bwd_b(dconv, w["conv_w"], dffn, rows, tm)
    grads["conv_w"] = jnp.concatenate([dcw0, dcw1, dcw2], axis=0)
    grads["conv_b"] = dcb
    du2 = _matmul(dffn, w["w_ffn_in"], "nt", F32, "mm_d_u2")
    grads["w_ffn_in"] = _matmul(u2, dffn, "tn", BF16, "mm_dw_ffn_in")

    def ffn_norm_bwd(i, t, h, du, dh, g):
        dx, dg = _rms_bwd(h, g, du)
        dh1v = dh + dx
        return dh1v, dh1v, dg

    dh1, dh1_bf, grads["g_ffn_norm"] = _rowwise(
        "ffn_norm_bwd", ffn_norm_bwd, [row(h1, D_MODEL), row(du2, D_MODEL), row(dh2, D_MODEL), ("bc", p["g_ffn_norm"])],
        [("row", D_MODEL, F32), ("row", D_MODEL, BF16), ("acc", (1, D_MODEL))], rows, tm)
    dmerged = _matmul(dh1_bf, w["w_out"], "nt", F32, "mm_d_merged")
    grads["w_out"] = _matmul(merged, dh1_bf, "tn", BF16, "mm_dw_out")

    def merge_bwd(i, t, dm, a, b, ga, gb):
        sa, sb = _sigmoid(ga), _sigmoid(gb)
        return dm * sa, dm * sb, dm * a * sa * (1.0 - sa), dm * b * sb * (1.0 - sb)

    da_bf, db_bf, dga, dgb = _rowwise(
        "merge_bwd", merge_bwd,
        [row(dmerged, D_MODEL), row(br_a, D_MODEL), row(br_b, D_MODEL),
         row(proj, D_MODEL, SEG_GA // D_MODEL), row(proj, D_MODEL, SEG_GB // D_MODEL)],
        [("row", D_MODEL, BF16)] * 4, rows, tm)
    do_mla = _matmul(da_bf, w["w_branch_mla"], "nt", BF16, "mm_d_o_mla")
    grads["w_branch_mla"] = _matmul(o_bf, da_bf, "tn", BF16, "mm_dw_branch_mla")
    do_hgrn = _matmul(db_bf, w["w_branch_hgrn"], "nt", F32, "mm_d_o_hgrn")
    grads["w_branch_hgrn"] = _matmul(o_hgrn, db_bf, "tn", BF16, "mm_dw_branch_hgrn")

    def hgrn_post_bwd(i, t, dy, oh, hg, g):
        def one(dyh, ohh, hgh):
            dx, dg = _rms_bwd(ohh, g, dyh * _silu(hgh))
            return dx, dyh * _rms_fwd(ohh, g) * _dsilu(hgh), dg

        dx, dhg, dg = _heads_map(one, dy, oh, hg)
        dg_sum = dg[:, 0:LANE]
        for h in range(1, HEADS):
            dg_sum = dg_sum + dg[:, h * LANE:(h + 1) * LANE]
        return dx, dhg, dg_sum

    do_h, dhg, grads["g_hgrn_norm"] = _rowwise(
        "hgrn_post_bwd", hgrn_post_bwd,
        [row(do_hgrn, D_MODEL), row(o_h, D_MODEL), row(proj, D_MODEL, SEG_HG // D_MODEL), ("bc", p["g_hgrn_norm"])],
        [("row", D_MODEL, F32), ("row", D_MODEL, BF16), ("acc", (1, LANE))], rows, tm)
    dhq, dhf, dhi, dlb = _hgrn_bwd(proj, lb, a_mat, s_states, do_h, rows)

    def lb_bwd(i, tm_, d, l):
        t = d * l * (1.0 - l)
        return t, -t

    dlb0, dlb1 = _rowwise("lb_bwd", lb_bwd, [("bc", dlb), ("bc", lb)], [("acc", (1, D_MODEL))] * 2, 1, 1)
    grads["lb_raw"] = jnp.concatenate([dlb0, dlb1], axis=0)

    dq_att, delta = _attn_bwd_dq(q_att, k_att, v_att, do_mla, o32, lse, rows)
    dk_att, dv_att = _attn_bwd_dkv(q_att, k_att, v_att, do_mla, lse.reshape(HEADS, 1, rows),
                                   delta.reshape(HEADS, 1, rows), rows)
    dq_full, dkv, dkr = _rope_bwd_call(dq_att, dk_att, dv_att, cos_t, sin_t, rows, tm)
    dqn = _matmul(dq_full, w["w_q_up"], "nt", F32, "mm_d_qn")
    grads["w_q_up"] = _matmul(qn, dq_full, "tn", BF16, "mm_dw_q_up")
    dkvn = _matmul(dkv, w["w_kv_up"], "nt", F32, "mm_d_kvn")
    grads["w_kv_up"] = _matmul(kvn, dkv, "tn", BF16, "mm_dw_kv_up")

    def latent_norm_bwd(i, t, ql, kl, dq, dk, gq, gk):
        dql, dgq = _rms_bwd(ql, gq, dq)
        dkl, dgk = _rms_bwd(kl, gk, dk)
        return dql, dkl, dgq, dgk

    dq_lat, dkv_lat, grads["g_q_norm"], grads["g_kv_norm"] = _rowwise(
        "latent_norm_bwd", latent_norm_bwd,
        [row(proj, Q_LORA, 0), row(proj, KV_LORA, SEG_KV_LAT // KV_LORA), row(dqn, Q_LORA), row(dkvn, KV_LORA),
         ("bc", p["g_q_norm"]), ("bc", p["g_kv_norm"])],
        [("row", Q_LORA, BF16), ("row", KV_LORA, BF16), ("acc", (1, Q_LORA)), ("acc", (1, KV_LORA))], rows, tm)
    dproj = jnp.concatenate([dq_lat, dkv_lat, dhq, dhf, dhi, dhg, dga, dgb, dkr], axis=1)
    du1 = _matmul(dproj, w["w_in"], "nt", F32, "mm_d_u1")
    grads["w_in"] = _matmul(u1, dproj, "tn", BF16, "mm_dw_in")

    def mix_norm_bwd(i, t, h, du, dh, g):
        dx, dg = _rms_bwd(h, g, du)
        return dh + dx, dg

    dh0, grads["g_mix_norm"] = _rowwise(
        "mix_norm_bwd", mix_norm_bwd, [row(h0, D_MODEL), row(du1, D_MODEL), row(dh1, D_MODEL), ("bc", p["g_mix_norm"])],
        [("row", D_MODEL, F32), ("acc", (1, D_MODEL))], rows, tm)
    grads["meta_tokens"] = dh0[PAD_LEN:BLOCK]
    return loss_acc[0, 0], dh0[BLOCK:], grads


K_ROPE_AT = Q_LORA + KV_LORA
COL_SHARDED = ("w_in", "w_q_up", "w_kv_up", "w_ffn_in", "conv_w", "meta_tokens")
ROW_SHARDED = ("w_branch_mla", "w_branch_hgrn", "w_out", "w_ffn_out")
BIG = ("w_in", "w_q_up", "w_kv_up", "w_branch_mla", "w_branch_hgrn", "w_out", "w_ffn_in", "w_ffn_out")
SMALL = ("conv_b", "g_mix_norm", "g_q_norm", "g_kv_norm", "g_hgrn_norm", "g_ffn_norm", "g_final_norm", "lb_raw")


def _unshard(name, stacked):
    if name in COL_SHARDED:
        return jnp.transpose(stacked, (1, 0, 2)).reshape(stacked.shape[1], N_DEV * stacked.shape[2])
    return stacked.reshape(N_DEV * stacked.shape[1], stacked.shape[2])


def _reshard(name, full):
    if name in COL_SHARDED:
        r, c = full.shape
        return jnp.transpose(full.reshape(r, N_DEV, c // N_DEV), (1, 0, 2))
    return full.reshape(N_DEV, full.shape[0] // N_DEV, full.shape[1])


def _to_kernel_layout(full):
    out = dict(full)
    w_in = full["w_in"]
    pad = jnp.zeros((D_MODEL, KR_W - ROPE), w_in.dtype)
    out["w_in"] = jnp.concatenate([w_in[:, :K_ROPE_AT], w_in[:, K_ROPE_AT + ROPE:], w_in[:, K_ROPE_AT:K_ROPE_AT + ROPE], pad], axis=1)
    wq = full["w_q_up"].reshape(Q_LORA, HEADS, NOPE + ROPE)
    out["w_q_up"] = jnp.pad(wq, ((0, 0), (0, 0), (0, QHEAD_W - NOPE - ROPE))).reshape(Q_LORA, HEADS * QHEAD_W)
    return out


def _from_kernel_layout(grads):
    out = dict(grads)
    g = grads["w_in"]
    out["w_in"] = jnp.concatenate([g[:, :K_ROPE_AT], g[:, SEG_KR:SEG_KR + ROPE], g[:, K_ROPE_AT:SEG_KR]], axis=1)
    out["w_q_up"] = grads["w_q_up"].reshape(Q_LORA, HEADS, QHEAD_W)[:, :, :NOPE + ROPE].reshape(Q_LORA, HEADS * (NOPE + ROPE))
    return out


MESH_ID = pl.DeviceIdType.MESH
ANY = pl.BlockSpec(memory_space=pl.ANY)


def _slot(dev):
    return 4 * dev[0] + 2 * dev[1] + dev[2]


def _all_gather(shards):
    n = len(shards)

    def body(*refs):
        ins, outs = refs[:n], refs[n:2 * n]
        send_sems, recv_sems, local_sems = refs[2 * n:]
        x, y, c = lax.axis_index("x"), lax.axis_index("y"), lax.axis_index("c")
        me, sibling = (x, y, c), (x, y, 1 - c)
        chips = [(1 - x, y), (x, 1 - y), (1 - x, 1 - y)]

        def copy(a, k, block, to, src=None):
            dst = outs[a].at[_slot(block)]
            return pltpu.make_async_remote_copy(
                src_ref=dst if src is None else src, dst_ref=dst, send_sem=send_sems.at[a, k],
                recv_sem=recv_sems.at[a, k], device_id=to, device_id_type=MESH_ID)

        mine = [pltpu.make_async_copy(ins[a], outs[a].at[_slot(me)], local_sems.at[a]) for a in range(n)]
        for cp in mine:
            cp.start()
        first = []
        for a in range(n):
            first.append(copy(a, 0, me, sibling, src=ins[a]))
            first += [copy(a, 1 + j, me, (*chip, c), src=ins[a]) for j, chip in enumerate(chips)]
        for cp in first:
            cp.start()
        passed = []
        for a in range(n):
            for j, chip in enumerate(chips):
                copy(a, 1 + j, (*chip, c), me).wait_recv()
                fwd = copy(a, 4 + j, (*chip, c), sibling)
                fwd.start()
                passed.append(fwd)
        for a in range(n):
            copy(a, 0, sibling, me).wait_recv()
            for j, chip in enumerate(chips):
                copy(a, 4 + j, (*chip, 1 - c), me).wait_recv()
        for cp in first + passed:
            cp.wait_send()
        for cp in mine:
            cp.wait()

    return pl.pallas_call(
        body,
        name="gather_weights",
        out_shape=[jax.ShapeDtypeStruct((N_DEV,) + s.shape, s.dtype) for s in shards],
        in_specs=[ANY] * n,
        out_specs=[ANY] * n,
        scratch_shapes=[pltpu.SemaphoreType.DMA((n, 7)), pltpu.SemaphoreType.DMA((n, 7)), pltpu.SemaphoreType.DMA((n,))],
    )(*shards)


def _exchange(blocked, replicated):
    nb, n = len(blocked), len(blocked) + len(replicated)
    arrays = list(blocked) + list(replicated)

    def body(*refs):
        ins, outs = refs[:n], refs[n:2 * n]
        send_sems, recv_sems, local_sems = refs[2 * n:]
        x, y, c = lax.axis_index("x"), lax.axis_index("y"), lax.axis_index("c")
        me = (x, y, c)
        peers = [(x, y, 1 - c), (1 - x, y, c), (x, 1 - y, c), (1 - x, 1 - y, c),
                 (1 - x, y, 1 - c), (x, 1 - y, 1 - c), (1 - x, 1 - y, 1 - c)]

        def src_of(a, dev):
            return ins[a].at[_slot(dev)] if a < nb else ins[a]

        def copy(a, k, frm, to):
            return pltpu.make_async_remote_copy(
                src_ref=src_of(a, to), dst_ref=outs[a].at[_slot(frm)], send_sem=send_sems.at[a, k],
                recv_sem=recv_sems.at[a, k], device_id=to, device_id_type=MESH_ID)

        mine = [pltpu.make_async_copy(src_of(a, me), outs[a].at[_slot(me)], local_sems.at[a]) for a in range(n)]
        for cp in mine:
            cp.start()
        sends = [copy(a, k, me, peer) for a in range(n) for k, peer in enumerate(peers)]
        for cp in sends:
            cp.start()
        for a in range(n):
            for k, peer in enumerate(peers):
                copy(a, k, peer, me).wait_recv()
        for cp in sends:
            cp.wait_send()
        for cp in mine:
            cp.wait()

    return pl.pallas_call(
        body,
        name="exchange_grads",
        out_shape=[jax.ShapeDtypeStruct(s.shape, s.dtype) for s in blocked]
        + [jax.ShapeDtypeStruct((N_DEV,) + s.shape, s.dtype) for s in replicated],
        in_specs=[ANY] * n,
        out_specs=[ANY] * n,
        scratch_shapes=[pltpu.SemaphoreType.DMA((n, 7)), pltpu.SemaphoreType.DMA((n, 7)), pltpu.SemaphoreType.DMA((n,))],
    )(*arrays)


ADAMW_BLOCK_ELEMS = 256 * 1024


def _adamw(name, parts, w, m, v):
    r, c = w.shape
    tr = _tile(r, max(16, ADAMW_BLOCK_ELEMS // c), 16)

    def body(p_ref, w_ref, m_ref, v_ref, g_ref, d_ref, nm_ref, nv_ref):
        g = p_ref[0].astype(F32)
        for s in range(1, N_DEV):
            g = g + p_ref[s].astype(F32)
        m_new = ADAM_B1 * m_ref[...] + (1.0 - ADAM_B1) * g
        v_new = ADAM_B2 * v_ref[...] + (1.0 - ADAM_B2) * (g * g)
        m_hat = m_new / (1.0 - ADAM_B1 ** ADAM_STEP)
        v_hat = v_new / (1.0 - ADAM_B2 ** ADAM_STEP)
        g_ref[...] = g
        d_ref[...] = -ADAM_LR * (m_hat / (jnp.sqrt(v_hat) + ADAM_EPS) + ADAM_WD * w_ref[...])
        nm_ref[...] = m_new
        nv_ref[...] = v_new

    blk = pl.BlockSpec((tr, c), lambda i: (i, 0))
    return pl.pallas_call(
        body,
        name="adamw_" + name,
        out_shape=[jax.ShapeDtypeStruct((r, c), F32)] * 4,
        grid=(r // tr,),
        in_specs=[pl.BlockSpec((N_DEV, tr, c), lambda i: (0, i, 0)), blk, blk, blk],
        out_specs=[blk] * 4,
        compiler_params=_params(("parallel",)),
    )(parts, w, m, v)


def _pack_small(vals):
    return jnp.concatenate([v.reshape(-1, LANE) for v in vals], axis=0)


def _unpack_small(packed, like):
    out, at = [], 0
    for ref in like:
        n = ref.size // LANE
        out.append(packed[at:at + n].reshape(ref.shape))
        at += n
    return out


def kernel(x, positions, meta_tokens, w_in, w_q_up, w_kv_up, w_branch_mla, w_branch_hgrn, w_out, w_ffn_in, w_ffn_out, conv_w, conv_b, g_mix_norm, g_q_norm, g_kv_norm, g_hgrn_norm, g_ffn_norm, g_final_norm, lb_raw, loss_target, m_meta_tokens, m_w_in, m_w_q_up, m_w_kv_up, m_w_branch_mla, m_w_branch_hgrn, m_w_out, m_w_ffn_in, m_w_ffn_out, m_conv_w, m_conv_b, m_g_mix_norm, m_g_q_norm, m_g_kv_norm, m_g_hgrn_norm, m_g_ffn_norm, m_g_final_norm, m_lb_raw, v_meta_tokens, v_w_in, v_w_q_up, v_w_kv_up, v_w_branch_mla, v_w_branch_hgrn, v_w_out, v_w_ffn_in, v_w_ffn_out, v_conv_w, v_conv_b, v_g_mix_norm, v_g_q_norm, v_g_kv_norm, v_g_hgrn_norm, v_g_ffn_norm, v_g_final_norm, v_lb_raw):
    names = BIG + ("conv_w", "meta_tokens") + SMALL
    local = dict(zip(
        ("meta_tokens", "w_in", "w_q_up", "w_kv_up", "w_branch_mla", "w_branch_hgrn", "w_out", "w_ffn_in", "w_ffn_out",
         "conv_w", "conv_b", "g_mix_norm", "g_q_norm", "g_kv_norm", "g_hgrn_norm", "g_ffn_norm", "g_final_norm", "lb_raw"),
        (meta_tokens, w_in, w_q_up, w_kv_up, w_branch_mla, w_branch_hgrn, w_out, w_ffn_in, w_ffn_out,
         conv_w, conv_b, g_mix_norm, g_q_norm, g_kv_norm, g_hgrn_norm, g_ffn_norm, g_final_norm, lb_raw)))
    mom_m = dict(zip(local, (m_meta_tokens, m_w_in, m_w_q_up, m_w_kv_up, m_w_branch_mla, m_w_branch_hgrn, m_w_out, m_w_ffn_in,
                             m_w_ffn_out, m_conv_w, m_conv_b, m_g_mix_norm, m_g_q_norm, m_g_kv_norm, m_g_hgrn_norm,
                             m_g_ffn_norm, m_g_final_norm, m_lb_raw)))
    mom_v = dict(zip(local, (v_meta_tokens, v_w_in, v_w_q_up, v_w_kv_up, v_w_branch_mla, v_w_branch_hgrn, v_w_out, v_w_ffn_in,
                             v_w_ffn_out, v_conv_w, v_conv_b, v_g_mix_norm, v_g_q_norm, v_g_kv_norm, v_g_hgrn_norm,
                             v_g_ffn_norm, v_g_final_norm, v_lb_raw)))
    sharded = BIG + ("conv_w", "meta_tokens")

    def shard2d(name, arr):
        return arr.reshape(arr.shape[-2:]) if name != "meta_tokens" else arr

    shards = [shard2d(n, local[n]).astype(BF16) for n in BIG] + [shard2d(n, local[n]) for n in ("conv_w", "meta_tokens")]
    gathered = _all_gather(shards)
    full = _to_kernel_layout({n: _unshard(n, g) for n, g in zip(sharded, gathered)})
    small = {n: local[n] for n in SMALL}

    loss, grad_x, grads = _local_step(x[0], positions[0], loss_target[0], full, small)
    grads = _from_kernel_layout(grads)

    small_like = [local[n] for n in SMALL]
    blocked = [_reshard(n, grads[n]) for n in sharded]
    parts = _exchange(blocked, [_pack_small([grads[n] for n in SMALL])])
    out = {}
    for n, part in zip(sharded, parts[:-1]):
        res = _adamw(n, part, shard2d(n, local[n]), shard2d(n, mom_m[n]), shard2d(n, mom_v[n]))
        out[n] = [r.reshape(local[n].shape) for r in res]
    res = _adamw("replicated", parts[-1], _pack_small(small_like), _pack_small([mom_m[n] for n in SMALL]),
                 _pack_small([mom_v[n] for n in SMALL]))
    for k, n in enumerate(SMALL):
        out[n] = [_unpack_small(r, small_like)[k] for r in res]

    loss = lax.psum(loss, ("x", "y", "c"))
    order = tuple(local)
    return (loss, grad_x[None], *[out[n][0] for n in order], *[out[n][1] for n in order],
            *[out[n][2] for n in order], *[out[n][3] for n in order])
```

```python
import functools

import jax
import jax.numpy as jnp
import numpy as np
from jax import lax
from jax.experimental import pallas as pl
from jax.experimental.pallas import tpu as pltpu

F32 = jnp.float32
BF16 = jnp.bfloat16

D_MODEL = 2048
N_META = 16
BLOCK = 128
PAD_LEN = BLOCK - N_META
HEADS = 16
Q_LORA = 1536
KV_LORA = 512
ROPE = 64
NOPE = 128
VDIM = 128
D_FF = 5632
NORM_EPS = 1e-6
ROPE_THETA = 10000.0
ATTN_SCALE = (NOPE + ROPE) ** -0.5
ADAM_LR = 0.001
ADAM_B1 = 0.9
ADAM_B2 = 0.999
ADAM_EPS = 1e-08
ADAM_WD = 0.01
ADAM_STEP = 10
N_DEV = 8

LANE = 128
SEG_Q_LAT = 0
SEG_KV_LAT = Q_LORA
SEG_HQ = 2048
SEG_HF = SEG_HQ + D_MODEL
SEG_HI = SEG_HF + D_MODEL
SEG_HG = SEG_HI + D_MODEL
SEG_GA = SEG_HG + D_MODEL
SEG_GB = SEG_GA + D_MODEL
SEG_KR = SEG_GB + D_MODEL
KR_W = 256
PROJ_W = SEG_KR + KR_W
QHEAD_W = 256

V7X_VMEM_BYTES = 64 * 1024 * 1024
VMEM_LIMIT = V7X_VMEM_BYTES * 7 // 8
NEG_BIG = -1e30
SUB = 16


def _tile(n, target, mult):
    best = None
    for t in range(mult, min(n, target) + 1, mult):
        if n % t == 0:
            best = t
    return n if best is None else best


def _params(sem):
    return pltpu.CompilerParams(dimension_semantics=sem, vmem_limit_bytes=VMEM_LIMIT)


def _sigmoid(x):
    return 1.0 / (1.0 + jnp.exp(-x))


_DIMS = {"nn": (((1,), (0,)), ((), ())), "nt": (((1,), (1,)), ((), ())), "tn": (((0,), (0,)), ((), ()))}


def _matmul(a, b, mode, out_dtype, name, after=None):
    if mode == "nn":
        (m, k), (_, n) = a.shape, b.shape
    elif mode == "nt":
        (m, k), (n, _) = a.shape, b.shape
    else:
        (k, m), (_, n) = a.shape, b.shape
    tm = _tile(m, 1040, 8) if mode != "tn" else _tile(m, 1024, LANE)
    tn = _tile(n, 1024, LANE)
    tk = _tile(k, 2816, LANE) if mode != "tn" else _tile(k, 2080, 8)
    nk = k // tk
    if mode == "nn":
        a_spec = pl.BlockSpec((tm, tk), lambda i, j, kk: (i, kk))
        b_spec = pl.BlockSpec((tk, tn), lambda i, j, kk: (kk, j))
    elif mode == "nt":
        a_spec = pl.BlockSpec((tm, tk), lambda i, j, kk: (i, kk))
        b_spec = pl.BlockSpec((tn, tk), lambda i, j, kk: (j, kk))
    else:
        a_spec = pl.BlockSpec((tk, tm), lambda i, j, kk: (kk, i))
        b_spec = pl.BlockSpec((tk, tn), lambda i, j, kk: (kk, j))
    dims = _DIMS[mode]

    n_after = 0 if after is None else 1

    def body(a_ref, b_ref, *rest):
        o_ref, acc = rest[n_after], rest[n_after + 1:]
        part = lax.dot_general(a_ref[...], b_ref[...], dims, preferred_element_type=F32)
        if nk == 1:
            o_ref[...] = part.astype(o_ref.dtype)
            return
        acc_ref, kk = acc[0], pl.program_id(2)

        @pl.when(kk == 0)
        def _():
            acc_ref[...] = part

        @pl.when((kk > 0) & (kk < nk - 1))
        def _():
            acc_ref[...] += part

        @pl.when(kk == nk - 1)
        def _():
            o_ref[...] = (acc_ref[...] + part).astype(o_ref.dtype)

    return pl.pallas_call(
        body,
        name=name,
        out_shape=jax.ShapeDtypeStruct((m, n), out_dtype),
        grid=(m // tm, n // tn, nk),
        in_specs=[a_spec, b_spec] + [pl.BlockSpec(memory_space=pl.ANY)] * n_after,
        out_specs=pl.BlockSpec((tm, tn), lambda i, j, kk: (i, j)),
        scratch_shapes=[pltpu.VMEM((tm, tn), F32)] if nk > 1 else [],
        compiler_params=_params(("parallel", "parallel", "arbitrary")),
    )(a, b, *([after] * n_after))


ROW_WINDOW_BYTES = 12 * 1024 * 1024


def _rowwise(name, fn, ins, outs, rows, tm, after=None):
    per_row = sum(s[2] * s[1].dtype.itemsize for s in ins if s[0] == "row")
    per_row += sum(s[1] * jnp.dtype(s[2]).itemsize for s in outs if s[0] == "row")
    if per_row:
        tm = _tile(rows, min(tm, max(8, ROW_WINDOW_BYTES // (2 * per_row))), 8)
    n_in = len(ins)
    in_specs, args = [], []
    for spec in ins:
        if spec[0] == "row":
            _, arr, w, cb = spec
            in_specs.append(pl.BlockSpec((tm, w), functools.partial(lambda i, cb: (i, cb), cb=cb)))
        else:
            arr = spec[1]
            in_specs.append(pl.BlockSpec(arr.shape, lambda i: (0, 0)))
        args.append(arr)
    out_shape, out_specs = [], []
    for spec in outs:
        if spec[0] == "row":
            out_shape.append(jax.ShapeDtypeStruct((rows, spec[1]), spec[2]))
            out_specs.append(pl.BlockSpec((tm, spec[1]), lambda i: (i, 0)))
        else:
            out_shape.append(jax.ShapeDtypeStruct(spec[1], F32))
            out_specs.append(pl.BlockSpec(spec[1], lambda i: (0, 0)))
    has_acc = any(s[0] == "acc" for s in outs)
    n_after = 0 if after is None else 1
    in_specs += [pl.BlockSpec(memory_space=pl.ANY)] * n_after
    args += [after] * n_after

    def body(*refs):
        i = pl.program_id(0)
        res = fn(i, tm, *[r[...] for r in refs[:n_in]])
        for spec, ref, val in zip(outs, refs[n_in + n_after:], res):
            if spec[0] == "row":
                ref[...] = val.astype(ref.dtype)
            else:
                @pl.when(i == 0)
                def _(ref=ref, val=val):
                    ref[...] = val

                @pl.when(i > 0)
                def _(ref=ref, val=val):
                    ref[...] += val

    return pl.pallas_call(
        body,
        name=name,
        out_shape=out_shape,
        grid=(rows // tm,),
        in_specs=in_specs,
        out_specs=out_specs,
        compiler_params=_params(("arbitrary" if has_acc else "parallel",)),
    )(*args)


def _row_ids(i, tm, shape):
    return i * tm + lax.broadcasted_iota(jnp.int32, shape, 0)


def _rms_fwd(x, g):
    r = lax.rsqrt(jnp.mean(x * x, axis=-1, keepdims=True) + NORM_EPS)
    return x * r * g


def _rms_bwd(x, g, dy):
    r = lax.rsqrt(jnp.mean(x * x, axis=-1, keepdims=True) + NORM_EPS)
    xhat = x * r
    dxhat = dy * g
    dx = r * (dxhat - xhat * jnp.mean(dxhat * xhat, axis=-1, keepdims=True))
    return dx, jnp.sum(dy * xhat, axis=0, keepdims=True)


def _silu(x):
    return x * _sigmoid(x)


def _dsilu(x):
    s = _sigmoid(x)
    return s * (1.0 + x * (1.0 - s))


def _rot_src(x):
    lane = lax.broadcasted_iota(jnp.int32, x.shape, 1)
    return jnp.where(lane < ROPE // 2, pltpu.roll(x, LANE - ROPE // 2, 1), pltpu.roll(x, ROPE // 2, 1))


def _rope_fwd_call(q_raw, kv, proj, cos_t, sin_t, rows, tm):
    def fn(i, tm_, q, kvv, kr, c, s):
        kr_rot = kr[:, :LANE]
        kr_rot = kr_rot * c + _rot_src(kr_rot) * s
        qs, ks, vs = [], [], []
        for h in range(HEADS):
            qn = q[:, h * QHEAD_W:h * QHEAD_W + NOPE]
            qr = q[:, h * QHEAD_W + NOPE:(h + 1) * QHEAD_W]
            qs += [qn, qr * c + _rot_src(qr) * s]
            ks += [kvv[:, h * 2 * NOPE:h * 2 * NOPE + NOPE], kr_rot]
            vs += [kvv[:, h * 2 * NOPE + NOPE:(h + 1) * 2 * NOPE]]
        return jnp.concatenate(qs, axis=1), jnp.concatenate(ks, axis=1), jnp.concatenate(vs, axis=1)

    return _rowwise(
        "rope_fwd", fn,
        [("row", q_raw, HEADS * QHEAD_W, 0), ("row", kv, HEADS * 2 * NOPE, 0), ("row", proj, KR_W, SEG_KR // KR_W),
         ("row", cos_t, LANE, 0), ("row", sin_t, LANE, 0)],
        [("row", HEADS * QHEAD_W, BF16), ("row", HEADS * QHEAD_W, BF16), ("row", HEADS * VDIM, BF16)],
        rows, tm)


def _rope_bwd_call(dq_att, dk_att, dv, cos_t, sin_t, rows, tm):
    def fn(i, tm_, dq, dk, dvv, c, s):
        qs, kvs = [], []
        dkr = jnp.zeros((dq.shape[0], LANE), F32)
        for h in range(HEADS):
            dqr = dq[:, h * QHEAD_W + NOPE:(h + 1) * QHEAD_W]
            qs += [dq[:, h * QHEAD_W:h * QHEAD_W + NOPE], dqr * c - _rot_src(dqr) * s]
            kvs += [dk[:, h * QHEAD_W:h * QHEAD_W + NOPE], dvv[:, h * VDIM:(h + 1) * VDIM]]
            dkr = dkr + dk[:, h * QHEAD_W + NOPE:(h + 1) * QHEAD_W]
        dkr = dkr * c - _rot_src(dkr) * s
        return (jnp.concatenate(qs, axis=1), jnp.concatenate(kvs, axis=1),
                jnp.concatenate([dkr, jnp.zeros_like(dkr)], axis=1))

    return _rowwise(
        "rope_bwd", fn,
        [("row", dq_att, HEADS * QHEAD_W, 0), ("row", dk_att, HEADS * QHEAD_W, 0), ("row", dv, HEADS * VDIM, 0),
         ("row", cos_t, LANE, 0), ("row", sin_t, LANE, 0)],
        [("row", HEADS * QHEAD_W, BF16), ("row", HEADS * 2 * NOPE, BF16), ("row", KR_W, BF16)],
        rows, tm)


def _attn_mask(q_blk, k_blk, t, keys_on_rows=False):
    qa, ka = (1, 0) if keys_on_rows else (0, 1)
    qs = q_blk * t + lax.broadcasted_iota(jnp.int32, (t, t), qa)
    ks = k_blk * t + lax.broadcasted_iota(jnp.int32, (t, t), ka)
    return (ks <= qs) & ((ks >= PAD_LEN) | (ks == qs))


_NT = _DIMS["nt"]
_TN = _DIMS["tn"]
LOG2E = 1.4426950408889634
SCORE_TO_LOG2 = ATTN_SCALE * LOG2E


def _causal_pairs(nb, by_key):
    if by_key:
        pairs = [(qi, kj) for kj in range(nb) for qi in range(kj, nb)]
    else:
        pairs = [(qi, kj) for qi in range(nb) for kj in range(qi + 1)]
    return (jnp.asarray(np.array([p[0] for p in pairs], np.int32)), jnp.asarray(np.array([p[1] for p in pairs], np.int32)))


def _attn_fwd(q_att, k_att, v, rows):
    t = _tile(rows, 640, LANE)
    nb = rows // t

    def body(qt_ref, kt_ref, q_ref, k_ref, v_ref, o32_ref, obf_ref, lse_ref, m_sc, l_sc, acc_sc):
        qi, kj = qt_ref[pl.program_id(1)], kt_ref[pl.program_id(1)]

        @pl.when(kj == 0)
        def _():
            m_sc[...] = jnp.full_like(m_sc, NEG_BIG)
            l_sc[...] = jnp.zeros_like(l_sc)
            acc_sc[...] = jnp.zeros_like(acc_sc)

        def step(masked):
            s = lax.dot_general(q_ref[...], k_ref[...], _NT, preferred_element_type=F32) * SCORE_TO_LOG2
            if masked:
                s = jnp.where(_attn_mask(qi, kj, t), s, NEG_BIG)
            m_prev = m_sc[...]
            m_new = jnp.maximum(m_prev, jnp.max(s, axis=1, keepdims=True))
            alpha = jnp.exp2(m_prev - m_new)
            p = jnp.exp2(s - jnp.tile(m_new, (1, t // LANE)))
            l_sc[...] = alpha * l_sc[...] + jnp.sum(p, axis=1, keepdims=True)
            acc_sc[...] = alpha * acc_sc[...] + jnp.dot(p.astype(BF16), v_ref[...], preferred_element_type=F32)
            m_sc[...] = m_new

        pl.when((kj == qi) | (kj == 0))(functools.partial(step, True))
        pl.when((kj < qi) & (kj > 0))(functools.partial(step, False))

        @pl.when(kj == qi)
        def _():
            o = acc_sc[...] / l_sc[...]
            o32_ref[...] = o
            obf_ref[...] = o.astype(BF16)
            lse_ref[0] = m_sc[:, 0:1] + jnp.log2(l_sc[:, 0:1])

    qt, kt = _causal_pairs(nb, by_key=False)
    qmap = lambda h, p, qt_ref, kt_ref: (qt_ref[p], h)
    kmap = lambda h, p, qt_ref, kt_ref: (kt_ref[p], h)
    return pl.pallas_call(
        body,
        name="attn_fwd",
        out_shape=[jax.ShapeDtypeStruct((rows, HEADS * VDIM), F32), jax.ShapeDtypeStruct((rows, HEADS * VDIM), BF16),
                   jax.ShapeDtypeStruct((HEADS, rows, 1), F32)],
        grid_spec=pltpu.PrefetchScalarGridSpec(
            num_scalar_prefetch=2,
            grid=(HEADS, len(qt)),
            in_specs=[pl.BlockSpec((t, QHEAD_W), qmap), pl.BlockSpec((t, QHEAD_W), kmap), pl.BlockSpec((t, VDIM), kmap)],
            out_specs=[pl.BlockSpec((t, VDIM), qmap), pl.BlockSpec((t, VDIM), qmap),
                       pl.BlockSpec((1, t, 1), lambda h, p, qt_ref, kt_ref: (h, qt_ref[p], 0))],
            scratch_shapes=[pltpu.VMEM((t, LANE), F32), pltpu.VMEM((t, LANE), F32), pltpu.VMEM((t, VDIM), F32)]),
        compiler_params=_params(("parallel", "arbitrary")),
    )(qt, kt, q_att, k_att, v)


def _attn_bwd_dq(q_att, k_att, v, do, o32, lse, rows):
    t = _tile(rows, 640, LANE)
    nb = rows // t

    def body(qt_ref, kt_ref, q_ref, k_ref, v_ref, do_ref, o_ref, lse_ref, dq_ref, delta_ref, acc_sc, dl_sc, lse_sc):
        qi, kj = qt_ref[pl.program_id(1)], kt_ref[pl.program_id(1)]

        @pl.when(kj == 0)
        def _():
            acc_sc[...] = jnp.zeros_like(acc_sc)
            delta = jnp.sum(do_ref[...].astype(F32) * o_ref[...], axis=1, keepdims=True)
            delta_ref[0] = delta
            dl_sc[...] = jnp.broadcast_to(delta, (t, LANE))
            lse_sc[...] = jnp.broadcast_to(lse_ref[0], (t, LANE))

        def step(masked):
            s = lax.dot_general(q_ref[...], k_ref[...], _NT, preferred_element_type=F32) * SCORE_TO_LOG2
            p = jnp.exp2(s - jnp.tile(lse_sc[...], (1, t // LANE)))
            if masked:
                p = jnp.where(_attn_mask(qi, kj, t), p, 0.0)
            dp = lax.dot_general(do_ref[...], v_ref[...], _NT, preferred_element_type=F32)
            ds = p * (dp - jnp.tile(dl_sc[...], (1, t // LANE)))
            acc_sc[...] += jnp.dot(ds.astype(BF16), k_ref[...], preferred_element_type=F32)

        pl.when((kj == qi) | (kj == 0))(functools.partial(step, True))
        pl.when((kj < qi) & (kj > 0))(functools.partial(step, False))

        @pl.when(kj == qi)
        def _():
            dq_ref[...] = acc_sc[...] * ATTN_SCALE

    qt, kt = _causal_pairs(nb, by_key=False)
    qmap = lambda h, p, qt_ref, kt_ref: (qt_ref[p], h)
    kmap = lambda h, p, qt_ref, kt_ref: (kt_ref[p], h)
    stat = pl.BlockSpec((1, t, 1), lambda h, p, qt_ref, kt_ref: (h, qt_ref[p], 0))
    return pl.pallas_call(
        body,
        name="attn_bwd_dq",
        out_shape=[jax.ShapeDtypeStruct((rows, HEADS * QHEAD_W), F32), jax.ShapeDtypeStruct((HEADS, rows, 1), F32)],
        grid_spec=pltpu.PrefetchScalarGridSpec(
            num_scalar_prefetch=2,
            grid=(HEADS, len(qt)),
            in_specs=[pl.BlockSpec((t, QHEAD_W), qmap), pl.BlockSpec((t, QHEAD_W), kmap), pl.BlockSpec((t, VDIM), kmap),
                      pl.BlockSpec((t, VDIM), qmap), pl.BlockSpec((t, VDIM), qmap), stat],
            out_specs=[pl.BlockSpec((t, QHEAD_W), qmap), stat],
            scratch_shapes=[pltpu.VMEM((t, QHEAD_W), F32), pltpu.VMEM((t, LANE), F32), pltpu.VMEM((t, LANE), F32)]),
        compiler_params=_params(("parallel", "arbitrary")),
    )(qt, kt, q_att, k_att, v, do, o32, lse)


def _attn_bwd_dkv(q_att, k_att, v, do, lse_row, delta_row, rows):
    t = _tile(rows, 640, LANE)
    nb = rows // t

    def body(qt_ref, kt_ref, q_ref, k_ref, v_ref, do_ref, lse_ref, delta_ref, dk_ref, dv_ref, dk_sc, dv_sc):
        qi, kj = qt_ref[pl.program_id(1)], kt_ref[pl.program_id(1)]

        @pl.when(qi == kj)
        def _():
            dk_sc[...] = jnp.zeros_like(dk_sc)
            dv_sc[...] = jnp.zeros_like(dv_sc)

        def step(masked):
            st = lax.dot_general(k_ref[...], q_ref[...], _NT, preferred_element_type=F32) * SCORE_TO_LOG2
            pt = jnp.exp2(st - lse_ref[0])
            if masked:
                pt = jnp.where(_attn_mask(qi, kj, t, keys_on_rows=True), pt, 0.0)
            dv_sc[...] += jnp.dot(pt.astype(BF16), do_ref[...], preferred_element_type=F32)
            dpt = lax.dot_general(v_ref[...], do_ref[...], _NT, preferred_element_type=F32)
            dst = pt * (dpt - delta_ref[0])
            dk_sc[...] += jnp.dot(dst.astype(BF16), q_ref[...], preferred_element_type=F32)

        pl.when((qi == kj) | (kj == 0))(functools.partial(step, True))
        pl.when((qi > kj) & (kj > 0))(functools.partial(step, False))

        @pl.when(qi == nb - 1)
        def _():
            dk_ref[...] = dk_sc[...] * ATTN_SCALE
            dv_ref[...] = dv_sc[...]

    qt, kt = _causal_pairs(nb, by_key=True)
    qmap = lambda h, p, qt_ref, kt_ref: (qt_ref[p], h)
    kmap = lambda h, p, qt_ref, kt_ref: (kt_ref[p], h)
    stat = pl.BlockSpec((1, 1, t), lambda h, p, qt_ref, kt_ref: (h, 0, qt_ref[p]))
    return pl.pallas_call(
        body,
        name="attn_bwd_dkv",
        out_shape=[jax.ShapeDtypeStruct((rows, HEADS * QHEAD_W), F32), jax.ShapeDtypeStruct((rows, HEADS * VDIM), F32)],
        grid_spec=pltpu.PrefetchScalarGridSpec(
            num_scalar_prefetch=2,
            grid=(HEADS, len(qt)),
            in_specs=[pl.BlockSpec((t, QHEAD_W), qmap), pl.BlockSpec((t, QHEAD_W), kmap), pl.BlockSpec((t, VDIM), kmap),
                      pl.BlockSpec((t, VDIM), qmap), stat, stat],
            out_specs=[pl.BlockSpec((t, QHEAD_W), kmap), pl.BlockSpec((t, VDIM), kmap)],
            scratch_shapes=[pltpu.VMEM((t, QHEAD_W), F32), pltpu.VMEM((t, VDIM), F32)]),
        compiler_params=_params(("parallel", "arbitrary")),
    )(qt, kt, q_att, k_att, v, do, lse_row, delta_row)


C = BLOCK


def _hgrn_prep(hq, hf, hi, lb, c):
    rows = c * C + lax.broadcasted_iota(jnp.int32, (C, C), 0)
    valid = rows >= PAD_LEN
    sg = _sigmoid(hf)
    f = lb + (1.0 - lb) * sg
    g = jnp.where(valid, jnp.log(f), 0.0)
    k = jnp.where(valid, 1.0 - f, 0.0)
    q = _silu(hq)
    r = lax.broadcasted_iota(jnp.int32, (C, C), 0)
    cc = lax.broadcasted_iota(jnp.int32, (C, C), 1)
    tri = jnp.where(cc <= r, 1.0, 0.0).astype(F32)
    b = jnp.dot(tri, g, precision=lax.Precision.HIGHEST, preferred_element_type=F32)
    return q, k, hi, b, f, sg, valid


def _last_row_as_col(b_t):
    lane = lax.broadcasted_iota(jnp.int32, b_t.shape, 1)
    return jnp.sum(jnp.where(lane == C - 1, b_t, 0.0), axis=1, keepdims=True)


def _k_scaled(k, b, bs):
    return (k * jnp.exp(jnp.minimum(bs - b, 0.0))).astype(BF16)


def _hgrn_fwd(proj, lb, rows):
    nc = rows // C

    def body(hq_ref, hf_ref, hi_ref, lb_ref, o_ref, a_ref, s_ref, s_sc, b_sc):
        c = pl.program_id(1)

        @pl.when(c == 0)
        def _():
            s_sc[...] = jnp.zeros_like(s_sc)

        q, k, v, b, _, _, _ = _hgrn_prep(hq_ref[...], hf_ref[...], hi_ref[...], lb_ref[...], c)
        b_sc[...] = b
        s0 = s_sc[...]
        s_ref[0, 0] = s0
        v_bf = v.astype(BF16)
        r16 = lax.broadcasted_iota(jnp.int32, (SUB, C), 0)
        c16 = lax.broadcasted_iota(jnp.int32, (SUB, C), 1)
        slabs = [jnp.zeros((SUB, C), F32)]
        for i in range(1, C // SUB):
            bs = b_sc[SUB * i - 1:SUB * i, :]
            qs = (q[SUB * i:SUB * (i + 1)] * jnp.exp(b[SUB * i:SUB * (i + 1)] - bs)).astype(BF16)
            a_i = lax.dot_general(qs, _k_scaled(k, b, bs), _NT, preferred_element_type=F32)
            slabs.append(jnp.where(c16 <= r16 + (SUB * i - SUB), a_i, 0.0))
        a_off = jnp.concatenate(slabs, axis=0)
        q_t, k_t, b_t = q.T, k.T, b.T
        sub = lax.broadcasted_iota(jnp.int32, (C, C), 0)
        lane = lax.broadcasted_iota(jnp.int32, (C, C), 1)
        lane1 = lax.broadcasted_iota(jnp.int32, (1, C), 1)
        at_band = jnp.zeros((C, C), F32)
        for dl in range(SUB):
            k_s = pltpu.roll(k_t, dl, 1) if dl else k_t
            b_s = pltpu.roll(b_t, dl, 1) if dl else b_t
            e = jnp.exp(jnp.minimum(b_t - b_s, 0.0))
            band = jnp.sum(q_t * k_s * e, axis=0, keepdims=True)
            band = jnp.where(lane1 >= dl, band, 0.0)
            at_band = at_band + jnp.where(sub == lane - dl, jnp.broadcast_to(band, (C, C)), 0.0)
        a = (a_off + at_band.T).astype(BF16)
        a_ref[0] = a
        qe = (q * jnp.exp(b)).astype(BF16)
        o_ref[...] = (jnp.dot(a, v_bf, preferred_element_type=F32)
                      + jnp.dot(qe, s0.astype(BF16), preferred_element_type=F32))
        b_last = b_sc[C - 1:C, :]
        kd = (k * jnp.exp(b_last - b)).astype(BF16)
        s_sc[...] = (jnp.exp(_last_row_as_col(b_t)) * s0
                     + lax.dot_general(kd, v_bf, _TN, preferred_element_type=F32))

    seg = lambda base: (lambda h, c: (c, base // C + h))
    return pl.pallas_call(
        body,
        name="hgrn_fwd",
        out_shape=[jax.ShapeDtypeStruct((rows, D_MODEL), F32), jax.ShapeDtypeStruct((HEADS, rows, C), BF16),
                   jax.ShapeDtypeStruct((HEADS, nc, C, C), F32)],
        grid=(HEADS, nc),
        in_specs=[pl.BlockSpec((C, C), seg(SEG_HQ)), pl.BlockSpec((C, C), seg(SEG_HF)), pl.BlockSpec((C, C), seg(SEG_HI)),
                  pl.BlockSpec((1, C), lambda h, c: (0, h))],
        out_specs=[pl.BlockSpec((C, C), lambda h, c: (c, h)), pl.BlockSpec((1, C, C), lambda h, c: (h, c, 0)),
                   pl.BlockSpec((1, 1, C, C), lambda h, c: (h, c, 0, 0))],
        scratch_shapes=[pltpu.VMEM((C, C), F32), pltpu.VMEM((C, C), F32)],
        compiler_params=_params(("parallel", "arbitrary")),
    )(proj, proj, proj, lb)


def _hgrn_bwd(proj, lb, a_mat, s_states, do_h, rows):
    nc = rows // C

    def body(hq_ref, hf_ref, hi_ref, lb_ref, a_ref, s_ref, do_ref, dhq_ref, dhf_ref, dhi_ref, dlb_ref, ds_sc, b_sc):
        step = pl.program_id(1)
        c = nc - 1 - step

        @pl.when(step == 0)
        def _():
            ds_sc[...] = jnp.zeros_like(ds_sc)
            dlb_ref[...] = jnp.zeros_like(dlb_ref)

        hq, hf = hq_ref[...], hf_ref[...]
        lb_row = lb_ref[...]
        q, k, v, b, f, sg, valid = _hgrn_prep(hq, hf, hi_ref[...], lb_row, c)
        b_sc[...] = b
        s0 = s_ref[0, 0]
        ds1 = ds_sc[...]
        s0_bf, ds1_bf = s0.astype(BF16), ds1.astype(BF16)
        do = do_ref[...]
        do_bf, v_bf = do.astype(BF16), v.astype(BF16)
        b_last = b_sc[C - 1:C, :]
        e_last = jnp.exp(b_last - b)
        eb = jnp.exp(b)
        sub = lax.broadcasted_iota(jnp.int32, (C, C), 0)
        lane = lax.broadcasted_iota(jnp.int32, (C, C), 1)
        r16 = lax.broadcasted_iota(jnp.int32, (SUB, C), 0)
        c16 = lax.broadcasted_iota(jnp.int32, (SUB, C), 1)

        dv = (lax.dot_general(a_ref[0], do_bf, _TN, preferred_element_type=F32)
              + jnp.dot((k * e_last).astype(BF16), ds1_bf, preferred_element_type=F32))
        da = jnp.where(lane <= sub, lax.dot_general(do_bf, v_bf, _NT, preferred_element_type=F32), 0.0)
        da_t = jnp.where(sub <= lane, lax.dot_general(v_bf, do_bf, _NT, preferred_element_type=F32), 0.0)

        dq_slabs = [jnp.zeros((SUB, C), F32)]
        for i in range(1, C // SUB):
            bs = b_sc[SUB * i - 1:SUB * i, :]
            da_i = jnp.where(c16 <= r16 + (SUB * i - SUB), da[SUB * i:SUB * (i + 1)], 0.0).astype(BF16)
            dq_slabs.append(jnp.exp(b[SUB * i:SUB * (i + 1)] - bs)
                            * jnp.dot(da_i, _k_scaled(k, b, bs), preferred_element_type=F32))
        dk_slabs = []
        for j in range(C // SUB - 1):
            be = b_sc[SUB * j + SUB - 1:SUB * (j + 1), :]
            qe_j = (q * jnp.exp(jnp.minimum(b - be, 0.0))).astype(BF16)
            da_j = jnp.where(c16 >= r16 + (SUB * j + SUB), da_t[SUB * j:SUB * (j + 1)], 0.0).astype(BF16)
            dk_slabs.append(jnp.exp(be - b[SUB * j:SUB * (j + 1)]) * jnp.dot(da_j, qe_j, preferred_element_type=F32))
        dk_slabs.append(jnp.zeros((SUB, C), F32))

        q_t, k_t, b_t = q.T, k.T, b.T
        lane1 = lax.broadcasted_iota(jnp.int32, (1, C), 1)
        dq_t = jnp.zeros((C, C), F32)
        dk_t = jnp.zeros((C, C), F32)
        for dl in range(SUB):
            k_s = pltpu.roll(k_t, dl, 1) if dl else k_t
            b_s = pltpu.roll(b_t, dl, 1) if dl else b_t
            e = jnp.exp(jnp.minimum(b_t - b_s, 0.0))
            dband = jnp.sum(jnp.where(sub == lane - dl, da_t, 0.0), axis=0, keepdims=True)
            w = jnp.where(lane1 >= dl, dband, 0.0) * e
            dq_t = dq_t + w * k_s
            back = w * q_t
            dk_t = dk_t + (pltpu.roll(back, C - dl, 1) if dl else back)

        dq = eb * lax.dot_general(do_bf, s0_bf, _NT, preferred_element_type=F32) + jnp.concatenate(dq_slabs, axis=0) + dq_t.T
        dk_inter = e_last * lax.dot_general(v_bf, ds1_bf, _NT, preferred_element_type=F32)
        dk = dk_inter + jnp.concatenate(dk_slabs, axis=0) + dk_t.T

        extra = (jnp.exp(b_last) * jnp.sum((s0 * ds1).T, axis=0, keepdims=True)
                 + jnp.sum(k * dk_inter, axis=0, keepdims=True))
        db = q * dq - k * dk + jnp.where(sub == C - 1, jnp.broadcast_to(extra, (C, C)), 0.0)
        tri_t = jnp.where(lane >= sub, 1.0, 0.0).astype(F32)
        dg = jnp.dot(tri_t, db, precision=lax.Precision.HIGHEST, preferred_element_type=F32)
        ds_sc[...] = (jnp.exp(_last_row_as_col(b_t)) * ds1
                      + lax.dot_general((q * eb).astype(BF16), do_bf, _TN, preferred_element_type=F32))

        df = jnp.where(valid, dg / f - dk, 0.0)
        dhf_ref[...] = (df * (1.0 - lb_row) * sg * (1.0 - sg)).astype(BF16)
        dlb_ref[...] += jnp.sum(df * (1.0 - sg), axis=0, keepdims=True)
        dhq_ref[...] = (dq * _dsilu(hq)).astype(BF16)
        dhi_ref[...] = dv.astype(BF16)

    seg = lambda base: (lambda h, s: (nc - 1 - s, base // C + h))
    rmap = lambda h, s: (nc - 1 - s, h)
    return pl.pallas_call(
        body,
        name="hgrn_bwd",
        out_shape=[jax.ShapeDtypeStruct((rows, D_MODEL), BF16)] * 3 + [jax.ShapeDtypeStruct((1, D_MODEL), F32)],
        grid=(HEADS, nc),
        in_specs=[pl.BlockSpec((C, C), seg(SEG_HQ)), pl.BlockSpec((C, C), seg(SEG_HF)), pl.BlockSpec((C, C), seg(SEG_HI)),
                  pl.BlockSpec((1, C), lambda h, s: (0, h)),
                  pl.BlockSpec((1, C, C), lambda h, s: (h, nc - 1 - s, 0)),
                  pl.BlockSpec((1, 1, C, C), lambda h, s: (h, nc - 1 - s, 0, 0)),
                  pl.BlockSpec((C, C), rmap)],
        out_specs=[pl.BlockSpec((C, C), rmap)] * 3 + [pl.BlockSpec((1, C), lambda h, s: (0, h))],
        scratch_shapes=[pltpu.VMEM((C, C), F32), pltpu.VMEM((C, C), F32)],
        compiler_params=_params(("parallel", "arbitrary")),
    )(proj, proj, proj, lb, a_mat, s_states, do_h)


CONV_TC = 512
HALO = 8


def _conv_taps(i, tm, g_ref, pg_ref):
    shape = g_ref.shape
    r = lax.broadcasted_iota(jnp.int32, shape, 0)
    g = jnp.where(i * tm + r >= PAD_LEN, g_ref[...], 0.0)
    p1 = jnp.where(i * tm - 1 >= PAD_LEN, pg_ref[HALO - 1:HALO, :], 0.0)
    p2 = jnp.where(i * tm - 2 >= PAD_LEN, pg_ref[HALO - 2:HALO - 1, :], 0.0)
    s1 = jnp.where(r == 0, p1, pltpu.roll(g, 1, 0))
    s2 = jnp.where(r == 0, p2, jnp.where(r == 1, p1, pltpu.roll(g, 2, 0)))
    return g, s1, s2


def _conv_specs(tm, tc, ncb, order):
    gate = pl.BlockSpec((tm, tc), lambda *ids: order(ids))
    halo = pl.BlockSpec((HALO, tc), lambda *ids: (jnp.maximum(order(ids)[0] * (tm // HALO) - 1, 0), order(ids)[1]))
    up = pl.BlockSpec((tm, tc), lambda *ids: (order(ids)[0], ncb + order(ids)[1]))
    return gate, halo, up


def _conv_fwd(ffn, conv_w, conv_b, rows, tm):
    tc = CONV_TC
    ncb = D_FF // tc

    def body(g_ref, pg_ref, up_ref, cw_ref, cb_ref, act_ref):
        i = pl.program_id(0)
        g, s1, s2 = _conv_taps(i, tm, g_ref, pg_ref)
        conv = (cw_ref[0:1, :] * s2 + cw_ref[1:2, :] * s1 + cw_ref[2:3, :] * g) + cb_ref[...]
        act_ref[...] = (_silu(conv) * up_ref[...]).astype(BF16)

    gate, halo, up = _conv_specs(tm, tc, ncb, lambda ids: (ids[0], ids[1]))
    return pl.pallas_call(
        body,
        name="conv_fwd",
        out_shape=jax.ShapeDtypeStruct((rows, D_FF), BF16),
        grid=(rows // tm, ncb),
        in_specs=[gate, halo, up, pl.BlockSpec((3, tc), lambda i, j: (0, j)), pl.BlockSpec((1, tc), lambda i, j: (0, j))],
        out_specs=pl.BlockSpec((tm, tc), lambda i, j: (i, j)),
        compiler_params=_params(("parallel", "parallel")),
    )(ffn, ffn, ffn, conv_w, conv_b)


def _conv_bwd_a(ffn, dact, conv_w, conv_b, rows, tm):
    tc = CONV_TC
    ncb = D_FF // tc

    def body(g_ref, pg_ref, up_ref, da_ref, cw_ref, cb_ref, dc_ref, dffn_ref, w0_ref, w1_ref, w2_ref, db_ref):
        i = pl.program_id(1)
        g, s1, s2 = _conv_taps(i, tm, g_ref, pg_ref)
        conv = (cw_ref[0:1, :] * s2 + cw_ref[1:2, :] * s1 + cw_ref[2:3, :] * g) + cb_ref[...]
        da = da_ref[...]
        dffn_ref[...] = (da * _silu(conv)).astype(BF16)
        dc = da * up_ref[...] * _dsilu(conv)
        dc_ref[...] = dc
        sums = [jnp.sum(dc * s2, axis=0, keepdims=True), jnp.sum(dc * s1, axis=0, keepdims=True),
                jnp.sum(dc * g, axis=0, keepdims=True), jnp.sum(dc, axis=0, keepdims=True)]
        for ref, val in zip((w0_ref, w1_ref, w2_ref, db_ref), sums):
            @pl.when(i == 0)
            def _(ref=ref, val=val):
                ref[...] = val

            @pl.when(i > 0)
            def _(ref=ref, val=val):
                ref[...] += val

    gate, halo, up = _conv_specs(tm, tc, ncb, lambda ids: (ids[1], ids[0]))
    col = pl.BlockSpec((1, tc), lambda j, i: (0, j))
    return pl.pallas_call(
        body,
        name="conv_bwd_a",
        out_shape=[jax.ShapeDtypeStruct((rows, D_FF), F32), jax.ShapeDtypeStruct((rows, 2 * D_FF), BF16)]
        + [jax.ShapeDtypeStruct((1, D_FF), F32)] * 4,
        grid=(ncb, rows // tm),
        in_specs=[gate, halo, up, pl.BlockSpec((tm, tc), lambda j, i: (i, j)),
                  pl.BlockSpec((3, tc), lambda j, i: (0, j)), col],
        out_specs=[pl.BlockSpec((tm, tc), lambda j, i: (i, j)), pl.BlockSpec((tm, tc), lambda j, i: (i, ncb + j)),
                   col, col, col, col],
        compiler_params=_params(("parallel", "arbitrary")),
    )(ffn, ffn, ffn, dact, conv_w, conv_b)


def _conv_bwd_b(dconv, conv_w, dffn, rows, tm):
    tc = CONV_TC
    ncb = D_FF // tc
    nrb = rows // tm

    def body(dc_ref, nx_ref, cw_ref, dffn_in, out_ref):
        del dffn_in
        i = pl.program_id(0)
        dc = dc_ref[...]
        r = lax.broadcasted_iota(jnp.int32, dc.shape, 0)
        last = i == nrb - 1
        x1 = jnp.where(last, 0.0, nx_ref[0:1, :])
        x2 = jnp.where(last, 0.0, nx_ref[1:2, :])
        n1 = jnp.where(r == tm - 1, x1, pltpu.roll(dc, tm - 1, 0))
        n2 = jnp.where(r == tm - 1, x2, jnp.where(r == tm - 2, x1, pltpu.roll(dc, tm - 2, 0)))
        dg = cw_ref[2:3, :] * dc + cw_ref[1:2, :] * n1 + cw_ref[0:1, :] * n2
        out_ref[...] = jnp.where(i * tm + r >= PAD_LEN, dg, 0.0).astype(BF16)

    return pl.pallas_call(
        body,
        name="conv_bwd_b",
        out_shape=jax.ShapeDtypeStruct((rows, 2 * D_FF), BF16),
        grid=(nrb, ncb),
        in_specs=[pl.BlockSpec((tm, tc), lambda i, j: (i, j)),
                  pl.BlockSpec((HALO, tc), lambda i, j: (jnp.minimum((i + 1) * (tm // HALO), rows // HALO - 1), j)),
                  pl.BlockSpec((3, tc), lambda i, j: (0, j)),
                  pl.BlockSpec(memory_space=pl.ANY)],
        out_specs=pl.BlockSpec((tm, tc), lambda i, j: (i, j)),
        input_output_aliases={3: 0},
        compiler_params=_params(("parallel", "parallel")),
    )(dconv, dconv, conv_w, dffn)


def _final_call(h1, y, target, g_final, rows):
    tm = BLOCK

    def fn(i, tm_, h1v, yv, tgt, g):
        h2 = h1v + yv
        out = _rms_fwd(h2, g)
        err = jnp.where(i > 0, out - tgt, 0.0)
        loss = 0.5 * jnp.sum(jnp.mean(err * err, axis=-1, keepdims=True), axis=0, keepdims=True)
        dx, dg = _rms_bwd(h2, g, err * (1.0 / D_MODEL))
        return dx, dx, jnp.broadcast_to(loss, (1, LANE)), dg

    n_in = 4
    in_specs = [pl.BlockSpec((tm, D_MODEL), lambda i: (i, 0)), pl.BlockSpec((tm, D_MODEL), lambda i: (i, 0)),
                pl.BlockSpec((tm, D_MODEL), lambda i: (jnp.maximum(i - 1, 0), 0)),
                pl.BlockSpec((1, D_MODEL), lambda i: (0, 0))]

    def body(*refs):
        i = pl.program_id(0)
        dx, dx2, loss, dg = fn(i, tm, *[r[...] for r in refs[:n_in]])
        refs[4][...] = dx
        refs[5][...] = dx2.astype(BF16)
        for ref, val in ((refs[6], loss), (refs[7], dg)):
            @pl.when(i == 0)
            def _(ref=ref, val=val):
                ref[...] = val

            @pl.when(i > 0)
            def _(ref=ref, val=val):
                ref[...] += val

    return pl.pallas_call(
        body,
        name="final_loss",
        out_shape=[jax.ShapeDtypeStruct((rows, D_MODEL), F32), jax.ShapeDtypeStruct((rows, D_MODEL), BF16),
                   jax.ShapeDtypeStruct((1, LANE), F32), jax.ShapeDtypeStruct((1, D_MODEL), F32)],
        grid=(rows // tm,),
        in_specs=in_specs,
        out_specs=[pl.BlockSpec((tm, D_MODEL), lambda i: (i, 0)), pl.BlockSpec((tm, D_MODEL), lambda i: (i, 0)),
                   pl.BlockSpec((1, LANE), lambda i: (0, 0)), pl.BlockSpec((1, D_MODEL), lambda i: (0, 0))],
        compiler_params=_params(("arbitrary",)),
    )(h1, y, target, g_final)


def _heads_map(fn, *slabs):
    outs = [fn(*[s[:, h * LANE:(h + 1) * LANE] for s in slabs]) for h in range(HEADS)]
    if isinstance(outs[0], tuple):
        return tuple(jnp.concatenate([o[k] for o in outs], axis=1) for k in range(len(outs[0])))
    return jnp.concatenate(outs, axis=1)


def _local_step(x, positions, target, w, p, emit=None):
    kept = {}
    if emit is None:
        def emit(group):
            kept.update(group)
            return None
    s_len = x.shape[0]
    rows = s_len + BLOCK
    tm = _tile(rows, 640, 8)
    row = lambda arr, width, cb=0: ("row", arr, width, cb)

    h0 = jnp.concatenate([jnp.zeros((PAD_LEN, D_MODEL), F32), w["meta_tokens"], x], axis=0)
    pos = jnp.concatenate([jnp.zeros((PAD_LEN,), jnp.int32), jnp.arange(N_META, dtype=jnp.int32),
                           positions.astype(jnp.int32) + N_META])
    inv = 1.0 / (ROPE_THETA ** (jnp.arange(0, ROPE, 2, dtype=F32) / ROPE))
    ang = pos.astype(F32)[:, None] * inv
    zero = jnp.zeros((rows, LANE - ROPE), F32)
    cos_t = jnp.concatenate([jnp.cos(ang), jnp.cos(ang), zero], axis=1)
    sin_t = jnp.concatenate([-jnp.sin(ang), jnp.sin(ang), zero], axis=1)
    lb_r0, lb_r1 = p["lb_raw"][0:1], p["lb_raw"][1:2]

    def lb_fn(i, tm_, r0, r1):
        m = jnp.maximum(r0, r1)
        e0, e1 = jnp.exp(r0 - m), jnp.exp(r1 - m)
        return (e0 / (e0 + e1),)

    (lb,) = _rowwise("lb_fwd", lb_fn, [("bc", lb_r0), ("bc", lb_r1)], [("acc", (1, D_MODEL))], 1, 1)

    (u1,) = _rowwise("mix_norm", lambda i, t, h, g: (_rms_fwd(h, g),),
                     [row(h0, D_MODEL), ("bc", p["g_mix_norm"])], [("row", D_MODEL, BF16)], rows, tm)
    proj = _matmul(u1, w["w_in"], "nn", F32, "mm_proj")
    qn, kvn = _rowwise(
        "latent_norm", lambda i, t, ql, kl, gq, gk: (_rms_fwd(ql, gq), _rms_fwd(kl, gk)),
        [row(proj, Q_LORA, 0), row(proj, KV_LORA, SEG_KV_LAT // KV_LORA), ("bc", p["g_q_norm"]), ("bc", p["g_kv_norm"])],
        [("row", Q_LORA, BF16), ("row", KV_LORA, BF16)], rows, tm)
    q_raw = _matmul(qn, w["w_q_up"], "nn", F32, "mm_q_up")
    kv = _matmul(kvn, w["w_kv_up"], "nn", F32, "mm_kv_up")
    q_att, k_att, v_att = _rope_fwd_call(q_raw, kv, proj, cos_t, sin_t, rows, tm)
    o32, o_bf, lse = _attn_fwd(q_att, k_att, v_att, rows)
    o_h, a_mat, s_states = _hgrn_fwd(proj, lb, rows)

    def hgrn_post(i, t, oh, hg, g):
        return (_heads_map(lambda a, b: _rms_fwd(a, g) * _silu(b), oh, hg),)

    (o_hgrn,) = _rowwise("hgrn_post", hgrn_post,
                         [row(o_h, D_MODEL), row(proj, D_MODEL, SEG_HG // D_MODEL), ("bc", p["g_hgrn_norm"])],
                         [("row", D_MODEL, BF16)], rows, tm)
    br_a = _matmul(o_bf, w["w_branch_mla"], "nn", F32, "mm_branch_mla")
    br_b = _matmul(o_hgrn, w["w_branch_hgrn"], "nn", F32, "mm_branch_hgrn")
    (merged,) = _rowwise(
        "merge", lambda i, t, a, b, ga, gb: (_sigmoid(ga) * a + _sigmoid(gb) * b,),
        [row(br_a, D_MODEL), row(br_b, D_MODEL), row(proj, D_MODEL, SEG_GA // D_MODEL), row(proj, D_MODEL, SEG_GB // D_MODEL)],
        [("row", D_MODEL, BF16)], rows, tm)
    mix_out = _matmul(merged, w["w_out"], "nn", F32, "mm_out")

    def ffn_norm(i, t, h, mo, g):
        h1v = h + mo
        return h1v, _rms_fwd(h1v, g)

    h1, u2 = _rowwise("ffn_norm", ffn_norm, [row(h0, D_MODEL), row(mix_out, D_MODEL), ("bc", p["g_ffn_norm"])],
                      [("row", D_MODEL, F32), ("row", D_MODEL, BF16)], rows, tm)
    ffn = _matmul(u2, w["w_ffn_in"], "nn", F32, "mm_ffn_in")
    act = _conv_fwd(ffn, w["conv_w"], p["conv_b"], rows, tm)
    y = _matmul(act, w["w_ffn_out"], "nn", F32, "mm_ffn_out")
    dh2, dh2_bf, loss_acc, dg_final = _final_call(h1, y, target, p["g_final_norm"].reshape(1, D_MODEL), rows)

    grads = {"g_final_norm": dg_final.reshape(D_MODEL)}
    dact = _matmul(dh2_bf, w["w_ffn_out"], "nt", F32, "mm_d_act")
    grads["w_ffn_out"] = _matmul(act, dh2_bf, "tn", BF16, "mm_dw_ffn_out")
    dconv, dffn, dcw0, dcw1, dcw2, dcb = _conv_bwd_a(ffn, dact, w["conv_w"], p["conv_b"], rows, tm)
    dffn = _conv_bwd_b(dconv, w["conv_w"], dffn, rows, tm)
    grads["conv_w"] = jnp.concatenate([dcw0, dcw1, dcw2], axis=0)
    grads["conv_b"] = dcb
    du2 = _matmul(dffn, w["w_ffn_in"], "nt", F32, "mm_d_u2")
    grads["w_ffn_in"] = _matmul(u2, dffn, "tn", BF16, "mm_dw_ffn_in")

    def ffn_norm_bwd(i, t, h, du, dh, g):
        dx, dg = _rms_bwd(h, g, du)
        dh1v = dh + dx
        return dh1v, dh1v, dg

    dh1, dh1_bf, grads["g_ffn_norm"] = _rowwise(
        "ffn_norm_bwd", ffn_norm_bwd, [row(h1, D_MODEL), row(du2, D_MODEL), row(dh2, D_MODEL), ("bc", p["g_ffn_norm"])],
        [("row", D_MODEL, F32), ("row", D_MODEL, BF16), ("acc", (1, D_MODEL))], rows, tm)
    tok = emit({n: grads.pop(n) for n in ("w_ffn_out", "w_ffn_in", "conv_w", "conv_b", "g_final_norm", "g_ffn_norm")})
    dmerged = _matmul(dh1_bf, w["w_out"], "nt", F32, "mm_d_merged", after=tok)
    grads["w_out"] = _matmul(merged, dh1_bf, "tn", BF16, "mm_dw_out")

    def merge_bwd(i, t, dm, a, b, ga, gb):
        sa, sb = _sigmoid(ga), _sigmoid(gb)
        return dm * sa, dm * sb, dm * a * sa * (1.0 - sa), dm * b * sb * (1.0 - sb)

    da_bf, db_bf, dga, dgb = _rowwise(
        "merge_bwd", merge_bwd,
        [row(dmerged, D_MODEL), row(br_a, D_MODEL), row(br_b, D_MODEL),
         row(proj, D_MODEL, SEG_GA // D_MODEL), row(proj, D_MODEL, SEG_GB // D_MODEL)],
        [("row", D_MODEL, BF16)] * 4, rows, tm)
    do_mla = _matmul(da_bf, w["w_branch_mla"], "nt", BF16, "mm_d_o_mla")
    grads["w_branch_mla"] = _matmul(o_bf, da_bf, "tn", BF16, "mm_dw_branch_mla")
    do_hgrn = _matmul(db_bf, w["w_branch_hgrn"], "nt", F32, "mm_d_o_hgrn")
    grads["w_branch_hgrn"] = _matmul(o_hgrn, db_bf, "tn", BF16, "mm_dw_branch_hgrn")

    def hgrn_post_bwd(i, t, dy, oh, hg, g):
        def one(dyh, ohh, hgh):
            dx, dg = _rms_bwd(ohh, g, dyh * _silu(hgh))
            return dx, dyh * _rms_fwd(ohh, g) * _dsilu(hgh), dg

        dx, dhg, dg = _heads_map(one, dy, oh, hg)
        dg_sum = dg[:, 0:LANE]
        for h in range(1, HEADS):
            dg_sum = dg_sum + dg[:, h * LANE:(h + 1) * LANE]
        return dx, dhg, dg_sum

    tok = emit({n: grads.pop(n) for n in ("w_out", "w_branch_mla", "w_branch_hgrn")})
    do_h, dhg, grads["g_hgrn_norm"] = _rowwise(
        "hgrn_post_bwd", hgrn_post_bwd,
        [row(do_hgrn, D_MODEL), row(o_h, D_MODEL), row(proj, D_MODEL, SEG_HG // D_MODEL), ("bc", p["g_hgrn_norm"])],
        [("row", D_MODEL, F32), ("row", D_MODEL, BF16), ("acc", (1, LANE))], rows, tm, after=tok)
    dhq, dhf, dhi, dlb = _hgrn_bwd(proj, lb, a_mat, s_states, do_h, rows)

    def lb_bwd(i, tm_, d, l):
        t = d * l * (1.0 - l)
        return t, -t

    dlb0, dlb1 = _rowwise("lb_bwd", lb_bwd, [("bc", dlb), ("bc", lb)], [("acc", (1, D_MODEL))] * 2, 1, 1)
    grads["lb_raw"] = jnp.concatenate([dlb0, dlb1], axis=0)

    dq_att, delta = _attn_bwd_dq(q_att, k_att, v_att, do_mla, o32, lse, rows)
    dk_att, dv_att = _attn_bwd_dkv(q_att, k_att, v_att, do_mla, lse.reshape(HEADS, 1, rows),
                                   delta.reshape(HEADS, 1, rows), rows)
    dq_full, dkv, dkr = _rope_bwd_call(dq_att, dk_att, dv_att, cos_t, sin_t, rows, tm)
    dqn = _matmul(dq_full, w["w_q_up"], "nt", F32, "mm_d_qn")
    grads["w_q_up"] = _matmul(qn, dq_full, "tn", BF16, "mm_dw_q_up")
    dkvn = _matmul(dkv, w["w_kv_up"], "nt", F32, "mm_d_kvn")
    grads["w_kv_up"] = _matmul(kvn, dkv, "tn", BF16, "mm_dw_kv_up")

    def latent_norm_bwd(i, t, ql, kl, dq, dk, gq, gk):
        dql, dgq = _rms_bwd(ql, gq, dq)
        dkl, dgk = _rms_bwd(kl, gk, dk)
        return dql, dkl, dgq, dgk

    dq_lat, dkv_lat, grads["g_q_norm"], grads["g_kv_norm"] = _rowwise(
        "latent_norm_bwd", latent_norm_bwd,
        [row(proj, Q_LORA, 0), row(proj, KV_LORA, SEG_KV_LAT // KV_LORA), row(dqn, Q_LORA), row(dkvn, KV_LORA),
         ("bc", p["g_q_norm"]), ("bc", p["g_kv_norm"])],
        [("row", Q_LORA, BF16), ("row", KV_LORA, BF16), ("acc", (1, Q_LORA)), ("acc", (1, KV_LORA))], rows, tm)
    dproj = jnp.concatenate([dq_lat, dkv_lat, dhq, dhf, dhi, dhg, dga, dgb, dkr], axis=1)
    grads["w_in"] = _matmul(u1, dproj, "tn", BF16, "mm_dw_in")
    tok = emit({n: grads.pop(n) for n in ("w_in", "w_q_up", "w_kv_up", "lb_raw", "g_q_norm", "g_kv_norm", "g_hgrn_norm")})
    du1 = _matmul(dproj, w["w_in"], "nt", F32, "mm_d_u1", after=tok)

    def mix_norm_bwd(i, t, h, du, dh, g):
        dx, dg = _rms_bwd(h, g, du)
        return dh + dx, dg

    dh0, grads["g_mix_norm"] = _rowwise(
        "mix_norm_bwd", mix_norm_bwd, [row(h0, D_MODEL), row(du1, D_MODEL), row(dh1, D_MODEL), ("bc", p["g_mix_norm"])],
        [("row", D_MODEL, F32), ("acc", (1, D_MODEL))], rows, tm)
    grads["meta_tokens"] = dh0[PAD_LEN:BLOCK]
    kept.update(grads)
    return loss_acc[0, 0], dh0[BLOCK:], kept


K_ROPE_AT = Q_LORA + KV_LORA
COL_SHARDED = ("w_in", "w_q_up", "w_kv_up", "w_ffn_in", "conv_w", "meta_tokens")
ROW_SHARDED = ("w_branch_mla", "w_branch_hgrn", "w_out", "w_ffn_out")
BIG = ("w_in", "w_q_up", "w_kv_up", "w_branch_mla", "w_branch_hgrn", "w_out", "w_ffn_in", "w_ffn_out")
SMALL = ("conv_b", "g_mix_norm", "g_q_norm", "g_kv_norm", "g_hgrn_norm", "g_ffn_norm", "g_final_norm", "lb_raw")


def _unshard(name, stacked):
    if name in COL_SHARDED:
        return jnp.transpose(stacked, (1, 0, 2)).reshape(stacked.shape[1], N_DEV * stacked.shape[2])
    return stacked.reshape(N_DEV * stacked.shape[1], stacked.shape[2])


def _reshard(name, full):
    if name in COL_SHARDED:
        r, c = full.shape
        return jnp.transpose(full.reshape(r, N_DEV, c // N_DEV), (1, 0, 2))
    return full.reshape(N_DEV, full.shape[0] // N_DEV, full.shape[1])


def _to_kernel_layout(full):
    out = dict(full)
    w_in = full["w_in"]
    pad = jnp.zeros((D_MODEL, KR_W - ROPE), w_in.dtype)
    out["w_in"] = jnp.concatenate([w_in[:, :K_ROPE_AT], w_in[:, K_ROPE_AT + ROPE:], w_in[:, K_ROPE_AT:K_ROPE_AT + ROPE], pad], axis=1)
    wq = full["w_q_up"].reshape(Q_LORA, HEADS, NOPE + ROPE)
    out["w_q_up"] = jnp.pad(wq, ((0, 0), (0, 0), (0, QHEAD_W - NOPE - ROPE))).reshape(Q_LORA, HEADS * QHEAD_W)
    return out


def _from_kernel_layout(grads):
    out = dict(grads)
    if "w_in" in grads:
        g = grads["w_in"]
        out["w_in"] = jnp.concatenate([g[:, :K_ROPE_AT], g[:, SEG_KR:SEG_KR + ROPE], g[:, K_ROPE_AT:SEG_KR]], axis=1)
    if "w_q_up" in grads:
        g = grads["w_q_up"].reshape(Q_LORA, HEADS, QHEAD_W)
        out["w_q_up"] = g[:, :, :NOPE + ROPE].reshape(Q_LORA, HEADS * (NOPE + ROPE))
    return out


MESH_ID = pl.DeviceIdType.MESH
ANY = pl.BlockSpec(memory_space=pl.ANY)


def _slot(dev):
    return 4 * dev[0] + 2 * dev[1] + dev[2]


def _all_gather(shards):
    n = len(shards)

    def body(*refs):
        ins, outs = refs[:n], refs[n:2 * n]
        send_sems, recv_sems, local_sems = refs[2 * n:]
        x, y, c = lax.axis_index("x"), lax.axis_index("y"), lax.axis_index("c")
        me, sibling = (x, y, c), (x, y, 1 - c)
        chips = [(1 - x, y), (x, 1 - y), (1 - x, 1 - y)]

        def copy(a, k, block, to, src=None):
            dst = outs[a].at[_slot(block)]
            return pltpu.make_async_remote_copy(
                src_ref=dst if src is None else src, dst_ref=dst, send_sem=send_sems.at[a, k],
                recv_sem=recv_sems.at[a, k], device_id=to, device_id_type=MESH_ID)

        mine = [pltpu.make_async_copy(ins[a], outs[a].at[_slot(me)], local_sems.at[a]) for a in range(n)]
        for cp in mine:
            cp.start()
        first = []
        for a in range(n):
            first.append(copy(a, 0, me, sibling, src=ins[a]))
            first += [copy(a, 1 + j, me, (*chip, c), src=ins[a]) for j, chip in enumerate(chips)]
        for cp in first:
            cp.start()
        passed = []
        for a in range(n):
            for j, chip in enumerate(chips):
                copy(a, 1 + j, (*chip, c), me).wait_recv()
                fwd = copy(a, 4 + j, (*chip, c), sibling)
                fwd.start()
                passed.append(fwd)
        for a in range(n):
            copy(a, 0, sibling, me).wait_recv()
            for j, chip in enumerate(chips):
                copy(a, 4 + j, (*chip, 1 - c), me).wait_recv()
        for cp in first + passed:
            cp.wait_send()
        for cp in mine:
            cp.wait()

    return pl.pallas_call(
        body,
        name="gather_weights",
        out_shape=[jax.ShapeDtypeStruct((N_DEV,) + s.shape, s.dtype) for s in shards],
        in_specs=[ANY] * n,
        out_specs=[ANY] * n,
        scratch_shapes=[pltpu.SemaphoreType.DMA((n, 7)), pltpu.SemaphoreType.DMA((n, 7)), pltpu.SemaphoreType.DMA((n,))],
    )(*shards)


def _exchange(blocked, replicated):
    nb, n = len(blocked), len(blocked) + len(replicated)
    arrays = list(blocked) + list(replicated)

    def body(*refs):
        ins, outs = refs[:n], refs[n:2 * n]
        send_sems, recv_sems, local_sems = refs[2 * n:]
        x, y, c = lax.axis_index("x"), lax.axis_index("y"), lax.axis_index("c")
        me = (x, y, c)
        peers = [(x, y, 1 - c), (1 - x, y, c), (x, 1 - y, c), (1 - x, 1 - y, c),
                 (1 - x, y, 1 - c), (x, 1 - y, 1 - c), (1 - x, 1 - y, 1 - c)]

        def src_of(a, dev):
            return ins[a].at[_slot(dev)] if a < nb else ins[a]

        def copy(a, k, frm, to):
            return pltpu.make_async_remote_copy(
                src_ref=src_of(a, to), dst_ref=outs[a].at[_slot(frm)], send_sem=send_sems.at[a, k],
                recv_sem=recv_sems.at[a, k], device_id=to, device_id_type=MESH_ID)

        mine = [pltpu.make_async_copy(src_of(a, me), outs[a].at[_slot(me)], local_sems.at[a]) for a in range(n)]
        for cp in mine:
            cp.start()
        sends = [copy(a, k, me, peer) for a in range(n) for k, peer in enumerate(peers)]
        for cp in sends:
            cp.start()
        for a in range(n):
            for k, peer in enumerate(peers):
                copy(a, k, peer, me).wait_recv()
        for cp in sends:
            cp.wait_send()
        for cp in mine:
            cp.wait()

    return pl.pallas_call(
        body,
        name="exchange_grads",
        out_shape=[jax.ShapeDtypeStruct(s.shape, s.dtype) for s in blocked]
        + [jax.ShapeDtypeStruct((N_DEV,) + s.shape, s.dtype) for s in replicated],
        in_specs=[ANY] * n,
        out_specs=[ANY] * n,
        scratch_shapes=[pltpu.SemaphoreType.DMA((n, 7)), pltpu.SemaphoreType.DMA((n, 7)), pltpu.SemaphoreType.DMA((n,))],
    )(*arrays)


ADAMW_BLOCK_ELEMS = 256 * 1024


def _adamw(name, parts, w, m, v):
    r, c = w.shape
    tr = _tile(r, max(16, ADAMW_BLOCK_ELEMS // c), 16)

    def body(p_ref, w_ref, m_ref, v_ref, g_ref, d_ref, nm_ref, nv_ref):
        g = p_ref[0].astype(F32)
        for s in range(1, N_DEV):
            g = g + p_ref[s].astype(F32)
        m_new = ADAM_B1 * m_ref[...] + (1.0 - ADAM_B1) * g
        v_new = ADAM_B2 * v_ref[...] + (1.0 - ADAM_B2) * (g * g)
        m_hat = m_new / (1.0 - ADAM_B1 ** ADAM_STEP)
        v_hat = v_new / (1.0 - ADAM_B2 ** ADAM_STEP)
        g_ref[...] = g
        d_ref[...] = -ADAM_LR * (m_hat / (jnp.sqrt(v_hat) + ADAM_EPS) + ADAM_WD * w_ref[...])
        nm_ref[...] = m_new
        nv_ref[...] = v_new

    blk = pl.BlockSpec((tr, c), lambda i: (i, 0))
    return pl.pallas_call(
        body,
        name="adamw_" + name,
        out_shape=[jax.ShapeDtypeStruct((r, c), F32)] * 4,
        grid=(r // tr,),
        in_specs=[pl.BlockSpec((N_DEV, tr, c), lambda i: (0, i, 0)), blk, blk, blk],
        out_specs=[blk] * 4,
        compiler_params=_params(("parallel",)),
    )(parts, w, m, v)


HBM_SPEC = pl.BlockSpec(memory_space=pltpu.HBM)
SEM_SPEC = pl.BlockSpec(memory_space=pltpu.SEMAPHORE)
SIDE_EFFECT = pltpu.SideEffectType.DATAFLOW_SIDE_EFFECTING
N_PEERS = N_DEV - 1


def _peers(x, y, c):
    return [(x, y, 1 - c), (1 - x, y, c), (x, 1 - y, c), (1 - x, 1 - y, c),
            (1 - x, y, 1 - c), (x, 1 - y, 1 - c), (1 - x, 1 - y, 1 - c)]


def _split_copy(srcs, lands, blocked, send_sems, recv_sems, a, k, frm, to):
    src = srcs[a].at[_slot(to)] if blocked[a] else srcs[a]
    return pltpu.make_async_remote_copy(
        src_ref=src, dst_ref=lands[a].at[_slot(frm)], send_sem=send_sems.at[a * N_PEERS + k],
        recv_sem=recv_sems.at[a * N_PEERS + k],
        device_id=to, device_id_type=MESH_ID)


def _exchange_start(name, srcs, lands, blocked):
    n = len(srcs)

    def body(*refs):
        src_refs, land_refs = refs[:n], refs[n:2 * n]
        send_sems, recv_sems = refs[2 * n], refs[2 * n + 1]
        token = refs[-1]
        x, y, c = lax.axis_index("x"), lax.axis_index("y"), lax.axis_index("c")
        for a in range(n):
            for k, peer in enumerate(_peers(x, y, c)):
                _split_copy(src_refs, land_refs, blocked, send_sems, recv_sems, a, k, (x, y, c), peer).start()
        token[...] = jnp.zeros_like(token)

    thru = [pltpu.HBM(s.shape, s.dtype) for s in list(srcs) + list(lands)]
    res = pl.pallas_call(
        body,
        name=name,
        out_shape=(pltpu.SemaphoreType.DMA((n * N_PEERS,)), pltpu.SemaphoreType.DMA((n * N_PEERS,)), *thru,
                   jax.ShapeDtypeStruct((8, LANE), F32)),
        in_specs=[HBM_SPEC] * (2 * n),
        out_specs=(SEM_SPEC, SEM_SPEC, *([HBM_SPEC] * (2 * n)), pl.BlockSpec(memory_space=pltpu.VMEM)),
        input_output_aliases={i: 2 + i for i in range(2 * n)},
        compiler_params=pltpu.CompilerParams(has_side_effects=SIDE_EFFECT),
    )(*[pltpu.with_memory_space_constraint(s, pltpu.HBM) for s in list(srcs) + list(lands)])
    return res[0], res[1], res[2:2 + n], res[2 + n:2 + 2 * n], res[-1]


def _exchange_wait(name, send_sems, recv_sems, srcs, lands, blocked, after):
    n, n_after = len(srcs), len(after)

    def body(*refs):
        src_refs, land_refs = refs[:n], refs[n:2 * n]
        send, recv = refs[2 * n], refs[2 * n + 1]
        x, y, c = lax.axis_index("x"), lax.axis_index("y"), lax.axis_index("c")
        for a in range(n):
            for k, peer in enumerate(_peers(x, y, c)):
                _split_copy(src_refs, land_refs, blocked, send, recv, a, k, (x, y, c), peer).wait_send()
                _split_copy(src_refs, land_refs, blocked, send, recv, a, k, peer, (x, y, c)).wait_recv()

    res = pl.pallas_call(
        body,
        name=name,
        out_shape=tuple(pltpu.HBM(s.shape, s.dtype) for s in list(srcs) + list(lands)),
        in_specs=[HBM_SPEC] * (2 * n) + [SEM_SPEC, SEM_SPEC] + [pl.BlockSpec(memory_space=pl.ANY)] * n_after,
        out_specs=tuple([HBM_SPEC] * (2 * n)),
        input_output_aliases={i: i for i in range(2 * n)},
        compiler_params=pltpu.CompilerParams(has_side_effects=SIDE_EFFECT),
    )(*srcs, *lands, send_sems, recv_sems, *after)
    return res[n:]


def kernel(x, positions, meta_tokens, w_in, w_q_up, w_kv_up, w_branch_mla, w_branch_hgrn, w_out, w_ffn_in, w_ffn_out, conv_w, conv_b, g_mix_norm, g_q_norm, g_kv_norm, g_hgrn_norm, g_ffn_norm, g_final_norm, lb_raw, loss_target, m_meta_tokens, m_w_in, m_w_q_up, m_w_kv_up, m_w_branch_mla, m_w_branch_hgrn, m_w_out, m_w_ffn_in, m_w_ffn_out, m_conv_w, m_conv_b, m_g_mix_norm, m_g_q_norm, m_g_kv_norm, m_g_hgrn_norm, m_g_ffn_norm, m_g_final_norm, m_lb_raw, v_meta_tokens, v_w_in, v_w_q_up, v_w_kv_up, v_w_branch_mla, v_w_branch_hgrn, v_w_out, v_w_ffn_in, v_w_ffn_out, v_conv_w, v_conv_b, v_g_mix_norm, v_g_q_norm, v_g_kv_norm, v_g_hgrn_norm, v_g_ffn_norm, v_g_final_norm, v_lb_raw):
    local = dict(zip(
        ("meta_tokens", "w_in", "w_q_up", "w_kv_up", "w_branch_mla", "w_branch_hgrn", "w_out", "w_ffn_in", "w_ffn_out",
         "conv_w", "conv_b", "g_mix_norm", "g_q_norm", "g_kv_norm", "g_hgrn_norm", "g_ffn_norm", "g_final_norm", "lb_raw"),
        (meta_tokens, w_in, w_q_up, w_kv_up, w_branch_mla, w_branch_hgrn, w_out, w_ffn_in, w_ffn_out,
         conv_w, conv_b, g_mix_norm, g_q_norm, g_kv_norm, g_hgrn_norm, g_ffn_norm, g_final_norm, lb_raw)))
    mom_m = dict(zip(local, (m_meta_tokens, m_w_in, m_w_q_up, m_w_kv_up, m_w_branch_mla, m_w_branch_hgrn, m_w_out, m_w_ffn_in,
                             m_w_ffn_out, m_conv_w, m_conv_b, m_g_mix_norm, m_g_q_norm, m_g_kv_norm, m_g_hgrn_norm,
                             m_g_ffn_norm, m_g_final_norm, m_lb_raw)))
    mom_v = dict(zip(local, (v_meta_tokens, v_w_in, v_w_q_up, v_w_kv_up, v_w_branch_mla, v_w_branch_hgrn, v_w_out, v_w_ffn_in,
                             v_w_ffn_out, v_conv_w, v_conv_b, v_g_mix_norm, v_g_q_norm, v_g_kv_norm, v_g_hgrn_norm,
                             v_g_ffn_norm, v_g_final_norm, v_lb_raw)))
    sharded = BIG + ("conv_w", "meta_tokens")

    def shard2d(name, arr):
        return arr.reshape(arr.shape[-2:]) if name != "meta_tokens" else arr

    shards = [shard2d(n, local[n]).astype(BF16) for n in BIG] + [shard2d(n, local[n]) for n in ("conv_w", "meta_tokens")]
    gathered = _all_gather(shards)
    full = _to_kernel_layout({n: _unshard(n, g) for n, g in zip(sharded, gathered)})
    small = {n: local[n] for n in SMALL}

    def as2d(name, arr):
        return arr.reshape(1, -1) if arr.ndim == 1 else shard2d(name, arr)

    me = 4 * lax.axis_index("x") + 2 * lax.axis_index("y") + lax.axis_index("c")
    started = []

    def sources(group):
        group = _from_kernel_layout(group)
        names = list(group)
        blocked = [n in sharded for n in names]
        srcs = [_reshard(n, group[n]) if b else as2d(n, group[n]) for n, b in zip(names, blocked)]
        return names, blocked, srcs

    def emit(group):
        names, blocked, srcs = sources(group)
        lands = []
        for s, b in zip(srcs, blocked):
            own = lax.dynamic_index_in_dim(s, me, 0, keepdims=True) if b else s[None]
            zone = lax.empty((N_DEV,) + own.shape[1:], own.dtype)
            lands.append(lax.dynamic_update_slice_in_dim(zone, own, me, 0))
        k = len(started)
        send, recv, srcs_thru, lands_thru, token = _exchange_start(f"exchange_start_{k}", srcs, lands, blocked)
        started.append((names, blocked, send, recv, srcs_thru, lands_thru))
        return token

    loss, grad_x, last = _local_step(x[0], positions[0], loss_target[0], full, small, emit)

    out = {}

    def update(names, parts):
        for n, part in zip(names, parts):
            res = _adamw(n, part, as2d(n, local[n]), as2d(n, mom_m[n]), as2d(n, mom_v[n]))
            out[n] = [r.reshape(local[n].shape) for r in res]

    after = [grad_x]
    for k, (names, blocked, send, recv, srcs_thru, lands_thru) in enumerate(started):
        update(names, _exchange_wait(f"exchange_wait_{k}", send, recv, srcs_thru, lands_thru, blocked, after))
        after = [out[names[0]][0]]
    names, blocked, srcs = sources(last)
    in_blocks = [(n, s) for n, s, b in zip(names, srcs, blocked) if b]
    whole = [(n, s) for n, s, b in zip(names, srcs, blocked) if not b]
    update([n for n, _ in in_blocks + whole], _exchange([s for _, s in in_blocks], [s for _, s in whole]))

    loss = lax.psum(loss, ("x", "y", "c"))
    order = tuple(local)
    return (loss, grad_x[None], *[out[n][0] for n in order], *[out[n][1] for n in order],
            *[out[n][2] for n in order], *[out[n][3] for n in order])
```

```python
import functools

import jax
import jax.numpy as jnp
import numpy as np
from jax import lax
from jax.experimental import pallas as pl
from jax.experimental.pallas import tpu as pltpu

F32 = jnp.float32
BF16 = jnp.bfloat16

D_MODEL = 2048
N_META = 16
BLOCK = 128
PAD_LEN = BLOCK - N_META
HEADS = 16
Q_LORA = 1536
KV_LORA = 512
ROPE = 64
NOPE = 128
VDIM = 128
D_FF = 5632
NORM_EPS = 1e-6
ROPE_THETA = 10000.0
ATTN_SCALE = (NOPE + ROPE) ** -0.5
ADAM_LR = 0.001
ADAM_B1 = 0.9
ADAM_B2 = 0.999
ADAM_EPS = 1e-08
ADAM_WD = 0.01
ADAM_STEP = 10
N_DEV = 8

LANE = 128
SEG_Q_LAT = 0
SEG_KV_LAT = Q_LORA
SEG_HQ = 2048
SEG_HF = SEG_HQ + D_MODEL
SEG_HI = SEG_HF + D_MODEL
SEG_HG = SEG_HI + D_MODEL
SEG_GA = SEG_HG + D_MODEL
SEG_GB = SEG_GA + D_MODEL
SEG_KR = SEG_GB + D_MODEL
KR_W = 256
PROJ_W = SEG_KR + KR_W
QHEAD_W = 256

V7X_VMEM_BYTES = 64 * 1024 * 1024
VMEM_LIMIT = V7X_VMEM_BYTES * 7 // 8
NEG_BIG = -1e30
SUB = 16


def _tile(n, target, mult):
    best = None
    for t in range(mult, min(n, target) + 1, mult):
        if n % t == 0:
            best = t
    return n if best is None else best


def _params(sem):
    return pltpu.CompilerParams(dimension_semantics=sem, vmem_limit_bytes=VMEM_LIMIT)


def _sigmoid(x):
    return 1.0 / (1.0 + jnp.exp(-x))


_DIMS = {"nn": (((1,), (0,)), ((), ())), "nt": (((1,), (1,)), ((), ())), "tn": (((0,), (0,)), ((), ()))}


def _matmul(a, b, mode, out_dtype, name, after=None):
    if mode == "nn":
        (m, k), (_, n) = a.shape, b.shape
    elif mode == "nt":
        (m, k), (n, _) = a.shape, b.shape
    else:
        (k, m), (_, n) = a.shape, b.shape
    tm = _tile(m, 1040, 8) if mode != "tn" else _tile(m, 1024, LANE)
    tn = _tile(n, 1024, LANE)
    tk = _tile(k, 2816, LANE) if mode != "tn" else _tile(k, 2080, 8)
    nk = k // tk
    if mode == "nn":
        a_spec = pl.BlockSpec((tm, tk), lambda i, j, kk: (i, kk))
        b_spec = pl.BlockSpec((tk, tn), lambda i, j, kk: (kk, j))
    elif mode == "nt":
        a_spec = pl.BlockSpec((tm, tk), lambda i, j, kk: (i, kk))
        b_spec = pl.BlockSpec((tn, tk), lambda i, j, kk: (j, kk))
    else:
        a_spec = pl.BlockSpec((tk, tm), lambda i, j, kk: (kk, i))
        b_spec = pl.BlockSpec((tk, tn), lambda i, j, kk: (kk, j))
    dims = _DIMS[mode]

    n_after = 0 if after is None else 1

    def body(a_ref, b_ref, *rest):
        o_ref, acc = rest[n_after], rest[n_after + 1:]
        part = lax.dot_general(a_ref[...], b_ref[...], dims, preferred_element_type=F32)
        if nk == 1:
            o_ref[...] = part.astype(o_ref.dtype)
            return
        acc_ref, kk = acc[0], pl.program_id(2)

        @pl.when(kk == 0)
        def _():
            acc_ref[...] = part

        @pl.when((kk > 0) & (kk < nk - 1))
        def _():
            acc_ref[...] += part

        @pl.when(kk == nk - 1)
        def _():
            o_ref[...] = (acc_ref[...] + part).astype(o_ref.dtype)

    return pl.pallas_call(
        body,
        name=name,
        out_shape=jax.ShapeDtypeStruct((m, n), out_dtype),
        grid=(m // tm, n // tn, nk),
        in_specs=[a_spec, b_spec] + [pl.BlockSpec(memory_space=pl.ANY)] * n_after,
        out_specs=pl.BlockSpec((tm, tn), lambda i, j, kk: (i, j)),
        scratch_shapes=[pltpu.VMEM((tm, tn), F32)] if nk > 1 else [],
        compiler_params=_params(("parallel", "parallel", "arbitrary")),
    )(a, b, *([after] * n_after))


ROW_WINDOW_BYTES = 12 * 1024 * 1024


def _rowwise(name, fn, ins, outs, rows, tm, after=None):
    per_row = sum(s[2] * s[1].dtype.itemsize for s in ins if s[0] == "row")
    per_row += sum(s[1] * jnp.dtype(s[2]).itemsize for s in outs if s[0] == "row")
    if per_row:
        tm = _tile(rows, min(tm, max(8, ROW_WINDOW_BYTES // (2 * per_row))), 8)
    n_in = len(ins)
    in_specs, args = [], []
    for spec in ins:
        if spec[0] == "row":
            _, arr, w, cb = spec
            in_specs.append(pl.BlockSpec((tm, w), functools.partial(lambda i, cb: (i, cb), cb=cb)))
        else:
            arr = spec[1]
            in_specs.append(pl.BlockSpec(arr.shape, lambda i: (0, 0)))
        args.append(arr)
    out_shape, out_specs = [], []
    for spec in outs:
        if spec[0] == "row":
            out_shape.append(jax.ShapeDtypeStruct((rows, spec[1]), spec[2]))
            out_specs.append(pl.BlockSpec((tm, spec[1]), lambda i: (i, 0)))
        else:
            out_shape.append(jax.ShapeDtypeStruct(spec[1], F32))
            out_specs.append(pl.BlockSpec(spec[1], lambda i: (0, 0)))
    has_acc = any(s[0] == "acc" for s in outs)
    n_after = 0 if after is None else 1
    in_specs += [pl.BlockSpec(memory_space=pl.ANY)] * n_after
    args += [after] * n_after

    def body(*refs):
        i = pl.program_id(0)
        res = fn(i, tm, *[r[...] for r in refs[:n_in]])
        for spec, ref, val in zip(outs, refs[n_in + n_after:], res):
            if spec[0] == "row":
                ref[...] = val.astype(ref.dtype)
            else:
                @pl.when(i == 0)
                def _(ref=ref, val=val):
                    ref[...] = val

                @pl.when(i > 0)
                def _(ref=ref, val=val):
                    ref[...] += val

    return pl.pallas_call(
        body,
        name=name,
        out_shape=out_shape,
        grid=(rows // tm,),
        in_specs=in_specs,
        out_specs=out_specs,
        compiler_params=_params(("arbitrary" if has_acc else "parallel",)),
    )(*args)


def _row_ids(i, tm, shape):
    return i * tm + lax.broadcasted_iota(jnp.int32, shape, 0)


def _rms_fwd(x, g):
    r = lax.rsqrt(jnp.mean(x * x, axis=-1, keepdims=True) + NORM_EPS)
    return x * r * g


def _rms_bwd(x, g, dy):
    r = lax.rsqrt(jnp.mean(x * x, axis=-1, keepdims=True) + NORM_EPS)
    xhat = x * r
    dxhat = dy * g
    dx = r * (dxhat - xhat * jnp.mean(dxhat * xhat, axis=-1, keepdims=True))
    return dx, jnp.sum(dy * xhat, axis=0, keepdims=True)


def _silu(x):
    return x * _sigmoid(x)


def _dsilu(x):
    s = _sigmoid(x)
    return s * (1.0 + x * (1.0 - s))


def _rot_src(x):
    lane = lax.broadcasted_iota(jnp.int32, x.shape, 1)
    return jnp.where(lane < ROPE // 2, pltpu.roll(x, LANE - ROPE // 2, 1), pltpu.roll(x, ROPE // 2, 1))


def _rope_fwd_call(q_raw, kv, proj, cos_t, sin_t, rows, tm):
    def fn(i, tm_, q, kvv, kr, c, s):
        kr_rot = kr[:, :LANE]
        kr_rot = kr_rot * c + _rot_src(kr_rot) * s
        qs, ks, vs = [], [], []
        for h in range(HEADS):
            qn = q[:, h * QHEAD_W:h * QHEAD_W + NOPE]
            qr = q[:, h * QHEAD_W + NOPE:(h + 1) * QHEAD_W]
            qs += [qn, qr * c + _rot_src(qr) * s]
            ks += [kvv[:, h * 2 * NOPE:h * 2 * NOPE + NOPE], kr_rot]
            vs += [kvv[:, h * 2 * NOPE + NOPE:(h + 1) * 2 * NOPE]]
        return jnp.concatenate(qs, axis=1), jnp.concatenate(ks, axis=1), jnp.concatenate(vs, axis=1)

    return _rowwise(
        "rope_fwd", fn,
        [("row", q_raw, HEADS * QHEAD_W, 0), ("row", kv, HEADS * 2 * NOPE, 0), ("row", proj, KR_W, SEG_KR // KR_W),
         ("row", cos_t, LANE, 0), ("row", sin_t, LANE, 0)],
        [("row", HEADS * QHEAD_W, BF16), ("row", HEADS * QHEAD_W, BF16), ("row", HEADS * VDIM, BF16)],
        rows, tm)


def _rope_bwd_call(dq_att, dk_att, dv, cos_t, sin_t, rows, tm):
    def fn(i, tm_, dq, dk, dvv, c, s):
        qs, kvs = [], []
        dkr = jnp.zeros((dq.shape[0], LANE), F32)
        for h in range(HEADS):
            dqr = dq[:, h * QHEAD_W + NOPE:(h + 1) * QHEAD_W]
            qs += [dq[:, h * QHEAD_W:h * QHEAD_W + NOPE], dqr * c - _rot_src(dqr) * s]
            kvs += [dk[:, h * QHEAD_W:h * QHEAD_W + NOPE], dvv[:, h * VDIM:(h + 1) * VDIM]]
            dkr = dkr + dk[:, h * QHEAD_W + NOPE:(h + 1) * QHEAD_W]
        dkr = dkr * c - _rot_src(dkr) * s
        return (jnp.concatenate(qs, axis=1), jnp.concatenate(kvs, axis=1),
                jnp.concatenate([dkr, jnp.zeros_like(dkr)], axis=1))

    return _rowwise(
        "rope_bwd", fn,
        [("row", dq_att, HEADS * QHEAD_W, 0), ("row", dk_att, HEADS * QHEAD_W, 0), ("row", dv, HEADS * VDIM, 0),
         ("row", cos_t, LANE, 0), ("row", sin_t, LANE, 0)],
        [("row", HEADS * QHEAD_W, BF16), ("row", HEADS * 2 * NOPE, BF16), ("row", KR_W, BF16)],
        rows, tm)


def _attn_mask(q_blk, k_blk, t, keys_on_rows=False):
    qa, ka = (1, 0) if keys_on_rows else (0, 1)
    qs = q_blk * t + lax.broadcasted_iota(jnp.int32, (t, t), qa)
    ks = k_blk * t + lax.broadcasted_iota(jnp.int32, (t, t), ka)
    return (ks <= qs) & ((ks >= PAD_LEN) | (ks == qs))


_NT = _DIMS["nt"]
_TN = _DIMS["tn"]
LOG2E = 1.4426950408889634
SCORE_TO_LOG2 = ATTN_SCALE * LOG2E


def _causal_pairs(nb, by_key):
    if by_key:
        pairs = [(qi, kj) for kj in range(nb) for qi in range(kj, nb)]
    else:
        pairs = [(qi, kj) for qi in range(nb) for kj in range(qi + 1)]
    return (jnp.asarray(np.array([p[0] for p in pairs], np.int32)), jnp.asarray(np.array([p[1] for p in pairs], np.int32)))


def _attn_fwd(q_att, k_att, v, rows):
    t = _tile(rows, 640, LANE)
    nb = rows // t

    def body(qt_ref, kt_ref, q_ref, k_ref, v_ref, o32_ref, obf_ref, lse_ref, m_sc, l_sc, acc_sc):
        qi, kj = qt_ref[pl.program_id(1)], kt_ref[pl.program_id(1)]

        @pl.when(kj == 0)
        def _():
            m_sc[...] = jnp.full_like(m_sc, NEG_BIG)
            l_sc[...] = jnp.zeros_like(l_sc)
            acc_sc[...] = jnp.zeros_like(acc_sc)

        def step(masked):
            s = lax.dot_general(q_ref[...], k_ref[...], _NT, preferred_element_type=F32) * SCORE_TO_LOG2
            if masked:
                s = jnp.where(_attn_mask(qi, kj, t), s, NEG_BIG)
            m_prev = m_sc[...]
            m_new = jnp.maximum(m_prev, jnp.max(s, axis=1, keepdims=True))
            alpha = jnp.exp2(m_prev - m_new)
            p = jnp.exp2(s - jnp.tile(m_new, (1, t // LANE)))
            l_sc[...] = alpha * l_sc[...] + jnp.sum(p, axis=1, keepdims=True)
            acc_sc[...] = alpha * acc_sc[...] + jnp.dot(p.astype(BF16), v_ref[...], preferred_element_type=F32)
            m_sc[...] = m_new

        pl.when((kj == qi) | (kj == 0))(functools.partial(step, True))
        pl.when((kj < qi) & (kj > 0))(functools.partial(step, False))

        @pl.when(kj == qi)
        def _():
            o = acc_sc[...] / l_sc[...]
            o32_ref[...] = o
            obf_ref[...] = o.astype(BF16)
            lse_ref[0] = m_sc[:, 0:1] + jnp.log2(l_sc[:, 0:1])

    qt, kt = _causal_pairs(nb, by_key=False)
    qmap = lambda h, p, qt_ref, kt_ref: (qt_ref[p], h)
    kmap = lambda h, p, qt_ref, kt_ref: (kt_ref[p], h)
    return pl.pallas_call(
        body,
        name="attn_fwd",
        out_shape=[jax.ShapeDtypeStruct((rows, HEADS * VDIM), F32), jax.ShapeDtypeStruct((rows, HEADS * VDIM), BF16),
                   jax.ShapeDtypeStruct((HEADS, rows, 1), F32)],
        grid_spec=pltpu.PrefetchScalarGridSpec(
            num_scalar_prefetch=2,
            grid=(HEADS, len(qt)),
            in_specs=[pl.BlockSpec((t, QHEAD_W), qmap), pl.BlockSpec((t, QHEAD_W), kmap), pl.BlockSpec((t, VDIM), kmap)],
            out_specs=[pl.BlockSpec((t, VDIM), qmap), pl.BlockSpec((t, VDIM), qmap),
                       pl.BlockSpec((1, t, 1), lambda h, p, qt_ref, kt_ref: (h, qt_ref[p], 0))],
            scratch_shapes=[pltpu.VMEM((t, LANE), F32), pltpu.VMEM((t, LANE), F32), pltpu.VMEM((t, VDIM), F32)]),
        compiler_params=_params(("parallel", "arbitrary")),
    )(qt, kt, q_att, k_att, v)


def _attn_bwd_dq(q_att, k_att, v, do, o32, lse, rows):
    t = _tile(rows, 640, LANE)
    nb = rows // t

    def body(qt_ref, kt_ref, q_ref, k_ref, v_ref, do_ref, o_ref, lse_ref, dq_ref, delta_ref, acc_sc, dl_sc, lse_sc):
        qi, kj = qt_ref[pl.program_id(1)], kt_ref[pl.program_id(1)]

        @pl.when(kj == 0)
        def _():
            acc_sc[...] = jnp.zeros_like(acc_sc)
            delta = jnp.sum(do_ref[...].astype(F32) * o_ref[...], axis=1, keepdims=True)
            delta_ref[0] = delta
            dl_sc[...] = jnp.broadcast_to(delta, (t, LANE))
            lse_sc[...] = jnp.broadcast_to(lse_ref[0], (t, LANE))

        def step(masked):
            s = lax.dot_general(q_ref[...], k_ref[...], _NT, preferred_element_type=F32) * SCORE_TO_LOG2
            p = jnp.exp2(s - jnp.tile(lse_sc[...], (1, t // LANE)))
            if masked:
                p = jnp.where(_attn_mask(qi, kj, t), p, 0.0)
            dp = lax.dot_general(do_ref[...], v_ref[...], _NT, preferred_element_type=F32)
            ds = p * (dp - jnp.tile(dl_sc[...], (1, t // LANE)))
            acc_sc[...] += jnp.dot(ds.astype(BF16), k_ref[...], preferred_element_type=F32)

        pl.when((kj == qi) | (kj == 0))(functools.partial(step, True))
        pl.when((kj < qi) & (kj > 0))(functools.partial(step, False))

        @pl.when(kj == qi)
        def _():
            dq_ref[...] = acc_sc[...] * ATTN_SCALE

    qt, kt = _causal_pairs(nb, by_key=False)
    qmap = lambda h, p, qt_ref, kt_ref: (qt_ref[p], h)
    kmap = lambda h, p, qt_ref, kt_ref: (kt_ref[p], h)
    stat = pl.BlockSpec((1, t, 1), lambda h, p, qt_ref, kt_ref: (h, qt_ref[p], 0))
    return pl.pallas_call(
        body,
        name="attn_bwd_dq",
        out_shape=[jax.ShapeDtypeStruct((rows, HEADS * QHEAD_W), F32), jax.ShapeDtypeStruct((HEADS, rows, 1), F32)],
        grid_spec=pltpu.PrefetchScalarGridSpec(
            num_scalar_prefetch=2,
            grid=(HEADS, len(qt)),
            in_specs=[pl.BlockSpec((t, QHEAD_W), qmap), pl.BlockSpec((t, QHEAD_W), kmap), pl.BlockSpec((t, VDIM), kmap),
                      pl.BlockSpec((t, VDIM), qmap), pl.BlockSpec((t, VDIM), qmap), stat],
            out_specs=[pl.BlockSpec((t, QHEAD_W), qmap), stat],
            scratch_shapes=[pltpu.VMEM((t, QHEAD_W), F32), pltpu.VMEM((t, LANE), F32), pltpu.VMEM((t, LANE), F32)]),
        compiler_params=_params(("parallel", "arbitrary")),
    )(qt, kt, q_att, k_att, v, do, o32, lse)


def _attn_bwd_dkv(q_att, k_att, v, do, lse_row, delta_row, rows):
    t = _tile(rows, 640, LANE)
    nb = rows // t

    def body(qt_ref, kt_ref, q_ref, k_ref, v_ref, do_ref, lse_ref, delta_ref, dk_ref, dv_ref, dk_sc, dv_sc):
        qi, kj = qt_ref[pl.program_id(1)], kt_ref[pl.program_id(1)]

        @pl.when(qi == kj)
        def _():
            dk_sc[...] = jnp.zeros_like(dk_sc)
            dv_sc[...] = jnp.zeros_like(dv_sc)

        def step(masked):
            st = lax.dot_general(k_ref[...], q_ref[...], _NT, preferred_element_type=F32) * SCORE_TO_LOG2
            pt = jnp.exp2(st - lse_ref[0])
            if masked:
                pt = jnp.where(_attn_mask(qi, kj, t, keys_on_rows=True), pt, 0.0)
            dv_sc[...] += jnp.dot(pt.astype(BF16), do_ref[...], preferred_element_type=F32)
            dpt = lax.dot_general(v_ref[...], do_ref[...], _NT, preferred_element_type=F32)
            dst = pt * (dpt - delta_ref[0])
            dk_sc[...] += jnp.dot(dst.astype(BF16), q_ref[...], preferred_element_type=F32)

        pl.when((qi == kj) | (kj == 0))(functools.partial(step, True))
        pl.when((qi > kj) & (kj > 0))(functools.partial(step, False))

        @pl.when(qi == nb - 1)
        def _():
            dk_ref[...] = dk_sc[...] * ATTN_SCALE
            dv_ref[...] = dv_sc[...]

    qt, kt = _causal_pairs(nb, by_key=True)
    qmap = lambda h, p, qt_ref, kt_ref: (qt_ref[p], h)
    kmap = lambda h, p, qt_ref, kt_ref: (kt_ref[p], h)
    stat = pl.BlockSpec((1, 1, t), lambda h, p, qt_ref, kt_ref: (h, 0, qt_ref[p]))
    return pl.pallas_call(
        body,
        name="attn_bwd_dkv",
        out_shape=[jax.ShapeDtypeStruct((rows, HEADS * QHEAD_W), F32), jax.ShapeDtypeStruct((rows, HEADS * VDIM), F32)],
        grid_spec=pltpu.PrefetchScalarGridSpec(
            num_scalar_prefetch=2,
            grid=(HEADS, len(qt)),
            in_specs=[pl.BlockSpec((t, QHEAD_W), qmap), pl.BlockSpec((t, QHEAD_W), kmap), pl.BlockSpec((t, VDIM), kmap),
                      pl.BlockSpec((t, VDIM), qmap), stat, stat],
            out_specs=[pl.BlockSpec((t, QHEAD_W), kmap), pl.BlockSpec((t, VDIM), kmap)],
            scratch_shapes=[pltpu.VMEM((t, QHEAD_W), F32), pltpu.VMEM((t, VDIM), F32)]),
        compiler_params=_params(("parallel", "arbitrary")),
    )(qt, kt, q_att, k_att, v, do, lse_row, delta_row)


C = BLOCK


def _hgrn_prep(hq, hf, hi, lb, c):
    rows = c * C + lax.broadcasted_iota(jnp.int32, (C, C), 0)
    valid = rows >= PAD_LEN
    sg = _sigmoid(hf)
    f = lb + (1.0 - lb) * sg
    g = jnp.where(valid, jnp.log(f), 0.0)
    k = jnp.where(valid, 1.0 - f, 0.0)
    q = _silu(hq)
    r = lax.broadcasted_iota(jnp.int32, (C, C), 0)
    cc = lax.broadcasted_iota(jnp.int32, (C, C), 1)
    tri = jnp.where(cc <= r, 1.0, 0.0).astype(F32)
    b = jnp.dot(tri, g, precision=lax.Precision.HIGHEST, preferred_element_type=F32)
    return q, k, hi, b, f, sg, valid


def _last_row_as_col(b_t):
    lane = lax.broadcasted_iota(jnp.int32, b_t.shape, 1)
    return jnp.sum(jnp.where(lane == C - 1, b_t, 0.0), axis=1, keepdims=True)


def _k_scaled(k, b, bs):
    return (k * jnp.exp(jnp.minimum(bs - b, 0.0))).astype(BF16)


def _hgrn_fwd(proj, lb, rows):
    nc = rows // C

    def body(hq_ref, hf_ref, hi_ref, lb_ref, o_ref, a_ref, s_ref, s_sc, b_sc):
        c = pl.program_id(1)

        @pl.when(c == 0)
        def _():
            s_sc[...] = jnp.zeros_like(s_sc)

        q, k, v, b, _, _, _ = _hgrn_prep(hq_ref[...], hf_ref[...], hi_ref[...], lb_ref[...], c)
        b_sc[...] = b
        s0 = s_sc[...]
        s_ref[0, 0] = s0
        v_bf = v.astype(BF16)
        r16 = lax.broadcasted_iota(jnp.int32, (SUB, C), 0)
        c16 = lax.broadcasted_iota(jnp.int32, (SUB, C), 1)
        slabs = [jnp.zeros((SUB, C), F32)]
        for i in range(1, C // SUB):
            bs = b_sc[SUB * i - 1:SUB * i, :]
            qs = (q[SUB * i:SUB * (i + 1)] * jnp.exp(b[SUB * i:SUB * (i + 1)] - bs)).astype(BF16)
            a_i = lax.dot_general(qs, _k_scaled(k, b, bs), _NT, preferred_element_type=F32)
            slabs.append(jnp.where(c16 <= r16 + (SUB * i - SUB), a_i, 0.0))
        a_off = jnp.concatenate(slabs, axis=0)
        q_t, k_t, b_t = q.T, k.T, b.T
        sub = lax.broadcasted_iota(jnp.int32, (C, C), 0)
        lane = lax.broadcasted_iota(jnp.int32, (C, C), 1)
        lane1 = lax.broadcasted_iota(jnp.int32, (1, C), 1)
        at_band = jnp.zeros((C, C), F32)
        for dl in range(SUB):
            k_s = pltpu.roll(k_t, dl, 1) if dl else k_t
            b_s = pltpu.roll(b_t, dl, 1) if dl else b_t
            e = jnp.exp(jnp.minimum(b_t - b_s, 0.0))
            band = jnp.sum(q_t * k_s * e, axis=0, keepdims=True)
            band = jnp.where(lane1 >= dl, band, 0.0)
            at_band = at_band + jnp.where(sub == lane - dl, jnp.broadcast_to(band, (C, C)), 0.0)
        a = (a_off + at_band.T).astype(BF16)
        a_ref[0] = a
        qe = (q * jnp.exp(b)).astype(BF16)
        o_ref[...] = (jnp.dot(a, v_bf, preferred_element_type=F32)
                      + jnp.dot(qe, s0.astype(BF16), preferred_element_type=F32))
        b_last = b_sc[C - 1:C, :]
        kd = (k * jnp.exp(b_last - b)).astype(BF16)
        s_sc[...] = (jnp.exp(_last_row_as_col(b_t)) * s0
                     + lax.dot_general(kd, v_bf, _TN, preferred_element_type=F32))

    seg = lambda base: (lambda h, c: (c, base // C + h))
    return pl.pallas_call(
        body,
        name="hgrn_fwd",
        out_shape=[jax.ShapeDtypeStruct((rows, D_MODEL), F32), jax.ShapeDtypeStruct((HEADS, rows, C), BF16),
                   jax.ShapeDtypeStruct((HEADS, nc, C, C), F32)],
        grid=(HEADS, nc),
        in_specs=[pl.BlockSpec((C, C), seg(SEG_HQ)), pl.BlockSpec((C, C), seg(SEG_HF)), pl.BlockSpec((C, C), seg(SEG_HI)),
                  pl.BlockSpec((1, C), lambda h, c: (0, h))],
        out_specs=[pl.BlockSpec((C, C), lambda h, c: (c, h)), pl.BlockSpec((1, C, C), lambda h, c: (h, c, 0)),
                   pl.BlockSpec((1, 1, C, C), lambda h, c: (h, c, 0, 0))],
        scratch_shapes=[pltpu.VMEM((C, C), F32), pltpu.VMEM((C, C), F32)],
        compiler_params=_params(("parallel", "arbitrary")),
    )(proj, proj, proj, lb)


def _hgrn_bwd(proj, lb, a_mat, s_states, do_h, rows):
    nc = rows // C

    def body(hq_ref, hf_ref, hi_ref, lb_ref, a_ref, s_ref, do_ref, dhq_ref, dhf_ref, dhi_ref, dlb_ref, ds_sc, b_sc):
        step = pl.program_id(1)
        c = nc - 1 - step

        @pl.when(step == 0)
        def _():
            ds_sc[...] = jnp.zeros_like(ds_sc)
            dlb_ref[...] = jnp.zeros_like(dlb_ref)

        hq, hf = hq_ref[...], hf_ref[...]
        lb_row = lb_ref[...]
        q, k, v, b, f, sg, valid = _hgrn_prep(hq, hf, hi_ref[...], lb_row, c)
        b_sc[...] = b
        s0 = s_ref[0, 0]
        ds1 = ds_sc[...]
        s0_bf, ds1_bf = s0.astype(BF16), ds1.astype(BF16)
        do = do_ref[...]
        do_bf, v_bf = do.astype(BF16), v.astype(BF16)
        b_last = b_sc[C - 1:C, :]
        e_last = jnp.exp(b_last - b)
        eb = jnp.exp(b)
        sub = lax.broadcasted_iota(jnp.int32, (C, C), 0)
        lane = lax.broadcasted_iota(jnp.int32, (C, C), 1)
        r16 = lax.broadcasted_iota(jnp.int32, (SUB, C), 0)
        c16 = lax.broadcasted_iota(jnp.int32, (SUB, C), 1)

        dv = (lax.dot_general(a_ref[0], do_bf, _TN, preferred_element_type=F32)
              + jnp.dot((k * e_last).astype(BF16), ds1_bf, preferred_element_type=F32))
        da = jnp.where(lane <= sub, lax.dot_general(do_bf, v_bf, _NT, preferred_element_type=F32), 0.0)
        da_t = jnp.where(sub <= lane, lax.dot_general(v_bf, do_bf, _NT, preferred_element_type=F32), 0.0)

        dq_slabs = [jnp.zeros((SUB, C), F32)]
        for i in range(1, C // SUB):
            bs = b_sc[SUB * i - 1:SUB * i, :]
            da_i = jnp.where(c16 <= r16 + (SUB * i - SUB), da[SUB * i:SUB * (i + 1)], 0.0).astype(BF16)
            dq_slabs.append(jnp.exp(b[SUB * i:SUB * (i + 1)] - bs)
                            * jnp.dot(da_i, _k_scaled(k, b, bs), preferred_element_type=F32))
        dk_slabs = []
        for j in range(C // SUB - 1):
            be = b_sc[SUB * j + SUB - 1:SUB * (j + 1), :]
            qe_j = (q * jnp.exp(jnp.minimum(b - be, 0.0))).astype(BF16)
            da_j = jnp.where(c16 >= r16 + (SUB * j + SUB), da_t[SUB * j:SUB * (j + 1)], 0.0).astype(BF16)
            dk_slabs.append(jnp.exp(be - b[SUB * j:SUB * (j + 1)]) * jnp.dot(da_j, qe_j, preferred_element_type=F32))
        dk_slabs.append(jnp.zeros((SUB, C), F32))

        q_t, k_t, b_t = q.T, k.T, b.T
        lane1 = lax.broadcasted_iota(jnp.int32, (1, C), 1)
        dq_t = jnp.zeros((C, C), F32)
        dk_t = jnp.zeros((C, C), F32)
        for dl in range(SUB):
            k_s = pltpu.roll(k_t, dl, 1) if dl else k_t
            b_s = pltpu.roll(b_t, dl, 1) if dl else b_t
            e = jnp.exp(jnp.minimum(b_t - b_s, 0.0))
            dband = jnp.sum(jnp.where(sub == lane - dl, da_t, 0.0), axis=0, keepdims=True)
            w = jnp.where(lane1 >= dl, dband, 0.0) * e
            dq_t = dq_t + w * k_s
            back = w * q_t
            dk_t = dk_t + (pltpu.roll(back, C - dl, 1) if dl else back)

        dq = eb * lax.dot_general(do_bf, s0_bf, _NT, preferred_element_type=F32) + jnp.concatenate(dq_slabs, axis=0) + dq_t.T
        dk_inter = e_last * lax.dot_general(v_bf, ds1_bf, _NT, preferred_element_type=F32)
        dk = dk_inter + jnp.concatenate(dk_slabs, axis=0) + dk_t.T

        extra = (jnp.exp(b_last) * jnp.sum((s0 * ds1).T, axis=0, keepdims=True)
                 + jnp.sum(k * dk_inter, axis=0, keepdims=True))
        db = q * dq - k * dk + jnp.where(sub == C - 1, jnp.broadcast_to(extra, (C, C)), 0.0)
        tri_t = jnp.where(lane >= sub, 1.0, 0.0).astype(F32)
        dg = jnp.dot(tri_t, db, precision=lax.Precision.HIGHEST, preferred_element_type=F32)
        ds_sc[...] = (jnp.exp(_last_row_as_col(b_t)) * ds1
                      + lax.dot_general((q * eb).astype(BF16), do_bf, _TN, preferred_element_type=F32))

        df = jnp.where(valid, dg / f - dk, 0.0)
        dhf_ref[...] = (df * (1.0 - lb_row) * sg * (1.0 - sg)).astype(BF16)
        dlb_ref[...] += jnp.sum(df * (1.0 - sg), axis=0, keepdims=True)
        dhq_ref[...] = (dq * _dsilu(hq)).astype(BF16)
        dhi_ref[...] = dv.astype(BF16)

    seg = lambda base: (lambda h, s: (nc - 1 - s, base // C + h))
    rmap = lambda h, s: (nc - 1 - s, h)
    return pl.pallas_call(
        body,
        name="hgrn_bwd",
        out_shape=[jax.ShapeDtypeStruct((rows, D_MODEL), BF16)] * 3 + [jax.ShapeDtypeStruct((1, D_MODEL), F32)],
        grid=(HEADS, nc),
        in_specs=[pl.BlockSpec((C, C), seg(SEG_HQ)), pl.BlockSpec((C, C), seg(SEG_HF)), pl.BlockSpec((C, C), seg(SEG_HI)),
                  pl.BlockSpec((1, C), lambda h, s: (0, h)),
                  pl.BlockSpec((1, C, C), lambda h, s: (h, nc - 1 - s, 0)),
                  pl.BlockSpec((1, 1, C, C), lambda h, s: (h, nc - 1 - s, 0, 0)),
                  pl.BlockSpec((C, C), rmap)],
        out_specs=[pl.BlockSpec((C, C), rmap)] * 3 + [pl.BlockSpec((1, C), lambda h, s: (0, h))],
        scratch_shapes=[pltpu.VMEM((C, C), F32), pltpu.VMEM((C, C), F32)],
        compiler_params=_params(("parallel", "arbitrary")),
    )(proj, proj, proj, lb, a_mat, s_states, do_h)


CONV_TC = 512
HALO = 8


def _conv_taps(i, tm, g_ref, pg_ref):
    shape = g_ref.shape
    r = lax.broadcasted_iota(jnp.int32, shape, 0)
    g = jnp.where(i * tm + r >= PAD_LEN, g_ref[...], 0.0)
    p1 = jnp.where(i * tm - 1 >= PAD_LEN, pg_ref[HALO - 1:HALO, :], 0.0)
    p2 = jnp.where(i * tm - 2 >= PAD_LEN, pg_ref[HALO - 2:HALO - 1, :], 0.0)
    s1 = jnp.where(r == 0, p1, pltpu.roll(g, 1, 0))
    s2 = jnp.where(r == 0, p2, jnp.where(r == 1, p1, pltpu.roll(g, 2, 0)))
    return g, s1, s2


def _conv_specs(tm, tc, ncb, order):
    gate = pl.BlockSpec((tm, tc), lambda *ids: order(ids))
    halo = pl.BlockSpec((HALO, tc), lambda *ids: (jnp.maximum(order(ids)[0] * (tm // HALO) - 1, 0), order(ids)[1]))
    up = pl.BlockSpec((tm, tc), lambda *ids: (order(ids)[0], ncb + order(ids)[1]))
    return gate, halo, up


def _conv_fwd(ffn, conv_w, conv_b, rows, tm):
    tc = CONV_TC
    ncb = D_FF // tc

    def body(g_ref, pg_ref, up_ref, cw_ref, cb_ref, act_ref):
        i = pl.program_id(0)
        g, s1, s2 = _conv_taps(i, tm, g_ref, pg_ref)
        conv = (cw_ref[0:1, :] * s2 + cw_ref[1:2, :] * s1 + cw_ref[2:3, :] * g) + cb_ref[...]
        act_ref[...] = (_silu(conv) * up_ref[...]).astype(BF16)

    gate, halo, up = _conv_specs(tm, tc, ncb, lambda ids: (ids[0], ids[1]))
    return pl.pallas_call(
        body,
        name="conv_fwd",
        out_shape=jax.ShapeDtypeStruct((rows, D_FF), BF16),
        grid=(rows // tm, ncb),
        in_specs=[gate, halo, up, pl.BlockSpec((3, tc), lambda i, j: (0, j)), pl.BlockSpec((1, tc), lambda i, j: (0, j))],
        out_specs=pl.BlockSpec((tm, tc), lambda i, j: (i, j)),
        compiler_params=_params(("parallel", "parallel")),
    )(ffn, ffn, ffn, conv_w, conv_b)


def _conv_bwd_a(ffn, dact, conv_w, conv_b, rows, tm):
    tc = CONV_TC
    ncb = D_FF // tc

    def body(g_ref, pg_ref, up_ref, da_ref, cw_ref, cb_ref, dc_ref, dffn_ref, w0_ref, w1_ref, w2_ref, db_ref):
        i = pl.program_id(1)
        g, s1, s2 = _conv_taps(i, tm, g_ref, pg_ref)
        conv = (cw_ref[0:1, :] * s2 + cw_ref[1:2, :] * s1 + cw_ref[2:3, :] * g) + cb_ref[...]
        da = da_ref[...]
        dffn_ref[...] = (da * _silu(conv)).astype(BF16)
        dc = da * up_ref[...] * _dsilu(conv)
        dc_ref[...] = dc
        sums = [jnp.sum(dc * s2, axis=0, keepdims=True), jnp.sum(dc * s1, axis=0, keepdims=True),
                jnp.sum(dc * g, axis=0, keepdims=True), jnp.sum(dc, axis=0, keepdims=True)]
        for ref, val in zip((w0_ref, w1_ref, w2_ref, db_ref), sums):
            @pl.when(i == 0)
            def _(ref=ref, val=val):
                ref[...] = val

            @pl.when(i > 0)
            def _(ref=ref, val=val):
                ref[...] += val

    gate, halo, up = _conv_specs(tm, tc, ncb, lambda ids: (ids[1], ids[0]))
    col = pl.BlockSpec((1, tc), lambda j, i: (0, j))
    return pl.pallas_call(
        body,
        name="conv_bwd_a",
        out_shape=[jax.ShapeDtypeStruct((rows, D_FF), F32), jax.ShapeDtypeStruct((rows, 2 * D_FF), BF16)]
        + [jax.ShapeDtypeStruct((1, D_FF), F32)] * 4,
        grid=(ncb, rows // tm),
        in_specs=[gate, halo, up, pl.BlockSpec((tm, tc), lambda j, i: (i, j)),
                  pl.BlockSpec((3, tc), lambda j, i: (0, j)), col],
        out_specs=[pl.BlockSpec((tm, tc), lambda j, i: (i, j)), pl.BlockSpec((tm, tc), lambda j, i: (i, ncb + j)),
                   col, col, col, col],
        compiler_params=_params(("parallel", "arbitrary")),
    )(ffn, ffn, ffn, dact, conv_w, conv_b)


def _conv_bwd_b(dconv, conv_w, dffn, rows, tm):
    tc = CONV_TC
    ncb = D_FF // tc
    nrb = rows // tm

    def body(dc_ref, nx_ref, cw_ref, dffn_in, out_ref):
        del dffn_in
        i = pl.program_id(0)
        dc = dc_ref[...]
        r = lax.broadcasted_iota(jnp.int32, dc.shape, 0)
        last = i == nrb - 1
        x1 = jnp.where(last, 0.0, nx_ref[0:1, :])
        x2 = jnp.where(last, 0.0, nx_ref[1:2, :])
        n1 = jnp.where(r == tm - 1, x1, pltpu.roll(dc, tm - 1, 0))
        n2 = jnp.where(r == tm - 1, x2, jnp.where(r == tm - 2, x1, pltpu.roll(dc, tm - 2, 0)))
        dg = cw_ref[2:3, :] * dc + cw_ref[1:2, :] * n1 + cw_ref[0:1, :] * n2
        out_ref[...] = jnp.where(i * tm + r >= PAD_LEN, dg, 0.0).astype(BF16)

    return pl.pallas_call(
        body,
        name="conv_bwd_b",
        out_shape=jax.ShapeDtypeStruct((rows, 2 * D_FF), BF16),
        grid=(nrb, ncb),
        in_specs=[pl.BlockSpec((tm, tc), lambda i, j: (i, j)),
                  pl.BlockSpec((HALO, tc), lambda i, j: (jnp.minimum((i + 1) * (tm // HALO), rows // HALO - 1), j)),
                  pl.BlockSpec((3, tc), lambda i, j: (0, j)),
                  pl.BlockSpec(memory_space=pl.ANY)],
        out_specs=pl.BlockSpec((tm, tc), lambda i, j: (i, j)),
        input_output_aliases={3: 0},
        compiler_params=_params(("parallel", "parallel")),
    )(dconv, dconv, conv_w, dffn)


def _final_call(h1, y, target, g_final, rows):
    tm = BLOCK

    def fn(i, tm_, h1v, yv, tgt, g):
        h2 = h1v + yv
        out = _rms_fwd(h2, g)
        err = jnp.where(i > 0, out - tgt, 0.0)
        loss = 0.5 * jnp.sum(jnp.mean(err * err, axis=-1, keepdims=True), axis=0, keepdims=True)
        dx, dg = _rms_bwd(h2, g, err * (1.0 / D_MODEL))
        return dx, dx, jnp.broadcast_to(loss, (1, LANE)), dg

    n_in = 4
    in_specs = [pl.BlockSpec((tm, D_MODEL), lambda i: (i, 0)), pl.BlockSpec((tm, D_MODEL), lambda i: (i, 0)),
                pl.BlockSpec((tm, D_MODEL), lambda i: (jnp.maximum(i - 1, 0), 0)),
                pl.BlockSpec((1, D_MODEL), lambda i: (0, 0))]

    def body(*refs):
        i = pl.program_id(0)
        dx, dx2, loss, dg = fn(i, tm, *[r[...] for r in refs[:n_in]])
        refs[4][...] = dx
        refs[5][...] = dx2.astype(BF16)
        for ref, val in ((refs[6], loss), (refs[7], dg)):
            @pl.when(i == 0)
            def _(ref=ref, val=val):
                ref[...] = val

            @pl.when(i > 0)
            def _(ref=ref, val=val):
                ref[...] += val

    return pl.pallas_call(
        body,
        name="final_loss",
        out_shape=[jax.ShapeDtypeStruct((rows, D_MODEL), F32), jax.ShapeDtypeStruct((rows, D_MODEL), BF16),
                   jax.ShapeDtypeStruct((1, LANE), F32), jax.ShapeDtypeStruct((1, D_MODEL), F32)],
        grid=(rows // tm,),
        in_specs=in_specs,
        out_specs=[pl.BlockSpec((tm, D_MODEL), lambda i: (i, 0)), pl.BlockSpec((tm, D_MODEL), lambda i: (i, 0)),
                   pl.BlockSpec((1, LANE), lambda i: (0, 0)), pl.BlockSpec((1, D_MODEL), lambda i: (0, 0))],
        compiler_params=_params(("arbitrary",)),
    )(h1, y, target, g_final)


def _heads_map(fn, *slabs):
    outs = [fn(*[s[:, h * LANE:(h + 1) * LANE] for s in slabs]) for h in range(HEADS)]
    if isinstance(outs[0], tuple):
        return tuple(jnp.concatenate([o[k] for o in outs], axis=1) for k in range(len(outs[0])))
    return jnp.concatenate(outs, axis=1)


def _local_step(x, positions, target, w, p, emit=None):
    kept = {}
    if emit is None:
        def emit(group):
            kept.update(group)
            return None
    s_len = x.shape[0]
    rows = s_len + BLOCK
    tm = _tile(rows, 640, 8)
    row = lambda arr, width, cb=0: ("row", arr, width, cb)

    h0 = jnp.concatenate([jnp.zeros((PAD_LEN, D_MODEL), F32), w["meta_tokens"], x], axis=0)
    pos = jnp.concatenate([jnp.zeros((PAD_LEN,), jnp.int32), jnp.arange(N_META, dtype=jnp.int32),
                           positions.astype(jnp.int32) + N_META])
    inv = 1.0 / (ROPE_THETA ** (jnp.arange(0, ROPE, 2, dtype=F32) / ROPE))
    ang = pos.astype(F32)[:, None] * inv
    zero = jnp.zeros((rows, LANE - ROPE), F32)
    cos_t = jnp.concatenate([jnp.cos(ang), jnp.cos(ang), zero], axis=1)
    sin_t = jnp.concatenate([-jnp.sin(ang), jnp.sin(ang), zero], axis=1)
    lb_r0, lb_r1 = p["lb_raw"][0:1], p["lb_raw"][1:2]

    def lb_fn(i, tm_, r0, r1):
        m = jnp.maximum(r0, r1)
        e0, e1 = jnp.exp(r0 - m), jnp.exp(r1 - m)
        return (e0 / (e0 + e1),)

    (lb,) = _rowwise("lb_fwd", lb_fn, [("bc", lb_r0), ("bc", lb_r1)], [("acc", (1, D_MODEL))], 1, 1)

    (u1,) = _rowwise("mix_norm", lambda i, t, h, g: (_rms_fwd(h, g),),
                     [row(h0, D_MODEL), ("bc", p["g_mix_norm"])], [("row", D_MODEL, BF16)], rows, tm)
    proj = _matmul(u1, w["w_in"], "nn", F32, "mm_proj")
    hint = getattr(w, "hint", lambda name, after: None)
    hint("w_q_up", proj)
    qn, kvn = _rowwise(
        "latent_norm", lambda i, t, ql, kl, gq, gk: (_rms_fwd(ql, gq), _rms_fwd(kl, gk)),
        [row(proj, Q_LORA, 0), row(proj, KV_LORA, SEG_KV_LAT // KV_LORA), ("bc", p["g_q_norm"]), ("bc", p["g_kv_norm"])],
        [("row", Q_LORA, BF16), ("row", KV_LORA, BF16)], rows, tm)
    q_raw = _matmul(qn, w["w_q_up"], "nn", F32, "mm_q_up")
    kv = _matmul(kvn, w["w_kv_up"], "nn", F32, "mm_kv_up")
    q_att, k_att, v_att = _rope_fwd_call(q_raw, kv, proj, cos_t, sin_t, rows, tm)
    o32, o_bf, lse = _attn_fwd(q_att, k_att, v_att, rows)
    hint("w_branch_mla", lse)
    o_h, a_mat, s_states = _hgrn_fwd(proj, lb, rows)

    def hgrn_post(i, t, oh, hg, g):
        return (_heads_map(lambda a, b: _rms_fwd(a, g) * _silu(b), oh, hg),)

    (o_hgrn,) = _rowwise("hgrn_post", hgrn_post,
                         [row(o_h, D_MODEL), row(proj, D_MODEL, SEG_HG // D_MODEL), ("bc", p["g_hgrn_norm"])],
                         [("row", D_MODEL, BF16)], rows, tm)
    br_a = _matmul(o_bf, w["w_branch_mla"], "nn", F32, "mm_branch_mla")
    br_b = _matmul(o_hgrn, w["w_branch_hgrn"], "nn", F32, "mm_branch_hgrn")
    (merged,) = _rowwise(
        "merge", lambda i, t, a, b, ga, gb: (_sigmoid(ga) * a + _sigmoid(gb) * b,),
        [row(br_a, D_MODEL), row(br_b, D_MODEL), row(proj, D_MODEL, SEG_GA // D_MODEL), row(proj, D_MODEL, SEG_GB // D_MODEL)],
        [("row", D_MODEL, BF16)], rows, tm)
    mix_out = _matmul(merged, w["w_out"], "nn", F32, "mm_out")

    def ffn_norm(i, t, h, mo, g):
        h1v = h + mo
        return h1v, _rms_fwd(h1v, g)

    h1, u2 = _rowwise("ffn_norm", ffn_norm, [row(h0, D_MODEL), row(mix_out, D_MODEL), ("bc", p["g_ffn_norm"])],
                      [("row", D_MODEL, F32), ("row", D_MODEL, BF16)], rows, tm)
    ffn = _matmul(u2, w["w_ffn_in"], "nn", F32, "mm_ffn_in")
    act = _conv_fwd(ffn, w["conv_w"], p["conv_b"], rows, tm)
    y = _matmul(act, w["w_ffn_out"], "nn", F32, "mm_ffn_out")
    dh2, dh2_bf, loss_acc, dg_final = _final_call(h1, y, target, p["g_final_norm"].reshape(1, D_MODEL), rows)

    grads = {"g_final_norm": dg_final.reshape(D_MODEL)}
    dact = _matmul(dh2_bf, w["w_ffn_out"], "nt", F32, "mm_d_act")
    grads["w_ffn_out"] = _matmul(act, dh2_bf, "tn", BF16, "mm_dw_ffn_out")
    dconv, dffn, dcw0, dcw1, dcw2, dcb = _conv_bwd_a(ffn, dact, w["conv_w"], p["conv_b"], rows, tm)
    dffn = _conv_bwd_b(dconv, w["conv_w"], dffn, rows, tm)
    grads["conv_w"] = jnp.concatenate([dcw0, dcw1, dcw2], axis=0)
    grads["conv_b"] = dcb
    du2 = _matmul(dffn, w["w_ffn_in"], "nt", F32, "mm_d_u2")
    grads["w_ffn_in"] = _matmul(u2, dffn, "tn", BF16, "mm_dw_ffn_in")

    def ffn_norm_bwd(i, t, h, du, dh, g):
        dx, dg = _rms_bwd(h, g, du)
        dh1v = dh + dx
        return dh1v, dh1v, dg

    dh1, dh1_bf, grads["g_ffn_norm"] = _rowwise(
        "ffn_norm_bwd", ffn_norm_bwd, [row(h1, D_MODEL), row(du2, D_MODEL), row(dh2, D_MODEL), ("bc", p["g_ffn_norm"])],
        [("row", D_MODEL, F32), ("row", D_MODEL, BF16), ("acc", (1, D_MODEL))], rows, tm)
    tok = emit({n: grads.pop(n) for n in ("w_ffn_out", "w_ffn_in", "conv_w", "conv_b", "g_final_norm", "g_ffn_norm")})
    dmerged = _matmul(dh1_bf, w["w_out"], "nt", F32, "mm_d_merged", after=tok)
    grads["w_out"] = _matmul(merged, dh1_bf, "tn", BF16, "mm_dw_out")

    def merge_bwd(i, t, dm, a, b, ga, gb):
        sa, sb = _sigmoid(ga), _sigmoid(gb)
        return dm * sa, dm * sb, dm * a * sa * (1.0 - sa), dm * b * sb * (1.0 - sb)

    da_bf, db_bf, dga, dgb = _rowwise(
        "merge_bwd", merge_bwd,
        [row(dmerged, D_MODEL), row(br_a, D_MODEL), row(br_b, D_MODEL),
         row(proj, D_MODEL, SEG_GA // D_MODEL), row(proj, D_MODEL, SEG_GB // D_MODEL)],
        [("row", D_MODEL, BF16)] * 4, rows, tm)
    do_mla = _matmul(da_bf, w["w_branch_mla"], "nt", BF16, "mm_d_o_mla")
    grads["w_branch_mla"] = _matmul(o_bf, da_bf, "tn", BF16, "mm_dw_branch_mla")
    do_hgrn = _matmul(db_bf, w["w_branch_hgrn"], "nt", F32, "mm_d_o_hgrn")
    grads["w_branch_hgrn"] = _matmul(o_hgrn, db_bf, "tn", BF16, "mm_dw_branch_hgrn")

    def hgrn_post_bwd(i, t, dy, oh, hg, g):
        def one(dyh, ohh, hgh):
            dx, dg = _rms_bwd(ohh, g, dyh * _silu(hgh))
            return dx, dyh * _rms_fwd(ohh, g) * _dsilu(hgh), dg

        dx, dhg, dg = _heads_map(one, dy, oh, hg)
        dg_sum = dg[:, 0:LANE]
        for h in range(1, HEADS):
            dg_sum = dg_sum + dg[:, h * LANE:(h + 1) * LANE]
        return dx, dhg, dg_sum

    tok = emit({n: grads.pop(n) for n in ("w_out", "w_branch_mla", "w_branch_hgrn")})
    do_h, dhg, grads["g_hgrn_norm"] = _rowwise(
        "hgrn_post_bwd", hgrn_post_bwd,
        [row(do_hgrn, D_MODEL), row(o_h, D_MODEL), row(proj, D_MODEL, SEG_HG // D_MODEL), ("bc", p["g_hgrn_norm"])],
        [("row", D_MODEL, F32), ("row", D_MODEL, BF16), ("acc", (1, LANE))], rows, tm, after=tok)
    dhq, dhf, dhi, dlb = _hgrn_bwd(proj, lb, a_mat, s_states, do_h, rows)

    def lb_bwd(i, tm_, d, l):
        t = d * l * (1.0 - l)
        return t, -t

    dlb0, dlb1 = _rowwise("lb_bwd", lb_bwd, [("bc", dlb), ("bc", lb)], [("acc", (1, D_MODEL))] * 2, 1, 1)
    grads["lb_raw"] = jnp.concatenate([dlb0, dlb1], axis=0)

    dq_att, delta = _attn_bwd_dq(q_att, k_att, v_att, do_mla, o32, lse, rows)
    dk_att, dv_att = _attn_bwd_dkv(q_att, k_att, v_att, do_mla, lse.reshape(HEADS, 1, rows),
                                   delta.reshape(HEADS, 1, rows), rows)
    dq_full, dkv, dkr = _rope_bwd_call(dq_att, dk_att, dv_att, cos_t, sin_t, rows, tm)
    dqn = _matmul(dq_full, w["w_q_up"], "nt", F32, "mm_d_qn")
    grads["w_q_up"] = _matmul(qn, dq_full, "tn", BF16, "mm_dw_q_up")
    dkvn = _matmul(dkv, w["w_kv_up"], "nt", F32, "mm_d_kvn")
    grads["w_kv_up"] = _matmul(kvn, dkv, "tn", BF16, "mm_dw_kv_up")

    def latent_norm_bwd(i, t, ql, kl, dq, dk, gq, gk):
        dql, dgq = _rms_bwd(ql, gq, dq)
        dkl, dgk = _rms_bwd(kl, gk, dk)
        return dql, dkl, dgq, dgk

    dq_lat, dkv_lat, grads["g_q_norm"], grads["g_kv_norm"] = _rowwise(
        "latent_norm_bwd", latent_norm_bwd,
        [row(proj, Q_LORA, 0), row(proj, KV_LORA, SEG_KV_LAT // KV_LORA), row(dqn, Q_LORA), row(dkvn, KV_LORA),
         ("bc", p["g_q_norm"]), ("bc", p["g_kv_norm"])],
        [("row", Q_LORA, BF16), ("row", KV_LORA, BF16), ("acc", (1, Q_LORA)), ("acc", (1, KV_LORA))], rows, tm)
    dproj = jnp.concatenate([dq_lat, dkv_lat, dhq, dhf, dhi, dhg, dga, dgb, dkr], axis=1)
    grads["w_in"] = _matmul(u1, dproj, "tn", BF16, "mm_dw_in")
    tok = emit({n: grads.pop(n) for n in ("w_in", "w_q_up", "w_kv_up", "lb_raw", "g_q_norm", "g_kv_norm", "g_hgrn_norm")})
    du1 = _matmul(dproj, w["w_in"], "nt", F32, "mm_d_u1", after=tok)

    def mix_norm_bwd(i, t, h, du, dh, g):
        dx, dg = _rms_bwd(h, g, du)
        return dh + dx, dg

    dh0, grads["g_mix_norm"] = _rowwise(
        "mix_norm_bwd", mix_norm_bwd, [row(h0, D_MODEL), row(du1, D_MODEL), row(dh1, D_MODEL), ("bc", p["g_mix_norm"])],
        [("row", D_MODEL, F32), ("acc", (1, D_MODEL))], rows, tm)
    grads["meta_tokens"] = dh0[PAD_LEN:BLOCK]
    kept.update(grads)
    return loss_acc[0, 0], dh0[BLOCK:], kept


K_ROPE_AT = Q_LORA + KV_LORA
COL_SHARDED = ("w_in", "w_q_up", "w_kv_up", "w_ffn_in", "conv_w", "meta_tokens")
ROW_SHARDED = ("w_branch_mla", "w_branch_hgrn", "w_out", "w_ffn_out")
BIG = ("w_in", "w_q_up", "w_kv_up", "w_branch_mla", "w_branch_hgrn", "w_out", "w_ffn_in", "w_ffn_out")
SMALL = ("conv_b", "g_mix_norm", "g_q_norm", "g_kv_norm", "g_hgrn_norm", "g_ffn_norm", "g_final_norm", "lb_raw")


def _unshard(name, stacked):
    if name in COL_SHARDED:
        return jnp.transpose(stacked, (1, 0, 2)).reshape(stacked.shape[1], N_DEV * stacked.shape[2])
    return stacked.reshape(N_DEV * stacked.shape[1], stacked.shape[2])


def _reshard(name, full):
    if name in COL_SHARDED:
        r, c = full.shape
        return jnp.transpose(full.reshape(r, N_DEV, c // N_DEV), (1, 0, 2))
    return full.reshape(N_DEV, full.shape[0] // N_DEV, full.shape[1])


def _to_kernel_layout(full):
    out = dict(full)
    if "w_in" in full:
        w_in = full["w_in"]
        pad = jnp.zeros((D_MODEL, KR_W - ROPE), w_in.dtype)
        out["w_in"] = jnp.concatenate(
            [w_in[:, :K_ROPE_AT], w_in[:, K_ROPE_AT + ROPE:], w_in[:, K_ROPE_AT:K_ROPE_AT + ROPE], pad], axis=1)
    if "w_q_up" in full:
        wq = full["w_q_up"].reshape(Q_LORA, HEADS, NOPE + ROPE)
        out["w_q_up"] = jnp.pad(wq, ((0, 0), (0, 0), (0, QHEAD_W - NOPE - ROPE))).reshape(Q_LORA, HEADS * QHEAD_W)
    return out


def _from_kernel_layout(grads):
    out = dict(grads)
    if "w_in" in grads:
        g = grads["w_in"]
        out["w_in"] = jnp.concatenate([g[:, :K_ROPE_AT], g[:, SEG_KR:SEG_KR + ROPE], g[:, K_ROPE_AT:SEG_KR]], axis=1)
    if "w_q_up" in grads:
        g = grads["w_q_up"].reshape(Q_LORA, HEADS, QHEAD_W)
        out["w_q_up"] = g[:, :, :NOPE + ROPE].reshape(Q_LORA, HEADS * (NOPE + ROPE))
    return out


MESH_ID = pl.DeviceIdType.MESH
ANY = pl.BlockSpec(memory_space=pl.ANY)


def _slot(dev):
    return 4 * dev[0] + 2 * dev[1] + dev[2]


def _all_gather(shards):
    n = len(shards)

    def body(*refs):
        ins, outs = refs[:n], refs[n:2 * n]
        send_sems, recv_sems, local_sems = refs[2 * n:]
        x, y, c = lax.axis_index("x"), lax.axis_index("y"), lax.axis_index("c")
        me, sibling = (x, y, c), (x, y, 1 - c)
        chips = [(1 - x, y), (x, 1 - y), (1 - x, 1 - y)]

        def copy(a, k, block, to, src=None):
            dst = outs[a].at[_slot(block)]
            return pltpu.make_async_remote_copy(
                src_ref=dst if src is None else src, dst_ref=dst, send_sem=send_sems.at[a, k],
                recv_sem=recv_sems.at[a, k], device_id=to, device_id_type=MESH_ID)

        mine = [pltpu.make_async_copy(ins[a], outs[a].at[_slot(me)], local_sems.at[a]) for a in range(n)]
        for cp in mine:
            cp.start()
        first = []
        for a in range(n):
            first.append(copy(a, 0, me, sibling, src=ins[a]))
            first += [copy(a, 1 + j, me, (*chip, c), src=ins[a]) for j, chip in enumerate(chips)]
        for cp in first:
            cp.start()
        passed = []
        for a in range(n):
            for j, chip in enumerate(chips):
                copy(a, 1 + j, (*chip, c), me).wait_recv()
                fwd = copy(a, 4 + j, (*chip, c), sibling)
                fwd.start()
                passed.append(fwd)
        for a in range(n):
            copy(a, 0, sibling, me).wait_recv()
            for j, chip in enumerate(chips):
                copy(a, 4 + j, (*chip, 1 - c), me).wait_recv()
        for cp in first + passed:
            cp.wait_send()
        for cp in mine:
            cp.wait()

    return pl.pallas_call(
        body,
        name="gather_weights",
        out_shape=[jax.ShapeDtypeStruct((N_DEV,) + s.shape, s.dtype) for s in shards],
        in_specs=[ANY] * n,
        out_specs=[ANY] * n,
        scratch_shapes=[pltpu.SemaphoreType.DMA((n, 7)), pltpu.SemaphoreType.DMA((n, 7)), pltpu.SemaphoreType.DMA((n,))],
    )(*shards)


def _exchange(blocked, replicated):
    nb, n = len(blocked), len(blocked) + len(replicated)
    arrays = list(blocked) + list(replicated)

    def body(*refs):
        ins, outs = refs[:n], refs[n:2 * n]
        send_sems, recv_sems, local_sems = refs[2 * n:]
        x, y, c = lax.axis_index("x"), lax.axis_index("y"), lax.axis_index("c")
        me = (x, y, c)
        peers = [(x, y, 1 - c), (1 - x, y, c), (x, 1 - y, c), (1 - x, 1 - y, c),
                 (1 - x, y, 1 - c), (x, 1 - y, 1 - c), (1 - x, 1 - y, 1 - c)]

        def src_of(a, dev):
            return ins[a].at[_slot(dev)] if a < nb else ins[a]

        def copy(a, k, frm, to):
            return pltpu.make_async_remote_copy(
                src_ref=src_of(a, to), dst_ref=outs[a].at[_slot(frm)], send_sem=send_sems.at[a, k],
                recv_sem=recv_sems.at[a, k], device_id=to, device_id_type=MESH_ID)

        mine = [pltpu.make_async_copy(src_of(a, me), outs[a].at[_slot(me)], local_sems.at[a]) for a in range(n)]
        for cp in mine:
            cp.start()
        sends = [copy(a, k, me, peer) for a in range(n) for k, peer in enumerate(peers)]
        for cp in sends:
            cp.start()
        for a in range(n):
            for k, peer in enumerate(peers):
                copy(a, k, peer, me).wait_recv()
        for cp in sends:
            cp.wait_send()
        for cp in mine:
            cp.wait()

    return pl.pallas_call(
        body,
        name="exchange_grads",
        out_shape=[jax.ShapeDtypeStruct(s.shape, s.dtype) for s in blocked]
        + [jax.ShapeDtypeStruct((N_DEV,) + s.shape, s.dtype) for s in replicated],
        in_specs=[ANY] * n,
        out_specs=[ANY] * n,
        scratch_shapes=[pltpu.SemaphoreType.DMA((n, 7)), pltpu.SemaphoreType.DMA((n, 7)), pltpu.SemaphoreType.DMA((n,))],
    )(*arrays)


ADAMW_BLOCK_ELEMS = 256 * 1024


def _adamw(name, parts, w, m, v):
    r, c = w.shape
    tr = _tile(r, max(16, ADAMW_BLOCK_ELEMS // c), 16)

    def body(p_ref, w_ref, m_ref, v_ref, g_ref, d_ref, nm_ref, nv_ref):
        g = p_ref[0].astype(F32)
        for s in range(1, N_DEV):
            g = g + p_ref[s].astype(F32)
        m_new = ADAM_B1 * m_ref[...] + (1.0 - ADAM_B1) * g
        v_new = ADAM_B2 * v_ref[...] + (1.0 - ADAM_B2) * (g * g)
        m_hat = m_new / (1.0 - ADAM_B1 ** ADAM_STEP)
        v_hat = v_new / (1.0 - ADAM_B2 ** ADAM_STEP)
        g_ref[...] = g
        d_ref[...] = -ADAM_LR * (m_hat / (jnp.sqrt(v_hat) + ADAM_EPS) + ADAM_WD * w_ref[...])
        nm_ref[...] = m_new
        nv_ref[...] = v_new

    blk = pl.BlockSpec((tr, c), lambda i: (i, 0))
    return pl.pallas_call(
        body,
        name="adamw_" + name,
        out_shape=[jax.ShapeDtypeStruct((r, c), F32)] * 4,
        grid=(r // tr,),
        in_specs=[pl.BlockSpec((N_DEV, tr, c), lambda i: (0, i, 0)), blk, blk, blk],
        out_specs=[blk] * 4,
        compiler_params=_params(("parallel",)),
    )(parts, w, m, v)


HBM_SPEC = pl.BlockSpec(memory_space=pltpu.HBM)
SEM_SPEC = pl.BlockSpec(memory_space=pltpu.SEMAPHORE)
SIDE_EFFECT = pltpu.SideEffectType.DATAFLOW_SIDE_EFFECTING
N_PEERS = N_DEV - 1


def _peers(x, y, c):
    return [(x, y, 1 - c), (1 - x, y, c), (x, 1 - y, c), (1 - x, 1 - y, c),
            (1 - x, y, 1 - c), (x, 1 - y, 1 - c), (1 - x, 1 - y, 1 - c)]


def _split_copy(srcs, lands, blocked, send_sems, recv_sems, a, k, frm, to):
    src = srcs[a].at[_slot(to)] if blocked[a] else srcs[a]
    return pltpu.make_async_remote_copy(
        src_ref=src, dst_ref=lands[a].at[_slot(frm)], send_sem=send_sems.at[a * N_PEERS + k],
        recv_sem=recv_sems.at[a * N_PEERS + k],
        device_id=to, device_id_type=MESH_ID)


def _exchange_start(name, srcs, lands, blocked, after=()):
    n = len(srcs)
    after = list(after)

    def body(*refs):
        src_refs, land_refs = refs[:n], refs[n:2 * n]
        send_sems, recv_sems = refs[2 * n + len(after)], refs[2 * n + len(after) + 1]
        token = refs[-1]
        x, y, c = lax.axis_index("x"), lax.axis_index("y"), lax.axis_index("c")
        for a in range(n):
            for k, peer in enumerate(_peers(x, y, c)):
                _split_copy(src_refs, land_refs, blocked, send_sems, recv_sems, a, k, (x, y, c), peer).start()
        token[...] = jnp.zeros_like(token)

    thru = [pltpu.HBM(s.shape, s.dtype) for s in list(srcs) + list(lands)]
    res = pl.pallas_call(
        body,
        name=name,
        out_shape=(pltpu.SemaphoreType.DMA((n * N_PEERS,)), pltpu.SemaphoreType.DMA((n * N_PEERS,)), *thru,
                   jax.ShapeDtypeStruct((8, LANE), F32)),
        in_specs=[HBM_SPEC] * (2 * n) + [pl.BlockSpec(memory_space=pl.ANY)] * len(after),
        out_specs=(SEM_SPEC, SEM_SPEC, *([HBM_SPEC] * (2 * n)), pl.BlockSpec(memory_space=pltpu.VMEM)),
        input_output_aliases={i: 2 + i for i in range(2 * n)},
        compiler_params=pltpu.CompilerParams(has_side_effects=SIDE_EFFECT),
    )(*[pltpu.with_memory_space_constraint(s, pltpu.HBM) for s in list(srcs) + list(lands)], *after)
    return res[0], res[1], res[2:2 + n], res[2 + n:2 + 2 * n], res[-1]


def _exchange_wait(name, send_sems, recv_sems, srcs, lands, blocked, after):
    n, n_after = len(srcs), len(after)

    def body(*refs):
        src_refs, land_refs = refs[:n], refs[n:2 * n]
        send, recv = refs[2 * n], refs[2 * n + 1]
        x, y, c = lax.axis_index("x"), lax.axis_index("y"), lax.axis_index("c")
        for a in range(n):
            for k, peer in enumerate(_peers(x, y, c)):
                _split_copy(src_refs, land_refs, blocked, send, recv, a, k, (x, y, c), peer).wait_send()
                _split_copy(src_refs, land_refs, blocked, send, recv, a, k, peer, (x, y, c)).wait_recv()

    res = pl.pallas_call(
        body,
        name=name,
        out_shape=tuple(pltpu.HBM(s.shape, s.dtype) for s in list(srcs) + list(lands)),
        in_specs=[HBM_SPEC] * (2 * n) + [SEM_SPEC, SEM_SPEC] + [pl.BlockSpec(memory_space=pl.ANY)] * n_after,
        out_specs=tuple([HBM_SPEC] * (2 * n)),
        input_output_aliases={i: i for i in range(2 * n)},
        compiler_params=pltpu.CompilerParams(has_side_effects=SIDE_EFFECT),
    )(*srcs, *lands, send_sems, recv_sems, *after)
    return res[n:]


class _LazyWeights:
    def __init__(self):
        self.ready, self.groups, self.hints = {}, {}, {}

    def add_group(self, wait_name, names, send, recv, srcs, lands):
        for n in names:
            self.groups[n] = (wait_name, names, send, recv, srcs, lands)

    def hint(self, name, after):
        self.hints[self.groups[name][0]] = after

    def __getitem__(self, name):
        if name not in self.ready:
            wait_name, names, send, recv, srcs, lands = self.groups[name]
            after = [self.hints[wait_name]] if wait_name in self.hints else []
            whole = _exchange_wait(wait_name, send, recv, srcs, lands, [False] * len(names), after)
            for n, stacked in zip(names, whole):
                self.ready[n] = _to_kernel_layout({n: _unshard(n, stacked)})[n]
        return self.ready[name]


def kernel(x, positions, meta_tokens, w_in, w_q_up, w_kv_up, w_branch_mla, w_branch_hgrn, w_out, w_ffn_in, w_ffn_out, conv_w, conv_b, g_mix_norm, g_q_norm, g_kv_norm, g_hgrn_norm, g_ffn_norm, g_final_norm, lb_raw, loss_target, m_meta_tokens, m_w_in, m_w_q_up, m_w_kv_up, m_w_branch_mla, m_w_branch_hgrn, m_w_out, m_w_ffn_in, m_w_ffn_out, m_conv_w, m_conv_b, m_g_mix_norm, m_g_q_norm, m_g_kv_norm, m_g_hgrn_norm, m_g_ffn_norm, m_g_final_norm, m_lb_raw, v_meta_tokens, v_w_in, v_w_q_up, v_w_kv_up, v_w_branch_mla, v_w_branch_hgrn, v_w_out, v_w_ffn_in, v_w_ffn_out, v_conv_w, v_conv_b, v_g_mix_norm, v_g_q_norm, v_g_kv_norm, v_g_hgrn_norm, v_g_ffn_norm, v_g_final_norm, v_lb_raw):
    local = dict(zip(
        ("meta_tokens", "w_in", "w_q_up", "w_kv_up", "w_branch_mla", "w_branch_hgrn", "w_out", "w_ffn_in", "w_ffn_out",
         "conv_w", "conv_b", "g_mix_norm", "g_q_norm", "g_kv_norm", "g_hgrn_norm", "g_ffn_norm", "g_final_norm", "lb_raw"),
        (meta_tokens, w_in, w_q_up, w_kv_up, w_branch_mla, w_branch_hgrn, w_out, w_ffn_in, w_ffn_out,
         conv_w, conv_b, g_mix_norm, g_q_norm, g_kv_norm, g_hgrn_norm, g_ffn_norm, g_final_norm, lb_raw)))
    mom_m = dict(zip(local, (m_meta_tokens, m_w_in, m_w_q_up, m_w_kv_up, m_w_branch_mla, m_w_branch_hgrn, m_w_out, m_w_ffn_in,
                             m_w_ffn_out, m_conv_w, m_conv_b, m_g_mix_norm, m_g_q_norm, m_g_kv_norm, m_g_hgrn_norm,
                             m_g_ffn_norm, m_g_final_norm, m_lb_raw)))
    mom_v = dict(zip(local, (v_meta_tokens, v_w_in, v_w_q_up, v_w_kv_up, v_w_branch_mla, v_w_branch_hgrn, v_w_out, v_w_ffn_in,
                             v_w_ffn_out, v_conv_w, v_conv_b, v_g_mix_norm, v_g_q_norm, v_g_kv_norm, v_g_hgrn_norm,
                             v_g_ffn_norm, v_g_final_norm, v_lb_raw)))
    sharded = BIG + ("conv_w", "meta_tokens")

    def shard2d(name, arr):
        return arr.reshape(arr.shape[-2:]) if name != "meta_tokens" else arr

    def as2d(name, arr):
        return arr.reshape(1, -1) if arr.ndim == 1 else shard2d(name, arr)

    me = 4 * lax.axis_index("x") + 2 * lax.axis_index("y") + lax.axis_index("c")

    def landing(own):
        zone = lax.empty((N_DEV,) + own.shape[1:], own.dtype)
        return lax.dynamic_update_slice_in_dim(zone, own, me, 0)

    shards = {n: shard2d(n, local[n]).astype(BF16) for n in BIG}
    shards.update({n: shard2d(n, local[n]) for n in ("conv_w", "meta_tokens")})
    full = _LazyWeights()
    first = ("w_in", "meta_tokens")
    gathered = _all_gather([shards[n] for n in first])
    for n, g in zip(first, gathered):
        full.ready[n] = _to_kernel_layout({n: _unshard(n, g)})[n]
    later = (("w_q_up", "w_kv_up"), ("w_branch_mla", "w_branch_hgrn", "w_out", "w_ffn_in", "w_ffn_out", "conv_w"))
    for k, names in enumerate(later):
        srcs = [shards[n] for n in names]
        send, recv, srcs_thru, lands_thru, _ = _exchange_start(
            f"gather_start_{k}", srcs, [landing(s[None]) for s in srcs], [False] * len(names), after=[gathered[0]])
        full.add_group(f"gather_wait_{k}", names, send, recv, srcs_thru, lands_thru)
    small = {n: local[n] for n in SMALL}

    started = []

    def sources(group):
        group = _from_kernel_layout(group)
        names = list(group)
        blocked = [n in sharded for n in names]
        srcs = [_reshard(n, group[n]) if b else as2d(n, group[n]) for n, b in zip(names, blocked)]
        return names, blocked, srcs

    def emit(group):
        names, blocked, srcs = sources(group)
        lands = [landing(lax.dynamic_index_in_dim(s, me, 0, keepdims=True) if b else s[None]) for s, b in zip(srcs, blocked)]
        k = len(started)
        send, recv, srcs_thru, lands_thru, token = _exchange_start(f"exchange_start_{k}", srcs, lands, blocked)
        started.append((names, blocked, send, recv, srcs_thru, lands_thru))
        return token

    loss, grad_x, last = _local_step(x[0], positions[0], loss_target[0], full, small, emit)

    out = {}

    def update(names, parts):
        for n, part in zip(names, parts):
            res = _adamw(n, part, as2d(n, local[n]), as2d(n, mom_m[n]), as2d(n, mom_v[n]))
            out[n] = [r.reshape(local[n].shape) for r in res]

    after = [grad_x]
    for k, (names, blocked, send, recv, srcs_thru, lands_thru) in enumerate(started):
        update(names, _exchange_wait(f"exchange_wait_{k}", send, recv, srcs_thru, lands_thru, blocked, after))
        after = [out[names[0]][0]]
    names, blocked, srcs = sources(last)
    in_blocks = [(n, s) for n, s, b in zip(names, srcs, blocked) if b]
    whole = [(n, s) for n, s, b in zip(names, srcs, blocked) if not b]
    update([n for n, _ in in_blocks + whole], _exchange([s for _, s in in_blocks], [s for _, s in whole]))

    loss = lax.psum(loss, ("x", "y", "c"))
    order = tuple(local)
    return (loss, grad_x[None], *[out[n][0] for n in order], *[out[n][1] for n in order],
            *[out[n][2] for n in order], *[out[n][3] for n in order])
```

```python
import functools

import jax
import jax.numpy as jnp
import numpy as np
from jax import lax
from jax.experimental import pallas as pl
from jax.experimental.pallas import tpu as pltpu

F32 = jnp.float32
BF16 = jnp.bfloat16

D_MODEL = 2048
N_META = 16
BLOCK = 128
PAD_LEN = BLOCK - N_META
HEADS = 16
Q_LORA = 1536
KV_LORA = 512
ROPE = 64
NOPE = 128
VDIM = 128
D_FF = 5632
NORM_EPS = 1e-6
ROPE_THETA = 10000.0
ATTN_SCALE = (NOPE + ROPE) ** -0.5
ADAM_LR = 0.001
ADAM_B1 = 0.9
ADAM_B2 = 0.999
ADAM_EPS = 1e-08
ADAM_WD = 0.01
ADAM_STEP = 10
N_DEV = 8

LANE = 128
SEG_Q_LAT = 0
SEG_KV_LAT = Q_LORA
SEG_HQ = 2048
SEG_HF = SEG_HQ + D_MODEL
SEG_HI = SEG_HF + D_MODEL
SEG_HG = SEG_HI + D_MODEL
SEG_GA = SEG_HG + D_MODEL
SEG_GB = SEG_GA + D_MODEL
SEG_KR = SEG_GB + D_MODEL
KR_W = 256
PROJ_W = SEG_KR + KR_W
QHEAD_W = 256

V7X_VMEM_BYTES = 64 * 1024 * 1024
VMEM_LIMIT = V7X_VMEM_BYTES * 7 // 8
NEG_BIG = -1e30
SUB = 8


def _tile(n, target, mult):
    best = None
    for t in range(mult, min(n, target) + 1, mult):
        if n % t == 0:
            best = t
    return n if best is None else best


def _params(sem):
    return pltpu.CompilerParams(dimension_semantics=sem, vmem_limit_bytes=VMEM_LIMIT)


def _sigmoid(x):
    return 1.0 / (1.0 + jnp.exp(-x))


_DIMS = {"nn": (((1,), (0,)), ((), ())), "nt": (((1,), (1,)), ((), ())), "tn": (((0,), (0,)), ((), ()))}


def _matmul(a, b, mode, out_dtype, name, after=None):
    if mode == "nn":
        (m, k), (_, n) = a.shape, b.shape
    elif mode == "nt":
        (m, k), (n, _) = a.shape, b.shape
    else:
        (k, m), (_, n) = a.shape, b.shape
    tm = _tile(m, 1040, 8) if mode != "tn" else _tile(m, 1024, LANE)
    tn = _tile(n, 1024, LANE)
    tk = _tile(k, 2816, LANE) if mode != "tn" else _tile(k, 2080, 8)
    nk = k // tk
    if mode == "nn":
        a_spec = pl.BlockSpec((tm, tk), lambda i, j, kk: (i, kk))
        b_spec = pl.BlockSpec((tk, tn), lambda i, j, kk: (kk, j))
    elif mode == "nt":
        a_spec = pl.BlockSpec((tm, tk), lambda i, j, kk: (i, kk))
        b_spec = pl.BlockSpec((tn, tk), lambda i, j, kk: (j, kk))
    else:
        a_spec = pl.BlockSpec((tk, tm), lambda i, j, kk: (kk, i))
        b_spec = pl.BlockSpec((tk, tn), lambda i, j, kk: (kk, j))
    dims = _DIMS[mode]

    n_after = 0 if after is None else 1

    def body(a_ref, b_ref, *rest):
        o_ref, acc = rest[n_after], rest[n_after + 1:]
        part = lax.dot_general(a_ref[...], b_ref[...], dims, preferred_element_type=F32)
        if nk == 1:
            o_ref[...] = part.astype(o_ref.dtype)
            return
        acc_ref, kk = acc[0], pl.program_id(2)

        @pl.when(kk == 0)
        def _():
            acc_ref[...] = part

        @pl.when((kk > 0) & (kk < nk - 1))
        def _():
            acc_ref[...] += part

        @pl.when(kk == nk - 1)
        def _():
            o_ref[...] = (acc_ref[...] + part).astype(o_ref.dtype)

    return pl.pallas_call(
        body,
        name=name,
        out_shape=jax.ShapeDtypeStruct((m, n), out_dtype),
        grid=(m // tm, n // tn, nk),
        in_specs=[a_spec, b_spec] + [pl.BlockSpec(memory_space=pl.ANY)] * n_after,
        out_specs=pl.BlockSpec((tm, tn), lambda i, j, kk: (i, j)),
        scratch_shapes=[pltpu.VMEM((tm, tn), F32)] if nk > 1 else [],
        compiler_params=_params(("parallel", "parallel", "arbitrary")),
    )(a, b, *([after] * n_after))


ROW_WINDOW_BYTES = 12 * 1024 * 1024


def _rowwise(name, fn, ins, outs, rows, tm, after=None):
    per_row = sum(s[2] * s[1].dtype.itemsize for s in ins if s[0] == "row")
    per_row += sum(s[1] * jnp.dtype(s[2]).itemsize for s in outs if s[0] == "row")
    if per_row:
        tm = _tile(rows, min(tm, max(8, ROW_WINDOW_BYTES // (2 * per_row))), 8)
    n_in = len(ins)
    in_specs, args = [], []
    for spec in ins:
        if spec[0] == "row":
            _, arr, w, cb = spec
            in_specs.append(pl.BlockSpec((tm, w), functools.partial(lambda i, cb: (i, cb), cb=cb)))
        else:
            arr = spec[1]
            in_specs.append(pl.BlockSpec(arr.shape, lambda i: (0, 0)))
        args.append(arr)
    out_shape, out_specs = [], []
    for spec in outs:
        if spec[0] == "row":
            out_shape.append(jax.ShapeDtypeStruct((rows, spec[1]), spec[2]))
            out_specs.append(pl.BlockSpec((tm, spec[1]), lambda i: (i, 0)))
        else:
            out_shape.append(jax.ShapeDtypeStruct(spec[1], F32))
            out_specs.append(pl.BlockSpec(spec[1], lambda i: (0, 0)))
    has_acc = any(s[0] == "acc" for s in outs)
    n_after = 0 if after is None else 1
    in_specs += [pl.BlockSpec(memory_space=pl.ANY)] * n_after
    args += [after] * n_after

    def body(*refs):
        i = pl.program_id(0)
        res = fn(i, tm, *[r[...] for r in refs[:n_in]])
        for spec, ref, val in zip(outs, refs[n_in + n_after:], res):
            if spec[0] == "row":
                ref[...] = val.astype(ref.dtype)
            else:
                @pl.when(i == 0)
                def _(ref=ref, val=val):
                    ref[...] = val

                @pl.when(i > 0)
                def _(ref=ref, val=val):
                    ref[...] += val

    return pl.pallas_call(
        body,
        name=name,
        out_shape=out_shape,
        grid=(rows // tm,),
        in_specs=in_specs,
        out_specs=out_specs,
        compiler_params=_params(("arbitrary" if has_acc else "parallel",)),
    )(*args)


def _row_ids(i, tm, shape):
    return i * tm + lax.broadcasted_iota(jnp.int32, shape, 0)


def _rms_fwd(x, g):
    r = lax.rsqrt(jnp.mean(x * x, axis=-1, keepdims=True) + NORM_EPS)
    return x * r * g


def _rms_bwd(x, g, dy):
    r = lax.rsqrt(jnp.mean(x * x, axis=-1, keepdims=True) + NORM_EPS)
    xhat = x * r
    dxhat = dy * g
    dx = r * (dxhat - xhat * jnp.mean(dxhat * xhat, axis=-1, keepdims=True))
    return dx, jnp.sum(dy * xhat, axis=0, keepdims=True)


def _silu(x):
    return x * _sigmoid(x)


def _dsilu(x):
    s = _sigmoid(x)
    return s * (1.0 + x * (1.0 - s))


def _rot_src(x):
    lane = lax.broadcasted_iota(jnp.int32, x.shape, 1)
    return jnp.where(lane < ROPE // 2, pltpu.roll(x, LANE - ROPE // 2, 1), pltpu.roll(x, ROPE // 2, 1))


def _rope_fwd_call(q_raw, kv, proj, cos_t, sin_t, rows, tm):
    def fn(i, tm_, q, kvv, kr, c, s):
        kr_rot = kr[:, :LANE]
        kr_rot = kr_rot * c + _rot_src(kr_rot) * s
        qs, ks, vs = [], [], []
        for h in range(HEADS):
            qn = q[:, h * QHEAD_W:h * QHEAD_W + NOPE]
            qr = q[:, h * QHEAD_W + NOPE:(h + 1) * QHEAD_W]
            qs += [qn * SCORE_TO_LOG2, (qr * c + _rot_src(qr) * s) * SCORE_TO_LOG2]
            ks += [kvv[:, h * 2 * NOPE:h * 2 * NOPE + NOPE], kr_rot]
            vs += [kvv[:, h * 2 * NOPE + NOPE:(h + 1) * 2 * NOPE]]
        return jnp.concatenate(qs, axis=1), jnp.concatenate(ks, axis=1), jnp.concatenate(vs, axis=1)

    return _rowwise(
        "rope_fwd", fn,
        [("row", q_raw, HEADS * QHEAD_W, 0), ("row", kv, HEADS * 2 * NOPE, 0), ("row", proj, KR_W, SEG_KR // KR_W),
         ("row", cos_t, LANE, 0), ("row", sin_t, LANE, 0)],
        [("row", HEADS * QHEAD_W, BF16), ("row", HEADS * QHEAD_W, BF16), ("row", HEADS * VDIM, BF16)],
        rows, tm)


def _rope_bwd_call(dq_att, dk_att, dv, cos_t, sin_t, rows, tm):
    def fn(i, tm_, dq, dk, dvv, c, s):
        qs, kvs = [], []
        dkr = jnp.zeros((dq.shape[0], LANE), F32)
        for h in range(HEADS):
            dqr = dq[:, h * QHEAD_W + NOPE:(h + 1) * QHEAD_W]
            qs += [dq[:, h * QHEAD_W:h * QHEAD_W + NOPE], dqr * c - _rot_src(dqr) * s]
            kvs += [dk[:, h * QHEAD_W:h * QHEAD_W + NOPE], dvv[:, h * VDIM:(h + 1) * VDIM]]
            dkr = dkr + dk[:, h * QHEAD_W + NOPE:(h + 1) * QHEAD_W]
        dkr = dkr * c - _rot_src(dkr) * s
        return (jnp.concatenate(qs, axis=1), jnp.concatenate(kvs, axis=1),
                jnp.concatenate([dkr, jnp.zeros_like(dkr)], axis=1))

    return _rowwise(
        "rope_bwd", fn,
        [("row", dq_att, HEADS * QHEAD_W, 0), ("row", dk_att, HEADS * QHEAD_W, 0), ("row", dv, HEADS * VDIM, 0),
         ("row", cos_t, LANE, 0), ("row", sin_t, LANE, 0)],
        [("row", HEADS * QHEAD_W, BF16), ("row", HEADS * 2 * NOPE, BF16), ("row", KR_W, BF16)],
        rows, tm)


def _attn_mask(q_blk, k_blk, t, keys_on_rows=False):
    qa, ka = (1, 0) if keys_on_rows else (0, 1)
    qs = q_blk * t + lax.broadcasted_iota(jnp.int32, (t, t), qa)
    ks = k_blk * t + lax.broadcasted_iota(jnp.int32, (t, t), ka)
    return (ks <= qs) & ((ks >= PAD_LEN) | (ks == qs))


_NT = _DIMS["nt"]
_TN = _DIMS["tn"]
LOG2E = 1.4426950408889634
SCORE_TO_LOG2 = ATTN_SCALE * LOG2E


def _causal_pairs(nb, by_key):
    if by_key:
        pairs = [(qi, kj) for kj in range(nb) for qi in range(kj, nb)]
    else:
        pairs = [(qi, kj) for qi in range(nb) for kj in range(qi + 1)]
    return (jnp.asarray(np.array([p[0] for p in pairs], np.int32)), jnp.asarray(np.array([p[1] for p in pairs], np.int32)))


def _attn_fwd(q_att, k_att, v, rows):
    t = _tile(rows, 640, LANE)
    nb = rows // t

    def body(qt_ref, kt_ref, q_ref, k_ref, v_ref, o32_ref, obf_ref, lse_ref, m_sc, l_sc, acc_sc):
        qi, kj = qt_ref[pl.program_id(1)], kt_ref[pl.program_id(1)]

        @pl.when(kj == 0)
        def _():
            m_sc[...] = jnp.full_like(m_sc, NEG_BIG)
            l_sc[...] = jnp.zeros_like(l_sc)
            acc_sc[...] = jnp.zeros_like(acc_sc)

        def step(masked):
            s = lax.dot_general(q_ref[...], k_ref[...], _NT, preferred_element_type=F32)
            if masked:
                s = jnp.where(_attn_mask(qi, kj, t), s, NEG_BIG)
            m_prev = m_sc[...]
            m_new = jnp.maximum(m_prev, jnp.max(s, axis=1, keepdims=True))
            alpha = jnp.exp2(m_prev - m_new)
            p = jnp.exp2(s - jnp.tile(m_new, (1, t // LANE)))
            l_sc[...] = alpha * l_sc[...] + jnp.sum(p, axis=1, keepdims=True)
            acc_sc[...] = alpha * acc_sc[...] + jnp.dot(p.astype(BF16), v_ref[...], preferred_element_type=F32)
            m_sc[...] = m_new

        pl.when((kj == qi) | (kj == 0))(functools.partial(step, True))
        pl.when((kj < qi) & (kj > 0))(functools.partial(step, False))

        @pl.when(kj == qi)
        def _():
            o = acc_sc[...] / l_sc[...]
            o32_ref[...] = o
            obf_ref[...] = o.astype(BF16)
            lse_ref[0] = m_sc[:, 0:1] + jnp.log2(l_sc[:, 0:1])

    qt, kt = _causal_pairs(nb, by_key=False)
    qmap = lambda h, p, qt_ref, kt_ref: (qt_ref[p], h)
    kmap = lambda h, p, qt_ref, kt_ref: (kt_ref[p], h)
    return pl.pallas_call(
        body,
        name="attn_fwd",
        out_shape=[jax.ShapeDtypeStruct((rows, HEADS * VDIM), F32), jax.ShapeDtypeStruct((rows, HEADS * VDIM), BF16),
                   jax.ShapeDtypeStruct((HEADS, rows, 1), F32)],
        grid_spec=pltpu.PrefetchScalarGridSpec(
            num_scalar_prefetch=2,
            grid=(HEADS, len(qt)),
            in_specs=[pl.BlockSpec((t, QHEAD_W), qmap), pl.BlockSpec((t, QHEAD_W), kmap), pl.BlockSpec((t, VDIM), kmap)],
            out_specs=[pl.BlockSpec((t, VDIM), qmap), pl.BlockSpec((t, VDIM), qmap),
                       pl.BlockSpec((1, t, 1), lambda h, p, qt_ref, kt_ref: (h, qt_ref[p], 0))],
            scratch_shapes=[pltpu.VMEM((t, LANE), F32), pltpu.VMEM((t, LANE), F32), pltpu.VMEM((t, VDIM), F32)]),
        compiler_params=_params(("parallel", "arbitrary")),
    )(qt, kt, q_att, k_att, v)


def _attn_bwd_dq(q_att, k_att, v, do, o32, lse, rows):
    t = _tile(rows, 640, LANE)
    nb = rows // t

    def body(qt_ref, kt_ref, q_ref, k_ref, v_ref, do_ref, o_ref, lse_ref, dq_ref, delta_ref, acc_sc, dl_sc, lse_sc):
        qi, kj = qt_ref[pl.program_id(1)], kt_ref[pl.program_id(1)]

        @pl.when(kj == 0)
        def _():
            acc_sc[...] = jnp.zeros_like(acc_sc)
            delta = jnp.sum(do_ref[...].astype(F32) * o_ref[...], axis=1, keepdims=True)
            delta_ref[0] = delta
            dl_sc[...] = jnp.broadcast_to(delta, (t, LANE))
            lse_sc[...] = jnp.broadcast_to(lse_ref[0], (t, LANE))

        def step(masked):
            s = lax.dot_general(q_ref[...], k_ref[...], _NT, preferred_element_type=F32)
            p = jnp.exp2(s - jnp.tile(lse_sc[...], (1, t // LANE)))
            if masked:
                p = jnp.where(_attn_mask(qi, kj, t), p, 0.0)
            dp = lax.dot_general(do_ref[...], v_ref[...], _NT, preferred_element_type=F32)
            ds = p * (dp - jnp.tile(dl_sc[...], (1, t // LANE)))
            acc_sc[...] += jnp.dot(ds.astype(BF16), k_ref[...], preferred_element_type=F32)

        pl.when((kj == qi) | (kj == 0))(functools.partial(step, True))
        pl.when((kj < qi) & (kj > 0))(functools.partial(step, False))

        @pl.when(kj == qi)
        def _():
            dq_ref[...] = acc_sc[...] * ATTN_SCALE

    qt, kt = _causal_pairs(nb, by_key=False)
    qmap = lambda h, p, qt_ref, kt_ref: (qt_ref[p], h)
    kmap = lambda h, p, qt_ref, kt_ref: (kt_ref[p], h)
    stat = pl.BlockSpec((1, t, 1), lambda h, p, qt_ref, kt_ref: (h, qt_ref[p], 0))
    return pl.pallas_call(
        body,
        name="attn_bwd_dq",
        out_shape=[jax.ShapeDtypeStruct((rows, HEADS * QHEAD_W), F32), jax.ShapeDtypeStruct((HEADS, rows, 1), F32)],
        grid_spec=pltpu.PrefetchScalarGridSpec(
            num_scalar_prefetch=2,
            grid=(HEADS, len(qt)),
            in_specs=[pl.BlockSpec((t, QHEAD_W), qmap), pl.BlockSpec((t, QHEAD_W), kmap), pl.BlockSpec((t, VDIM), kmap),
                      pl.BlockSpec((t, VDIM), qmap), pl.BlockSpec((t, VDIM), qmap), stat],
            out_specs=[pl.BlockSpec((t, QHEAD_W), qmap), stat],
            scratch_shapes=[pltpu.VMEM((t, QHEAD_W), F32), pltpu.VMEM((t, LANE), F32), pltpu.VMEM((t, LANE), F32)]),
        compiler_params=_params(("parallel", "arbitrary")),
    )(qt, kt, q_att, k_att, v, do, o32, lse)


def _attn_bwd_dkv(q_att, k_att, v, do, lse_row, delta_row, rows):
    t = _tile(rows, 640, LANE)
    nb = rows // t

    def body(qt_ref, kt_ref, q_ref, k_ref, v_ref, do_ref, lse_ref, delta_ref, dk_ref, dv_ref, dk_sc, dv_sc):
        qi, kj = qt_ref[pl.program_id(1)], kt_ref[pl.program_id(1)]

        @pl.when(qi == kj)
        def _():
            dk_sc[...] = jnp.zeros_like(dk_sc)
            dv_sc[...] = jnp.zeros_like(dv_sc)

        def step(masked):
            st = lax.dot_general(k_ref[...], q_ref[...], _NT, preferred_element_type=F32)
            pt = jnp.exp2(st - lse_ref[0])
            if masked:
                pt = jnp.where(_attn_mask(qi, kj, t, keys_on_rows=True), pt, 0.0)
            dv_sc[...] += jnp.dot(pt.astype(BF16), do_ref[...], preferred_element_type=F32)
            dpt = lax.dot_general(v_ref[...], do_ref[...], _NT, preferred_element_type=F32)
            dst = pt * (dpt - delta_ref[0])
            dk_sc[...] += jnp.dot(dst.astype(BF16), q_ref[...], preferred_element_type=F32)

        pl.when((qi == kj) | (kj == 0))(functools.partial(step, True))
        pl.when((qi > kj) & (kj > 0))(functools.partial(step, False))

        @pl.when(qi == nb - 1)
        def _():
            dk_ref[...] = dk_sc[...] * (1.0 / LOG2E)
            dv_ref[...] = dv_sc[...]

    qt, kt = _causal_pairs(nb, by_key=True)
    qmap = lambda h, p, qt_ref, kt_ref: (qt_ref[p], h)
    kmap = lambda h, p, qt_ref, kt_ref: (kt_ref[p], h)
    stat = pl.BlockSpec((1, 1, t), lambda h, p, qt_ref, kt_ref: (h, 0, qt_ref[p]))
    return pl.pallas_call(
        body,
        name="attn_bwd_dkv",
        out_shape=[jax.ShapeDtypeStruct((rows, HEADS * QHEAD_W), F32), jax.ShapeDtypeStruct((rows, HEADS * VDIM), F32)],
        grid_spec=pltpu.PrefetchScalarGridSpec(
            num_scalar_prefetch=2,
            grid=(HEADS, len(qt)),
            in_specs=[pl.BlockSpec((t, QHEAD_W), qmap), pl.BlockSpec((t, QHEAD_W), kmap), pl.BlockSpec((t, VDIM), kmap),
                      pl.BlockSpec((t, VDIM), qmap), stat, stat],
            out_specs=[pl.BlockSpec((t, QHEAD_W), kmap), pl.BlockSpec((t, VDIM), kmap)],
            scratch_shapes=[pltpu.VMEM((t, QHEAD_W), F32), pltpu.VMEM((t, VDIM), F32)]),
        compiler_params=_params(("parallel", "arbitrary")),
    )(qt, kt, q_att, k_att, v, do, lse_row, delta_row)


C = BLOCK


def _hgrn_prep(hq, hf, hi, lb, c):
    rows = c * C + lax.broadcasted_iota(jnp.int32, (C, C), 0)
    valid = rows >= PAD_LEN
    sg = _sigmoid(hf)
    f = lb + (1.0 - lb) * sg
    g = jnp.where(valid, jnp.log(f), 0.0)
    k = jnp.where(valid, 1.0 - f, 0.0)
    q = _silu(hq)
    r = lax.broadcasted_iota(jnp.int32, (C, C), 0)
    cc = lax.broadcasted_iota(jnp.int32, (C, C), 1)
    tri = jnp.where(cc <= r, 1.0, 0.0).astype(F32)
    b = jnp.dot(tri, g, precision=lax.Precision.HIGHEST, preferred_element_type=F32)
    return q, k, hi, b, f, sg, valid


def _last_row_as_col(b_t):
    lane = lax.broadcasted_iota(jnp.int32, b_t.shape, 1)
    return jnp.sum(jnp.where(lane == C - 1, b_t, 0.0), axis=1, keepdims=True)


def _k_scaled(k, b, bs):
    return (k * jnp.exp(jnp.minimum(bs - b, 0.0))).astype(BF16)


def _hgrn_fwd(proj, lb, rows):
    nc = rows // C

    def body(hq_ref, hf_ref, hi_ref, lb_ref, o_ref, a_ref, s_ref, s_sc, b_sc):
        c = pl.program_id(1)

        @pl.when(c == 0)
        def _():
            s_sc[...] = jnp.zeros_like(s_sc)

        q, k, v, b, _, _, _ = _hgrn_prep(hq_ref[...], hf_ref[...], hi_ref[...], lb_ref[...], c)
        b_sc[...] = b
        s0 = s_sc[...]
        s_ref[0, 0] = s0
        v_bf = v.astype(BF16)
        r16 = lax.broadcasted_iota(jnp.int32, (SUB, C), 0)
        c16 = lax.broadcasted_iota(jnp.int32, (SUB, C), 1)
        slabs = [jnp.zeros((SUB, C), F32)]
        for i in range(1, C // SUB):
            bs = b_sc[SUB * i - 1:SUB * i, :]
            qs = (q[SUB * i:SUB * (i + 1)] * jnp.exp(b[SUB * i:SUB * (i + 1)] - bs)).astype(BF16)
            a_i = lax.dot_general(qs, _k_scaled(k, b, bs), _NT, preferred_element_type=F32)
            slabs.append(jnp.where(c16 <= r16 + (SUB * i - SUB), a_i, 0.0))
        a_off = jnp.concatenate(slabs, axis=0)
        q_t, k_t, b_t = q.T, k.T, b.T
        sub = lax.broadcasted_iota(jnp.int32, (C, C), 0)
        lane = lax.broadcasted_iota(jnp.int32, (C, C), 1)
        lane1 = lax.broadcasted_iota(jnp.int32, (1, C), 1)
        at_band = jnp.zeros((C, C), F32)
        ahead = lane - sub
        for dl in range(SUB):
            k_s = pltpu.roll(k_t, dl, 1) if dl else k_t
            b_s = pltpu.roll(b_t, dl, 1) if dl else b_t
            e = jnp.exp(b_t - b_s)
            band = jnp.sum(q_t * k_s * e, axis=0, keepdims=True)
            band = jnp.where(lane1 >= dl, band, 0.0)
            at_band = at_band + jnp.where(ahead == dl, jnp.broadcast_to(band, (C, C)), 0.0)
        a = (a_off + at_band.T).astype(BF16)
        a_ref[0] = a
        qe = (q * jnp.exp(b)).astype(BF16)
        o_ref[...] = (jnp.dot(a, v_bf, preferred_element_type=F32)
                      + jnp.dot(qe, s0.astype(BF16), preferred_element_type=F32))
        b_last = b_sc[C - 1:C, :]
        kd = (k * jnp.exp(b_last - b)).astype(BF16)
        s_sc[...] = (jnp.exp(_last_row_as_col(b_t)) * s0
                     + lax.dot_general(kd, v_bf, _TN, preferred_element_type=F32))

    seg = lambda base: (lambda h, c: (c, base // C + h))
    return pl.pallas_call(
        body,
        name="hgrn_fwd",
        out_shape=[jax.ShapeDtypeStruct((rows, D_MODEL), F32), jax.ShapeDtypeStruct((HEADS, rows, C), BF16),
                   jax.ShapeDtypeStruct((HEADS, nc, C, C), F32)],
        grid=(HEADS, nc),
        in_specs=[pl.BlockSpec((C, C), seg(SEG_HQ)), pl.BlockSpec((C, C), seg(SEG_HF)), pl.BlockSpec((C, C), seg(SEG_HI)),
                  pl.BlockSpec((1, C), lambda h, c: (0, h))],
        out_specs=[pl.BlockSpec((C, C), lambda h, c: (c, h)), pl.BlockSpec((1, C, C), lambda h, c: (h, c, 0)),
                   pl.BlockSpec((1, 1, C, C), lambda h, c: (h, c, 0, 0))],
        scratch_shapes=[pltpu.VMEM((C, C), F32), pltpu.VMEM((C, C), F32)],
        compiler_params=_params(("parallel", "arbitrary")),
    )(proj, proj, proj, lb)


def _hgrn_bwd(proj, lb, a_mat, s_states, do_h, rows):
    nc = rows // C

    def body(hq_ref, hf_ref, hi_ref, lb_ref, a_ref, s_ref, do_ref, dhq_ref, dhf_ref, dhi_ref, dlb_ref, ds_sc, b_sc):
        step = pl.program_id(1)
        c = nc - 1 - step

        @pl.when(step == 0)
        def _():
            ds_sc[...] = jnp.zeros_like(ds_sc)
            dlb_ref[...] = jnp.zeros_like(dlb_ref)

        hq, hf = hq_ref[...], hf_ref[...]
        lb_row = lb_ref[...]
        q, k, v, b, f, sg, valid = _hgrn_prep(hq, hf, hi_ref[...], lb_row, c)
        b_sc[...] = b
        s0 = s_ref[0, 0]
        ds1 = ds_sc[...]
        s0_bf, ds1_bf = s0.astype(BF16), ds1.astype(BF16)
        do = do_ref[...]
        do_bf, v_bf = do.astype(BF16), v.astype(BF16)
        b_last = b_sc[C - 1:C, :]
        e_last = jnp.exp(b_last - b)
        eb = jnp.exp(b)
        sub = lax.broadcasted_iota(jnp.int32, (C, C), 0)
        lane = lax.broadcasted_iota(jnp.int32, (C, C), 1)
        r16 = lax.broadcasted_iota(jnp.int32, (SUB, C), 0)
        c16 = lax.broadcasted_iota(jnp.int32, (SUB, C), 1)

        dv = (lax.dot_general(a_ref[0], do_bf, _TN, preferred_element_type=F32)
              + jnp.dot((k * e_last).astype(BF16), ds1_bf, preferred_element_type=F32))
        da = jnp.where(lane <= sub, lax.dot_general(do_bf, v_bf, _NT, preferred_element_type=F32), 0.0)
        da_t = jnp.where(sub <= lane, lax.dot_general(v_bf, do_bf, _NT, preferred_element_type=F32), 0.0)

        dq_slabs = [jnp.zeros((SUB, C), F32)]
        for i in range(1, C // SUB):
            bs = b_sc[SUB * i - 1:SUB * i, :]
            da_i = jnp.where(c16 <= r16 + (SUB * i - SUB), da[SUB * i:SUB * (i + 1)], 0.0).astype(BF16)
            dq_slabs.append(jnp.exp(b[SUB * i:SUB * (i + 1)] - bs)
                            * jnp.dot(da_i, _k_scaled(k, b, bs), preferred_element_type=F32))
        dk_slabs = []
        for j in range(C // SUB - 1):
            be = b_sc[SUB * j + SUB - 1:SUB * (j + 1), :]
            qe_j = (q * jnp.exp(jnp.minimum(b - be, 0.0))).astype(BF16)
            da_j = jnp.where(c16 >= r16 + (SUB * j + SUB), da_t[SUB * j:SUB * (j + 1)], 0.0).astype(BF16)
            dk_slabs.append(jnp.exp(be - b[SUB * j:SUB * (j + 1)]) * jnp.dot(da_j, qe_j, preferred_element_type=F32))
        dk_slabs.append(jnp.zeros((SUB, C), F32))

        q_t, k_t, b_t = q.T, k.T, b.T
        lane1 = lax.broadcasted_iota(jnp.int32, (1, C), 1)
        dq_t = jnp.zeros((C, C), F32)
        dk_t = jnp.zeros((C, C), F32)
        ahead = lane - sub
        for dl in range(SUB):
            k_s = pltpu.roll(k_t, dl, 1) if dl else k_t
            b_s = pltpu.roll(b_t, dl, 1) if dl else b_t
            e = jnp.exp(jnp.minimum(b_t - b_s, 0.0))
            dband = jnp.sum(jnp.where(ahead == dl, da_t, 0.0), axis=0, keepdims=True)
            w = jnp.where(lane1 >= dl, dband, 0.0) * e
            dq_t = dq_t + w * k_s
            back = w * q_t
            dk_t = dk_t + (pltpu.roll(back, C - dl, 1) if dl else back)

        dq = eb * lax.dot_general(do_bf, s0_bf, _NT, preferred_element_type=F32) + jnp.concatenate(dq_slabs, axis=0) + dq_t.T
        dk_inter = e_last * lax.dot_general(v_bf, ds1_bf, _NT, preferred_element_type=F32)
        dk = dk_inter + jnp.concatenate(dk_slabs, axis=0) + dk_t.T

        extra = (jnp.exp(b_last) * jnp.sum((s0 * ds1).T, axis=0, keepdims=True)
                 + jnp.sum(k * dk_inter, axis=0, keepdims=True))
        db = q * dq - k * dk + jnp.where(sub == C - 1, jnp.broadcast_to(extra, (C, C)), 0.0)
        tri_t = jnp.where(lane >= sub, 1.0, 0.0).astype(F32)
        dg = jnp.dot(tri_t, db, precision=lax.Precision.HIGHEST, preferred_element_type=F32)
        ds_sc[...] = (jnp.exp(_last_row_as_col(b_t)) * ds1
                      + lax.dot_general((q * eb).astype(BF16), do_bf, _TN, preferred_element_type=F32))

        df = jnp.where(valid, dg / f - dk, 0.0)
        dhf_ref[...] = (df * (1.0 - lb_row) * sg * (1.0 - sg)).astype(BF16)
        dlb_ref[...] += jnp.sum(df * (1.0 - sg), axis=0, keepdims=True)
        dhq_ref[...] = (dq * _dsilu(hq)).astype(BF16)
        dhi_ref[...] = dv.astype(BF16)

    seg = lambda base: (lambda h, s: (nc - 1 - s, base // C + h))
    rmap = lambda h, s: (nc - 1 - s, h)
    return pl.pallas_call(
        body,
        name="hgrn_bwd",
        out_shape=[jax.ShapeDtypeStruct((rows, D_MODEL), BF16)] * 3 + [jax.ShapeDtypeStruct((1, D_MODEL), F32)],
        grid=(HEADS, nc),
        in_specs=[pl.BlockSpec((C, C), seg(SEG_HQ)), pl.BlockSpec((C, C), seg(SEG_HF)), pl.BlockSpec((C, C), seg(SEG_HI)),
                  pl.BlockSpec((1, C), lambda h, s: (0, h)),
                  pl.BlockSpec((1, C, C), lambda h, s: (h, nc - 1 - s, 0)),
                  pl.BlockSpec((1, 1, C, C), lambda h, s: (h, nc - 1 - s, 0, 0)),
                  pl.BlockSpec((C, C), rmap)],
        out_specs=[pl.BlockSpec((C, C), rmap)] * 3 + [pl.BlockSpec((1, C), lambda h, s: (0, h))],
        scratch_shapes=[pltpu.VMEM((C, C), F32), pltpu.VMEM((C, C), F32)],
        compiler_params=_params(("parallel", "arbitrary")),
    )(proj, proj, proj, lb, a_mat, s_states, do_h)


CONV_TC = 512
HALO = 8


def _conv_taps(i, tm, g_ref, pg_ref):
    shape = g_ref.shape
    r = lax.broadcasted_iota(jnp.int32, shape, 0)
    g = jnp.where(i * tm + r >= PAD_LEN, g_ref[...], 0.0)
    p1 = jnp.where(i * tm - 1 >= PAD_LEN, pg_ref[HALO - 1:HALO, :], 0.0)
    p2 = jnp.where(i * tm - 2 >= PAD_LEN, pg_ref[HALO - 2:HALO - 1, :], 0.0)
    s1 = jnp.where(r == 0, p1, pltpu.roll(g, 1, 0))
    s2 = jnp.where(r == 0, p2, jnp.where(r == 1, p1, pltpu.roll(g, 2, 0)))
    return g, s1, s2


def _conv_specs(tm, tc, ncb, order):
    gate = pl.BlockSpec((tm, tc), lambda *ids: order(ids))
    halo = pl.BlockSpec((HALO, tc), lambda *ids: (jnp.maximum(order(ids)[0] * (tm // HALO) - 1, 0), order(ids)[1]))
    up = pl.BlockSpec((tm, tc), lambda *ids: (order(ids)[0], ncb + order(ids)[1]))
    return gate, halo, up


def _conv_fwd(ffn, conv_w, conv_b, rows, tm):
    tc = CONV_TC
    ncb = D_FF // tc

    def body(g_ref, pg_ref, up_ref, cw_ref, cb_ref, act_ref):
        i = pl.program_id(0)
        g, s1, s2 = _conv_taps(i, tm, g_ref, pg_ref)
        conv = (cw_ref[0:1, :] * s2 + cw_ref[1:2, :] * s1 + cw_ref[2:3, :] * g) + cb_ref[...]
        act_ref[...] = (_silu(conv) * up_ref[...]).astype(BF16)

    gate, halo, up = _conv_specs(tm, tc, ncb, lambda ids: (ids[0], ids[1]))
    return pl.pallas_call(
        body,
        name="conv_fwd",
        out_shape=jax.ShapeDtypeStruct((rows, D_FF), BF16),
        grid=(rows // tm, ncb),
        in_specs=[gate, halo, up, pl.BlockSpec((3, tc), lambda i, j: (0, j)), pl.BlockSpec((1, tc), lambda i, j: (0, j))],
        out_specs=pl.BlockSpec((tm, tc), lambda i, j: (i, j)),
        compiler_params=_params(("parallel", "parallel")),
    )(ffn, ffn, ffn, conv_w, conv_b)


def _conv_bwd_a(ffn, dact, conv_w, conv_b, rows, tm):
    tc = CONV_TC
    ncb = D_FF // tc

    def body(g_ref, pg_ref, up_ref, da_ref, cw_ref, cb_ref, dc_ref, dffn_ref, w0_ref, w1_ref, w2_ref, db_ref):
        i = pl.program_id(1)
        g, s1, s2 = _conv_taps(i, tm, g_ref, pg_ref)
        conv = (cw_ref[0:1, :] * s2 + cw_ref[1:2, :] * s1 + cw_ref[2:3, :] * g) + cb_ref[...]
        da = da_ref[...]
        dffn_ref[...] = (da * _silu(conv)).astype(BF16)
        dc = da * up_ref[...] * _dsilu(conv)
        dc_ref[...] = dc
        sums = [jnp.sum(dc * s2, axis=0, keepdims=True), jnp.sum(dc * s1, axis=0, keepdims=True),
                jnp.sum(dc * g, axis=0, keepdims=True), jnp.sum(dc, axis=0, keepdims=True)]
        for ref, val in zip((w0_ref, w1_ref, w2_ref, db_ref), sums):
            @pl.when(i == 0)
            def _(ref=ref, val=val):
                ref[...] = val

            @pl.when(i > 0)
            def _(ref=ref, val=val):
                ref[...] += val

    gate, halo, up = _conv_specs(tm, tc, ncb, lambda ids: (ids[1], ids[0]))
    col = pl.BlockSpec((1, tc), lambda j, i: (0, j))
    return pl.pallas_call(
        body,
        name="conv_bwd_a",
        out_shape=[jax.ShapeDtypeStruct((rows, D_FF), F32), jax.ShapeDtypeStruct((rows, 2 * D_FF), BF16)]
        + [jax.ShapeDtypeStruct((1, D_FF), F32)] * 4,
        grid=(ncb, rows // tm),
        in_specs=[gate, halo, up, pl.BlockSpec((tm, tc), lambda j, i: (i, j)),
                  pl.BlockSpec((3, tc), lambda j, i: (0, j)), col],
        out_specs=[pl.BlockSpec((tm, tc), lambda j, i: (i, j)), pl.BlockSpec((tm, tc), lambda j, i: (i, ncb + j)),
                   col, col, col, col],
        compiler_params=_params(("parallel", "arbitrary")),
    )(ffn, ffn, ffn, dact, conv_w, conv_b)


def _conv_bwd_b(dconv, conv_w, dffn, rows, tm):
    tc = CONV_TC
    ncb = D_FF // tc
    nrb = rows // tm

    def body(dc_ref, nx_ref, cw_ref, dffn_in, out_ref):
        del dffn_in
        i = pl.program_id(0)
        dc = dc_ref[...]
        r = lax.broadcasted_iota(jnp.int32, dc.shape, 0)
        last = i == nrb - 1
        x1 = jnp.where(last, 0.0, nx_ref[0:1, :])
        x2 = jnp.where(last, 0.0, nx_ref[1:2, :])
        n1 = jnp.where(r == tm - 1, x1, pltpu.roll(dc, tm - 1, 0))
        n2 = jnp.where(r == tm - 1, x2, jnp.where(r == tm - 2, x1, pltpu.roll(dc, tm - 2, 0)))
        dg = cw_ref[2:3, :] * dc + cw_ref[1:2, :] * n1 + cw_ref[0:1, :] * n2
        out_ref[...] = jnp.where(i * tm + r >= PAD_LEN, dg, 0.0).astype(BF16)

    return pl.pallas_call(
        body,
        name="conv_bwd_b",
        out_shape=jax.ShapeDtypeStruct((rows, 2 * D_FF), BF16),
        grid=(nrb, ncb),
        in_specs=[pl.BlockSpec((tm, tc), lambda i, j: (i, j)),
                  pl.BlockSpec((HALO, tc), lambda i, j: (jnp.minimum((i + 1) * (tm // HALO), rows // HALO - 1), j)),
                  pl.BlockSpec((3, tc), lambda i, j: (0, j)),
                  pl.BlockSpec(memory_space=pl.ANY)],
        out_specs=pl.BlockSpec((tm, tc), lambda i, j: (i, j)),
        input_output_aliases={3: 0},
        compiler_params=_params(("parallel", "parallel")),
    )(dconv, dconv, conv_w, dffn)


def _final_call(h1, y, target, g_final, rows):
    tm = BLOCK

    def fn(i, tm_, h1v, yv, tgt, g):
        h2 = h1v + yv
        out = _rms_fwd(h2, g)
        err = jnp.where(i > 0, out - tgt, 0.0)
        loss = 0.5 * jnp.sum(jnp.mean(err * err, axis=-1, keepdims=True), axis=0, keepdims=True)
        dx, dg = _rms_bwd(h2, g, err * (1.0 / D_MODEL))
        return dx, dx, jnp.broadcast_to(loss, (1, LANE)), dg

    n_in = 4
    in_specs = [pl.BlockSpec((tm, D_MODEL), lambda i: (i, 0)), pl.BlockSpec((tm, D_MODEL), lambda i: (i, 0)),
                pl.BlockSpec((tm, D_MODEL), lambda i: (jnp.maximum(i - 1, 0), 0)),
                pl.BlockSpec((1, D_MODEL), lambda i: (0, 0))]

    def body(*refs):
        i = pl.program_id(0)
        dx, dx2, loss, dg = fn(i, tm, *[r[...] for r in refs[:n_in]])
        refs[4][...] = dx
        refs[5][...] = dx2.astype(BF16)
        for ref, val in ((refs[6], loss), (refs[7], dg)):
            @pl.when(i == 0)
            def _(ref=ref, val=val):
                ref[...] = val

            @pl.when(i > 0)
            def _(ref=ref, val=val):
                ref[...] += val

    return pl.pallas_call(
        body,
        name="final_loss",
        out_shape=[jax.ShapeDtypeStruct((rows, D_MODEL), F32), jax.ShapeDtypeStruct((rows, D_MODEL), BF16),
                   jax.ShapeDtypeStruct((1, LANE), F32), jax.ShapeDtypeStruct((1, D_MODEL), F32)],
        grid=(rows // tm,),
        in_specs=in_specs,
        out_specs=[pl.BlockSpec((tm, D_MODEL), lambda i: (i, 0)), pl.BlockSpec((tm, D_MODEL), lambda i: (i, 0)),
                   pl.BlockSpec((1, LANE), lambda i: (0, 0)), pl.BlockSpec((1, D_MODEL), lambda i: (0, 0))],
        compiler_params=_params(("arbitrary",)),
    )(h1, y, target, g_final)


def _heads_map(fn, *slabs):
    outs = [fn(*[s[:, h * LANE:(h + 1) * LANE] for s in slabs]) for h in range(HEADS)]
    if isinstance(outs[0], tuple):
        return tuple(jnp.concatenate([o[k] for o in outs], axis=1) for k in range(len(outs[0])))
    return jnp.concatenate(outs, axis=1)


def _local_step(x, positions, target, w, p, emit=None):
    kept = {}
    if emit is None:
        def emit(group):
            kept.update(group)
            return None
    s_len = x.shape[0]
    rows = s_len + BLOCK
    tm = _tile(rows, 640, 8)
    row = lambda arr, width, cb=0: ("row", arr, width, cb)

    h0 = jnp.concatenate([jnp.zeros((PAD_LEN, D_MODEL), F32), w["meta_tokens"], x], axis=0)
    pos = jnp.concatenate([jnp.zeros((PAD_LEN,), jnp.int32), jnp.arange(N_META, dtype=jnp.int32),
                           positions.astype(jnp.int32) + N_META])
    inv = 1.0 / (ROPE_THETA ** (jnp.arange(0, ROPE, 2, dtype=F32) / ROPE))
    ang = pos.astype(F32)[:, None] * inv
    zero = jnp.zeros((rows, LANE - ROPE), F32)
    cos_t = jnp.concatenate([jnp.cos(ang), jnp.cos(ang), zero], axis=1)
    sin_t = jnp.concatenate([-jnp.sin(ang), jnp.sin(ang), zero], axis=1)
    lb_r0, lb_r1 = p["lb_raw"][0:1], p["lb_raw"][1:2]

    def lb_fn(i, tm_, r0, r1):
        m = jnp.maximum(r0, r1)
        e0, e1 = jnp.exp(r0 - m), jnp.exp(r1 - m)
        return (e0 / (e0 + e1),)

    (lb,) = _rowwise("lb_fwd", lb_fn, [("bc", lb_r0), ("bc", lb_r1)], [("acc", (1, D_MODEL))], 1, 1)

    (u1,) = _rowwise("mix_norm", lambda i, t, h, g: (_rms_fwd(h, g),),
                     [row(h0, D_MODEL), ("bc", p["g_mix_norm"])], [("row", D_MODEL, BF16)], rows, tm)
    proj = _matmul(u1, w["w_in"], "nn", F32, "mm_proj")
    hint = getattr(w, "hint", lambda name, after: None)
    hint("w_q_up", proj)
    qn, kvn = _rowwise(
        "latent_norm", lambda i, t, ql, kl, gq, gk: (_rms_fwd(ql, gq), _rms_fwd(kl, gk)),
        [row(proj, Q_LORA, 0), row(proj, KV_LORA, SEG_KV_LAT // KV_LORA), ("bc", p["g_q_norm"]), ("bc", p["g_kv_norm"])],
        [("row", Q_LORA, BF16), ("row", KV_LORA, BF16)], rows, tm)
    q_raw = _matmul(qn, w["w_q_up"], "nn", F32, "mm_q_up")
    kv = _matmul(kvn, w["w_kv_up"], "nn", F32, "mm_kv_up")
    q_att, k_att, v_att = _rope_fwd_call(q_raw, kv, proj, cos_t, sin_t, rows, tm)
    o32, o_bf, lse = _attn_fwd(q_att, k_att, v_att, rows)
    hint("w_branch_mla", lse)
    o_h, a_mat, s_states = _hgrn_fwd(proj, lb, rows)

    def hgrn_post(i, t, oh, hg, g):
        return (_heads_map(lambda a, b: _rms_fwd(a, g) * _silu(b), oh, hg),)

    (o_hgrn,) = _rowwise("hgrn_post", hgrn_post,
                         [row(o_h, D_MODEL), row(proj, D_MODEL, SEG_HG // D_MODEL), ("bc", p["g_hgrn_norm"])],
                         [("row", D_MODEL, BF16)], rows, tm)
    br_a = _matmul(o_bf, w["w_branch_mla"], "nn", F32, "mm_branch_mla")
    br_b = _matmul(o_hgrn, w["w_branch_hgrn"], "nn", F32, "mm_branch_hgrn")
    (merged,) = _rowwise(
        "merge", lambda i, t, a, b, ga, gb: (_sigmoid(ga) * a + _sigmoid(gb) * b,),
        [row(br_a, D_MODEL), row(br_b, D_MODEL), row(proj, D_MODEL, SEG_GA // D_MODEL), row(proj, D_MODEL, SEG_GB // D_MODEL)],
        [("row", D_MODEL, BF16)], rows, tm)
    mix_out = _matmul(merged, w["w_out"], "nn", F32, "mm_out")

    def ffn_norm(i, t, h, mo, g):
        h1v = h + mo
        return h1v, _rms_fwd(h1v, g)

    h1, u2 = _rowwise("ffn_norm", ffn_norm, [row(h0, D_MODEL), row(mix_out, D_MODEL), ("bc", p["g_ffn_norm"])],
                      [("row", D_MODEL, F32), ("row", D_MODEL, BF16)], rows, tm)
    ffn = _matmul(u2, w["w_ffn_in"], "nn", F32, "mm_ffn_in")
    act = _conv_fwd(ffn, w["conv_w"], p["conv_b"], rows, tm)
    y = _matmul(act, w["w_ffn_out"], "nn", F32, "mm_ffn_out")
    dh2, dh2_bf, loss_acc, dg_final = _final_call(h1, y, target, p["g_final_norm"].reshape(1, D_MODEL), rows)

    grads = {"g_final_norm": dg_final.reshape(D_MODEL)}
    dact = _matmul(dh2_bf, w["w_ffn_out"], "nt", F32, "mm_d_act")
    grads["w_ffn_out"] = _matmul(act, dh2_bf, "tn", BF16, "mm_dw_ffn_out")
    dconv, dffn, dcw0, dcw1, dcw2, dcb = _conv_bwd_a(ffn, dact, w["conv_w"], p["conv_b"], rows, tm)
    dffn = _conv_bwd_b(dconv, w["conv_w"], dffn, rows, tm)
    grads["conv_w"] = jnp.concatenate([dcw0, dcw1, dcw2], axis=0)
    grads["conv_b"] = dcb
    du2 = _matmul(dffn, w["w_ffn_in"], "nt", F32, "mm_d_u2")
    grads["w_ffn_in"] = _matmul(u2, dffn, "tn", BF16, "mm_dw_ffn_in")

    def ffn_norm_bwd(i, t, h, du, dh, g):
        dx, dg = _rms_bwd(h, g, du)
        dh1v = dh + dx
        return dh1v, dh1v, dg

    dh1, dh1_bf, grads["g_ffn_norm"] = _rowwise(
        "ffn_norm_bwd", ffn_norm_bwd, [row(h1, D_MODEL), row(du2, D_MODEL), row(dh2, D_MODEL), ("bc", p["g_ffn_norm"])],
        [("row", D_MODEL, F32), ("row", D_MODEL, BF16), ("acc", (1, D_MODEL))], rows, tm)
    tok = emit({n: grads.pop(n) for n in ("w_ffn_out", "w_ffn_in", "conv_w", "conv_b", "g_final_norm", "g_ffn_norm")})
    dmerged = _matmul(dh1_bf, w["w_out"], "nt", F32, "mm_d_merged", after=tok)
    grads["w_out"] = _matmul(merged, dh1_bf, "tn", BF16, "mm_dw_out")

    def merge_bwd(i, t, dm, a, b, ga, gb):
        sa, sb = _sigmoid(ga), _sigmoid(gb)
        return dm * sa, dm * sb, dm * a * sa * (1.0 - sa), dm * b * sb * (1.0 - sb)

    da_bf, db_bf, dga, dgb = _rowwise(
        "merge_bwd", merge_bwd,
        [row(dmerged, D_MODEL), row(br_a, D_MODEL), row(br_b, D_MODEL),
         row(proj, D_MODEL, SEG_GA // D_MODEL), row(proj, D_MODEL, SEG_GB // D_MODEL)],
        [("row", D_MODEL, BF16)] * 4, rows, tm)
    do_mla = _matmul(da_bf, w["w_branch_mla"], "nt", BF16, "mm_d_o_mla")
    grads["w_branch_mla"] = _matmul(o_bf, da_bf, "tn", BF16, "mm_dw_branch_mla")
    do_hgrn = _matmul(db_bf, w["w_branch_hgrn"], "nt", F32, "mm_d_o_hgrn")
    grads["w_branch_hgrn"] = _matmul(o_hgrn, db_bf, "tn", BF16, "mm_dw_branch_hgrn")

    def hgrn_post_bwd(i, t, dy, oh, hg, g):
        def one(dyh, ohh, hgh):
            dx, dg = _rms_bwd(ohh, g, dyh * _silu(hgh))
            return dx, dyh * _rms_fwd(ohh, g) * _dsilu(hgh), dg

        dx, dhg, dg = _heads_map(one, dy, oh, hg)
        dg_sum = dg[:, 0:LANE]
        for h in range(1, HEADS):
            dg_sum = dg_sum + dg[:, h * LANE:(h + 1) * LANE]
        return dx, dhg, dg_sum

    tok = emit({n: grads.pop(n) for n in ("w_out", "w_branch_mla", "w_branch_hgrn")})
    do_h, dhg, grads["g_hgrn_norm"] = _rowwise(
        "hgrn_post_bwd", hgrn_post_bwd,
        [row(do_hgrn, D_MODEL), row(o_h, D_MODEL), row(proj, D_MODEL, SEG_HG // D_MODEL), ("bc", p["g_hgrn_norm"])],
        [("row", D_MODEL, F32), ("row", D_MODEL, BF16), ("acc", (1, LANE))], rows, tm, after=tok)
    dhq, dhf, dhi, dlb = _hgrn_bwd(proj, lb, a_mat, s_states, do_h, rows)

    def lb_bwd(i, tm_, d, l):
        t = d * l * (1.0 - l)
        return t, -t

    dlb0, dlb1 = _rowwise("lb_bwd", lb_bwd, [("bc", dlb), ("bc", lb)], [("acc", (1, D_MODEL))] * 2, 1, 1)
    grads["lb_raw"] = jnp.concatenate([dlb0, dlb1], axis=0)

    dq_att, delta = _attn_bwd_dq(q_att, k_att, v_att, do_mla, o32, lse, rows)
    dk_att, dv_att = _attn_bwd_dkv(q_att, k_att, v_att, do_mla, lse.reshape(HEADS, 1, rows),
                                   delta.reshape(HEADS, 1, rows), rows)
    dq_full, dkv, dkr = _rope_bwd_call(dq_att, dk_att, dv_att, cos_t, sin_t, rows, tm)
    dqn = _matmul(dq_full, w["w_q_up"], "nt", F32, "mm_d_qn")
    grads["w_q_up"] = _matmul(qn, dq_full, "tn", BF16, "mm_dw_q_up")
    dkvn = _matmul(dkv, w["w_kv_up"], "nt", F32, "mm_d_kvn")
    grads["w_kv_up"] = _matmul(kvn, dkv, "tn", BF16, "mm_dw_kv_up")

    def latent_norm_bwd(i, t, ql, kl, dq, dk, gq, gk):
        dql, dgq = _rms_bwd(ql, gq, dq)
        dkl, dgk = _rms_bwd(kl, gk, dk)
        return dql, dkl, dgq, dgk

    dq_lat, dkv_lat, grads["g_q_norm"], grads["g_kv_norm"] = _rowwise(
        "latent_norm_bwd", latent_norm_bwd,
        [row(proj, Q_LORA, 0), row(proj, KV_LORA, SEG_KV_LAT // KV_LORA), row(dqn, Q_LORA), row(dkvn, KV_LORA),
         ("bc", p["g_q_norm"]), ("bc", p["g_kv_norm"])],
        [("row", Q_LORA, BF16), ("row", KV_LORA, BF16), ("acc", (1, Q_LORA)), ("acc", (1, KV_LORA))], rows, tm)
    dproj = jnp.concatenate([dq_lat, dkv_lat, dhq, dhf, dhi, dhg, dga, dgb, dkr], axis=1)
    grads["w_in"] = _matmul(u1, dproj, "tn", BF16, "mm_dw_in")
    tok = emit({n: grads.pop(n) for n in ("w_in", "w_q_up", "w_kv_up", "lb_raw", "g_q_norm", "g_kv_norm", "g_hgrn_norm")})
    du1 = _matmul(dproj, w["w_in"], "nt", F32, "mm_d_u1", after=tok)

    def mix_norm_bwd(i, t, h, du, dh, g):
        dx, dg = _rms_bwd(h, g, du)
        return dh + dx, dg

    dh0, grads["g_mix_norm"] = _rowwise(
        "mix_norm_bwd", mix_norm_bwd, [row(h0, D_MODEL), row(du1, D_MODEL), row(dh1, D_MODEL), ("bc", p["g_mix_norm"])],
        [("row", D_MODEL, F32), ("acc", (1, D_MODEL))], rows, tm)
    grads["meta_tokens"] = dh0[PAD_LEN:BLOCK]
    kept.update(grads)
    return loss_acc[0, 0], dh0[BLOCK:], kept


K_ROPE_AT = Q_LORA + KV_LORA
COL_SHARDED = ("w_in", "w_q_up", "w_kv_up", "w_ffn_in", "conv_w", "meta_tokens")
ROW_SHARDED = ("w_branch_mla", "w_branch_hgrn", "w_out", "w_ffn_out")
BIG = ("w_in", "w_q_up", "w_kv_up", "w_branch_mla", "w_branch_hgrn", "w_out", "w_ffn_in", "w_ffn_out")
SMALL = ("conv_b", "g_mix_norm", "g_q_norm", "g_kv_norm", "g_hgrn_norm", "g_ffn_norm", "g_final_norm", "lb_raw")


def _unshard(name, stacked):
    if name in COL_SHARDED:
        return jnp.transpose(stacked, (1, 0, 2)).reshape(stacked.shape[1], N_DEV * stacked.shape[2])
    return stacked.reshape(N_DEV * stacked.shape[1], stacked.shape[2])


def _reshard(name, full):
    if name in COL_SHARDED:
        r, c = full.shape
        return jnp.transpose(full.reshape(r, N_DEV, c // N_DEV), (1, 0, 2))
    return full.reshape(N_DEV, full.shape[0] // N_DEV, full.shape[1])


def _to_kernel_layout(full):
    out = dict(full)
    if "w_in" in full:
        w_in = full["w_in"]
        pad = jnp.zeros((D_MODEL, KR_W - ROPE), w_in.dtype)
        out["w_in"] = jnp.concatenate(
            [w_in[:, :K_ROPE_AT], w_in[:, K_ROPE_AT + ROPE:], w_in[:, K_ROPE_AT:K_ROPE_AT + ROPE], pad], axis=1)
    if "w_q_up" in full:
        wq = full["w_q_up"].reshape(Q_LORA, HEADS, NOPE + ROPE)
        out["w_q_up"] = jnp.pad(wq, ((0, 0), (0, 0), (0, QHEAD_W - NOPE - ROPE))).reshape(Q_LORA, HEADS * QHEAD_W)
    return out


def _from_kernel_layout(grads):
    out = dict(grads)
    if "w_in" in grads:
        g = grads["w_in"]
        out["w_in"] = jnp.concatenate([g[:, :K_ROPE_AT], g[:, SEG_KR:SEG_KR + ROPE], g[:, K_ROPE_AT:SEG_KR]], axis=1)
    if "w_q_up" in grads:
        g = grads["w_q_up"].reshape(Q_LORA, HEADS, QHEAD_W)
        out["w_q_up"] = g[:, :, :NOPE + ROPE].reshape(Q_LORA, HEADS * (NOPE + ROPE))
    return out


MESH_ID = pl.DeviceIdType.MESH
ANY = pl.BlockSpec(memory_space=pl.ANY)


def _slot(dev):
    return 4 * dev[0] + 2 * dev[1] + dev[2]


def _all_gather(shards):
    n = len(shards)

    def body(*refs):
        ins, outs = refs[:n], refs[n:2 * n]
        send_sems, recv_sems, local_sems = refs[2 * n:]
        x, y, c = lax.axis_index("x"), lax.axis_index("y"), lax.axis_index("c")
        me, sibling = (x, y, c), (x, y, 1 - c)
        chips = [(1 - x, y), (x, 1 - y), (1 - x, 1 - y)]

        def copy(a, k, block, to, src=None):
            dst = outs[a].at[_slot(block)]
            return pltpu.make_async_remote_copy(
                src_ref=dst if src is None else src, dst_ref=dst, send_sem=send_sems.at[a, k],
                recv_sem=recv_sems.at[a, k], device_id=to, device_id_type=MESH_ID)

        mine = [pltpu.make_async_copy(ins[a], outs[a].at[_slot(me)], local_sems.at[a]) for a in range(n)]
        for cp in mine:
            cp.start()
        first = []
        for a in range(n):
            first.append(copy(a, 0, me, sibling, src=ins[a]))
            first += [copy(a, 1 + j, me, (*chip, c), src=ins[a]) for j, chip in enumerate(chips)]
        for cp in first:
            cp.start()
        passed = []
        for a in range(n):
            for j, chip in enumerate(chips):
                copy(a, 1 + j, (*chip, c), me).wait_recv()
                fwd = copy(a, 4 + j, (*chip, c), sibling)
                fwd.start()
                passed.append(fwd)
        for a in range(n):
            copy(a, 0, sibling, me).wait_recv()
            for j, chip in enumerate(chips):
                copy(a, 4 + j, (*chip, 1 - c), me).wait_recv()
        for cp in first + passed:
            cp.wait_send()
        for cp in mine:
            cp.wait()

    return pl.pallas_call(
        body,
        name="gather_weights",
        out_shape=[jax.ShapeDtypeStruct((N_DEV,) + s.shape, s.dtype) for s in shards],
        in_specs=[ANY] * n,
        out_specs=[ANY] * n,
        scratch_shapes=[pltpu.SemaphoreType.DMA((n, 7)), pltpu.SemaphoreType.DMA((n, 7)), pltpu.SemaphoreType.DMA((n,))],
    )(*shards)


def _exchange(blocked, replicated):
    nb, n = len(blocked), len(blocked) + len(replicated)
    arrays = list(blocked) + list(replicated)

    def body(*refs):
        ins, outs = refs[:n], refs[n:2 * n]
        send_sems, recv_sems, local_sems = refs[2 * n:]
        x, y, c = lax.axis_index("x"), lax.axis_index("y"), lax.axis_index("c")
        me = (x, y, c)
        peers = [(x, y, 1 - c), (1 - x, y, c), (x, 1 - y, c), (1 - x, 1 - y, c),
                 (1 - x, y, 1 - c), (x, 1 - y, 1 - c), (1 - x, 1 - y, 1 - c)]

        def src_of(a, dev):
            return ins[a].at[_slot(dev)] if a < nb else ins[a]

        def copy(a, k, frm, to):
            return pltpu.make_async_remote_copy(
                src_ref=src_of(a, to), dst_ref=outs[a].at[_slot(frm)], send_sem=send_sems.at[a, k],
                recv_sem=recv_sems.at[a, k], device_id=to, device_id_type=MESH_ID)

        mine = [pltpu.make_async_copy(src_of(a, me), outs[a].at[_slot(me)], local_sems.at[a]) for a in range(n)]
        for cp in mine:
            cp.start()
        sends = [copy(a, k, me, peer) for a in range(n) for k, peer in enumerate(peers)]
        for cp in sends:
            cp.start()
        for a in range(n):
            for k, peer in enumerate(peers):
                copy(a, k, peer, me).wait_recv()
        for cp in sends:
            cp.wait_send()
        for cp in mine:
            cp.wait()

    return pl.pallas_call(
        body,
        name="exchange_grads",
        out_shape=[jax.ShapeDtypeStruct(s.shape, s.dtype) for s in blocked]
        + [jax.ShapeDtypeStruct((N_DEV,) + s.shape, s.dtype) for s in replicated],
        in_specs=[ANY] * n,
        out_specs=[ANY] * n,
        scratch_shapes=[pltpu.SemaphoreType.DMA((n, 7)), pltpu.SemaphoreType.DMA((n, 7)), pltpu.SemaphoreType.DMA((n,))],
    )(*arrays)


ADAMW_BLOCK_ELEMS = 256 * 1024


def _adamw(name, parts, w, m, v):
    r, c = w.shape
    tr = _tile(r, max(16, ADAMW_BLOCK_ELEMS // c), 16)

    def body(p_ref, w_ref, m_ref, v_ref, g_ref, d_ref, nm_ref, nv_ref):
        g = p_ref[0].astype(F32)
        for s in range(1, N_DEV):
            g = g + p_ref[s].astype(F32)
        m_new = ADAM_B1 * m_ref[...] + (1.0 - ADAM_B1) * g
        v_new = ADAM_B2 * v_ref[...] + (1.0 - ADAM_B2) * (g * g)
        m_hat = m_new / (1.0 - ADAM_B1 ** ADAM_STEP)
        v_hat = v_new / (1.0 - ADAM_B2 ** ADAM_STEP)
        g_ref[...] = g
        d_ref[...] = -ADAM_LR * (m_hat / (jnp.sqrt(v_hat) + ADAM_EPS) + ADAM_WD * w_ref[...])
        nm_ref[...] = m_new
        nv_ref[...] = v_new

    blk = pl.BlockSpec((tr, c), lambda i: (i, 0))
    return pl.pallas_call(
        body,
        name="adamw_" + name,
        out_shape=[jax.ShapeDtypeStruct((r, c), F32)] * 4,
        grid=(r // tr,),
        in_specs=[pl.BlockSpec((N_DEV, tr, c), lambda i: (0, i, 0)), blk, blk, blk],
        out_specs=[blk] * 4,
        compiler_params=_params(("parallel",)),
    )(parts, w, m, v)


HBM_SPEC = pl.BlockSpec(memory_space=pltpu.HBM)
SEM_SPEC = pl.BlockSpec(memory_space=pltpu.SEMAPHORE)
SIDE_EFFECT = pltpu.SideEffectType.DATAFLOW_SIDE_EFFECTING
N_PEERS = N_DEV - 1


def _peers(x, y, c):
    return [(x, y, 1 - c), (1 - x, y, c), (x, 1 - y, c), (1 - x, 1 - y, c),
            (1 - x, y, 1 - c), (x, 1 - y, 1 - c), (1 - x, 1 - y, 1 - c)]


def _split_copy(srcs, lands, blocked, send_sems, recv_sems, a, k, frm, to):
    src = srcs[a].at[_slot(to)] if blocked[a] else srcs[a]
    return pltpu.make_async_remote_copy(
        src_ref=src, dst_ref=lands[a].at[_slot(frm)], send_sem=send_sems.at[a * N_PEERS + k],
        recv_sem=recv_sems.at[a * N_PEERS + k],
        device_id=to, device_id_type=MESH_ID)


def _exchange_start(name, srcs, lands, blocked, after=()):
    n = len(srcs)
    after = list(after)

    def body(*refs):
        src_refs, land_refs = refs[:n], refs[n:2 * n]
        send_sems, recv_sems = refs[2 * n + len(after)], refs[2 * n + len(after) + 1]
        token = refs[-1]
        x, y, c = lax.axis_index("x"), lax.axis_index("y"), lax.axis_index("c")
        for a in range(n):
            for k, peer in enumerate(_peers(x, y, c)):
                _split_copy(src_refs, land_refs, blocked, send_sems, recv_sems, a, k, (x, y, c), peer).start()
        token[...] = jnp.zeros_like(token)

    thru = [pltpu.HBM(s.shape, s.dtype) for s in list(srcs) + list(lands)]
    res = pl.pallas_call(
        body,
        name=name,
        out_shape=(pltpu.SemaphoreType.DMA((n * N_PEERS,)), pltpu.SemaphoreType.DMA((n * N_PEERS,)), *thru,
                   jax.ShapeDtypeStruct((8, LANE), F32)),
        in_specs=[HBM_SPEC] * (2 * n) + [pl.BlockSpec(memory_space=pl.ANY)] * len(after),
        out_specs=(SEM_SPEC, SEM_SPEC, *([HBM_SPEC] * (2 * n)), pl.BlockSpec(memory_space=pltpu.VMEM)),
        input_output_aliases={i: 2 + i for i in range(2 * n)},
        compiler_params=pltpu.CompilerParams(has_side_effects=SIDE_EFFECT),
    )(*[pltpu.with_memory_space_constraint(s, pltpu.HBM) for s in list(srcs) + list(lands)], *after)
    return res[0], res[1], res[2:2 + n], res[2 + n:2 + 2 * n], res[-1]


def _exchange_wait(name, send_sems, recv_sems, srcs, lands, blocked, after):
    n, n_after = len(srcs), len(after)

    def body(*refs):
        src_refs, land_refs = refs[:n], refs[n:2 * n]
        send, recv = refs[2 * n], refs[2 * n + 1]
        x, y, c = lax.axis_index("x"), lax.axis_index("y"), lax.axis_index("c")
        for a in range(n):
            for k, peer in enumerate(_peers(x, y, c)):
                _split_copy(src_refs, land_refs, blocked, send, recv, a, k, (x, y, c), peer).wait_send()
                _split_copy(src_refs, land_refs, blocked, send, recv, a, k, peer, (x, y, c)).wait_recv()

    res = pl.pallas_call(
        body,
        name=name,
        out_shape=tuple(pltpu.HBM(s.shape, s.dtype) for s in list(srcs) + list(lands)),
        in_specs=[HBM_SPEC] * (2 * n) + [SEM_SPEC, SEM_SPEC] + [pl.BlockSpec(memory_space=pl.ANY)] * n_after,
        out_specs=tuple([HBM_SPEC] * (2 * n)),
        input_output_aliases={i: i for i in range(2 * n)},
        compiler_params=pltpu.CompilerParams(has_side_effects=SIDE_EFFECT),
    )(*srcs, *lands, send_sems, recv_sems, *after)
    return res[n:]


class _LazyWeights:
    def __init__(self):
        self.ready, self.groups, self.hints = {}, {}, {}

    def add_group(self, wait_name, names, send, recv, srcs, lands):
        for n in names:
            self.groups[n] = (wait_name, names, send, recv, srcs, lands)

    def hint(self, name, after):
        self.hints[self.groups[name][0]] = after

    def __getitem__(self, name):
        if name not in self.ready:
            wait_name, names, send, recv, srcs, lands = self.groups[name]
            after = [self.hints[wait_name]] if wait_name in self.hints else []
            whole = _exchange_wait(wait_name, send, recv, srcs, lands, [False] * len(names), after)
            for n, stacked in zip(names, whole):
                self.ready[n] = _to_kernel_layout({n: _unshard(n, stacked)})[n]
        return self.ready[name]


def kernel(x, positions, meta_tokens, w_in, w_q_up, w_kv_up, w_branch_mla, w_branch_hgrn, w_out, w_ffn_in, w_ffn_out, conv_w, conv_b, g_mix_norm, g_q_norm, g_kv_norm, g_hgrn_norm, g_ffn_norm, g_final_norm, lb_raw, loss_target, m_meta_tokens, m_w_in, m_w_q_up, m_w_kv_up, m_w_branch_mla, m_w_branch_hgrn, m_w_out, m_w_ffn_in, m_w_ffn_out, m_conv_w, m_conv_b, m_g_mix_norm, m_g_q_norm, m_g_kv_norm, m_g_hgrn_norm, m_g_ffn_norm, m_g_final_norm, m_lb_raw, v_meta_tokens, v_w_in, v_w_q_up, v_w_kv_up, v_w_branch_mla, v_w_branch_hgrn, v_w_out, v_w_ffn_in, v_w_ffn_out, v_conv_w, v_conv_b, v_g_mix_norm, v_g_q_norm, v_g_kv_norm, v_g_hgrn_norm, v_g_ffn_norm, v_g_final_norm, v_lb_raw):
    local = dict(zip(
        ("meta_tokens", "w_in", "w_q_up", "w_kv_up", "w_branch_mla", "w_branch_hgrn", "w_out", "w_ffn_in", "w_ffn_out",
         "conv_w", "conv_b", "g_mix_norm", "g_q_norm", "g_kv_norm", "g_hgrn_norm", "g_ffn_norm", "g_final_norm", "lb_raw"),
        (meta_tokens, w_in, w_q_up, w_kv_up, w_branch_mla, w_branch_hgrn, w_out, w_ffn_in, w_ffn_out,
         conv_w, conv_b, g_mix_norm, g_q_norm, g_kv_norm, g_hgrn_norm, g_ffn_norm, g_final_norm, lb_raw)))
    mom_m = dict(zip(local, (m_meta_tokens, m_w_in, m_w_q_up, m_w_kv_up, m_w_branch_mla, m_w_branch_hgrn, m_w_out, m_w_ffn_in,
                             m_w_ffn_out, m_conv_w, m_conv_b, m_g_mix_norm, m_g_q_norm, m_g_kv_norm, m_g_hgrn_norm,
                             m_g_ffn_norm, m_g_final_norm, m_lb_raw)))
    mom_v = dict(zip(local, (v_meta_tokens, v_w_in, v_w_q_up, v_w_kv_up, v_w_branch_mla, v_w_branch_hgrn, v_w_out, v_w_ffn_in,
                             v_w_ffn_out, v_conv_w, v_conv_b, v_g_mix_norm, v_g_q_norm, v_g_kv_norm, v_g_hgrn_norm,
                             v_g_ffn_norm, v_g_final_norm, v_lb_raw)))
    sharded = BIG + ("conv_w", "meta_tokens")

    def shard2d(name, arr):
        return arr.reshape(arr.shape[-2:]) if name != "meta_tokens" else arr

    def as2d(name, arr):
        return arr.reshape(1, -1) if arr.ndim == 1 else shard2d(name, arr)

    me = 4 * lax.axis_index("x") + 2 * lax.axis_index("y") + lax.axis_index("c")

    def landing(own):
        zone = lax.empty((N_DEV,) + own.shape[1:], own.dtype)
        return lax.dynamic_update_slice_in_dim(zone, own, me, 0)

    shards = {n: shard2d(n, local[n]).astype(BF16) for n in BIG}
    shards.update({n: shard2d(n, local[n]) for n in ("conv_w", "meta_tokens")})
    full = _LazyWeights()
    first = ("w_in", "meta_tokens")
    gathered = _all_gather([shards[n] for n in first])
    for n, g in zip(first, gathered):
        full.ready[n] = _to_kernel_layout({n: _unshard(n, g)})[n]
    later = (("w_q_up", "w_kv_up"), ("w_branch_mla", "w_branch_hgrn", "w_out", "w_ffn_in", "w_ffn_out", "conv_w"))
    for k, names in enumerate(later):
        srcs = [shards[n] for n in names]
        send, recv, srcs_thru, lands_thru, _ = _exchange_start(
            f"gather_start_{k}", srcs, [landing(s[None]) for s in srcs], [False] * len(names), after=[gathered[0]])
        full.add_group(f"gather_wait_{k}", names, send, recv, srcs_thru, lands_thru)
    small = {n: local[n] for n in SMALL}

    started = []

    def sources(group):
        group = _from_kernel_layout(group)
        names = list(group)
        blocked = [n in sharded for n in names]
        srcs = [_reshard(n, group[n]) if b else as2d(n, group[n]) for n, b in zip(names, blocked)]
        return names, blocked, srcs

    def emit(group):
        names, blocked, srcs = sources(group)
        lands = [landing(lax.dynamic_index_in_dim(s, me, 0, keepdims=True) if b else s[None]) for s, b in zip(srcs, blocked)]
        k = len(started)
        send, recv, srcs_thru, lands_thru, token = _exchange_start(f"exchange_start_{k}", srcs, lands, blocked)
        started.append((names, blocked, send, recv, srcs_thru, lands_thru))
        return token

    loss, grad_x, last = _local_step(x[0], positions[0], loss_target[0], full, small, emit)

    out = {}

    def update(names, parts):
        for n, part in zip(names, parts):
            res = _adamw(n, part, as2d(n, local[n]), as2d(n, mom_m[n]), as2d(n, mom_v[n]))
            out[n] = [r.reshape(local[n].shape) for r in res]

    after = [grad_x]
    for k, (names, blocked, send, recv, srcs_thru, lands_thru) in enumerate(started):
        update(names, _exchange_wait(f"exchange_wait_{k}", send, recv, srcs_thru, lands_thru, blocked, after))
        after = [out[names[0]][0]]
    names, blocked, srcs = sources(last)
    in_blocks = [(n, s) for n, s, b in zip(names, srcs, blocked) if b]
    whole = [(n, s) for n, s, b in zip(names, srcs, blocked) if not b]
    update([n for n, _ in in_blocks + whole], _exchange([s for _, s in in_blocks], [s for _, s in whole]))

    loss = lax.psum(loss, ("x", "y", "c"))
    order = tuple(local)
    return (loss, grad_x[None], *[out[n][0] for n in order], *[out[n][1] for n in order],
            *[out[n][2] for n in order], *[out[n][3] for n in order])
```

```python
import functools

import jax
import jax.numpy as jnp
import numpy as np
from jax import lax
from jax.experimental import pallas as pl
from jax.experimental.pallas import tpu as pltpu

F32 = jnp.float32
BF16 = jnp.bfloat16

D_MODEL = 2048
N_META = 16
BLOCK = 128
PAD_LEN = BLOCK - N_META
HEADS = 16
Q_LORA = 1536
KV_LORA = 512
ROPE = 64
NOPE = 128
VDIM = 128
D_FF = 5632
NORM_EPS = 1e-6
ROPE_THETA = 10000.0
ATTN_SCALE = (NOPE + ROPE) ** -0.5
ADAM_LR = 0.001
ADAM_B1 = 0.9
ADAM_B2 = 0.999
ADAM_EPS = 1e-08
ADAM_WD = 0.01
ADAM_STEP = 10
N_DEV = 8

LANE = 128
SEG_Q_LAT = 0
SEG_KV_LAT = Q_LORA
SEG_HQ = 2048
SEG_HF = SEG_HQ + D_MODEL
SEG_HI = SEG_HF + D_MODEL
SEG_HG = SEG_HI + D_MODEL
SEG_GA = SEG_HG + D_MODEL
SEG_GB = SEG_GA + D_MODEL
SEG_KR = SEG_GB + D_MODEL
KR_W = 256
PROJ_W = SEG_KR + KR_W
QHEAD_W = 256

V7X_VMEM_BYTES = 64 * 1024 * 1024
VMEM_LIMIT = V7X_VMEM_BYTES * 7 // 8
NEG_BIG = -1e30
SUB = 8


def _tile(n, target, mult):
    best = None
    for t in range(mult, min(n, target) + 1, mult):
        if n % t == 0:
            best = t
    return n if best is None else best


def _params(sem):
    return pltpu.CompilerParams(dimension_semantics=sem, vmem_limit_bytes=VMEM_LIMIT)


def _sigmoid(x):
    return 1.0 / (1.0 + jnp.exp(-x))


_DIMS = {"nn": (((1,), (0,)), ((), ())), "nt": (((1,), (1,)), ((), ())), "tn": (((0,), (0,)), ((), ()))}


def _matmul(a, b, mode, out_dtype, name, after=None):
    if mode == "nn":
        (m, k), (_, n) = a.shape, b.shape
    elif mode == "nt":
        (m, k), (n, _) = a.shape, b.shape
    else:
        (k, m), (_, n) = a.shape, b.shape
    tm = _tile(m, 1040, 8) if mode != "tn" else _tile(m, 1024, LANE)
    tn = _tile(n, 1024, LANE)
    tk = _tile(k, 2816, LANE) if mode != "tn" else _tile(k, 2080, 8)
    nk = k // tk
    if mode == "nn":
        a_spec = pl.BlockSpec((tm, tk), lambda i, j, kk: (i, kk))
        b_spec = pl.BlockSpec((tk, tn), lambda i, j, kk: (kk, j))
    elif mode == "nt":
        a_spec = pl.BlockSpec((tm, tk), lambda i, j, kk: (i, kk))
        b_spec = pl.BlockSpec((tn, tk), lambda i, j, kk: (j, kk))
    else:
        a_spec = pl.BlockSpec((tk, tm), lambda i, j, kk: (kk, i))
        b_spec = pl.BlockSpec((tk, tn), lambda i, j, kk: (kk, j))
    dims = _DIMS[mode]

    n_after = 0 if after is None else 1

    def body(a_ref, b_ref, *rest):
        o_ref, acc = rest[n_after], rest[n_after + 1:]
        part = lax.dot_general(a_ref[...], b_ref[...], dims, preferred_element_type=F32)
        if nk == 1:
            o_ref[...] = part.astype(o_ref.dtype)
            return
        acc_ref, kk = acc[0], pl.program_id(2)

        @pl.when(kk == 0)
        def _():
            acc_ref[...] = part

        @pl.when((kk > 0) & (kk < nk - 1))
        def _():
            acc_ref[...] += part

        @pl.when(kk == nk - 1)
        def _():
            o_ref[...] = (acc_ref[...] + part).astype(o_ref.dtype)

    return pl.pallas_call(
        body,
        name=name,
        out_shape=jax.ShapeDtypeStruct((m, n), out_dtype),
        grid=(m // tm, n // tn, nk),
        in_specs=[a_spec, b_spec] + [pl.BlockSpec(memory_space=pl.ANY)] * n_after,
        out_specs=pl.BlockSpec((tm, tn), lambda i, j, kk: (i, j)),
        scratch_shapes=[pltpu.VMEM((tm, tn), F32)] if nk > 1 else [],
        compiler_params=_params(("parallel", "parallel", "arbitrary")),
    )(a, b, *([after] * n_after))


ROW_WINDOW_BYTES = 12 * 1024 * 1024


def _rowwise(name, fn, ins, outs, rows, tm, after=None):
    per_row = sum(s[2] * s[1].dtype.itemsize for s in ins if s[0] == "row")
    per_row += sum(s[1] * jnp.dtype(s[2]).itemsize for s in outs if s[0] == "row")
    if per_row:
        tm = _tile(rows, min(tm, max(8, ROW_WINDOW_BYTES // (2 * per_row))), 8)
    n_in = len(ins)
    in_specs, args = [], []
    for spec in ins:
        if spec[0] == "row":
            _, arr, w, cb = spec
            in_specs.append(pl.BlockSpec((tm, w), functools.partial(lambda i, cb: (i, cb), cb=cb)))
        else:
            arr = spec[1]
            in_specs.append(pl.BlockSpec(arr.shape, lambda i: (0, 0)))
        args.append(arr)
    out_shape, out_specs = [], []
    for spec in outs:
        if spec[0] == "row":
            out_shape.append(jax.ShapeDtypeStruct((rows, spec[1]), spec[2]))
            out_specs.append(pl.BlockSpec((tm, spec[1]), lambda i: (i, 0)))
        else:
            out_shape.append(jax.ShapeDtypeStruct(spec[1], F32))
            out_specs.append(pl.BlockSpec(spec[1], lambda i: (0, 0)))
    has_acc = any(s[0] == "acc" for s in outs)
    n_after = 0 if after is None else 1
    in_specs += [pl.BlockSpec(memory_space=pl.ANY)] * n_after
    args += [after] * n_after

    def body(*refs):
        i = pl.program_id(0)
        res = fn(i, tm, *[r[...] for r in refs[:n_in]])
        for spec, ref, val in zip(outs, refs[n_in + n_after:], res):
            if spec[0] == "row":
                ref[...] = val.astype(ref.dtype)
            else:
                @pl.when(i == 0)
                def _(ref=ref, val=val):
                    ref[...] = val

                @pl.when(i > 0)
                def _(ref=ref, val=val):
                    ref[...] += val

    return pl.pallas_call(
        body,
        name=name,
        out_shape=out_shape,
        grid=(rows // tm,),
        in_specs=in_specs,
        out_specs=out_specs,
        compiler_params=_params(("arbitrary" if has_acc else "parallel",)),
    )(*args)


def _row_ids(i, tm, shape):
    return i * tm + lax.broadcasted_iota(jnp.int32, shape, 0)


def _rms_fwd(x, g):
    r = lax.rsqrt(jnp.mean(x * x, axis=-1, keepdims=True) + NORM_EPS)
    return x * r * g


def _rms_bwd(x, g, dy):
    r = lax.rsqrt(jnp.mean(x * x, axis=-1, keepdims=True) + NORM_EPS)
    xhat = x * r
    dxhat = dy * g
    dx = r * (dxhat - xhat * jnp.mean(dxhat * xhat, axis=-1, keepdims=True))
    return dx, jnp.sum(dy * xhat, axis=0, keepdims=True)


def _silu(x):
    return x * _sigmoid(x)


def _dsilu(x):
    s = _sigmoid(x)
    return s * (1.0 + x * (1.0 - s))


def _rot_src(x):
    lane = lax.broadcasted_iota(jnp.int32, x.shape, 1)
    return jnp.where(lane < ROPE // 2, pltpu.roll(x, LANE - ROPE // 2, 1), pltpu.roll(x, ROPE // 2, 1))


def _rope_fwd_call(q_raw, kv, proj, cos_t, sin_t, rows, tm):
    def fn(i, tm_, q, kvv, kr, c, s):
        kr_rot = kr[:, :LANE]
        kr_rot = kr_rot * c + _rot_src(kr_rot) * s
        qs, ks, vs = [], [], []
        for h in range(HEADS):
            qn = q[:, h * QHEAD_W:h * QHEAD_W + NOPE]
            qr = q[:, h * QHEAD_W + NOPE:(h + 1) * QHEAD_W]
            qs += [qn * SCORE_TO_LOG2, (qr * c + _rot_src(qr) * s) * SCORE_TO_LOG2]
            ks += [kvv[:, h * 2 * NOPE:h * 2 * NOPE + NOPE], kr_rot]
            vs += [kvv[:, h * 2 * NOPE + NOPE:(h + 1) * 2 * NOPE]]
        return jnp.concatenate(qs, axis=1), jnp.concatenate(ks, axis=1), jnp.concatenate(vs, axis=1)

    return _rowwise(
        "rope_fwd", fn,
        [("row", q_raw, HEADS * QHEAD_W, 0), ("row", kv, HEADS * 2 * NOPE, 0), ("row", proj, KR_W, SEG_KR // KR_W),
         ("row", cos_t, LANE, 0), ("row", sin_t, LANE, 0)],
        [("row", HEADS * QHEAD_W, BF16), ("row", HEADS * QHEAD_W, BF16), ("row", HEADS * VDIM, BF16)],
        rows, tm)


def _rope_bwd_call(dq_att, dk_att, dv, cos_t, sin_t, rows, tm):
    def fn(i, tm_, dq, dk, dvv, c, s):
        qs, kvs = [], []
        dkr = jnp.zeros((dq.shape[0], LANE), F32)
        for h in range(HEADS):
            dqr = dq[:, h * QHEAD_W + NOPE:(h + 1) * QHEAD_W] * ATTN_SCALE
            qs += [dq[:, h * QHEAD_W:h * QHEAD_W + NOPE] * ATTN_SCALE, dqr * c - _rot_src(dqr) * s]
            kvs += [dk[:, h * QHEAD_W:h * QHEAD_W + NOPE], dvv[:, h * VDIM:(h + 1) * VDIM]]
            dkr = dkr + dk[:, h * QHEAD_W + NOPE:(h + 1) * QHEAD_W]
        dkr = dkr * c - _rot_src(dkr) * s
        return (jnp.concatenate(qs, axis=1), jnp.concatenate(kvs, axis=1),
                jnp.concatenate([dkr, jnp.zeros_like(dkr)], axis=1))

    return _rowwise(
        "rope_bwd", fn,
        [("row", dq_att, HEADS * QHEAD_W, 0), ("row", dk_att, HEADS * QHEAD_W, 0), ("row", dv, HEADS * VDIM, 0),
         ("row", cos_t, LANE, 0), ("row", sin_t, LANE, 0)],
        [("row", HEADS * QHEAD_W, BF16), ("row", HEADS * 2 * NOPE, BF16), ("row", KR_W, BF16)],
        rows, tm)


def _attn_mask(q_blk, k_blk, t, keys_on_rows=False):
    qa, ka = (1, 0) if keys_on_rows else (0, 1)
    qs = q_blk * t + lax.broadcasted_iota(jnp.int32, (t, t), qa)
    ks = k_blk * t + lax.broadcasted_iota(jnp.int32, (t, t), ka)
    return (ks <= qs) & ((ks >= PAD_LEN) | (ks == qs))


_NT = _DIMS["nt"]
_TN = _DIMS["tn"]
LOG2E = 1.4426950408889634
SCORE_TO_LOG2 = ATTN_SCALE * LOG2E


def _causal_pairs(nb, by_key):
    if by_key:
        pairs = [(qi, kj) for kj in range(nb) for qi in range(kj, nb)]
    else:
        pairs = [(qi, kj) for qi in range(nb) for kj in range(qi + 1)]
    return (jnp.asarray(np.array([p[0] for p in pairs], np.int32)), jnp.asarray(np.array([p[1] for p in pairs], np.int32)))


def _two_parts(t):
    cut = (t // LANE + 1) // 2 * LANE
    return ((0, cut), (cut, t)) if cut < t else ((0, t),)


def _attn_fwd(q_att, k_att, v, rows):
    t = _tile(rows, 640, LANE)
    nb = rows // t

    def body(qt_ref, kt_ref, q_ref, k_ref, v_ref, o32_ref, obf_ref, lse_ref, m_sc, l_sc, acc_sc):
        qi, kj = qt_ref[pl.program_id(1)], kt_ref[pl.program_id(1)]

        @pl.when(kj == 0)
        def _():
            m_sc[...] = jnp.full_like(m_sc, NEG_BIG)
            l_sc[...] = jnp.zeros_like(l_sc)
            acc_sc[...] = jnp.zeros_like(acc_sc)

        def step(masked):
            q = q_ref[...]
            parts = _two_parts(t)
            scores =[lax.dot_general(q, k_ref[lo:hi, :], _NT, preferred_element_type=F32) for lo, hi in parts]
            m, l, acc = m_sc[...], l_sc[...], acc_sc[...]
            for (lo, hi), s in zip(parts, scores):
                if masked:
                    qs = qi * t + lax.broadcasted_iota(jnp.int32, (t, hi - lo), 0)
                    ks = kj * t + lo + lax.broadcasted_iota(jnp.int32, (t, hi - lo), 1)
                    s = jnp.where((ks <= qs) & ((ks >= PAD_LEN) | (ks == qs)), s, NEG_BIG)
                m_new = jnp.maximum(m, jnp.max(s, axis=1, keepdims=True))
                alpha = jnp.exp2(m - m_new)
                p = jnp.exp2(s - jnp.tile(m_new, (1, (hi - lo) // LANE)))
                l = alpha * l + jnp.sum(p, axis=1, keepdims=True)
                acc = alpha * acc + jnp.dot(p.astype(BF16), v_ref[lo:hi, :], preferred_element_type=F32)
                m = m_new
            m_sc[...], l_sc[...], acc_sc[...] = m, l, acc

        pl.when((kj == qi) | (kj == 0))(functools.partial(step, True))
        pl.when((kj < qi) & (kj > 0))(functools.partial(step, False))

        @pl.when(kj == qi)
        def _():
            o = acc_sc[...] / l_sc[...]
            o32_ref[...] = o
            obf_ref[...] = o.astype(BF16)
            lse_ref[0] = m_sc[:, 0:1] + jnp.log2(l_sc[:, 0:1])

    qt, kt = _causal_pairs(nb, by_key=False)
    qmap = lambda h, p, qt_ref, kt_ref: (qt_ref[p], h)
    kmap = lambda h, p, qt_ref, kt_ref: (kt_ref[p], h)
    return pl.pallas_call(
        body,
        name="attn_fwd",
        out_shape=[jax.ShapeDtypeStruct((rows, HEADS * VDIM), F32), jax.ShapeDtypeStruct((rows, HEADS * VDIM), BF16),
                   jax.ShapeDtypeStruct((HEADS, rows, 1), F32)],
        grid_spec=pltpu.PrefetchScalarGridSpec(
            num_scalar_prefetch=2,
            grid=(HEADS, len(qt)),
            in_specs=[pl.BlockSpec((t, QHEAD_W), qmap), pl.BlockSpec((t, QHEAD_W), kmap), pl.BlockSpec((t, VDIM), kmap)],
            out_specs=[pl.BlockSpec((t, VDIM), qmap), pl.BlockSpec((t, VDIM), qmap),
                       pl.BlockSpec((1, t, 1), lambda h, p, qt_ref, kt_ref: (h, qt_ref[p], 0))],
            scratch_shapes=[pltpu.VMEM((t, LANE), F32), pltpu.VMEM((t, LANE), F32), pltpu.VMEM((t, VDIM), F32)]),
        compiler_params=_params(("parallel", "arbitrary")),
    )(qt, kt, q_att, k_att, v)


def _attn_delta(do, o32, rows, tm):
    def fn(i, tm_, dov, ov):
        prod = dov.astype(F32) * ov
        head_of = lax.broadcasted_iota(jnp.int32, (HEADS * VDIM, LANE), 0) // VDIM
        pick = jnp.where(head_of == lax.broadcasted_iota(jnp.int32, (HEADS * VDIM, LANE), 1), 1.0, 0.0).astype(F32)
        return (jnp.dot(prod, pick, precision=lax.Precision.HIGHEST, preferred_element_type=F32),)

    (delta,) = _rowwise("attn_delta", fn, [("row", do, HEADS * VDIM, 0), ("row", o32, HEADS * VDIM, 0)],
                        [("row", LANE, F32)], rows, tm)
    return delta


def _attn_bwd(q_att, k_att, v, do, lse_row, delta_row, rows):
    t = _tile(rows, 640, LANE)
    nb = rows // t

    def body(qt_ref, kt_ref, q_ref, k_ref, v_ref, do_ref, lse_ref, delta_ref, dq_ref, dk_ref, dv_ref, dk_sc, dv_sc):
        qi, kj = qt_ref[pl.program_id(1)], kt_ref[pl.program_id(1)]

        @pl.when(pl.program_id(1) == 0)
        def _():
            dq_ref[...] = jnp.zeros_like(dq_ref)

        @pl.when(qi == kj)
        def _():
            dk_sc[...] = jnp.zeros_like(dk_sc)
            dv_sc[...] = jnp.zeros_like(dv_sc)

        def step(masked):
            k, vv = k_ref[...], v_ref[...]
            parts = _two_parts(t)
            st_all = [lax.dot_general(k, q_ref[lo:hi, :], _NT, preferred_element_type=F32) for lo, hi in parts]
            dpt_all = [lax.dot_general(vv, do_ref[lo:hi, :], _NT, preferred_element_type=F32) for lo, hi in parts]
            dk, dv = dk_sc[...], dv_sc[...]
            for (lo, hi), st, dpt in zip(parts, st_all, dpt_all):
                pt = jnp.exp2(st - lse_ref[0, :, lo:hi])
                if masked:
                    ks = kj * t + lax.broadcasted_iota(jnp.int32, (t, hi - lo), 0)
                    qs = qi * t + lo + lax.broadcasted_iota(jnp.int32, (t, hi - lo), 1)
                    pt = jnp.where((ks <= qs) & ((ks >= PAD_LEN) | (ks == qs)), pt, 0.0)
                dv = dv + jnp.dot(pt.astype(BF16), do_ref[lo:hi, :], preferred_element_type=F32)
                dst = (pt * (dpt - delta_ref[0, :, lo:hi])).astype(BF16)
                dk = dk + jnp.dot(dst, q_ref[lo:hi, :], preferred_element_type=F32)
                q_rows = pl.ds(pl.multiple_of(qi * t + lo, LANE), hi - lo)
                dq_ref[q_rows, :] += lax.dot_general(dst, k, _TN, preferred_element_type=F32)
            dk_sc[...], dv_sc[...] = dk, dv

        pl.when((qi == kj) | (kj == 0))(functools.partial(step, True))
        pl.when((qi > kj) & (kj > 0))(functools.partial(step, False))

        @pl.when(qi == nb - 1)
        def _():
            dk_ref[...] = dk_sc[...] * (1.0 / LOG2E)
            dv_ref[...] = dv_sc[...]

    qt, kt = _causal_pairs(nb, by_key=True)
    qmap = lambda h, p, qt_ref, kt_ref: (qt_ref[p], h)
    kmap = lambda h, p, qt_ref, kt_ref: (kt_ref[p], h)
    stat = pl.BlockSpec((1, 1, t), lambda h, p, qt_ref, kt_ref: (h, 0, qt_ref[p]))
    return pl.pallas_call(
        body,
        name="attn_bwd",
        out_shape=[jax.ShapeDtypeStruct((rows, HEADS * QHEAD_W), F32), jax.ShapeDtypeStruct((rows, HEADS * QHEAD_W), F32),
                   jax.ShapeDtypeStruct((rows, HEADS * VDIM), F32)],
        grid_spec=pltpu.PrefetchScalarGridSpec(
            num_scalar_prefetch=2,
            grid=(HEADS, len(qt)),
            in_specs=[pl.BlockSpec((t, QHEAD_W), qmap), pl.BlockSpec((t, QHEAD_W), kmap), pl.BlockSpec((t, VDIM), kmap),
                      pl.BlockSpec((t, VDIM), qmap), stat, stat],
            out_specs=[pl.BlockSpec((rows, QHEAD_W), lambda h, p, qt_ref, kt_ref: (0, h)),
                       pl.BlockSpec((t, QHEAD_W), kmap), pl.BlockSpec((t, VDIM), kmap)],
            scratch_shapes=[pltpu.VMEM((t, QHEAD_W), F32), pltpu.VMEM((t, VDIM), F32)]),
        compiler_params=_params(("parallel", "arbitrary")),
    )(qt, kt, q_att, k_att, v, do, lse_row, delta_row)


C = BLOCK


def _hgrn_prep(hq, hf, hi, lb, c):
    rows = c * C + lax.broadcasted_iota(jnp.int32, (C, C), 0)
    valid = rows >= PAD_LEN
    sg = _sigmoid(hf)
    f = lb + (1.0 - lb) * sg
    g = jnp.where(valid, jnp.log(f), 0.0)
    k = jnp.where(valid, 1.0 - f, 0.0)
    q = _silu(hq)
    r = lax.broadcasted_iota(jnp.int32, (C, C), 0)
    cc = lax.broadcasted_iota(jnp.int32, (C, C), 1)
    tri = jnp.where(cc <= r, 1.0, 0.0).astype(F32)
    b = jnp.dot(tri, g, precision=lax.Precision.HIGHEST, preferred_element_type=F32)
    return q, k, hi, b, f, sg, valid


def _last_row_as_col(b_t):
    lane = lax.broadcasted_iota(jnp.int32, b_t.shape, 1)
    return jnp.sum(jnp.where(lane == C - 1, b_t, 0.0), axis=1, keepdims=True)


def _k_scaled(k, b, bs):
    return (k * jnp.exp(jnp.minimum(bs - b, 0.0))).astype(BF16)


def _hgrn_fwd(proj, lb, rows):
    nc = rows // C

    def body(hq_ref, hf_ref, hi_ref, lb_ref, o_ref, a_ref, s_ref, s_sc, b_sc):
        c = pl.program_id(1)

        @pl.when(c == 0)
        def _():
            s_sc[...] = jnp.zeros_like(s_sc)

        q, k, v, b, _, _, _ = _hgrn_prep(hq_ref[...], hf_ref[...], hi_ref[...], lb_ref[...], c)
        b_sc[...] = b
        s0 = s_sc[...]
        s_ref[0, 0] = s0
        v_bf = v.astype(BF16)
        r16 = lax.broadcasted_iota(jnp.int32, (SUB, C), 0)
        c16 = lax.broadcasted_iota(jnp.int32, (SUB, C), 1)
        slabs = [jnp.zeros((SUB, C), F32)]
        for i in range(1, C // SUB):
            bs = b_sc[SUB * i - 1:SUB * i, :]
            qs = (q[SUB * i:SUB * (i + 1)] * jnp.exp(b[SUB * i:SUB * (i + 1)] - bs)).astype(BF16)
            a_i = lax.dot_general(qs, _k_scaled(k, b, bs), _NT, preferred_element_type=F32)
            slabs.append(jnp.where(c16 <= r16 + (SUB * i - SUB), a_i, 0.0))
        a_off = jnp.concatenate(slabs, axis=0)
        q_t, k_t, b_t = q.T, k.T, b.T
        sub = lax.broadcasted_iota(jnp.int32, (C, C), 0)
        lane = lax.broadcasted_iota(jnp.int32, (C, C), 1)
        lane1 = lax.broadcasted_iota(jnp.int32, (1, C), 1)
        at_band = jnp.zeros((C, C), F32)
        ahead = lane - sub
        for dl in range(SUB):
            k_s = pltpu.roll(k_t, dl, 1) if dl else k_t
            b_s = pltpu.roll(b_t, dl, 1) if dl else b_t
            e = jnp.exp(b_t - b_s)
            band = jnp.sum(q_t * k_s * e, axis=0, keepdims=True)
            band = jnp.where(lane1 >= dl, band, 0.0)
            at_band = at_band + jnp.where(ahead == dl, jnp.broadcast_to(band, (C, C)), 0.0)
        a = (a_off + at_band.T).astype(BF16)
        a_ref[0] = a
        qe = (q * jnp.exp(b)).astype(BF16)
        o_ref[...] = (jnp.dot(a, v_bf, preferred_element_type=F32)
                      + jnp.dot(qe, s0.astype(BF16), preferred_element_type=F32))
        b_last = b_sc[C - 1:C, :]
        kd = (k * jnp.exp(b_last - b)).astype(BF16)
        s_sc[...] = (jnp.exp(_last_row_as_col(b_t)) * s0
                     + lax.dot_general(kd, v_bf, _TN, preferred_element_type=F32))

    seg = lambda base: (lambda h, c: (c, base // C + h))
    return pl.pallas_call(
        body,
        name="hgrn_fwd",
        out_shape=[jax.ShapeDtypeStruct((rows, D_MODEL), F32), jax.ShapeDtypeStruct((HEADS, rows, C), BF16),
                   jax.ShapeDtypeStruct((HEADS, nc, C, C), F32)],
        grid=(HEADS, nc),
        in_specs=[pl.BlockSpec((C, C), seg(SEG_HQ)), pl.BlockSpec((C, C), seg(SEG_HF)), pl.BlockSpec((C, C), seg(SEG_HI)),
                  pl.BlockSpec((1, C), lambda h, c: (0, h))],
        out_specs=[pl.BlockSpec((C, C), lambda h, c: (c, h)), pl.BlockSpec((1, C, C), lambda h, c: (h, c, 0)),
                   pl.BlockSpec((1, 1, C, C), lambda h, c: (h, c, 0, 0))],
        scratch_shapes=[pltpu.VMEM((C, C), F32), pltpu.VMEM((C, C), F32)],
        compiler_params=_params(("parallel", "arbitrary")),
    )(proj, proj, proj, lb)


def _hgrn_bwd(proj, lb, a_mat, s_states, do_h, rows):
    nc = rows // C

    def body(hq_ref, hf_ref, hi_ref, lb_ref, a_ref, s_ref, do_ref, dhq_ref, dhf_ref, dhi_ref, dlb_ref, ds_sc, b_sc):
        step = pl.program_id(1)
        c = nc - 1 - step

        @pl.when(step == 0)
        def _():
            ds_sc[...] = jnp.zeros_like(ds_sc)
            dlb_ref[...] = jnp.zeros_like(dlb_ref)

        hq, hf = hq_ref[...], hf_ref[...]
        lb_row = lb_ref[...]
        q, k, v, b, f, sg, valid = _hgrn_prep(hq, hf, hi_ref[...], lb_row, c)
        b_sc[...] = b
        s0 = s_ref[0, 0]
        ds1 = ds_sc[...]
        s0_bf, ds1_bf = s0.astype(BF16), ds1.astype(BF16)
        do = do_ref[...]
        do_bf, v_bf = do.astype(BF16), v.astype(BF16)
        b_last = b_sc[C - 1:C, :]
        e_last = jnp.exp(b_last - b)
        eb = jnp.exp(b)
        sub = lax.broadcasted_iota(jnp.int32, (C, C), 0)
        lane = lax.broadcasted_iota(jnp.int32, (C, C), 1)
        r16 = lax.broadcasted_iota(jnp.int32, (SUB, C), 0)
        c16 = lax.broadcasted_iota(jnp.int32, (SUB, C), 1)

        dv = (lax.dot_general(a_ref[0], do_bf, _TN, preferred_element_type=F32)
              + jnp.dot((k * e_last).astype(BF16), ds1_bf, preferred_element_type=F32))
        da = jnp.where(lane <= sub, lax.dot_general(do_bf, v_bf, _NT, preferred_element_type=F32), 0.0)
        da_t = jnp.where(sub <= lane, lax.dot_general(v_bf, do_bf, _NT, preferred_element_type=F32), 0.0)

        dq_slabs = [jnp.zeros((SUB, C), F32)]
        for i in range(1, C // SUB):
            bs = b_sc[SUB * i - 1:SUB * i, :]
            da_i = jnp.where(c16 <= r16 + (SUB * i - SUB), da[SUB * i:SUB * (i + 1)], 0.0).astype(BF16)
            dq_slabs.append(jnp.exp(b[SUB * i:SUB * (i + 1)] - bs)
                            * jnp.dot(da_i, _k_scaled(k, b, bs), preferred_element_type=F32))
        dk_slabs = []
        for j in range(C // SUB - 1):
            be = b_sc[SUB * j + SUB - 1:SUB * (j + 1), :]
            qe_j = (q * jnp.exp(jnp.minimum(b - be, 0.0))).astype(BF16)
            da_j = jnp.where(c16 >= r16 + (SUB * j + SUB), da_t[SUB * j:SUB * (j + 1)], 0.0).astype(BF16)
            dk_slabs.append(jnp.exp(be - b[SUB * j:SUB * (j + 1)]) * jnp.dot(da_j, qe_j, preferred_element_type=F32))
        dk_slabs.append(jnp.zeros((SUB, C), F32))

        q_t, k_t, b_t = q.T, k.T, b.T
        lane1 = lax.broadcasted_iota(jnp.int32, (1, C), 1)
        dq_t = jnp.zeros((C, C), F32)
        dk_t = jnp.zeros((C, C), F32)
        ahead = lane - sub
        for dl in range(SUB):
            k_s = pltpu.roll(k_t, dl, 1) if dl else k_t
            b_s = pltpu.roll(b_t, dl, 1) if dl else b_t
            e = jnp.exp(jnp.minimum(b_t - b_s, 0.0))
            dband = jnp.sum(jnp.where(ahead == dl, da_t, 0.0), axis=0, keepdims=True)
            w = jnp.where(lane1 >= dl, dband, 0.0) * e
            dq_t = dq_t + w * k_s
            back = w * q_t
            dk_t = dk_t + (pltpu.roll(back, C - dl, 1) if dl else back)

        dq = eb * lax.dot_general(do_bf, s0_bf, _NT, preferred_element_type=F32) + jnp.concatenate(dq_slabs, axis=0) + dq_t.T
        dk_inter = e_last * lax.dot_general(v_bf, ds1_bf, _NT, preferred_element_type=F32)
        dk = dk_inter + jnp.concatenate(dk_slabs, axis=0) + dk_t.T

        extra = (jnp.exp(b_last) * jnp.sum((s0 * ds1).T, axis=0, keepdims=True)
                 + jnp.sum(k * dk_inter, axis=0, keepdims=True))
        db = q * dq - k * dk + jnp.where(sub == C - 1, jnp.broadcast_to(extra, (C, C)), 0.0)
        tri_t = jnp.where(lane >= sub, 1.0, 0.0).astype(F32)
        dg = jnp.dot(tri_t, db, precision=lax.Precision.HIGHEST, preferred_element_type=F32)
        ds_sc[...] = (jnp.exp(_last_row_as_col(b_t)) * ds1
                      + lax.dot_general((q * eb).astype(BF16), do_bf, _TN, preferred_element_type=F32))

        df = jnp.where(valid, dg / f - dk, 0.0)
        dhf_ref[...] = (df * (1.0 - lb_row) * sg * (1.0 - sg)).astype(BF16)
        dlb_ref[...] += jnp.sum(df * (1.0 - sg), axis=0, keepdims=True)
        dhq_ref[...] = (dq * _dsilu(hq)).astype(BF16)
        dhi_ref[...] = dv.astype(BF16)

    seg = lambda base: (lambda h, s: (nc - 1 - s, base // C + h))
    rmap = lambda h, s: (nc - 1 - s, h)
    return pl.pallas_call(
        body,
        name="hgrn_bwd",
        out_shape=[jax.ShapeDtypeStruct((rows, D_MODEL), BF16)] * 3 + [jax.ShapeDtypeStruct((1, D_MODEL), F32)],
        grid=(HEADS, nc),
        in_specs=[pl.BlockSpec((C, C), seg(SEG_HQ)), pl.BlockSpec((C, C), seg(SEG_HF)), pl.BlockSpec((C, C), seg(SEG_HI)),
                  pl.BlockSpec((1, C), lambda h, s: (0, h)),
                  pl.BlockSpec((1, C, C), lambda h, s: (h, nc - 1 - s, 0)),
                  pl.BlockSpec((1, 1, C, C), lambda h, s: (h, nc - 1 - s, 0, 0)),
                  pl.BlockSpec((C, C), rmap)],
        out_specs=[pl.BlockSpec((C, C), rmap)] * 3 + [pl.BlockSpec((1, C), lambda h, s: (0, h))],
        scratch_shapes=[pltpu.VMEM((C, C), F32), pltpu.VMEM((C, C), F32)],
        compiler_params=_params(("parallel", "arbitrary")),
    )(proj, proj, proj, lb, a_mat, s_states, do_h)


CONV_TC = 512
HALO = 8


def _conv_taps(i, tm, g_ref, pg_ref):
    shape = g_ref.shape
    r = lax.broadcasted_iota(jnp.int32, shape, 0)
    g = jnp.where(i * tm + r >= PAD_LEN, g_ref[...], 0.0)
    p1 = jnp.where(i * tm - 1 >= PAD_LEN, pg_ref[HALO - 1:HALO, :], 0.0)
    p2 = jnp.where(i * tm - 2 >= PAD_LEN, pg_ref[HALO - 2:HALO - 1, :], 0.0)
    s1 = jnp.where(r == 0, p1, pltpu.roll(g, 1, 0))
    s2 = jnp.where(r == 0, p2, jnp.where(r == 1, p1, pltpu.roll(g, 2, 0)))
    return g, s1, s2


def _conv_specs(tm, tc, ncb, order):
    gate = pl.BlockSpec((tm, tc), lambda *ids: order(ids))
    halo = pl.BlockSpec((HALO, tc), lambda *ids: (jnp.maximum(order(ids)[0] * (tm // HALO) - 1, 0), order(ids)[1]))
    up = pl.BlockSpec((tm, tc), lambda *ids: (order(ids)[0], ncb + order(ids)[1]))
    return gate, halo, up


def _conv_fwd(ffn, conv_w, conv_b, rows, tm):
    tc = CONV_TC
    ncb = D_FF // tc

    def body(g_ref, pg_ref, up_ref, cw_ref, cb_ref, act_ref):
        i = pl.program_id(0)
        g, s1, s2 = _conv_taps(i, tm, g_ref, pg_ref)
        conv = (cw_ref[0:1, :] * s2 + cw_ref[1:2, :] * s1 + cw_ref[2:3, :] * g) + cb_ref[...]
        act_ref[...] = (_silu(conv) * up_ref[...]).astype(BF16)

    gate, halo, up = _conv_specs(tm, tc, ncb, lambda ids: (ids[0], ids[1]))
    return pl.pallas_call(
        body,
        name="conv_fwd",
        out_shape=jax.ShapeDtypeStruct((rows, D_FF), BF16),
        grid=(rows // tm, ncb),
        in_specs=[gate, halo, up, pl.BlockSpec((3, tc), lambda i, j: (0, j)), pl.BlockSpec((1, tc), lambda i, j: (0, j))],
        out_specs=pl.BlockSpec((tm, tc), lambda i, j: (i, j)),
        compiler_params=_params(("parallel", "parallel")),
    )(ffn, ffn, ffn, conv_w, conv_b)


def _conv_bwd_a(ffn, dact, conv_w, conv_b, rows, tm):
    tc = CONV_TC
    ncb = D_FF // tc

    def body(g_ref, pg_ref, up_ref, da_ref, cw_ref, cb_ref, dc_ref, dffn_ref, w0_ref, w1_ref, w2_ref, db_ref):
        i = pl.program_id(1)
        g, s1, s2 = _conv_taps(i, tm, g_ref, pg_ref)
        conv = (cw_ref[0:1, :] * s2 + cw_ref[1:2, :] * s1 + cw_ref[2:3, :] * g) + cb_ref[...]
        da = da_ref[...]
        dffn_ref[...] = (da * _silu(conv)).astype(BF16)
        dc = da * up_ref[...] * _dsilu(conv)
        dc_ref[...] = dc
        sums = [jnp.sum(dc * s2, axis=0, keepdims=True), jnp.sum(dc * s1, axis=0, keepdims=True),
                jnp.sum(dc * g, axis=0, keepdims=True), jnp.sum(dc, axis=0, keepdims=True)]
        for ref, val in zip((w0_ref, w1_ref, w2_ref, db_ref), sums):
            @pl.when(i == 0)
            def _(ref=ref, val=val):
                ref[...] = val

            @pl.when(i > 0)
            def _(ref=ref, val=val):
                ref[...] += val

    gate, halo, up = _conv_specs(tm, tc, ncb, lambda ids: (ids[1], ids[0]))
    col = pl.BlockSpec((1, tc), lambda j, i: (0, j))
    return pl.pallas_call(
        body,
        name="conv_bwd_a",
        out_shape=[jax.ShapeDtypeStruct((rows, D_FF), F32), jax.ShapeDtypeStruct((rows, 2 * D_FF), BF16)]
        + [jax.ShapeDtypeStruct((1, D_FF), F32)] * 4,
        grid=(ncb, rows // tm),
        in_specs=[gate, halo, up, pl.BlockSpec((tm, tc), lambda j, i: (i, j)),
                  pl.BlockSpec((3, tc), lambda j, i: (0, j)), col],
        out_specs=[pl.BlockSpec((tm, tc), lambda j, i: (i, j)), pl.BlockSpec((tm, tc), lambda j, i: (i, ncb + j)),
                   col, col, col, col],
        compiler_params=_params(("parallel", "arbitrary")),
    )(ffn, ffn, ffn, dact, conv_w, conv_b)


def _conv_bwd_b(dconv, conv_w, dffn, rows, tm):
    tc = CONV_TC
    ncb = D_FF // tc
    nrb = rows // tm

    def body(dc_ref, nx_ref, cw_ref, dffn_in, out_ref):
        del dffn_in
        i = pl.program_id(0)
        dc = dc_ref[...]
        r = lax.broadcasted_iota(jnp.int32, dc.shape, 0)
        last = i == nrb - 1
        x1 = jnp.where(last, 0.0, nx_ref[0:1, :])
        x2 = jnp.where(last, 0.0, nx_ref[1:2, :])
        n1 = jnp.where(r == tm - 1, x1, pltpu.roll(dc, tm - 1, 0))
        n2 = jnp.where(r == tm - 1, x2, jnp.where(r == tm - 2, x1, pltpu.roll(dc, tm - 2, 0)))
        dg = cw_ref[2:3, :] * dc + cw_ref[1:2, :] * n1 + cw_ref[0:1, :] * n2
        out_ref[...] = jnp.where(i * tm + r >= PAD_LEN, dg, 0.0).astype(BF16)

    return pl.pallas_call(
        body,
        name="conv_bwd_b",
        out_shape=jax.ShapeDtypeStruct((rows, 2 * D_FF), BF16),
        grid=(nrb, ncb),
        in_specs=[pl.BlockSpec((tm, tc), lambda i, j: (i, j)),
                  pl.BlockSpec((HALO, tc), lambda i, j: (jnp.minimum((i + 1) * (tm // HALO), rows // HALO - 1), j)),
                  pl.BlockSpec((3, tc), lambda i, j: (0, j)),
                  pl.BlockSpec(memory_space=pl.ANY)],
        out_specs=pl.BlockSpec((tm, tc), lambda i, j: (i, j)),
        input_output_aliases={3: 0},
        compiler_params=_params(("parallel", "parallel")),
    )(dconv, dconv, conv_w, dffn)


def _final_call(h1, y, target, g_final, rows):
    tm = BLOCK

    def fn(i, tm_, h1v, yv, tgt, g):
        h2 = h1v + yv
        out = _rms_fwd(h2, g)
        err = jnp.where(i > 0, out - tgt, 0.0)
        loss = 0.5 * jnp.sum(jnp.mean(err * err, axis=-1, keepdims=True), axis=0, keepdims=True)
        dx, dg = _rms_bwd(h2, g, err * (1.0 / D_MODEL))
        return dx, dx, jnp.broadcast_to(loss, (1, LANE)), dg

    n_in = 4
    in_specs = [pl.BlockSpec((tm, D_MODEL), lambda i: (i, 0)), pl.BlockSpec((tm, D_MODEL), lambda i: (i, 0)),
                pl.BlockSpec((tm, D_MODEL), lambda i: (jnp.maximum(i - 1, 0), 0)),
                pl.BlockSpec((1, D_MODEL), lambda i: (0, 0))]

    def body(*refs):
        i = pl.program_id(0)
        dx, dx2, loss, dg = fn(i, tm, *[r[...] for r in refs[:n_in]])
        refs[4][...] = dx
        refs[5][...] = dx2.astype(BF16)
        for ref, val in ((refs[6], loss), (refs[7], dg)):
            @pl.when(i == 0)
            def _(ref=ref, val=val):
                ref[...] = val

            @pl.when(i > 0)
            def _(ref=ref, val=val):
                ref[...] += val

    return pl.pallas_call(
        body,
        name="final_loss",
        out_shape=[jax.ShapeDtypeStruct((rows, D_MODEL), F32), jax.ShapeDtypeStruct((rows, D_MODEL), BF16),
                   jax.ShapeDtypeStruct((1, LANE), F32), jax.ShapeDtypeStruct((1, D_MODEL), F32)],
        grid=(rows // tm,),
        in_specs=in_specs,
        out_specs=[pl.BlockSpec((tm, D_MODEL), lambda i: (i, 0)), pl.BlockSpec((tm, D_MODEL), lambda i: (i, 0)),
                   pl.BlockSpec((1, LANE), lambda i: (0, 0)), pl.BlockSpec((1, D_MODEL), lambda i: (0, 0))],
        compiler_params=_params(("arbitrary",)),
    )(h1, y, target, g_final)


def _heads_map(fn, *slabs):
    outs = [fn(*[s[:, h * LANE:(h + 1) * LANE] for s in slabs]) for h in range(HEADS)]
    if isinstance(outs[0], tuple):
        return tuple(jnp.concatenate([o[k] for o in outs], axis=1) for k in range(len(outs[0])))
    return jnp.concatenate(outs, axis=1)


def _local_step(x, positions, target, w, p, emit=None):
    kept = {}
    if emit is None:
        def emit(group):
            kept.update(group)
            return None
    s_len = x.shape[0]
    rows = s_len + BLOCK
    tm = _tile(rows, 640, 8)
    row = lambda arr, width, cb=0: ("row", arr, width, cb)

    h0 = jnp.concatenate([jnp.zeros((PAD_LEN, D_MODEL), F32), w["meta_tokens"], x], axis=0)
    pos = jnp.concatenate([jnp.zeros((PAD_LEN,), jnp.int32), jnp.arange(N_META, dtype=jnp.int32),
                           positions.astype(jnp.int32) + N_META])
    inv = 1.0 / (ROPE_THETA ** (jnp.arange(0, ROPE, 2, dtype=F32) / ROPE))
    ang = pos.astype(F32)[:, None] * inv
    zero = jnp.zeros((rows, LANE - ROPE), F32)
    cos_t = jnp.concatenate([jnp.cos(ang), jnp.cos(ang), zero], axis=1)
    sin_t = jnp.concatenate([-jnp.sin(ang), jnp.sin(ang), zero], axis=1)
    lb_r0, lb_r1 = p["lb_raw"][0:1], p["lb_raw"][1:2]

    def lb_fn(i, tm_, r0, r1):
        m = jnp.maximum(r0, r1)
        e0, e1 = jnp.exp(r0 - m), jnp.exp(r1 - m)
        return (e0 / (e0 + e1),)

    (lb,) = _rowwise("lb_fwd", lb_fn, [("bc", lb_r0), ("bc", lb_r1)], [("acc", (1, D_MODEL))], 1, 1)

    (u1,) = _rowwise("mix_norm", lambda i, t, h, g: (_rms_fwd(h, g),),
                     [row(h0, D_MODEL), ("bc", p["g_mix_norm"])], [("row", D_MODEL, BF16)], rows, tm)
    proj = _matmul(u1, w["w_in"], "nn", F32, "mm_proj")
    hint = getattr(w, "hint", lambda name, after: None)
    hint("w_q_up", proj)
    qn, kvn = _rowwise(
        "latent_norm", lambda i, t, ql, kl, gq, gk: (_rms_fwd(ql, gq), _rms_fwd(kl, gk)),
        [row(proj, Q_LORA, 0), row(proj, KV_LORA, SEG_KV_LAT // KV_LORA), ("bc", p["g_q_norm"]), ("bc", p["g_kv_norm"])],
        [("row", Q_LORA, BF16), ("row", KV_LORA, BF16)], rows, tm)
    q_raw = _matmul(qn, w["w_q_up"], "nn", F32, "mm_q_up")
    kv = _matmul(kvn, w["w_kv_up"], "nn", F32, "mm_kv_up")
    q_att, k_att, v_att = _rope_fwd_call(q_raw, kv, proj, cos_t, sin_t, rows, tm)
    o32, o_bf, lse = _attn_fwd(q_att, k_att, v_att, rows)
    hint("w_branch_mla", lse)
    o_h, a_mat, s_states = _hgrn_fwd(proj, lb, rows)

    def hgrn_post(i, t, oh, hg, g):
        return (_heads_map(lambda a, b: _rms_fwd(a, g) * _silu(b), oh, hg),)

    (o_hgrn,) = _rowwise("hgrn_post", hgrn_post,
                         [row(o_h, D_MODEL), row(proj, D_MODEL, SEG_HG // D_MODEL), ("bc", p["g_hgrn_norm"])],
                         [("row", D_MODEL, BF16)], rows, tm)
    br_a = _matmul(o_bf, w["w_branch_mla"], "nn", F32, "mm_branch_mla")
    br_b = _matmul(o_hgrn, w["w_branch_hgrn"], "nn", F32, "mm_branch_hgrn")
    (merged,) = _rowwise(
        "merge", lambda i, t, a, b, ga, gb: (_sigmoid(ga) * a + _sigmoid(gb) * b,),
        [row(br_a, D_MODEL), row(br_b, D_MODEL), row(proj, D_MODEL, SEG_GA // D_MODEL), row(proj, D_MODEL, SEG_GB // D_MODEL)],
        [("row", D_MODEL, BF16)], rows, tm)
    mix_out = _matmul(merged, w["w_out"], "nn", F32, "mm_out")

    def ffn_norm(i, t, h, mo, g):
        h1v = h + mo
        return h1v, _rms_fwd(h1v, g)

    h1, u2 = _rowwise("ffn_norm", ffn_norm, [row(h0, D_MODEL), row(mix_out, D_MODEL), ("bc", p["g_ffn_norm"])],
                      [("row", D_MODEL, F32), ("row", D_MODEL, BF16)], rows, tm)
    ffn = _matmul(u2, w["w_ffn_in"], "nn", F32, "mm_ffn_in")
    act = _conv_fwd(ffn, w["conv_w"], p["conv_b"], rows, tm)
    y = _matmul(act, w["w_ffn_out"], "nn", F32, "mm_ffn_out")
    dh2, dh2_bf, loss_acc, dg_final = _final_call(h1, y, target, p["g_final_norm"].reshape(1, D_MODEL), rows)

    grads = {"g_final_norm": dg_final.reshape(D_MODEL)}
    dact = _matmul(dh2_bf, w["w_ffn_out"], "nt", F32, "mm_d_act")
    grads["w_ffn_out"] = _matmul(act, dh2_bf, "tn", BF16, "mm_dw_ffn_out")
    dconv, dffn, dcw0, dcw1, dcw2, dcb = _conv_bwd_a(ffn, dact, w["conv_w"], p["conv_b"], rows, tm)
    dffn = _conv_bwd_b(dconv, w["conv_w"], dffn, rows, tm)
    grads["conv_w"] = jnp.concatenate([dcw0, dcw1, dcw2], axis=0)
    grads["conv_b"] = dcb
    du2 = _matmul(dffn, w["w_ffn_in"], "nt", F32, "mm_d_u2")
    grads["w_ffn_in"] = _matmul(u2, dffn, "tn", BF16, "mm_dw_ffn_in")

    def ffn_norm_bwd(i, t, h, du, dh, g):
        dx, dg = _rms_bwd(h, g, du)
        dh1v = dh + dx
        return dh1v, dh1v, dg

    dh1, dh1_bf, grads["g_ffn_norm"] = _rowwise(
        "ffn_norm_bwd", ffn_norm_bwd, [row(h1, D_MODEL), row(du2, D_MODEL), row(dh2, D_MODEL), ("bc", p["g_ffn_norm"])],
        [("row", D_MODEL, F32), ("row", D_MODEL, BF16), ("acc", (1, D_MODEL))], rows, tm)
    tok = emit({n: grads.pop(n) for n in ("w_ffn_out", "w_ffn_in", "conv_w", "conv_b", "g_final_norm", "g_ffn_norm")})
    dmerged = _matmul(dh1_bf, w["w_out"], "nt", F32, "mm_d_merged", after=tok)
    grads["w_out"] = _matmul(merged, dh1_bf, "tn", BF16, "mm_dw_out")

    def merge_bwd(i, t, dm, a, b, ga, gb):
        sa, sb = _sigmoid(ga), _sigmoid(gb)
        return dm * sa, dm * sb, dm * a * sa * (1.0 - sa), dm * b * sb * (1.0 - sb)

    da_bf, db_bf, dga, dgb = _rowwise(
        "merge_bwd", merge_bwd,
        [row(dmerged, D_MODEL), row(br_a, D_MODEL), row(br_b, D_MODEL),
         row(proj, D_MODEL, SEG_GA // D_MODEL), row(proj, D_MODEL, SEG_GB // D_MODEL)],
        [("row", D_MODEL, BF16)] * 4, rows, tm)
    do_mla = _matmul(da_bf, w["w_branch_mla"], "nt", BF16, "mm_d_o_mla")
    grads["w_branch_mla"] = _matmul(o_bf, da_bf, "tn", BF16, "mm_dw_branch_mla")
    do_hgrn = _matmul(db_bf, w["w_branch_hgrn"], "nt", F32, "mm_d_o_hgrn")
    grads["w_branch_hgrn"] = _matmul(o_hgrn, db_bf, "tn", BF16, "mm_dw_branch_hgrn")

    def hgrn_post_bwd(i, t, dy, oh, hg, g):
        def one(dyh, ohh, hgh):
            dx, dg = _rms_bwd(ohh, g, dyh * _silu(hgh))
            return dx, dyh * _rms_fwd(ohh, g) * _dsilu(hgh), dg

        dx, dhg, dg = _heads_map(one, dy, oh, hg)
        dg_sum = dg[:, 0:LANE]
        for h in range(1, HEADS):
            dg_sum = dg_sum + dg[:, h * LANE:(h + 1) * LANE]
        return dx, dhg, dg_sum

    tok = emit({n: grads.pop(n) for n in ("w_out", "w_branch_mla", "w_branch_hgrn")})
    do_h, dhg, grads["g_hgrn_norm"] = _rowwise(
        "hgrn_post_bwd", hgrn_post_bwd,
        [row(do_hgrn, D_MODEL), row(o_h, D_MODEL), row(proj, D_MODEL, SEG_HG // D_MODEL), ("bc", p["g_hgrn_norm"])],
        [("row", D_MODEL, F32), ("row", D_MODEL, BF16), ("acc", (1, LANE))], rows, tm, after=tok)
    dhq, dhf, dhi, dlb = _hgrn_bwd(proj, lb, a_mat, s_states, do_h, rows)

    def lb_bwd(i, tm_, d, l):
        t = d * l * (1.0 - l)
        return t, -t

    dlb0, dlb1 = _rowwise("lb_bwd", lb_bwd, [("bc", dlb), ("bc", lb)], [("acc", (1, D_MODEL))] * 2, 1, 1)
    grads["lb_raw"] = jnp.concatenate([dlb0, dlb1], axis=0)

    delta = _attn_delta(do_mla, o32, rows, tm)
    dq_att, dk_att, dv_att = _attn_bwd(q_att, k_att, v_att, do_mla, lse.reshape(HEADS, 1, rows),
                                       jnp.transpose(delta[:, :HEADS]).reshape(HEADS, 1, rows), rows)
    dq_full, dkv, dkr = _rope_bwd_call(dq_att, dk_att, dv_att, cos_t, sin_t, rows, tm)
    dqn = _matmul(dq_full, w["w_q_up"], "nt", F32, "mm_d_qn")
    grads["w_q_up"] = _matmul(qn, dq_full, "tn", BF16, "mm_dw_q_up")
    dkvn = _matmul(dkv, w["w_kv_up"], "nt", F32, "mm_d_kvn")
    grads["w_kv_up"] = _matmul(kvn, dkv, "tn", BF16, "mm_dw_kv_up")

    def latent_norm_bwd(i, t, ql, kl, dq, dk, gq, gk):
        dql, dgq = _rms_bwd(ql, gq, dq)
        dkl, dgk = _rms_bwd(kl, gk, dk)
        return dql, dkl, dgq, dgk

    dq_lat, dkv_lat, grads["g_q_norm"], grads["g_kv_norm"] = _rowwise(
        "latent_norm_bwd", latent_norm_bwd,
        [row(proj, Q_LORA, 0), row(proj, KV_LORA, SEG_KV_LAT // KV_LORA), row(dqn, Q_LORA), row(dkvn, KV_LORA),
         ("bc", p["g_q_norm"]), ("bc", p["g_kv_norm"])],
        [("row", Q_LORA, BF16), ("row", KV_LORA, BF16), ("acc", (1, Q_LORA)), ("acc", (1, KV_LORA))], rows, tm)
    dproj = jnp.concatenate([dq_lat, dkv_lat, dhq, dhf, dhi, dhg, dga, dgb, dkr], axis=1)
    grads["w_in"] = _matmul(u1, dproj, "tn", BF16, "mm_dw_in")
    tok = emit({n: grads.pop(n) for n in ("w_in", "w_q_up", "w_kv_up", "lb_raw", "g_q_norm", "g_kv_norm", "g_hgrn_norm")})
    du1 = _matmul(dproj, w["w_in"], "nt", F32, "mm_d_u1", after=tok)

    def mix_norm_bwd(i, t, h, du, dh, g):
        dx, dg = _rms_bwd(h, g, du)
        return dh + dx, dg

    dh0, grads["g_mix_norm"] = _rowwise(
        "mix_norm_bwd", mix_norm_bwd, [row(h0, D_MODEL), row(du1, D_MODEL), row(dh1, D_MODEL), ("bc", p["g_mix_norm"])],
        [("row", D_MODEL, F32), ("acc", (1, D_MODEL))], rows, tm)
    grads["meta_tokens"] = dh0[PAD_LEN:BLOCK]
    kept.update(grads)
    return loss_acc[0, 0], dh0[BLOCK:], kept


K_ROPE_AT = Q_LORA + KV_LORA
COL_SHARDED = ("w_in", "w_q_up", "w_kv_up", "w_ffn_in", "conv_w", "meta_tokens")
ROW_SHARDED = ("w_branch_mla", "w_branch_hgrn", "w_out", "w_ffn_out")
BIG = ("w_in", "w_q_up", "w_kv_up", "w_branch_mla", "w_branch_hgrn", "w_out", "w_ffn_in", "w_ffn_out")
SMALL = ("conv_b", "g_mix_norm", "g_q_norm", "g_kv_norm", "g_hgrn_norm", "g_ffn_norm", "g_final_norm", "lb_raw")


def _unshard(name, stacked):
    if name in COL_SHARDED:
        return jnp.transpose(stacked, (1, 0, 2)).reshape(stacked.shape[1], N_DEV * stacked.shape[2])
    return stacked.reshape(N_DEV * stacked.shape[1], stacked.shape[2])


def _reshard(name, full):
    if name in COL_SHARDED:
        r, c = full.shape
        return jnp.transpose(full.reshape(r, N_DEV, c // N_DEV), (1, 0, 2))
    return full.reshape(N_DEV, full.shape[0] // N_DEV, full.shape[1])


def _to_kernel_layout(full):
    out = dict(full)
    if "w_in" in full:
        w_in = full["w_in"]
        pad = jnp.zeros((D_MODEL, KR_W - ROPE), w_in.dtype)
        out["w_in"] = jnp.concatenate(
            [w_in[:, :K_ROPE_AT], w_in[:, K_ROPE_AT + ROPE:], w_in[:, K_ROPE_AT:K_ROPE_AT + ROPE], pad], axis=1)
    if "w_q_up" in full:
        wq = full["w_q_up"].reshape(Q_LORA, HEADS, NOPE + ROPE)
        out["w_q_up"] = jnp.pad(wq, ((0, 0), (0, 0), (0, QHEAD_W - NOPE - ROPE))).reshape(Q_LORA, HEADS * QHEAD_W)
    return out


def _from_kernel_layout(grads):
    out = dict(grads)
    if "w_in" in grads:
        g = grads["w_in"]
        out["w_in"] = jnp.concatenate([g[:, :K_ROPE_AT], g[:, SEG_KR:SEG_KR + ROPE], g[:, K_ROPE_AT:SEG_KR]], axis=1)
    if "w_q_up" in grads:
        g = grads["w_q_up"].reshape(Q_LORA, HEADS, QHEAD_W)
        out["w_q_up"] = g[:, :, :NOPE + ROPE].reshape(Q_LORA, HEADS * (NOPE + ROPE))
    return out


MESH_ID = pl.DeviceIdType.MESH
ANY = pl.BlockSpec(memory_space=pl.ANY)


def _slot(dev):
    return 4 * dev[0] + 2 * dev[1] + dev[2]


def _all_gather(shards):
    n = len(shards)

    def body(*refs):
        ins, outs = refs[:n], refs[n:2 * n]
        send_sems, recv_sems, local_sems = refs[2 * n:]
        x, y, c = lax.axis_index("x"), lax.axis_index("y"), lax.axis_index("c")
        me, sibling = (x, y, c), (x, y, 1 - c)
        chips = [(1 - x, y), (x, 1 - y), (1 - x, 1 - y)]

        def copy(a, k, block, to, src=None):
            dst = outs[a].at[_slot(block)]
            return pltpu.make_async_remote_copy(
                src_ref=dst if src is None else src, dst_ref=dst, send_sem=send_sems.at[a, k],
                recv_sem=recv_sems.at[a, k], device_id=to, device_id_type=MESH_ID)

        mine = [pltpu.make_async_copy(ins[a], outs[a].at[_slot(me)], local_sems.at[a]) for a in range(n)]
        for cp in mine:
            cp.start()
        first = []
        for a in range(n):
            first.append(copy(a, 0, me, sibling, src=ins[a]))
            first += [copy(a, 1 + j, me, (*chip, c), src=ins[a]) for j, chip in enumerate(chips)]
        for cp in first:
            cp.start()
        passed = []
        for a in range(n):
            for j, chip in enumerate(chips):
                copy(a, 1 + j, (*chip, c), me).wait_recv()
                fwd = copy(a, 4 + j, (*chip, c), sibling)
                fwd.start()
                passed.append(fwd)
        for a in range(n):
            copy(a, 0, sibling, me).wait_recv()
            for j, chip in enumerate(chips):
                copy(a, 4 + j, (*chip, 1 - c), me).wait_recv()
        for cp in first + passed:
            cp.wait_send()
        for cp in mine:
            cp.wait()

    return pl.pallas_call(
        body,
        name="gather_weights",
        out_shape=[jax.ShapeDtypeStruct((N_DEV,) + s.shape, s.dtype) for s in shards],
        in_specs=[ANY] * n,
        out_specs=[ANY] * n,
        scratch_shapes=[pltpu.SemaphoreType.DMA((n, 7)), pltpu.SemaphoreType.DMA((n, 7)), pltpu.SemaphoreType.DMA((n,))],
    )(*shards)


def _exchange(blocked, replicated):
    nb, n = len(blocked), len(blocked) + len(replicated)
    arrays = list(blocked) + list(replicated)

    def body(*refs):
        ins, outs = refs[:n], refs[n:2 * n]
        send_sems, recv_sems, local_sems = refs[2 * n:]
        x, y, c = lax.axis_index("x"), lax.axis_index("y"), lax.axis_index("c")
        me = (x, y, c)
        peers = [(x, y, 1 - c), (1 - x, y, c), (x, 1 - y, c), (1 - x, 1 - y, c),
                 (1 - x, y, 1 - c), (x, 1 - y, 1 - c), (1 - x, 1 - y, 1 - c)]

        def src_of(a, dev):
            return ins[a].at[_slot(dev)] if a < nb else ins[a]

        def copy(a, k, frm, to):
            return pltpu.make_async_remote_copy(
                src_ref=src_of(a, to), dst_ref=outs[a].at[_slot(frm)], send_sem=send_sems.at[a, k],
                recv_sem=recv_sems.at[a, k], device_id=to, device_id_type=MESH_ID)

        mine = [pltpu.make_async_copy(src_of(a, me), outs[a].at[_slot(me)], local_sems.at[a]) for a in range(n)]
        for cp in mine:
            cp.start()
        sends = [copy(a, k, me, peer) for a in range(n) for k, peer in enumerate(peers)]
        for cp in sends:
            cp.start()
        for a in range(n):
            for k, peer in enumerate(peers):
                copy(a, k, peer, me).wait_recv()
        for cp in sends:
            cp.wait_send()
        for cp in mine:
            cp.wait()

    return pl.pallas_call(
        body,
        name="exchange_grads",
        out_shape=[jax.ShapeDtypeStruct(s.shape, s.dtype) for s in blocked]
        + [jax.ShapeDtypeStruct((N_DEV,) + s.shape, s.dtype) for s in replicated],
        in_specs=[ANY] * n,
        out_specs=[ANY] * n,
        scratch_shapes=[pltpu.SemaphoreType.DMA((n, 7)), pltpu.SemaphoreType.DMA((n, 7)), pltpu.SemaphoreType.DMA((n,))],
    )(*arrays)


ADAMW_BLOCK_ELEMS = 256 * 1024


def _adamw(name, parts, w, m, v):
    r, c = w.shape
    tr = _tile(r, max(16, ADAMW_BLOCK_ELEMS // c), 16)

    def body(p_ref, w_ref, m_ref, v_ref, g_ref, d_ref, nm_ref, nv_ref):
        g = p_ref[0].astype(F32)
        for s in range(1, N_DEV):
            g = g + p_ref[s].astype(F32)
        m_new = ADAM_B1 * m_ref[...] + (1.0 - ADAM_B1) * g
        v_new = ADAM_B2 * v_ref[...] + (1.0 - ADAM_B2) * (g * g)
        m_hat = m_new / (1.0 - ADAM_B1 ** ADAM_STEP)
        v_hat = v_new / (1.0 - ADAM_B2 ** ADAM_STEP)
        g_ref[...] = g
        d_ref[...] = -ADAM_LR * (m_hat / (jnp.sqrt(v_hat) + ADAM_EPS) + ADAM_WD * w_ref[...])
        nm_ref[...] = m_new
        nv_ref[...] = v_new

    blk = pl.BlockSpec((tr, c), lambda i: (i, 0))
    return pl.pallas_call(
        body,
        name="adamw_" + name,
        out_shape=[jax.ShapeDtypeStruct((r, c), F32)] * 4,
        grid=(r // tr,),
        in_specs=[pl.BlockSpec((N_DEV, tr, c), lambda i: (0, i, 0)), blk, blk, blk],
        out_specs=[blk] * 4,
        compiler_params=_params(("parallel",)),
    )(parts, w, m, v)


HBM_SPEC = pl.BlockSpec(memory_space=pltpu.HBM)
SEM_SPEC = pl.BlockSpec(memory_space=pltpu.SEMAPHORE)
SIDE_EFFECT = pltpu.SideEffectType.DATAFLOW_SIDE_EFFECTING
N_PEERS = N_DEV - 1


def _peers(x, y, c):
    return [(x, y, 1 - c), (1 - x, y, c), (x, 1 - y, c), (1 - x, 1 - y, c),
            (1 - x, y, 1 - c), (x, 1 - y, 1 - c), (1 - x, 1 - y, 1 - c)]


def _split_copy(srcs, lands, blocked, send_sems, recv_sems, a, k, frm, to):
    src = srcs[a].at[_slot(to)] if blocked[a] else srcs[a]
    return pltpu.make_async_remote_copy(
        src_ref=src, dst_ref=lands[a].at[_slot(frm)], send_sem=send_sems.at[a * N_PEERS + k],
        recv_sem=recv_sems.at[a * N_PEERS + k],
        device_id=to, device_id_type=MESH_ID)


def _exchange_start(name, srcs, lands, blocked, after=()):
    n = len(srcs)
    after = list(after)

    def body(*refs):
        src_refs, land_refs = refs[:n], refs[n:2 * n]
        send_sems, recv_sems = refs[2 * n + len(after)], refs[2 * n + len(after) + 1]
        token = refs[-1]
        x, y, c = lax.axis_index("x"), lax.axis_index("y"), lax.axis_index("c")
        for a in range(n):
            for k, peer in enumerate(_peers(x, y, c)):
                _split_copy(src_refs, land_refs, blocked, send_sems, recv_sems, a, k, (x, y, c), peer).start()
        token[...] = jnp.zeros_like(token)

    thru = [pltpu.HBM(s.shape, s.dtype) for s in list(srcs) + list(lands)]
    res = pl.pallas_call(
        body,
        name=name,
        out_shape=(pltpu.SemaphoreType.DMA((n * N_PEERS,)), pltpu.SemaphoreType.DMA((n * N_PEERS,)), *thru,
                   jax.ShapeDtypeStruct((8, LANE), F32)),
        in_specs=[HBM_SPEC] * (2 * n) + [pl.BlockSpec(memory_space=pl.ANY)] * len(after),
        out_specs=(SEM_SPEC, SEM_SPEC, *([HBM_SPEC] * (2 * n)), pl.BlockSpec(memory_space=pltpu.VMEM)),
        input_output_aliases={i: 2 + i for i in range(2 * n)},
        compiler_params=pltpu.CompilerParams(has_side_effects=SIDE_EFFECT),
    )(*[pltpu.with_memory_space_constraint(s, pltpu.HBM) for s in list(srcs) + list(lands)], *after)
    return res[0], res[1], res[2:2 + n], res[2 + n:2 + 2 * n], res[-1]


def _exchange_wait(name, send_sems, recv_sems, srcs, lands, blocked, after):
    n, n_after = len(srcs), len(after)

    def body(*refs):
        src_refs, land_refs = refs[:n], refs[n:2 * n]
        send, recv = refs[2 * n], refs[2 * n + 1]
        x, y, c = lax.axis_index("x"), lax.axis_index("y"), lax.axis_index("c")
        for a in range(n):
            for k, peer in enumerate(_peers(x, y, c)):
                _split_copy(src_refs, land_refs, blocked, send, recv, a, k, (x, y, c), peer).wait_send()
                _split_copy(src_refs, land_refs, blocked, send, recv, a, k, peer, (x, y, c)).wait_recv()

    res = pl.pallas_call(
        body,
        name=name,
        out_shape=tuple(pltpu.HBM(s.shape, s.dtype) for s in list(srcs) + list(lands)),
        in_specs=[HBM_SPEC] * (2 * n) + [SEM_SPEC, SEM_SPEC] + [pl.BlockSpec(memory_space=pl.ANY)] * n_after,
        out_specs=tuple([HBM_SPEC] * (2 * n)),
        input_output_aliases={i: i for i in range(2 * n)},
        compiler_params=pltpu.CompilerParams(has_side_effects=SIDE_EFFECT),
    )(*srcs, *lands, send_sems, recv_sems, *after)
    return res[n:]


class _LazyWeights:
    def __init__(self):
        self.ready, self.groups, self.hints = {}, {}, {}

    def add_group(self, wait_name, names, send, recv, srcs, lands):
        for n in names:
            self.groups[n] = (wait_name, names, send, recv, srcs, lands)

    def hint(self, name, after):
        self.hints[self.groups[name][0]] = after

    def __getitem__(self, name):
        if name not in self.ready:
            wait_name, names, send, recv, srcs, lands = self.groups[name]
            after = [self.hints[wait_name]] if wait_name in self.hints else []
            whole = _exchange_wait(wait_name, send, recv, srcs, lands, [False] * len(names), after)
            for n, stacked in zip(names, whole):
                self.ready[n] = _to_kernel_layout({n: _unshard(n, stacked)})[n]
        return self.ready[name]


def kernel(x, positions, meta_tokens, w_in, w_q_up, w_kv_up, w_branch_mla, w_branch_hgrn, w_out, w_ffn_in, w_ffn_out, conv_w, conv_b, g_mix_norm, g_q_norm, g_kv_norm, g_hgrn_norm, g_ffn_norm, g_final_norm, lb_raw, loss_target, m_meta_tokens, m_w_in, m_w_q_up, m_w_kv_up, m_w_branch_mla, m_w_branch_hgrn, m_w_out, m_w_ffn_in, m_w_ffn_out, m_conv_w, m_conv_b, m_g_mix_norm, m_g_q_norm, m_g_kv_norm, m_g_hgrn_norm, m_g_ffn_norm, m_g_final_norm, m_lb_raw, v_meta_tokens, v_w_in, v_w_q_up, v_w_kv_up, v_w_branch_mla, v_w_branch_hgrn, v_w_out, v_w_ffn_in, v_w_ffn_out, v_conv_w, v_conv_b, v_g_mix_norm, v_g_q_norm, v_g_kv_norm, v_g_hgrn_norm, v_g_ffn_norm, v_g_final_norm, v_lb_raw):
    local = dict(zip(
        ("meta_tokens", "w_in", "w_q_up", "w_kv_up", "w_branch_mla", "w_branch_hgrn", "w_out", "w_ffn_in", "w_ffn_out",
         "conv_w", "conv_b", "g_mix_norm", "g_q_norm", "g_kv_norm", "g_hgrn_norm", "g_ffn_norm", "g_final_norm", "lb_raw"),
        (meta_tokens, w_in, w_q_up, w_kv_up, w_branch_mla, w_branch_hgrn, w_out, w_ffn_in, w_ffn_out,
         conv_w, conv_b, g_mix_norm, g_q_norm, g_kv_norm, g_hgrn_norm, g_ffn_norm, g_final_norm, lb_raw)))
    mom_m = dict(zip(local, (m_meta_tokens, m_w_in, m_w_q_up, m_w_kv_up, m_w_branch_mla, m_w_branch_hgrn, m_w_out, m_w_ffn_in,
                             m_w_ffn_out, m_conv_w, m_conv_b, m_g_mix_norm, m_g_q_norm, m_g_kv_norm, m_g_hgrn_norm,
                             m_g_ffn_norm, m_g_final_norm, m_lb_raw)))
    mom_v = dict(zip(local, (v_meta_tokens, v_w_in, v_w_q_up, v_w_kv_up, v_w_branch_mla, v_w_branch_hgrn, v_w_out, v_w_ffn_in,
                             v_w_ffn_out, v_conv_w, v_conv_b, v_g_mix_norm, v_g_q_norm, v_g_kv_norm, v_g_hgrn_norm,
                             v_g_ffn_norm, v_g_final_norm, v_lb_raw)))
    sharded = BIG + ("conv_w", "meta_tokens")

    def shard2d(name, arr):
        return arr.reshape(arr.shape[-2:]) if name != "meta_tokens" else arr

    def as2d(name, arr):
        return arr.reshape(1, -1) if arr.ndim == 1 else shard2d(name, arr)

    me = 4 * lax.axis_index("x") + 2 * lax.axis_index("y") + lax.axis_index("c")

    def landing(own):
        zone = lax.empty((N_DEV,) + own.shape[1:], own.dtype)
        return lax.dynamic_update_slice_in_dim(zone, own, me, 0)

    shards = {n: shard2d(n, local[n]).astype(BF16) for n in BIG}
    shards.update({n: shard2d(n, local[n]) for n in ("conv_w", "meta_tokens")})
    full = _LazyWeights()
    first = ("w_in", "meta_tokens")
    gathered = _all_gather([shards[n] for n in first])
    for n, g in zip(first, gathered):
        full.ready[n] = _to_kernel_layout({n: _unshard(n, g)})[n]
    later = (("w_q_up", "w_kv_up"), ("w_branch_mla", "w_branch_hgrn", "w_out", "w_ffn_in", "w_ffn_out", "conv_w"))
    for k, names in enumerate(later):
        srcs = [shards[n] for n in names]
        send, recv, srcs_thru, lands_thru, _ = _exchange_start(
            f"gather_start_{k}", srcs, [landing(s[None]) for s in srcs], [False] * len(names), after=[gathered[0]])
        full.add_group(f"gather_wait_{k}", names, send, recv, srcs_thru, lands_thru)
    small = {n: local[n] for n in SMALL}

    started = []

    def sources(group):
        group = _from_kernel_layout(group)
        names = list(group)
        blocked = [n in sharded for n in names]
        srcs = [_reshard(n, group[n]) if b else as2d(n, group[n]) for n, b in zip(names, blocked)]
        return names, blocked, srcs

    def emit(group):
        names, blocked, srcs = sources(group)
        lands = [landing(lax.dynamic_index_in_dim(s, me, 0, keepdims=True) if b else s[None]) for s, b in zip(srcs, blocked)]
        k = len(started)
        send, recv, srcs_thru, lands_thru, token = _exchange_start(f"exchange_start_{k}", srcs, lands, blocked)
        started.append((names, blocked, send, recv, srcs_thru, lands_thru))
        return token

    loss, grad_x, last = _local_step(x[0], positions[0], loss_target[0], full, small, emit)

    out = {}

    def update(names, parts):
        for n, part in zip(names, parts):
            res = _adamw(n, part, as2d(n, local[n]), as2d(n, mom_m[n]), as2d(n, mom_v[n]))
            out[n] = [r.reshape(local[n].shape) for r in res]

    after = [grad_x]
    for k, (names, blocked, send, recv, srcs_thru, lands_thru) in enumerate(started):
        update(names, _exchange_wait(f"exchange_wait_{k}", send, recv, srcs_thru, lands_thru, blocked, after))
        after = [out[names[0]][0]]
    names, blocked, srcs = sources(last)
    in_blocks = [(n, s) for n, s, b in zip(names, srcs, blocked) if b]
    whole = [(n, s) for n, s, b in zip(names, srcs, blocked) if not b]
    update([n for n, _ in in_blocks + whole], _exchange([s for _, s in in_blocks], [s for _, s in whole]))

    loss = lax.psum(loss, ("x", "y", "c"))
    order = tuple(local)
    return (loss, grad_x[None], *[out[n][0] for n in order], *[out[n][1] for n in order],
            *[out[n][2] for n in order], *[out[n][3] for n in order])
```

```python
import functools

import jax
import jax.numpy as jnp
import numpy as np
from jax import lax
from jax.experimental import pallas as pl
from jax.experimental.pallas import tpu as pltpu

F32 = jnp.float32
BF16 = jnp.bfloat16

D_MODEL = 2048
N_META = 16
BLOCK = 128
PAD_LEN = BLOCK - N_META
HEADS = 16
Q_LORA = 1536
KV_LORA = 512
ROPE = 64
NOPE = 128
VDIM = 128
D_FF = 5632
NORM_EPS = 1e-6
ROPE_THETA = 10000.0
ATTN_SCALE = (NOPE + ROPE) ** -0.5
ADAM_LR = 0.001
ADAM_B1 = 0.9
ADAM_B2 = 0.999
ADAM_EPS = 1e-08
ADAM_WD = 0.01
ADAM_STEP = 10
N_DEV = 8

LANE = 128
SEG_Q_LAT = 0
SEG_KV_LAT = Q_LORA
SEG_HQ = 2048
SEG_HF = SEG_HQ + D_MODEL
SEG_HI = SEG_HF + D_MODEL
SEG_HG = SEG_HI + D_MODEL
SEG_GA = SEG_HG + D_MODEL
SEG_GB = SEG_GA + D_MODEL
SEG_KR = SEG_GB + D_MODEL
KR_W = 256
PROJ_W = SEG_KR + KR_W
QHEAD_W = 256

V7X_VMEM_BYTES = 64 * 1024 * 1024
VMEM_LIMIT = V7X_VMEM_BYTES * 7 // 8
NEG_BIG = -1e30
SUB = 8


def _tile(n, target, mult):
    best = None
    for t in range(mult, min(n, target) + 1, mult):
        if n % t == 0:
            best = t
    return n if best is None else best


def _params(sem):
    return pltpu.CompilerParams(dimension_semantics=sem, vmem_limit_bytes=VMEM_LIMIT)


def _sigmoid(x):
    return 1.0 / (1.0 + jnp.exp(-x))


_DIMS = {"nn": (((1,), (0,)), ((), ())), "nt": (((1,), (1,)), ((), ())), "tn": (((0,), (0,)), ((), ()))}


def _matmul(a, b, mode, out_dtype, name, after=None):
    if mode == "nn":
        (m, k), (_, n) = a.shape, b.shape
    elif mode == "nt":
        (m, k), (n, _) = a.shape, b.shape
    else:
        (k, m), (_, n) = a.shape, b.shape
    tm = _tile(m, 1040, 8) if mode != "tn" else _tile(m, 1024, LANE)
    tn = _tile(n, 1024, LANE)
    tk = _tile(k, 2816, LANE) if mode != "tn" else _tile(k, 2080, 8)
    nk = k // tk
    if mode == "nn":
        a_spec = pl.BlockSpec((tm, tk), lambda i, j, kk: (i, kk))
        b_spec = pl.BlockSpec((tk, tn), lambda i, j, kk: (kk, j))
    elif mode == "nt":
        a_spec = pl.BlockSpec((tm, tk), lambda i, j, kk: (i, kk))
        b_spec = pl.BlockSpec((tn, tk), lambda i, j, kk: (j, kk))
    else:
        a_spec = pl.BlockSpec((tk, tm), lambda i, j, kk: (kk, i))
        b_spec = pl.BlockSpec((tk, tn), lambda i, j, kk: (kk, j))
    dims = _DIMS[mode]

    n_after = 0 if after is None else 1

    def body(a_ref, b_ref, *rest):
        o_ref, acc = rest[n_after], rest[n_after + 1:]
        part = lax.dot_general(a_ref[...], b_ref[...], dims, preferred_element_type=F32)
        if nk == 1:
            o_ref[...] = part.astype(o_ref.dtype)
            return
        acc_ref, kk = acc[0], pl.program_id(2)

        @pl.when(kk == 0)
        def _():
            acc_ref[...] = part

        @pl.when((kk > 0) & (kk < nk - 1))
        def _():
            acc_ref[...] += part

        @pl.when(kk == nk - 1)
        def _():
            o_ref[...] = (acc_ref[...] + part).astype(o_ref.dtype)

    return pl.pallas_call(
        body,
        name=name,
        out_shape=jax.ShapeDtypeStruct((m, n), out_dtype),
        grid=(m // tm, n // tn, nk),
        in_specs=[a_spec, b_spec] + [pl.BlockSpec(memory_space=pl.ANY)] * n_after,
        out_specs=pl.BlockSpec((tm, tn), lambda i, j, kk: (i, j)),
        scratch_shapes=[pltpu.VMEM((tm, tn), F32)] if nk > 1 else [],
        compiler_params=_params(("parallel", "parallel", "arbitrary")),
    )(a, b, *([after] * n_after))


ROW_WINDOW_BYTES = 12 * 1024 * 1024


def _rowwise(name, fn, ins, outs, rows, tm, after=None):
    per_row = sum(s[2] * s[1].dtype.itemsize for s in ins if s[0] == "row")
    per_row += sum(s[1] * jnp.dtype(s[2]).itemsize for s in outs if s[0] == "row")
    if per_row:
        tm = _tile(rows, min(tm, max(8, ROW_WINDOW_BYTES // (2 * per_row))), 8)
    n_in = len(ins)
    in_specs, args = [], []
    for spec in ins:
        if spec[0] == "row":
            _, arr, w, cb = spec
            in_specs.append(pl.BlockSpec((tm, w), functools.partial(lambda i, cb: (i, cb), cb=cb)))
        else:
            arr = spec[1]
            in_specs.append(pl.BlockSpec(arr.shape, lambda i: (0, 0)))
        args.append(arr)
    out_shape, out_specs = [], []
    for spec in outs:
        if spec[0] == "row":
            out_shape.append(jax.ShapeDtypeStruct((rows, spec[1]), spec[2]))
            out_specs.append(pl.BlockSpec((tm, spec[1]), lambda i: (i, 0)))
        else:
            out_shape.append(jax.ShapeDtypeStruct(spec[1], F32))
            out_specs.append(pl.BlockSpec(spec[1], lambda i: (0, 0)))
    has_acc = any(s[0] == "acc" for s in outs)
    n_after = 0 if after is None else 1
    in_specs += [pl.BlockSpec(memory_space=pl.ANY)] * n_after
    args += [after] * n_after

    def body(*refs):
        i = pl.program_id(0)
        res = fn(i, tm, *[r[...] for r in refs[:n_in]])
        for spec, ref, val in zip(outs, refs[n_in + n_after:], res):
            if spec[0] == "row":
                ref[...] = val.astype(ref.dtype)
            else:
                @pl.when(i == 0)
                def _(ref=ref, val=val):
                    ref[...] = val

                @pl.when(i > 0)
                def _(ref=ref, val=val):
                    ref[...] += val

    return pl.pallas_call(
        body,
        name=name,
        out_shape=out_shape,
        grid=(rows // tm,),
        in_specs=in_specs,
        out_specs=out_specs,
        compiler_params=_params(("arbitrary" if has_acc else "parallel",)),
    )(*args)


def _row_ids(i, tm, shape):
    return i * tm + lax.broadcasted_iota(jnp.int32, shape, 0)


def _rms_fwd(x, g):
    r = lax.rsqrt(jnp.mean(x * x, axis=-1, keepdims=True) + NORM_EPS)
    return x * r * g


def _rms_bwd(x, g, dy):
    r = lax.rsqrt(jnp.mean(x * x, axis=-1, keepdims=True) + NORM_EPS)
    xhat = x * r
    dxhat = dy * g
    dx = r * (dxhat - xhat * jnp.mean(dxhat * xhat, axis=-1, keepdims=True))
    return dx, jnp.sum(dy * xhat, axis=0, keepdims=True)


def _silu(x):
    return x * _sigmoid(x)


def _dsilu(x):
    s = _sigmoid(x)
    return s * (1.0 + x * (1.0 - s))


def _rot_src(x):
    lane = lax.broadcasted_iota(jnp.int32, x.shape, 1)
    return jnp.where(lane < ROPE // 2, pltpu.roll(x, LANE - ROPE // 2, 1), pltpu.roll(x, ROPE // 2, 1))


def _rope_fwd_call(q_raw, kv, proj, cos_t, sin_t, rows, tm):
    def fn(i, tm_, q, kvv, kr, c, s):
        kr_rot = kr[:, :LANE]
        kr_rot = kr_rot * c + _rot_src(kr_rot) * s
        qs, ks, vs = [], [], []
        for h in range(HEADS):
            qn = q[:, h * QHEAD_W:h * QHEAD_W + NOPE]
            qr = q[:, h * QHEAD_W + NOPE:(h + 1) * QHEAD_W]
            qs += [qn * SCORE_TO_LOG2, (qr * c + _rot_src(qr) * s) * SCORE_TO_LOG2]
            ks += [kvv[:, h * 2 * NOPE:h * 2 * NOPE + NOPE], kr_rot]
            vs += [kvv[:, h * 2 * NOPE + NOPE:(h + 1) * 2 * NOPE]]
        return jnp.concatenate(qs, axis=1), jnp.concatenate(ks, axis=1), jnp.concatenate(vs, axis=1)

    return _rowwise(
        "rope_fwd", fn,
        [("row", q_raw, HEADS * QHEAD_W, 0), ("row", kv, HEADS * 2 * NOPE, 0), ("row", proj, KR_W, SEG_KR // KR_W),
         ("row", cos_t, LANE, 0), ("row", sin_t, LANE, 0)],
        [("row", HEADS * QHEAD_W, BF16), ("row", HEADS * QHEAD_W, BF16), ("row", HEADS * VDIM, BF16)],
        rows, tm)


def _rope_bwd_call(dq_att, dk_att, dv, cos_t, sin_t, rows, tm):
    def fn(i, tm_, dq, dk, dvv, c, s):
        qs, kvs = [], []
        dkr = jnp.zeros((dq.shape[0], LANE), F32)
        for h in range(HEADS):
            dqr = dq[:, h * QHEAD_W + NOPE:(h + 1) * QHEAD_W] * ATTN_SCALE
            qs += [dq[:, h * QHEAD_W:h * QHEAD_W + NOPE] * ATTN_SCALE, dqr * c - _rot_src(dqr) * s]
            kvs += [dk[:, h * QHEAD_W:h * QHEAD_W + NOPE], dvv[:, h * VDIM:(h + 1) * VDIM]]
            dkr = dkr + dk[:, h * QHEAD_W + NOPE:(h + 1) * QHEAD_W]
        dkr = dkr * c - _rot_src(dkr) * s
        return (jnp.concatenate(qs, axis=1), jnp.concatenate(kvs, axis=1),
                jnp.concatenate([dkr, jnp.zeros_like(dkr)], axis=1))

    return _rowwise(
        "rope_bwd", fn,
        [("row", dq_att, HEADS * QHEAD_W, 0), ("row", dk_att, HEADS * QHEAD_W, 0), ("row", dv, HEADS * VDIM, 0),
         ("row", cos_t, LANE, 0), ("row", sin_t, LANE, 0)],
        [("row", HEADS * QHEAD_W, BF16), ("row", HEADS * 2 * NOPE, BF16), ("row", KR_W, BF16)],
        rows, tm)


def _attn_mask(q_blk, k_blk, t, keys_on_rows=False):
    qa, ka = (1, 0) if keys_on_rows else (0, 1)
    qs = q_blk * t + lax.broadcasted_iota(jnp.int32, (t, t), qa)
    ks = k_blk * t + lax.broadcasted_iota(jnp.int32, (t, t), ka)
    return (ks <= qs) & ((ks >= PAD_LEN) | (ks == qs))


_NT = _DIMS["nt"]
_TN = _DIMS["tn"]
LOG2E = 1.4426950408889634
SCORE_TO_LOG2 = ATTN_SCALE * LOG2E


def _causal_pairs(nb, by_key):
    if by_key:
        pairs = [(qi, kj) for kj in range(nb) for qi in range(kj, nb)]
    else:
        pairs = [(qi, kj) for qi in range(nb) for kj in range(qi + 1)]
    return (jnp.asarray(np.array([p[0] for p in pairs], np.int32)), jnp.asarray(np.array([p[1] for p in pairs], np.int32)))


def _two_parts(t):
    cut = (t // LANE + 1) // 2 * LANE
    return ((0, cut), (cut, t)) if cut < t else ((0, t),)


def _attn_fwd(q_att, k_att, v, rows):
    t = _tile(rows, 640, LANE)
    nb = rows // t

    def body(qt_ref, kt_ref, q_ref, k_ref, v_ref, o32_ref, obf_ref, lse_ref, m_sc, l_sc, acc_sc):
        qi, kj = qt_ref[pl.program_id(1)], kt_ref[pl.program_id(1)]

        @pl.when(kj == 0)
        def _():
            m_sc[...] = jnp.full_like(m_sc, NEG_BIG)
            l_sc[...] = jnp.zeros_like(l_sc)
            acc_sc[...] = jnp.zeros_like(acc_sc)

        def step(masked):
            q = q_ref[...]
            parts = _two_parts(t)
            scores =[lax.dot_general(q, k_ref[lo:hi, :], _NT, preferred_element_type=F32) for lo, hi in parts]
            m, l, acc = m_sc[...], l_sc[...], acc_sc[...]
            for (lo, hi), s in zip(parts, scores):
                if masked:
                    qs = qi * t + lax.broadcasted_iota(jnp.int32, (t, hi - lo), 0)
                    ks = kj * t + lo + lax.broadcasted_iota(jnp.int32, (t, hi - lo), 1)
                    s = jnp.where((ks <= qs) & ((ks >= PAD_LEN) | (ks == qs)), s, NEG_BIG)
                m_new = jnp.maximum(m, jnp.max(s, axis=1, keepdims=True))
                alpha = jnp.exp2(m - m_new)
                p = jnp.exp2(s - jnp.tile(m_new, (1, (hi - lo) // LANE)))
                l = alpha * l + jnp.sum(p, axis=1, keepdims=True)
                acc = alpha * acc + jnp.dot(p.astype(BF16), v_ref[lo:hi, :], preferred_element_type=F32)
                m = m_new
            m_sc[...], l_sc[...], acc_sc[...] = m, l, acc

        pl.when((kj == qi) | (kj == 0))(functools.partial(step, True))
        pl.when((kj < qi) & (kj > 0))(functools.partial(step, False))

        @pl.when(kj == qi)
        def _():
            o = acc_sc[...] / l_sc[...]
            o32_ref[...] = o
            obf_ref[...] = o.astype(BF16)
            lse_ref[0] = m_sc[:, 0:1] + jnp.log2(l_sc[:, 0:1])

    qt, kt = _causal_pairs(nb, by_key=False)
    qmap = lambda h, p, qt_ref, kt_ref: (qt_ref[p], h)
    kmap = lambda h, p, qt_ref, kt_ref: (kt_ref[p], h)
    return pl.pallas_call(
        body,
        name="attn_fwd",
        out_shape=[jax.ShapeDtypeStruct((rows, HEADS * VDIM), F32), jax.ShapeDtypeStruct((rows, HEADS * VDIM), BF16),
                   jax.ShapeDtypeStruct((HEADS, rows, 1), F32)],
        grid_spec=pltpu.PrefetchScalarGridSpec(
            num_scalar_prefetch=2,
            grid=(HEADS, len(qt)),
            in_specs=[pl.BlockSpec((t, QHEAD_W), qmap), pl.BlockSpec((t, QHEAD_W), kmap), pl.BlockSpec((t, VDIM), kmap)],
            out_specs=[pl.BlockSpec((t, VDIM), qmap), pl.BlockSpec((t, VDIM), qmap),
                       pl.BlockSpec((1, t, 1), lambda h, p, qt_ref, kt_ref: (h, qt_ref[p], 0))],
            scratch_shapes=[pltpu.VMEM((t, LANE), F32), pltpu.VMEM((t, LANE), F32), pltpu.VMEM((t, VDIM), F32)]),
        compiler_params=_params(("parallel", "arbitrary")),
    )(qt, kt, q_att, k_att, v)


def _attn_delta(do, o32, rows, tm):
    def fn(i, tm_, dov, ov):
        prod = dov.astype(F32) * ov
        head_of = lax.broadcasted_iota(jnp.int32, (HEADS * VDIM, LANE), 0) // VDIM
        pick = jnp.where(head_of == lax.broadcasted_iota(jnp.int32, (HEADS * VDIM, LANE), 1), 1.0, 0.0).astype(F32)
        return (jnp.dot(prod, pick, precision=lax.Precision.HIGHEST, preferred_element_type=F32),)

    (delta,) = _rowwise("attn_delta", fn, [("row", do, HEADS * VDIM, 0), ("row", o32, HEADS * VDIM, 0)],
                        [("row", LANE, F32)], rows, tm)
    return delta


def _attn_bwd(q_att, k_att, v, do, lse_row, delta_row, rows):
    t = _tile(rows, 640, LANE)
    nb = rows // t

    def body(qt_ref, kt_ref, q_ref, k_ref, v_ref, do_ref, lse_ref, delta_ref, dq_ref, dk_ref, dv_ref, dk_sc, dv_sc):
        qi, kj = qt_ref[pl.program_id(1)], kt_ref[pl.program_id(1)]

        @pl.when(pl.program_id(1) == 0)
        def _():
            dq_ref[...] = jnp.zeros_like(dq_ref)

        @pl.when(qi == kj)
        def _():
            dk_sc[...] = jnp.zeros_like(dk_sc)
            dv_sc[...] = jnp.zeros_like(dv_sc)

        def step(masked):
            k, vv = k_ref[...], v_ref[...]
            parts = _two_parts(t)
            st_all = [lax.dot_general(k, q_ref[lo:hi, :], _NT, preferred_element_type=F32) for lo, hi in parts]
            dpt_all = [lax.dot_general(vv, do_ref[lo:hi, :], _NT, preferred_element_type=F32) for lo, hi in parts]
            dk, dv = dk_sc[...], dv_sc[...]
            for (lo, hi), st, dpt in zip(parts, st_all, dpt_all):
                pt = jnp.exp2(st - lse_ref[0, :, lo:hi])
                if masked:
                    ks = kj * t + lax.broadcasted_iota(jnp.int32, (t, hi - lo), 0)
                    qs = qi * t + lo + lax.broadcasted_iota(jnp.int32, (t, hi - lo), 1)
                    pt = jnp.where((ks <= qs) & ((ks >= PAD_LEN) | (ks == qs)), pt, 0.0)
                dv = dv + jnp.dot(pt.astype(BF16), do_ref[lo:hi, :], preferred_element_type=F32)
                dst = (pt * (dpt - delta_ref[0, :, lo:hi])).astype(BF16)
                dk = dk + jnp.dot(dst, q_ref[lo:hi, :], preferred_element_type=F32)
                q_rows = pl.ds(pl.multiple_of(qi * t + lo, LANE), hi - lo)
                dq_ref[q_rows, :] += lax.dot_general(dst, k, _TN, preferred_element_type=F32)
            dk_sc[...], dv_sc[...] = dk, dv

        pl.when((qi == kj) | (kj == 0))(functools.partial(step, True))
        pl.when((qi > kj) & (kj > 0))(functools.partial(step, False))

        @pl.when(qi == nb - 1)
        def _():
            dk_ref[...] = dk_sc[...] * (1.0 / LOG2E)
            dv_ref[...] = dv_sc[...]

    qt, kt = _causal_pairs(nb, by_key=True)
    qmap = lambda h, p, qt_ref, kt_ref: (qt_ref[p], h)
    kmap = lambda h, p, qt_ref, kt_ref: (kt_ref[p], h)
    stat = pl.BlockSpec((1, 1, t), lambda h, p, qt_ref, kt_ref: (h, 0, qt_ref[p]))
    return pl.pallas_call(
        body,
        name="attn_bwd",
        out_shape=[jax.ShapeDtypeStruct((rows, HEADS * QHEAD_W), F32), jax.ShapeDtypeStruct((rows, HEADS * QHEAD_W), F32),
                   jax.ShapeDtypeStruct((rows, HEADS * VDIM), F32)],
        grid_spec=pltpu.PrefetchScalarGridSpec(
            num_scalar_prefetch=2,
            grid=(HEADS, len(qt)),
            in_specs=[pl.BlockSpec((t, QHEAD_W), qmap), pl.BlockSpec((t, QHEAD_W), kmap), pl.BlockSpec((t, VDIM), kmap),
                      pl.BlockSpec((t, VDIM), qmap), stat, stat],
            out_specs=[pl.BlockSpec((rows, QHEAD_W), lambda h, p, qt_ref, kt_ref: (0, h)),
                       pl.BlockSpec((t, QHEAD_W), kmap), pl.BlockSpec((t, VDIM), kmap)],
            scratch_shapes=[pltpu.VMEM((t, QHEAD_W), F32), pltpu.VMEM((t, VDIM), F32)]),
        compiler_params=_params(("parallel", "arbitrary")),
    )(qt, kt, q_att, k_att, v, do, lse_row, delta_row)


C = BLOCK


def _hgrn_prep(hq, hf, hi, lb, c):
    rows = c * C + lax.broadcasted_iota(jnp.int32, (C, C), 0)
    valid = rows >= PAD_LEN
    sg = _sigmoid(hf)
    f = lb + (1.0 - lb) * sg
    g = jnp.where(valid, jnp.log(f), 0.0)
    k = jnp.where(valid, 1.0 - f, 0.0)
    q = _silu(hq)
    r = lax.broadcasted_iota(jnp.int32, (C, C), 0)
    cc = lax.broadcasted_iota(jnp.int32, (C, C), 1)
    tri = jnp.where(cc <= r, 1.0, 0.0).astype(F32)
    b = jnp.dot(tri, g, precision=lax.Precision.HIGHEST, preferred_element_type=F32)
    return q, k, hi, b, f, sg, valid


def _last_row_as_col(b_t):
    lane = lax.broadcasted_iota(jnp.int32, b_t.shape, 1)
    return jnp.sum(jnp.where(lane == C - 1, b_t, 0.0), axis=1, keepdims=True)


def _k_scaled(k, b, bs):
    return (k * jnp.exp(jnp.minimum(bs - b, 0.0))).astype(BF16)


def _hgrn_fwd(proj, lb, rows):
    nc = rows // C

    def body(hq_ref, hf_ref, hi_ref, lb_ref, o_ref, a_ref, s_ref, s_sc, b_sc):
        c = pl.program_id(1)

        @pl.when(c == 0)
        def _():
            s_sc[...] = jnp.zeros_like(s_sc)

        q, k, v, b, _, _, _ = _hgrn_prep(hq_ref[...], hf_ref[...], hi_ref[...], lb_ref[...], c)
        b_sc[...] = b
        s0 = s_sc[...]
        s_ref[0, 0] = s0
        v_bf = v.astype(BF16)
        r16 = lax.broadcasted_iota(jnp.int32, (SUB, C), 0)
        c16 = lax.broadcasted_iota(jnp.int32, (SUB, C), 1)
        slabs = [jnp.zeros((SUB, C), F32)]
        for i in range(1, C // SUB):
            bs = b_sc[SUB * i - 1:SUB * i, :]
            qs = (q[SUB * i:SUB * (i + 1)] * jnp.exp(b[SUB * i:SUB * (i + 1)] - bs)).astype(BF16)
            a_i = lax.dot_general(qs, _k_scaled(k, b, bs), _NT, preferred_element_type=F32)
            slabs.append(jnp.where(c16 <= r16 + (SUB * i - SUB), a_i, 0.0))
        a_off = jnp.concatenate(slabs, axis=0)
        q_t, k_t, b_t = q.T, k.T, b.T
        sub = lax.broadcasted_iota(jnp.int32, (C, C), 0)
        lane = lax.broadcasted_iota(jnp.int32, (C, C), 1)
        lane1 = lax.broadcasted_iota(jnp.int32, (1, C), 1)
        at_band = jnp.zeros((C, C), F32)
        ahead = lane - sub
        for dl in range(SUB):
            k_s = pltpu.roll(k_t, dl, 1) if dl else k_t
            b_s = pltpu.roll(b_t, dl, 1) if dl else b_t
            e = jnp.exp(b_t - b_s)
            band = jnp.sum(q_t * k_s * e, axis=0, keepdims=True)
            band = jnp.where(lane1 >= dl, band, 0.0)
            at_band = at_band + jnp.where(ahead == dl, jnp.broadcast_to(band, (C, C)), 0.0)
        a = (a_off + at_band.T).astype(BF16)
        a_ref[0] = a
        qe = (q * jnp.exp(b)).astype(BF16)
        o_ref[...] = (jnp.dot(a, v_bf, preferred_element_type=F32)
                      + jnp.dot(qe, s0.astype(BF16), preferred_element_type=F32))
        b_last = b_sc[C - 1:C, :]
        kd = (k * jnp.exp(b_last - b)).astype(BF16)
        s_sc[...] = (jnp.exp(_last_row_as_col(b_t)) * s0
                     + lax.dot_general(kd, v_bf, _TN, preferred_element_type=F32))

    seg = lambda base: (lambda h, c: (c, base // C + h))
    return pl.pallas_call(
        body,
        name="hgrn_fwd",
        out_shape=[jax.ShapeDtypeStruct((rows, D_MODEL), F32), jax.ShapeDtypeStruct((HEADS, rows, C), BF16),
                   jax.ShapeDtypeStruct((HEADS, nc, C, C), F32)],
        grid=(HEADS, nc),
        in_specs=[pl.BlockSpec((C, C), seg(SEG_HQ)), pl.BlockSpec((C, C), seg(SEG_HF)), pl.BlockSpec((C, C), seg(SEG_HI)),
                  pl.BlockSpec((1, C), lambda h, c: (0, h))],
        out_specs=[pl.BlockSpec((C, C), lambda h, c: (c, h)), pl.BlockSpec((1, C, C), lambda h, c: (h, c, 0)),
                   pl.BlockSpec((1, 1, C, C), lambda h, c: (h, c, 0, 0))],
        scratch_shapes=[pltpu.VMEM((C, C), F32), pltpu.VMEM((C, C), F32)],
        compiler_params=_params(("parallel", "arbitrary")),
    )(proj, proj, proj, lb)


def _hgrn_bwd(proj, lb, a_mat, s_states, do_h, rows):
    nc = rows // C

    def body(hq_ref, hf_ref, hi_ref, lb_ref, a_ref, s_ref, do_ref, dhq_ref, dhf_ref, dhi_ref, dlb_ref, ds_sc, b_sc):
        step = pl.program_id(1)
        c = nc - 1 - step

        @pl.when(step == 0)
        def _():
            ds_sc[...] = jnp.zeros_like(ds_sc)
            dlb_ref[...] = jnp.zeros_like(dlb_ref)

        hq, hf = hq_ref[...], hf_ref[...]
        lb_row = lb_ref[...]
        q, k, v, b, f, sg, valid = _hgrn_prep(hq, hf, hi_ref[...], lb_row, c)
        b_sc[...] = b
        s0 = s_ref[0, 0]
        ds1 = ds_sc[...]
        s0_bf, ds1_bf = s0.astype(BF16), ds1.astype(BF16)
        do = do_ref[...]
        do_bf, v_bf = do.astype(BF16), v.astype(BF16)
        b_last = b_sc[C - 1:C, :]
        e_last = jnp.exp(b_last - b)
        eb = jnp.exp(b)
        sub = lax.broadcasted_iota(jnp.int32, (C, C), 0)
        lane = lax.broadcasted_iota(jnp.int32, (C, C), 1)
        r16 = lax.broadcasted_iota(jnp.int32, (SUB, C), 0)
        c16 = lax.broadcasted_iota(jnp.int32, (SUB, C), 1)

        dv = (lax.dot_general(a_ref[0], do_bf, _TN, preferred_element_type=F32)
              + jnp.dot((k * e_last).astype(BF16), ds1_bf, preferred_element_type=F32))
        da = jnp.where(lane <= sub, lax.dot_general(do_bf, v_bf, _NT, preferred_element_type=F32), 0.0)
        da_t = jnp.where(sub <= lane, lax.dot_general(v_bf, do_bf, _NT, preferred_element_type=F32), 0.0)

        dq_slabs = [jnp.zeros((SUB, C), F32)]
        for i in range(1, C // SUB):
            bs = b_sc[SUB * i - 1:SUB * i, :]
            da_i = jnp.where(c16 <= r16 + (SUB * i - SUB), da[SUB * i:SUB * (i + 1)], 0.0).astype(BF16)
            dq_slabs.append(jnp.exp(b[SUB * i:SUB * (i + 1)] - bs)
                            * jnp.dot(da_i, _k_scaled(k, b, bs), preferred_element_type=F32))
        dk_slabs = []
        for j in range(C // SUB - 1):
            be = b_sc[SUB * j + SUB - 1:SUB * (j + 1), :]
            qe_j = (q * jnp.exp(jnp.minimum(b - be, 0.0))).astype(BF16)
            da_j = jnp.where(c16 >= r16 + (SUB * j + SUB), da_t[SUB * j:SUB * (j + 1)], 0.0).astype(BF16)
            dk_slabs.append(jnp.exp(be - b[SUB * j:SUB * (j + 1)]) * jnp.dot(da_j, qe_j, preferred_element_type=F32))
        dk_slabs.append(jnp.zeros((SUB, C), F32))

        q_t, k_t, b_t = q.T, k.T, b.T
        lane1 = lax.broadcasted_iota(jnp.int32, (1, C), 1)
        dq_t = jnp.zeros((C, C), F32)
        dk_t = jnp.zeros((C, C), F32)
        ahead = lane - sub
        for dl in range(SUB):
            k_s = pltpu.roll(k_t, dl, 1) if dl else k_t
            b_s = pltpu.roll(b_t, dl, 1) if dl else b_t
            e = jnp.exp(jnp.minimum(b_t - b_s, 0.0))
            dband = jnp.sum(jnp.where(ahead == dl, da_t, 0.0), axis=0, keepdims=True)
            w = jnp.where(lane1 >= dl, dband, 0.0) * e
            dq_t = dq_t + w * k_s
            back = w * q_t
            dk_t = dk_t + (pltpu.roll(back, C - dl, 1) if dl else back)

        dq = eb * lax.dot_general(do_bf, s0_bf, _NT, preferred_element_type=F32) + jnp.concatenate(dq_slabs, axis=0) + dq_t.T
        dk_inter = e_last * lax.dot_general(v_bf, ds1_bf, _NT, preferred_element_type=F32)
        dk = dk_inter + jnp.concatenate(dk_slabs, axis=0) + dk_t.T

        extra = (jnp.exp(b_last) * jnp.sum((s0 * ds1).T, axis=0, keepdims=True)
                 + jnp.sum(k * dk_inter, axis=0, keepdims=True))
        db = q * dq - k * dk + jnp.where(sub == C - 1, jnp.broadcast_to(extra, (C, C)), 0.0)
        tri_t = jnp.where(lane >= sub, 1.0, 0.0).astype(F32)
        dg = jnp.dot(tri_t, db, precision=lax.Precision.HIGHEST, preferred_element_type=F32)
        ds_sc[...] = (jnp.exp(_last_row_as_col(b_t)) * ds1
                      + lax.dot_general((q * eb).astype(BF16), do_bf, _TN, preferred_element_type=F32))

        df = jnp.where(valid, dg / f - dk, 0.0)
        dhf_ref[...] = (df * (1.0 - lb_row) * sg * (1.0 - sg)).astype(BF16)
        dlb_ref[...] += jnp.sum(df * (1.0 - sg), axis=0, keepdims=True)
        dhq_ref[...] = (dq * _dsilu(hq)).astype(BF16)
        dhi_ref[...] = dv.astype(BF16)

    seg = lambda base: (lambda h, s: (nc - 1 - s, base // C + h))
    rmap = lambda h, s: (nc - 1 - s, h)
    return pl.pallas_call(
        body,
        name="hgrn_bwd",
        out_shape=[jax.ShapeDtypeStruct((rows, D_MODEL), BF16)] * 3 + [jax.ShapeDtypeStruct((1, D_MODEL), F32)],
        grid=(HEADS, nc),
        in_specs=[pl.BlockSpec((C, C), seg(SEG_HQ)), pl.BlockSpec((C, C), seg(SEG_HF)), pl.BlockSpec((C, C), seg(SEG_HI)),
                  pl.BlockSpec((1, C), lambda h, s: (0, h)),
                  pl.BlockSpec((1, C, C), lambda h, s: (h, nc - 1 - s, 0)),
                  pl.BlockSpec((1, 1, C, C), lambda h, s: (h, nc - 1 - s, 0, 0)),
                  pl.BlockSpec((C, C), rmap)],
        out_specs=[pl.BlockSpec((C, C), rmap)] * 3 + [pl.BlockSpec((1, C), lambda h, s: (0, h))],
        scratch_shapes=[pltpu.VMEM((C, C), F32), pltpu.VMEM((C, C), F32)],
        compiler_params=_params(("parallel", "arbitrary")),
    )(proj, proj, proj, lb, a_mat, s_states, do_h)


CONV_TC = 512
HALO = 16


def _halo_row(block, k):
    r = lax.broadcasted_iota(jnp.int32, block.shape, 0)
    return jnp.sum(jnp.where(r == k, block, 0.0), axis=0, keepdims=True)


def _conv_taps(i, tm, g_ref, pg_ref):
    shape = g_ref.shape
    r = lax.broadcasted_iota(jnp.int32, shape, 0)
    g = jnp.where(i * tm + r >= PAD_LEN, g_ref[...].astype(F32), 0.0)
    prev = pg_ref[...].astype(F32)
    p1 = jnp.where(i * tm - 1 >= PAD_LEN, _halo_row(prev, HALO - 1), 0.0)
    p2 = jnp.where(i * tm - 2 >= PAD_LEN, _halo_row(prev, HALO - 2), 0.0)
    s1 = jnp.where(r == 0, p1, pltpu.roll(g, 1, 0))
    s2 = jnp.where(r == 0, p2, jnp.where(r == 1, p1, pltpu.roll(g, 2, 0)))
    return g, s1, s2


def _conv_specs(tm, tc, ncb, order):
    gate = pl.BlockSpec((tm, tc), lambda *ids: order(ids))
    halo = pl.BlockSpec((HALO, tc), lambda *ids: (jnp.maximum(order(ids)[0] * (tm // HALO) - 1, 0), order(ids)[1]))
    up = pl.BlockSpec((tm, tc), lambda *ids: (order(ids)[0], ncb + order(ids)[1]))
    return gate, halo, up


def _conv_fwd(ffn, conv_w, conv_b, rows, tm):
    tc = CONV_TC
    ncb = D_FF // tc

    def body(g_ref, pg_ref, up_ref, cw_ref, cb_ref, act_ref):
        i = pl.program_id(0)
        g, s1, s2 = _conv_taps(i, tm, g_ref, pg_ref)
        conv = (cw_ref[0:1, :] * s2 + cw_ref[1:2, :] * s1 + cw_ref[2:3, :] * g) + cb_ref[...]
        act_ref[...] = (_silu(conv) * up_ref[...].astype(F32)).astype(BF16)

    gate, halo, up = _conv_specs(tm, tc, ncb, lambda ids: (ids[0], ids[1]))
    return pl.pallas_call(
        body,
        name="conv_fwd",
        out_shape=jax.ShapeDtypeStruct((rows, D_FF), BF16),
        grid=(rows // tm, ncb),
        in_specs=[gate, halo, up, pl.BlockSpec((3, tc), lambda i, j: (0, j)), pl.BlockSpec((1, tc), lambda i, j: (0, j))],
        out_specs=pl.BlockSpec((tm, tc), lambda i, j: (i, j)),
        compiler_params=_params(("parallel", "parallel")),
    )(ffn, ffn, ffn, conv_w, conv_b)


def _conv_bwd_a(ffn, dact, conv_w, conv_b, rows, tm):
    tc = CONV_TC
    ncb = D_FF // tc

    def body(g_ref, pg_ref, up_ref, da_ref, cw_ref, cb_ref, dc_ref, dffn_ref, w0_ref, w1_ref, w2_ref, db_ref):
        i = pl.program_id(1)
        g, s1, s2 = _conv_taps(i, tm, g_ref, pg_ref)
        conv = (cw_ref[0:1, :] * s2 + cw_ref[1:2, :] * s1 + cw_ref[2:3, :] * g) + cb_ref[...]
        da = da_ref[...].astype(F32)
        dffn_ref[...] = (da * _silu(conv)).astype(BF16)
        dc = da * up_ref[...].astype(F32) * _dsilu(conv)
        dc_ref[...] = dc.astype(BF16)
        sums = [jnp.sum(dc * s2, axis=0, keepdims=True), jnp.sum(dc * s1, axis=0, keepdims=True),
                jnp.sum(dc * g, axis=0, keepdims=True), jnp.sum(dc, axis=0, keepdims=True)]
        for ref, val in zip((w0_ref, w1_ref, w2_ref, db_ref), sums):
            @pl.when(i == 0)
            def _(ref=ref, val=val):
                ref[...] = val

            @pl.when(i > 0)
            def _(ref=ref, val=val):
                ref[...] += val

    gate, halo, up = _conv_specs(tm, tc, ncb, lambda ids: (ids[1], ids[0]))
    col = pl.BlockSpec((1, tc), lambda j, i: (0, j))
    return pl.pallas_call(
        body,
        name="conv_bwd_a",
        out_shape=[jax.ShapeDtypeStruct((rows, D_FF), BF16), jax.ShapeDtypeStruct((rows, 2 * D_FF), BF16)]
        + [jax.ShapeDtypeStruct((1, D_FF), F32)] * 4,
        grid=(ncb, rows // tm),
        in_specs=[gate, halo, up, pl.BlockSpec((tm, tc), lambda j, i: (i, j)),
                  pl.BlockSpec((3, tc), lambda j, i: (0, j)), col],
        out_specs=[pl.BlockSpec((tm, tc), lambda j, i: (i, j)), pl.BlockSpec((tm, tc), lambda j, i: (i, ncb + j)),
                   col, col, col, col],
        compiler_params=_params(("parallel", "arbitrary")),
    )(ffn, ffn, ffn, dact, conv_w, conv_b)


def _conv_bwd_b(dconv, conv_w, dffn, rows, tm):
    tc = CONV_TC
    ncb = D_FF // tc
    nrb = rows // tm

    def body(dc_ref, nx_ref, cw_ref, dffn_in, out_ref):
        del dffn_in
        i = pl.program_id(0)
        dc = dc_ref[...].astype(F32)
        r = lax.broadcasted_iota(jnp.int32, dc.shape, 0)
        last = i == nrb - 1
        nxt = nx_ref[...].astype(F32)
        x1 = jnp.where(last, 0.0, _halo_row(nxt, 0))
        x2 = jnp.where(last, 0.0, _halo_row(nxt, 1))
        n1 = jnp.where(r == tm - 1, x1, pltpu.roll(dc, tm - 1, 0))
        n2 = jnp.where(r == tm - 1, x2, jnp.where(r == tm - 2, x1, pltpu.roll(dc, tm - 2, 0)))
        dg = cw_ref[2:3, :] * dc + cw_ref[1:2, :] * n1 + cw_ref[0:1, :] * n2
        out_ref[...] = jnp.where(i * tm + r >= PAD_LEN, dg, 0.0).astype(BF16)

    return pl.pallas_call(
        body,
        name="conv_bwd_b",
        out_shape=jax.ShapeDtypeStruct((rows, 2 * D_FF), BF16),
        grid=(nrb, ncb),
        in_specs=[pl.BlockSpec((tm, tc), lambda i, j: (i, j)),
                  pl.BlockSpec((HALO, tc), lambda i, j: (jnp.minimum((i + 1) * (tm // HALO), rows // HALO - 1), j)),
                  pl.BlockSpec((3, tc), lambda i, j: (0, j)),
                  pl.BlockSpec(memory_space=pl.ANY)],
        out_specs=pl.BlockSpec((tm, tc), lambda i, j: (i, j)),
        input_output_aliases={3: 0},
        compiler_params=_params(("parallel", "parallel")),
    )(dconv, dconv, conv_w, dffn)


def _final_call(h1, y, target, g_final, rows):
    tm = BLOCK

    def fn(i, tm_, h1v, yv, tgt, g):
        h2 = h1v + yv
        out = _rms_fwd(h2, g)
        err = jnp.where(i > 0, out - tgt, 0.0)
        loss = 0.5 * jnp.sum(jnp.mean(err * err, axis=-1, keepdims=True), axis=0, keepdims=True)
        dx, dg = _rms_bwd(h2, g, err * (1.0 / D_MODEL))
        return dx, dx, jnp.broadcast_to(loss, (1, LANE)), dg

    n_in = 4
    in_specs = [pl.BlockSpec((tm, D_MODEL), lambda i: (i, 0)), pl.BlockSpec((tm, D_MODEL), lambda i: (i, 0)),
                pl.BlockSpec((tm, D_MODEL), lambda i: (jnp.maximum(i - 1, 0), 0)),
                pl.BlockSpec((1, D_MODEL), lambda i: (0, 0))]

    def body(*refs):
        i = pl.program_id(0)
        dx, dx2, loss, dg = fn(i, tm, *[r[...] for r in refs[:n_in]])
        refs[4][...] = dx
        refs[5][...] = dx2.astype(BF16)
        for ref, val in ((refs[6], loss), (refs[7], dg)):
            @pl.when(i == 0)
            def _(ref=ref, val=val):
                ref[...] = val

            @pl.when(i > 0)
            def _(ref=ref, val=val):
                ref[...] += val

    return pl.pallas_call(
        body,
        name="final_loss",
        out_shape=[jax.ShapeDtypeStruct((rows, D_MODEL), F32), jax.ShapeDtypeStruct((rows, D_MODEL), BF16),
                   jax.ShapeDtypeStruct((1, LANE), F32), jax.ShapeDtypeStruct((1, D_MODEL), F32)],
        grid=(rows // tm,),
        in_specs=in_specs,
        out_specs=[pl.BlockSpec((tm, D_MODEL), lambda i: (i, 0)), pl.BlockSpec((tm, D_MODEL), lambda i: (i, 0)),
                   pl.BlockSpec((1, LANE), lambda i: (0, 0)), pl.BlockSpec((1, D_MODEL), lambda i: (0, 0))],
        compiler_params=_params(("arbitrary",)),
    )(h1, y, target, g_final)


def _heads_map(fn, *slabs):
    outs = [fn(*[s[:, h * LANE:(h + 1) * LANE] for s in slabs]) for h in range(HEADS)]
    if isinstance(outs[0], tuple):
        return tuple(jnp.concatenate([o[k] for o in outs], axis=1) for k in range(len(outs[0])))
    return jnp.concatenate(outs, axis=1)


def _local_step(x, positions, target, w, p, emit=None):
    kept = {}
    if emit is None:
        def emit(group):
            kept.update(group)
            return None
    s_len = x.shape[0]
    rows = s_len + BLOCK
    tm = _tile(rows, 640, 8)
    row = lambda arr, width, cb=0: ("row", arr, width, cb)

    h0 = jnp.concatenate([jnp.zeros((PAD_LEN, D_MODEL), F32), w["meta_tokens"], x], axis=0)
    pos = jnp.concatenate([jnp.zeros((PAD_LEN,), jnp.int32), jnp.arange(N_META, dtype=jnp.int32),
                           positions.astype(jnp.int32) + N_META])
    inv = 1.0 / (ROPE_THETA ** (jnp.arange(0, ROPE, 2, dtype=F32) / ROPE))
    ang = pos.astype(F32)[:, None] * inv
    zero = jnp.zeros((rows, LANE - ROPE), F32)
    cos_t = jnp.concatenate([jnp.cos(ang), jnp.cos(ang), zero], axis=1)
    sin_t = jnp.concatenate([-jnp.sin(ang), jnp.sin(ang), zero], axis=1)
    lb_r0, lb_r1 = p["lb_raw"][0:1], p["lb_raw"][1:2]

    def lb_fn(i, tm_, r0, r1):
        m = jnp.maximum(r0, r1)
        e0, e1 = jnp.exp(r0 - m), jnp.exp(r1 - m)
        return (e0 / (e0 + e1),)

    (lb,) = _rowwise("lb_fwd", lb_fn, [("bc", lb_r0), ("bc", lb_r1)], [("acc", (1, D_MODEL))], 1, 1)

    (u1,) = _rowwise("mix_norm", lambda i, t, h, g: (_rms_fwd(h, g),),
                     [row(h0, D_MODEL), ("bc", p["g_mix_norm"])], [("row", D_MODEL, BF16)], rows, tm)
    proj = _matmul(u1, w["w_in"], "nn", F32, "mm_proj")
    hint = getattr(w, "hint", lambda name, after: None)
    hint("w_q_up", proj)
    qn, kvn = _rowwise(
        "latent_norm", lambda i, t, ql, kl, gq, gk: (_rms_fwd(ql, gq), _rms_fwd(kl, gk)),
        [row(proj, Q_LORA, 0), row(proj, KV_LORA, SEG_KV_LAT // KV_LORA), ("bc", p["g_q_norm"]), ("bc", p["g_kv_norm"])],
        [("row", Q_LORA, BF16), ("row", KV_LORA, BF16)], rows, tm)
    q_raw = _matmul(qn, w["w_q_up"], "nn", F32, "mm_q_up")
    kv = _matmul(kvn, w["w_kv_up"], "nn", F32, "mm_kv_up")
    q_att, k_att, v_att = _rope_fwd_call(q_raw, kv, proj, cos_t, sin_t, rows, tm)
    o32, o_bf, lse = _attn_fwd(q_att, k_att, v_att, rows)
    hint("w_branch_mla", lse)
    o_h, a_mat, s_states = _hgrn_fwd(proj, lb, rows)

    def hgrn_post(i, t, oh, hg, g):
        return (_heads_map(lambda a, b: _rms_fwd(a, g) * _silu(b), oh, hg),)

    (o_hgrn,) = _rowwise("hgrn_post", hgrn_post,
                         [row(o_h, D_MODEL), row(proj, D_MODEL, SEG_HG // D_MODEL), ("bc", p["g_hgrn_norm"])],
                         [("row", D_MODEL, BF16)], rows, tm)
    br_a = _matmul(o_bf, w["w_branch_mla"], "nn", F32, "mm_branch_mla")
    br_b = _matmul(o_hgrn, w["w_branch_hgrn"], "nn", F32, "mm_branch_hgrn")
    (merged,) = _rowwise(
        "merge", lambda i, t, a, b, ga, gb: (_sigmoid(ga) * a + _sigmoid(gb) * b,),
        [row(br_a, D_MODEL), row(br_b, D_MODEL), row(proj, D_MODEL, SEG_GA // D_MODEL), row(proj, D_MODEL, SEG_GB // D_MODEL)],
        [("row", D_MODEL, BF16)], rows, tm)
    mix_out = _matmul(merged, w["w_out"], "nn", F32, "mm_out")

    def ffn_norm(i, t, h, mo, g):
        h1v = h + mo
        return h1v, _rms_fwd(h1v, g)

    h1, u2 = _rowwise("ffn_norm", ffn_norm, [row(h0, D_MODEL), row(mix_out, D_MODEL), ("bc", p["g_ffn_norm"])],
                      [("row", D_MODEL, F32), ("row", D_MODEL, BF16)], rows, tm)
    ffn = _matmul(u2, w["w_ffn_in"], "nn", BF16, "mm_ffn_in")
    act = _conv_fwd(ffn, w["conv_w"], p["conv_b"], rows, tm)
    y = _matmul(act, w["w_ffn_out"], "nn", F32, "mm_ffn_out")
    dh2, dh2_bf, loss_acc, dg_final = _final_call(h1, y, target, p["g_final_norm"].reshape(1, D_MODEL), rows)

    grads = {"g_final_norm": dg_final.reshape(D_MODEL)}
    dact = _matmul(dh2_bf, w["w_ffn_out"], "nt", BF16, "mm_d_act")
    grads["w_ffn_out"] = _matmul(act, dh2_bf, "tn", BF16, "mm_dw_ffn_out")
    dconv, dffn, dcw0, dcw1, dcw2, dcb = _conv_bwd_a(ffn, dact, w["conv_w"], p["conv_b"], rows, tm)
    dffn = _conv_bwd_b(dconv, w["conv_w"], dffn, rows, tm)
    grads["conv_w"] = jnp.concatenate([dcw0, dcw1, dcw2], axis=0)
    grads["conv_b"] = dcb
    du2 = _matmul(dffn, w["w_ffn_in"], "nt", F32, "mm_d_u2")
    grads["w_ffn_in"] = _matmul(u2, dffn, "tn", BF16, "mm_dw_ffn_in")

    def ffn_norm_bwd(i, t, h, du, dh, g):
        dx, dg = _rms_bwd(h, g, du)
        dh1v = dh + dx
        return dh1v, dh1v, dg

    dh1, dh1_bf, grads["g_ffn_norm"] = _rowwise(
        "ffn_norm_bwd", ffn_norm_bwd, [row(h1, D_MODEL), row(du2, D_MODEL), row(dh2, D_MODEL), ("bc", p["g_ffn_norm"])],
        [("row", D_MODEL, F32), ("row", D_MODEL, BF16), ("acc", (1, D_MODEL))], rows, tm)
    tok = emit({n: grads.pop(n) for n in ("w_ffn_out", "w_ffn_in", "conv_w", "conv_b", "g_final_norm", "g_ffn_norm")})
    dmerged = _matmul(dh1_bf, w["w_out"], "nt", F32, "mm_d_merged", after=tok)
    grads["w_out"] = _matmul(merged, dh1_bf, "tn", BF16, "mm_dw_out")

    def merge_bwd(i, t, dm, a, b, ga, gb):
        sa, sb = _sigmoid(ga), _sigmoid(gb)
        return dm * sa, dm * sb, dm * a * sa * (1.0 - sa), dm * b * sb * (1.0 - sb)

    da_bf, db_bf, dga, dgb = _rowwise(
        "merge_bwd", merge_bwd,
        [row(dmerged, D_MODEL), row(br_a, D_MODEL), row(br_b, D_MODEL),
         row(proj, D_MODEL, SEG_GA // D_MODEL), row(proj, D_MODEL, SEG_GB // D_MODEL)],
        [("row", D_MODEL, BF16)] * 4, rows, tm)
    do_mla = _matmul(da_bf, w["w_branch_mla"], "nt", BF16, "mm_d_o_mla")
    grads["w_branch_mla"] = _matmul(o_bf, da_bf, "tn", BF16, "mm_dw_branch_mla")
    do_hgrn = _matmul(db_bf, w["w_branch_hgrn"], "nt", F32, "mm_d_o_hgrn")
    grads["w_branch_hgrn"] = _matmul(o_hgrn, db_bf, "tn", BF16, "mm_dw_branch_hgrn")

    def hgrn_post_bwd(i, t, dy, oh, hg, g):
        def one(dyh, ohh, hgh):
            dx, dg = _rms_bwd(ohh, g, dyh * _silu(hgh))
            return dx, dyh * _rms_fwd(ohh, g) * _dsilu(hgh), dg

        dx, dhg, dg = _heads_map(one, dy, oh, hg)
        dg_sum = dg[:, 0:LANE]
        for h in range(1, HEADS):
            dg_sum = dg_sum + dg[:, h * LANE:(h + 1) * LANE]
        return dx, dhg, dg_sum

    tok = emit({n: grads.pop(n) for n in ("w_out", "w_branch_mla", "w_branch_hgrn")})
    do_h, dhg, grads["g_hgrn_norm"] = _rowwise(
        "hgrn_post_bwd", hgrn_post_bwd,
        [row(do_hgrn, D_MODEL), row(o_h, D_MODEL), row(proj, D_MODEL, SEG_HG // D_MODEL), ("bc", p["g_hgrn_norm"])],
        [("row", D_MODEL, F32), ("row", D_MODEL, BF16), ("acc", (1, LANE))], rows, tm, after=tok)
    dhq, dhf, dhi, dlb = _hgrn_bwd(proj, lb, a_mat, s_states, do_h, rows)

    def lb_bwd(i, tm_, d, l):
        t = d * l * (1.0 - l)
        return t, -t

    dlb0, dlb1 = _rowwise("lb_bwd", lb_bwd, [("bc", dlb), ("bc", lb)], [("acc", (1, D_MODEL))] * 2, 1, 1)
    grads["lb_raw"] = jnp.concatenate([dlb0, dlb1], axis=0)

    delta = _attn_delta(do_mla, o32, rows, tm)
    dq_att, dk_att, dv_att = _attn_bwd(q_att, k_att, v_att, do_mla, lse.reshape(HEADS, 1, rows),
                                       jnp.transpose(delta[:, :HEADS]).reshape(HEADS, 1, rows), rows)
    dq_full, dkv, dkr = _rope_bwd_call(dq_att, dk_att, dv_att, cos_t, sin_t, rows, tm)
    dqn = _matmul(dq_full, w["w_q_up"], "nt", F32, "mm_d_qn")
    grads["w_q_up"] = _matmul(qn, dq_full, "tn", BF16, "mm_dw_q_up")
    dkvn = _matmul(dkv, w["w_kv_up"], "nt", F32, "mm_d_kvn")
    grads["w_kv_up"] = _matmul(kvn, dkv, "tn", BF16, "mm_dw_kv_up")

    def latent_norm_bwd(i, t, ql, kl, dq, dk, gq, gk):
        dql, dgq = _rms_bwd(ql, gq, dq)
        dkl, dgk = _rms_bwd(kl, gk, dk)
        return dql, dkl, dgq, dgk

    dq_lat, dkv_lat, grads["g_q_norm"], grads["g_kv_norm"] = _rowwise(
        "latent_norm_bwd", latent_norm_bwd,
        [row(proj, Q_LORA, 0), row(proj, KV_LORA, SEG_KV_LAT // KV_LORA), row(dqn, Q_LORA), row(dkvn, KV_LORA),
         ("bc", p["g_q_norm"]), ("bc", p["g_kv_norm"])],
        [("row", Q_LORA, BF16), ("row", KV_LORA, BF16), ("acc", (1, Q_LORA)), ("acc", (1, KV_LORA))], rows, tm)
    dproj = jnp.concatenate([dq_lat, dkv_lat, dhq, dhf, dhi, dhg, dga, dgb, dkr], axis=1)
    grads["w_in"] = _matmul(u1, dproj, "tn", BF16, "mm_dw_in")
    tok = emit({n: grads.pop(n) for n in ("w_in", "w_q_up", "w_kv_up", "lb_raw", "g_q_norm", "g_kv_norm", "g_hgrn_norm")})
    du1 = _matmul(dproj, w["w_in"], "nt", F32, "mm_d_u1", after=tok)

    def mix_norm_bwd(i, t, h, du, dh, g):
        dx, dg = _rms_bwd(h, g, du)
        return dh + dx, dg

    dh0, grads["g_mix_norm"] = _rowwise(
        "mix_norm_bwd", mix_norm_bwd, [row(h0, D_MODEL), row(du1, D_MODEL), row(dh1, D_MODEL), ("bc", p["g_mix_norm"])],
        [("row", D_MODEL, F32), ("acc", (1, D_MODEL))], rows, tm)
    grads["meta_tokens"] = dh0[PAD_LEN:BLOCK]
    kept.update(grads)
    return loss_acc[0, 0], dh0[BLOCK:], kept


K_ROPE_AT = Q_LORA + KV_LORA
COL_SHARDED = ("w_in", "w_q_up", "w_kv_up", "w_ffn_in", "conv_w", "meta_tokens")
ROW_SHARDED = ("w_branch_mla", "w_branch_hgrn", "w_out", "w_ffn_out")
BIG = ("w_in", "w_q_up", "w_kv_up", "w_branch_mla", "w_branch_hgrn", "w_out", "w_ffn_in", "w_ffn_out")
SMALL = ("conv_b", "g_mix_norm", "g_q_norm", "g_kv_norm", "g_hgrn_norm", "g_ffn_norm", "g_final_norm", "lb_raw")


def _unshard(name, stacked):
    if name in COL_SHARDED:
        return jnp.transpose(stacked, (1, 0, 2)).reshape(stacked.shape[1], N_DEV * stacked.shape[2])
    return stacked.reshape(N_DEV * stacked.shape[1], stacked.shape[2])


def _reshard(name, full):
    if name in COL_SHARDED:
        r, c = full.shape
        return jnp.transpose(full.reshape(r, N_DEV, c // N_DEV), (1, 0, 2))
    return full.reshape(N_DEV, full.shape[0] // N_DEV, full.shape[1])


def _to_kernel_layout(full):
    out = dict(full)
    if "w_in" in full:
        w_in = full["w_in"]
        pad = jnp.zeros((D_MODEL, KR_W - ROPE), w_in.dtype)
        out["w_in"] = jnp.concatenate(
            [w_in[:, :K_ROPE_AT], w_in[:, K_ROPE_AT + ROPE:], w_in[:, K_ROPE_AT:K_ROPE_AT + ROPE], pad], axis=1)
    if "w_q_up" in full:
        wq = full["w_q_up"].reshape(Q_LORA, HEADS, NOPE + ROPE)
        out["w_q_up"] = jnp.pad(wq, ((0, 0), (0, 0), (0, QHEAD_W - NOPE - ROPE))).reshape(Q_LORA, HEADS * QHEAD_W)
    return out


def _from_kernel_layout(grads):
    out = dict(grads)
    if "w_in" in grads:
        g = grads["w_in"]
        out["w_in"] = jnp.concatenate([g[:, :K_ROPE_AT], g[:, SEG_KR:SEG_KR + ROPE], g[:, K_ROPE_AT:SEG_KR]], axis=1)
    if "w_q_up" in grads:
        g = grads["w_q_up"].reshape(Q_LORA, HEADS, QHEAD_W)
        out["w_q_up"] = g[:, :, :NOPE + ROPE].reshape(Q_LORA, HEADS * (NOPE + ROPE))
    return out


MESH_ID = pl.DeviceIdType.MESH
ANY = pl.BlockSpec(memory_space=pl.ANY)


def _slot(dev):
    return 4 * dev[0] + 2 * dev[1] + dev[2]


def _all_gather(shards):
    n = len(shards)

    def body(*refs):
        ins, outs = refs[:n], refs[n:2 * n]
        send_sems, recv_sems, local_sems = refs[2 * n:]
        x, y, c = lax.axis_index("x"), lax.axis_index("y"), lax.axis_index("c")
        me, sibling = (x, y, c), (x, y, 1 - c)
        chips = [(1 - x, y), (x, 1 - y), (1 - x, 1 - y)]

        def copy(a, k, block, to, src=None):
            dst = outs[a].at[_slot(block)]
            return pltpu.make_async_remote_copy(
                src_ref=dst if src is None else src, dst_ref=dst, send_sem=send_sems.at[a, k],
                recv_sem=recv_sems.at[a, k], device_id=to, device_id_type=MESH_ID)

        mine = [pltpu.make_async_copy(ins[a], outs[a].at[_slot(me)], local_sems.at[a]) for a in range(n)]
        for cp in mine:
            cp.start()
        first = []
        for a in range(n):
            first.append(copy(a, 0, me, sibling, src=ins[a]))
            first += [copy(a, 1 + j, me, (*chip, c), src=ins[a]) for j, chip in enumerate(chips)]
        for cp in first:
            cp.start()
        passed = []
        for a in range(n):
            for j, chip in enumerate(chips):
                copy(a, 1 + j, (*chip, c), me).wait_recv()
                fwd = copy(a, 4 + j, (*chip, c), sibling)
                fwd.start()
                passed.append(fwd)
        for a in range(n):
            copy(a, 0, sibling, me).wait_recv()
            for j, chip in enumerate(chips):
                copy(a, 4 + j, (*chip, 1 - c), me).wait_recv()
        for cp in first + passed:
            cp.wait_send()
        for cp in mine:
            cp.wait()

    return pl.pallas_call(
        body,
        name="gather_weights",
        out_shape=[jax.ShapeDtypeStruct((N_DEV,) + s.shape, s.dtype) for s in shards],
        in_specs=[ANY] * n,
        out_specs=[ANY] * n,
        scratch_shapes=[pltpu.SemaphoreType.DMA((n, 7)), pltpu.SemaphoreType.DMA((n, 7)), pltpu.SemaphoreType.DMA((n,))],
    )(*shards)


def _exchange(blocked, replicated):
    nb, n = len(blocked), len(blocked) + len(replicated)
    arrays = list(blocked) + list(replicated)

    def body(*refs):
        ins, outs = refs[:n], refs[n:2 * n]
        send_sems, recv_sems, local_sems = refs[2 * n:]
        x, y, c = lax.axis_index("x"), lax.axis_index("y"), lax.axis_index("c")
        me = (x, y, c)
        peers = [(x, y, 1 - c), (1 - x, y, c), (x, 1 - y, c), (1 - x, 1 - y, c),
                 (1 - x, y, 1 - c), (x, 1 - y, 1 - c), (1 - x, 1 - y, 1 - c)]

        def src_of(a, dev):
            return ins[a].at[_slot(dev)] if a < nb else ins[a]

        def copy(a, k, frm, to):
            return pltpu.make_async_remote_copy(
                src_ref=src_of(a, to), dst_ref=outs[a].at[_slot(frm)], send_sem=send_sems.at[a, k],
                recv_sem=recv_sems.at[a, k], device_id=to, device_id_type=MESH_ID)

        mine = [pltpu.make_async_copy(src_of(a, me), outs[a].at[_slot(me)], local_sems.at[a]) for a in range(n)]
        for cp in mine:
            cp.start()
        sends = [copy(a, k, me, peer) for a in range(n) for k, peer in enumerate(peers)]
        for cp in sends:
            cp.start()
        for a in range(n):
            for k, peer in enumerate(peers):
                copy(a, k, peer, me).wait_recv()
        for cp in sends:
            cp.wait_send()
        for cp in mine:
            cp.wait()

    return pl.pallas_call(
        body,
        name="exchange_grads",
        out_shape=[jax.ShapeDtypeStruct(s.shape, s.dtype) for s in blocked]
        + [jax.ShapeDtypeStruct((N_DEV,) + s.shape, s.dtype) for s in replicated],
        in_specs=[ANY] * n,
        out_specs=[ANY] * n,
        scratch_shapes=[pltpu.SemaphoreType.DMA((n, 7)), pltpu.SemaphoreType.DMA((n, 7)), pltpu.SemaphoreType.DMA((n,))],
    )(*arrays)


ADAMW_BLOCK_ELEMS = 256 * 1024


def _adamw(name, parts, w, m, v, own=None, me=None):
    r, c = w.shape
    tr = _tile(r, max(16, ADAMW_BLOCK_ELEMS // c), 16)

    def body(*refs):
        if own is None:
            p_ref, w_ref, m_ref, v_ref, g_ref, d_ref, nm_ref, nv_ref = refs
            terms = [p_ref[s].astype(F32) for s in range(N_DEV)]
        else:
            me_ref, p_ref, own_ref, w_ref, m_ref, v_ref, g_ref, d_ref, nm_ref, nv_ref = refs
            mine = own_ref[0].astype(F32)
            terms = [jnp.where(me_ref[0] == s, mine, p_ref[s].astype(F32)) for s in range(N_DEV)]
        g = terms[0]
        for s in range(1, N_DEV):
            g = g + terms[s]
        m_new = ADAM_B1 * m_ref[...] + (1.0 - ADAM_B1) * g
        v_new = ADAM_B2 * v_ref[...] + (1.0 - ADAM_B2) * (g * g)
        m_hat = m_new / (1.0 - ADAM_B1 ** ADAM_STEP)
        v_hat = v_new / (1.0 - ADAM_B2 ** ADAM_STEP)
        g_ref[...] = g
        d_ref[...] = -ADAM_LR * (m_hat / (jnp.sqrt(v_hat) + ADAM_EPS) + ADAM_WD * w_ref[...])
        nm_ref[...] = m_new
        nv_ref[...] = v_new

    if own is None:
        blk = pl.BlockSpec((tr, c), lambda i: (i, 0))
        return pl.pallas_call(
            body,
            name="adamw_" + name,
            out_shape=[jax.ShapeDtypeStruct((r, c), F32)] * 4,
            grid=(r // tr,),
            in_specs=[pl.BlockSpec((N_DEV, tr, c), lambda i: (0, i, 0)), blk, blk, blk],
            out_specs=[blk] * 4,
            compiler_params=_params(("parallel",)),
        )(parts, w, m, v)
    blk = pl.BlockSpec((tr, c), lambda i, me_ref: (i, 0))
    own_at = (lambda i, me_ref: (me_ref[0], i, 0)) if own.shape[0] == N_DEV else (lambda i, me_ref: (0, i, 0))
    return pl.pallas_call(
        body,
        name="adamw_" + name,
        out_shape=[jax.ShapeDtypeStruct((r, c), F32)] * 4,
        grid_spec=pltpu.PrefetchScalarGridSpec(
            num_scalar_prefetch=1,
            grid=(r // tr,),
            in_specs=[pl.BlockSpec((N_DEV, tr, c), lambda i, me_ref: (0, i, 0)), pl.BlockSpec((1, tr, c), own_at),
                      blk, blk, blk],
            out_specs=[blk] * 4),
        compiler_params=_params(("parallel",)),
    )(me, parts, own, w, m, v)


HBM_SPEC = pl.BlockSpec(memory_space=pltpu.HBM)
SEM_SPEC = pl.BlockSpec(memory_space=pltpu.SEMAPHORE)
SIDE_EFFECT = pltpu.SideEffectType.DATAFLOW_SIDE_EFFECTING
N_PEERS = N_DEV - 1


def _peers(x, y, c):
    return [(x, y, 1 - c), (1 - x, y, c), (x, 1 - y, c), (1 - x, 1 - y, c),
            (1 - x, y, 1 - c), (x, 1 - y, 1 - c), (1 - x, 1 - y, 1 - c)]


def _split_copy(srcs, lands, blocked, send_sems, recv_sems, a, k, frm, to):
    src = srcs[a].at[_slot(to)] if blocked[a] else srcs[a]
    return pltpu.make_async_remote_copy(
        src_ref=src, dst_ref=lands[a].at[_slot(frm)], send_sem=send_sems.at[a * N_PEERS + k],
        recv_sem=recv_sems.at[a * N_PEERS + k],
        device_id=to, device_id_type=MESH_ID)


def _exchange_start(name, srcs, lands, blocked, after=()):
    n = len(srcs)
    after = list(after)

    def body(*refs):
        src_refs, land_refs = refs[:n], refs[n:2 * n]
        send_sems, recv_sems = refs[2 * n + len(after)], refs[2 * n + len(after) + 1]
        token = refs[-1]
        x, y, c = lax.axis_index("x"), lax.axis_index("y"), lax.axis_index("c")
        for a in range(n):
            for k, peer in enumerate(_peers(x, y, c)):
                _split_copy(src_refs, land_refs, blocked, send_sems, recv_sems, a, k, (x, y, c), peer).start()
        token[...] = jnp.zeros_like(token)

    thru = [pltpu.HBM(s.shape, s.dtype) for s in list(srcs) + list(lands)]
    res = pl.pallas_call(
        body,
        name=name,
        out_shape=(pltpu.SemaphoreType.DMA((n * N_PEERS,)), pltpu.SemaphoreType.DMA((n * N_PEERS,)), *thru,
                   jax.ShapeDtypeStruct((8, LANE), F32)),
        in_specs=[HBM_SPEC] * (2 * n) + [pl.BlockSpec(memory_space=pl.ANY)] * len(after),
        out_specs=(SEM_SPEC, SEM_SPEC, *([HBM_SPEC] * (2 * n)), pl.BlockSpec(memory_space=pltpu.VMEM)),
        input_output_aliases={i: 2 + i for i in range(2 * n)},
        compiler_params=pltpu.CompilerParams(has_side_effects=SIDE_EFFECT),
    )(*[pltpu.with_memory_space_constraint(s, pltpu.HBM) for s in list(srcs) + list(lands)], *after)
    return res[0], res[1], res[2:2 + n], res[2 + n:2 + 2 * n], res[-1]


def _exchange_wait(name, send_sems, recv_sems, srcs, lands, blocked, after):
    n, n_after = len(srcs), len(after)

    def body(*refs):
        src_refs, land_refs = refs[:n], refs[n:2 * n]
        send, recv = refs[2 * n], refs[2 * n + 1]
        x, y, c = lax.axis_index("x"), lax.axis_index("y"), lax.axis_index("c")
        for a in range(n):
            for k, peer in enumerate(_peers(x, y, c)):
                _split_copy(src_refs, land_refs, blocked, send, recv, a, k, (x, y, c), peer).wait_send()
                _split_copy(src_refs, land_refs, blocked, send, recv, a, k, peer, (x, y, c)).wait_recv()

    res = pl.pallas_call(
        body,
        name=name,
        out_shape=tuple(pltpu.HBM(s.shape, s.dtype) for s in list(srcs) + list(lands)),
        in_specs=[HBM_SPEC] * (2 * n) + [SEM_SPEC, SEM_SPEC] + [pl.BlockSpec(memory_space=pl.ANY)] * n_after,
        out_specs=tuple([HBM_SPEC] * (2 * n)),
        input_output_aliases={i: i for i in range(2 * n)},
        compiler_params=pltpu.CompilerParams(has_side_effects=SIDE_EFFECT),
    )(*srcs, *lands, send_sems, recv_sems, *after)
    return res[:n], res[n:]


class _LazyWeights:
    def __init__(self):
        self.ready, self.groups, self.hints = {}, {}, {}

    def add_group(self, wait_name, names, send, recv, srcs, lands):
        for n in names:
            self.groups[n] = (wait_name, names, send, recv, srcs, lands)

    def hint(self, name, after):
        self.hints[self.groups[name][0]] = after

    def __getitem__(self, name):
        if name not in self.ready:
            wait_name, names, send, recv, srcs, lands = self.groups[name]
            after = [self.hints[wait_name]] if wait_name in self.hints else []
            _, whole = _exchange_wait(wait_name, send, recv, srcs, lands, [False] * len(names), after)
            for n, stacked in zip(names, whole):
                self.ready[n] = _to_kernel_layout({n: _unshard(n, stacked)})[n]
        return self.ready[name]


def kernel(x, positions, meta_tokens, w_in, w_q_up, w_kv_up, w_branch_mla, w_branch_hgrn, w_out, w_ffn_in, w_ffn_out, conv_w, conv_b, g_mix_norm, g_q_norm, g_kv_norm, g_hgrn_norm, g_ffn_norm, g_final_norm, lb_raw, loss_target, m_meta_tokens, m_w_in, m_w_q_up, m_w_kv_up, m_w_branch_mla, m_w_branch_hgrn, m_w_out, m_w_ffn_in, m_w_ffn_out, m_conv_w, m_conv_b, m_g_mix_norm, m_g_q_norm, m_g_kv_norm, m_g_hgrn_norm, m_g_ffn_norm, m_g_final_norm, m_lb_raw, v_meta_tokens, v_w_in, v_w_q_up, v_w_kv_up, v_w_branch_mla, v_w_branch_hgrn, v_w_out, v_w_ffn_in, v_w_ffn_out, v_conv_w, v_conv_b, v_g_mix_norm, v_g_q_norm, v_g_kv_norm, v_g_hgrn_norm, v_g_ffn_norm, v_g_final_norm, v_lb_raw):
    local = dict(zip(
        ("meta_tokens", "w_in", "w_q_up", "w_kv_up", "w_branch_mla", "w_branch_hgrn", "w_out", "w_ffn_in", "w_ffn_out",
         "conv_w", "conv_b", "g_mix_norm", "g_q_norm", "g_kv_norm", "g_hgrn_norm", "g_ffn_norm", "g_final_norm", "lb_raw"),
        (meta_tokens, w_in, w_q_up, w_kv_up, w_branch_mla, w_branch_hgrn, w_out, w_ffn_in, w_ffn_out,
         conv_w, conv_b, g_mix_norm, g_q_norm, g_kv_norm, g_hgrn_norm, g_ffn_norm, g_final_norm, lb_raw)))
    mom_m = dict(zip(local, (m_meta_tokens, m_w_in, m_w_q_up, m_w_kv_up, m_w_branch_mla, m_w_branch_hgrn, m_w_out, m_w_ffn_in,
                             m_w_ffn_out, m_conv_w, m_conv_b, m_g_mix_norm, m_g_q_norm, m_g_kv_norm, m_g_hgrn_norm,
                             m_g_ffn_norm, m_g_final_norm, m_lb_raw)))
    mom_v = dict(zip(local, (v_meta_tokens, v_w_in, v_w_q_up, v_w_kv_up, v_w_branch_mla, v_w_branch_hgrn, v_w_out, v_w_ffn_in,
                             v_w_ffn_out, v_conv_w, v_conv_b, v_g_mix_norm, v_g_q_norm, v_g_kv_norm, v_g_hgrn_norm,
                             v_g_ffn_norm, v_g_final_norm, v_lb_raw)))
    sharded = BIG + ("conv_w", "meta_tokens")

    def shard2d(name, arr):
        return arr.reshape(arr.shape[-2:]) if name != "meta_tokens" else arr

    def as2d(name, arr):
        return arr.reshape(1, -1) if arr.ndim == 1 else shard2d(name, arr)

    me = 4 * lax.axis_index("x") + 2 * lax.axis_index("y") + lax.axis_index("c")

    def landing(own):
        zone = lax.empty((N_DEV,) + own.shape[1:], own.dtype)
        return lax.dynamic_update_slice_in_dim(zone, own, me, 0)

    shards = {n: shard2d(n, local[n]).astype(BF16) for n in BIG}
    shards.update({n: shard2d(n, local[n]) for n in ("conv_w", "meta_tokens")})
    full = _LazyWeights()
    first = ("w_in", "meta_tokens")
    gathered = _all_gather([shards[n] for n in first])
    for n, g in zip(first, gathered):
        full.ready[n] = _to_kernel_layout({n: _unshard(n, g)})[n]
    later = (("w_q_up", "w_kv_up"), ("w_branch_mla", "w_branch_hgrn", "w_out", "w_ffn_in", "w_ffn_out", "conv_w"))
    for k, names in enumerate(later):
        srcs = [shards[n] for n in names]
        send, recv, srcs_thru, lands_thru, _ = _exchange_start(
            f"gather_start_{k}", srcs, [landing(s[None]) for s in srcs], [False] * len(names), after=[gathered[0]])
        full.add_group(f"gather_wait_{k}", names, send, recv, srcs_thru, lands_thru)
    small = {n: local[n] for n in SMALL}

    started = []

    def sources(group):
        group = _from_kernel_layout(group)
        names = list(group)
        blocked = [n in sharded for n in names]
        srcs = [_reshard(n, group[n]) if b else as2d(n, group[n]) for n, b in zip(names, blocked)]
        return names, blocked, srcs

    def emit(group):
        names, blocked, srcs = sources(group)
        lands = [lax.empty((N_DEV,) + (s.shape[1:] if b else s.shape), s.dtype) for s, b in zip(srcs, blocked)]
        k = len(started)
        send, recv, srcs_thru, lands_thru, token = _exchange_start(f"exchange_start_{k}", srcs, lands, blocked)
        started.append((names, blocked, send, recv, srcs_thru, lands_thru))
        return token

    loss, grad_x, last = _local_step(x[0], positions[0], loss_target[0], full, small, emit)

    out = {}

    me_arr = me.astype(jnp.int32).reshape(1)

    def update(names, parts, owns=None):
        for k, (n, part) in enumerate(zip(names, parts)):
            own = None if owns is None else (owns[k] if owns[k].ndim == 3 else owns[k][None])
            res = _adamw(n, part, as2d(n, local[n]), as2d(n, mom_m[n]), as2d(n, mom_v[n]), own,
                         None if owns is None else me_arr)
            out[n] = [r.reshape(local[n].shape) for r in res]

    after = [grad_x]
    for k, (names, blocked, send, recv, srcs_thru, lands_thru) in enumerate(started):
        srcs_done, parts = _exchange_wait(f"exchange_wait_{k}", send, recv, srcs_thru, lands_thru, blocked, after)
        update(names, parts, srcs_done)
        after = [out[names[0]][0]]
    names, blocked, srcs = sources(last)
    in_blocks = [(n, s) for n, s, b in zip(names, srcs, blocked) if b]
    whole = [(n, s) for n, s, b in zip(names, srcs, blocked) if not b]
    update([n for n, _ in in_blocks + whole], _exchange([s for _, s in in_blocks], [s for _, s in whole]))

    loss = lax.psum(loss, ("x", "y", "c"))
    order = tuple(local)
    return (loss, grad_x[None], *[out[n][0] for n in order], *[out[n][1] for n in order],
            *[out[n][2] for n in order], *[out[n][3] for n in order])
```

```python
import functools

import jax
import jax.numpy as jnp
import numpy as np
from jax import lax
from jax.experimental import pallas as pl
from jax.experimental.pallas import tpu as pltpu

F32 = jnp.float32
BF16 = jnp.bfloat16

D_MODEL = 2048
N_META = 16
BLOCK = 128
PAD_LEN = BLOCK - N_META
HEADS = 16
Q_LORA = 1536
KV_LORA = 512
ROPE = 64
NOPE = 128
VDIM = 128
D_FF = 5632
NORM_EPS = 1e-6
ROPE_THETA = 10000.0
ATTN_SCALE = (NOPE + ROPE) ** -0.5
ADAM_LR = 0.001
ADAM_B1 = 0.9
ADAM_B2 = 0.999
ADAM_EPS = 1e-08
ADAM_WD = 0.01
ADAM_STEP = 10
N_DEV = 8

LANE = 128
SEG_Q_LAT = 0
SEG_KV_LAT = Q_LORA
SEG_HQ = 2048
SEG_HF = SEG_HQ + D_MODEL
SEG_HI = SEG_HF + D_MODEL
SEG_HG = SEG_HI + D_MODEL
SEG_GA = SEG_HG + D_MODEL
SEG_GB = SEG_GA + D_MODEL
SEG_KR = SEG_GB + D_MODEL
KR_W = 256
PROJ_W = SEG_KR + KR_W
QHEAD_W = 256

V7X_VMEM_BYTES = 64 * 1024 * 1024
VMEM_LIMIT = V7X_VMEM_BYTES * 7 // 8
NEG_BIG = -1e30
SUB = 8


def _tile(n, target, mult):
    best = None
    for t in range(mult, min(n, target) + 1, mult):
        if n % t == 0:
            best = t
    return n if best is None else best


def _params(sem):
    return pltpu.CompilerParams(dimension_semantics=sem, vmem_limit_bytes=VMEM_LIMIT)


def _sigmoid(x):
    return 0.5 * jnp.tanh(0.5 * x) + 0.5


_DIMS = {"nn": (((1,), (0,)), ((), ())), "nt": (((1,), (1,)), ((), ())), "tn": (((0,), (0,)), ((), ()))}


def _matmul(a, b, mode, out_dtype, name, after=None):
    if mode == "nn":
        (m, k), (_, n) = a.shape, b.shape
    elif mode == "nt":
        (m, k), (n, _) = a.shape, b.shape
    else:
        (k, m), (_, n) = a.shape, b.shape
    tm = _tile(m, 1040, 8) if mode != "tn" else _tile(m, 1024, LANE)
    tn = _tile(n, 1024, LANE)
    tk = _tile(k, 2816, LANE) if mode != "tn" else _tile(k, 2080, 8)
    nk = k // tk
    if mode == "nn":
        a_spec = pl.BlockSpec((tm, tk), lambda i, j, kk: (i, kk))
        b_spec = pl.BlockSpec((tk, tn), lambda i, j, kk: (kk, j))
    elif mode == "nt":
        a_spec = pl.BlockSpec((tm, tk), lambda i, j, kk: (i, kk))
        b_spec = pl.BlockSpec((tn, tk), lambda i, j, kk: (j, kk))
    else:
        a_spec = pl.BlockSpec((tk, tm), lambda i, j, kk: (kk, i))
        b_spec = pl.BlockSpec((tk, tn), lambda i, j, kk: (kk, j))
    dims = _DIMS[mode]

    n_after = 0 if after is None else 1

    def body(a_ref, b_ref, *rest):
        o_ref, acc = rest[n_after], rest[n_after + 1:]
        part = lax.dot_general(a_ref[...], b_ref[...], dims, preferred_element_type=F32)
        if nk == 1:
            o_ref[...] = part.astype(o_ref.dtype)
            return
        acc_ref, kk = acc[0], pl.program_id(2)

        @pl.when(kk == 0)
        def _():
            acc_ref[...] = part

        @pl.when((kk > 0) & (kk < nk - 1))
        def _():
            acc_ref[...] += part

        @pl.when(kk == nk - 1)
        def _():
            o_ref[...] = (acc_ref[...] + part).astype(o_ref.dtype)

    return pl.pallas_call(
        body,
        name=name,
        out_shape=jax.ShapeDtypeStruct((m, n), out_dtype),
        grid=(m // tm, n // tn, nk),
        in_specs=[a_spec, b_spec] + [pl.BlockSpec(memory_space=pl.ANY)] * n_after,
        out_specs=pl.BlockSpec((tm, tn), lambda i, j, kk: (i, j)),
        scratch_shapes=[pltpu.VMEM((tm, tn), F32)] if nk > 1 else [],
        compiler_params=_params(("parallel", "parallel", "arbitrary")),
    )(a, b, *([after] * n_after))


ROW_WINDOW_BYTES = 12 * 1024 * 1024


def _rowwise(name, fn, ins, outs, rows, tm, after=None):
    per_row = sum(s[2] * s[1].dtype.itemsize for s in ins if s[0] == "row")
    per_row += sum(s[1] * jnp.dtype(s[2]).itemsize for s in outs if s[0] in ("row", "tail"))
    if per_row:
        tm = _tile(rows, min(tm, max(8, ROW_WINDOW_BYTES // (2 * per_row))), 8)
    n_in = len(ins)
    in_specs, args = [], []
    for spec in ins:
        if spec[0] == "row":
            _, arr, w, cb = spec
            in_specs.append(pl.BlockSpec((tm, w), functools.partial(lambda i, cb: (i, cb), cb=cb)))
        else:
            arr = spec[1]
            in_specs.append(pl.BlockSpec(arr.shape, lambda i: (0, 0)))
        args.append(arr)
    out_shape, out_specs = [], []
    for spec in outs:
        if spec[0] == "row":
            out_shape.append(jax.ShapeDtypeStruct((rows, spec[1]), spec[2]))
            out_specs.append(pl.BlockSpec((tm, spec[1]), lambda i: (i, 0)))
        elif spec[0] == "tail":
            out_shape.append(jax.ShapeDtypeStruct((rows - tm, spec[1]), spec[2]))
            out_specs.append(pl.BlockSpec((tm, spec[1]), lambda i: (jnp.maximum(i - 1, 0), 0)))
        elif spec[0] == "head":
            out_shape.append(jax.ShapeDtypeStruct((tm, spec[1]), spec[2]))
            out_specs.append(pl.BlockSpec((tm, spec[1]), lambda i: (0, 0)))
        else:
            out_shape.append(jax.ShapeDtypeStruct(spec[1], F32))
            out_specs.append(pl.BlockSpec(spec[1], lambda i: (0, 0)))
    has_acc = any(s[0] != "row" for s in outs)
    n_after = 0 if after is None else 1
    in_specs += [pl.BlockSpec(memory_space=pl.ANY)] * n_after
    args += [after] * n_after

    def body(*refs):
        i = pl.program_id(0)
        res = fn(i, tm, *[r[...] for r in refs[:n_in]])
        for spec, ref, val in zip(outs, refs[n_in + n_after:], res):
            if spec[0] in ("row", "tail"):
                ref[...] = val.astype(ref.dtype)
            elif spec[0] == "head":
                @pl.when(i == 0)
                def _(ref=ref, val=val):
                    ref[...] = val.astype(ref.dtype)
            else:
                @pl.when(i == 0)
                def _(ref=ref, val=val):
                    ref[...] = val

                @pl.when(i > 0)
                def _(ref=ref, val=val):
                    ref[...] += val

    return pl.pallas_call(
        body,
        name=name,
        out_shape=out_shape,
        grid=(rows // tm,),
        in_specs=in_specs,
        out_specs=out_specs,
        compiler_params=_params(("arbitrary" if has_acc else "parallel",)),
    )(*args)


def _row_ids(i, tm, shape):
    return i * tm + lax.broadcasted_iota(jnp.int32, shape, 0)


def _rms_fwd(x, g):
    r = lax.rsqrt(jnp.mean(x * x, axis=-1, keepdims=True) + NORM_EPS)
    return x * r * g


def _rms_bwd(x, g, dy):
    r = lax.rsqrt(jnp.mean(x * x, axis=-1, keepdims=True) + NORM_EPS)
    xhat = x * r
    dxhat = dy * g
    dx = r * (dxhat - xhat * jnp.mean(dxhat * xhat, axis=-1, keepdims=True))
    return dx, jnp.sum(dy * xhat, axis=0, keepdims=True)


def _silu(x):
    return x * _sigmoid(x)


def _dsilu(x):
    s = _sigmoid(x)
    return s * (1.0 + x * (1.0 - s))


def _rot_src(x):
    lane = lax.broadcasted_iota(jnp.int32, x.shape, 1)
    return jnp.where(lane < ROPE // 2, pltpu.roll(x, LANE - ROPE // 2, 1), pltpu.roll(x, ROPE // 2, 1))


def _rope_fwd_call(q_raw, kv, proj, cos_t, sin_t, rows, tm):
    def fn(i, tm_, q, kvv, kr, c, s):
        kr_rot = kr[:, :LANE]
        kr_rot = kr_rot * c + _rot_src(kr_rot) * s
        qs, ks, vs = [], [], []
        for h in range(HEADS):
            qn = q[:, h * QHEAD_W:h * QHEAD_W + NOPE]
            qr = q[:, h * QHEAD_W + NOPE:(h + 1) * QHEAD_W]
            qs += [qn * SCORE_TO_LOG2, (qr * c + _rot_src(qr) * s) * SCORE_TO_LOG2]
            ks += [kvv[:, h * 2 * NOPE:h * 2 * NOPE + NOPE], kr_rot]
            vs += [kvv[:, h * 2 * NOPE + NOPE:(h + 1) * 2 * NOPE]]
        return jnp.concatenate(qs, axis=1), jnp.concatenate(ks, axis=1), jnp.concatenate(vs, axis=1)

    return _rowwise(
        "rope_fwd", fn,
        [("row", q_raw, HEADS * QHEAD_W, 0), ("row", kv, HEADS * 2 * NOPE, 0), ("row", proj, KR_W, SEG_KR // KR_W),
         ("row", cos_t, LANE, 0), ("row", sin_t, LANE, 0)],
        [("row", HEADS * QHEAD_W, BF16), ("row", HEADS * QHEAD_W, BF16), ("row", HEADS * VDIM, BF16)],
        rows, tm)


def _rope_bwd_call(dq_att, dk_att, dv, cos_t, sin_t, rows, tm):
    def fn(i, tm_, dq, dk, dvv, c, s):
        qs, kvs = [], []
        dkr = jnp.zeros((dq.shape[0], LANE), F32)
        for h in range(HEADS):
            dqr = dq[:, h * QHEAD_W + NOPE:(h + 1) * QHEAD_W] * ATTN_SCALE
            qs += [dq[:, h * QHEAD_W:h * QHEAD_W + NOPE] * ATTN_SCALE, dqr * c - _rot_src(dqr) * s]
            kvs += [dk[:, h * QHEAD_W:h * QHEAD_W + NOPE], dvv[:, h * VDIM:(h + 1) * VDIM]]
            dkr = dkr + dk[:, h * QHEAD_W + NOPE:(h + 1) * QHEAD_W]
        dkr = dkr * c - _rot_src(dkr) * s
        return (jnp.concatenate(qs, axis=1), jnp.concatenate(kvs, axis=1),
                jnp.concatenate([dkr, jnp.zeros_like(dkr)], axis=1))

    return _rowwise(
        "rope_bwd", fn,
        [("row", dq_att, HEADS * QHEAD_W, 0), ("row", dk_att, HEADS * QHEAD_W, 0), ("row", dv, HEADS * VDIM, 0),
         ("row", cos_t, LANE, 0), ("row", sin_t, LANE, 0)],
        [("row", HEADS * QHEAD_W, BF16), ("row", HEADS * 2 * NOPE, BF16), ("row", KR_W, BF16)],
        rows, tm)


def _attn_mask(q_blk, k_blk, t, keys_on_rows=False):
    qa, ka = (1, 0) if keys_on_rows else (0, 1)
    qs = q_blk * t + lax.broadcasted_iota(jnp.int32, (t, t), qa)
    ks = k_blk * t + lax.broadcasted_iota(jnp.int32, (t, t), ka)
    return (ks <= qs) & ((ks >= PAD_LEN) | (ks == qs))


_NT = _DIMS["nt"]
_TN = _DIMS["tn"]
LOG2E = 1.4426950408889634
SCORE_TO_LOG2 = ATTN_SCALE * LOG2E


def _causal_pairs(nb, by_key):
    if by_key:
        pairs = [(qi, kj) for kj in range(nb) for qi in range(kj, nb)]
    else:
        pairs = [(qi, kj) for qi in range(nb) for kj in range(qi + 1)]
    return (jnp.asarray(np.array([p[0] for p in pairs], np.int32)), jnp.asarray(np.array([p[1] for p in pairs], np.int32)))


def _two_parts(t):
    cut = (t // LANE + 1) // 2 * LANE
    return ((0, cut), (cut, t)) if cut < t else ((0, t),)


def _attn_fwd(q_att, k_att, v, rows):
    t = _tile(rows, 640, LANE)
    nb = rows // t

    def body(qt_ref, kt_ref, q_ref, k_ref, v_ref, o32_ref, obf_ref, lse_ref, m_sc, l_sc, acc_sc):
        qi, kj = qt_ref[pl.program_id(1)], kt_ref[pl.program_id(1)]

        @pl.when(kj == 0)
        def _():
            m_sc[...] = jnp.full_like(m_sc, NEG_BIG)
            l_sc[...] = jnp.zeros_like(l_sc)
            acc_sc[...] = jnp.zeros_like(acc_sc)

        def step(masked):
            q = q_ref[...]
            parts = _two_parts(t)
            scores =[lax.dot_general(q, k_ref[lo:hi, :], _NT, preferred_element_type=F32) for lo, hi in parts]
            m, l, acc = m_sc[...], l_sc[...], acc_sc[...]
            for (lo, hi), s in zip(parts, scores):
                if masked:
                    qs = qi * t + lax.broadcasted_iota(jnp.int32, (t, hi - lo), 0)
                    ks = kj * t + lo + lax.broadcasted_iota(jnp.int32, (t, hi - lo), 1)
                    s = jnp.where((ks <= qs) & ((ks >= PAD_LEN) | (ks == qs)), s, NEG_BIG)
                m_new = jnp.maximum(m, jnp.max(s, axis=1, keepdims=True))
                alpha = jnp.exp2(m - m_new)
                p = jnp.exp2(s - jnp.tile(m_new, (1, (hi - lo) // LANE)))
                l = alpha * l + jnp.sum(p, axis=1, keepdims=True)
                acc = alpha * acc + jnp.dot(p.astype(BF16), v_ref[lo:hi, :], preferred_element_type=F32)
                m = m_new
            m_sc[...], l_sc[...], acc_sc[...] = m, l, acc

        pl.when((kj == qi) | (kj == 0))(functools.partial(step, True))
        pl.when((kj < qi) & (kj > 0))(functools.partial(step, False))

        @pl.when(kj == qi)
        def _():
            o = acc_sc[...] / l_sc[...]
            o32_ref[...] = o
            obf_ref[...] = o.astype(BF16)
            lse_ref[0] = m_sc[:, 0:1] + jnp.log2(l_sc[:, 0:1])

    qt, kt = _causal_pairs(nb, by_key=False)
    qmap = lambda h, p, qt_ref, kt_ref: (qt_ref[p], h)
    kmap = lambda h, p, qt_ref, kt_ref: (kt_ref[p], h)
    return pl.pallas_call(
        body,
        name="attn_fwd",
        out_shape=[jax.ShapeDtypeStruct((rows, HEADS * VDIM), F32), jax.ShapeDtypeStruct((rows, HEADS * VDIM), BF16),
                   jax.ShapeDtypeStruct((HEADS, rows, 1), F32)],
        grid_spec=pltpu.PrefetchScalarGridSpec(
            num_scalar_prefetch=2,
            grid=(HEADS, len(qt)),
            in_specs=[pl.BlockSpec((t, QHEAD_W), qmap), pl.BlockSpec((t, QHEAD_W), kmap), pl.BlockSpec((t, VDIM), kmap)],
            out_specs=[pl.BlockSpec((t, VDIM), qmap), pl.BlockSpec((t, VDIM), qmap),
                       pl.BlockSpec((1, t, 1), lambda h, p, qt_ref, kt_ref: (h, qt_ref[p], 0))],
            scratch_shapes=[pltpu.VMEM((t, LANE), F32), pltpu.VMEM((t, LANE), F32), pltpu.VMEM((t, VDIM), F32)]),
        compiler_params=_params(("parallel", "arbitrary")),
    )(qt, kt, q_att, k_att, v)


def _attn_delta(do, o32, rows, tm):
    def fn(i, tm_, dov, ov):
        prod = dov.astype(F32) * ov
        head_of = lax.broadcasted_iota(jnp.int32, (HEADS * VDIM, LANE), 0) // VDIM
        pick = jnp.where(head_of == lax.broadcasted_iota(jnp.int32, (HEADS * VDIM, LANE), 1), 1.0, 0.0).astype(F32)
        return (jnp.dot(prod, pick, precision=lax.Precision.HIGHEST, preferred_element_type=F32),)

    (delta,) = _rowwise("attn_delta", fn, [("row", do, HEADS * VDIM, 0), ("row", o32, HEADS * VDIM, 0)],
                        [("row", LANE, F32)], rows, tm)
    return delta


def _attn_bwd(q_att, k_att, v, do, lse_row, delta_row, rows):
    t = _tile(rows, 640, LANE)
    nb = rows // t

    def body(qt_ref, kt_ref, q_ref, k_ref, v_ref, do_ref, lse_ref, delta_ref, dq_ref, dk_ref, dv_ref, dk_sc, dv_sc):
        qi, kj = qt_ref[pl.program_id(1)], kt_ref[pl.program_id(1)]

        @pl.when(pl.program_id(1) == 0)
        def _():
            dq_ref[...] = jnp.zeros_like(dq_ref)

        @pl.when(qi == kj)
        def _():
            dk_sc[...] = jnp.zeros_like(dk_sc)
            dv_sc[...] = jnp.zeros_like(dv_sc)

        def step(masked):
            k, vv = k_ref[...], v_ref[...]
            parts = _two_parts(t)
            st_all = [lax.dot_general(k, q_ref[lo:hi, :], _NT, preferred_element_type=F32) for lo, hi in parts]
            dpt_all = [lax.dot_general(vv, do_ref[lo:hi, :], _NT, preferred_element_type=F32) for lo, hi in parts]
            dk, dv = dk_sc[...], dv_sc[...]
            for (lo, hi), st, dpt in zip(parts, st_all, dpt_all):
                pt = jnp.exp2(st - lse_ref[0, :, lo:hi])
                if masked:
                    ks = kj * t + lax.broadcasted_iota(jnp.int32, (t, hi - lo), 0)
                    qs = qi * t + lo + lax.broadcasted_iota(jnp.int32, (t, hi - lo), 1)
                    pt = jnp.where((ks <= qs) & ((ks >= PAD_LEN) | (ks == qs)), pt, 0.0)
                dv = dv + jnp.dot(pt.astype(BF16), do_ref[lo:hi, :], preferred_element_type=F32)
                dst = (pt * (dpt - delta_ref[0, :, lo:hi])).astype(BF16)
                dk = dk + jnp.dot(dst, q_ref[lo:hi, :], preferred_element_type=F32)
                q_rows = pl.ds(pl.multiple_of(qi * t + lo, LANE), hi - lo)
                dq_ref[q_rows, :] += lax.dot_general(dst, k, _TN, preferred_element_type=F32)
            dk_sc[...], dv_sc[...] = dk, dv

        pl.when((qi == kj) | (kj == 0))(functools.partial(step, True))
        pl.when((qi > kj) & (kj > 0))(functools.partial(step, False))

        @pl.when(qi == nb - 1)
        def _():
            dk_ref[...] = dk_sc[...] * (1.0 / LOG2E)
            dv_ref[...] = dv_sc[...]

    qt, kt = _causal_pairs(nb, by_key=True)
    qmap = lambda h, p, qt_ref, kt_ref: (qt_ref[p], h)
    kmap = lambda h, p, qt_ref, kt_ref: (kt_ref[p], h)
    stat = pl.BlockSpec((1, 1, t), lambda h, p, qt_ref, kt_ref: (h, 0, qt_ref[p]))
    return pl.pallas_call(
        body,
        name="attn_bwd",
        out_shape=[jax.ShapeDtypeStruct((rows, HEADS * QHEAD_W), F32), jax.ShapeDtypeStruct((rows, HEADS * QHEAD_W), F32),
                   jax.ShapeDtypeStruct((rows, HEADS * VDIM), F32)],
        grid_spec=pltpu.PrefetchScalarGridSpec(
            num_scalar_prefetch=2,
            grid=(HEADS, len(qt)),
            in_specs=[pl.BlockSpec((t, QHEAD_W), qmap), pl.BlockSpec((t, QHEAD_W), kmap), pl.BlockSpec((t, VDIM), kmap),
                      pl.BlockSpec((t, VDIM), qmap), stat, stat],
            out_specs=[pl.BlockSpec((rows, QHEAD_W), lambda h, p, qt_ref, kt_ref: (0, h)),
                       pl.BlockSpec((t, QHEAD_W), kmap), pl.BlockSpec((t, VDIM), kmap)],
            scratch_shapes=[pltpu.VMEM((t, QHEAD_W), F32), pltpu.VMEM((t, VDIM), F32)]),
        compiler_params=_params(("parallel", "arbitrary")),
    )(qt, kt, q_att, k_att, v, do, lse_row, delta_row)


C = BLOCK


def _hgrn_prep(hq, hf, hi, lb, c):
    rows = c * C + lax.broadcasted_iota(jnp.int32, (C, C), 0)
    valid = rows >= PAD_LEN
    sg = _sigmoid(hf)
    f = lb + (1.0 - lb) * sg
    g = jnp.where(valid, jnp.log(f), 0.0)
    k = jnp.where(valid, 1.0 - f, 0.0)
    q = _silu(hq)
    r = lax.broadcasted_iota(jnp.int32, (C, C), 0)
    cc = lax.broadcasted_iota(jnp.int32, (C, C), 1)
    tri = jnp.where(cc <= r, 1.0, 0.0).astype(F32)
    b = jnp.dot(tri, g, precision=lax.Precision.HIGHEST, preferred_element_type=F32)
    return q, k, hi, b, f, sg, valid


def _last_row_as_col(b_t):
    lane = lax.broadcasted_iota(jnp.int32, b_t.shape, 1)
    return jnp.sum(jnp.where(lane == C - 1, b_t, 0.0), axis=1, keepdims=True)


def _k_scaled(k, b, bs):
    return (k * jnp.exp(jnp.minimum(bs - b, 0.0))).astype(BF16)


def _hgrn_fwd(proj, lb, rows):
    nc = rows // C

    def body(hq_ref, hf_ref, hi_ref, lb_ref, o_ref, a_ref, s_ref, s_sc, b_sc):
        c = pl.program_id(1)

        @pl.when(c == 0)
        def _():
            s_sc[...] = jnp.zeros_like(s_sc)

        q, k, v, b, _, _, _ = _hgrn_prep(hq_ref[...], hf_ref[...], hi_ref[...], lb_ref[...], c)
        b_sc[...] = b
        s0 = s_sc[...]
        s_ref[0, 0] = s0
        v_bf = v.astype(BF16)
        r16 = lax.broadcasted_iota(jnp.int32, (SUB, C), 0)
        c16 = lax.broadcasted_iota(jnp.int32, (SUB, C), 1)
        slabs = [jnp.zeros((SUB, C), F32)]
        for i in range(1, C // SUB):
            bs = b_sc[SUB * i - 1:SUB * i, :]
            qs = (q[SUB * i:SUB * (i + 1)] * jnp.exp(b[SUB * i:SUB * (i + 1)] - bs)).astype(BF16)
            a_i = lax.dot_general(qs, _k_scaled(k, b, bs), _NT, preferred_element_type=F32)
            slabs.append(jnp.where(c16 <= r16 + (SUB * i - SUB), a_i, 0.0))
        a_off = jnp.concatenate(slabs, axis=0)
        q_t, k_t, b_t = q.T, k.T, b.T
        sub = lax.broadcasted_iota(jnp.int32, (C, C), 0)
        lane = lax.broadcasted_iota(jnp.int32, (C, C), 1)
        lane1 = lax.broadcasted_iota(jnp.int32, (1, C), 1)
        at_band = jnp.zeros((C, C), F32)
        ahead = lane - sub
        for dl in range(SUB):
            k_s = pltpu.roll(k_t, dl, 1) if dl else k_t
            b_s = pltpu.roll(b_t, dl, 1) if dl else b_t
            e = jnp.exp(b_t - b_s)
            band = jnp.sum(q_t * k_s * e, axis=0, keepdims=True)
            band = jnp.where(lane1 >= dl, band, 0.0)
            at_band = at_band + jnp.where(ahead == dl, jnp.broadcast_to(band, (C, C)), 0.0)
        a = (a_off + at_band.T).astype(BF16)
        a_ref[0] = a
        qe = (q * jnp.exp(b)).astype(BF16)
        o_ref[...] = (jnp.dot(a, v_bf, preferred_element_type=F32)
                      + jnp.dot(qe, s0.astype(BF16), preferred_element_type=F32))
        b_last = b_sc[C - 1:C, :]
        kd = (k * jnp.exp(b_last - b)).astype(BF16)
        s_sc[...] = (jnp.exp(_last_row_as_col(b_t)) * s0
                     + lax.dot_general(kd, v_bf, _TN, preferred_element_type=F32))

    seg = lambda base: (lambda h, c: (c, base // C + h))
    return pl.pallas_call(
        body,
        name="hgrn_fwd",
        out_shape=[jax.ShapeDtypeStruct((rows, D_MODEL), F32), jax.ShapeDtypeStruct((HEADS, rows, C), BF16),
                   jax.ShapeDtypeStruct((HEADS, nc, C, C), F32)],
        grid=(HEADS, nc),
        in_specs=[pl.BlockSpec((C, C), seg(SEG_HQ)), pl.BlockSpec((C, C), seg(SEG_HF)), pl.BlockSpec((C, C), seg(SEG_HI)),
                  pl.BlockSpec((1, C), lambda h, c: (0, h))],
        out_specs=[pl.BlockSpec((C, C), lambda h, c: (c, h)), pl.BlockSpec((1, C, C), lambda h, c: (h, c, 0)),
                   pl.BlockSpec((1, 1, C, C), lambda h, c: (h, c, 0, 0))],
        scratch_shapes=[pltpu.VMEM((C, C), F32), pltpu.VMEM((C, C), F32)],
        compiler_params=_params(("parallel", "arbitrary")),
    )(proj, proj, proj, lb)


def _hgrn_bwd(proj, lb, a_mat, s_states, do_h, rows):
    nc = rows // C

    def body(hq_ref, hf_ref, hi_ref, lb_ref, a_ref, s_ref, do_ref, dhq_ref, dhf_ref, dhi_ref, dlb_ref, ds_sc, b_sc):
        step = pl.program_id(1)
        c = nc - 1 - step

        @pl.when(step == 0)
        def _():
            ds_sc[...] = jnp.zeros_like(ds_sc)
            dlb_ref[...] = jnp.zeros_like(dlb_ref)

        hq, hf = hq_ref[...], hf_ref[...]
        lb_row = lb_ref[...]
        q, k, v, b, f, sg, valid = _hgrn_prep(hq, hf, hi_ref[...], lb_row, c)
        b_sc[...] = b
        s0 = s_ref[0, 0]
        ds1 = ds_sc[...]
        s0_bf, ds1_bf = s0.astype(BF16), ds1.astype(BF16)
        do = do_ref[...]
        do_bf, v_bf = do.astype(BF16), v.astype(BF16)
        b_last = b_sc[C - 1:C, :]
        e_last = jnp.exp(b_last - b)
        eb = jnp.exp(b)
        sub = lax.broadcasted_iota(jnp.int32, (C, C), 0)
        lane = lax.broadcasted_iota(jnp.int32, (C, C), 1)
        r16 = lax.broadcasted_iota(jnp.int32, (SUB, C), 0)
        c16 = lax.broadcasted_iota(jnp.int32, (SUB, C), 1)

        dv = (lax.dot_general(a_ref[0], do_bf, _TN, preferred_element_type=F32)
              + jnp.dot((k * e_last).astype(BF16), ds1_bf, preferred_element_type=F32))
        da = jnp.where(lane <= sub, lax.dot_general(do_bf, v_bf, _NT, preferred_element_type=F32), 0.0)
        da_t = jnp.where(sub <= lane, lax.dot_general(v_bf, do_bf, _NT, preferred_element_type=F32), 0.0)

        dq_slabs = [jnp.zeros((SUB, C), F32)]
        for i in range(1, C // SUB):
            bs = b_sc[SUB * i - 1:SUB * i, :]
            da_i = jnp.where(c16 <= r16 + (SUB * i - SUB), da[SUB * i:SUB * (i + 1)], 0.0).astype(BF16)
            dq_slabs.append(jnp.exp(b[SUB * i:SUB * (i + 1)] - bs)
                            * jnp.dot(da_i, _k_scaled(k, b, bs), preferred_element_type=F32))
        dk_slabs = []
        for j in range(C // SUB - 1):
            be = b_sc[SUB * j + SUB - 1:SUB * (j + 1), :]
            qe_j = (q * jnp.exp(jnp.minimum(b - be, 0.0))).astype(BF16)
            da_j = jnp.where(c16 >= r16 + (SUB * j + SUB), da_t[SUB * j:SUB * (j + 1)], 0.0).astype(BF16)
            dk_slabs.append(jnp.exp(be - b[SUB * j:SUB * (j + 1)]) * jnp.dot(da_j, qe_j, preferred_element_type=F32))
        dk_slabs.append(jnp.zeros((SUB, C), F32))

        q_t, k_t, b_t = q.T, k.T, b.T
        lane1 = lax.broadcasted_iota(jnp.int32, (1, C), 1)
        dq_t = jnp.zeros((C, C), F32)
        dk_t = jnp.zeros((C, C), F32)
        ahead = lane - sub
        for dl in range(SUB):
            k_s = pltpu.roll(k_t, dl, 1) if dl else k_t
            b_s = pltpu.roll(b_t, dl, 1) if dl else b_t
            e = jnp.exp(jnp.minimum(b_t - b_s, 0.0))
            dband = jnp.sum(jnp.where(ahead == dl, da_t, 0.0), axis=0, keepdims=True)
            w = jnp.where(lane1 >= dl, dband, 0.0) * e
            dq_t = dq_t + w * k_s
            back = w * q_t
            dk_t = dk_t + (pltpu.roll(back, C - dl, 1) if dl else back)

        dq = eb * lax.dot_general(do_bf, s0_bf, _NT, preferred_element_type=F32) + jnp.concatenate(dq_slabs, axis=0) + dq_t.T
        dk_inter = e_last * lax.dot_general(v_bf, ds1_bf, _NT, preferred_element_type=F32)
        dk = dk_inter + jnp.concatenate(dk_slabs, axis=0) + dk_t.T

        extra = (jnp.exp(b_last) * jnp.sum((s0 * ds1).T, axis=0, keepdims=True)
                 + jnp.sum(k * dk_inter, axis=0, keepdims=True))
        db = q * dq - k * dk + jnp.where(sub == C - 1, jnp.broadcast_to(extra, (C, C)), 0.0)
        tri_t = jnp.where(lane >= sub, 1.0, 0.0).astype(F32)
        dg = jnp.dot(tri_t, db, precision=lax.Precision.HIGHEST, preferred_element_type=F32)
        ds_sc[...] = (jnp.exp(_last_row_as_col(b_t)) * ds1
                      + lax.dot_general((q * eb).astype(BF16), do_bf, _TN, preferred_element_type=F32))

        df = jnp.where(valid, dg / f - dk, 0.0)
        dhf_ref[...] = (df * (1.0 - lb_row) * sg * (1.0 - sg)).astype(BF16)
        dlb_ref[...] += jnp.sum(df * (1.0 - sg), axis=0, keepdims=True)
        dhq_ref[...] = (dq * _dsilu(hq)).astype(BF16)
        dhi_ref[...] = dv.astype(BF16)

    seg = lambda base: (lambda h, s: (nc - 1 - s, base // C + h))
    rmap = lambda h, s: (nc - 1 - s, h)
    return pl.pallas_call(
        body,
        name="hgrn_bwd",
        out_shape=[jax.ShapeDtypeStruct((rows, D_MODEL), BF16)] * 3 + [jax.ShapeDtypeStruct((1, D_MODEL), F32)],
        grid=(HEADS, nc),
        in_specs=[pl.BlockSpec((C, C), seg(SEG_HQ)), pl.BlockSpec((C, C), seg(SEG_HF)), pl.BlockSpec((C, C), seg(SEG_HI)),
                  pl.BlockSpec((1, C), lambda h, s: (0, h)),
                  pl.BlockSpec((1, C, C), lambda h, s: (h, nc - 1 - s, 0)),
                  pl.BlockSpec((1, 1, C, C), lambda h, s: (h, nc - 1 - s, 0, 0)),
                  pl.BlockSpec((C, C), rmap)],
        out_specs=[pl.BlockSpec((C, C), rmap)] * 3 + [pl.BlockSpec((1, C), lambda h, s: (0, h))],
        scratch_shapes=[pltpu.VMEM((C, C), F32), pltpu.VMEM((C, C), F32)],
        compiler_params=_params(("parallel", "arbitrary")),
    )(proj, proj, proj, lb, a_mat, s_states, do_h)


CONV_TC = 512
HALO = 16


def _halo_row(block, k):
    r = lax.broadcasted_iota(jnp.int32, block.shape, 0)
    return jnp.sum(jnp.where(r == k, block, 0.0), axis=0, keepdims=True)


def _conv_taps(i, tm, g_ref, pg_ref):
    shape = g_ref.shape
    r = lax.broadcasted_iota(jnp.int32, shape, 0)
    g = jnp.where(i * tm + r >= PAD_LEN, g_ref[...].astype(F32), 0.0)
    prev = pg_ref[...].astype(F32)
    p1 = jnp.where(i * tm - 1 >= PAD_LEN, _halo_row(prev, HALO - 1), 0.0)
    p2 = jnp.where(i * tm - 2 >= PAD_LEN, _halo_row(prev, HALO - 2), 0.0)
    s1 = jnp.where(r == 0, p1, pltpu.roll(g, 1, 0))
    s2 = jnp.where(r == 0, p2, jnp.where(r == 1, p1, pltpu.roll(g, 2, 0)))
    return g, s1, s2


def _conv_specs(tm, tc, ncb, order):
    gate = pl.BlockSpec((tm, tc), lambda *ids: order(ids))
    halo = pl.BlockSpec((HALO, tc), lambda *ids: (jnp.maximum(order(ids)[0] * (tm // HALO) - 1, 0), order(ids)[1]))
    up = pl.BlockSpec((tm, tc), lambda *ids: (order(ids)[0], ncb + order(ids)[1]))
    return gate, halo, up


def _conv_fwd(ffn, conv_w, conv_b, rows, tm):
    tc = CONV_TC
    ncb = D_FF // tc

    def body(g_ref, pg_ref, up_ref, cw_ref, cb_ref, act_ref):
        i = pl.program_id(0)
        g, s1, s2 = _conv_taps(i, tm, g_ref, pg_ref)
        conv = (cw_ref[0:1, :] * s2 + cw_ref[1:2, :] * s1 + cw_ref[2:3, :] * g) + cb_ref[...]
        act_ref[...] = (_silu(conv) * up_ref[...].astype(F32)).astype(BF16)

    gate, halo, up = _conv_specs(tm, tc, ncb, lambda ids: (ids[0], ids[1]))
    return pl.pallas_call(
        body,
        name="conv_fwd",
        out_shape=jax.ShapeDtypeStruct((rows, D_FF), BF16),
        grid=(rows // tm, ncb),
        in_specs=[gate, halo, up, pl.BlockSpec((3, tc), lambda i, j: (0, j)), pl.BlockSpec((1, tc), lambda i, j: (0, j))],
        out_specs=pl.BlockSpec((tm, tc), lambda i, j: (i, j)),
        compiler_params=_params(("parallel", "parallel")),
    )(ffn, ffn, ffn, conv_w, conv_b)


def _conv_bwd_a(ffn, dact, conv_w, conv_b, rows, tm):
    tc = CONV_TC
    ncb = D_FF // tc

    def body(g_ref, pg_ref, up_ref, da_ref, cw_ref, cb_ref, dc_ref, dffn_ref, w0_ref, w1_ref, w2_ref, db_ref):
        i = pl.program_id(1)
        g, s1, s2 = _conv_taps(i, tm, g_ref, pg_ref)
        conv = (cw_ref[0:1, :] * s2 + cw_ref[1:2, :] * s1 + cw_ref[2:3, :] * g) + cb_ref[...]
        da = da_ref[...].astype(F32)
        dffn_ref[...] = (da * _silu(conv)).astype(BF16)
        dc = da * up_ref[...].astype(F32) * _dsilu(conv)
        dc_ref[...] = dc.astype(BF16)
        sums = [jnp.sum(dc * s2, axis=0, keepdims=True), jnp.sum(dc * s1, axis=0, keepdims=True),
                jnp.sum(dc * g, axis=0, keepdims=True), jnp.sum(dc, axis=0, keepdims=True)]
        for ref, val in zip((w0_ref, w1_ref, w2_ref, db_ref), sums):
            @pl.when(i == 0)
            def _(ref=ref, val=val):
                ref[...] = val

            @pl.when(i > 0)
            def _(ref=ref, val=val):
                ref[...] += val

    gate, halo, up = _conv_specs(tm, tc, ncb, lambda ids: (ids[1], ids[0]))
    col = pl.BlockSpec((1, tc), lambda j, i: (0, j))
    return pl.pallas_call(
        body,
        name="conv_bwd_a",
        out_shape=[jax.ShapeDtypeStruct((rows, D_FF), BF16), jax.ShapeDtypeStruct((rows, 2 * D_FF), BF16)]
        + [jax.ShapeDtypeStruct((1, D_FF), F32)] * 4,
        grid=(ncb, rows // tm),
        in_specs=[gate, halo, up, pl.BlockSpec((tm, tc), lambda j, i: (i, j)),
                  pl.BlockSpec((3, tc), lambda j, i: (0, j)), col],
        out_specs=[pl.BlockSpec((tm, tc), lambda j, i: (i, j)), pl.BlockSpec((tm, tc), lambda j, i: (i, ncb + j)),
                   col, col, col, col],
        compiler_params=_params(("parallel", "arbitrary")),
    )(ffn, ffn, ffn, dact, conv_w, conv_b)


def _conv_bwd_b(dconv, conv_w, dffn, rows, tm):
    tc = CONV_TC
    ncb = D_FF // tc
    nrb = rows // tm

    def body(dc_ref, nx_ref, cw_ref, dffn_in, out_ref):
        del dffn_in
        i = pl.program_id(0)
        dc = dc_ref[...].astype(F32)
        r = lax.broadcasted_iota(jnp.int32, dc.shape, 0)
        last = i == nrb - 1
        nxt = nx_ref[...].astype(F32)
        x1 = jnp.where(last, 0.0, _halo_row(nxt, 0))
        x2 = jnp.where(last, 0.0, _halo_row(nxt, 1))
        n1 = jnp.where(r == tm - 1, x1, pltpu.roll(dc, tm - 1, 0))
        n2 = jnp.where(r == tm - 1, x2, jnp.where(r == tm - 2, x1, pltpu.roll(dc, tm - 2, 0)))
        dg = cw_ref[2:3, :] * dc + cw_ref[1:2, :] * n1 + cw_ref[0:1, :] * n2
        out_ref[...] = jnp.where(i * tm + r >= PAD_LEN, dg, 0.0).astype(BF16)

    return pl.pallas_call(
        body,
        name="conv_bwd_b",
        out_shape=jax.ShapeDtypeStruct((rows, 2 * D_FF), BF16),
        grid=(nrb, ncb),
        in_specs=[pl.BlockSpec((tm, tc), lambda i, j: (i, j)),
                  pl.BlockSpec((HALO, tc), lambda i, j: (jnp.minimum((i + 1) * (tm // HALO), rows // HALO - 1), j)),
                  pl.BlockSpec((3, tc), lambda i, j: (0, j)),
                  pl.BlockSpec(memory_space=pl.ANY)],
        out_specs=pl.BlockSpec((tm, tc), lambda i, j: (i, j)),
        input_output_aliases={3: 0},
        compiler_params=_params(("parallel", "parallel")),
    )(dconv, dconv, conv_w, dffn)


def _final_call(h1, y, target, g_final, rows):
    tm = BLOCK

    def fn(i, tm_, h1v, yv, tgt, g):
        h2 = h1v + yv
        out = _rms_fwd(h2, g)
        err = jnp.where(i > 0, out - tgt, 0.0)
        loss = 0.5 * jnp.sum(jnp.mean(err * err, axis=-1, keepdims=True), axis=0, keepdims=True)
        dx, dg = _rms_bwd(h2, g, err * (1.0 / D_MODEL))
        return dx, dx, jnp.broadcast_to(loss, (1, LANE)), dg

    n_in = 4
    in_specs = [pl.BlockSpec((tm, D_MODEL), lambda i: (i, 0)), pl.BlockSpec((tm, D_MODEL), lambda i: (i, 0)),
                pl.BlockSpec((tm, D_MODEL), lambda i: (jnp.maximum(i - 1, 0), 0)),
                pl.BlockSpec((1, D_MODEL), lambda i: (0, 0))]

    def body(*refs):
        i = pl.program_id(0)
        dx, dx2, loss, dg = fn(i, tm, *[r[...] for r in refs[:n_in]])
        refs[4][...] = dx
        refs[5][...] = dx2.astype(BF16)
        for ref, val in ((refs[6], loss), (refs[7], dg)):
            @pl.when(i == 0)
            def _(ref=ref, val=val):
                ref[...] = val

            @pl.when(i > 0)
            def _(ref=ref, val=val):
                ref[...] += val

    return pl.pallas_call(
        body,
        name="final_loss",
        out_shape=[jax.ShapeDtypeStruct((rows, D_MODEL), F32), jax.ShapeDtypeStruct((rows, D_MODEL), BF16),
                   jax.ShapeDtypeStruct((1, LANE), F32), jax.ShapeDtypeStruct((1, D_MODEL), F32)],
        grid=(rows // tm,),
        in_specs=in_specs,
        out_specs=[pl.BlockSpec((tm, D_MODEL), lambda i: (i, 0)), pl.BlockSpec((tm, D_MODEL), lambda i: (i, 0)),
                   pl.BlockSpec((1, LANE), lambda i: (0, 0)), pl.BlockSpec((1, D_MODEL), lambda i: (0, 0))],
        compiler_params=_params(("arbitrary",)),
    )(h1, y, target, g_final)


def _heads_map(fn, *slabs):
    outs = [fn(*[s[:, h * LANE:(h + 1) * LANE] for s in slabs]) for h in range(HEADS)]
    if isinstance(outs[0], tuple):
        return tuple(jnp.concatenate([o[k] for o in outs], axis=1) for k in range(len(outs[0])))
    return jnp.concatenate(outs, axis=1)


def _local_step(x, positions, target, w, p, emit=None):
    kept = {}
    if emit is None:
        def emit(group):
            kept.update(group)
            return None
    s_len = x.shape[0]
    rows = s_len + BLOCK
    tm = _tile(rows, 640, 8)
    row = lambda arr, width, cb=0: ("row", arr, width, cb)

    h0 = jnp.concatenate([jnp.zeros((PAD_LEN, D_MODEL), F32), w["meta_tokens"], x], axis=0)
    pos = jnp.concatenate([jnp.zeros((PAD_LEN,), jnp.int32), jnp.arange(N_META, dtype=jnp.int32),
                           positions.astype(jnp.int32) + N_META])
    inv = 1.0 / (ROPE_THETA ** (jnp.arange(0, ROPE, 2, dtype=F32) / ROPE))
    ang = pos.astype(F32)[:, None] * inv
    zero = jnp.zeros((rows, LANE - ROPE), F32)
    cos_t = jnp.concatenate([jnp.cos(ang), jnp.cos(ang), zero], axis=1)
    sin_t = jnp.concatenate([-jnp.sin(ang), jnp.sin(ang), zero], axis=1)
    lb_r0, lb_r1 = p["lb_raw"][0:1], p["lb_raw"][1:2]

    def lb_fn(i, tm_, r0, r1):
        m = jnp.maximum(r0, r1)
        e0, e1 = jnp.exp(r0 - m), jnp.exp(r1 - m)
        return (e0 / (e0 + e1),)

    (lb,) = _rowwise("lb_fwd", lb_fn, [("bc", lb_r0), ("bc", lb_r1)], [("acc", (1, D_MODEL))], 1, 1)

    (u1,) = _rowwise("mix_norm", lambda i, t, h, g: (_rms_fwd(h, g),),
                     [row(h0, D_MODEL), ("bc", p["g_mix_norm"])], [("row", D_MODEL, BF16)], rows, tm)
    proj = _matmul(u1, w["w_in"], "nn", F32, "mm_proj")
    hint = getattr(w, "hint", lambda name, after: None)
    hint("w_q_up", proj)
    qn, kvn = _rowwise(
        "latent_norm", lambda i, t, ql, kl, gq, gk: (_rms_fwd(ql, gq), _rms_fwd(kl, gk)),
        [row(proj, Q_LORA, 0), row(proj, KV_LORA, SEG_KV_LAT // KV_LORA), ("bc", p["g_q_norm"]), ("bc", p["g_kv_norm"])],
        [("row", Q_LORA, BF16), ("row", KV_LORA, BF16)], rows, tm)
    q_raw = _matmul(qn, w["w_q_up"], "nn", F32, "mm_q_up")
    kv = _matmul(kvn, w["w_kv_up"], "nn", F32, "mm_kv_up")
    q_att, k_att, v_att = _rope_fwd_call(q_raw, kv, proj, cos_t, sin_t, rows, tm)
    o32, o_bf, lse = _attn_fwd(q_att, k_att, v_att, rows)
    hint("w_branch_mla", lse)
    o_h, a_mat, s_states = _hgrn_fwd(proj, lb, rows)

    def hgrn_post(i, t, oh, hg, g):
        return (_heads_map(lambda a, b: _rms_fwd(a, g) * _silu(b), oh, hg),)

    (o_hgrn,) = _rowwise("hgrn_post", hgrn_post,
                         [row(o_h, D_MODEL), row(proj, D_MODEL, SEG_HG // D_MODEL), ("bc", p["g_hgrn_norm"])],
                         [("row", D_MODEL, BF16)], rows, tm)
    br_a = _matmul(o_bf, w["w_branch_mla"], "nn", F32, "mm_branch_mla")
    br_b = _matmul(o_hgrn, w["w_branch_hgrn"], "nn", F32, "mm_branch_hgrn")
    (merged,) = _rowwise(
        "merge", lambda i, t, a, b, ga, gb: (_sigmoid(ga) * a + _sigmoid(gb) * b,),
        [row(br_a, D_MODEL), row(br_b, D_MODEL), row(proj, D_MODEL, SEG_GA // D_MODEL), row(proj, D_MODEL, SEG_GB // D_MODEL)],
        [("row", D_MODEL, BF16)], rows, tm)
    mix_out = _matmul(merged, w["w_out"], "nn", F32, "mm_out")

    def ffn_norm(i, t, h, mo, g):
        h1v = h + mo
        return h1v, _rms_fwd(h1v, g)

    h1, u2 = _rowwise("ffn_norm", ffn_norm, [row(h0, D_MODEL), row(mix_out, D_MODEL), ("bc", p["g_ffn_norm"])],
                      [("row", D_MODEL, F32), ("row", D_MODEL, BF16)], rows, tm)
    ffn = _matmul(u2, w["w_ffn_in"], "nn", BF16, "mm_ffn_in")
    act = _conv_fwd(ffn, w["conv_w"], p["conv_b"], rows, tm)
    y = _matmul(act, w["w_ffn_out"], "nn", F32, "mm_ffn_out")
    dh2, dh2_bf, loss_acc, dg_final = _final_call(h1, y, target, p["g_final_norm"].reshape(1, D_MODEL), rows)

    grads = {"g_final_norm": dg_final.reshape(D_MODEL)}
    dact = _matmul(dh2_bf, w["w_ffn_out"], "nt", BF16, "mm_d_act")
    grads["w_ffn_out"] = _matmul(act, dh2_bf, "tn", BF16, "mm_dw_ffn_out")
    dconv, dffn, dcw0, dcw1, dcw2, dcb = _conv_bwd_a(ffn, dact, w["conv_w"], p["conv_b"], rows, tm)
    dffn = _conv_bwd_b(dconv, w["conv_w"], dffn, rows, tm)
    grads["conv_w"] = jnp.concatenate([dcw0, dcw1, dcw2], axis=0)
    grads["conv_b"] = dcb
    du2 = _matmul(dffn, w["w_ffn_in"], "nt", F32, "mm_d_u2")
    grads["w_ffn_in"] = _matmul(u2, dffn, "tn", BF16, "mm_dw_ffn_in")

    def ffn_norm_bwd(i, t, h, du, dh, g):
        dx, dg = _rms_bwd(h, g, du)
        dh1v = dh + dx
        return dh1v, dh1v, dg

    dh1, dh1_bf, grads["g_ffn_norm"] = _rowwise(
        "ffn_norm_bwd", ffn_norm_bwd, [row(h1, D_MODEL), row(du2, D_MODEL), row(dh2, D_MODEL), ("bc", p["g_ffn_norm"])],
        [("row", D_MODEL, F32), ("row", D_MODEL, BF16), ("acc", (1, D_MODEL))], rows, tm)
    tok = emit({n: grads.pop(n) for n in ("w_ffn_out", "w_ffn_in", "conv_w", "conv_b", "g_final_norm", "g_ffn_norm")})
    dmerged = _matmul(dh1_bf, w["w_out"], "nt", F32, "mm_d_merged", after=tok)
    grads["w_out"] = _matmul(merged, dh1_bf, "tn", BF16, "mm_dw_out")

    def merge_bwd(i, t, dm, a, b, ga, gb):
        sa, sb = _sigmoid(ga), _sigmoid(gb)
        return dm * sa, dm * sb, dm * a * sa * (1.0 - sa), dm * b * sb * (1.0 - sb)

    da_bf, db_bf, dga, dgb = _rowwise(
        "merge_bwd", merge_bwd,
        [row(dmerged, D_MODEL), row(br_a, D_MODEL), row(br_b, D_MODEL),
         row(proj, D_MODEL, SEG_GA // D_MODEL), row(proj, D_MODEL, SEG_GB // D_MODEL)],
        [("row", D_MODEL, BF16)] * 4, rows, tm)
    do_mla = _matmul(da_bf, w["w_branch_mla"], "nt", BF16, "mm_d_o_mla")
    grads["w_branch_mla"] = _matmul(o_bf, da_bf, "tn", BF16, "mm_dw_branch_mla")
    do_hgrn = _matmul(db_bf, w["w_branch_hgrn"], "nt", F32, "mm_d_o_hgrn")
    grads["w_branch_hgrn"] = _matmul(o_hgrn, db_bf, "tn", BF16, "mm_dw_branch_hgrn")

    def hgrn_post_bwd(i, t, dy, oh, hg, g):
        def one(dyh, ohh, hgh):
            dx, dg = _rms_bwd(ohh, g, dyh * _silu(hgh))
            return dx, dyh * _rms_fwd(ohh, g) * _dsilu(hgh), dg

        dx, dhg, dg = _heads_map(one, dy, oh, hg)
        dg_sum = dg[:, 0:LANE]
        for h in range(1, HEADS):
            dg_sum = dg_sum + dg[:, h * LANE:(h + 1) * LANE]
        return dx, dhg, dg_sum

    tok = emit({n: grads.pop(n) for n in ("w_out", "w_branch_mla", "w_branch_hgrn")})
    do_h, dhg, grads["g_hgrn_norm"] = _rowwise(
        "hgrn_post_bwd", hgrn_post_bwd,
        [row(do_hgrn, D_MODEL), row(o_h, D_MODEL), row(proj, D_MODEL, SEG_HG // D_MODEL), ("bc", p["g_hgrn_norm"])],
        [("row", D_MODEL, F32), ("row", D_MODEL, BF16), ("acc", (1, LANE))], rows, tm, after=tok)
    dhq, dhf, dhi, dlb = _hgrn_bwd(proj, lb, a_mat, s_states, do_h, rows)

    def lb_bwd(i, tm_, d, l):
        t = d * l * (1.0 - l)
        return t, -t

    dlb0, dlb1 = _rowwise("lb_bwd", lb_bwd, [("bc", dlb), ("bc", lb)], [("acc", (1, D_MODEL))] * 2, 1, 1)
    grads["lb_raw"] = jnp.concatenate([dlb0, dlb1], axis=0)

    delta = _attn_delta(do_mla, o32, rows, tm)
    dq_att, dk_att, dv_att = _attn_bwd(q_att, k_att, v_att, do_mla, lse.reshape(HEADS, 1, rows),
                                       jnp.transpose(delta[:, :HEADS]).reshape(HEADS, 1, rows), rows)
    dq_full, dkv, dkr = _rope_bwd_call(dq_att, dk_att, dv_att, cos_t, sin_t, rows, tm)
    dqn = _matmul(dq_full, w["w_q_up"], "nt", F32, "mm_d_qn")
    grads["w_q_up"] = _matmul(qn, dq_full, "tn", BF16, "mm_dw_q_up")
    dkvn = _matmul(dkv, w["w_kv_up"], "nt", F32, "mm_d_kvn")
    grads["w_kv_up"] = _matmul(kvn, dkv, "tn", BF16, "mm_dw_kv_up")

    def latent_norm_bwd(i, t, ql, kl, dq, dk, gq, gk):
        dql, dgq = _rms_bwd(ql, gq, dq)
        dkl, dgk = _rms_bwd(kl, gk, dk)
        return dql, dkl, dgq, dgk

    dq_lat, dkv_lat, grads["g_q_norm"], grads["g_kv_norm"] = _rowwise(
        "latent_norm_bwd", latent_norm_bwd,
        [row(proj, Q_LORA, 0), row(proj, KV_LORA, SEG_KV_LAT // KV_LORA), row(dqn, Q_LORA), row(dkvn, KV_LORA),
         ("bc", p["g_q_norm"]), ("bc", p["g_kv_norm"])],
        [("row", Q_LORA, BF16), ("row", KV_LORA, BF16), ("acc", (1, Q_LORA)), ("acc", (1, KV_LORA))], rows, tm)
    dproj = jnp.concatenate([dq_lat, dkv_lat, dhq, dhf, dhi, dhg, dga, dgb, dkr], axis=1)
    grads["w_in"] = _matmul(u1, dproj, "tn", BF16, "mm_dw_in")
    tok = emit({n: grads.pop(n) for n in ("w_in", "w_q_up", "w_kv_up", "lb_raw", "g_q_norm", "g_kv_norm", "g_hgrn_norm")})
    du1 = _matmul(dproj, w["w_in"], "nt", F32, "mm_d_u1", after=tok)

    def mix_norm_bwd(i, t, h, du, dh, g):
        dx, dg = _rms_bwd(h, g, du)
        return dh + dx, dh + dx, dg

    grad_x, d_prefix, grads["g_mix_norm"] = _rowwise(
        "mix_norm_bwd", mix_norm_bwd, [row(h0, D_MODEL), row(du1, D_MODEL), row(dh1, D_MODEL), ("bc", p["g_mix_norm"])],
        [("tail", D_MODEL, F32), ("head", D_MODEL, F32), ("acc", (1, D_MODEL))], rows, BLOCK)
    grads["meta_tokens"] = d_prefix[PAD_LEN:BLOCK]
    kept.update(grads)
    return loss_acc[0, 0], grad_x, kept


K_ROPE_AT = Q_LORA + KV_LORA
COL_SHARDED = ("w_in", "w_q_up", "w_kv_up", "w_ffn_in", "conv_w", "meta_tokens")
ROW_SHARDED = ("w_branch_mla", "w_branch_hgrn", "w_out", "w_ffn_out")
BIG = ("w_in", "w_q_up", "w_kv_up", "w_branch_mla", "w_branch_hgrn", "w_out", "w_ffn_in", "w_ffn_out")
SMALL = ("conv_b", "g_mix_norm", "g_q_norm", "g_kv_norm", "g_hgrn_norm", "g_ffn_norm", "g_final_norm", "lb_raw")


def _unshard(name, stacked):
    if name in COL_SHARDED:
        return jnp.transpose(stacked, (1, 0, 2)).reshape(stacked.shape[1], N_DEV * stacked.shape[2])
    return stacked.reshape(N_DEV * stacked.shape[1], stacked.shape[2])


def _reshard(name, full):
    if name in COL_SHARDED:
        r, c = full.shape
        return jnp.transpose(full.reshape(r, N_DEV, c // N_DEV), (1, 0, 2))
    return full.reshape(N_DEV, full.shape[0] // N_DEV, full.shape[1])


def _to_kernel_layout(full):
    out = dict(full)
    if "w_in" in full:
        w_in = full["w_in"]
        pad = jnp.zeros((D_MODEL, KR_W - ROPE), w_in.dtype)
        out["w_in"] = jnp.concatenate(
            [w_in[:, :K_ROPE_AT], w_in[:, K_ROPE_AT + ROPE:], w_in[:, K_ROPE_AT:K_ROPE_AT + ROPE], pad], axis=1)
    if "w_q_up" in full:
        wq = full["w_q_up"].reshape(Q_LORA, HEADS, NOPE + ROPE)
        out["w_q_up"] = jnp.pad(wq, ((0, 0), (0, 0), (0, QHEAD_W - NOPE - ROPE))).reshape(Q_LORA, HEADS * QHEAD_W)
    return out


def _from_kernel_layout(grads):
    out = dict(grads)
    if "w_in" in grads:
        g = grads["w_in"]
        out["w_in"] = jnp.concatenate([g[:, :K_ROPE_AT], g[:, SEG_KR:SEG_KR + ROPE], g[:, K_ROPE_AT:SEG_KR]], axis=1)
    if "w_q_up" in grads:
        g = grads["w_q_up"].reshape(Q_LORA, HEADS, QHEAD_W)
        out["w_q_up"] = g[:, :, :NOPE + ROPE].reshape(Q_LORA, HEADS * (NOPE + ROPE))
    return out


MESH_ID = pl.DeviceIdType.MESH
ANY = pl.BlockSpec(memory_space=pl.ANY)


def _slot(dev):
    return 4 * dev[0] + 2 * dev[1] + dev[2]


def _all_gather(shards):
    n = len(shards)

    def body(*refs):
        ins, outs = refs[:n], refs[n:2 * n]
        send_sems, recv_sems, local_sems = refs[2 * n:]
        x, y, c = lax.axis_index("x"), lax.axis_index("y"), lax.axis_index("c")
        me, sibling = (x, y, c), (x, y, 1 - c)
        chips = [(1 - x, y), (x, 1 - y), (1 - x, 1 - y)]

        def copy(a, k, block, to, src=None):
            dst = outs[a].at[_slot(block)]
            return pltpu.make_async_remote_copy(
                src_ref=dst if src is None else src, dst_ref=dst, send_sem=send_sems.at[a, k],
                recv_sem=recv_sems.at[a, k], device_id=to, device_id_type=MESH_ID)

        mine = [pltpu.make_async_copy(ins[a], outs[a].at[_slot(me)], local_sems.at[a]) for a in range(n)]
        for cp in mine:
            cp.start()
        first = []
        for a in range(n):
            first.append(copy(a, 0, me, sibling, src=ins[a]))
            first += [copy(a, 1 + j, me, (*chip, c), src=ins[a]) for j, chip in enumerate(chips)]
        for cp in first:
            cp.start()
        passed = []
        for a in range(n):
            for j, chip in enumerate(chips):
                copy(a, 1 + j, (*chip, c), me).wait_recv()
                fwd = copy(a, 4 + j, (*chip, c), sibling)
                fwd.start()
                passed.append(fwd)
        for a in range(n):
            copy(a, 0, sibling, me).wait_recv()
            for j, chip in enumerate(chips):
                copy(a, 4 + j, (*chip, 1 - c), me).wait_recv()
        for cp in first + passed:
            cp.wait_send()
        for cp in mine:
            cp.wait()

    return pl.pallas_call(
        body,
        name="gather_weights",
        out_shape=[jax.ShapeDtypeStruct((N_DEV,) + s.shape, s.dtype) for s in shards],
        in_specs=[ANY] * n,
        out_specs=[ANY] * n,
        scratch_shapes=[pltpu.SemaphoreType.DMA((n, 7)), pltpu.SemaphoreType.DMA((n, 7)), pltpu.SemaphoreType.DMA((n,))],
    )(*shards)


def _exchange(blocked, replicated):
    nb, n = len(blocked), len(blocked) + len(replicated)
    arrays = list(blocked) + list(replicated)

    def body(*refs):
        ins, outs = refs[:n], refs[n:2 * n]
        send_sems, recv_sems, local_sems = refs[2 * n:]
        x, y, c = lax.axis_index("x"), lax.axis_index("y"), lax.axis_index("c")
        me = (x, y, c)
        peers = [(x, y, 1 - c), (1 - x, y, c), (x, 1 - y, c), (1 - x, 1 - y, c),
                 (1 - x, y, 1 - c), (x, 1 - y, 1 - c), (1 - x, 1 - y, 1 - c)]

        def src_of(a, dev):
            return ins[a].at[_slot(dev)] if a < nb else ins[a]

        def copy(a, k, frm, to):
            return pltpu.make_async_remote_copy(
                src_ref=src_of(a, to), dst_ref=outs[a].at[_slot(frm)], send_sem=send_sems.at[a, k],
                recv_sem=recv_sems.at[a, k], device_id=to, device_id_type=MESH_ID)

        mine = [pltpu.make_async_copy(src_of(a, me), outs[a].at[_slot(me)], local_sems.at[a]) for a in range(n)]
        for cp in mine:
            cp.start()
        sends = [copy(a, k, me, peer) for a in range(n) for k, peer in enumerate(peers)]
        for cp in sends:
            cp.start()
        for a in range(n):
            for k, peer in enumerate(peers):
                copy(a, k, peer, me).wait_recv()
        for cp in sends:
            cp.wait_send()
        for cp in mine:
            cp.wait()

    return pl.pallas_call(
        body,
        name="exchange_grads",
        out_shape=[jax.ShapeDtypeStruct(s.shape, s.dtype) for s in blocked]
        + [jax.ShapeDtypeStruct((N_DEV,) + s.shape, s.dtype) for s in replicated],
        in_specs=[ANY] * n,
        out_specs=[ANY] * n,
        scratch_shapes=[pltpu.SemaphoreType.DMA((n, 7)), pltpu.SemaphoreType.DMA((n, 7)), pltpu.SemaphoreType.DMA((n,))],
    )(*arrays)


ADAMW_BLOCK_ELEMS = 256 * 1024


def _adamw(name, parts, w, m, v, own=None, me=None):
    r, c = w.shape
    tr = _tile(r, max(16, ADAMW_BLOCK_ELEMS // c), 16)

    def body(*refs):
        if own is None:
            p_ref, w_ref, m_ref, v_ref, g_ref, d_ref, nm_ref, nv_ref = refs
            terms = [p_ref[s].astype(F32) for s in range(N_DEV)]
        else:
            me_ref, p_ref, own_ref, w_ref, m_ref, v_ref, g_ref, d_ref, nm_ref, nv_ref = refs
            mine = own_ref[0].astype(F32)
            terms = [jnp.where(me_ref[0] == s, mine, p_ref[s].astype(F32)) for s in range(N_DEV)]
        g = terms[0]
        for s in range(1, N_DEV):
            g = g + terms[s]
        m_new = ADAM_B1 * m_ref[...] + (1.0 - ADAM_B1) * g
        v_new = ADAM_B2 * v_ref[...] + (1.0 - ADAM_B2) * (g * g)
        m_hat = m_new / (1.0 - ADAM_B1 ** ADAM_STEP)
        v_hat = v_new / (1.0 - ADAM_B2 ** ADAM_STEP)
        g_ref[...] = g
        d_ref[...] = -ADAM_LR * (m_hat / (jnp.sqrt(v_hat) + ADAM_EPS) + ADAM_WD * w_ref[...])
        nm_ref[...] = m_new
        nv_ref[...] = v_new

    if own is None:
        blk = pl.BlockSpec((tr, c), lambda i: (i, 0))
        return pl.pallas_call(
            body,
            name="adamw_" + name,
            out_shape=[jax.ShapeDtypeStruct((r, c), F32)] * 4,
            grid=(r // tr,),
            in_specs=[pl.BlockSpec((N_DEV, tr, c), lambda i: (0, i, 0)), blk, blk, blk],
            out_specs=[blk] * 4,
            compiler_params=_params(("parallel",)),
        )(parts, w, m, v)
    blk = pl.BlockSpec((tr, c), lambda i, me_ref: (i, 0))
    own_at = (lambda i, me_ref: (me_ref[0], i, 0)) if own.shape[0] == N_DEV else (lambda i, me_ref: (0, i, 0))
    return pl.pallas_call(
        body,
        name="adamw_" + name,
        out_shape=[jax.ShapeDtypeStruct((r, c), F32)] * 4,
        grid_spec=pltpu.PrefetchScalarGridSpec(
            num_scalar_prefetch=1,
            grid=(r // tr,),
            in_specs=[pl.BlockSpec((N_DEV, tr, c), lambda i, me_ref: (0, i, 0)), pl.BlockSpec((1, tr, c), own_at),
                      blk, blk, blk],
            out_specs=[blk] * 4),
        compiler_params=_params(("parallel",)),
    )(me, parts, own, w, m, v)


HBM_SPEC = pl.BlockSpec(memory_space=pltpu.HBM)
SEM_SPEC = pl.BlockSpec(memory_space=pltpu.SEMAPHORE)
SIDE_EFFECT = pltpu.SideEffectType.DATAFLOW_SIDE_EFFECTING
N_PEERS = N_DEV - 1


def _peers(x, y, c):
    return [(x, y, 1 - c), (1 - x, y, c), (x, 1 - y, c), (1 - x, 1 - y, c),
            (1 - x, y, 1 - c), (x, 1 - y, 1 - c), (1 - x, 1 - y, 1 - c)]


def _split_copy(srcs, lands, blocked, send_sems, recv_sems, a, k, frm, to):
    src = srcs[a].at[_slot(to)] if blocked[a] else srcs[a]
    return pltpu.make_async_remote_copy(
        src_ref=src, dst_ref=lands[a].at[_slot(frm)], send_sem=send_sems.at[a * N_PEERS + k],
        recv_sem=recv_sems.at[a * N_PEERS + k],
        device_id=to, device_id_type=MESH_ID)


def _exchange_start(name, srcs, lands, blocked, after=()):
    n = len(srcs)
    after = list(after)

    def body(*refs):
        src_refs, land_refs = refs[:n], refs[n:2 * n]
        send_sems, recv_sems = refs[2 * n + len(after)], refs[2 * n + len(after) + 1]
        token = refs[-1]
        x, y, c = lax.axis_index("x"), lax.axis_index("y"), lax.axis_index("c")
        for a in range(n):
            for k, peer in enumerate(_peers(x, y, c)):
                _split_copy(src_refs, land_refs, blocked, send_sems, recv_sems, a, k, (x, y, c), peer).start()
        token[...] = jnp.zeros_like(token)

    thru = [pltpu.HBM(s.shape, s.dtype) for s in list(srcs) + list(lands)]
    res = pl.pallas_call(
        body,
        name=name,
        out_shape=(pltpu.SemaphoreType.DMA((n * N_PEERS,)), pltpu.SemaphoreType.DMA((n * N_PEERS,)), *thru,
                   jax.ShapeDtypeStruct((8, LANE), F32)),
        in_specs=[HBM_SPEC] * (2 * n) + [pl.BlockSpec(memory_space=pl.ANY)] * len(after),
        out_specs=(SEM_SPEC, SEM_SPEC, *([HBM_SPEC] * (2 * n)), pl.BlockSpec(memory_space=pltpu.VMEM)),
        input_output_aliases={i: 2 + i for i in range(2 * n)},
        compiler_params=pltpu.CompilerParams(has_side_effects=SIDE_EFFECT),
    )(*[pltpu.with_memory_space_constraint(s, pltpu.HBM) for s in list(srcs) + list(lands)], *after)
    return res[0], res[1], res[2:2 + n], res[2 + n:2 + 2 * n], res[-1]


def _exchange_wait(name, send_sems, recv_sems, srcs, lands, blocked, after):
    n, n_after = len(srcs), len(after)

    def body(*refs):
        src_refs, land_refs = refs[:n], refs[n:2 * n]
        send, recv = refs[2 * n], refs[2 * n + 1]
        x, y, c = lax.axis_index("x"), lax.axis_index("y"), lax.axis_index("c")
        for a in range(n):
            for k, peer in enumerate(_peers(x, y, c)):
                _split_copy(src_refs, land_refs, blocked, send, recv, a, k, (x, y, c), peer).wait_send()
                _split_copy(src_refs, land_refs, blocked, send, recv, a, k, peer, (x, y, c)).wait_recv()

    res = pl.pallas_call(
        body,
        name=name,
        out_shape=tuple(pltpu.HBM(s.shape, s.dtype) for s in list(srcs) + list(lands)),
        in_specs=[HBM_SPEC] * (2 * n) + [SEM_SPEC, SEM_SPEC] + [pl.BlockSpec(memory_space=pl.ANY)] * n_after,
        out_specs=tuple([HBM_SPEC] * (2 * n)),
        input_output_aliases={i: i for i in range(2 * n)},
        compiler_params=pltpu.CompilerParams(has_side_effects=SIDE_EFFECT),
    )(*srcs, *lands, send_sems, recv_sems, *after)
    return res[:n], res[n:]


class _LazyWeights:
    def __init__(self):
        self.ready, self.groups, self.hints = {}, {}, {}

    def add_group(self, wait_name, names, send, recv, srcs, lands):
        for n in names:
            self.groups[n] = (wait_name, names, send, recv, srcs, lands)

    def hint(self, name, after):
        self.hints[self.groups[name][0]] = after

    def __getitem__(self, name):
        if name not in self.ready:
            wait_name, names, send, recv, srcs, lands = self.groups[name]
            after = [self.hints[wait_name]] if wait_name in self.hints else []
            _, whole = _exchange_wait(wait_name, send, recv, srcs, lands, [False] * len(names), after)
            for n, stacked in zip(names, whole):
                self.ready[n] = _to_kernel_layout({n: _unshard(n, stacked)})[n]
        return self.ready[name]


def kernel(x, positions, meta_tokens, w_in, w_q_up, w_kv_up, w_branch_mla, w_branch_hgrn, w_out, w_ffn_in, w_ffn_out, conv_w, conv_b, g_mix_norm, g_q_norm, g_kv_norm, g_hgrn_norm, g_ffn_norm, g_final_norm, lb_raw, loss_target, m_meta_tokens, m_w_in, m_w_q_up, m_w_kv_up, m_w_branch_mla, m_w_branch_hgrn, m_w_out, m_w_ffn_in, m_w_ffn_out, m_conv_w, m_conv_b, m_g_mix_norm, m_g_q_norm, m_g_kv_norm, m_g_hgrn_norm, m_g_ffn_norm, m_g_final_norm, m_lb_raw, v_meta_tokens, v_w_in, v_w_q_up, v_w_kv_up, v_w_branch_mla, v_w_branch_hgrn, v_w_out, v_w_ffn_in, v_w_ffn_out, v_conv_w, v_conv_b, v_g_mix_norm, v_g_q_norm, v_g_kv_norm, v_g_hgrn_norm, v_g_ffn_norm, v_g_final_norm, v_lb_raw):
    local = dict(zip(
        ("meta_tokens", "w_in", "w_q_up", "w_kv_up", "w_branch_mla", "w_branch_hgrn", "w_out", "w_ffn_in", "w_ffn_out",
         "conv_w", "conv_b", "g_mix_norm", "g_q_norm", "g_kv_norm", "g_hgrn_norm", "g_ffn_norm", "g_final_norm", "lb_raw"),
        (meta_tokens, w_in, w_q_up, w_kv_up, w_branch_mla, w_branch_hgrn, w_out, w_ffn_in, w_ffn_out,
         conv_w, conv_b, g_mix_norm, g_q_norm, g_kv_norm, g_hgrn_norm, g_ffn_norm, g_final_norm, lb_raw)))
    mom_m = dict(zip(local, (m_meta_tokens, m_w_in, m_w_q_up, m_w_kv_up, m_w_branch_mla, m_w_branch_hgrn, m_w_out, m_w_ffn_in,
                             m_w_ffn_out, m_conv_w, m_conv_b, m_g_mix_norm, m_g_q_norm, m_g_kv_norm, m_g_hgrn_norm,
                             m_g_ffn_norm, m_g_final_norm, m_lb_raw)))
    mom_v = dict(zip(local, (v_meta_tokens, v_w_in, v_w_q_up, v_w_kv_up, v_w_branch_mla, v_w_branch_hgrn, v_w_out, v_w_ffn_in,
                             v_w_ffn_out, v_conv_w, v_conv_b, v_g_mix_norm, v_g_q_norm, v_g_kv_norm, v_g_hgrn_norm,
                             v_g_ffn_norm, v_g_final_norm, v_lb_raw)))
    sharded = BIG + ("conv_w", "meta_tokens")

    def shard2d(name, arr):
        return arr.reshape(arr.shape[-2:]) if name != "meta_tokens" else arr

    def as2d(name, arr):
        return arr.reshape(1, -1) if arr.ndim == 1 else shard2d(name, arr)

    me = 4 * lax.axis_index("x") + 2 * lax.axis_index("y") + lax.axis_index("c")

    def landing(own):
        zone = lax.empty((N_DEV,) + own.shape[1:], own.dtype)
        return lax.dynamic_update_slice_in_dim(zone, own, me, 0)

    shards = {n: shard2d(n, local[n]).astype(BF16) for n in BIG}
    shards.update({n: shard2d(n, local[n]) for n in ("conv_w", "meta_tokens")})
    full = _LazyWeights()
    first = ("w_in", "meta_tokens")
    gathered = _all_gather([shards[n] for n in first])
    for n, g in zip(first, gathered):
        full.ready[n] = _to_kernel_layout({n: _unshard(n, g)})[n]
    later = (("w_q_up", "w_kv_up"), ("w_branch_mla", "w_branch_hgrn", "w_out", "w_ffn_in", "w_ffn_out", "conv_w"))
    for k, names in enumerate(later):
        srcs = [shards[n] for n in names]
        send, recv, srcs_thru, lands_thru, _ = _exchange_start(
            f"gather_start_{k}", srcs, [landing(s[None]) for s in srcs], [False] * len(names), after=[gathered[0]])
        full.add_group(f"gather_wait_{k}", names, send, recv, srcs_thru, lands_thru)
    small = {n: local[n] for n in SMALL}

    started = []

    def sources(group):
        group = _from_kernel_layout(group)
        names = list(group)
        blocked = [n in sharded for n in names]
        srcs = [_reshard(n, group[n]) if b else as2d(n, group[n]) for n, b in zip(names, blocked)]
        return names, blocked, srcs

    def emit(group):
        names, blocked, srcs = sources(group)
        lands = [lax.empty((N_DEV,) + (s.shape[1:] if b else s.shape), s.dtype) for s, b in zip(srcs, blocked)]
        k = len(started)
        send, recv, srcs_thru, lands_thru, token = _exchange_start(f"exchange_start_{k}", srcs, lands, blocked)
        started.append((names, blocked, send, recv, srcs_thru, lands_thru))
        return token

    loss, grad_x, last = _local_step(x[0], positions[0], loss_target[0], full, small, emit)

    out = {}

    me_arr = me.astype(jnp.int32).reshape(1)

    def update(names, parts, owns=None):
        for k, (n, part) in enumerate(zip(names, parts)):
            own = None if owns is None else (owns[k] if owns[k].ndim == 3 else owns[k][None])
            res = _adamw(n, part, as2d(n, local[n]), as2d(n, mom_m[n]), as2d(n, mom_v[n]), own,
                         None if owns is None else me_arr)
            out[n] = [r.reshape(local[n].shape) for r in res]

    after = [last["g_mix_norm"]]
    for k, (names, blocked, send, recv, srcs_thru, lands_thru) in enumerate(started):
        srcs_done, parts = _exchange_wait(f"exchange_wait_{k}", send, recv, srcs_thru, lands_thru, blocked, after)
        update(names, parts, srcs_done)
        after = [out[names[0]][0]]
    names, blocked, srcs = sources(last)
    in_blocks = [(n, s) for n, s, b in zip(names, srcs, blocked) if b]
    whole = [(n, s) for n, s, b in zip(names, srcs, blocked) if not b]
    update([n for n, _ in in_blocks + whole], _exchange([s for _, s in in_blocks], [s for _, s in whole]))

    loss = lax.psum(loss, ("x", "y", "c"))
    order = tuple(local)
    return (loss, grad_x[None], *[out[n][0] for n in order], *[out[n][1] for n in order],
            *[out[n][2] for n in order], *[out[n][3] for n in order])
```

```python
import functools

import jax
import jax.numpy as jnp
import numpy as np
from jax import lax
from jax.experimental import pallas as pl
from jax.experimental.pallas import tpu as pltpu

F32 = jnp.float32
BF16 = jnp.bfloat16

D_MODEL = 2048
N_META = 16
BLOCK = 128
PAD_LEN = BLOCK - N_META
HEADS = 16
Q_LORA = 1536
KV_LORA = 512
ROPE = 64
NOPE = 128
VDIM = 128
D_FF = 5632
NORM_EPS = 1e-6
ROPE_THETA = 10000.0
ATTN_SCALE = (NOPE + ROPE) ** -0.5
ADAM_LR = 0.001
ADAM_B1 = 0.9
ADAM_B2 = 0.999
ADAM_EPS = 1e-08
ADAM_WD = 0.01
ADAM_STEP = 10
N_DEV = 8

LANE = 128
SEG_Q_LAT = 0
SEG_KV_LAT = Q_LORA
SEG_HQ = 2048
SEG_HF = SEG_HQ + D_MODEL
SEG_HI = SEG_HF + D_MODEL
SEG_HG = SEG_HI + D_MODEL
SEG_GA = SEG_HG + D_MODEL
SEG_GB = SEG_GA + D_MODEL
SEG_KR = SEG_GB + D_MODEL
KR_W = 256
PROJ_W = SEG_KR + KR_W
QHEAD_W = 256

V7X_VMEM_BYTES = 64 * 1024 * 1024
VMEM_LIMIT = V7X_VMEM_BYTES * 7 // 8
NEG_BIG = -1e30
SUB = 8


def _tile(n, target, mult):
    best = None
    for t in range(mult, min(n, target) + 1, mult):
        if n % t == 0:
            best = t
    return n if best is None else best


def _params(sem):
    return pltpu.CompilerParams(dimension_semantics=sem, vmem_limit_bytes=VMEM_LIMIT)


def _sigmoid(x):
    return 0.5 * jnp.tanh(0.5 * x) + 0.5


_DIMS = {"nn": (((1,), (0,)), ((), ())), "nt": (((1,), (1,)), ((), ())), "tn": (((0,), (0,)), ((), ()))}


def _matmul(a, b, mode, out_dtype, name, after=None, col_blocks=1):
    if mode == "nn":
        (m, k), (_, n) = a.shape, b.shape
    elif mode == "nt":
        (m, k), (n, _) = a.shape, b.shape
    else:
        (k, m), (_, n) = a.shape, b.shape
    tm = _tile(m, 1040, 8) if mode != "tn" else _tile(m, 1024, LANE)
    tn = _tile(n, 1024, LANE) if col_blocks == 1 else n // col_blocks
    tk = _tile(k, 2816, LANE) if mode != "tn" else _tile(k, 2080, 8)
    nk = k // tk
    if mode == "nn":
        a_spec = pl.BlockSpec((tm, tk), lambda i, j, kk: (i, kk))
        b_spec = pl.BlockSpec((tk, tn), lambda i, j, kk: (kk, j))
    elif mode == "nt":
        a_spec = pl.BlockSpec((tm, tk), lambda i, j, kk: (i, kk))
        b_spec = pl.BlockSpec((tn, tk), lambda i, j, kk: (j, kk))
    else:
        a_spec = pl.BlockSpec((tk, tm), lambda i, j, kk: (kk, i))
        b_spec = pl.BlockSpec((tk, tn), lambda i, j, kk: (kk, j))
    dims = _DIMS[mode]

    n_after = 0 if after is None else 1

    def body(a_ref, b_ref, *rest):
        o_ref, acc = rest[n_after], rest[n_after + 1:]
        part = lax.dot_general(a_ref[...], b_ref[...], dims, preferred_element_type=F32)
        if nk == 1:
            o_ref[...] = part.astype(o_ref.dtype)
            return
        acc_ref, kk = acc[0], pl.program_id(2)

        @pl.when(kk == 0)
        def _():
            acc_ref[...] = part

        @pl.when((kk > 0) & (kk < nk - 1))
        def _():
            acc_ref[...] += part

        @pl.when(kk == nk - 1)
        def _():
            o_ref[...] = (acc_ref[...] + part).astype(o_ref.dtype)

    if col_blocks == 1:
        out_shape = jax.ShapeDtypeStruct((m, n), out_dtype)
        out_spec = pl.BlockSpec((tm, tn), lambda i, j, kk: (i, j))
    else:
        out_shape = jax.ShapeDtypeStruct((col_blocks, m, tn), out_dtype)
        out_spec = pl.BlockSpec((None, tm, tn), lambda i, j, kk: (j, i, 0))
    return pl.pallas_call(
        body,
        name=name,
        out_shape=out_shape,
        grid=(m // tm, n // tn, nk),
        in_specs=[a_spec, b_spec] + [pl.BlockSpec(memory_space=pl.ANY)] * n_after,
        out_specs=out_spec,
        scratch_shapes=[pltpu.VMEM((tm, tn), F32)] if nk > 1 else [],
        compiler_params=_params(("parallel", "parallel", "arbitrary")),
    )(a, b, *([after] * n_after))


ROW_WINDOW_BYTES = 12 * 1024 * 1024


def _rowwise(name, fn, ins, outs, rows, tm, after=None):
    per_row = sum(s[2] * s[1].dtype.itemsize for s in ins if s[0] == "row")
    per_row += sum(s[1] * jnp.dtype(s[2]).itemsize for s in outs if s[0] in ("row", "tail"))
    if per_row:
        tm = _tile(rows, min(tm, max(8, ROW_WINDOW_BYTES // (2 * per_row))), 8)
    n_in = len(ins)
    in_specs, args = [], []
    for spec in ins:
        if spec[0] == "row":
            _, arr, w, cb = spec
            in_specs.append(pl.BlockSpec((tm, w), functools.partial(lambda i, cb: (i, cb), cb=cb)))
        else:
            arr = spec[1]
            in_specs.append(pl.BlockSpec(arr.shape, lambda i: (0, 0)))
        args.append(arr)
    out_shape, out_specs = [], []
    for spec in outs:
        if spec[0] == "row":
            out_shape.append(jax.ShapeDtypeStruct((rows, spec[1]), spec[2]))
            out_specs.append(pl.BlockSpec((tm, spec[1]), lambda i: (i, 0)))
        elif spec[0] == "tail":
            out_shape.append(jax.ShapeDtypeStruct((rows - tm, spec[1]), spec[2]))
            out_specs.append(pl.BlockSpec((tm, spec[1]), lambda i: (jnp.maximum(i - 1, 0), 0)))
        elif spec[0] == "head":
            out_shape.append(jax.ShapeDtypeStruct((tm, spec[1]), spec[2]))
            out_specs.append(pl.BlockSpec((tm, spec[1]), lambda i: (0, 0)))
        else:
            out_shape.append(jax.ShapeDtypeStruct(spec[1], F32))
            out_specs.append(pl.BlockSpec(spec[1], lambda i: (0, 0)))
    has_acc = any(s[0] != "row" for s in outs)
    n_after = 0 if after is None else 1
    in_specs += [pl.BlockSpec(memory_space=pl.ANY)] * n_after
    args += [after] * n_after

    def body(*refs):
        i = pl.program_id(0)
        res = fn(i, tm, *[r[...] for r in refs[:n_in]])
        for spec, ref, val in zip(outs, refs[n_in + n_after:], res):
            if spec[0] in ("row", "tail"):
                ref[...] = val.astype(ref.dtype)
            elif spec[0] == "head":
                @pl.when(i == 0)
                def _(ref=ref, val=val):
                    ref[...] = val.astype(ref.dtype)
            else:
                @pl.when(i == 0)
                def _(ref=ref, val=val):
                    ref[...] = val

                @pl.when(i > 0)
                def _(ref=ref, val=val):
                    ref[...] += val

    return pl.pallas_call(
        body,
        name=name,
        out_shape=out_shape,
        grid=(rows // tm,),
        in_specs=in_specs,
        out_specs=out_specs,
        compiler_params=_params(("arbitrary" if has_acc else "parallel",)),
    )(*args)


def _row_ids(i, tm, shape):
    return i * tm + lax.broadcasted_iota(jnp.int32, shape, 0)


def _rms_fwd(x, g):
    r = lax.rsqrt(jnp.mean(x * x, axis=-1, keepdims=True) + NORM_EPS)
    return x * r * g


def _rms_bwd(x, g, dy):
    r = lax.rsqrt(jnp.mean(x * x, axis=-1, keepdims=True) + NORM_EPS)
    xhat = x * r
    dxhat = dy * g
    dx = r * (dxhat - xhat * jnp.mean(dxhat * xhat, axis=-1, keepdims=True))
    return dx, jnp.sum(dy * xhat, axis=0, keepdims=True)


def _silu(x):
    return x * _sigmoid(x)


def _dsilu(x):
    s = _sigmoid(x)
    return s * (1.0 + x * (1.0 - s))


def _rot_src(x):
    lane = lax.broadcasted_iota(jnp.int32, x.shape, 1)
    return jnp.where(lane < ROPE // 2, pltpu.roll(x, LANE - ROPE // 2, 1), pltpu.roll(x, ROPE // 2, 1))


def _rope_fwd_call(q_raw, kv, proj, cos_t, sin_t, rows, tm):
    def fn(i, tm_, q, kvv, kr, c, s):
        kr_rot = kr[:, :LANE]
        kr_rot = kr_rot * c + _rot_src(kr_rot) * s
        qs, ks, vs = [], [], []
        for h in range(HEADS):
            qn = q[:, h * QHEAD_W:h * QHEAD_W + NOPE]
            qr = q[:, h * QHEAD_W + NOPE:(h + 1) * QHEAD_W]
            qs += [qn * SCORE_TO_LOG2, (qr * c + _rot_src(qr) * s) * SCORE_TO_LOG2]
            ks += [kvv[:, h * 2 * NOPE:h * 2 * NOPE + NOPE], kr_rot]
            vs += [kvv[:, h * 2 * NOPE + NOPE:(h + 1) * 2 * NOPE]]
        return jnp.concatenate(qs, axis=1), jnp.concatenate(ks, axis=1), jnp.concatenate(vs, axis=1)

    return _rowwise(
        "rope_fwd", fn,
        [("row", q_raw, HEADS * QHEAD_W, 0), ("row", kv, HEADS * 2 * NOPE, 0), ("row", proj, KR_W, SEG_KR // KR_W),
         ("row", cos_t, LANE, 0), ("row", sin_t, LANE, 0)],
        [("row", HEADS * QHEAD_W, BF16), ("row", HEADS * QHEAD_W, BF16), ("row", HEADS * VDIM, BF16)],
        rows, tm)


def _rope_bwd_call(dq_att, dk_att, dv, cos_t, sin_t, rows, tm):
    def fn(i, tm_, dq, dk, dvv, c, s):
        qs, kvs = [], []
        dkr = jnp.zeros((dq.shape[0], LANE), F32)
        for h in range(HEADS):
            dqr = dq[:, h * QHEAD_W + NOPE:(h + 1) * QHEAD_W] * ATTN_SCALE
            qs += [dq[:, h * QHEAD_W:h * QHEAD_W + NOPE] * ATTN_SCALE, dqr * c - _rot_src(dqr) * s]
            kvs += [dk[:, h * QHEAD_W:h * QHEAD_W + NOPE], dvv[:, h * VDIM:(h + 1) * VDIM]]
            dkr = dkr + dk[:, h * QHEAD_W + NOPE:(h + 1) * QHEAD_W]
        dkr = dkr * c - _rot_src(dkr) * s
        return (jnp.concatenate(qs, axis=1), jnp.concatenate(kvs, axis=1),
                jnp.concatenate([dkr, jnp.zeros_like(dkr)], axis=1))

    return _rowwise(
        "rope_bwd", fn,
        [("row", dq_att, HEADS * QHEAD_W, 0), ("row", dk_att, HEADS * QHEAD_W, 0), ("row", dv, HEADS * VDIM, 0),
         ("row", cos_t, LANE, 0), ("row", sin_t, LANE, 0)],
        [("row", HEADS * QHEAD_W, BF16), ("row", HEADS * 2 * NOPE, BF16), ("row", KR_W, BF16)],
        rows, tm)


def _attn_mask(q_blk, k_blk, t, keys_on_rows=False):
    qa, ka = (1, 0) if keys_on_rows else (0, 1)
    qs = q_blk * t + lax.broadcasted_iota(jnp.int32, (t, t), qa)
    ks = k_blk * t + lax.broadcasted_iota(jnp.int32, (t, t), ka)
    return (ks <= qs) & ((ks >= PAD_LEN) | (ks == qs))


_NT = _DIMS["nt"]
_TN = _DIMS["tn"]
LOG2E = 1.4426950408889634
SCORE_TO_LOG2 = ATTN_SCALE * LOG2E


def _causal_pairs(nb, by_key):
    if by_key:
        pairs = [(qi, kj) for kj in range(nb) for qi in range(kj, nb)]
    else:
        pairs = [(qi, kj) for qi in range(nb) for kj in range(qi + 1)]
    return (jnp.asarray(np.array([p[0] for p in pairs], np.int32)), jnp.asarray(np.array([p[1] for p in pairs], np.int32)))


def _two_parts(t):
    cut = (t // LANE + 1) // 2 * LANE
    return ((0, cut), (cut, t)) if cut < t else ((0, t),)


def _attn_fwd(q_att, k_att, v, rows):
    t = _tile(rows, 640, LANE)
    nb = rows // t

    def body(qt_ref, kt_ref, q_ref, k_ref, v_ref, o32_ref, obf_ref, lse_ref, m_sc, l_sc, acc_sc):
        qi, kj = qt_ref[pl.program_id(1)], kt_ref[pl.program_id(1)]

        @pl.when(kj == 0)
        def _():
            m_sc[...] = jnp.full_like(m_sc, NEG_BIG)
            l_sc[...] = jnp.zeros_like(l_sc)
            acc_sc[...] = jnp.zeros_like(acc_sc)

        def step(masked):
            q = q_ref[...]
            parts = _two_parts(t)
            scores =[lax.dot_general(q, k_ref[lo:hi, :], _NT, preferred_element_type=F32) for lo, hi in parts]
            m, l, acc = m_sc[...], l_sc[...], acc_sc[...]
            for (lo, hi), s in zip(parts, scores):
                if masked:
                    qs = qi * t + lax.broadcasted_iota(jnp.int32, (t, hi - lo), 0)
                    ks = kj * t + lo + lax.broadcasted_iota(jnp.int32, (t, hi - lo), 1)
                    s = jnp.where((ks <= qs) & ((ks >= PAD_LEN) | (ks == qs)), s, NEG_BIG)
                m_new = jnp.maximum(m, jnp.max(s, axis=1, keepdims=True))
                alpha = jnp.exp2(m - m_new)
                p = jnp.exp2(s - jnp.tile(m_new, (1, (hi - lo) // LANE)))
                l = alpha * l + jnp.sum(p, axis=1, keepdims=True)
                acc = alpha * acc + jnp.dot(p.astype(BF16), v_ref[lo:hi, :], preferred_element_type=F32)
                m = m_new
            m_sc[...], l_sc[...], acc_sc[...] = m, l, acc

        pl.when((kj == qi) | (kj == 0))(functools.partial(step, True))
        pl.when((kj < qi) & (kj > 0))(functools.partial(step, False))

        @pl.when(kj == qi)
        def _():
            o = acc_sc[...] / l_sc[...]
            o32_ref[...] = o
            obf_ref[...] = o.astype(BF16)
            lse_ref[0] = jnp.max((m_sc[...] + jnp.log2(l_sc[...])).T, axis=0, keepdims=True)

    qt, kt = _causal_pairs(nb, by_key=False)
    qmap = lambda h, p, qt_ref, kt_ref: (qt_ref[p], h)
    kmap = lambda h, p, qt_ref, kt_ref: (kt_ref[p], h)
    return pl.pallas_call(
        body,
        name="attn_fwd",
        out_shape=[jax.ShapeDtypeStruct((rows, HEADS * VDIM), F32), jax.ShapeDtypeStruct((rows, HEADS * VDIM), BF16),
                   jax.ShapeDtypeStruct((HEADS, 1, rows), F32)],
        grid_spec=pltpu.PrefetchScalarGridSpec(
            num_scalar_prefetch=2,
            grid=(HEADS, len(qt)),
            in_specs=[pl.BlockSpec((t, QHEAD_W), qmap), pl.BlockSpec((t, QHEAD_W), kmap), pl.BlockSpec((t, VDIM), kmap)],
            out_specs=[pl.BlockSpec((t, VDIM), qmap), pl.BlockSpec((t, VDIM), qmap),
                       pl.BlockSpec((1, 1, t), lambda h, p, qt_ref, kt_ref: (h, 0, qt_ref[p]))],
            scratch_shapes=[pltpu.VMEM((t, LANE), F32), pltpu.VMEM((t, LANE), F32), pltpu.VMEM((t, VDIM), F32)]),
        compiler_params=_params(("parallel", "arbitrary")),
    )(qt, kt, q_att, k_att, v)


def _attn_delta(do, o32, rows, tm):
    def fn(i, tm_, dov, ov):
        prod = dov.astype(F32) * ov
        head_of = lax.broadcasted_iota(jnp.int32, (HEADS * VDIM, LANE), 0) // VDIM
        pick = jnp.where(head_of == lax.broadcasted_iota(jnp.int32, (HEADS * VDIM, LANE), 1), 1.0, 0.0).astype(F32)
        return (jnp.dot(prod, pick, precision=lax.Precision.HIGHEST, preferred_element_type=F32),)

    (delta,) = _rowwise("attn_delta", fn, [("row", do, HEADS * VDIM, 0), ("row", o32, HEADS * VDIM, 0)],
                        [("row", LANE, F32)], rows, tm)
    return delta


def _attn_bwd(q_att, k_att, v, do, lse_row, delta_row, rows):
    t = _tile(rows, 640, LANE)
    nb = rows // t

    def body(qt_ref, kt_ref, q_ref, k_ref, v_ref, do_ref, lse_ref, delta_ref, dq_ref, dk_ref, dv_ref, dk_sc, dv_sc):
        qi, kj = qt_ref[pl.program_id(1)], kt_ref[pl.program_id(1)]

        @pl.when(pl.program_id(1) == 0)
        def _():
            dq_ref[...] = jnp.zeros_like(dq_ref)

        @pl.when(qi == kj)
        def _():
            dk_sc[...] = jnp.zeros_like(dk_sc)
            dv_sc[...] = jnp.zeros_like(dv_sc)

        def step(masked):
            k, vv = k_ref[...], v_ref[...]
            parts = _two_parts(t)
            st_all = [lax.dot_general(k, q_ref[lo:hi, :], _NT, preferred_element_type=F32) for lo, hi in parts]
            dpt_all = [lax.dot_general(vv, do_ref[lo:hi, :], _NT, preferred_element_type=F32) for lo, hi in parts]
            dk, dv = dk_sc[...], dv_sc[...]
            for (lo, hi), st, dpt in zip(parts, st_all, dpt_all):
                pt = jnp.exp2(st - lse_ref[0, :, lo:hi])
                if masked:
                    ks = kj * t + lax.broadcasted_iota(jnp.int32, (t, hi - lo), 0)
                    qs = qi * t + lo + lax.broadcasted_iota(jnp.int32, (t, hi - lo), 1)
                    pt = jnp.where((ks <= qs) & ((ks >= PAD_LEN) | (ks == qs)), pt, 0.0)
                dv = dv + jnp.dot(pt.astype(BF16), do_ref[lo:hi, :], preferred_element_type=F32)
                dst = (pt * (dpt - delta_ref[0, :, lo:hi])).astype(BF16)
                dk = dk + jnp.dot(dst, q_ref[lo:hi, :], preferred_element_type=F32)
                q_rows = pl.ds(pl.multiple_of(qi * t + lo, LANE), hi - lo)
                dq_ref[q_rows, :] += lax.dot_general(dst, k, _TN, preferred_element_type=F32)
            dk_sc[...], dv_sc[...] = dk, dv

        pl.when((qi == kj) | (kj == 0))(functools.partial(step, True))
        pl.when((qi > kj) & (kj > 0))(functools.partial(step, False))

        @pl.when(qi == nb - 1)
        def _():
            dk_ref[...] = dk_sc[...] * (1.0 / LOG2E)
            dv_ref[...] = dv_sc[...]

    qt, kt = _causal_pairs(nb, by_key=True)
    qmap = lambda h, p, qt_ref, kt_ref: (qt_ref[p], h)
    kmap = lambda h, p, qt_ref, kt_ref: (kt_ref[p], h)
    stat = pl.BlockSpec((1, 1, t), lambda h, p, qt_ref, kt_ref: (h, 0, qt_ref[p]))
    return pl.pallas_call(
        body,
        name="attn_bwd",
        out_shape=[jax.ShapeDtypeStruct((rows, HEADS * QHEAD_W), F32), jax.ShapeDtypeStruct((rows, HEADS * QHEAD_W), F32),
                   jax.ShapeDtypeStruct((rows, HEADS * VDIM), F32)],
        grid_spec=pltpu.PrefetchScalarGridSpec(
            num_scalar_prefetch=2,
            grid=(HEADS, len(qt)),
            in_specs=[pl.BlockSpec((t, QHEAD_W), qmap), pl.BlockSpec((t, QHEAD_W), kmap), pl.BlockSpec((t, VDIM), kmap),
                      pl.BlockSpec((t, VDIM), qmap), stat, stat],
            out_specs=[pl.BlockSpec((rows, QHEAD_W), lambda h, p, qt_ref, kt_ref: (0, h)),
                       pl.BlockSpec((t, QHEAD_W), kmap), pl.BlockSpec((t, VDIM), kmap)],
            scratch_shapes=[pltpu.VMEM((t, QHEAD_W), F32), pltpu.VMEM((t, VDIM), F32)]),
        compiler_params=_params(("parallel", "arbitrary")),
    )(qt, kt, q_att, k_att, v, do, lse_row, delta_row)


C = BLOCK


def _hgrn_prep(hq, hf, hi, lb, c):
    rows = c * C + lax.broadcasted_iota(jnp.int32, (C, C), 0)
    valid = rows >= PAD_LEN
    sg = _sigmoid(hf)
    f = lb + (1.0 - lb) * sg
    g = jnp.where(valid, jnp.log(f), 0.0)
    k = jnp.where(valid, 1.0 - f, 0.0)
    q = _silu(hq)
    r = lax.broadcasted_iota(jnp.int32, (C, C), 0)
    cc = lax.broadcasted_iota(jnp.int32, (C, C), 1)
    tri = jnp.where(cc <= r, 1.0, 0.0).astype(F32)
    b = jnp.dot(tri, g, precision=lax.Precision.HIGHEST, preferred_element_type=F32)
    return q, k, hi, b, f, sg, valid


def _last_row_as_col(b_t):
    lane = lax.broadcasted_iota(jnp.int32, b_t.shape, 1)
    return jnp.sum(jnp.where(lane == C - 1, b_t, 0.0), axis=1, keepdims=True)


def _k_scaled(k, b, bs):
    return (k * jnp.exp(jnp.minimum(bs - b, 0.0))).astype(BF16)


def _hgrn_fwd(proj, lb, rows):
    nc = rows // C

    def body(hq_ref, hf_ref, hi_ref, lb_ref, o_ref, a_ref, s_ref, s_sc, b_sc):
        c = pl.program_id(1)

        @pl.when(c == 0)
        def _():
            s_sc[...] = jnp.zeros_like(s_sc)

        q, k, v, b, _, _, _ = _hgrn_prep(hq_ref[...], hf_ref[...], hi_ref[...], lb_ref[...], c)
        b_sc[...] = b
        s0 = s_sc[...]
        s_ref[0, 0] = s0
        v_bf = v.astype(BF16)
        r16 = lax.broadcasted_iota(jnp.int32, (SUB, C), 0)
        c16 = lax.broadcasted_iota(jnp.int32, (SUB, C), 1)
        slabs = [jnp.zeros((SUB, C), F32)]
        for i in range(1, C // SUB):
            bs = b_sc[SUB * i - 1:SUB * i, :]
            qs = (q[SUB * i:SUB * (i + 1)] * jnp.exp(b[SUB * i:SUB * (i + 1)] - bs)).astype(BF16)
            a_i = lax.dot_general(qs, _k_scaled(k, b, bs), _NT, preferred_element_type=F32)
            slabs.append(jnp.where(c16 <= r16 + (SUB * i - SUB), a_i, 0.0))
        a_off = jnp.concatenate(slabs, axis=0)
        q_t, k_t, b_t = q.T, k.T, b.T
        sub = lax.broadcasted_iota(jnp.int32, (C, C), 0)
        lane = lax.broadcasted_iota(jnp.int32, (C, C), 1)
        lane1 = lax.broadcasted_iota(jnp.int32, (1, C), 1)
        at_band = jnp.zeros((C, C), F32)
        ahead = lane - sub
        for dl in range(SUB):
            k_s = pltpu.roll(k_t, dl, 1) if dl else k_t
            b_s = pltpu.roll(b_t, dl, 1) if dl else b_t
            e = jnp.exp(b_t - b_s)
            band = jnp.sum(q_t * k_s * e, axis=0, keepdims=True)
            band = jnp.where(lane1 >= dl, band, 0.0)
            at_band = at_band + jnp.where(ahead == dl, jnp.broadcast_to(band, (C, C)), 0.0)
        a = (a_off + at_band.T).astype(BF16)
        a_ref[0] = a
        qe = (q * jnp.exp(b)).astype(BF16)
        o_ref[...] = (jnp.dot(a, v_bf, preferred_element_type=F32)
                      + jnp.dot(qe, s0.astype(BF16), preferred_element_type=F32))
        b_last = b_sc[C - 1:C, :]
        kd = (k * jnp.exp(b_last - b)).astype(BF16)
        s_sc[...] = (jnp.exp(_last_row_as_col(b_t)) * s0
                     + lax.dot_general(kd, v_bf, _TN, preferred_element_type=F32))

    seg = lambda base: (lambda h, c: (c, base // C + h))
    return pl.pallas_call(
        body,
        name="hgrn_fwd",
        out_shape=[jax.ShapeDtypeStruct((rows, D_MODEL), F32), jax.ShapeDtypeStruct((HEADS, rows, C), BF16),
                   jax.ShapeDtypeStruct((HEADS, nc, C, C), F32)],
        grid=(HEADS, nc),
        in_specs=[pl.BlockSpec((C, C), seg(SEG_HQ)), pl.BlockSpec((C, C), seg(SEG_HF)), pl.BlockSpec((C, C), seg(SEG_HI)),
                  pl.BlockSpec((1, C), lambda h, c: (0, h))],
        out_specs=[pl.BlockSpec((C, C), lambda h, c: (c, h)), pl.BlockSpec((1, C, C), lambda h, c: (h, c, 0)),
                   pl.BlockSpec((1, 1, C, C), lambda h, c: (h, c, 0, 0))],
        scratch_shapes=[pltpu.VMEM((C, C), F32), pltpu.VMEM((C, C), F32)],
        compiler_params=_params(("parallel", "arbitrary")),
    )(proj, proj, proj, lb)


def _hgrn_bwd(proj, lb, a_mat, s_states, do_h, rows):
    nc = rows // C

    def body(hq_ref, hf_ref, hi_ref, lb_ref, a_ref, s_ref, do_ref, dhq_ref, dhf_ref, dhi_ref, dlb_ref, ds_sc, b_sc):
        step = pl.program_id(1)
        c = nc - 1 - step

        @pl.when(step == 0)
        def _():
            ds_sc[...] = jnp.zeros_like(ds_sc)
            dlb_ref[...] = jnp.zeros_like(dlb_ref)

        hq, hf = hq_ref[...], hf_ref[...]
        lb_row = lb_ref[...]
        q, k, v, b, f, sg, valid = _hgrn_prep(hq, hf, hi_ref[...], lb_row, c)
        b_sc[...] = b
        s0 = s_ref[0, 0]
        ds1 = ds_sc[...]
        s0_bf, ds1_bf = s0.astype(BF16), ds1.astype(BF16)
        do = do_ref[...]
        do_bf, v_bf = do.astype(BF16), v.astype(BF16)
        b_last = b_sc[C - 1:C, :]
        e_last = jnp.exp(b_last - b)
        eb = jnp.exp(b)
        sub = lax.broadcasted_iota(jnp.int32, (C, C), 0)
        lane = lax.broadcasted_iota(jnp.int32, (C, C), 1)
        r16 = lax.broadcasted_iota(jnp.int32, (SUB, C), 0)
        c16 = lax.broadcasted_iota(jnp.int32, (SUB, C), 1)

        dv = (lax.dot_general(a_ref[0], do_bf, _TN, preferred_element_type=F32)
              + jnp.dot((k * e_last).astype(BF16), ds1_bf, preferred_element_type=F32))
        da = jnp.where(lane <= sub, lax.dot_general(do_bf, v_bf, _NT, preferred_element_type=F32), 0.0)
        da_t = jnp.where(sub <= lane, lax.dot_general(v_bf, do_bf, _NT, preferred_element_type=F32), 0.0)

        dq_slabs = [jnp.zeros((SUB, C), F32)]
        for i in range(1, C // SUB):
            bs = b_sc[SUB * i - 1:SUB * i, :]
            da_i = jnp.where(c16 <= r16 + (SUB * i - SUB), da[SUB * i:SUB * (i + 1)], 0.0).astype(BF16)
            dq_slabs.append(jnp.exp(b[SUB * i:SUB * (i + 1)] - bs)
                            * jnp.dot(da_i, _k_scaled(k, b, bs), preferred_element_type=F32))
        dk_slabs = []
        for j in range(C // SUB - 1):
            be = b_sc[SUB * j + SUB - 1:SUB * (j + 1), :]
            qe_j = (q * jnp.exp(jnp.minimum(b - be, 0.0))).astype(BF16)
            da_j = jnp.where(c16 >= r16 + (SUB * j + SUB), da_t[SUB * j:SUB * (j + 1)], 0.0).astype(BF16)
            dk_slabs.append(jnp.exp(be - b[SUB * j:SUB * (j + 1)]) * jnp.dot(da_j, qe_j, preferred_element_type=F32))
        dk_slabs.append(jnp.zeros((SUB, C), F32))

        q_t, k_t, b_t = q.T, k.T, b.T
        lane1 = lax.broadcasted_iota(jnp.int32, (1, C), 1)
        dq_t = jnp.zeros((C, C), F32)
        dk_t = jnp.zeros((C, C), F32)
        ahead = lane - sub
        for dl in range(SUB):
            k_s = pltpu.roll(k_t, dl, 1) if dl else k_t
            b_s = pltpu.roll(b_t, dl, 1) if dl else b_t
            e = jnp.exp(jnp.minimum(b_t - b_s, 0.0))
            dband = jnp.sum(jnp.where(ahead == dl, da_t, 0.0), axis=0, keepdims=True)
            w = jnp.where(lane1 >= dl, dband, 0.0) * e
            dq_t = dq_t + w * k_s
            back = w * q_t
            dk_t = dk_t + (pltpu.roll(back, C - dl, 1) if dl else back)

        dq = eb * lax.dot_general(do_bf, s0_bf, _NT, preferred_element_type=F32) + jnp.concatenate(dq_slabs, axis=0) + dq_t.T
        dk_inter = e_last * lax.dot_general(v_bf, ds1_bf, _NT, preferred_element_type=F32)
        dk = dk_inter + jnp.concatenate(dk_slabs, axis=0) + dk_t.T

        extra = (jnp.exp(b_last) * jnp.sum((s0 * ds1).T, axis=0, keepdims=True)
                 + jnp.sum(k * dk_inter, axis=0, keepdims=True))
        db = q * dq - k * dk + jnp.where(sub == C - 1, jnp.broadcast_to(extra, (C, C)), 0.0)
        tri_t = jnp.where(lane >= sub, 1.0, 0.0).astype(F32)
        dg = jnp.dot(tri_t, db, precision=lax.Precision.HIGHEST, preferred_element_type=F32)
        ds_sc[...] = (jnp.exp(_last_row_as_col(b_t)) * ds1
                      + lax.dot_general((q * eb).astype(BF16), do_bf, _TN, preferred_element_type=F32))

        df = jnp.where(valid, dg / f - dk, 0.0)
        dhf_ref[...] = (df * (1.0 - lb_row) * sg * (1.0 - sg)).astype(BF16)
        dlb_ref[...] += jnp.sum(df * (1.0 - sg), axis=0, keepdims=True)
        dhq_ref[...] = (dq * _dsilu(hq)).astype(BF16)
        dhi_ref[...] = dv.astype(BF16)

    seg = lambda base: (lambda h, s: (nc - 1 - s, base // C + h))
    rmap = lambda h, s: (nc - 1 - s, h)
    return pl.pallas_call(
        body,
        name="hgrn_bwd",
        out_shape=[jax.ShapeDtypeStruct((rows, D_MODEL), BF16)] * 3 + [jax.ShapeDtypeStruct((1, D_MODEL), F32)],
        grid=(HEADS, nc),
        in_specs=[pl.BlockSpec((C, C), seg(SEG_HQ)), pl.BlockSpec((C, C), seg(SEG_HF)), pl.BlockSpec((C, C), seg(SEG_HI)),
                  pl.BlockSpec((1, C), lambda h, s: (0, h)),
                  pl.BlockSpec((1, C, C), lambda h, s: (h, nc - 1 - s, 0)),
                  pl.BlockSpec((1, 1, C, C), lambda h, s: (h, nc - 1 - s, 0, 0)),
                  pl.BlockSpec((C, C), rmap)],
        out_specs=[pl.BlockSpec((C, C), rmap)] * 3 + [pl.BlockSpec((1, C), lambda h, s: (0, h))],
        scratch_shapes=[pltpu.VMEM((C, C), F32), pltpu.VMEM((C, C), F32)],
        compiler_params=_params(("parallel", "arbitrary")),
    )(proj, proj, proj, lb, a_mat, s_states, do_h)


CONV_TC = 512
HALO = 16


def _halo_row(block, k):
    r = lax.broadcasted_iota(jnp.int32, block.shape, 0)
    return jnp.sum(jnp.where(r == k, block, 0.0), axis=0, keepdims=True)


def _conv_taps(i, tm, g_ref, pg_ref):
    shape = g_ref.shape
    r = lax.broadcasted_iota(jnp.int32, shape, 0)
    g = jnp.where(i * tm + r >= PAD_LEN, g_ref[...].astype(F32), 0.0)
    prev = pg_ref[...].astype(F32)
    p1 = jnp.where(i * tm - 1 >= PAD_LEN, _halo_row(prev, HALO - 1), 0.0)
    p2 = jnp.where(i * tm - 2 >= PAD_LEN, _halo_row(prev, HALO - 2), 0.0)
    s1 = jnp.where(r == 0, p1, pltpu.roll(g, 1, 0))
    s2 = jnp.where(r == 0, p2, jnp.where(r == 1, p1, pltpu.roll(g, 2, 0)))
    return g, s1, s2


def _conv_specs(tm, tc, ncb, order):
    gate = pl.BlockSpec((tm, tc), lambda *ids: order(ids))
    halo = pl.BlockSpec((HALO, tc), lambda *ids: (jnp.maximum(order(ids)[0] * (tm // HALO) - 1, 0), order(ids)[1]))
    up = pl.BlockSpec((tm, tc), lambda *ids: (order(ids)[0], ncb + order(ids)[1]))
    return gate, halo, up


def _conv_fwd(ffn, conv_w, conv_b, rows, tm):
    tc = CONV_TC
    ncb = D_FF // tc

    def body(g_ref, pg_ref, up_ref, cw_ref, cb_ref, act_ref):
        i = pl.program_id(0)
        g, s1, s2 = _conv_taps(i, tm, g_ref, pg_ref)
        conv = (cw_ref[0:1, :] * s2 + cw_ref[1:2, :] * s1 + cw_ref[2:3, :] * g) + cb_ref[...]
        act_ref[...] = (_silu(conv) * up_ref[...].astype(F32)).astype(BF16)

    gate, halo, up = _conv_specs(tm, tc, ncb, lambda ids: (ids[0], ids[1]))
    return pl.pallas_call(
        body,
        name="conv_fwd",
        out_shape=jax.ShapeDtypeStruct((rows, D_FF), BF16),
        grid=(rows // tm, ncb),
        in_specs=[gate, halo, up, pl.BlockSpec((3, tc), lambda i, j: (0, j)), pl.BlockSpec((1, tc), lambda i, j: (0, j))],
        out_specs=pl.BlockSpec((tm, tc), lambda i, j: (i, j)),
        compiler_params=_params(("parallel", "parallel")),
    )(ffn, ffn, ffn, conv_w, conv_b)


def _conv_bwd_a(ffn, dact, conv_w, conv_b, rows, tm):
    tc = CONV_TC
    ncb = D_FF // tc

    def body(g_ref, pg_ref, up_ref, da_ref, cw_ref, cb_ref, dc_ref, dffn_ref, w0_ref, w1_ref, w2_ref, db_ref):
        i = pl.program_id(1)
        g, s1, s2 = _conv_taps(i, tm, g_ref, pg_ref)
        conv = (cw_ref[0:1, :] * s2 + cw_ref[1:2, :] * s1 + cw_ref[2:3, :] * g) + cb_ref[...]
        da = da_ref[...].astype(F32)
        dffn_ref[...] = (da * _silu(conv)).astype(BF16)
        dc = da * up_ref[...].astype(F32) * _dsilu(conv)
        dc_ref[...] = dc.astype(BF16)
        sums = [jnp.sum(dc * s2, axis=0, keepdims=True), jnp.sum(dc * s1, axis=0, keepdims=True),
                jnp.sum(dc * g, axis=0, keepdims=True), jnp.sum(dc, axis=0, keepdims=True)]
        for ref, val in zip((w0_ref, w1_ref, w2_ref, db_ref), sums):
            @pl.when(i == 0)
            def _(ref=ref, val=val):
                ref[...] = val

            @pl.when(i > 0)
            def _(ref=ref, val=val):
                ref[...] += val

    gate, halo, up = _conv_specs(tm, tc, ncb, lambda ids: (ids[1], ids[0]))
    col = pl.BlockSpec((1, tc), lambda j, i: (0, j))
    return pl.pallas_call(
        body,
        name="conv_bwd_a",
        out_shape=[jax.ShapeDtypeStruct((rows, D_FF), BF16), jax.ShapeDtypeStruct((rows, 2 * D_FF), BF16)]
        + [jax.ShapeDtypeStruct((1, D_FF), F32)] * 4,
        grid=(ncb, rows // tm),
        in_specs=[gate, halo, up, pl.BlockSpec((tm, tc), lambda j, i: (i, j)),
                  pl.BlockSpec((3, tc), lambda j, i: (0, j)), col],
        out_specs=[pl.BlockSpec((tm, tc), lambda j, i: (i, j)), pl.BlockSpec((tm, tc), lambda j, i: (i, ncb + j)),
                   col, col, col, col],
        compiler_params=_params(("parallel", "arbitrary")),
    )(ffn, ffn, ffn, dact, conv_w, conv_b)


def _conv_bwd_b(dconv, conv_w, dffn, rows, tm):
    tc = CONV_TC
    ncb = D_FF // tc
    nrb = rows // tm

    def body(dc_ref, nx_ref, cw_ref, dffn_in, out_ref):
        del dffn_in
        i = pl.program_id(0)
        dc = dc_ref[...].astype(F32)
        r = lax.broadcasted_iota(jnp.int32, dc.shape, 0)
        last = i == nrb - 1
        nxt = nx_ref[...].astype(F32)
        x1 = jnp.where(last, 0.0, _halo_row(nxt, 0))
        x2 = jnp.where(last, 0.0, _halo_row(nxt, 1))
        n1 = jnp.where(r == tm - 1, x1, pltpu.roll(dc, tm - 1, 0))
        n2 = jnp.where(r == tm - 1, x2, jnp.where(r == tm - 2, x1, pltpu.roll(dc, tm - 2, 0)))
        dg = cw_ref[2:3, :] * dc + cw_ref[1:2, :] * n1 + cw_ref[0:1, :] * n2
        out_ref[...] = jnp.where(i * tm + r >= PAD_LEN, dg, 0.0).astype(BF16)

    return pl.pallas_call(
        body,
        name="conv_bwd_b",
        out_shape=jax.ShapeDtypeStruct((rows, 2 * D_FF), BF16),
        grid=(nrb, ncb),
        in_specs=[pl.BlockSpec((tm, tc), lambda i, j: (i, j)),
                  pl.BlockSpec((HALO, tc), lambda i, j: (jnp.minimum((i + 1) * (tm // HALO), rows // HALO - 1), j)),
                  pl.BlockSpec((3, tc), lambda i, j: (0, j)),
                  pl.BlockSpec(memory_space=pl.ANY)],
        out_specs=pl.BlockSpec((tm, tc), lambda i, j: (i, j)),
        input_output_aliases={3: 0},
        compiler_params=_params(("parallel", "parallel")),
    )(dconv, dconv, conv_w, dffn)


def _final_call(h1, y, target, g_final, rows):
    tm = BLOCK

    def fn(i, tm_, h1v, yv, tgt, g):
        h2 = h1v + yv
        out = _rms_fwd(h2, g)
        err = jnp.where(i > 0, out - tgt, 0.0)
        loss = 0.5 * jnp.sum(jnp.mean(err * err, axis=-1, keepdims=True), axis=0, keepdims=True)
        dx, dg = _rms_bwd(h2, g, err * (1.0 / D_MODEL))
        return dx, dx, jnp.broadcast_to(loss, (1, LANE)), dg

    n_in = 4
    in_specs = [pl.BlockSpec((tm, D_MODEL), lambda i: (i, 0)), pl.BlockSpec((tm, D_MODEL), lambda i: (i, 0)),
                pl.BlockSpec((tm, D_MODEL), lambda i: (jnp.maximum(i - 1, 0), 0)),
                pl.BlockSpec((1, D_MODEL), lambda i: (0, 0))]

    def body(*refs):
        i = pl.program_id(0)
        dx, dx2, loss, dg = fn(i, tm, *[r[...] for r in refs[:n_in]])
        refs[4][...] = dx
        refs[5][...] = dx2.astype(BF16)
        for ref, val in ((refs[6], loss), (refs[7], dg)):
            @pl.when(i == 0)
            def _(ref=ref, val=val):
                ref[...] = val

            @pl.when(i > 0)
            def _(ref=ref, val=val):
                ref[...] += val

    return pl.pallas_call(
        body,
        name="final_loss",
        out_shape=[jax.ShapeDtypeStruct((rows, D_MODEL), F32), jax.ShapeDtypeStruct((rows, D_MODEL), BF16),
                   jax.ShapeDtypeStruct((1, LANE), F32), jax.ShapeDtypeStruct((1, D_MODEL), F32)],
        grid=(rows // tm,),
        in_specs=in_specs,
        out_specs=[pl.BlockSpec((tm, D_MODEL), lambda i: (i, 0)), pl.BlockSpec((tm, D_MODEL), lambda i: (i, 0)),
                   pl.BlockSpec((1, LANE), lambda i: (0, 0)), pl.BlockSpec((1, D_MODEL), lambda i: (0, 0))],
        compiler_params=_params(("arbitrary",)),
    )(h1, y, target, g_final)


def _heads_map(fn, *slabs):
    outs = [fn(*[s[:, h * LANE:(h + 1) * LANE] for s in slabs]) for h in range(HEADS)]
    if isinstance(outs[0], tuple):
        return tuple(jnp.concatenate([o[k] for o in outs], axis=1) for k in range(len(outs[0])))
    return jnp.concatenate(outs, axis=1)


def _local_step(x, positions, target, w, p, emit=None):
    kept = {}
    if emit is None:
        def emit(group):
            kept.update(group)
            return None
    s_len = x.shape[0]
    rows = s_len + BLOCK
    tm = _tile(rows, 640, 8)
    row = lambda arr, width, cb=0: ("row", arr, width, cb)

    h0 = jnp.concatenate([jnp.zeros((PAD_LEN, D_MODEL), F32), w["meta_tokens"], x], axis=0)
    pos = jnp.concatenate([jnp.zeros((PAD_LEN,), jnp.int32), jnp.arange(N_META, dtype=jnp.int32),
                           positions.astype(jnp.int32) + N_META])
    inv = 1.0 / (ROPE_THETA ** (jnp.arange(0, ROPE, 2, dtype=F32) / ROPE))
    ang = pos.astype(F32)[:, None] * inv
    zero = jnp.zeros((rows, LANE - ROPE), F32)
    cos_t = jnp.concatenate([jnp.cos(ang), jnp.cos(ang), zero], axis=1)
    sin_t = jnp.concatenate([-jnp.sin(ang), jnp.sin(ang), zero], axis=1)
    lb_r0, lb_r1 = p["lb_raw"][0:1], p["lb_raw"][1:2]

    def lb_fn(i, tm_, r0, r1):
        m = jnp.maximum(r0, r1)
        e0, e1 = jnp.exp(r0 - m), jnp.exp(r1 - m)
        return (e0 / (e0 + e1),)

    (lb,) = _rowwise("lb_fwd", lb_fn, [("bc", lb_r0), ("bc", lb_r1)], [("acc", (1, D_MODEL))], 1, 1)

    (u1,) = _rowwise("mix_norm", lambda i, t, h, g: (_rms_fwd(h, g),),
                     [row(h0, D_MODEL), ("bc", p["g_mix_norm"])], [("row", D_MODEL, BF16)], rows, tm)
    proj = _matmul(u1, w["w_in"], "nn", F32, "mm_proj")
    hint = getattr(w, "hint", lambda name, after: None)
    hint("w_q_up", proj)
    qn, kvn = _rowwise(
        "latent_norm", lambda i, t, ql, kl, gq, gk: (_rms_fwd(ql, gq), _rms_fwd(kl, gk)),
        [row(proj, Q_LORA, 0), row(proj, KV_LORA, SEG_KV_LAT // KV_LORA), ("bc", p["g_q_norm"]), ("bc", p["g_kv_norm"])],
        [("row", Q_LORA, BF16), ("row", KV_LORA, BF16)], rows, tm)
    q_raw = _matmul(qn, w["w_q_up"], "nn", F32, "mm_q_up")
    kv = _matmul(kvn, w["w_kv_up"], "nn", F32, "mm_kv_up")
    q_att, k_att, v_att = _rope_fwd_call(q_raw, kv, proj, cos_t, sin_t, rows, tm)
    o32, o_bf, lse = _attn_fwd(q_att, k_att, v_att, rows)
    hint("w_branch_mla", lse)
    o_h, a_mat, s_states = _hgrn_fwd(proj, lb, rows)

    def hgrn_post(i, t, oh, hg, g):
        return (_heads_map(lambda a, b: _rms_fwd(a, g) * _silu(b), oh, hg),)

    (o_hgrn,) = _rowwise("hgrn_post", hgrn_post,
                         [row(o_h, D_MODEL), row(proj, D_MODEL, SEG_HG // D_MODEL), ("bc", p["g_hgrn_norm"])],
                         [("row", D_MODEL, BF16)], rows, tm)
    br_a = _matmul(o_bf, w["w_branch_mla"], "nn", F32, "mm_branch_mla")
    br_b = _matmul(o_hgrn, w["w_branch_hgrn"], "nn", F32, "mm_branch_hgrn")
    (merged,) = _rowwise(
        "merge", lambda i, t, a, b, ga, gb: (_sigmoid(ga) * a + _sigmoid(gb) * b,),
        [row(br_a, D_MODEL), row(br_b, D_MODEL), row(proj, D_MODEL, SEG_GA // D_MODEL), row(proj, D_MODEL, SEG_GB // D_MODEL)],
        [("row", D_MODEL, BF16)], rows, tm)
    mix_out = _matmul(merged, w["w_out"], "nn", F32, "mm_out")

    def ffn_norm(i, t, h, mo, g):
        h1v = h + mo
        return h1v, _rms_fwd(h1v, g)

    h1, u2 = _rowwise("ffn_norm", ffn_norm, [row(h0, D_MODEL), row(mix_out, D_MODEL), ("bc", p["g_ffn_norm"])],
                      [("row", D_MODEL, F32), ("row", D_MODEL, BF16)], rows, tm)
    ffn = _matmul(u2, w["w_ffn_in"], "nn", BF16, "mm_ffn_in")
    act = _conv_fwd(ffn, w["conv_w"], p["conv_b"], rows, tm)
    y = _matmul(act, w["w_ffn_out"], "nn", F32, "mm_ffn_out")
    dh2, dh2_bf, loss_acc, dg_final = _final_call(h1, y, target, p["g_final_norm"].reshape(1, D_MODEL), rows)

    grads = {"g_final_norm": dg_final.reshape(D_MODEL)}
    dact = _matmul(dh2_bf, w["w_ffn_out"], "nt", BF16, "mm_d_act")
    grads["w_ffn_out"] = _matmul(act, dh2_bf, "tn", BF16, "mm_dw_ffn_out")
    dconv, dffn, dcw0, dcw1, dcw2, dcb = _conv_bwd_a(ffn, dact, w["conv_w"], p["conv_b"], rows, tm)
    dffn = _conv_bwd_b(dconv, w["conv_w"], dffn, rows, tm)
    grads["conv_w"] = jnp.concatenate([dcw0, dcw1, dcw2], axis=0)
    grads["conv_b"] = dcb
    du2 = _matmul(dffn, w["w_ffn_in"], "nt", F32, "mm_d_u2")
    grads["w_ffn_in"] = _matmul(u2, dffn, "tn", BF16, "mm_dw_ffn_in", col_blocks=N_DEV)

    def ffn_norm_bwd(i, t, h, du, dh, g):
        dx, dg = _rms_bwd(h, g, du)
        dh1v = dh + dx
        return dh1v, dh1v, dg

    dh1, dh1_bf, grads["g_ffn_norm"] = _rowwise(
        "ffn_norm_bwd", ffn_norm_bwd, [row(h1, D_MODEL), row(du2, D_MODEL), row(dh2, D_MODEL), ("bc", p["g_ffn_norm"])],
        [("row", D_MODEL, F32), ("row", D_MODEL, BF16), ("acc", (1, D_MODEL))], rows, tm)
    tok = emit({n: grads.pop(n) for n in ("w_ffn_out", "w_ffn_in", "conv_w", "conv_b", "g_final_norm", "g_ffn_norm")})
    dmerged = _matmul(dh1_bf, w["w_out"], "nt", F32, "mm_d_merged", after=tok)
    grads["w_out"] = _matmul(merged, dh1_bf, "tn", BF16, "mm_dw_out")

    def merge_bwd(i, t, dm, a, b, ga, gb):
        sa, sb = _sigmoid(ga), _sigmoid(gb)
        return dm * sa, dm * sb, dm * a * sa * (1.0 - sa), dm * b * sb * (1.0 - sb)

    da_bf, db_bf, dga, dgb = _rowwise(
        "merge_bwd", merge_bwd,
        [row(dmerged, D_MODEL), row(br_a, D_MODEL), row(br_b, D_MODEL),
         row(proj, D_MODEL, SEG_GA // D_MODEL), row(proj, D_MODEL, SEG_GB // D_MODEL)],
        [("row", D_MODEL, BF16)] * 4, rows, tm)
    do_mla = _matmul(da_bf, w["w_branch_mla"], "nt", BF16, "mm_d_o_mla")
    grads["w_branch_mla"] = _matmul(o_bf, da_bf, "tn", BF16, "mm_dw_branch_mla")
    do_hgrn = _matmul(db_bf, w["w_branch_hgrn"], "nt", F32, "mm_d_o_hgrn")
    grads["w_branch_hgrn"] = _matmul(o_hgrn, db_bf, "tn", BF16, "mm_dw_branch_hgrn")

    def hgrn_post_bwd(i, t, dy, oh, hg, g):
        def one(dyh, ohh, hgh):
            dx, dg = _rms_bwd(ohh, g, dyh * _silu(hgh))
            return dx, dyh * _rms_fwd(ohh, g) * _dsilu(hgh), dg

        dx, dhg, dg = _heads_map(one, dy, oh, hg)
        dg_sum = dg[:, 0:LANE]
        for h in range(1, HEADS):
            dg_sum = dg_sum + dg[:, h * LANE:(h + 1) * LANE]
        return dx, dhg, dg_sum

    tok = emit({n: grads.pop(n) for n in ("w_out", "w_branch_mla", "w_branch_hgrn")})
    do_h, dhg, grads["g_hgrn_norm"] = _rowwise(
        "hgrn_post_bwd", hgrn_post_bwd,
        [row(do_hgrn, D_MODEL), row(o_h, D_MODEL), row(proj, D_MODEL, SEG_HG // D_MODEL), ("bc", p["g_hgrn_norm"])],
        [("row", D_MODEL, F32), ("row", D_MODEL, BF16), ("acc", (1, LANE))], rows, tm, after=tok)
    dhq, dhf, dhi, dlb = _hgrn_bwd(proj, lb, a_mat, s_states, do_h, rows)

    def lb_bwd(i, tm_, d, l):
        t = d * l * (1.0 - l)
        return t, -t

    dlb0, dlb1 = _rowwise("lb_bwd", lb_bwd, [("bc", dlb), ("bc", lb)], [("acc", (1, D_MODEL))] * 2, 1, 1)
    grads["lb_raw"] = jnp.concatenate([dlb0, dlb1], axis=0)

    delta = _attn_delta(do_mla, o32, rows, tm)
    dq_att, dk_att, dv_att = _attn_bwd(q_att, k_att, v_att, do_mla, lse,
                                       jnp.transpose(delta[:, :HEADS]).reshape(HEADS, 1, rows), rows)
    dq_full, dkv, dkr = _rope_bwd_call(dq_att, dk_att, dv_att, cos_t, sin_t, rows, tm)
    dqn = _matmul(dq_full, w["w_q_up"], "nt", F32, "mm_d_qn")
    grads["w_q_up"] = _matmul(qn, dq_full, "tn", BF16, "mm_dw_q_up")
    dkvn = _matmul(dkv, w["w_kv_up"], "nt", F32, "mm_d_kvn")
    grads["w_kv_up"] = _matmul(kvn, dkv, "tn", BF16, "mm_dw_kv_up")

    def latent_norm_bwd(i, t, ql, kl, dq, dk, gq, gk):
        dql, dgq = _rms_bwd(ql, gq, dq)
        dkl, dgk = _rms_bwd(kl, gk, dk)
        return dql, dkl, dgq, dgk

    dq_lat, dkv_lat, grads["g_q_norm"], grads["g_kv_norm"] = _rowwise(
        "latent_norm_bwd", latent_norm_bwd,
        [row(proj, Q_LORA, 0), row(proj, KV_LORA, SEG_KV_LAT // KV_LORA), row(dqn, Q_LORA), row(dkvn, KV_LORA),
         ("bc", p["g_q_norm"]), ("bc", p["g_kv_norm"])],
        [("row", Q_LORA, BF16), ("row", KV_LORA, BF16), ("acc", (1, Q_LORA)), ("acc", (1, KV_LORA))], rows, tm)
    dproj = jnp.concatenate([dq_lat, dkv_lat, dhq, dhf, dhi, dhg, dga, dgb, dkr], axis=1)
    grads["w_in"] = _matmul(u1, dproj, "tn", BF16, "mm_dw_in")
    tok = emit({n: grads.pop(n) for n in ("w_in", "w_q_up", "w_kv_up", "lb_raw", "g_q_norm", "g_kv_norm", "g_hgrn_norm")})
    du1 = _matmul(dproj, w["w_in"], "nt", F32, "mm_d_u1", after=tok)

    def mix_norm_bwd(i, t, h, du, dh, g):
        dx, dg = _rms_bwd(h, g, du)
        return dh + dx, dh + dx, dg

    grad_x, d_prefix, grads["g_mix_norm"] = _rowwise(
        "mix_norm_bwd", mix_norm_bwd, [row(h0, D_MODEL), row(du1, D_MODEL), row(dh1, D_MODEL), ("bc", p["g_mix_norm"])],
        [("tail", D_MODEL, F32), ("head", D_MODEL, F32), ("acc", (1, D_MODEL))], rows, BLOCK)
    grads["meta_tokens"] = d_prefix[PAD_LEN:BLOCK]
    kept.update(grads)
    return loss_acc[0, 0], grad_x, kept


K_ROPE_AT = Q_LORA + KV_LORA
COL_SHARDED = ("w_in", "w_q_up", "w_kv_up", "w_ffn_in", "conv_w", "meta_tokens")
ROW_SHARDED = ("w_branch_mla", "w_branch_hgrn", "w_out", "w_ffn_out")
BIG = ("w_in", "w_q_up", "w_kv_up", "w_branch_mla", "w_branch_hgrn", "w_out", "w_ffn_in", "w_ffn_out")
SMALL = ("conv_b", "g_mix_norm", "g_q_norm", "g_kv_norm", "g_hgrn_norm", "g_ffn_norm", "g_final_norm", "lb_raw")


def _unshard(name, stacked):
    if name in COL_SHARDED:
        return jnp.transpose(stacked, (1, 0, 2)).reshape(stacked.shape[1], N_DEV * stacked.shape[2])
    return stacked.reshape(N_DEV * stacked.shape[1], stacked.shape[2])


def _reshard(name, full):
    if full.ndim == 3:
        return full
    if name in COL_SHARDED:
        r, c = full.shape
        return jnp.transpose(full.reshape(r, N_DEV, c // N_DEV), (1, 0, 2))
    return full.reshape(N_DEV, full.shape[0] // N_DEV, full.shape[1])


def _to_kernel_layout(full):
    out = dict(full)
    if "w_in" in full:
        w_in = full["w_in"]
        pad = jnp.zeros((D_MODEL, KR_W - ROPE), w_in.dtype)
        out["w_in"] = jnp.concatenate(
            [w_in[:, :K_ROPE_AT], w_in[:, K_ROPE_AT + ROPE:], w_in[:, K_ROPE_AT:K_ROPE_AT + ROPE], pad], axis=1)
    if "w_q_up" in full:
        wq = full["w_q_up"].reshape(Q_LORA, HEADS, NOPE + ROPE)
        out["w_q_up"] = jnp.pad(wq, ((0, 0), (0, 0), (0, QHEAD_W - NOPE - ROPE))).reshape(Q_LORA, HEADS * QHEAD_W)
    return out


def _from_kernel_layout(grads):
    out = dict(grads)
    if "w_in" in grads:
        g = grads["w_in"]
        out["w_in"] = jnp.concatenate([g[:, :K_ROPE_AT], g[:, SEG_KR:SEG_KR + ROPE], g[:, K_ROPE_AT:SEG_KR]], axis=1)
    if "w_q_up" in grads:
        g = grads["w_q_up"].reshape(Q_LORA, HEADS, QHEAD_W)
        out["w_q_up"] = g[:, :, :NOPE + ROPE].reshape(Q_LORA, HEADS * (NOPE + ROPE))
    return out


MESH_ID = pl.DeviceIdType.MESH
ANY = pl.BlockSpec(memory_space=pl.ANY)


def _slot(dev):
    return 4 * dev[0] + 2 * dev[1] + dev[2]


def _all_gather(shards):
    n = len(shards)

    def body(*refs):
        ins, outs = refs[:n], refs[n:2 * n]
        send_sems, recv_sems, local_sems = refs[2 * n:]
        x, y, c = lax.axis_index("x"), lax.axis_index("y"), lax.axis_index("c")
        me, sibling = (x, y, c), (x, y, 1 - c)
        chips = [(1 - x, y), (x, 1 - y), (1 - x, 1 - y)]

        def copy(a, k, block, to, src=None):
            dst = outs[a].at[_slot(block)]
            return pltpu.make_async_remote_copy(
                src_ref=dst if src is None else src, dst_ref=dst, send_sem=send_sems.at[a, k],
                recv_sem=recv_sems.at[a, k], device_id=to, device_id_type=MESH_ID)

        mine = [pltpu.make_async_copy(ins[a], outs[a].at[_slot(me)], local_sems.at[a]) for a in range(n)]
        for cp in mine:
            cp.start()
        first = []
        for a in range(n):
            first.append(copy(a, 0, me, sibling, src=ins[a]))
            first += [copy(a, 1 + j, me, (*chip, c), src=ins[a]) for j, chip in enumerate(chips)]
        for cp in first:
            cp.start()
        passed = []
        for a in range(n):
            for j, chip in enumerate(chips):
                copy(a, 1 + j, (*chip, c), me).wait_recv()
                fwd = copy(a, 4 + j, (*chip, c), sibling)
                fwd.start()
                passed.append(fwd)
        for a in range(n):
            copy(a, 0, sibling, me).wait_recv()
            for j, chip in enumerate(chips):
                copy(a, 4 + j, (*chip, 1 - c), me).wait_recv()
        for cp in first + passed:
            cp.wait_send()
        for cp in mine:
            cp.wait()

    return pl.pallas_call(
        body,
        name="gather_weights",
        out_shape=[jax.ShapeDtypeStruct((N_DEV,) + s.shape, s.dtype) for s in shards],
        in_specs=[ANY] * n,
        out_specs=[ANY] * n,
        scratch_shapes=[pltpu.SemaphoreType.DMA((n, 7)), pltpu.SemaphoreType.DMA((n, 7)), pltpu.SemaphoreType.DMA((n,))],
    )(*shards)


def _exchange(blocked, replicated):
    nb, n = len(blocked), len(blocked) + len(replicated)
    arrays = list(blocked) + list(replicated)

    def body(*refs):
        ins, outs = refs[:n], refs[n:2 * n]
        send_sems, recv_sems, local_sems = refs[2 * n:]
        x, y, c = lax.axis_index("x"), lax.axis_index("y"), lax.axis_index("c")
        me = (x, y, c)
        peers = [(x, y, 1 - c), (1 - x, y, c), (x, 1 - y, c), (1 - x, 1 - y, c),
                 (1 - x, y, 1 - c), (x, 1 - y, 1 - c), (1 - x, 1 - y, 1 - c)]

        def src_of(a, dev):
            return ins[a].at[_slot(dev)] if a < nb else ins[a]

        def copy(a, k, frm, to):
            return pltpu.make_async_remote_copy(
                src_ref=src_of(a, to), dst_ref=outs[a].at[_slot(frm)], send_sem=send_sems.at[a, k],
                recv_sem=recv_sems.at[a, k], device_id=to, device_id_type=MESH_ID)

        mine = [pltpu.make_async_copy(src_of(a, me), outs[a].at[_slot(me)], local_sems.at[a]) for a in range(n)]
        for cp in mine:
            cp.start()
        sends = [copy(a, k, me, peer) for a in range(n) for k, peer in enumerate(peers)]
        for cp in sends:
            cp.start()
        for a in range(n):
            for k, peer in enumerate(peers):
                copy(a, k, peer, me).wait_recv()
        for cp in sends:
            cp.wait_send()
        for cp in mine:
            cp.wait()

    return pl.pallas_call(
        body,
        name="exchange_grads",
        out_shape=[jax.ShapeDtypeStruct(s.shape, s.dtype) for s in blocked]
        + [jax.ShapeDtypeStruct((N_DEV,) + s.shape, s.dtype) for s in replicated],
        in_specs=[ANY] * n,
        out_specs=[ANY] * n,
        scratch_shapes=[pltpu.SemaphoreType.DMA((n, 7)), pltpu.SemaphoreType.DMA((n, 7)), pltpu.SemaphoreType.DMA((n,))],
    )(*arrays)


ADAMW_BLOCK_ELEMS = 256 * 1024


def _adamw(name, parts, w, m, v, own=None, me=None):
    r, c = w.shape
    tr = _tile(r, max(16, ADAMW_BLOCK_ELEMS // c), 16)

    def body(*refs):
        if own is None:
            p_ref, w_ref, m_ref, v_ref, g_ref, d_ref, nm_ref, nv_ref = refs
            terms = [p_ref[s].astype(F32) for s in range(N_DEV)]
        else:
            me_ref, p_ref, own_ref, w_ref, m_ref, v_ref, g_ref, d_ref, nm_ref, nv_ref = refs
            mine = own_ref[0].astype(F32)
            terms = [jnp.where(me_ref[0] == s, mine, p_ref[s].astype(F32)) for s in range(N_DEV)]
        g = terms[0]
        for s in range(1, N_DEV):
            g = g + terms[s]
        m_new = ADAM_B1 * m_ref[...] + (1.0 - ADAM_B1) * g
        v_new = ADAM_B2 * v_ref[...] + (1.0 - ADAM_B2) * (g * g)
        m_hat = m_new / (1.0 - ADAM_B1 ** ADAM_STEP)
        v_hat = v_new / (1.0 - ADAM_B2 ** ADAM_STEP)
        g_ref[...] = g
        d_ref[...] = -ADAM_LR * (m_hat / (jnp.sqrt(v_hat) + ADAM_EPS) + ADAM_WD * w_ref[...])
        nm_ref[...] = m_new
        nv_ref[...] = v_new

    if own is None:
        blk = pl.BlockSpec((tr, c), lambda i: (i, 0))
        return pl.pallas_call(
            body,
            name="adamw_" + name,
            out_shape=[jax.ShapeDtypeStruct((r, c), F32)] * 4,
            grid=(r // tr,),
            in_specs=[pl.BlockSpec((N_DEV, tr, c), lambda i: (0, i, 0)), blk, blk, blk],
            out_specs=[blk] * 4,
            compiler_params=_params(("parallel",)),
        )(parts, w, m, v)
    blk = pl.BlockSpec((tr, c), lambda i, me_ref: (i, 0))
    own_at = (lambda i, me_ref: (me_ref[0], i, 0)) if own.shape[0] == N_DEV else (lambda i, me_ref: (0, i, 0))
    return pl.pallas_call(
        body,
        name="adamw_" + name,
        out_shape=[jax.ShapeDtypeStruct((r, c), F32)] * 4,
        grid_spec=pltpu.PrefetchScalarGridSpec(
            num_scalar_prefetch=1,
            grid=(r // tr,),
            in_specs=[pl.BlockSpec((N_DEV, tr, c), lambda i, me_ref: (0, i, 0)), pl.BlockSpec((1, tr, c), own_at),
                      blk, blk, blk],
            out_specs=[blk] * 4),
        compiler_params=_params(("parallel",)),
    )(me, parts, own, w, m, v)


HBM_SPEC = pl.BlockSpec(memory_space=pltpu.HBM)
SEM_SPEC = pl.BlockSpec(memory_space=pltpu.SEMAPHORE)
SIDE_EFFECT = pltpu.SideEffectType.DATAFLOW_SIDE_EFFECTING
N_PEERS = N_DEV - 1


def _peers(x, y, c):
    return [(x, y, 1 - c), (1 - x, y, c), (x, 1 - y, c), (1 - x, 1 - y, c),
            (1 - x, y, 1 - c), (x, 1 - y, 1 - c), (1 - x, 1 - y, 1 - c)]


def _split_copy(srcs, lands, blocked, send_sems, recv_sems, a, k, frm, to):
    src = srcs[a].at[_slot(to)] if blocked[a] else srcs[a]
    return pltpu.make_async_remote_copy(
        src_ref=src, dst_ref=lands[a].at[_slot(frm)], send_sem=send_sems.at[a * N_PEERS + k],
        recv_sem=recv_sems.at[a * N_PEERS + k],
        device_id=to, device_id_type=MESH_ID)


def _exchange_start(name, srcs, lands, blocked, after=()):
    n = len(srcs)
    after = list(after)

    def body(*refs):
        src_refs, land_refs = refs[:n], refs[n:2 * n]
        send_sems, recv_sems = refs[2 * n + len(after)], refs[2 * n + len(after) + 1]
        token = refs[-1]
        x, y, c = lax.axis_index("x"), lax.axis_index("y"), lax.axis_index("c")
        for a in range(n):
            for k, peer in enumerate(_peers(x, y, c)):
                _split_copy(src_refs, land_refs, blocked, send_sems, recv_sems, a, k, (x, y, c), peer).start()
        token[...] = jnp.zeros_like(token)

    thru = [pltpu.HBM(s.shape, s.dtype) for s in list(srcs) + list(lands)]
    res = pl.pallas_call(
        body,
        name=name,
        out_shape=(pltpu.SemaphoreType.DMA((n * N_PEERS,)), pltpu.SemaphoreType.DMA((n * N_PEERS,)), *thru,
                   jax.ShapeDtypeStruct((8, LANE), F32)),
        in_specs=[HBM_SPEC] * (2 * n) + [pl.BlockSpec(memory_space=pl.ANY)] * len(after),
        out_specs=(SEM_SPEC, SEM_SPEC, *([HBM_SPEC] * (2 * n)), pl.BlockSpec(memory_space=pltpu.VMEM)),
        input_output_aliases={i: 2 + i for i in range(2 * n)},
        compiler_params=pltpu.CompilerParams(has_side_effects=SIDE_EFFECT),
    )(*[pltpu.with_memory_space_constraint(s, pltpu.HBM) for s in list(srcs) + list(lands)], *after)
    return res[0], res[1], res[2:2 + n], res[2 + n:2 + 2 * n], res[-1]


def _exchange_wait(name, send_sems, recv_sems, srcs, lands, blocked, after):
    n, n_after = len(srcs), len(after)

    def body(*refs):
        src_refs, land_refs = refs[:n], refs[n:2 * n]
        send, recv = refs[2 * n], refs[2 * n + 1]
        x, y, c = lax.axis_index("x"), lax.axis_index("y"), lax.axis_index("c")
        for a in range(n):
            for k, peer in enumerate(_peers(x, y, c)):
                _split_copy(src_refs, land_refs, blocked, send, recv, a, k, (x, y, c), peer).wait_send()
                _split_copy(src_refs, land_refs, blocked, send, recv, a, k, peer, (x, y, c)).wait_recv()

    res = pl.pallas_call(
        body,
        name=name,
        out_shape=tuple(pltpu.HBM(s.shape, s.dtype) for s in list(srcs) + list(lands)),
        in_specs=[HBM_SPEC] * (2 * n) + [SEM_SPEC, SEM_SPEC] + [pl.BlockSpec(memory_space=pl.ANY)] * n_after,
        out_specs=tuple([HBM_SPEC] * (2 * n)),
        input_output_aliases={i: i for i in range(2 * n)},
        compiler_params=pltpu.CompilerParams(has_side_effects=SIDE_EFFECT),
    )(*srcs, *lands, send_sems, recv_sems, *after)
    return res[:n], res[n:]


class _LazyWeights:
    def __init__(self):
        self.ready, self.groups, self.hints = {}, {}, {}

    def add_group(self, wait_name, names, send, recv, srcs, lands):
        for n in names:
            self.groups[n] = (wait_name, names, send, recv, srcs, lands)

    def hint(self, name, after):
        self.hints[self.groups[name][0]] = after

    def __getitem__(self, name):
        if name not in self.ready:
            wait_name, names, send, recv, srcs, lands = self.groups[name]
            after = [self.hints[wait_name]] if wait_name in self.hints else []
            _, whole = _exchange_wait(wait_name, send, recv, srcs, lands, [False] * len(names), after)
            for n, stacked in zip(names, whole):
                self.ready[n] = _to_kernel_layout({n: _unshard(n, stacked)})[n]
        return self.ready[name]


def kernel(x, positions, meta_tokens, w_in, w_q_up, w_kv_up, w_branch_mla, w_branch_hgrn, w_out, w_ffn_in, w_ffn_out, conv_w, conv_b, g_mix_norm, g_q_norm, g_kv_norm, g_hgrn_norm, g_ffn_norm, g_final_norm, lb_raw, loss_target, m_meta_tokens, m_w_in, m_w_q_up, m_w_kv_up, m_w_branch_mla, m_w_branch_hgrn, m_w_out, m_w_ffn_in, m_w_ffn_out, m_conv_w, m_conv_b, m_g_mix_norm, m_g_q_norm, m_g_kv_norm, m_g_hgrn_norm, m_g_ffn_norm, m_g_final_norm, m_lb_raw, v_meta_tokens, v_w_in, v_w_q_up, v_w_kv_up, v_w_branch_mla, v_w_branch_hgrn, v_w_out, v_w_ffn_in, v_w_ffn_out, v_conv_w, v_conv_b, v_g_mix_norm, v_g_q_norm, v_g_kv_norm, v_g_hgrn_norm, v_g_ffn_norm, v_g_final_norm, v_lb_raw):
    local = dict(zip(
        ("meta_tokens", "w_in", "w_q_up", "w_kv_up", "w_branch_mla", "w_branch_hgrn", "w_out", "w_ffn_in", "w_ffn_out",
         "conv_w", "conv_b", "g_mix_norm", "g_q_norm", "g_kv_norm", "g_hgrn_norm", "g_ffn_norm", "g_final_norm", "lb_raw"),
        (meta_tokens, w_in, w_q_up, w_kv_up, w_branch_mla, w_branch_hgrn, w_out, w_ffn_in, w_ffn_out,
         conv_w, conv_b, g_mix_norm, g_q_norm, g_kv_norm, g_hgrn_norm, g_ffn_norm, g_final_norm, lb_raw)))
    mom_m = dict(zip(local, (m_meta_tokens, m_w_in, m_w_q_up, m_w_kv_up, m_w_branch_mla, m_w_branch_hgrn, m_w_out, m_w_ffn_in,
                             m_w_ffn_out, m_conv_w, m_conv_b, m_g_mix_norm, m_g_q_norm, m_g_kv_norm, m_g_hgrn_norm,
                             m_g_ffn_norm, m_g_final_norm, m_lb_raw)))
    mom_v = dict(zip(local, (v_meta_tokens, v_w_in, v_w_q_up, v_w_kv_up, v_w_branch_mla, v_w_branch_hgrn, v_w_out, v_w_ffn_in,
                             v_w_ffn_out, v_conv_w, v_conv_b, v_g_mix_norm, v_g_q_norm, v_g_kv_norm, v_g_hgrn_norm,
                             v_g_ffn_norm, v_g_final_norm, v_lb_raw)))
    sharded = BIG + ("conv_w", "meta_tokens")

    def shard2d(name, arr):
        return arr.reshape(arr.shape[-2:]) if name != "meta_tokens" else arr

    def as2d(name, arr):
        return arr.reshape(1, -1) if arr.ndim == 1 else shard2d(name, arr)

    me = 4 * lax.axis_index("x") + 2 * lax.axis_index("y") + lax.axis_index("c")

    def landing(own):
        zone = lax.empty((N_DEV,) + own.shape[1:], own.dtype)
        return lax.dynamic_update_slice_in_dim(zone, own, me, 0)

    shards = {n: shard2d(n, local[n]).astype(BF16) for n in BIG}
    shards.update({n: shard2d(n, local[n]) for n in ("conv_w", "meta_tokens")})
    full = _LazyWeights()
    first = ("w_in", "meta_tokens")
    gathered = _all_gather([shards[n] for n in first])
    for n, g in zip(first, gathered):
        full.ready[n] = _to_kernel_layout({n: _unshard(n, g)})[n]
    later = (("w_q_up", "w_kv_up"), ("w_branch_mla", "w_branch_hgrn", "w_out", "w_ffn_in", "w_ffn_out", "conv_w"))
    for k, names in enumerate(later):
        srcs = [shards[n] for n in names]
        send, recv, srcs_thru, lands_thru, _ = _exchange_start(
            f"gather_start_{k}", srcs, [landing(s[None]) for s in srcs], [False] * len(names), after=[gathered[0]])
        full.add_group(f"gather_wait_{k}", names, send, recv, srcs_thru, lands_thru)
    small = {n: local[n] for n in SMALL}

    started = []

    def sources(group):
        group = _from_kernel_layout(group)
        names = list(group)
        blocked = [n in sharded for n in names]
        srcs = [_reshard(n, group[n]) if b else as2d(n, group[n]) for n, b in zip(names, blocked)]
        return names, blocked, srcs

    def emit(group):
        names, blocked, srcs = sources(group)
        lands = [lax.empty((N_DEV,) + (s.shape[1:] if b else s.shape), s.dtype) for s, b in zip(srcs, blocked)]
        k = len(started)
        send, recv, srcs_thru, lands_thru, token = _exchange_start(f"exchange_start_{k}", srcs, lands, blocked)
        started.append((names, blocked, send, recv, srcs_thru, lands_thru))
        return token

    loss, grad_x, last = _local_step(x[0], positions[0], loss_target[0], full, small, emit)

    out = {}

    me_arr = me.astype(jnp.int32).reshape(1)

    def update(names, parts, owns=None):
        for k, (n, part) in enumerate(zip(names, parts)):
            own = None if owns is None else (owns[k] if owns[k].ndim == 3 else owns[k][None])
            res = _adamw(n, part, as2d(n, local[n]), as2d(n, mom_m[n]), as2d(n, mom_v[n]), own,
                         None if owns is None else me_arr)
            out[n] = [r.reshape(local[n].shape) for r in res]

    after = [last["g_mix_norm"]]
    for k, (names, blocked, send, recv, srcs_thru, lands_thru) in enumerate(started):
        srcs_done, parts = _exchange_wait(f"exchange_wait_{k}", send, recv, srcs_thru, lands_thru, blocked, after)
        update(names, parts, srcs_done)
        after = [out[names[0]][0]]
    names, blocked, srcs = sources(last)
    in_blocks = [(n, s) for n, s, b in zip(names, srcs, blocked) if b]
    whole = [(n, s) for n, s, b in zip(names, srcs, blocked) if not b]
    update([n for n, _ in in_blocks + whole], _exchange([s for _, s in in_blocks], [s for _, s in whole]))

    loss = lax.psum(loss, ("x", "y", "c"))
    order = tuple(local)
    return (loss, grad_x[None], *[out[n][0] for n in order], *[out[n][1] for n in order],
            *[out[n][2] for n in order], *[out[n][3] for n in order])
```

```python
import functools

import jax
import jax.numpy as jnp
import numpy as np
from jax import lax
from jax.experimental import pallas as pl
from jax.experimental.pallas import tpu as pltpu

F32 = jnp.float32
BF16 = jnp.bfloat16

D_MODEL = 2048
N_META = 16
BLOCK = 128
PAD_LEN = BLOCK - N_META
HEADS = 16
Q_LORA = 1536
KV_LORA = 512
ROPE = 64
NOPE = 128
VDIM = 128
D_FF = 5632
NORM_EPS = 1e-6
ROPE_THETA = 10000.0
ATTN_SCALE = (NOPE + ROPE) ** -0.5
ADAM_LR = 0.001
ADAM_B1 = 0.9
ADAM_B2 = 0.999
ADAM_EPS = 1e-08
ADAM_WD = 0.01
ADAM_STEP = 10
N_DEV = 8

LANE = 128
SEG_Q_LAT = 0
SEG_KV_LAT = Q_LORA
SEG_HQ = 2048
SEG_HF = SEG_HQ + D_MODEL
SEG_HI = SEG_HF + D_MODEL
SEG_HG = SEG_HI + D_MODEL
SEG_GA = SEG_HG + D_MODEL
SEG_GB = SEG_GA + D_MODEL
SEG_KR = SEG_GB + D_MODEL
KR_W = 256
PROJ_W = SEG_KR + KR_W
QHEAD_W = 256

V7X_VMEM_BYTES = 64 * 1024 * 1024
VMEM_LIMIT = V7X_VMEM_BYTES * 7 // 8
NEG_BIG = -1e30
SUB = 8


def _tile(n, target, mult):
    best = None
    for t in range(mult, min(n, target) + 1, mult):
        if n % t == 0:
            best = t
    return n if best is None else best


def _params(sem):
    return pltpu.CompilerParams(dimension_semantics=sem, vmem_limit_bytes=VMEM_LIMIT)


def _sigmoid(x):
    return 0.5 * jnp.tanh(0.5 * x) + 0.5


MATMUL_WINDOW_BYTES = 40 * 1024 * 1024
_DIMS = {"nn":(((1,), (0,)), ((), ())), "nt": (((1,), (1,)), ((), ())), "tn": (((0,), (0,)), ((), ()))}


def _matmul(a, b, mode, out_dtype, name, after=None, col_blocks=1):
    if mode == "nn":
        (m, k), (_, n) = a.shape, b.shape
    elif mode == "nt":
        (m, k), (n, _) = a.shape, b.shape
    else:
        (k, m), (_, n) = a.shape, b.shape
    tm = _tile(m, 1040, 8) if mode != "tn" else _tile(m, 1024, LANE)
    tn = _tile(n, 1024, LANE) if col_blocks == 1 else n // col_blocks
    if mode != "tn":
        tk = _tile(k, 2816, LANE)
    else:
        out_bytes = 2 * tm * tn * jnp.dtype(out_dtype).itemsize + 4 * tm * tn
        tk = _tile(k, max(8, (MATMUL_WINDOW_BYTES - out_bytes) // (4 * (tm + tn))), 8)
    nk = k // tk
    if mode == "nn":
        a_spec = pl.BlockSpec((tm, tk), lambda i, j, kk: (i, kk))
        b_spec = pl.BlockSpec((tk, tn), lambda i, j, kk: (kk, j))
    elif mode == "nt":
        a_spec = pl.BlockSpec((tm, tk), lambda i, j, kk: (i, kk))
        b_spec = pl.BlockSpec((tn, tk), lambda i, j, kk: (j, kk))
    else:
        a_spec = pl.BlockSpec((tk, tm), lambda i, j, kk: (kk, i))
        b_spec = pl.BlockSpec((tk, tn), lambda i, j, kk: (kk, j))
    dims = _DIMS[mode]

    n_after = 0 if after is None else 1

    def body(a_ref, b_ref, *rest):
        o_ref, acc = rest[n_after], rest[n_after + 1:]
        part = lax.dot_general(a_ref[...], b_ref[...], dims, preferred_element_type=F32)
        if nk == 1:
            o_ref[...] = part.astype(o_ref.dtype)
            return
        acc_ref, kk = acc[0], pl.program_id(2)

        @pl.when(kk == 0)
        def _():
            acc_ref[...] = part

        @pl.when((kk > 0) & (kk < nk - 1))
        def _():
            acc_ref[...] += part

        @pl.when(kk == nk - 1)
        def _():
            o_ref[...] = (acc_ref[...] + part).astype(o_ref.dtype)

    if col_blocks == 1:
        out_shape = jax.ShapeDtypeStruct((m, n), out_dtype)
        out_spec = pl.BlockSpec((tm, tn), lambda i, j, kk: (i, j))
    else:
        out_shape = jax.ShapeDtypeStruct((col_blocks, m, tn), out_dtype)
        out_spec = pl.BlockSpec((None, tm, tn), lambda i, j, kk: (j, i, 0))
    return pl.pallas_call(
        body,
        name=name,
        out_shape=out_shape,
        grid=(m // tm, n // tn, nk),
        in_specs=[a_spec, b_spec] + [pl.BlockSpec(memory_space=pl.ANY)] * n_after,
        out_specs=out_spec,
        scratch_shapes=[pltpu.VMEM((tm, tn), F32)] if nk > 1 else [],
        compiler_params=_params(("parallel", "parallel", "arbitrary")),
    )(a, b, *([after] * n_after))


ROW_WINDOW_BYTES = 12 * 1024 * 1024


def _rowwise(name, fn, ins, outs, rows, tm, after=None):
    per_row = sum(s[2] * s[1].dtype.itemsize for s in ins if s[0] == "row")
    per_row += sum(s[1] * jnp.dtype(s[2]).itemsize for s in outs if s[0] in ("row", "tail"))
    if per_row:
        tm = _tile(rows, min(tm, max(8, ROW_WINDOW_BYTES // (2 * per_row))), 8)
    n_in = len(ins)
    in_specs, args = [], []
    for spec in ins:
        if spec[0] == "row":
            _, arr, w, cb = spec
            in_specs.append(pl.BlockSpec((tm, w), functools.partial(lambda i, cb: (i, cb), cb=cb)))
        else:
            arr = spec[1]
            in_specs.append(pl.BlockSpec(arr.shape, lambda i: (0, 0)))
        args.append(arr)
    out_shape, out_specs = [], []
    for spec in outs:
        if spec[0] == "row":
            out_shape.append(jax.ShapeDtypeStruct((rows, spec[1]), spec[2]))
            out_specs.append(pl.BlockSpec((tm, spec[1]), lambda i: (i, 0)))
        elif spec[0] == "tail":
            out_shape.append(jax.ShapeDtypeStruct((rows - tm, spec[1]), spec[2]))
            out_specs.append(pl.BlockSpec((tm, spec[1]), lambda i: (jnp.maximum(i - 1, 0), 0)))
        elif spec[0] == "head":
            out_shape.append(jax.ShapeDtypeStruct((tm, spec[1]), spec[2]))
            out_specs.append(pl.BlockSpec((tm, spec[1]), lambda i: (0, 0)))
        else:
            out_shape.append(jax.ShapeDtypeStruct(spec[1], F32))
            out_specs.append(pl.BlockSpec(spec[1], lambda i: (0, 0)))
    has_acc = any(s[0] != "row" for s in outs)
    n_after = 0 if after is None else 1
    in_specs += [pl.BlockSpec(memory_space=pl.ANY)] * n_after
    args += [after] * n_after

    def body(*refs):
        i = pl.program_id(0)
        res = fn(i, tm, *[r[...] for r in refs[:n_in]])
        for spec, ref, val in zip(outs, refs[n_in + n_after:], res):
            if spec[0] in ("row", "tail"):
                ref[...] = val.astype(ref.dtype)
            elif spec[0] == "head":
                @pl.when(i == 0)
                def _(ref=ref, val=val):
                    ref[...] = val.astype(ref.dtype)
            else:
                @pl.when(i == 0)
                def _(ref=ref, val=val):
                    ref[...] = val

                @pl.when(i > 0)
                def _(ref=ref, val=val):
                    ref[...] += val

    return pl.pallas_call(
        body,
        name=name,
        out_shape=out_shape,
        grid=(rows // tm,),
        in_specs=in_specs,
        out_specs=out_specs,
        compiler_params=_params(("arbitrary" if has_acc else "parallel",)),
    )(*args)


def _rms_fwd(x, g):
    r = lax.rsqrt(jnp.mean(x * x, axis=-1, keepdims=True) + NORM_EPS)
    return x * r * g


def _rms_bwd(x, g, dy):
    r = lax.rsqrt(jnp.mean(x * x, axis=-1, keepdims=True) + NORM_EPS)
    xhat = x * r
    dxhat = dy * g
    dx = r * (dxhat - xhat * jnp.mean(dxhat * xhat, axis=-1, keepdims=True))
    return dx, jnp.sum(dy * xhat, axis=0, keepdims=True)


def _silu(x):
    return x * _sigmoid(x)


def _dsilu(x):
    s = _sigmoid(x)
    return s * (1.0 + x * (1.0 - s))


def _silu_both(x):
    s = _sigmoid(x)
    return x * s, s * (1.0 + x * (1.0 - s))


def _rot_src(x):
    lane = lax.broadcasted_iota(jnp.int32, x.shape, 1)
    return jnp.where(lane < ROPE // 2, pltpu.roll(x, LANE - ROPE // 2, 1), pltpu.roll(x, ROPE // 2, 1))


def _rope_fwd_call(q_raw, kv, proj, cos_t, sin_t, rows, tm):
    def fn(i, tm_, q, kvv, kr, c, s):
        kr_rot = kr[:, :LANE]
        kr_rot = kr_rot * c + _rot_src(kr_rot) * s
        qs, ks, vs = [], [], []
        for h in range(HEADS):
            qn = q[:, h * QHEAD_W:h * QHEAD_W + NOPE]
            qr = q[:, h * QHEAD_W + NOPE:(h + 1) * QHEAD_W]
            qs += [qn * SCORE_TO_LOG2, (qr * c + _rot_src(qr) * s) * SCORE_TO_LOG2]
            ks += [kvv[:, h * 2 * NOPE:h * 2 * NOPE + NOPE], kr_rot]
            vs += [kvv[:, h * 2 * NOPE + NOPE:(h + 1) * 2 * NOPE]]
        return jnp.concatenate(qs, axis=1), jnp.concatenate(ks, axis=1), jnp.concatenate(vs, axis=1)

    return _rowwise(
        "rope_fwd", fn,
        [("row", q_raw, HEADS * QHEAD_W, 0), ("row", kv, HEADS * 2 * NOPE, 0), ("row", proj, KR_W, SEG_KR // KR_W),
         ("row", cos_t, LANE, 0), ("row", sin_t, LANE, 0)],
        [("row", HEADS * QHEAD_W, BF16), ("row", HEADS * QHEAD_W, BF16), ("row", HEADS * VDIM, BF16)],
        rows, tm)


def _rope_bwd_call(dq_att, dk_att, dv, cos_t, sin_t, rows, tm):
    def fn(i, tm_, dq, dk, dvv, c, s):
        qs, kvs = [], []
        dkr = jnp.zeros((dq.shape[0], LANE), F32)
        for h in range(HEADS):
            dqr = dq[:, h * QHEAD_W + NOPE:(h + 1) * QHEAD_W] * ATTN_SCALE
            qs += [dq[:, h * QHEAD_W:h * QHEAD_W + NOPE] * ATTN_SCALE, dqr * c - _rot_src(dqr) * s]
            kvs += [dk[:, h * QHEAD_W:h * QHEAD_W + NOPE], dvv[:, h * VDIM:(h + 1) * VDIM]]
            dkr = dkr + dk[:, h * QHEAD_W + NOPE:(h + 1) * QHEAD_W]
        dkr = dkr * c - _rot_src(dkr) * s
        return (jnp.concatenate(qs, axis=1), jnp.concatenate(kvs, axis=1),
                jnp.concatenate([dkr, jnp.zeros_like(dkr)], axis=1))

    return _rowwise(
        "rope_bwd", fn,
        [("row", dq_att, HEADS * QHEAD_W, 0), ("row", dk_att, HEADS * QHEAD_W, 0), ("row", dv, HEADS * VDIM, 0),
         ("row", cos_t, LANE, 0), ("row", sin_t, LANE, 0)],
        [("row", HEADS * QHEAD_W, BF16), ("row", HEADS * 2 * NOPE, BF16), ("row", KR_W, BF16)],
        rows, tm)


def _attn_mask(q_blk, k_blk, t, keys_on_rows=False):
    qa, ka = (1, 0) if keys_on_rows else (0, 1)
    qs = q_blk * t + lax.broadcasted_iota(jnp.int32, (t, t), qa)
    ks = k_blk * t + lax.broadcasted_iota(jnp.int32, (t, t), ka)
    return (ks <= qs) & ((ks >= PAD_LEN) | (ks == qs))


_NT = _DIMS["nt"]
_TN = _DIMS["tn"]
LOG2E = 1.4426950408889634
SCORE_TO_LOG2 = ATTN_SCALE * LOG2E


def _causal_pairs(nb, by_key):
    if by_key:
        pairs = [(qi, kj) for kj in range(nb) for qi in range(kj, nb)]
    else:
        pairs = [(qi, kj) for qi in range(nb) for kj in range(qi + 1)]
    return (jnp.asarray(np.array([p[0] for p in pairs], np.int32)), jnp.asarray(np.array([p[1] for p in pairs], np.int32)))


def _two_parts(t):
    cut = (t // LANE + 1) // 2 * LANE
    return ((0, cut), (cut, t)) if cut < t else ((0, t),)


def _attn_fwd(q_att, k_att, v, rows):
    t = _tile(rows, 640, LANE)
    nb = rows // t

    def body(qt_ref, kt_ref, q_ref, k_ref, v_ref, o32_ref, obf_ref, lse_ref, m_sc, l_sc, acc_sc):
        qi, kj = qt_ref[pl.program_id(1)], kt_ref[pl.program_id(1)]

        @pl.when(kj == 0)
        def _():
            m_sc[...] = jnp.full_like(m_sc, NEG_BIG)
            l_sc[...] = jnp.zeros_like(l_sc)
            acc_sc[...] = jnp.zeros_like(acc_sc)

        def step(masked):
            q = q_ref[...]
            parts = _two_parts(t)
            scores =[lax.dot_general(q, k_ref[lo:hi, :], _NT, preferred_element_type=F32) for lo, hi in parts]
            m, l, acc = m_sc[...], l_sc[...], acc_sc[...]
            for (lo, hi), s in zip(parts, scores):
                if masked:
                    qs = qi * t + lax.broadcasted_iota(jnp.int32, (t, hi - lo), 0)
                    ks = kj * t + lo + lax.broadcasted_iota(jnp.int32, (t, hi - lo), 1)
                    s = jnp.where((ks <= qs) & ((ks >= PAD_LEN) | (ks == qs)), s, NEG_BIG)
                m_new = jnp.maximum(m, jnp.max(s, axis=1, keepdims=True))
                alpha = jnp.exp2(m - m_new)
                p = jnp.exp2(s - jnp.tile(m_new, (1, (hi - lo) // LANE)))
                l = alpha * l + jnp.sum(p, axis=1, keepdims=True)
                acc = alpha * acc + jnp.dot(p.astype(BF16), v_ref[lo:hi, :], preferred_element_type=F32)
                m = m_new
            m_sc[...], l_sc[...], acc_sc[...] = m, l, acc

        pl.when((kj == qi) | (kj == 0))(functools.partial(step, True))
        pl.when((kj < qi) & (kj > 0))(functools.partial(step, False))

        @pl.when(kj == qi)
        def _():
            o = acc_sc[...] / l_sc[...]
            o32_ref[...] = o
            obf_ref[...] = o.astype(BF16)
            lse_ref[0] = jnp.max((m_sc[...] + jnp.log2(l_sc[...])).T, axis=0, keepdims=True)

    qt, kt = _causal_pairs(nb, by_key=False)
    qmap = lambda h, p, qt_ref, kt_ref: (qt_ref[p], h)
    kmap = lambda h, p, qt_ref, kt_ref: (kt_ref[p], h)
    return pl.pallas_call(
        body,
        name="attn_fwd",
        out_shape=[jax.ShapeDtypeStruct((rows, HEADS * VDIM), F32), jax.ShapeDtypeStruct((rows, HEADS * VDIM), BF16),
                   jax.ShapeDtypeStruct((HEADS, 1, rows), F32)],
        grid_spec=pltpu.PrefetchScalarGridSpec(
            num_scalar_prefetch=2,
            grid=(HEADS, len(qt)),
            in_specs=[pl.BlockSpec((t, QHEAD_W), qmap), pl.BlockSpec((t, QHEAD_W), kmap), pl.BlockSpec((t, VDIM), kmap)],
            out_specs=[pl.BlockSpec((t, VDIM), qmap), pl.BlockSpec((t, VDIM), qmap),
                       pl.BlockSpec((1, 1, t), lambda h, p, qt_ref, kt_ref: (h, 0, qt_ref[p]))],
            scratch_shapes=[pltpu.VMEM((t, LANE), F32), pltpu.VMEM((t, LANE), F32), pltpu.VMEM((t, VDIM), F32)]),
        compiler_params=_params(("parallel", "arbitrary")),
    )(qt, kt, q_att, k_att, v)


def _attn_delta(do, o32, rows, tm):
    def fn(i, tm_, dov, ov):
        prod = dov.astype(F32) * ov
        head_of = lax.broadcasted_iota(jnp.int32, (HEADS * VDIM, LANE), 0) // VDIM
        pick = jnp.where(head_of == lax.broadcasted_iota(jnp.int32, (HEADS * VDIM, LANE), 1), 1.0, 0.0).astype(F32)
        return (jnp.dot(prod, pick, precision=lax.Precision.HIGHEST, preferred_element_type=F32),)

    (delta,) = _rowwise("attn_delta", fn, [("row", do, HEADS * VDIM, 0), ("row", o32, HEADS * VDIM, 0)],
                        [("row", LANE, F32)], rows, tm)
    return delta


def _attn_bwd(q_att, k_att, v, do, lse_row, delta_row, rows):
    t = _tile(rows, 640, LANE)
    nb = rows // t

    def body(qt_ref, kt_ref, q_ref, k_ref, v_ref, do_ref, lse_ref, delta_ref, dq_ref, dk_ref, dv_ref, dk_sc, dv_sc):
        qi, kj = qt_ref[pl.program_id(1)], kt_ref[pl.program_id(1)]

        @pl.when(pl.program_id(1) == 0)
        def _():
            dq_ref[...] = jnp.zeros_like(dq_ref)

        @pl.when(qi == kj)
        def _():
            dk_sc[...] = jnp.zeros_like(dk_sc)
            dv_sc[...] = jnp.zeros_like(dv_sc)

        def step(masked):
            k, vv = k_ref[...], v_ref[...]
            parts = _two_parts(t)
            st_all = [lax.dot_general(k, q_ref[lo:hi, :], _NT, preferred_element_type=F32) for lo, hi in parts]
            dpt_all = [lax.dot_general(vv, do_ref[lo:hi, :], _NT, preferred_element_type=F32) for lo, hi in parts]
            dk, dv = dk_sc[...], dv_sc[...]
            for (lo, hi), st, dpt in zip(parts, st_all, dpt_all):
                pt = jnp.exp2(st - lse_ref[0, :, lo:hi])
                if masked:
                    ks = kj * t + lax.broadcasted_iota(jnp.int32, (t, hi - lo), 0)
                    qs = qi * t + lo + lax.broadcasted_iota(jnp.int32, (t, hi - lo), 1)
                    pt = jnp.where((ks <= qs) & ((ks >= PAD_LEN) | (ks == qs)), pt, 0.0)
                dv = dv + jnp.dot(pt.astype(BF16), do_ref[lo:hi, :], preferred_element_type=F32)
                dst = (pt * (dpt - delta_ref[0, :, lo:hi])).astype(BF16)
                dk = dk + jnp.dot(dst, q_ref[lo:hi, :], preferred_element_type=F32)
                q_rows = pl.ds(pl.multiple_of(qi * t + lo, LANE), hi - lo)
                dq_ref[q_rows, :] += lax.dot_general(dst, k, _TN, preferred_element_type=F32)
            dk_sc[...], dv_sc[...] = dk, dv

        pl.when((qi == kj) | (kj == 0))(functools.partial(step, True))
        pl.when((qi > kj) & (kj > 0))(functools.partial(step, False))

        @pl.when(qi == nb - 1)
        def _():
            dk_ref[...] = dk_sc[...] * (1.0 / LOG2E)
            dv_ref[...] = dv_sc[...]

    qt, kt = _causal_pairs(nb, by_key=True)
    qmap = lambda h, p, qt_ref, kt_ref: (qt_ref[p], h)
    kmap = lambda h, p, qt_ref, kt_ref: (kt_ref[p], h)
    stat = pl.BlockSpec((1, 1, t), lambda h, p, qt_ref, kt_ref: (h, 0, qt_ref[p]))
    return pl.pallas_call(
        body,
        name="attn_bwd",
        out_shape=[jax.ShapeDtypeStruct((rows, HEADS * QHEAD_W), F32), jax.ShapeDtypeStruct((rows, HEADS * QHEAD_W), F32),
                   jax.ShapeDtypeStruct((rows, HEADS * VDIM), F32)],
        grid_spec=pltpu.PrefetchScalarGridSpec(
            num_scalar_prefetch=2,
            grid=(HEADS, len(qt)),
            in_specs=[pl.BlockSpec((t, QHEAD_W), qmap), pl.BlockSpec((t, QHEAD_W), kmap), pl.BlockSpec((t, VDIM), kmap),
                      pl.BlockSpec((t, VDIM), qmap), stat, stat],
            out_specs=[pl.BlockSpec((rows, QHEAD_W), lambda h, p, qt_ref, kt_ref: (0, h)),
                       pl.BlockSpec((t, QHEAD_W), kmap), pl.BlockSpec((t, VDIM), kmap)],
            scratch_shapes=[pltpu.VMEM((t, QHEAD_W), F32), pltpu.VMEM((t, VDIM), F32)]),
        compiler_params=_params(("parallel", "arbitrary")),
    )(qt, kt, q_att, k_att, v, do, lse_row, delta_row)


C = BLOCK


def _hgrn_prep(hq, hf, hi, lb, c):
    rows = c * C + lax.broadcasted_iota(jnp.int32, (C, C), 0)
    valid = rows >= PAD_LEN
    sg = _sigmoid(hf)
    f = lb + (1.0 - lb) * sg
    g = jnp.where(valid, jnp.log(f), 0.0)
    k = jnp.where(valid, 1.0 - f, 0.0)
    q = _silu(hq)
    r = lax.broadcasted_iota(jnp.int32, (C, C), 0)
    cc = lax.broadcasted_iota(jnp.int32, (C, C), 1)
    tri = jnp.where(cc <= r, 1.0, 0.0).astype(F32)
    b = jnp.dot(tri, g, precision=lax.Precision.HIGHEST, preferred_element_type=F32)
    return q, k, hi, b, f, sg, valid


def _last_row_as_col(b_t):
    lane = lax.broadcasted_iota(jnp.int32, b_t.shape, 1)
    return jnp.sum(jnp.where(lane == C - 1, b_t, 0.0), axis=1, keepdims=True)


def _k_scaled(k, b, bs):
    return (k * jnp.exp(jnp.minimum(bs - b, 0.0))).astype(BF16)


def _hgrn_fwd(proj, lb, rows):
    nc = rows // C

    def body(hq_ref, hf_ref, hi_ref, lb_ref, o_ref, a_ref, s_ref, s_sc, b_sc):
        c = pl.program_id(1)

        @pl.when(c == 0)
        def _():
            s_sc[...] = jnp.zeros_like(s_sc)

        q, k, v, b, _, _, _ = _hgrn_prep(hq_ref[...], hf_ref[...], hi_ref[...], lb_ref[...], c)
        b_sc[...] = b
        s0 = s_sc[...]
        s_ref[0, 0] = s0
        v_bf = v.astype(BF16)
        r16 = lax.broadcasted_iota(jnp.int32, (SUB, C), 0)
        c16 = lax.broadcasted_iota(jnp.int32, (SUB, C), 1)
        slabs = [jnp.zeros((SUB, C), F32)]
        for i in range(1, C // SUB):
            bs = b_sc[SUB * i - 1:SUB * i, :]
            qs = (q[SUB * i:SUB * (i + 1)] * jnp.exp(b[SUB * i:SUB * (i + 1)] - bs)).astype(BF16)
            a_i = lax.dot_general(qs, _k_scaled(k, b, bs), _NT, preferred_element_type=F32)
            slabs.append(jnp.where(c16 <= r16 + (SUB * i - SUB), a_i, 0.0))
        a_off = jnp.concatenate(slabs, axis=0)
        q_t, k_t, b_t = q.T, k.T, b.T
        sub = lax.broadcasted_iota(jnp.int32, (C, C), 0)
        lane = lax.broadcasted_iota(jnp.int32, (C, C), 1)
        lane1 = lax.broadcasted_iota(jnp.int32, (1, C), 1)
        at_band = jnp.zeros((C, C), F32)
        ahead = lane - sub
        for dl in range(SUB):
            k_s = pltpu.roll(k_t, dl, 1) if dl else k_t
            b_s = pltpu.roll(b_t, dl, 1) if dl else b_t
            e = jnp.exp(b_t - b_s)
            band = jnp.sum(q_t * k_s * e, axis=0, keepdims=True)
            band = jnp.where(lane1 >= dl, band, 0.0)
            at_band = at_band + jnp.where(ahead == dl, jnp.broadcast_to(band, (C, C)), 0.0)
        a = (a_off + at_band.T).astype(BF16)
        a_ref[0] = a
        qe = (q * jnp.exp(b)).astype(BF16)
        o_ref[...] = (jnp.dot(a, v_bf, preferred_element_type=F32)
                      + jnp.dot(qe, s0.astype(BF16), preferred_element_type=F32))
        b_last = b_sc[C - 1:C, :]
        kd = (k * jnp.exp(b_last - b)).astype(BF16)
        s_sc[...] = (jnp.exp(_last_row_as_col(b_t)) * s0
                     + lax.dot_general(kd, v_bf, _TN, preferred_element_type=F32))

    seg = lambda base: (lambda h, c: (c, base // C + h))
    return pl.pallas_call(
        body,
        name="hgrn_fwd",
        out_shape=[jax.ShapeDtypeStruct((rows, D_MODEL), F32), jax.ShapeDtypeStruct((HEADS, rows, C), BF16),
                   jax.ShapeDtypeStruct((HEADS, nc, C, C), F32)],
        grid=(HEADS, nc),
        in_specs=[pl.BlockSpec((C, C), seg(SEG_HQ)), pl.BlockSpec((C, C), seg(SEG_HF)), pl.BlockSpec((C, C), seg(SEG_HI)),
                  pl.BlockSpec((1, C), lambda h, c: (0, h))],
        out_specs=[pl.BlockSpec((C, C), lambda h, c: (c, h)), pl.BlockSpec((1, C, C), lambda h, c: (h, c, 0)),
                   pl.BlockSpec((1, 1, C, C), lambda h, c: (h, c, 0, 0))],
        scratch_shapes=[pltpu.VMEM((C, C), F32), pltpu.VMEM((C, C), F32)],
        compiler_params=_params(("parallel", "arbitrary")),
    )(proj, proj, proj, lb)


def _hgrn_bwd(proj, lb, a_mat, s_states, do_h, rows):
    nc = rows // C

    def body(hq_ref, hf_ref, hi_ref, lb_ref, a_ref, s_ref, do_ref, dhq_ref, dhf_ref, dhi_ref, dlb_ref, ds_sc, b_sc):
        step = pl.program_id(1)
        c = nc - 1 - step

        @pl.when(step == 0)
        def _():
            ds_sc[...] = jnp.zeros_like(ds_sc)
            dlb_ref[...] = jnp.zeros_like(dlb_ref)

        hq, hf = hq_ref[...], hf_ref[...]
        lb_row = lb_ref[...]
        q, k, v, b, f, sg, valid = _hgrn_prep(hq, hf, hi_ref[...], lb_row, c)
        b_sc[...] = b
        s0 = s_ref[0, 0]
        ds1 = ds_sc[...]
        s0_bf, ds1_bf = s0.astype(BF16), ds1.astype(BF16)
        do = do_ref[...]
        do_bf, v_bf = do.astype(BF16), v.astype(BF16)
        b_last = b_sc[C - 1:C, :]
        e_last = jnp.exp(b_last - b)
        eb = jnp.exp(b)
        sub = lax.broadcasted_iota(jnp.int32, (C, C), 0)
        lane = lax.broadcasted_iota(jnp.int32, (C, C), 1)
        r16 = lax.broadcasted_iota(jnp.int32, (SUB, C), 0)
        c16 = lax.broadcasted_iota(jnp.int32, (SUB, C), 1)

        dv = (lax.dot_general(a_ref[0], do_bf, _TN, preferred_element_type=F32)
              + jnp.dot((k * e_last).astype(BF16), ds1_bf, preferred_element_type=F32))
        da = jnp.where(lane <= sub, lax.dot_general(do_bf, v_bf, _NT, preferred_element_type=F32), 0.0)
        da_t = jnp.where(sub <= lane, lax.dot_general(v_bf, do_bf, _NT, preferred_element_type=F32), 0.0)

        dq_slabs = [jnp.zeros((SUB, C), F32)]
        for i in range(1, C // SUB):
            bs = b_sc[SUB * i - 1:SUB * i, :]
            da_i = jnp.where(c16 <= r16 + (SUB * i - SUB), da[SUB * i:SUB * (i + 1)], 0.0).astype(BF16)
            dq_slabs.append(jnp.exp(b[SUB * i:SUB * (i + 1)] - bs)
                            * jnp.dot(da_i, _k_scaled(k, b, bs), preferred_element_type=F32))
        dk_slabs = []
        for j in range(C // SUB - 1):
            be = b_sc[SUB * j + SUB - 1:SUB * (j + 1), :]
            qe_j = (q * jnp.exp(jnp.minimum(b - be, 0.0))).astype(BF16)
            da_j = jnp.where(c16 >= r16 + (SUB * j + SUB), da_t[SUB * j:SUB * (j + 1)], 0.0).astype(BF16)
            dk_slabs.append(jnp.exp(be - b[SUB * j:SUB * (j + 1)]) * jnp.dot(da_j, qe_j, preferred_element_type=F32))
        dk_slabs.append(jnp.zeros((SUB, C), F32))

        q_t, k_t, b_t = q.T, k.T, b.T
        lane1 = lax.broadcasted_iota(jnp.int32, (1, C), 1)
        dq_t = jnp.zeros((C, C), F32)
        dk_t = jnp.zeros((C, C), F32)
        ahead = lane - sub
        for dl in range(SUB):
            k_s = pltpu.roll(k_t, dl, 1) if dl else k_t
            b_s = pltpu.roll(b_t, dl, 1) if dl else b_t
            e = jnp.exp(jnp.minimum(b_t - b_s, 0.0))
            dband = jnp.sum(jnp.where(ahead == dl, da_t, 0.0), axis=0, keepdims=True)
            w = jnp.where(lane1 >= dl, dband, 0.0) * e
            dq_t = dq_t + w * k_s
            back = w * q_t
            dk_t = dk_t + (pltpu.roll(back, C - dl, 1) if dl else back)

        dq = eb * lax.dot_general(do_bf, s0_bf, _NT, preferred_element_type=F32) + jnp.concatenate(dq_slabs, axis=0) + dq_t.T
        dk_inter = e_last * lax.dot_general(v_bf, ds1_bf, _NT, preferred_element_type=F32)
        dk = dk_inter + jnp.concatenate(dk_slabs, axis=0) + dk_t.T

        extra = (jnp.exp(b_last) * jnp.sum((s0 * ds1).T, axis=0, keepdims=True)
                 + jnp.sum(k * dk_inter, axis=0, keepdims=True))
        db = q * dq - k * dk + jnp.where(sub == C - 1, jnp.broadcast_to(extra, (C, C)), 0.0)
        tri_t = jnp.where(lane >= sub, 1.0, 0.0).astype(F32)
        dg = jnp.dot(tri_t, db, precision=lax.Precision.HIGHEST, preferred_element_type=F32)
        ds_sc[...] = (jnp.exp(_last_row_as_col(b_t)) * ds1
                      + lax.dot_general((q * eb).astype(BF16), do_bf, _TN, preferred_element_type=F32))

        df = jnp.where(valid, dg / f - dk, 0.0)
        dhf_ref[...] = (df * (1.0 - lb_row) * sg * (1.0 - sg)).astype(BF16)
        dlb_ref[...] += jnp.sum(df * (1.0 - sg), axis=0, keepdims=True)
        dhq_ref[...] = (dq * _dsilu(hq)).astype(BF16)
        dhi_ref[...] = dv.astype(BF16)

    seg = lambda base: (lambda h, s: (nc - 1 - s, base // C + h))
    rmap = lambda h, s: (nc - 1 - s, h)
    return pl.pallas_call(
        body,
        name="hgrn_bwd",
        out_shape=[jax.ShapeDtypeStruct((rows, D_MODEL), BF16)] * 3 + [jax.ShapeDtypeStruct((1, D_MODEL), F32)],
        grid=(HEADS, nc),
        in_specs=[pl.BlockSpec((C, C), seg(SEG_HQ)), pl.BlockSpec((C, C), seg(SEG_HF)), pl.BlockSpec((C, C), seg(SEG_HI)),
                  pl.BlockSpec((1, C), lambda h, s: (0, h)),
                  pl.BlockSpec((1, C, C), lambda h, s: (h, nc - 1 - s, 0)),
                  pl.BlockSpec((1, 1, C, C), lambda h, s: (h, nc - 1 - s, 0, 0)),
                  pl.BlockSpec((C, C), rmap)],
        out_specs=[pl.BlockSpec((C, C), rmap)] * 3 + [pl.BlockSpec((1, C), lambda h, s: (0, h))],
        scratch_shapes=[pltpu.VMEM((C, C), F32), pltpu.VMEM((C, C), F32)],
        compiler_params=_params(("parallel", "arbitrary")),
    )(proj, proj, proj, lb, a_mat, s_states, do_h)


CONV_TC = 512
HALO = 16


def _halo_row(block, k):
    r = lax.broadcasted_iota(jnp.int32, block.shape, 0)
    return jnp.sum(jnp.where(r == k, block, 0.0), axis=0, keepdims=True)


def _conv_taps(i, tm, g_ref, pg_ref):
    shape = g_ref.shape
    r = lax.broadcasted_iota(jnp.int32, shape, 0)
    g = jnp.where(i * tm + r >= PAD_LEN, g_ref[...].astype(F32), 0.0)
    prev = pg_ref[...].astype(F32)
    p1 = jnp.where(i * tm - 1 >= PAD_LEN, _halo_row(prev, HALO - 1), 0.0)
    p2 = jnp.where(i * tm - 2 >= PAD_LEN, _halo_row(prev, HALO - 2), 0.0)
    s1 = jnp.where(r == 0, p1, pltpu.roll(g, 1, 0))
    s2 = jnp.where(r == 0, p2, jnp.where(r == 1, p1, pltpu.roll(g, 2, 0)))
    return g, s1, s2


def _conv_specs(tm, tc, ncb, order):
    gate = pl.BlockSpec((tm, tc), lambda *ids: order(ids))
    halo = pl.BlockSpec((HALO, tc), lambda *ids: (jnp.maximum(order(ids)[0] * (tm // HALO) - 1, 0), order(ids)[1]))
    up = pl.BlockSpec((tm, tc), lambda *ids: (order(ids)[0], ncb + order(ids)[1]))
    return gate, halo, up


def _conv_fwd(ffn, conv_w, conv_b, rows, tm):
    tc = CONV_TC
    ncb = D_FF // tc

    def body(g_ref, pg_ref, up_ref, cw_ref, cb_ref, act_ref):
        i = pl.program_id(0)
        g, s1, s2 = _conv_taps(i, tm, g_ref, pg_ref)
        conv = (cw_ref[0:1, :] * s2 + cw_ref[1:2, :] * s1 + cw_ref[2:3, :] * g) + cb_ref[...]
        act_ref[...] = (_silu(conv) * up_ref[...].astype(F32)).astype(BF16)

    gate, halo, up = _conv_specs(tm, tc, ncb, lambda ids: (ids[0], ids[1]))
    return pl.pallas_call(
        body,
        name="conv_fwd",
        out_shape=jax.ShapeDtypeStruct((rows, D_FF), BF16),
        grid=(rows // tm, ncb),
        in_specs=[gate, halo, up, pl.BlockSpec((3, tc), lambda i, j: (0, j)), pl.BlockSpec((1, tc), lambda i, j: (0, j))],
        out_specs=pl.BlockSpec((tm, tc), lambda i, j: (i, j)),
        compiler_params=_params(("parallel", "parallel")),
    )(ffn, ffn, ffn, conv_w, conv_b)


def _conv_bwd_a(ffn, dact, conv_w, conv_b, rows, tm):
    tc = CONV_TC
    ncb = D_FF // tc

    def body(g_ref, pg_ref, up_ref, da_ref, cw_ref, cb_ref, dc_ref, dffn_ref, w0_ref, w1_ref, w2_ref, db_ref):
        i = pl.program_id(1)
        g, s1, s2 = _conv_taps(i, tm, g_ref, pg_ref)
        conv = (cw_ref[0:1, :] * s2 + cw_ref[1:2, :] * s1 + cw_ref[2:3, :] * g) + cb_ref[...]
        da = da_ref[...].astype(F32)
        act, dact_dconv = _silu_both(conv)
        dffn_ref[...] = (da * act).astype(BF16)
        dc = da * up_ref[...].astype(F32) * dact_dconv
        dc_ref[...] = dc.astype(BF16)
        sums = [jnp.sum(dc * s2, axis=0, keepdims=True), jnp.sum(dc * s1, axis=0, keepdims=True),
                jnp.sum(dc * g, axis=0, keepdims=True), jnp.sum(dc, axis=0, keepdims=True)]
        for ref, val in zip((w0_ref, w1_ref, w2_ref, db_ref), sums):
            @pl.when(i == 0)
            def _(ref=ref, val=val):
                ref[...] = val

            @pl.when(i > 0)
            def _(ref=ref, val=val):
                ref[...] += val

    gate, halo, up = _conv_specs(tm, tc, ncb, lambda ids: (ids[1], ids[0]))
    col = pl.BlockSpec((1, tc), lambda j, i: (0, j))
    return pl.pallas_call(
        body,
        name="conv_bwd_a",
        out_shape=[jax.ShapeDtypeStruct((rows, D_FF), BF16), jax.ShapeDtypeStruct((rows, 2 * D_FF), BF16)]
        + [jax.ShapeDtypeStruct((1, D_FF), F32)] * 4,
        grid=(ncb, rows // tm),
        in_specs=[gate, halo, up, pl.BlockSpec((tm, tc), lambda j, i: (i, j)),
                  pl.BlockSpec((3, tc), lambda j, i: (0, j)), col],
        out_specs=[pl.BlockSpec((tm, tc), lambda j, i: (i, j)), pl.BlockSpec((tm, tc), lambda j, i: (i, ncb + j)),
                   col, col, col, col],
        compiler_params=_params(("parallel", "arbitrary")),
    )(ffn, ffn, ffn, dact, conv_w, conv_b)


def _conv_bwd_b(dconv, conv_w, dffn, rows, tm):
    tc = CONV_TC
    ncb = D_FF // tc
    nrb = rows // tm

    def body(dc_ref, nx_ref, cw_ref, dffn_in, out_ref):
        del dffn_in
        i = pl.program_id(0)
        dc = dc_ref[...].astype(F32)
        r = lax.broadcasted_iota(jnp.int32, dc.shape, 0)
        last = i == nrb - 1
        nxt = nx_ref[...].astype(F32)
        x1 = jnp.where(last, 0.0, _halo_row(nxt, 0))
        x2 = jnp.where(last, 0.0, _halo_row(nxt, 1))
        n1 = jnp.where(r == tm - 1, x1, pltpu.roll(dc, tm - 1, 0))
        n2 = jnp.where(r == tm - 1, x2, jnp.where(r == tm - 2, x1, pltpu.roll(dc, tm - 2, 0)))
        dg = cw_ref[2:3, :] * dc + cw_ref[1:2, :] * n1 + cw_ref[0:1, :] * n2
        out_ref[...] = jnp.where(i * tm + r >= PAD_LEN, dg, 0.0).astype(BF16)

    return pl.pallas_call(
        body,
        name="conv_bwd_b",
        out_shape=jax.ShapeDtypeStruct((rows, 2 * D_FF), BF16),
        grid=(nrb, ncb),
        in_specs=[pl.BlockSpec((tm, tc), lambda i, j: (i, j)),
                  pl.BlockSpec((HALO, tc), lambda i, j: (jnp.minimum((i + 1) * (tm // HALO), rows // HALO - 1), j)),
                  pl.BlockSpec((3, tc), lambda i, j: (0, j)),
                  pl.BlockSpec(memory_space=pl.ANY)],
        out_specs=pl.BlockSpec((tm, tc), lambda i, j: (i, j)),
        input_output_aliases={3: 0},
        compiler_params=_params(("parallel", "parallel")),
    )(dconv, dconv, conv_w, dffn)


def _final_call(h1, y, target, g_final, rows):
    tm = BLOCK

    def fn(i, tm_, h1v, yv, tgt, g):
        h2 = h1v + yv
        out = _rms_fwd(h2, g)
        err = jnp.where(i > 0, out - tgt, 0.0)
        loss = 0.5 * jnp.sum(jnp.mean(err * err, axis=-1, keepdims=True), axis=0, keepdims=True)
        dx, dg = _rms_bwd(h2, g, err * (1.0 / D_MODEL))
        return dx, dx, jnp.broadcast_to(loss, (1, LANE)), dg

    n_in = 4
    in_specs = [pl.BlockSpec((tm, D_MODEL), lambda i: (i, 0)), pl.BlockSpec((tm, D_MODEL), lambda i: (i, 0)),
                pl.BlockSpec((tm, D_MODEL), lambda i: (jnp.maximum(i - 1, 0), 0)),
                pl.BlockSpec((1, D_MODEL), lambda i: (0, 0))]

    def body(*refs):
        i = pl.program_id(0)
        dx, dx2, loss, dg = fn(i, tm, *[r[...] for r in refs[:n_in]])
        refs[4][...] = dx
        refs[5][...] = dx2.astype(BF16)
        for ref, val in ((refs[6], loss), (refs[7], dg)):
            @pl.when(i == 0)
            def _(ref=ref, val=val):
                ref[...] = val

            @pl.when(i > 0)
            def _(ref=ref, val=val):
                ref[...] += val

    return pl.pallas_call(
        body,
        name="final_loss",
        out_shape=[jax.ShapeDtypeStruct((rows, D_MODEL), F32), jax.ShapeDtypeStruct((rows, D_MODEL), BF16),
                   jax.ShapeDtypeStruct((1, LANE), F32), jax.ShapeDtypeStruct((1, D_MODEL), F32)],
        grid=(rows // tm,),
        in_specs=in_specs,
        out_specs=[pl.BlockSpec((tm, D_MODEL), lambda i: (i, 0)), pl.BlockSpec((tm, D_MODEL), lambda i: (i, 0)),
                   pl.BlockSpec((1, LANE), lambda i: (0, 0)), pl.BlockSpec((1, D_MODEL), lambda i: (0, 0))],
        compiler_params=_params(("arbitrary",)),
    )(h1, y, target, g_final)


def _heads_map(fn, *slabs):
    outs = [fn(*[s[:, h * LANE:(h + 1) * LANE] for s in slabs]) for h in range(HEADS)]
    if isinstance(outs[0], tuple):
        return tuple(jnp.concatenate([o[k] for o in outs], axis=1) for k in range(len(outs[0])))
    return jnp.concatenate(outs, axis=1)


def _local_step(x, positions, target, w, p, emit=None):
    kept = {}
    if emit is None:
        def emit(group):
            kept.update(group)
            return None
    s_len = x.shape[0]
    rows = s_len + BLOCK
    tm = _tile(rows, 640, 8)
    row = lambda arr, width, cb=0: ("row", arr, width, cb)

    h0 = jnp.concatenate([jnp.zeros((PAD_LEN, D_MODEL), F32), w["meta_tokens"], x], axis=0)
    pos = jnp.concatenate([jnp.zeros((PAD_LEN,), jnp.int32), jnp.arange(N_META, dtype=jnp.int32),
                           positions.astype(jnp.int32) + N_META])
    inv = 1.0 / (ROPE_THETA ** (jnp.arange(0, ROPE, 2, dtype=F32) / ROPE))
    ang = pos.astype(F32)[:, None] * inv
    zero = jnp.zeros((rows, LANE - ROPE), F32)
    cos_t = jnp.concatenate([jnp.cos(ang), jnp.cos(ang), zero], axis=1)
    sin_t = jnp.concatenate([-jnp.sin(ang), jnp.sin(ang), zero], axis=1)
    lb_r0, lb_r1 = p["lb_raw"][0:1], p["lb_raw"][1:2]

    def lb_fn(i, tm_, r0, r1):
        m = jnp.maximum(r0, r1)
        e0, e1 = jnp.exp(r0 - m), jnp.exp(r1 - m)
        return (e0 / (e0 + e1),)

    (lb,) = _rowwise("lb_fwd", lb_fn, [("bc", lb_r0), ("bc", lb_r1)], [("acc", (1, D_MODEL))], 1, 1)

    (u1,) = _rowwise("mix_norm", lambda i, t, h, g: (_rms_fwd(h, g),),
                     [row(h0, D_MODEL), ("bc", p["g_mix_norm"])], [("row", D_MODEL, BF16)], rows, tm)
    proj = _matmul(u1, w["w_in"], "nn", F32, "mm_proj")
    hint = getattr(w, "hint", lambda name, after: None)
    hint("w_q_up", proj)
    qn, kvn = _rowwise(
        "latent_norm", lambda i, t, ql, kl, gq, gk: (_rms_fwd(ql, gq), _rms_fwd(kl, gk)),
        [row(proj, Q_LORA, 0), row(proj, KV_LORA, SEG_KV_LAT // KV_LORA), ("bc", p["g_q_norm"]), ("bc", p["g_kv_norm"])],
        [("row", Q_LORA, BF16), ("row", KV_LORA, BF16)], rows, tm)
    q_raw = _matmul(qn, w["w_q_up"], "nn", F32, "mm_q_up")
    kv = _matmul(kvn, w["w_kv_up"], "nn", F32, "mm_kv_up")
    q_att, k_att, v_att = _rope_fwd_call(q_raw, kv, proj, cos_t, sin_t, rows, tm)
    o32, o_bf, lse = _attn_fwd(q_att, k_att, v_att, rows)
    hint("w_branch_mla", lse)
    o_h, a_mat, s_states = _hgrn_fwd(proj, lb, rows)

    def hgrn_post(i, t, oh, hg, g):
        return (_heads_map(lambda a, b: _rms_fwd(a, g) * _silu(b), oh, hg),)

    (o_hgrn,) = _rowwise("hgrn_post", hgrn_post,
                         [row(o_h, D_MODEL), row(proj, D_MODEL, SEG_HG // D_MODEL), ("bc", p["g_hgrn_norm"])],
                         [("row", D_MODEL, BF16)], rows, tm)
    br_a = _matmul(o_bf, w["w_branch_mla"], "nn", F32, "mm_branch_mla")
    br_b = _matmul(o_hgrn, w["w_branch_hgrn"], "nn", F32, "mm_branch_hgrn")
    (merged,) = _rowwise(
        "merge", lambda i, t, a, b, ga, gb: (_sigmoid(ga) * a + _sigmoid(gb) * b,),
        [row(br_a, D_MODEL), row(br_b, D_MODEL), row(proj, D_MODEL, SEG_GA // D_MODEL), row(proj, D_MODEL, SEG_GB // D_MODEL)],
        [("row", D_MODEL, BF16)], rows, tm)
    mix_out = _matmul(merged, w["w_out"], "nn", F32, "mm_out")

    def ffn_norm(i, t, h, mo, g):
        h1v = h + mo
        return h1v, _rms_fwd(h1v, g)

    h1, u2 = _rowwise("ffn_norm", ffn_norm, [row(h0, D_MODEL), row(mix_out, D_MODEL), ("bc", p["g_ffn_norm"])],
                      [("row", D_MODEL, F32), ("row", D_MODEL, BF16)], rows, tm)
    ffn = _matmul(u2, w["w_ffn_in"], "nn", BF16, "mm_ffn_in")
    act = _conv_fwd(ffn, w["conv_w"], p["conv_b"], rows, tm)
    y = _matmul(act, w["w_ffn_out"], "nn", F32, "mm_ffn_out")
    dh2, dh2_bf, loss_acc, dg_final = _final_call(h1, y, target, p["g_final_norm"].reshape(1, D_MODEL), rows)

    grads = {"g_final_norm": dg_final.reshape(D_MODEL)}
    dact = _matmul(dh2_bf, w["w_ffn_out"], "nt", BF16, "mm_d_act")
    grads["w_ffn_out"] = _matmul(act, dh2_bf, "tn", BF16, "mm_dw_ffn_out")
    dconv, dffn, dcw0, dcw1, dcw2, dcb = _conv_bwd_a(ffn, dact, w["conv_w"], p["conv_b"], rows, tm)
    dffn = _conv_bwd_b(dconv, w["conv_w"], dffn, rows, tm)
    grads["conv_w"] = jnp.concatenate([dcw0, dcw1, dcw2], axis=0)
    grads["conv_b"] = dcb
    du2 = _matmul(dffn, w["w_ffn_in"], "nt", F32, "mm_d_u2")
    grads["w_ffn_in"] = _matmul(u2, dffn, "tn", BF16, "mm_dw_ffn_in", col_blocks=N_DEV)

    def ffn_norm_bwd(i, t, h, du, dh, g):
        dx, dg = _rms_bwd(h, g, du)
        dh1v = dh + dx
        return dh1v, dh1v, dg

    dh1, dh1_bf, grads["g_ffn_norm"] = _rowwise(
        "ffn_norm_bwd", ffn_norm_bwd, [row(h1, D_MODEL), row(du2, D_MODEL), row(dh2, D_MODEL), ("bc", p["g_ffn_norm"])],
        [("row", D_MODEL, F32), ("row", D_MODEL, BF16), ("acc", (1, D_MODEL))], rows, tm)
    tok = emit({n: grads.pop(n) for n in ("w_ffn_out", "w_ffn_in", "conv_w", "conv_b", "g_final_norm", "g_ffn_norm")})
    dmerged = _matmul(dh1_bf, w["w_out"], "nt", F32, "mm_d_merged", after=tok)
    grads["w_out"] = _matmul(merged, dh1_bf, "tn", BF16, "mm_dw_out")

    def merge_bwd(i, t, dm, a, b, ga, gb):
        sa, sb = _sigmoid(ga), _sigmoid(gb)
        return dm * sa, dm * sb, dm * a * sa * (1.0 - sa), dm * b * sb * (1.0 - sb)

    da_bf, db_bf, dga, dgb = _rowwise(
        "merge_bwd", merge_bwd,
        [row(dmerged, D_MODEL), row(br_a, D_MODEL), row(br_b, D_MODEL),
         row(proj, D_MODEL, SEG_GA // D_MODEL), row(proj, D_MODEL, SEG_GB // D_MODEL)],
        [("row", D_MODEL, BF16)] * 4, rows, tm)
    do_mla = _matmul(da_bf, w["w_branch_mla"], "nt", BF16, "mm_d_o_mla")
    grads["w_branch_mla"] = _matmul(o_bf, da_bf, "tn", BF16, "mm_dw_branch_mla")
    do_hgrn = _matmul(db_bf, w["w_branch_hgrn"], "nt", F32, "mm_d_o_hgrn")
    grads["w_branch_hgrn"] = _matmul(o_hgrn, db_bf, "tn", BF16, "mm_dw_branch_hgrn")

    def hgrn_post_bwd(i, t, dy, oh, hg, g):
        def one(dyh, ohh, hgh):
            gate, dgate = _silu_both(hgh)
            dx, dg = _rms_bwd(ohh, g, dyh * gate)
            return dx, dyh * _rms_fwd(ohh, g) * dgate, dg

        dx, dhg, dg = _heads_map(one, dy, oh, hg)
        dg_sum = dg[:, 0:LANE]
        for h in range(1, HEADS):
            dg_sum = dg_sum + dg[:, h * LANE:(h + 1) * LANE]
        return dx, dhg, dg_sum

    tok = emit({n: grads.pop(n) for n in ("w_out", "w_branch_mla", "w_branch_hgrn")})
    do_h, dhg, grads["g_hgrn_norm"] = _rowwise(
        "hgrn_post_bwd", hgrn_post_bwd,
        [row(do_hgrn, D_MODEL), row(o_h, D_MODEL), row(proj, D_MODEL, SEG_HG // D_MODEL), ("bc", p["g_hgrn_norm"])],
        [("row", D_MODEL, F32), ("row", D_MODEL, BF16), ("acc", (1, LANE))], rows, tm, after=tok)
    dhq, dhf, dhi, dlb = _hgrn_bwd(proj, lb, a_mat, s_states, do_h, rows)

    def lb_bwd(i, tm_, d, l):
        t = d * l * (1.0 - l)
        return t, -t

    dlb0, dlb1 = _rowwise("lb_bwd", lb_bwd, [("bc", dlb), ("bc", lb)], [("acc", (1, D_MODEL))] * 2, 1, 1)
    grads["lb_raw"] = jnp.concatenate([dlb0, dlb1], axis=0)

    delta = _attn_delta(do_mla, o32, rows, tm)
    dq_att, dk_att, dv_att = _attn_bwd(q_att, k_att, v_att, do_mla, lse,
                                       jnp.transpose(delta[:, :HEADS]).reshape(HEADS, 1, rows), rows)
    dq_full, dkv, dkr = _rope_bwd_call(dq_att, dk_att, dv_att, cos_t, sin_t, rows, tm)
    dqn = _matmul(dq_full, w["w_q_up"], "nt", F32, "mm_d_qn")
    grads["w_q_up"] = _matmul(qn, dq_full, "tn", BF16, "mm_dw_q_up")
    dkvn = _matmul(dkv, w["w_kv_up"], "nt", F32, "mm_d_kvn")
    grads["w_kv_up"] = _matmul(kvn, dkv, "tn", BF16, "mm_dw_kv_up")

    def latent_norm_bwd(i, t, ql, kl, dq, dk, gq, gk):
        dql, dgq = _rms_bwd(ql, gq, dq)
        dkl, dgk = _rms_bwd(kl, gk, dk)
        return dql, dkl, dgq, dgk

    dq_lat, dkv_lat, grads["g_q_norm"], grads["g_kv_norm"] = _rowwise(
        "latent_norm_bwd", latent_norm_bwd,
        [row(proj, Q_LORA, 0), row(proj, KV_LORA, SEG_KV_LAT // KV_LORA), row(dqn, Q_LORA), row(dkvn, KV_LORA),
         ("bc", p["g_q_norm"]), ("bc", p["g_kv_norm"])],
        [("row", Q_LORA, BF16), ("row", KV_LORA, BF16), ("acc", (1, Q_LORA)), ("acc", (1, KV_LORA))], rows, tm)
    dproj = jnp.concatenate([dq_lat, dkv_lat, dhq, dhf, dhi, dhg, dga, dgb, dkr], axis=1)
    tok = emit({n: grads.pop(n) for n in ("w_q_up", "w_kv_up", "lb_raw", "g_q_norm", "g_kv_norm", "g_hgrn_norm")})
    grads["w_in"] = _matmul(u1, dproj, "tn", BF16, "mm_dw_in", after=tok)
    tok = emit({"w_in": grads.pop("w_in")})
    du1 = _matmul(dproj, w["w_in"], "nt", F32, "mm_d_u1", after=tok)

    def mix_norm_bwd(i, t, h, du, dh, g):
        dx, dg = _rms_bwd(h, g, du)
        return dh + dx, dh + dx, dg

    grad_x, d_prefix, grads["g_mix_norm"] = _rowwise(
        "mix_norm_bwd", mix_norm_bwd, [row(h0, D_MODEL), row(du1, D_MODEL), row(dh1, D_MODEL), ("bc", p["g_mix_norm"])],
        [("tail", D_MODEL, F32), ("head", D_MODEL, F32), ("acc", (1, D_MODEL))], rows, BLOCK)
    grads["meta_tokens"] = d_prefix[PAD_LEN:BLOCK]
    kept.update(grads)
    return loss_acc[0, 0], grad_x, kept


K_ROPE_AT = Q_LORA + KV_LORA
COL_SHARDED = ("w_in", "w_q_up", "w_kv_up", "w_ffn_in", "conv_w", "meta_tokens")
BIG = ("w_in", "w_q_up", "w_kv_up", "w_branch_mla", "w_branch_hgrn", "w_out", "w_ffn_in", "w_ffn_out")
SMALL = ("conv_b", "g_mix_norm", "g_q_norm", "g_kv_norm", "g_hgrn_norm", "g_ffn_norm", "g_final_norm", "lb_raw")


def _unshard(name, stacked):
    if name in COL_SHARDED:
        return jnp.transpose(stacked, (1, 0, 2)).reshape(stacked.shape[1], N_DEV * stacked.shape[2])
    return stacked.reshape(N_DEV * stacked.shape[1], stacked.shape[2])


def _reshard(name, full):
    if full.ndim == 3:
        return full
    if name in COL_SHARDED:
        r, c = full.shape
        return jnp.transpose(full.reshape(r, N_DEV, c // N_DEV), (1, 0, 2))
    return full.reshape(N_DEV, full.shape[0] // N_DEV, full.shape[1])


def _to_kernel_layout(full):
    out = dict(full)
    if "w_in" in full:
        w_in = full["w_in"]
        pad = jnp.zeros((D_MODEL, KR_W - ROPE), w_in.dtype)
        out["w_in"] = jnp.concatenate(
            [w_in[:, :K_ROPE_AT], w_in[:, K_ROPE_AT + ROPE:], w_in[:, K_ROPE_AT:K_ROPE_AT + ROPE], pad], axis=1)
    if "w_q_up" in full:
        wq = full["w_q_up"].reshape(Q_LORA, HEADS, NOPE + ROPE)
        out["w_q_up"] = jnp.pad(wq, ((0, 0), (0, 0), (0, QHEAD_W - NOPE - ROPE))).reshape(Q_LORA, HEADS * QHEAD_W)
    return out


def _from_kernel_layout(grads):
    out = dict(grads)
    if "w_in" in grads:
        g = grads["w_in"]
        out["w_in"] = jnp.concatenate([g[:, :K_ROPE_AT], g[:, SEG_KR:SEG_KR + ROPE], g[:, K_ROPE_AT:SEG_KR]], axis=1)
    if "w_q_up" in grads:
        g = grads["w_q_up"].reshape(Q_LORA, HEADS, QHEAD_W)
        out["w_q_up"] = g[:, :, :NOPE + ROPE].reshape(Q_LORA, HEADS * (NOPE + ROPE))
    return out


MESH_ID = pl.DeviceIdType.MESH
ANY = pl.BlockSpec(memory_space=pl.ANY)


def _slot(dev):
    return 4 * dev[0] + 2 * dev[1] + dev[2]


def _all_gather(shards):
    n = len(shards)

    def body(*refs):
        ins, outs = refs[:n], refs[n:2 * n]
        send_sems, recv_sems, local_sems = refs[2 * n:]
        x, y, c = lax.axis_index("x"), lax.axis_index("y"), lax.axis_index("c")
        me, sibling = (x, y, c), (x, y, 1 - c)
        chips = [(1 - x, y), (x, 1 - y), (1 - x, 1 - y)]

        def copy(a, k, block, to, src=None):
            dst = outs[a].at[_slot(block)]
            return pltpu.make_async_remote_copy(
                src_ref=dst if src is None else src, dst_ref=dst, send_sem=send_sems.at[a, k],
                recv_sem=recv_sems.at[a, k], device_id=to, device_id_type=MESH_ID)

        mine = [pltpu.make_async_copy(ins[a], outs[a].at[_slot(me)], local_sems.at[a]) for a in range(n)]
        for cp in mine:
            cp.start()
        first = []
        for a in range(n):
            first.append(copy(a, 0, me, sibling, src=ins[a]))
            first += [copy(a, 1 + j, me, (*chip, c), src=ins[a]) for j, chip in enumerate(chips)]
        for cp in first:
            cp.start()
        passed = []
        for a in range(n):
            for j, chip in enumerate(chips):
                copy(a, 1 + j, (*chip, c), me).wait_recv()
                fwd = copy(a, 4 + j, (*chip, c), sibling)
                fwd.start()
                passed.append(fwd)
        for a in range(n):
            copy(a, 0, sibling, me).wait_recv()
            for j, chip in enumerate(chips):
                copy(a, 4 + j, (*chip, 1 - c), me).wait_recv()
        for cp in first + passed:
            cp.wait_send()
        for cp in mine:
            cp.wait()

    return pl.pallas_call(
        body,
        name="gather_weights",
        out_shape=[jax.ShapeDtypeStruct((N_DEV,) + s.shape, s.dtype) for s in shards],
        in_specs=[ANY] * n,
        out_specs=[ANY] * n,
        scratch_shapes=[pltpu.SemaphoreType.DMA((n, 7)), pltpu.SemaphoreType.DMA((n, 7)), pltpu.SemaphoreType.DMA((n,))],
    )(*shards)


def _exchange(blocked, replicated):
    nb, n = len(blocked), len(blocked) + len(replicated)
    arrays = list(blocked) + list(replicated)

    def body(*refs):
        ins, outs = refs[:n], refs[n:2 * n]
        send_sems, recv_sems, local_sems = refs[2 * n:]
        x, y, c = lax.axis_index("x"), lax.axis_index("y"), lax.axis_index("c")
        me = (x, y, c)
        peers = [(x, y, 1 - c), (1 - x, y, c), (x, 1 - y, c), (1 - x, 1 - y, c),
                 (1 - x, y, 1 - c), (x, 1 - y, 1 - c), (1 - x, 1 - y, 1 - c)]

        def src_of(a, dev):
            return ins[a].at[_slot(dev)] if a < nb else ins[a]

        def copy(a, k, frm, to):
            return pltpu.make_async_remote_copy(
                src_ref=src_of(a, to), dst_ref=outs[a].at[_slot(frm)], send_sem=send_sems.at[a, k],
                recv_sem=recv_sems.at[a, k], device_id=to, device_id_type=MESH_ID)

        mine = [pltpu.make_async_copy(src_of(a, me), outs[a].at[_slot(me)], local_sems.at[a]) for a in range(n)]
        for cp in mine:
            cp.start()
        sends = [copy(a, k, me, peer) for a in range(n) for k, peer in enumerate(peers)]
        for cp in sends:
            cp.start()
        for a in range(n):
            for k, peer in enumerate(peers):
                copy(a, k, peer, me).wait_recv()
        for cp in sends:
            cp.wait_send()
        for cp in mine:
            cp.wait()

    return pl.pallas_call(
        body,
        name="exchange_grads",
        out_shape=[jax.ShapeDtypeStruct(s.shape, s.dtype) for s in blocked]
        + [jax.ShapeDtypeStruct((N_DEV,) + s.shape, s.dtype) for s in replicated],
        in_specs=[ANY] * n,
        out_specs=[ANY] * n,
        scratch_shapes=[pltpu.SemaphoreType.DMA((n, 7)), pltpu.SemaphoreType.DMA((n, 7)), pltpu.SemaphoreType.DMA((n,))],
    )(*arrays)


ADAMW_BLOCK_ELEMS = 256 * 1024


def _adamw(name, parts, w, m, v, own=None, me=None):
    r, c = w.shape
    tr = _tile(r, max(16, ADAMW_BLOCK_ELEMS // c), 16)

    def body(*refs):
        if own is None:
            p_ref, w_ref, m_ref, v_ref, g_ref, d_ref, nm_ref, nv_ref = refs
            terms = [p_ref[s].astype(F32) for s in range(N_DEV)]
        else:
            me_ref, p_ref, own_ref, w_ref, m_ref, v_ref, g_ref, d_ref, nm_ref, nv_ref = refs
            mine = own_ref[0].astype(F32)
            terms = [jnp.where(me_ref[0] == s, mine, p_ref[s].astype(F32)) for s in range(N_DEV)]
        g = terms[0]
        for s in range(1, N_DEV):
            g = g + terms[s]
        m_new = ADAM_B1 * m_ref[...] + (1.0 - ADAM_B1) * g
        v_new = ADAM_B2 * v_ref[...] + (1.0 - ADAM_B2) * (g * g)
        m_hat = m_new / (1.0 - ADAM_B1 ** ADAM_STEP)
        v_hat = v_new / (1.0 - ADAM_B2 ** ADAM_STEP)
        g_ref[...] = g
        d_ref[...] = -ADAM_LR * (m_hat / (jnp.sqrt(v_hat) + ADAM_EPS) + ADAM_WD * w_ref[...])
        nm_ref[...] = m_new
        nv_ref[...] = v_new

    if own is None:
        blk = pl.BlockSpec((tr, c), lambda i: (i, 0))
        return pl.pallas_call(
            body,
            name="adamw_" + name,
            out_shape=[jax.ShapeDtypeStruct((r, c), F32)] * 4,
            grid=(r // tr,),
            in_specs=[pl.BlockSpec((N_DEV, tr, c), lambda i: (0, i, 0)), blk, blk, blk],
            out_specs=[blk] * 4,
            compiler_params=_params(("parallel",)),
        )(parts, w, m, v)
    blk = pl.BlockSpec((tr, c), lambda i, me_ref: (i, 0))
    own_at = (lambda i, me_ref: (me_ref[0], i, 0)) if own.shape[0] == N_DEV else (lambda i, me_ref: (0, i, 0))
    return pl.pallas_call(
        body,
        name="adamw_" + name,
        out_shape=[jax.ShapeDtypeStruct((r, c), F32)] * 4,
        grid_spec=pltpu.PrefetchScalarGridSpec(
            num_scalar_prefetch=1,
            grid=(r // tr,),
            in_specs=[pl.BlockSpec((N_DEV, tr, c), lambda i, me_ref: (0, i, 0)), pl.BlockSpec((1, tr, c), own_at),
                      blk, blk, blk],
            out_specs=[blk] * 4),
        compiler_params=_params(("parallel",)),
    )(me, parts, own, w, m, v)


HBM_SPEC = pl.BlockSpec(memory_space=pltpu.HBM)
SEM_SPEC = pl.BlockSpec(memory_space=pltpu.SEMAPHORE)
SIDE_EFFECT = pltpu.SideEffectType.DATAFLOW_SIDE_EFFECTING
N_PEERS = N_DEV - 1


def _peers(x, y, c):
    return [(x, y, 1 - c), (1 - x, y, c), (x, 1 - y, c), (1 - x, 1 - y, c),
            (1 - x, y, 1 - c), (x, 1 - y, 1 - c), (1 - x, 1 - y, 1 - c)]


def _split_copy(srcs, lands, blocked, send_sems, recv_sems, a, k, frm, to):
    src = srcs[a].at[_slot(to)] if blocked[a] else srcs[a]
    return pltpu.make_async_remote_copy(
        src_ref=src, dst_ref=lands[a].at[_slot(frm)], send_sem=send_sems.at[a * N_PEERS + k],
        recv_sem=recv_sems.at[a * N_PEERS + k],
        device_id=to, device_id_type=MESH_ID)


def _exchange_start(name, srcs, lands, blocked, after=()):
    n = len(srcs)
    after = list(after)

    def body(*refs):
        src_refs, land_refs = refs[:n], refs[n:2 * n]
        send_sems, recv_sems = refs[2 * n + len(after)], refs[2 * n + len(after) + 1]
        token = refs[-1]
        x, y, c = lax.axis_index("x"), lax.axis_index("y"), lax.axis_index("c")
        for a in range(n):
            for k, peer in enumerate(_peers(x, y, c)):
                _split_copy(src_refs, land_refs, blocked, send_sems, recv_sems, a, k, (x, y, c), peer).start()
        token[...] = jnp.zeros_like(token)

    thru = [pltpu.HBM(s.shape, s.dtype) for s in list(srcs) + list(lands)]
    res = pl.pallas_call(
        body,
        name=name,
        out_shape=(pltpu.SemaphoreType.DMA((n * N_PEERS,)), pltpu.SemaphoreType.DMA((n * N_PEERS,)), *thru,
                   jax.ShapeDtypeStruct((8, LANE), F32)),
        in_specs=[HBM_SPEC] * (2 * n) + [pl.BlockSpec(memory_space=pl.ANY)] * len(after),
        out_specs=(SEM_SPEC, SEM_SPEC, *([HBM_SPEC] * (2 * n)), pl.BlockSpec(memory_space=pltpu.VMEM)),
        input_output_aliases={i: 2 + i for i in range(2 * n)},
        compiler_params=pltpu.CompilerParams(has_side_effects=SIDE_EFFECT),
    )(*[pltpu.with_memory_space_constraint(s, pltpu.HBM) for s in list(srcs) + list(lands)], *after)
    return res[0], res[1], res[2:2 + n], res[2 + n:2 + 2 * n], res[-1]


def _exchange_wait(name, send_sems, recv_sems, srcs, lands, blocked, after):
    n, n_after = len(srcs), len(after)

    def body(*refs):
        src_refs, land_refs = refs[:n], refs[n:2 * n]
        send, recv = refs[2 * n], refs[2 * n + 1]
        x, y, c = lax.axis_index("x"), lax.axis_index("y"), lax.axis_index("c")
        for a in range(n):
            for k, peer in enumerate(_peers(x, y, c)):
                _split_copy(src_refs, land_refs, blocked, send, recv, a, k, (x, y, c), peer).wait_send()
                _split_copy(src_refs, land_refs, blocked, send, recv, a, k, peer, (x, y, c)).wait_recv()

    res = pl.pallas_call(
        body,
        name=name,
        out_shape=tuple(pltpu.HBM(s.shape, s.dtype) for s in list(srcs) + list(lands)),
        in_specs=[HBM_SPEC] * (2 * n) + [SEM_SPEC, SEM_SPEC] + [pl.BlockSpec(memory_space=pl.ANY)] * n_after,
        out_specs=tuple([HBM_SPEC] * (2 * n)),
        input_output_aliases={i: i for i in range(2 * n)},
        compiler_params=pltpu.CompilerParams(has_side_effects=SIDE_EFFECT),
    )(*srcs, *lands, send_sems, recv_sems, *after)
    return res[:n], res[n:]


class _LazyWeights:
    def __init__(self):
        self.ready, self.groups, self.hints = {}, {}, {}

    def add_group(self, wait_name, names, send, recv, srcs, lands):
        for n in names:
            self.groups[n] = (wait_name, names, send, recv, srcs, lands)

    def hint(self, name, after):
        self.hints[self.groups[name][0]] = after

    def __getitem__(self, name):
        if name not in self.ready:
            wait_name, names, send, recv, srcs, lands = self.groups[name]
            after = [self.hints[wait_name]] if wait_name in self.hints else []
            _, whole = _exchange_wait(wait_name, send, recv, srcs, lands, [False] * len(names), after)
            for n, stacked in zip(names, whole):
                self.ready[n] = _to_kernel_layout({n: _unshard(n, stacked)})[n]
        return self.ready[name]


def kernel(x, positions, meta_tokens, w_in, w_q_up, w_kv_up, w_branch_mla, w_branch_hgrn, w_out, w_ffn_in, w_ffn_out, conv_w, conv_b, g_mix_norm, g_q_norm, g_kv_norm, g_hgrn_norm, g_ffn_norm, g_final_norm, lb_raw, loss_target, m_meta_tokens, m_w_in, m_w_q_up, m_w_kv_up, m_w_branch_mla, m_w_branch_hgrn, m_w_out, m_w_ffn_in, m_w_ffn_out, m_conv_w, m_conv_b, m_g_mix_norm, m_g_q_norm, m_g_kv_norm, m_g_hgrn_norm, m_g_ffn_norm, m_g_final_norm, m_lb_raw, v_meta_tokens, v_w_in, v_w_q_up, v_w_kv_up, v_w_branch_mla, v_w_branch_hgrn, v_w_out, v_w_ffn_in, v_w_ffn_out, v_conv_w, v_conv_b, v_g_mix_norm, v_g_q_norm, v_g_kv_norm, v_g_hgrn_norm, v_g_ffn_norm, v_g_final_norm, v_lb_raw):
    local = dict(zip(
        ("meta_tokens", "w_in", "w_q_up", "w_kv_up", "w_branch_mla", "w_branch_hgrn", "w_out", "w_ffn_in", "w_ffn_out",
         "conv_w", "conv_b", "g_mix_norm", "g_q_norm", "g_kv_norm", "g_hgrn_norm", "g_ffn_norm", "g_final_norm", "lb_raw"),
        (meta_tokens, w_in, w_q_up, w_kv_up, w_branch_mla, w_branch_hgrn, w_out, w_ffn_in, w_ffn_out,
         conv_w, conv_b, g_mix_norm, g_q_norm, g_kv_norm, g_hgrn_norm, g_ffn_norm, g_final_norm, lb_raw)))
    mom_m = dict(zip(local, (m_meta_tokens, m_w_in, m_w_q_up, m_w_kv_up, m_w_branch_mla, m_w_branch_hgrn, m_w_out, m_w_ffn_in,
                             m_w_ffn_out, m_conv_w, m_conv_b, m_g_mix_norm, m_g_q_norm, m_g_kv_norm, m_g_hgrn_norm,
                             m_g_ffn_norm, m_g_final_norm, m_lb_raw)))
    mom_v = dict(zip(local, (v_meta_tokens, v_w_in, v_w_q_up, v_w_kv_up, v_w_branch_mla, v_w_branch_hgrn, v_w_out, v_w_ffn_in,
                             v_w_ffn_out, v_conv_w, v_conv_b, v_g_mix_norm, v_g_q_norm, v_g_kv_norm, v_g_hgrn_norm,
                             v_g_ffn_norm, v_g_final_norm, v_lb_raw)))
    sharded = BIG + ("conv_w", "meta_tokens")

    def shard2d(name, arr):
        return arr.reshape(arr.shape[-2:]) if name != "meta_tokens" else arr

    def as2d(name, arr):
        return arr.reshape(1, -1) if arr.ndim == 1 else shard2d(name, arr)

    me = 4 * lax.axis_index("x") + 2 * lax.axis_index("y") + lax.axis_index("c")

    def landing(own):
        zone = lax.empty((N_DEV,) + own.shape[1:], own.dtype)
        return lax.dynamic_update_slice_in_dim(zone, own, me, 0)

    shards = {n: shard2d(n, local[n]).astype(BF16) for n in BIG}
    shards.update({n: shard2d(n, local[n]) for n in ("conv_w", "meta_tokens")})
    full = _LazyWeights()
    first = ("w_in", "meta_tokens")
    gathered = _all_gather([shards[n] for n in first])
    for n, g in zip(first, gathered):
        full.ready[n] = _to_kernel_layout({n: _unshard(n, g)})[n]
    later = (("w_q_up", "w_kv_up"), ("w_branch_mla", "w_branch_hgrn", "w_out", "w_ffn_in", "w_ffn_out", "conv_w"))
    for k, names in enumerate(later):
        srcs = [shards[n] for n in names]
        send, recv, srcs_thru, lands_thru, _ = _exchange_start(
            f"gather_start_{k}", srcs, [landing(s[None]) for s in srcs], [False] * len(names), after=[gathered[0]])
        full.add_group(f"gather_wait_{k}", names, send, recv, srcs_thru, lands_thru)
    small = {n: local[n] for n in SMALL}

    started = []

    def sources(group):
        group = _from_kernel_layout(group)
        names = list(group)
        blocked = [n in sharded for n in names]
        srcs = [_reshard(n, group[n]) if b else as2d(n, group[n]) for n, b in zip(names, blocked)]
        return names, blocked, srcs

    def emit(group):
        names, blocked, srcs = sources(group)
        lands = [lax.empty((N_DEV,) + (s.shape[1:] if b else s.shape), s.dtype) for s, b in zip(srcs, blocked)]
        k = len(started)
        send, recv, srcs_thru, lands_thru, token = _exchange_start(f"exchange_start_{k}", srcs, lands, blocked)
        started.append((names, blocked, send, recv, srcs_thru, lands_thru))
        return token

    loss, grad_x, last = _local_step(x[0], positions[0], loss_target[0], full, small, emit)

    out = {}

    me_arr = me.astype(jnp.int32).reshape(1)

    def update(names, parts, owns=None):
        for k, (n, part) in enumerate(zip(names, parts)):
            own = None if owns is None else (owns[k] if owns[k].ndim == 3 else owns[k][None])
            res = _adamw(n, part, as2d(n, local[n]), as2d(n, mom_m[n]), as2d(n, mom_v[n]), own,
                         None if owns is None else me_arr)
            out[n] = [r.reshape(local[n].shape) for r in res]

    after = [last["g_mix_norm"]]
    for k, (names, blocked, send, recv, srcs_thru, lands_thru) in enumerate(started):
        srcs_done, parts = _exchange_wait(f"exchange_wait_{k}", send, recv, srcs_thru, lands_thru, blocked, after)
        update(names, parts, srcs_done)
        after = [out[names[0]][0]]
    names, blocked, srcs = sources(last)
    in_blocks = [(n, s) for n, s, b in zip(names, srcs, blocked) if b]
    whole = [(n, s) for n, s, b in zip(names, srcs, blocked) if not b]
    update([n for n, _ in in_blocks + whole], _exchange([s for _, s in in_blocks], [s for _, s in whole]))

    loss = lax.psum(loss, ("x", "y", "c"))
    order = tuple(local)
    return (loss, grad_x[None], *[out[n][0] for n in order], *[out[n][1] for n in order],
            *[out[n][2] for n in order], *[out[n][3] for n in order])
```

```python
import functools

import jax
import jax.numpy as jnp
import numpy as np
from jax import lax
from jax.experimental import pallas as pl
from jax.experimental.pallas import tpu as pltpu

F32 = jnp.float32
BF16 = jnp.bfloat16

D_MODEL = 2048
N_META = 16
BLOCK = 128
PAD_LEN = BLOCK - N_META
HEADS = 16
Q_LORA = 1536
KV_LORA = 512
ROPE = 64
NOPE = 128
VDIM = 128
D_FF = 5632
NORM_EPS = 1e-6
ROPE_THETA = 10000.0
ATTN_SCALE = (NOPE + ROPE) ** -0.5
ADAM_LR = 0.001
ADAM_B1 = 0.9
ADAM_B2 = 0.999
ADAM_EPS = 1e-08
ADAM_WD = 0.01
ADAM_STEP = 10
N_DEV = 8

LANE = 128
SEG_Q_LAT = 0
SEG_KV_LAT = Q_LORA
SEG_HQ = 2048
SEG_HF = SEG_HQ + D_MODEL
SEG_HI = SEG_HF + D_MODEL
SEG_HG = SEG_HI + D_MODEL
SEG_GA = SEG_HG + D_MODEL
SEG_GB = SEG_GA + D_MODEL
SEG_KR = SEG_GB + D_MODEL
KR_W = 256
PROJ_W = SEG_KR + KR_W
QHEAD_W = 256

V7X_VMEM_BYTES = 64 * 1024 * 1024
VMEM_LIMIT = V7X_VMEM_BYTES * 7 // 8
NEG_BIG = -1e30
SUB = 8


def _tile(n, target, mult):
    best = None
    for t in range(mult, min(n, target) + 1, mult):
        if n % t == 0:
            best = t
    return n if best is None else best


def _params(sem):
    return pltpu.CompilerParams(dimension_semantics=sem, vmem_limit_bytes=VMEM_LIMIT)


def _sigmoid(x):
    return 0.5 * jnp.tanh(0.5 * x) + 0.5


MATMUL_WINDOW_BYTES = 40 * 1024 * 1024
_DIMS = {"nn":(((1,), (0,)), ((), ())), "nt": (((1,), (1,)), ((), ())), "tn": (((0,), (0,)), ((), ()))}


def _matmul(a, b, mode, out_dtype, name, after=None, col_blocks=1):
    if mode == "nn":
        (m, k), (_, n) = a.shape, b.shape
    elif mode == "nt":
        (m, k), (n, _) = a.shape, b.shape
    else:
        (k, m), (_, n) = a.shape, b.shape
    tm = _tile(m, 1040, 8) if mode != "tn" else _tile(m, 1024, LANE)
    tn = _tile(n, 1024, LANE) if col_blocks == 1 else n // col_blocks
    if mode != "tn":
        tk = _tile(k, 2816, LANE)
    else:
        for tm in (_tile(m, 1024, LANE), _tile(m, 512, LANE)):
            out_bytes = 2 * tm * tn * jnp.dtype(out_dtype).itemsize + 4 * tm * tn
            tk = _tile(k, max(8, (MATMUL_WINDOW_BYTES - out_bytes) // (4 * (tm + tn))), 8)
            if 2 * tk >= k:
                break
    nk = k // tk
    if mode == "nn":
        a_spec = pl.BlockSpec((tm, tk), lambda i, j, kk: (i, kk))
        b_spec = pl.BlockSpec((tk, tn), lambda i, j, kk: (kk, j))
    elif mode == "nt":
        a_spec = pl.BlockSpec((tm, tk), lambda i, j, kk: (i, kk))
        b_spec = pl.BlockSpec((tn, tk), lambda i, j, kk: (j, kk))
    else:
        a_spec = pl.BlockSpec((tk, tm), lambda i, j, kk: (kk, i))
        b_spec = pl.BlockSpec((tk, tn), lambda i, j, kk: (kk, j))
    dims = _DIMS[mode]

    n_after = 0 if after is None else 1

    def body(a_ref, b_ref, *rest):
        o_ref, acc = rest[n_after], rest[n_after + 1:]
        part = lax.dot_general(a_ref[...], b_ref[...], dims, preferred_element_type=F32)
        if nk == 1:
            o_ref[...] = part.astype(o_ref.dtype)
            return
        acc_ref, kk = acc[0], pl.program_id(2)

        @pl.when(kk == 0)
        def _():
            acc_ref[...] = part

        @pl.when((kk > 0) & (kk < nk - 1))
        def _():
            acc_ref[...] += part

        @pl.when(kk == nk - 1)
        def _():
            o_ref[...] = (acc_ref[...] + part).astype(o_ref.dtype)

    if col_blocks == 1:
        out_shape = jax.ShapeDtypeStruct((m, n), out_dtype)
        out_spec = pl.BlockSpec((tm, tn), lambda i, j, kk: (i, j))
    else:
        out_shape = jax.ShapeDtypeStruct((col_blocks, m, tn), out_dtype)
        out_spec = pl.BlockSpec((None, tm, tn), lambda i, j, kk: (j, i, 0))
    return pl.pallas_call(
        body,
        name=name,
        out_shape=out_shape,
        grid=(m // tm, n // tn, nk),
        in_specs=[a_spec, b_spec] + [pl.BlockSpec(memory_space=pl.ANY)] * n_after,
        out_specs=out_spec,
        scratch_shapes=[pltpu.VMEM((tm, tn), F32)] if nk > 1 else [],
        compiler_params=_params(("parallel", "parallel", "arbitrary")),
    )(a, b, *([after] * n_after))


ROW_WINDOW_BYTES = 12 * 1024 * 1024


def _rowwise(name, fn, ins, outs, rows, tm, after=None):
    per_row = sum(s[2] * s[1].dtype.itemsize for s in ins if s[0] == "row")
    per_row += sum(s[1] * jnp.dtype(s[2]).itemsize for s in outs if s[0] in ("row", "tail"))
    if per_row:
        tm = _tile(rows, min(tm, max(8, ROW_WINDOW_BYTES // (2 * per_row))), 8)
    n_in = len(ins)
    in_specs, args = [], []
    for spec in ins:
        if spec[0] == "row":
            _, arr, w, cb = spec
            in_specs.append(pl.BlockSpec((tm, w), functools.partial(lambda i, cb: (i, cb), cb=cb)))
        else:
            arr = spec[1]
            in_specs.append(pl.BlockSpec(arr.shape, lambda i: (0, 0)))
        args.append(arr)
    out_shape, out_specs = [], []
    for spec in outs:
        if spec[0] == "row":
            out_shape.append(jax.ShapeDtypeStruct((rows, spec[1]), spec[2]))
            out_specs.append(pl.BlockSpec((tm, spec[1]), lambda i: (i, 0)))
        elif spec[0] == "tail":
            out_shape.append(jax.ShapeDtypeStruct((rows - tm, spec[1]), spec[2]))
            out_specs.append(pl.BlockSpec((tm, spec[1]), lambda i: (jnp.maximum(i - 1, 0), 0)))
        elif spec[0] == "head":
            out_shape.append(jax.ShapeDtypeStruct((tm, spec[1]), spec[2]))
            out_specs.append(pl.BlockSpec((tm, spec[1]), lambda i: (0, 0)))
        else:
            out_shape.append(jax.ShapeDtypeStruct(spec[1], F32))
            out_specs.append(pl.BlockSpec(spec[1], lambda i: (0, 0)))
    has_acc = any(s[0] != "row" for s in outs)
    n_after = 0 if after is None else 1
    in_specs += [pl.BlockSpec(memory_space=pl.ANY)] * n_after
    args += [after] * n_after

    def body(*refs):
        i = pl.program_id(0)
        res = fn(i, tm, *[r[...] for r in refs[:n_in]])
        for spec, ref, val in zip(outs, refs[n_in + n_after:], res):
            if spec[0] in ("row", "tail"):
                ref[...] = val.astype(ref.dtype)
            elif spec[0] == "head":
                @pl.when(i == 0)
                def _(ref=ref, val=val):
                    ref[...] = val.astype(ref.dtype)
            else:
                @pl.when(i == 0)
                def _(ref=ref, val=val):
                    ref[...] = val

                @pl.when(i > 0)
                def _(ref=ref, val=val):
                    ref[...] += val

    return pl.pallas_call(
        body,
        name=name,
        out_shape=out_shape,
        grid=(rows // tm,),
        in_specs=in_specs,
        out_specs=out_specs,
        compiler_params=_params(("arbitrary" if has_acc else "parallel",)),
    )(*args)


def _rms_fwd(x, g):
    r = lax.rsqrt(jnp.mean(x * x, axis=-1, keepdims=True) + NORM_EPS)
    return x * r * g


def _rms_bwd(x, g, dy):
    r = lax.rsqrt(jnp.mean(x * x, axis=-1, keepdims=True) + NORM_EPS)
    xhat = x * r
    dxhat = dy * g
    dx = r * (dxhat - xhat * jnp.mean(dxhat * xhat, axis=-1, keepdims=True))
    return dx, jnp.sum(dy * xhat, axis=0, keepdims=True)


def _silu(x):
    return x * _sigmoid(x)


def _dsilu(x):
    s = _sigmoid(x)
    return s * (1.0 + x * (1.0 - s))


def _silu_both(x):
    s = _sigmoid(x)
    return x * s, s * (1.0 + x * (1.0 - s))


def _rot_src(x):
    lane = lax.broadcasted_iota(jnp.int32, x.shape, 1)
    return jnp.where(lane < ROPE // 2, pltpu.roll(x, LANE - ROPE // 2, 1), pltpu.roll(x, ROPE // 2, 1))


def _rope_fwd_call(q_raw, kv, proj, cos_t, sin_t, rows, tm):
    def fn(i, tm_, q, kvv, kr, c, s):
        kr_rot = kr[:, :LANE]
        kr_rot = kr_rot * c + _rot_src(kr_rot) * s
        qs, ks, vs = [], [], []
        for h in range(HEADS):
            qn = q[:, h * QHEAD_W:h * QHEAD_W + NOPE]
            qr = q[:, h * QHEAD_W + NOPE:(h + 1) * QHEAD_W]
            qs += [qn * SCORE_TO_LOG2, (qr * c + _rot_src(qr) * s) * SCORE_TO_LOG2]
            ks += [kvv[:, h * 2 * NOPE:h * 2 * NOPE + NOPE], kr_rot]
            vs += [kvv[:, h * 2 * NOPE + NOPE:(h + 1) * 2 * NOPE]]
        return jnp.concatenate(qs, axis=1), jnp.concatenate(ks, axis=1), jnp.concatenate(vs, axis=1)

    return _rowwise(
        "rope_fwd", fn,
        [("row", q_raw, HEADS * QHEAD_W, 0), ("row", kv, HEADS * 2 * NOPE, 0), ("row", proj, KR_W, SEG_KR // KR_W),
         ("row", cos_t, LANE, 0), ("row", sin_t, LANE, 0)],
        [("row", HEADS * QHEAD_W, BF16), ("row", HEADS * QHEAD_W, BF16), ("row", HEADS * VDIM, BF16)],
        rows, tm)


def _rope_bwd_call(dq_att, dk_att, dv, cos_t, sin_t, rows, tm):
    def fn(i, tm_, dq, dk, dvv, c, s):
        qs, kvs = [], []
        dkr = jnp.zeros((dq.shape[0], LANE), F32)
        for h in range(HEADS):
            dqr = dq[:, h * QHEAD_W + NOPE:(h + 1) * QHEAD_W] * ATTN_SCALE
            qs += [dq[:, h * QHEAD_W:h * QHEAD_W + NOPE] * ATTN_SCALE, dqr * c - _rot_src(dqr) * s]
            kvs += [dk[:, h * QHEAD_W:h * QHEAD_W + NOPE], dvv[:, h * VDIM:(h + 1) * VDIM]]
            dkr = dkr + dk[:, h * QHEAD_W + NOPE:(h + 1) * QHEAD_W]
        dkr = dkr * c - _rot_src(dkr) * s
        return (jnp.concatenate(qs, axis=1), jnp.concatenate(kvs, axis=1),
                jnp.concatenate([dkr, jnp.zeros_like(dkr)], axis=1))

    return _rowwise(
        "rope_bwd", fn,
        [("row", dq_att, HEADS * QHEAD_W, 0), ("row", dk_att, HEADS * QHEAD_W, 0), ("row", dv, HEADS * VDIM, 0),
         ("row", cos_t, LANE, 0), ("row", sin_t, LANE, 0)],
        [("row", HEADS * QHEAD_W, BF16), ("row", HEADS * 2 * NOPE, BF16), ("row", KR_W, BF16)],
        rows, tm)


def _attn_mask(q_blk, k_blk, t, keys_on_rows=False):
    qa, ka = (1, 0) if keys_on_rows else (0, 1)
    qs = q_blk * t + lax.broadcasted_iota(jnp.int32, (t, t), qa)
    ks = k_blk * t + lax.broadcasted_iota(jnp.int32, (t, t), ka)
    return (ks <= qs) & ((ks >= PAD_LEN) | (ks == qs))


_NT = _DIMS["nt"]
_TN = _DIMS["tn"]
LOG2E = 1.4426950408889634
SCORE_TO_LOG2 = ATTN_SCALE * LOG2E


def _causal_pairs(nb, by_key):
    if by_key:
        pairs = [(qi, kj) for kj in range(nb) for qi in range(kj, nb)]
    else:
        pairs = [(qi, kj) for qi in range(nb) for kj in range(qi + 1)]
    return (jnp.asarray(np.array([p[0] for p in pairs], np.int32)), jnp.asarray(np.array([p[1] for p in pairs], np.int32)))


def _two_parts(t):
    cut = (t // LANE + 1) // 2 * LANE
    return ((0, cut), (cut, t)) if cut < t else ((0, t),)


def _attn_fwd(q_att, k_att, v, rows):
    t = _tile(rows, 640, LANE)
    nb = rows // t

    def body(qt_ref, kt_ref, q_ref, k_ref, v_ref, o32_ref, obf_ref, lse_ref, m_sc, l_sc, acc_sc):
        qi, kj = qt_ref[pl.program_id(1)], kt_ref[pl.program_id(1)]

        @pl.when(kj == 0)
        def _():
            m_sc[...] = jnp.full_like(m_sc, NEG_BIG)
            l_sc[...] = jnp.zeros_like(l_sc)
            acc_sc[...] = jnp.zeros_like(acc_sc)

        def step(masked):
            q = q_ref[...]
            parts = _two_parts(t)
            scores =[lax.dot_general(q, k_ref[lo:hi, :], _NT, preferred_element_type=F32) for lo, hi in parts]
            m, l, acc = m_sc[...], l_sc[...], acc_sc[...]
            for (lo, hi), s in zip(parts, scores):
                if masked:
                    qs = qi * t + lax.broadcasted_iota(jnp.int32, (t, hi - lo), 0)
                    ks = kj * t + lo + lax.broadcasted_iota(jnp.int32, (t, hi - lo), 1)
                    s = jnp.where((ks <= qs) & ((ks >= PAD_LEN) | (ks == qs)), s, NEG_BIG)
                m_new = jnp.maximum(m, jnp.max(s, axis=1, keepdims=True))
                alpha = jnp.exp2(m - m_new)
                p = jnp.exp2(s - jnp.tile(m_new, (1, (hi - lo) // LANE)))
                l = alpha * l + jnp.sum(p, axis=1, keepdims=True)
                acc = alpha * acc + jnp.dot(p.astype(BF16), v_ref[lo:hi, :], preferred_element_type=F32)
                m = m_new
            m_sc[...], l_sc[...], acc_sc[...] = m, l, acc

        pl.when((kj == qi) | (kj == 0))(functools.partial(step, True))
        pl.when((kj < qi) & (kj > 0))(functools.partial(step, False))

        @pl.when(kj == qi)
        def _():
            o = acc_sc[...] / l_sc[...]
            o32_ref[...] = o
            obf_ref[...] = o.astype(BF16)
            lse_ref[0] = jnp.max((m_sc[...] + jnp.log2(l_sc[...])).T, axis=0, keepdims=True)

    qt, kt = _causal_pairs(nb, by_key=False)
    qmap = lambda h, p, qt_ref, kt_ref: (qt_ref[p], h)
    kmap = lambda h, p, qt_ref, kt_ref: (kt_ref[p], h)
    return pl.pallas_call(
        body,
        name="attn_fwd",
        out_shape=[jax.ShapeDtypeStruct((rows, HEADS * VDIM), F32), jax.ShapeDtypeStruct((rows, HEADS * VDIM), BF16),
                   jax.ShapeDtypeStruct((HEADS, 1, rows), F32)],
        grid_spec=pltpu.PrefetchScalarGridSpec(
            num_scalar_prefetch=2,
            grid=(HEADS, len(qt)),
            in_specs=[pl.BlockSpec((t, QHEAD_W), qmap), pl.BlockSpec((t, QHEAD_W), kmap), pl.BlockSpec((t, VDIM), kmap)],
            out_specs=[pl.BlockSpec((t, VDIM), qmap), pl.BlockSpec((t, VDIM), qmap),
                       pl.BlockSpec((1, 1, t), lambda h, p, qt_ref, kt_ref: (h, 0, qt_ref[p]))],
            scratch_shapes=[pltpu.VMEM((t, LANE), F32), pltpu.VMEM((t, LANE), F32), pltpu.VMEM((t, VDIM), F32)]),
        compiler_params=_params(("parallel", "arbitrary")),
    )(qt, kt, q_att, k_att, v)


def _attn_delta(do, o32, rows, tm):
    def fn(i, tm_, dov, ov):
        prod = dov.astype(F32) * ov
        head_of = lax.broadcasted_iota(jnp.int32, (HEADS * VDIM, LANE), 0) // VDIM
        pick = jnp.where(head_of == lax.broadcasted_iota(jnp.int32, (HEADS * VDIM, LANE), 1), 1.0, 0.0).astype(F32)
        return (jnp.dot(prod, pick, precision=lax.Precision.HIGHEST, preferred_element_type=F32),)

    (delta,) = _rowwise("attn_delta", fn, [("row", do, HEADS * VDIM, 0), ("row", o32, HEADS * VDIM, 0)],
                        [("row", LANE, F32)], rows, tm)
    return delta


def _attn_bwd(q_att, k_att, v, do, lse_row, delta_row, rows):
    t = _tile(rows, 640, LANE)
    nb = rows // t

    def body(qt_ref, kt_ref, q_ref, k_ref, v_ref, do_ref, lse_ref, delta_ref, dq_ref, dk_ref, dv_ref, dk_sc, dv_sc):
        qi, kj = qt_ref[pl.program_id(1)], kt_ref[pl.program_id(1)]

        @pl.when(pl.program_id(1) == 0)
        def _():
            dq_ref[...] = jnp.zeros_like(dq_ref)

        @pl.when(qi == kj)
        def _():
            dk_sc[...] = jnp.zeros_like(dk_sc)
            dv_sc[...] = jnp.zeros_like(dv_sc)

        def step(masked):
            k, vv = k_ref[...], v_ref[...]
            parts = _two_parts(t)
            st_all = [lax.dot_general(k, q_ref[lo:hi, :], _NT, preferred_element_type=F32) for lo, hi in parts]
            dpt_all = [lax.dot_general(vv, do_ref[lo:hi, :], _NT, preferred_element_type=F32) for lo, hi in parts]
            dk, dv = dk_sc[...], dv_sc[...]
            for (lo, hi), st, dpt in zip(parts, st_all, dpt_all):
                pt = jnp.exp2(st - lse_ref[0, :, lo:hi])
                if masked:
                    ks = kj * t + lax.broadcasted_iota(jnp.int32, (t, hi - lo), 0)
                    qs = qi * t + lo + lax.broadcasted_iota(jnp.int32, (t, hi - lo), 1)
                    pt = jnp.where((ks <= qs) & ((ks >= PAD_LEN) | (ks == qs)), pt, 0.0)
                dv = dv + jnp.dot(pt.astype(BF16), do_ref[lo:hi, :], preferred_element_type=F32)
                dst = (pt * (dpt - delta_ref[0, :, lo:hi])).astype(BF16)
                dk = dk + jnp.dot(dst, q_ref[lo:hi, :], preferred_element_type=F32)
                q_rows = pl.ds(pl.multiple_of(qi * t + lo, LANE), hi - lo)
                dq_ref[q_rows, :] += lax.dot_general(dst, k, _TN, preferred_element_type=F32)
            dk_sc[...], dv_sc[...] = dk, dv

        pl.when((qi == kj) | (kj == 0))(functools.partial(step, True))
        pl.when((qi > kj) & (kj > 0))(functools.partial(step, False))

        @pl.when(qi == nb - 1)
        def _():
            dk_ref[...] = dk_sc[...] * (1.0 / LOG2E)
            dv_ref[...] = dv_sc[...]

    qt, kt = _causal_pairs(nb, by_key=True)
    qmap = lambda h, p, qt_ref, kt_ref: (qt_ref[p], h)
    kmap = lambda h, p, qt_ref, kt_ref: (kt_ref[p], h)
    stat = pl.BlockSpec((1, 1, t), lambda h, p, qt_ref, kt_ref: (h, 0, qt_ref[p]))
    return pl.pallas_call(
        body,
        name="attn_bwd",
        out_shape=[jax.ShapeDtypeStruct((rows, HEADS * QHEAD_W), F32), jax.ShapeDtypeStruct((rows, HEADS * QHEAD_W), F32),
                   jax.ShapeDtypeStruct((rows, HEADS * VDIM), F32)],
        grid_spec=pltpu.PrefetchScalarGridSpec(
            num_scalar_prefetch=2,
            grid=(HEADS, len(qt)),
            in_specs=[pl.BlockSpec((t, QHEAD_W), qmap), pl.BlockSpec((t, QHEAD_W), kmap), pl.BlockSpec((t, VDIM), kmap),
                      pl.BlockSpec((t, VDIM), qmap), stat, stat],
            out_specs=[pl.BlockSpec((rows, QHEAD_W), lambda h, p, qt_ref, kt_ref: (0, h)),
                       pl.BlockSpec((t, QHEAD_W), kmap), pl.BlockSpec((t, VDIM), kmap)],
            scratch_shapes=[pltpu.VMEM((t, QHEAD_W), F32), pltpu.VMEM((t, VDIM), F32)]),
        compiler_params=_params(("parallel", "arbitrary")),
    )(qt, kt, q_att, k_att, v, do, lse_row, delta_row)


C = BLOCK


def _hgrn_prep(hq, hf, hi, lb, c):
    rows = c * C + lax.broadcasted_iota(jnp.int32, (C, C), 0)
    valid = rows >= PAD_LEN
    sg = _sigmoid(hf)
    f = lb + (1.0 - lb) * sg
    g = jnp.where(valid, jnp.log(f), 0.0)
    k = jnp.where(valid, 1.0 - f, 0.0)
    q = _silu(hq)
    r = lax.broadcasted_iota(jnp.int32, (C, C), 0)
    cc = lax.broadcasted_iota(jnp.int32, (C, C), 1)
    tri = jnp.where(cc <= r, 1.0, 0.0).astype(F32)
    b = jnp.dot(tri, g, precision=lax.Precision.HIGHEST, preferred_element_type=F32)
    return q, k, hi, b, f, sg, valid


def _last_row_as_col(b_t):
    lane = lax.broadcasted_iota(jnp.int32, b_t.shape, 1)
    return jnp.sum(jnp.where(lane == C - 1, b_t, 0.0), axis=1, keepdims=True)


def _k_scaled(k, b, bs):
    return (k * jnp.exp(jnp.minimum(bs - b, 0.0))).astype(BF16)


def _hgrn_fwd(proj, lb, rows):
    nc = rows // C

    def body(hq_ref, hf_ref, hi_ref, lb_ref, o_ref, a_ref, s_ref, s_sc, b_sc):
        c = pl.program_id(1)

        @pl.when(c == 0)
        def _():
            s_sc[...] = jnp.zeros_like(s_sc)

        q, k, v, b, _, _, _ = _hgrn_prep(hq_ref[...], hf_ref[...], hi_ref[...], lb_ref[...], c)
        b_sc[...] = b
        s0 = s_sc[...]
        s_ref[0, 0] = s0
        v_bf = v.astype(BF16)
        r16 = lax.broadcasted_iota(jnp.int32, (SUB, C), 0)
        c16 = lax.broadcasted_iota(jnp.int32, (SUB, C), 1)
        slabs = [jnp.zeros((SUB, C), F32)]
        for i in range(1, C // SUB):
            bs = b_sc[SUB * i - 1:SUB * i, :]
            qs = (q[SUB * i:SUB * (i + 1)] * jnp.exp(b[SUB * i:SUB * (i + 1)] - bs)).astype(BF16)
            a_i = lax.dot_general(qs, _k_scaled(k, b, bs), _NT, preferred_element_type=F32)
            slabs.append(jnp.where(c16 <= r16 + (SUB * i - SUB), a_i, 0.0))
        a_off = jnp.concatenate(slabs, axis=0)
        q_t, k_t, b_t = q.T, k.T, b.T
        sub = lax.broadcasted_iota(jnp.int32, (C, C), 0)
        lane = lax.broadcasted_iota(jnp.int32, (C, C), 1)
        lane1 = lax.broadcasted_iota(jnp.int32, (1, C), 1)
        at_band = jnp.zeros((C, C), F32)
        ahead = lane - sub
        for dl in range(SUB):
            k_s = pltpu.roll(k_t, dl, 1) if dl else k_t
            b_s = pltpu.roll(b_t, dl, 1) if dl else b_t
            e = jnp.exp(b_t - b_s)
            band = jnp.sum(q_t * k_s * e, axis=0, keepdims=True)
            band = jnp.where(lane1 >= dl, band, 0.0)
            at_band = at_band + jnp.where(ahead == dl, jnp.broadcast_to(band, (C, C)), 0.0)
        a = (a_off + at_band.T).astype(BF16)
        a_ref[0] = a
        qe = (q * jnp.exp(b)).astype(BF16)
        o_ref[...] = (jnp.dot(a, v_bf, preferred_element_type=F32)
                      + jnp.dot(qe, s0.astype(BF16), preferred_element_type=F32))
        b_last = b_sc[C - 1:C, :]
        kd = (k * jnp.exp(b_last - b)).astype(BF16)
        s_sc[...] = (jnp.exp(_last_row_as_col(b_t)) * s0
                     + lax.dot_general(kd, v_bf, _TN, preferred_element_type=F32))

    seg = lambda base: (lambda h, c: (c, base // C + h))
    return pl.pallas_call(
        body,
        name="hgrn_fwd",
        out_shape=[jax.ShapeDtypeStruct((rows, D_MODEL), F32), jax.ShapeDtypeStruct((HEADS, rows, C), BF16),
                   jax.ShapeDtypeStruct((HEADS, nc, C, C), F32)],
        grid=(HEADS, nc),
        in_specs=[pl.BlockSpec((C, C), seg(SEG_HQ)), pl.BlockSpec((C, C), seg(SEG_HF)), pl.BlockSpec((C, C), seg(SEG_HI)),
                  pl.BlockSpec((1, C), lambda h, c: (0, h))],
        out_specs=[pl.BlockSpec((C, C), lambda h, c: (c, h)), pl.BlockSpec((1, C, C), lambda h, c: (h, c, 0)),
                   pl.BlockSpec((1, 1, C, C), lambda h, c: (h, c, 0, 0))],
        scratch_shapes=[pltpu.VMEM((C, C), F32), pltpu.VMEM((C, C), F32)],
        compiler_params=_params(("parallel", "arbitrary")),
    )(proj, proj, proj, lb)


def _hgrn_bwd(proj, lb, a_mat, s_states, do_h, rows):
    nc = rows // C

    def body(hq_ref, hf_ref, hi_ref, lb_ref, a_ref, s_ref, do_ref, dhq_ref, dhf_ref, dhi_ref, dlb_ref, ds_sc, b_sc):
        step = pl.program_id(1)
        c = nc - 1 - step

        @pl.when(step == 0)
        def _():
            ds_sc[...] = jnp.zeros_like(ds_sc)
            dlb_ref[...] = jnp.zeros_like(dlb_ref)

        hq, hf = hq_ref[...], hf_ref[...]
        lb_row = lb_ref[...]
        q, k, v, b, f, sg, valid = _hgrn_prep(hq, hf, hi_ref[...], lb_row, c)
        b_sc[...] = b
        s0 = s_ref[0, 0]
        ds1 = ds_sc[...]
        s0_bf, ds1_bf = s0.astype(BF16), ds1.astype(BF16)
        do = do_ref[...]
        do_bf, v_bf = do.astype(BF16), v.astype(BF16)
        b_last = b_sc[C - 1:C, :]
        e_last = jnp.exp(b_last - b)
        eb = jnp.exp(b)
        sub = lax.broadcasted_iota(jnp.int32, (C, C), 0)
        lane = lax.broadcasted_iota(jnp.int32, (C, C), 1)
        r16 = lax.broadcasted_iota(jnp.int32, (SUB, C), 0)
        c16 = lax.broadcasted_iota(jnp.int32, (SUB, C), 1)

        dv = (lax.dot_general(a_ref[0], do_bf, _TN, preferred_element_type=F32)
              + jnp.dot((k * e_last).astype(BF16), ds1_bf, preferred_element_type=F32))
        da = jnp.where(lane <= sub, lax.dot_general(do_bf, v_bf, _NT, preferred_element_type=F32), 0.0)
        da_t = jnp.where(sub <= lane, lax.dot_general(v_bf, do_bf, _NT, preferred_element_type=F32), 0.0)

        dq_slabs = [jnp.zeros((SUB, C), F32)]
        for i in range(1, C // SUB):
            bs = b_sc[SUB * i - 1:SUB * i, :]
            da_i = jnp.where(c16 <= r16 + (SUB * i - SUB), da[SUB * i:SUB * (i + 1)], 0.0).astype(BF16)
            dq_slabs.append(jnp.exp(b[SUB * i:SUB * (i + 1)] - bs)
                            * jnp.dot(da_i, _k_scaled(k, b, bs), preferred_element_type=F32))
        dk_slabs = []
        for j in range(C // SUB - 1):
            be = b_sc[SUB * j + SUB - 1:SUB * (j + 1), :]
            qe_j = (q * jnp.exp(jnp.minimum(b - be, 0.0))).astype(BF16)
            da_j = jnp.where(c16 >= r16 + (SUB * j + SUB), da_t[SUB * j:SUB * (j + 1)], 0.0).astype(BF16)
            dk_slabs.append(jnp.exp(be - b[SUB * j:SUB * (j + 1)]) * jnp.dot(da_j, qe_j, preferred_element_type=F32))
        dk_slabs.append(jnp.zeros((SUB, C), F32))

        q_t, k_t, b_t = q.T, k.T, b.T
        lane1 = lax.broadcasted_iota(jnp.int32, (1, C), 1)
        dq_t = jnp.zeros((C, C), F32)
        dk_t = jnp.zeros((C, C), F32)
        ahead = lane - sub
        for dl in range(SUB):
            k_s = pltpu.roll(k_t, dl, 1) if dl else k_t
            b_s = pltpu.roll(b_t, dl, 1) if dl else b_t
            e = jnp.exp(jnp.minimum(b_t - b_s, 0.0))
            dband = jnp.sum(jnp.where(ahead == dl, da_t, 0.0), axis=0, keepdims=True)
            w = jnp.where(lane1 >= dl, dband, 0.0) * e
            dq_t = dq_t + w * k_s
            back = w * q_t
            dk_t = dk_t + (pltpu.roll(back, C - dl, 1) if dl else back)

        dq = eb * lax.dot_general(do_bf, s0_bf, _NT, preferred_element_type=F32) + jnp.concatenate(dq_slabs, axis=0) + dq_t.T
        dk_inter = e_last * lax.dot_general(v_bf, ds1_bf, _NT, preferred_element_type=F32)
        dk = dk_inter + jnp.concatenate(dk_slabs, axis=0) + dk_t.T

        extra = (jnp.exp(b_last) * jnp.sum((s0 * ds1).T, axis=0, keepdims=True)
                 + jnp.sum(k * dk_inter, axis=0, keepdims=True))
        db = q * dq - k * dk + jnp.where(sub == C - 1, jnp.broadcast_to(extra, (C, C)), 0.0)
        tri_t = jnp.where(lane >= sub, 1.0, 0.0).astype(F32)
        dg = jnp.dot(tri_t, db, precision=lax.Precision.HIGHEST, preferred_element_type=F32)
        ds_sc[...] = (jnp.exp(_last_row_as_col(b_t)) * ds1
                      + lax.dot_general((q * eb).astype(BF16), do_bf, _TN, preferred_element_type=F32))

        df = jnp.where(valid, dg / f - dk, 0.0)
        dhf_ref[...] = (df * (1.0 - lb_row) * sg * (1.0 - sg)).astype(BF16)
        dlb_ref[...] += jnp.sum(df * (1.0 - sg), axis=0, keepdims=True)
        dhq_ref[...] = (dq * _dsilu(hq)).astype(BF16)
        dhi_ref[...] = dv.astype(BF16)

    seg = lambda base: (lambda h, s: (nc - 1 - s, base // C + h))
    rmap = lambda h, s: (nc - 1 - s, h)
    return pl.pallas_call(
        body,
        name="hgrn_bwd",
        out_shape=[jax.ShapeDtypeStruct((rows, D_MODEL), BF16)] * 3 + [jax.ShapeDtypeStruct((1, D_MODEL), F32)],
        grid=(HEADS, nc),
        in_specs=[pl.BlockSpec((C, C), seg(SEG_HQ)), pl.BlockSpec((C, C), seg(SEG_HF)), pl.BlockSpec((C, C), seg(SEG_HI)),
                  pl.BlockSpec((1, C), lambda h, s: (0, h)),
                  pl.BlockSpec((1, C, C), lambda h, s: (h, nc - 1 - s, 0)),
                  pl.BlockSpec((1, 1, C, C), lambda h, s: (h, nc - 1 - s, 0, 0)),
                  pl.BlockSpec((C, C), rmap)],
        out_specs=[pl.BlockSpec((C, C), rmap)] * 3 + [pl.BlockSpec((1, C), lambda h, s: (0, h))],
        scratch_shapes=[pltpu.VMEM((C, C), F32), pltpu.VMEM((C, C), F32)],
        compiler_params=_params(("parallel", "arbitrary")),
    )(proj, proj, proj, lb, a_mat, s_states, do_h)


CONV_TC = 512
HALO = 16


def _halo_row(block, k):
    r = lax.broadcasted_iota(jnp.int32, block.shape, 0)
    return jnp.sum(jnp.where(r == k, block, 0.0), axis=0, keepdims=True)


def _conv_taps(i, tm, g_ref, pg_ref):
    shape = g_ref.shape
    r = lax.broadcasted_iota(jnp.int32, shape, 0)
    g = jnp.where(i * tm + r >= PAD_LEN, g_ref[...].astype(F32), 0.0)
    prev = pg_ref[...].astype(F32)
    p1 = jnp.where(i * tm - 1 >= PAD_LEN, _halo_row(prev, HALO - 1), 0.0)
    p2 = jnp.where(i * tm - 2 >= PAD_LEN, _halo_row(prev, HALO - 2), 0.0)
    s1 = jnp.where(r == 0, p1, pltpu.roll(g, 1, 0))
    s2 = jnp.where(r == 0, p2, jnp.where(r == 1, p1, pltpu.roll(g, 2, 0)))
    return g, s1, s2


def _conv_specs(tm, tc, ncb, order):
    gate = pl.BlockSpec((tm, tc), lambda *ids: order(ids))
    halo = pl.BlockSpec((HALO, tc), lambda *ids: (jnp.maximum(order(ids)[0] * (tm // HALO) - 1, 0), order(ids)[1]))
    up = pl.BlockSpec((tm, tc), lambda *ids: (order(ids)[0], ncb + order(ids)[1]))
    return gate, halo, up


def _conv_fwd(ffn, conv_w, conv_b, rows, tm):
    tc = CONV_TC
    ncb = D_FF // tc

    def body(g_ref, pg_ref, up_ref, cw_ref, cb_ref, act_ref):
        i = pl.program_id(0)
        g, s1, s2 = _conv_taps(i, tm, g_ref, pg_ref)
        conv = (cw_ref[0:1, :] * s2 + cw_ref[1:2, :] * s1 + cw_ref[2:3, :] * g) + cb_ref[...]
        act_ref[...] = (_silu(conv) * up_ref[...].astype(F32)).astype(BF16)

    gate, halo, up = _conv_specs(tm, tc, ncb, lambda ids: (ids[0], ids[1]))
    return pl.pallas_call(
        body,
        name="conv_fwd",
        out_shape=jax.ShapeDtypeStruct((rows, D_FF), BF16),
        grid=(rows // tm, ncb),
        in_specs=[gate, halo, up, pl.BlockSpec((3, tc), lambda i, j: (0, j)), pl.BlockSpec((1, tc), lambda i, j: (0, j))],
        out_specs=pl.BlockSpec((tm, tc), lambda i, j: (i, j)),
        compiler_params=_params(("parallel", "parallel")),
    )(ffn, ffn, ffn, conv_w, conv_b)


def _conv_bwd_a(ffn, dact, conv_w, conv_b, rows, tm):
    tc = CONV_TC
    ncb = D_FF // tc

    def body(g_ref, pg_ref, up_ref, da_ref, cw_ref, cb_ref, dc_ref, dffn_ref, w0_ref, w1_ref, w2_ref, db_ref):
        i = pl.program_id(1)
        g, s1, s2 = _conv_taps(i, tm, g_ref, pg_ref)
        conv = (cw_ref[0:1, :] * s2 + cw_ref[1:2, :] * s1 + cw_ref[2:3, :] * g) + cb_ref[...]
        da = da_ref[...].astype(F32)
        act, dact_dconv = _silu_both(conv)
        dffn_ref[...] = (da * act).astype(BF16)
        dc = da * up_ref[...].astype(F32) * dact_dconv
        dc_ref[...] = dc.astype(BF16)
        sums = [jnp.sum(dc * s2, axis=0, keepdims=True), jnp.sum(dc * s1, axis=0, keepdims=True),
                jnp.sum(dc * g, axis=0, keepdims=True), jnp.sum(dc, axis=0, keepdims=True)]
        for ref, val in zip((w0_ref, w1_ref, w2_ref, db_ref), sums):
            @pl.when(i == 0)
            def _(ref=ref, val=val):
                ref[...] = val

            @pl.when(i > 0)
            def _(ref=ref, val=val):
                ref[...] += val

    gate, halo, up = _conv_specs(tm, tc, ncb, lambda ids: (ids[1], ids[0]))
    col = pl.BlockSpec((1, tc), lambda j, i: (0, j))
    return pl.pallas_call(
        body,
        name="conv_bwd_a",
        out_shape=[jax.ShapeDtypeStruct((rows, D_FF), BF16), jax.ShapeDtypeStruct((rows, 2 * D_FF), BF16)]
        + [jax.ShapeDtypeStruct((1, D_FF), F32)] * 4,
        grid=(ncb, rows // tm),
        in_specs=[gate, halo, up, pl.BlockSpec((tm, tc), lambda j, i: (i, j)),
                  pl.BlockSpec((3, tc), lambda j, i: (0, j)), col],
        out_specs=[pl.BlockSpec((tm, tc), lambda j, i: (i, j)), pl.BlockSpec((tm, tc), lambda j, i: (i, ncb + j)),
                   col, col, col, col],
        compiler_params=_params(("parallel", "arbitrary")),
    )(ffn, ffn, ffn, dact, conv_w, conv_b)


def _conv_bwd_b(dconv, conv_w, dffn, rows, tm):
    tc = CONV_TC
    ncb = D_FF // tc
    nrb = rows // tm

    def body(dc_ref, nx_ref, cw_ref, dffn_in, out_ref):
        del dffn_in
        i = pl.program_id(0)
        dc = dc_ref[...].astype(F32)
        r = lax.broadcasted_iota(jnp.int32, dc.shape, 0)
        last = i == nrb - 1
        nxt = nx_ref[...].astype(F32)
        x1 = jnp.where(last, 0.0, _halo_row(nxt, 0))
        x2 = jnp.where(last, 0.0, _halo_row(nxt, 1))
        n1 = jnp.where(r == tm - 1, x1, pltpu.roll(dc, tm - 1, 0))
        n2 = jnp.where(r == tm - 1, x2, jnp.where(r == tm - 2, x1, pltpu.roll(dc, tm - 2, 0)))
        dg = cw_ref[2:3, :] * dc + cw_ref[1:2, :] * n1 + cw_ref[0:1, :] * n2
        out_ref[...] = jnp.where(i * tm + r >= PAD_LEN, dg, 0.0).astype(BF16)

    return pl.pallas_call(
        body,
        name="conv_bwd_b",
        out_shape=jax.ShapeDtypeStruct((rows, 2 * D_FF), BF16),
        grid=(nrb, ncb),
        in_specs=[pl.BlockSpec((tm, tc), lambda i, j: (i, j)),
                  pl.BlockSpec((HALO, tc), lambda i, j: (jnp.minimum((i + 1) * (tm // HALO), rows // HALO - 1), j)),
                  pl.BlockSpec((3, tc), lambda i, j: (0, j)),
                  pl.BlockSpec(memory_space=pl.ANY)],
        out_specs=pl.BlockSpec((tm, tc), lambda i, j: (i, j)),
        input_output_aliases={3: 0},
        compiler_params=_params(("parallel", "parallel")),
    )(dconv, dconv, conv_w, dffn)


def _final_call(h1, y, target, g_final, rows):
    tm = BLOCK

    def fn(i, tm_, h1v, yv, tgt, g):
        h2 = h1v + yv
        out = _rms_fwd(h2, g)
        err = jnp.where(i > 0, out - tgt, 0.0)
        loss = 0.5 * jnp.sum(jnp.mean(err * err, axis=-1, keepdims=True), axis=0, keepdims=True)
        dx, dg = _rms_bwd(h2, g, err * (1.0 / D_MODEL))
        return dx, dx, jnp.broadcast_to(loss, (1, LANE)), dg

    n_in = 4
    in_specs = [pl.BlockSpec((tm, D_MODEL), lambda i: (i, 0)), pl.BlockSpec((tm, D_MODEL), lambda i: (i, 0)),
                pl.BlockSpec((tm, D_MODEL), lambda i: (jnp.maximum(i - 1, 0), 0)),
                pl.BlockSpec((1, D_MODEL), lambda i: (0, 0))]

    def body(*refs):
        i = pl.program_id(0)
        dx, dx2, loss, dg = fn(i, tm, *[r[...] for r in refs[:n_in]])
        refs[4][...] = dx
        refs[5][...] = dx2.astype(BF16)
        for ref, val in ((refs[6], loss), (refs[7], dg)):
            @pl.when(i == 0)
            def _(ref=ref, val=val):
                ref[...] = val

            @pl.when(i > 0)
            def _(ref=ref, val=val):
                ref[...] += val

    return pl.pallas_call(
        body,
        name="final_loss",
        out_shape=[jax.ShapeDtypeStruct((rows, D_MODEL), F32), jax.ShapeDtypeStruct((rows, D_MODEL), BF16),
                   jax.ShapeDtypeStruct((1, LANE), F32), jax.ShapeDtypeStruct((1, D_MODEL), F32)],
        grid=(rows // tm,),
        in_specs=in_specs,
        out_specs=[pl.BlockSpec((tm, D_MODEL), lambda i: (i, 0)), pl.BlockSpec((tm, D_MODEL), lambda i: (i, 0)),
                   pl.BlockSpec((1, LANE), lambda i: (0, 0)), pl.BlockSpec((1, D_MODEL), lambda i: (0, 0))],
        compiler_params=_params(("arbitrary",)),
    )(h1, y, target, g_final)


def _heads_map(fn, *slabs):
    outs = [fn(*[s[:, h * LANE:(h + 1) * LANE] for s in slabs]) for h in range(HEADS)]
    if isinstance(outs[0], tuple):
        return tuple(jnp.concatenate([o[k] for o in outs], axis=1) for k in range(len(outs[0])))
    return jnp.concatenate(outs, axis=1)


def _local_step(x, positions, target, w, p, emit=None):
    kept = {}
    if emit is None:
        def emit(group):
            kept.update(group)
            return None
    s_len = x.shape[0]
    rows = s_len + BLOCK
    tm = _tile(rows, 640, 8)
    row = lambda arr, width, cb=0: ("row", arr, width, cb)

    h0 = jnp.concatenate([jnp.zeros((PAD_LEN, D_MODEL), F32), w["meta_tokens"], x], axis=0)
    pos = jnp.concatenate([jnp.zeros((PAD_LEN,), jnp.int32), jnp.arange(N_META, dtype=jnp.int32),
                           positions.astype(jnp.int32) + N_META])
    inv = 1.0 / (ROPE_THETA ** (jnp.arange(0, ROPE, 2, dtype=F32) / ROPE))
    ang = pos.astype(F32)[:, None] * inv
    zero = jnp.zeros((rows, LANE - ROPE), F32)
    cos_t = jnp.concatenate([jnp.cos(ang), jnp.cos(ang), zero], axis=1)
    sin_t = jnp.concatenate([-jnp.sin(ang), jnp.sin(ang), zero], axis=1)
    lb_r0, lb_r1 = p["lb_raw"][0:1], p["lb_raw"][1:2]

    def lb_fn(i, tm_, r0, r1):
        m = jnp.maximum(r0, r1)
        e0, e1 = jnp.exp(r0 - m), jnp.exp(r1 - m)
        return (e0 / (e0 + e1),)

    (lb,) = _rowwise("lb_fwd", lb_fn, [("bc", lb_r0), ("bc", lb_r1)], [("acc", (1, D_MODEL))], 1, 1)

    (u1,) = _rowwise("mix_norm", lambda i, t, h, g: (_rms_fwd(h, g),),
                     [row(h0, D_MODEL), ("bc", p["g_mix_norm"])], [("row", D_MODEL, BF16)], rows, tm)
    proj = _matmul(u1, w["w_in"], "nn", F32, "mm_proj")
    hint = getattr(w, "hint", lambda name, after: None)
    hint("w_q_up", proj)
    qn, kvn = _rowwise(
        "latent_norm", lambda i, t, ql, kl, gq, gk: (_rms_fwd(ql, gq), _rms_fwd(kl, gk)),
        [row(proj, Q_LORA, 0), row(proj, KV_LORA, SEG_KV_LAT // KV_LORA), ("bc", p["g_q_norm"]), ("bc", p["g_kv_norm"])],
        [("row", Q_LORA, BF16), ("row", KV_LORA, BF16)], rows, tm)
    q_raw = _matmul(qn, w["w_q_up"], "nn", F32, "mm_q_up")
    kv = _matmul(kvn, w["w_kv_up"], "nn", F32, "mm_kv_up")
    q_att, k_att, v_att = _rope_fwd_call(q_raw, kv, proj, cos_t, sin_t, rows, tm)
    o32, o_bf, lse = _attn_fwd(q_att, k_att, v_att, rows)
    hint("w_branch_mla", lse)
    o_h, a_mat, s_states = _hgrn_fwd(proj, lb, rows)

    def hgrn_post(i, t, oh, hg, g):
        return (_heads_map(lambda a, b: _rms_fwd(a, g) * _silu(b), oh, hg),)

    (o_hgrn,) = _rowwise("hgrn_post", hgrn_post,
                         [row(o_h, D_MODEL), row(proj, D_MODEL, SEG_HG // D_MODEL), ("bc", p["g_hgrn_norm"])],
                         [("row", D_MODEL, BF16)], rows, tm)
    br_a = _matmul(o_bf, w["w_branch_mla"], "nn", F32, "mm_branch_mla")
    br_b = _matmul(o_hgrn, w["w_branch_hgrn"], "nn", F32, "mm_branch_hgrn")
    (merged,) = _rowwise(
        "merge", lambda i, t, a, b, ga, gb: (_sigmoid(ga) * a + _sigmoid(gb) * b,),
        [row(br_a, D_MODEL), row(br_b, D_MODEL), row(proj, D_MODEL, SEG_GA // D_MODEL), row(proj, D_MODEL, SEG_GB // D_MODEL)],
        [("row", D_MODEL, BF16)], rows, tm)
    mix_out = _matmul(merged, w["w_out"], "nn", F32, "mm_out")

    def ffn_norm(i, t, h, mo, g):
        h1v = h + mo
        return h1v, _rms_fwd(h1v, g)

    h1, u2 = _rowwise("ffn_norm", ffn_norm, [row(h0, D_MODEL), row(mix_out, D_MODEL), ("bc", p["g_ffn_norm"])],
                      [("row", D_MODEL, F32), ("row", D_MODEL, BF16)], rows, tm)
    ffn = _matmul(u2, w["w_ffn_in"], "nn", BF16, "mm_ffn_in")
    act = _conv_fwd(ffn, w["conv_w"], p["conv_b"], rows, tm)
    y = _matmul(act, w["w_ffn_out"], "nn", F32, "mm_ffn_out")
    dh2, dh2_bf, loss_acc, dg_final = _final_call(h1, y, target, p["g_final_norm"].reshape(1, D_MODEL), rows)

    grads = {"g_final_norm": dg_final.reshape(D_MODEL)}
    dact = _matmul(dh2_bf, w["w_ffn_out"], "nt", BF16, "mm_d_act")
    grads["w_ffn_out"] = _matmul(act, dh2_bf, "tn", BF16, "mm_dw_ffn_out")
    dconv, dffn, dcw0, dcw1, dcw2, dcb = _conv_bwd_a(ffn, dact, w["conv_w"], p["conv_b"], rows, tm)
    dffn = _conv_bwd_b(dconv, w["conv_w"], dffn, rows, tm)
    grads["conv_w"] = jnp.concatenate([dcw0, dcw1, dcw2], axis=0)
    grads["conv_b"] = dcb
    du2 = _matmul(dffn, w["w_ffn_in"], "nt", F32, "mm_d_u2")
    grads["w_ffn_in"] = _matmul(u2, dffn, "tn", BF16, "mm_dw_ffn_in", col_blocks=N_DEV)

    def ffn_norm_bwd(i, t, h, du, dh, g):
        dx, dg = _rms_bwd(h, g, du)
        dh1v = dh + dx
        return dh1v, dh1v, dg

    dh1, dh1_bf, grads["g_ffn_norm"] = _rowwise(
        "ffn_norm_bwd", ffn_norm_bwd, [row(h1, D_MODEL), row(du2, D_MODEL), row(dh2, D_MODEL), ("bc", p["g_ffn_norm"])],
        [("row", D_MODEL, F32), ("row", D_MODEL, BF16), ("acc", (1, D_MODEL))], rows, tm)
    tok = emit({n: grads.pop(n) for n in ("w_ffn_out", "w_ffn_in", "conv_w", "conv_b", "g_final_norm", "g_ffn_norm")})
    dmerged = _matmul(dh1_bf, w["w_out"], "nt", F32, "mm_d_merged", after=tok)
    grads["w_out"] = _matmul(merged, dh1_bf, "tn", BF16, "mm_dw_out")

    def merge_bwd(i, t, dm, a, b, ga, gb):
        sa, sb = _sigmoid(ga), _sigmoid(gb)
        return dm * sa, dm * sb, dm * a * sa * (1.0 - sa), dm * b * sb * (1.0 - sb)

    da_bf, db_bf, dga, dgb = _rowwise(
        "merge_bwd", merge_bwd,
        [row(dmerged, D_MODEL), row(br_a, D_MODEL), row(br_b, D_MODEL),
         row(proj, D_MODEL, SEG_GA // D_MODEL), row(proj, D_MODEL, SEG_GB // D_MODEL)],
        [("row", D_MODEL, BF16)] * 4, rows, tm)
    do_mla = _matmul(da_bf, w["w_branch_mla"], "nt", BF16, "mm_d_o_mla")
    grads["w_branch_mla"] = _matmul(o_bf, da_bf, "tn", BF16, "mm_dw_branch_mla")
    do_hgrn = _matmul(db_bf, w["w_branch_hgrn"], "nt", F32, "mm_d_o_hgrn")
    grads["w_branch_hgrn"] = _matmul(o_hgrn, db_bf, "tn", BF16, "mm_dw_branch_hgrn")

    def hgrn_post_bwd(i, t, dy, oh, hg, g):
        def one(dyh, ohh, hgh):
            gate, dgate = _silu_both(hgh)
            dx, dg = _rms_bwd(ohh, g, dyh * gate)
            return dx, dyh * _rms_fwd(ohh, g) * dgate, dg

        dx, dhg, dg = _heads_map(one, dy, oh, hg)
        dg_sum = dg[:, 0:LANE]
        for h in range(1, HEADS):
            dg_sum = dg_sum + dg[:, h * LANE:(h + 1) * LANE]
        return dx, dhg, dg_sum

    tok = emit({n: grads.pop(n) for n in ("w_out", "w_branch_mla", "w_branch_hgrn")})
    do_h, dhg, grads["g_hgrn_norm"] = _rowwise(
        "hgrn_post_bwd", hgrn_post_bwd,
        [row(do_hgrn, D_MODEL), row(o_h, D_MODEL), row(proj, D_MODEL, SEG_HG // D_MODEL), ("bc", p["g_hgrn_norm"])],
        [("row", D_MODEL, F32), ("row", D_MODEL, BF16), ("acc", (1, LANE))], rows, tm, after=tok)
    dhq, dhf, dhi, dlb = _hgrn_bwd(proj, lb, a_mat, s_states, do_h, rows)

    def lb_bwd(i, tm_, d, l):
        t = d * l * (1.0 - l)
        return t, -t

    dlb0, dlb1 = _rowwise("lb_bwd", lb_bwd, [("bc", dlb), ("bc", lb)], [("acc", (1, D_MODEL))] * 2, 1, 1)
    grads["lb_raw"] = jnp.concatenate([dlb0, dlb1], axis=0)

    delta = _attn_delta(do_mla, o32, rows, tm)
    dq_att, dk_att, dv_att = _attn_bwd(q_att, k_att, v_att, do_mla, lse,
                                       jnp.transpose(delta[:, :HEADS]).reshape(HEADS, 1, rows), rows)
    dq_full, dkv, dkr = _rope_bwd_call(dq_att, dk_att, dv_att, cos_t, sin_t, rows, tm)
    dqn = _matmul(dq_full, w["w_q_up"], "nt", F32, "mm_d_qn")
    grads["w_q_up"] = _matmul(qn, dq_full, "tn", BF16, "mm_dw_q_up")
    dkvn = _matmul(dkv, w["w_kv_up"], "nt", F32, "mm_d_kvn")
    grads["w_kv_up"] = _matmul(kvn, dkv, "tn", BF16, "mm_dw_kv_up")

    def latent_norm_bwd(i, t, ql, kl, dq, dk, gq, gk):
        dql, dgq = _rms_bwd(ql, gq, dq)
        dkl, dgk = _rms_bwd(kl, gk, dk)
        return dql, dkl, dgq, dgk

    dq_lat, dkv_lat, grads["g_q_norm"], grads["g_kv_norm"] = _rowwise(
        "latent_norm_bwd", latent_norm_bwd,
        [row(proj, Q_LORA, 0), row(proj, KV_LORA, SEG_KV_LAT // KV_LORA), row(dqn, Q_LORA), row(dkvn, KV_LORA),
         ("bc", p["g_q_norm"]), ("bc", p["g_kv_norm"])],
        [("row", Q_LORA, BF16), ("row", KV_LORA, BF16), ("acc", (1, Q_LORA)), ("acc", (1, KV_LORA))], rows, tm)
    dproj = jnp.concatenate([dq_lat, dkv_lat, dhq, dhf, dhi, dhg, dga, dgb, dkr], axis=1)
    tok = emit({n: grads.pop(n) for n in ("w_q_up", "w_kv_up", "lb_raw", "g_q_norm", "g_kv_norm", "g_hgrn_norm")})
    grads["w_in"] = _matmul(u1, dproj, "tn", BF16, "mm_dw_in", after=tok)
    tok = emit({"w_in": grads.pop("w_in")})
    du1 = _matmul(dproj, w["w_in"], "nt", F32, "mm_d_u1", after=tok)

    def mix_norm_bwd(i, t, h, du, dh, g):
        dx, dg = _rms_bwd(h, g, du)
        return dh + dx, dh + dx, dg

    grad_x, d_prefix, grads["g_mix_norm"] = _rowwise(
        "mix_norm_bwd", mix_norm_bwd, [row(h0, D_MODEL), row(du1, D_MODEL), row(dh1, D_MODEL), ("bc", p["g_mix_norm"])],
        [("tail", D_MODEL, F32), ("head", D_MODEL, F32), ("acc", (1, D_MODEL))], rows, BLOCK)
    grads["meta_tokens"] = d_prefix[PAD_LEN:BLOCK]
    kept.update(grads)
    return loss_acc[0, 0], grad_x, kept


K_ROPE_AT = Q_LORA + KV_LORA
COL_SHARDED = ("w_in", "w_q_up", "w_kv_up", "w_ffn_in", "conv_w", "meta_tokens")
BIG = ("w_in", "w_q_up", "w_kv_up", "w_branch_mla", "w_branch_hgrn", "w_out", "w_ffn_in", "w_ffn_out")
SMALL = ("conv_b", "g_mix_norm", "g_q_norm", "g_kv_norm", "g_hgrn_norm", "g_ffn_norm", "g_final_norm", "lb_raw")


def _unshard(name, stacked):
    if name in COL_SHARDED:
        return jnp.transpose(stacked, (1, 0, 2)).reshape(stacked.shape[1], N_DEV * stacked.shape[2])
    return stacked.reshape(N_DEV * stacked.shape[1], stacked.shape[2])


def _reshard(name, full):
    if full.ndim == 3:
        return full
    if name in COL_SHARDED:
        r, c = full.shape
        return jnp.transpose(full.reshape(r, N_DEV, c // N_DEV), (1, 0, 2))
    return full.reshape(N_DEV, full.shape[0] // N_DEV, full.shape[1])


def _to_kernel_layout(full):
    out = dict(full)
    if "w_in" in full:
        w_in = full["w_in"]
        pad = jnp.zeros((D_MODEL, KR_W - ROPE), w_in.dtype)
        out["w_in"] = jnp.concatenate(
            [w_in[:, :K_ROPE_AT], w_in[:, K_ROPE_AT + ROPE:], w_in[:, K_ROPE_AT:K_ROPE_AT + ROPE], pad], axis=1)
    if "w_q_up" in full:
        wq = full["w_q_up"].reshape(Q_LORA, HEADS, NOPE + ROPE)
        out["w_q_up"] = jnp.pad(wq, ((0, 0), (0, 0), (0, QHEAD_W - NOPE - ROPE))).reshape(Q_LORA, HEADS * QHEAD_W)
    return out


def _from_kernel_layout(grads):
    out = dict(grads)
    if "w_in" in grads:
        g = grads["w_in"]
        out["w_in"] = jnp.concatenate([g[:, :K_ROPE_AT], g[:, SEG_KR:SEG_KR + ROPE], g[:, K_ROPE_AT:SEG_KR]], axis=1)
    if "w_q_up" in grads:
        g = grads["w_q_up"].reshape(Q_LORA, HEADS, QHEAD_W)
        out["w_q_up"] = g[:, :, :NOPE + ROPE].reshape(Q_LORA, HEADS * (NOPE + ROPE))
    return out


MESH_ID = pl.DeviceIdType.MESH
ANY = pl.BlockSpec(memory_space=pl.ANY)


def _slot(dev):
    return 4 * dev[0] + 2 * dev[1] + dev[2]


def _all_gather(shards):
    n = len(shards)

    def body(*refs):
        ins, outs = refs[:n], refs[n:2 * n]
        send_sems, recv_sems, local_sems = refs[2 * n:]
        x, y, c = lax.axis_index("x"), lax.axis_index("y"), lax.axis_index("c")
        me, sibling = (x, y, c), (x, y, 1 - c)
        chips = [(1 - x, y), (x, 1 - y), (1 - x, 1 - y)]

        def copy(a, k, block, to, src=None):
            dst = outs[a].at[_slot(block)]
            return pltpu.make_async_remote_copy(
                src_ref=dst if src is None else src, dst_ref=dst, send_sem=send_sems.at[a, k],
                recv_sem=recv_sems.at[a, k], device_id=to, device_id_type=MESH_ID)

        mine = [pltpu.make_async_copy(ins[a], outs[a].at[_slot(me)], local_sems.at[a]) for a in range(n)]
        for cp in mine:
            cp.start()
        first = []
        for a in range(n):
            first.append(copy(a, 0, me, sibling, src=ins[a]))
            first += [copy(a, 1 + j, me, (*chip, c), src=ins[a]) for j, chip in enumerate(chips)]
        for cp in first:
            cp.start()
        passed = []
        for a in range(n):
            for j, chip in enumerate(chips):
                copy(a, 1 + j, (*chip, c), me).wait_recv()
                fwd = copy(a, 4 + j, (*chip, c), sibling)
                fwd.start()
                passed.append(fwd)
        for a in range(n):
            copy(a, 0, sibling, me).wait_recv()
            for j, chip in enumerate(chips):
                copy(a, 4 + j, (*chip, 1 - c), me).wait_recv()
        for cp in first + passed:
            cp.wait_send()
        for cp in mine:
            cp.wait()

    return pl.pallas_call(
        body,
        name="gather_weights",
        out_shape=[jax.ShapeDtypeStruct((N_DEV,) + s.shape, s.dtype) for s in shards],
        in_specs=[ANY] * n,
        out_specs=[ANY] * n,
        scratch_shapes=[pltpu.SemaphoreType.DMA((n, 7)), pltpu.SemaphoreType.DMA((n, 7)), pltpu.SemaphoreType.DMA((n,))],
    )(*shards)


def _exchange(blocked, replicated):
    nb, n = len(blocked), len(blocked) + len(replicated)
    arrays = list(blocked) + list(replicated)

    def body(*refs):
        ins, outs = refs[:n], refs[n:2 * n]
        send_sems, recv_sems, local_sems = refs[2 * n:]
        x, y, c = lax.axis_index("x"), lax.axis_index("y"), lax.axis_index("c")
        me = (x, y, c)
        peers = [(x, y, 1 - c), (1 - x, y, c), (x, 1 - y, c), (1 - x, 1 - y, c),
                 (1 - x, y, 1 - c), (x, 1 - y, 1 - c), (1 - x, 1 - y, 1 - c)]

        def src_of(a, dev):
            return ins[a].at[_slot(dev)] if a < nb else ins[a]

        def copy(a, k, frm, to):
            return pltpu.make_async_remote_copy(
                src_ref=src_of(a, to), dst_ref=outs[a].at[_slot(frm)], send_sem=send_sems.at[a, k],
                recv_sem=recv_sems.at[a, k], device_id=to, device_id_type=MESH_ID)

        mine = [pltpu.make_async_copy(src_of(a, me), outs[a].at[_slot(me)], local_sems.at[a]) for a in range(n)]
        for cp in mine:
            cp.start()
        sends = [copy(a, k, me, peer) for a in range(n) for k, peer in enumerate(peers)]
        for cp in sends:
            cp.start()
        for a in range(n):
            for k, peer in enumerate(peers):
                copy(a, k, peer, me).wait_recv()
        for cp in sends:
            cp.wait_send()
        for cp in mine:
            cp.wait()

    return pl.pallas_call(
        body,
        name="exchange_grads",
        out_shape=[jax.ShapeDtypeStruct(s.shape, s.dtype) for s in blocked]
        + [jax.ShapeDtypeStruct((N_DEV,) + s.shape, s.dtype) for s in replicated],
        in_specs=[ANY] * n,
        out_specs=[ANY] * n,
        scratch_shapes=[pltpu.SemaphoreType.DMA((n, 7)), pltpu.SemaphoreType.DMA((n, 7)), pltpu.SemaphoreType.DMA((n,))],
    )(*arrays)


ADAMW_BLOCK_ELEMS = 256 * 1024


def _adamw(name, parts, w, m, v, own=None, me=None):
    r, c = w.shape
    tr = _tile(r, max(16, ADAMW_BLOCK_ELEMS // c), 16)

    def body(*refs):
        if own is None:
            p_ref, w_ref, m_ref, v_ref, g_ref, d_ref, nm_ref, nv_ref = refs
            terms = [p_ref[s].astype(F32) for s in range(N_DEV)]
        else:
            me_ref, p_ref, own_ref, w_ref, m_ref, v_ref, g_ref, d_ref, nm_ref, nv_ref = refs
            mine = own_ref[0].astype(F32)
            terms = [jnp.where(me_ref[0] == s, mine, p_ref[s].astype(F32)) for s in range(N_DEV)]
        g = terms[0]
        for s in range(1, N_DEV):
            g = g + terms[s]
        m_new = ADAM_B1 * m_ref[...] + (1.0 - ADAM_B1) * g
        v_new = ADAM_B2 * v_ref[...] + (1.0 - ADAM_B2) * (g * g)
        m_hat = m_new / (1.0 - ADAM_B1 ** ADAM_STEP)
        v_hat = v_new / (1.0 - ADAM_B2 ** ADAM_STEP)
        g_ref[...] = g
        d_ref[...] = -ADAM_LR * (m_hat / (jnp.sqrt(v_hat) + ADAM_EPS) + ADAM_WD * w_ref[...])
        nm_ref[...] = m_new
        nv_ref[...] = v_new

    if own is None:
        blk = pl.BlockSpec((tr, c), lambda i: (i, 0))
        return pl.pallas_call(
            body,
            name="adamw_" + name,
            out_shape=[jax.ShapeDtypeStruct((r, c), F32)] * 4,
            grid=(r // tr,),
            in_specs=[pl.BlockSpec((N_DEV, tr, c), lambda i: (0, i, 0)), blk, blk, blk],
            out_specs=[blk] * 4,
            compiler_params=_params(("parallel",)),
        )(parts, w, m, v)
    blk = pl.BlockSpec((tr, c), lambda i, me_ref: (i, 0))
    own_at = (lambda i, me_ref: (me_ref[0], i, 0)) if own.shape[0] == N_DEV else (lambda i, me_ref: (0, i, 0))
    return pl.pallas_call(
        body,
        name="adamw_" + name,
        out_shape=[jax.ShapeDtypeStruct((r, c), F32)] * 4,
        grid_spec=pltpu.PrefetchScalarGridSpec(
            num_scalar_prefetch=1,
            grid=(r // tr,),
            in_specs=[pl.BlockSpec((N_DEV, tr, c), lambda i, me_ref: (0, i, 0)), pl.BlockSpec((1, tr, c), own_at),
                      blk, blk, blk],
            out_specs=[blk] * 4),
        compiler_params=_params(("parallel",)),
    )(me, parts, own, w, m, v)


HBM_SPEC = pl.BlockSpec(memory_space=pltpu.HBM)
SEM_SPEC = pl.BlockSpec(memory_space=pltpu.SEMAPHORE)
SIDE_EFFECT = pltpu.SideEffectType.DATAFLOW_SIDE_EFFECTING
N_PEERS = N_DEV - 1


def _peers(x, y, c):
    return [(x, y, 1 - c), (1 - x, y, c), (x, 1 - y, c), (1 - x, 1 - y, c),
            (1 - x, y, 1 - c), (x, 1 - y, 1 - c), (1 - x, 1 - y, 1 - c)]


def _split_copy(srcs, lands, blocked, send_sems, recv_sems, a, k, frm, to):
    src = srcs[a].at[_slot(to)] if blocked[a] else srcs[a]
    return pltpu.make_async_remote_copy(
        src_ref=src, dst_ref=lands[a].at[_slot(frm)], send_sem=send_sems.at[a * N_PEERS + k],
        recv_sem=recv_sems.at[a * N_PEERS + k],
        device_id=to, device_id_type=MESH_ID)


def _exchange_start(name, srcs, lands, blocked, after=()):
    n = len(srcs)
    after = list(after)

    def body(*refs):
        src_refs, land_refs = refs[:n], refs[n:2 * n]
        send_sems, recv_sems = refs[2 * n + len(after)], refs[2 * n + len(after) + 1]
        token = refs[-1]
        x, y, c = lax.axis_index("x"), lax.axis_index("y"), lax.axis_index("c")
        for a in range(n):
            for k, peer in enumerate(_peers(x, y, c)):
                _split_copy(src_refs, land_refs, blocked, send_sems, recv_sems, a, k, (x, y, c), peer).start()
        token[...] = jnp.zeros_like(token)

    thru = [pltpu.HBM(s.shape, s.dtype) for s in list(srcs) + list(lands)]
    res = pl.pallas_call(
        body,
        name=name,
        out_shape=(pltpu.SemaphoreType.DMA((n * N_PEERS,)), pltpu.SemaphoreType.DMA((n * N_PEERS,)), *thru,
                   jax.ShapeDtypeStruct((8, LANE), F32)),
        in_specs=[HBM_SPEC] * (2 * n) + [pl.BlockSpec(memory_space=pl.ANY)] * len(after),
        out_specs=(SEM_SPEC, SEM_SPEC, *([HBM_SPEC] * (2 * n)), pl.BlockSpec(memory_space=pltpu.VMEM)),
        input_output_aliases={i: 2 + i for i in range(2 * n)},
        compiler_params=pltpu.CompilerParams(has_side_effects=SIDE_EFFECT),
    )(*[pltpu.with_memory_space_constraint(s, pltpu.HBM) for s in list(srcs) + list(lands)], *after)
    return res[0], res[1], res[2:2 + n], res[2 + n:2 + 2 * n], res[-1]


def _exchange_wait(name, send_sems, recv_sems, srcs, lands, blocked, after):
    n, n_after = len(srcs), len(after)

    def body(*refs):
        src_refs, land_refs = refs[:n], refs[n:2 * n]
        send, recv = refs[2 * n], refs[2 * n + 1]
        x, y, c = lax.axis_index("x"), lax.axis_index("y"), lax.axis_index("c")
        for a in range(n):
            for k, peer in enumerate(_peers(x, y, c)):
                _split_copy(src_refs, land_refs, blocked, send, recv, a, k, (x, y, c), peer).wait_send()
                _split_copy(src_refs, land_refs, blocked, send, recv, a, k, peer, (x, y, c)).wait_recv()

    res = pl.pallas_call(
        body,
        name=name,
        out_shape=tuple(pltpu.HBM(s.shape, s.dtype) for s in list(srcs) + list(lands)),
        in_specs=[HBM_SPEC] * (2 * n) + [SEM_SPEC, SEM_SPEC] + [pl.BlockSpec(memory_space=pl.ANY)] * n_after,
        out_specs=tuple([HBM_SPEC] * (2 * n)),
        input_output_aliases={i: i for i in range(2 * n)},
        compiler_params=pltpu.CompilerParams(has_side_effects=SIDE_EFFECT),
    )(*srcs, *lands, send_sems, recv_sems, *after)
    return res[:n], res[n:]


class _LazyWeights:
    def __init__(self):
        self.ready, self.groups, self.hints = {}, {}, {}

    def add_group(self, wait_name, names, send, recv, srcs, lands):
        for n in names:
            self.groups[n] = (wait_name, names, send, recv, srcs, lands)

    def hint(self, name, after):
        self.hints[self.groups[name][0]] = after

    def __getitem__(self, name):
        if name not in self.ready:
            wait_name, names, send, recv, srcs, lands = self.groups[name]
            after = [self.hints[wait_name]] if wait_name in self.hints else []
            _, whole = _exchange_wait(wait_name, send, recv, srcs, lands, [False] * len(names), after)
            for n, stacked in zip(names, whole):
                self.ready[n] = _to_kernel_layout({n: _unshard(n, stacked)})[n]
        return self.ready[name]


def kernel(x, positions, meta_tokens, w_in, w_q_up, w_kv_up, w_branch_mla, w_branch_hgrn, w_out, w_ffn_in, w_ffn_out, conv_w, conv_b, g_mix_norm, g_q_norm, g_kv_norm, g_hgrn_norm, g_ffn_norm, g_final_norm, lb_raw, loss_target, m_meta_tokens, m_w_in, m_w_q_up, m_w_kv_up, m_w_branch_mla, m_w_branch_hgrn, m_w_out, m_w_ffn_in, m_w_ffn_out, m_conv_w, m_conv_b, m_g_mix_norm, m_g_q_norm, m_g_kv_norm, m_g_hgrn_norm, m_g_ffn_norm, m_g_final_norm, m_lb_raw, v_meta_tokens, v_w_in, v_w_q_up, v_w_kv_up, v_w_branch_mla, v_w_branch_hgrn, v_w_out, v_w_ffn_in, v_w_ffn_out, v_conv_w, v_conv_b, v_g_mix_norm, v_g_q_norm, v_g_kv_norm, v_g_hgrn_norm, v_g_ffn_norm, v_g_final_norm, v_lb_raw):
    local = dict(zip(
        ("meta_tokens", "w_in", "w_q_up", "w_kv_up", "w_branch_mla", "w_branch_hgrn", "w_out", "w_ffn_in", "w_ffn_out",
         "conv_w", "conv_b", "g_mix_norm", "g_q_norm", "g_kv_norm", "g_hgrn_norm", "g_ffn_norm", "g_final_norm", "lb_raw"),
        (meta_tokens, w_in, w_q_up, w_kv_up, w_branch_mla, w_branch_hgrn, w_out, w_ffn_in, w_ffn_out,
         conv_w, conv_b, g_mix_norm, g_q_norm, g_kv_norm, g_hgrn_norm, g_ffn_norm, g_final_norm, lb_raw)))
    mom_m = dict(zip(local, (m_meta_tokens, m_w_in, m_w_q_up, m_w_kv_up, m_w_branch_mla, m_w_branch_hgrn, m_w_out, m_w_ffn_in,
                             m_w_ffn_out, m_conv_w, m_conv_b, m_g_mix_norm, m_g_q_norm, m_g_kv_norm, m_g_hgrn_norm,
                             m_g_ffn_norm, m_g_final_norm, m_lb_raw)))
    mom_v = dict(zip(local, (v_meta_tokens, v_w_in, v_w_q_up, v_w_kv_up, v_w_branch_mla, v_w_branch_hgrn, v_w_out, v_w_ffn_in,
                             v_w_ffn_out, v_conv_w, v_conv_b, v_g_mix_norm, v_g_q_norm, v_g_kv_norm, v_g_hgrn_norm,
                             v_g_ffn_norm, v_g_final_norm, v_lb_raw)))
    sharded = BIG + ("conv_w", "meta_tokens")

    def shard2d(name, arr):
        return arr.reshape(arr.shape[-2:]) if name != "meta_tokens" else arr

    def as2d(name, arr):
        return arr.reshape(1, -1) if arr.ndim == 1 else shard2d(name, arr)

    me = 4 * lax.axis_index("x") + 2 * lax.axis_index("y") + lax.axis_index("c")

    def landing(own):
        zone = lax.empty((N_DEV,) + own.shape[1:], own.dtype)
        return lax.dynamic_update_slice_in_dim(zone, own, me, 0)

    shards = {n: shard2d(n, local[n]).astype(BF16) for n in BIG}
    shards.update({n: shard2d(n, local[n]) for n in ("conv_w", "meta_tokens")})
    full = _LazyWeights()
    first = ("w_in", "meta_tokens")
    gathered = _all_gather([shards[n] for n in first])
    for n, g in zip(first, gathered):
        full.ready[n] = _to_kernel_layout({n: _unshard(n, g)})[n]
    later = (("w_q_up", "w_kv_up"), ("w_branch_mla", "w_branch_hgrn", "w_out", "w_ffn_in", "w_ffn_out", "conv_w"))
    for k, names in enumerate(later):
        srcs = [shards[n] for n in names]
        send, recv, srcs_thru, lands_thru, _ = _exchange_start(
            f"gather_start_{k}", srcs, [landing(s[None]) for s in srcs], [False] * len(names), after=[gathered[0]])
        full.add_group(f"gather_wait_{k}", names, send, recv, srcs_thru, lands_thru)
    small = {n: local[n] for n in SMALL}

    started = []

    def sources(group):
        group = _from_kernel_layout(group)
        names = list(group)
        blocked = [n in sharded for n in names]
        srcs = [_reshard(n, group[n]) if b else as2d(n, group[n]) for n, b in zip(names, blocked)]
        return names, blocked, srcs

    def emit(group):
        names, blocked, srcs = sources(group)
        lands = [lax.empty((N_DEV,) + (s.shape[1:] if b else s.shape), s.dtype) for s, b in zip(srcs, blocked)]
        k = len(started)
        send, recv, srcs_thru, lands_thru, token = _exchange_start(f"exchange_start_{k}", srcs, lands, blocked)
        started.append((names, blocked, send, recv, srcs_thru, lands_thru))
        return token

    loss, grad_x, last = _local_step(x[0], positions[0], loss_target[0], full, small, emit)

    out = {}

    me_arr = me.astype(jnp.int32).reshape(1)

    def update(names, parts, owns=None):
        for k, (n, part) in enumerate(zip(names, parts)):
            own = None if owns is None else (owns[k] if owns[k].ndim == 3 else owns[k][None])
            res = _adamw(n, part, as2d(n, local[n]), as2d(n, mom_m[n]), as2d(n, mom_v[n]), own,
                         None if owns is None else me_arr)
            out[n] = [r.reshape(local[n].shape) for r in res]

    after = [last["g_mix_norm"]]
    for k, (names, blocked, send, recv, srcs_thru, lands_thru) in enumerate(started):
        srcs_done, parts = _exchange_wait(f"exchange_wait_{k}", send, recv, srcs_thru, lands_thru, blocked, after)
        update(names, parts, srcs_done)
        after = [out[names[0]][0]]
    names, blocked, srcs = sources(last)
    in_blocks = [(n, s) for n, s, b in zip(names, srcs, blocked) if b]
    whole = [(n, s) for n, s, b in zip(names, srcs, blocked) if not b]
    update([n for n, _ in in_blocks + whole], _exchange([s for _, s in in_blocks], [s for _, s in whole]))

    loss = lax.psum(loss, ("x", "y", "c"))
    order = tuple(local)
    return (loss, grad_x[None], *[out[n][0] for n in order], *[out[n][1] for n in order],
            *[out[n][2] for n in order], *[out[n][3] for n in order])
```

```python
import functools

import jax
import jax.numpy as jnp
import numpy as np
from jax import lax
from jax.experimental import pallas as pl
from jax.experimental.pallas import tpu as pltpu

F32 = jnp.float32
BF16 = jnp.bfloat16

D_MODEL = 2048
N_META = 16
BLOCK = 128
PAD_LEN = BLOCK - N_META
HEADS = 16
Q_LORA = 1536
KV_LORA = 512
ROPE = 64
NOPE = 128
VDIM = 128
D_FF = 5632
NORM_EPS = 1e-6
ROPE_THETA = 10000.0
ATTN_SCALE = (NOPE + ROPE) ** -0.5
ADAM_LR = 0.001
ADAM_B1 = 0.9
ADAM_B2 = 0.999
ADAM_EPS = 1e-08
ADAM_WD = 0.01
ADAM_STEP = 10
N_DEV = 8

LANE = 128
SEG_Q_LAT = 0
SEG_KV_LAT = Q_LORA
SEG_HQ = 2048
SEG_HF = SEG_HQ + D_MODEL
SEG_HI = SEG_HF + D_MODEL
SEG_HG = SEG_HI + D_MODEL
SEG_GA = SEG_HG + D_MODEL
SEG_GB = SEG_GA + D_MODEL
SEG_KR = SEG_GB + D_MODEL
KR_W = 256
PROJ_W = SEG_KR + KR_W
QHEAD_W = 256

V7X_VMEM_BYTES = 64 * 1024 * 1024
VMEM_LIMIT = V7X_VMEM_BYTES * 7 // 8
NEG_BIG = -1e30
SUB = 8


def _tile(n, target, mult):
    best = None
    for t in range(mult, min(n, target) + 1, mult):
        if n % t == 0:
            best = t
    return n if best is None else best


def _params(sem):
    return pltpu.CompilerParams(dimension_semantics=sem, vmem_limit_bytes=VMEM_LIMIT)


def _sigmoid(x):
    return 0.5 * jnp.tanh(0.5 * x) + 0.5


MATMUL_WINDOW_BYTES = 40 * 1024 * 1024
_DIMS = {"nn":(((1,), (0,)), ((), ())), "nt": (((1,), (1,)), ((), ())), "tn": (((0,), (0,)), ((), ()))}


def _matmul(a, b, mode, out_dtype, name, after=None, col_blocks=1):
    if mode == "nn":
        (m, k), (_, n) = a.shape, b.shape
    elif mode == "nt":
        (m, k), (n, _) = a.shape, b.shape
    else:
        (k, m), (_, n) = a.shape, b.shape
    tm = _tile(m, 1040, 8) if mode != "tn" else _tile(m, 1024, LANE)
    tn = _tile(n, 1024, LANE) if col_blocks == 1 else n // col_blocks
    if mode == "nn":
        tk = _tile(k, 2816, LANE)
    elif mode == "nt":
        tk = _tile(k, 2816, LANE)
        if 3 * tk < k:
            half = _tile(m, 520, 8)
            out_bytes = 2 * half * tn * jnp.dtype(out_dtype).itemsize + 4 * half * tn
            longer = _tile(k, max(LANE, (MATMUL_WINDOW_BYTES - out_bytes) // (4 * (half + tn))), LANE)
            if 3 * longer >= k:
                tm, tk = half, longer
    else:
        for tm in (_tile(m, 1024, LANE), _tile(m, 512, LANE)):
            out_bytes = 2 * tm * tn * jnp.dtype(out_dtype).itemsize + 4 * tm * tn
            tk = _tile(k, max(8, (MATMUL_WINDOW_BYTES - out_bytes) // (4 * (tm + tn))), 8)
            if 2 * tk >= k:
                break
    nk = k // tk
    if mode == "nn":
        a_spec = pl.BlockSpec((tm, tk), lambda i, j, kk: (i, kk))
        b_spec = pl.BlockSpec((tk, tn), lambda i, j, kk: (kk, j))
    elif mode == "nt":
        a_spec = pl.BlockSpec((tm, tk), lambda i, j, kk: (i, kk))
        b_spec = pl.BlockSpec((tn, tk), lambda i, j, kk: (j, kk))
    else:
        a_spec = pl.BlockSpec((tk, tm), lambda i, j, kk: (kk, i))
        b_spec = pl.BlockSpec((tk, tn), lambda i, j, kk: (kk, j))
    dims = _DIMS[mode]

    n_after = 0 if after is None else 1

    def body(a_ref, b_ref, *rest):
        o_ref, acc = rest[n_after], rest[n_after + 1:]
        part = lax.dot_general(a_ref[...], b_ref[...], dims, preferred_element_type=F32)
        if nk == 1:
            o_ref[...] = part.astype(o_ref.dtype)
            return
        acc_ref, kk = acc[0], pl.program_id(2)

        @pl.when(kk == 0)
        def _():
            acc_ref[...] = part

        @pl.when((kk > 0) & (kk < nk - 1))
        def _():
            acc_ref[...] += part

        @pl.when(kk == nk - 1)
        def _():
            o_ref[...] = (acc_ref[...] + part).astype(o_ref.dtype)

    if col_blocks == 1:
        out_shape = jax.ShapeDtypeStruct((m, n), out_dtype)
        out_spec = pl.BlockSpec((tm, tn), lambda i, j, kk: (i, j))
    else:
        out_shape = jax.ShapeDtypeStruct((col_blocks, m, tn), out_dtype)
        out_spec = pl.BlockSpec((None, tm, tn), lambda i, j, kk: (j, i, 0))
    return pl.pallas_call(
        body,
        name=name,
        out_shape=out_shape,
        grid=(m // tm, n // tn, nk),
        in_specs=[a_spec, b_spec] + [pl.BlockSpec(memory_space=pl.ANY)] * n_after,
        out_specs=out_spec,
        scratch_shapes=[pltpu.VMEM((tm, tn), F32)] if nk > 1 else [],
        compiler_params=_params(("parallel", "parallel", "arbitrary")),
    )(a, b, *([after] * n_after))


ROW_WINDOW_BYTES = 12 * 1024 * 1024


def _rowwise(name, fn, ins, outs, rows, tm, after=None):
    per_row = sum(s[2] * s[1].dtype.itemsize for s in ins if s[0] == "row")
    per_row += sum(s[1] * jnp.dtype(s[2]).itemsize for s in outs if s[0] in ("row", "tail"))
    if per_row:
        tm = _tile(rows, min(tm, max(8, ROW_WINDOW_BYTES // (2 * per_row))), 8)
    n_in = len(ins)
    in_specs, args = [], []
    for spec in ins:
        if spec[0] == "row":
            _, arr, w, cb = spec
            in_specs.append(pl.BlockSpec((tm, w), functools.partial(lambda i, cb: (i, cb), cb=cb)))
        else:
            arr = spec[1]
            in_specs.append(pl.BlockSpec(arr.shape, lambda i: (0, 0)))
        args.append(arr)
    out_shape, out_specs = [], []
    for spec in outs:
        if spec[0] == "row":
            out_shape.append(jax.ShapeDtypeStruct((rows, spec[1]), spec[2]))
            out_specs.append(pl.BlockSpec((tm, spec[1]), lambda i: (i, 0)))
        elif spec[0] == "tail":
            out_shape.append(jax.ShapeDtypeStruct((rows - tm, spec[1]), spec[2]))
            out_specs.append(pl.BlockSpec((tm, spec[1]), lambda i: (jnp.maximum(i - 1, 0), 0)))
        elif spec[0] == "head":
            out_shape.append(jax.ShapeDtypeStruct((tm, spec[1]), spec[2]))
            out_specs.append(pl.BlockSpec((tm, spec[1]), lambda i: (0, 0)))
        else:
            out_shape.append(jax.ShapeDtypeStruct(spec[1], F32))
            out_specs.append(pl.BlockSpec(spec[1], lambda i: (0, 0)))
    has_acc = any(s[0] != "row" for s in outs)
    n_after = 0 if after is None else 1
    in_specs += [pl.BlockSpec(memory_space=pl.ANY)] * n_after
    args += [after] * n_after

    def body(*refs):
        i = pl.program_id(0)
        res = fn(i, tm, *[r[...] for r in refs[:n_in]])
        for spec, ref, val in zip(outs, refs[n_in + n_after:], res):
            if spec[0] in ("row", "tail"):
                ref[...] = val.astype(ref.dtype)
            elif spec[0] == "head":
                @pl.when(i == 0)
                def _(ref=ref, val=val):
                    ref[...] = val.astype(ref.dtype)
            else:
                @pl.when(i == 0)
                def _(ref=ref, val=val):
                    ref[...] = val

                @pl.when(i > 0)
                def _(ref=ref, val=val):
                    ref[...] += val

    return pl.pallas_call(
        body,
        name=name,
        out_shape=out_shape,
        grid=(rows // tm,),
        in_specs=in_specs,
        out_specs=out_specs,
        compiler_params=_params(("arbitrary" if has_acc else "parallel",)),
    )(*args)


def _rms_fwd(x, g):
    r = lax.rsqrt(jnp.mean(x * x, axis=-1, keepdims=True) + NORM_EPS)
    return x * r * g


def _rms_bwd(x, g, dy):
    r = lax.rsqrt(jnp.mean(x * x, axis=-1, keepdims=True) + NORM_EPS)
    xhat = x * r
    dxhat = dy * g
    dx = r * (dxhat - xhat * jnp.mean(dxhat * xhat, axis=-1, keepdims=True))
    return dx, jnp.sum(dy * xhat, axis=0, keepdims=True)


def _silu(x):
    return x * _sigmoid(x)


def _dsilu(x):
    s = _sigmoid(x)
    return s * (1.0 + x * (1.0 - s))


def _silu_both(x):
    s = _sigmoid(x)
    return x * s, s * (1.0 + x * (1.0 - s))


def _rot_src(x):
    lane = lax.broadcasted_iota(jnp.int32, x.shape, 1)
    return jnp.where(lane < ROPE // 2, pltpu.roll(x, LANE - ROPE // 2, 1), pltpu.roll(x, ROPE // 2, 1))


def _rope_fwd_call(q_raw, kv, proj, cos_t, sin_t, rows, tm):
    def fn(i, tm_, q, kvv, kr, c, s):
        kr_rot = kr[:, :LANE]
        kr_rot = kr_rot * c + _rot_src(kr_rot) * s
        qs, ks, vs = [], [], []
        for h in range(HEADS):
            qn = q[:, h * QHEAD_W:h * QHEAD_W + NOPE]
            qr = q[:, h * QHEAD_W + NOPE:(h + 1) * QHEAD_W]
            qs += [qn * SCORE_TO_LOG2, (qr * c + _rot_src(qr) * s) * SCORE_TO_LOG2]
            ks += [kvv[:, h * 2 * NOPE:h * 2 * NOPE + NOPE], kr_rot]
            vs += [kvv[:, h * 2 * NOPE + NOPE:(h + 1) * 2 * NOPE]]
        return jnp.concatenate(qs, axis=1), jnp.concatenate(ks, axis=1), jnp.concatenate(vs, axis=1)

    return _rowwise(
        "rope_fwd", fn,
        [("row", q_raw, HEADS * QHEAD_W, 0), ("row", kv, HEADS * 2 * NOPE, 0), ("row", proj, KR_W, SEG_KR // KR_W),
         ("row", cos_t, LANE, 0), ("row", sin_t, LANE, 0)],
        [("row", HEADS * QHEAD_W, BF16), ("row", HEADS * QHEAD_W, BF16), ("row", HEADS * VDIM, BF16)],
        rows, tm)


def _rope_bwd_call(dq_att, dk_att, dv, cos_t, sin_t, rows, tm):
    def fn(i, tm_, dq, dk, dvv, c, s):
        qs, kvs = [], []
        dkr = jnp.zeros((dq.shape[0], LANE), F32)
        for h in range(HEADS):
            dqr = dq[:, h * QHEAD_W + NOPE:(h + 1) * QHEAD_W] * ATTN_SCALE
            qs += [dq[:, h * QHEAD_W:h * QHEAD_W + NOPE] * ATTN_SCALE, dqr * c - _rot_src(dqr) * s]
            kvs += [dk[:, h * QHEAD_W:h * QHEAD_W + NOPE], dvv[:, h * VDIM:(h + 1) * VDIM]]
            dkr = dkr + dk[:, h * QHEAD_W + NOPE:(h + 1) * QHEAD_W]
        dkr = dkr * c - _rot_src(dkr) * s
        return (jnp.concatenate(qs, axis=1), jnp.concatenate(kvs, axis=1),
                jnp.concatenate([dkr, jnp.zeros_like(dkr)], axis=1))

    return _rowwise(
        "rope_bwd", fn,
        [("row", dq_att, HEADS * QHEAD_W, 0), ("row", dk_att, HEADS * QHEAD_W, 0), ("row", dv, HEADS * VDIM, 0),
         ("row", cos_t, LANE, 0), ("row", sin_t, LANE, 0)],
        [("row", HEADS * QHEAD_W, BF16), ("row", HEADS * 2 * NOPE, BF16), ("row", KR_W, BF16)],
        rows, tm)


def _attn_mask(q_blk, k_blk, t, keys_on_rows=False):
    qa, ka = (1, 0) if keys_on_rows else (0, 1)
    qs = q_blk * t + lax.broadcasted_iota(jnp.int32, (t, t), qa)
    ks = k_blk * t + lax.broadcasted_iota(jnp.int32, (t, t), ka)
    return (ks <= qs) & ((ks >= PAD_LEN) | (ks == qs))


_NT = _DIMS["nt"]
_TN = _DIMS["tn"]
LOG2E = 1.4426950408889634
SCORE_TO_LOG2 = ATTN_SCALE * LOG2E


def _causal_pairs(nb, by_key):
    if by_key:
        pairs = [(qi, kj) for kj in range(nb) for qi in range(kj, nb)]
    else:
        pairs = [(qi, kj) for qi in range(nb) for kj in range(qi + 1)]
    return (jnp.asarray(np.array([p[0] for p in pairs], np.int32)), jnp.asarray(np.array([p[1] for p in pairs], np.int32)))


def _two_parts(t):
    cut = (t // LANE + 1) // 2 * LANE
    return ((0, cut), (cut, t)) if cut < t else ((0, t),)


def _attn_fwd(q_att, k_att, v, rows):
    t = _tile(rows, 640, LANE)
    nb = rows // t

    def body(qt_ref, kt_ref, q_ref, k_ref, v_ref, o32_ref, obf_ref, lse_ref, m_sc, l_sc, acc_sc):
        qi, kj = qt_ref[pl.program_id(1)], kt_ref[pl.program_id(1)]

        @pl.when(kj == 0)
        def _():
            m_sc[...] = jnp.full_like(m_sc, NEG_BIG)
            l_sc[...] = jnp.zeros_like(l_sc)
            acc_sc[...] = jnp.zeros_like(acc_sc)

        def step(masked):
            q = q_ref[...]
            parts = _two_parts(t)
            scores =[lax.dot_general(q, k_ref[lo:hi, :], _NT, preferred_element_type=F32) for lo, hi in parts]
            m, l, acc = m_sc[...], l_sc[...], acc_sc[...]
            for (lo, hi), s in zip(parts, scores):
                if masked:
                    qs = qi * t + lax.broadcasted_iota(jnp.int32, (t, hi - lo), 0)
                    ks = kj * t + lo + lax.broadcasted_iota(jnp.int32, (t, hi - lo), 1)
                    s = jnp.where((ks <= qs) & ((ks >= PAD_LEN) | (ks == qs)), s, NEG_BIG)
                m_new = jnp.maximum(m, jnp.max(s, axis=1, keepdims=True))
                alpha = jnp.exp2(m - m_new)
                p = jnp.exp2(s - jnp.tile(m_new, (1, (hi - lo) // LANE)))
                l = alpha * l + jnp.sum(p, axis=1, keepdims=True)
                acc = alpha * acc + jnp.dot(p.astype(BF16), v_ref[lo:hi, :], preferred_element_type=F32)
                m = m_new
            m_sc[...], l_sc[...], acc_sc[...] = m, l, acc

        pl.when((kj == qi) | (kj == 0))(functools.partial(step, True))
        pl.when((kj < qi) & (kj > 0))(functools.partial(step, False))

        @pl.when(kj == qi)
        def _():
            o = acc_sc[...] / l_sc[...]
            o32_ref[...] = o
            obf_ref[...] = o.astype(BF16)
            lse_ref[0] = jnp.max((m_sc[...] + jnp.log2(l_sc[...])).T, axis=0, keepdims=True)

    qt, kt = _causal_pairs(nb, by_key=False)
    qmap = lambda h, p, qt_ref, kt_ref: (qt_ref[p], h)
    kmap = lambda h, p, qt_ref, kt_ref: (kt_ref[p], h)
    return pl.pallas_call(
        body,
        name="attn_fwd",
        out_shape=[jax.ShapeDtypeStruct((rows, HEADS * VDIM), F32), jax.ShapeDtypeStruct((rows, HEADS * VDIM), BF16),
                   jax.ShapeDtypeStruct((HEADS, 1, rows), F32)],
        grid_spec=pltpu.PrefetchScalarGridSpec(
            num_scalar_prefetch=2,
            grid=(HEADS, len(qt)),
            in_specs=[pl.BlockSpec((t, QHEAD_W), qmap), pl.BlockSpec((t, QHEAD_W), kmap), pl.BlockSpec((t, VDIM), kmap)],
            out_specs=[pl.BlockSpec((t, VDIM), qmap), pl.BlockSpec((t, VDIM), qmap),
                       pl.BlockSpec((1, 1, t), lambda h, p, qt_ref, kt_ref: (h, 0, qt_ref[p]))],
            scratch_shapes=[pltpu.VMEM((t, LANE), F32), pltpu.VMEM((t, LANE), F32), pltpu.VMEM((t, VDIM), F32)]),
        compiler_params=_params(("parallel", "arbitrary")),
    )(qt, kt, q_att, k_att, v)


def _attn_delta(do, o32, rows, tm):
    def fn(i, tm_, dov, ov):
        prod = dov.astype(F32) * ov
        head_of = lax.broadcasted_iota(jnp.int32, (HEADS * VDIM, LANE), 0) // VDIM
        pick = jnp.where(head_of == lax.broadcasted_iota(jnp.int32, (HEADS * VDIM, LANE), 1), 1.0, 0.0).astype(F32)
        return (jnp.dot(prod, pick, precision=lax.Precision.HIGHEST, preferred_element_type=F32),)

    (delta,) = _rowwise("attn_delta", fn, [("row", do, HEADS * VDIM, 0), ("row", o32, HEADS * VDIM, 0)],
                        [("row", LANE, F32)], rows, tm)
    return delta


def _attn_bwd(q_att, k_att, v, do, lse_row, delta_row, rows):
    t = _tile(rows, 640, LANE)
    nb = rows // t

    def body(qt_ref, kt_ref, q_ref, k_ref, v_ref, do_ref, lse_ref, delta_ref, dq_ref, dk_ref, dv_ref, dk_sc, dv_sc):
        qi, kj = qt_ref[pl.program_id(1)], kt_ref[pl.program_id(1)]

        @pl.when(pl.program_id(1) == 0)
        def _():
            dq_ref[...] = jnp.zeros_like(dq_ref)

        @pl.when(qi == kj)
        def _():
            dk_sc[...] = jnp.zeros_like(dk_sc)
            dv_sc[...] = jnp.zeros_like(dv_sc)

        def step(masked):
            k, vv = k_ref[...], v_ref[...]
            parts = _two_parts(t)
            st_all = [lax.dot_general(k, q_ref[lo:hi, :], _NT, preferred_element_type=F32) for lo, hi in parts]
            dpt_all = [lax.dot_general(vv, do_ref[lo:hi, :], _NT, preferred_element_type=F32) for lo, hi in parts]
            dk, dv = dk_sc[...], dv_sc[...]
            for (lo, hi), st, dpt in zip(parts, st_all, dpt_all):
                pt = jnp.exp2(st - lse_ref[0, :, lo:hi])
                if masked:
                    ks = kj * t + lax.broadcasted_iota(jnp.int32, (t, hi - lo), 0)
                    qs = qi * t + lo + lax.broadcasted_iota(jnp.int32, (t, hi - lo), 1)
                    pt = jnp.where((ks <= qs) & ((ks >= PAD_LEN) | (ks == qs)), pt, 0.0)
                dv = dv + jnp.dot(pt.astype(BF16), do_ref[lo:hi, :], preferred_element_type=F32)
                dst = (pt * (dpt - delta_ref[0, :, lo:hi])).astype(BF16)
                dk = dk + jnp.dot(dst, q_ref[lo:hi, :], preferred_element_type=F32)
                q_rows = pl.ds(pl.multiple_of(qi * t + lo, LANE), hi - lo)
                dq_ref[q_rows, :] += lax.dot_general(dst, k, _TN, preferred_element_type=F32)
            dk_sc[...], dv_sc[...] = dk, dv

        pl.when((qi == kj) | (kj == 0))(functools.partial(step, True))
        pl.when((qi > kj) & (kj > 0))(functools.partial(step, False))

        @pl.when(qi == nb - 1)
        def _():
            dk_ref[...] = dk_sc[...] * (1.0 / LOG2E)
            dv_ref[...] = dv_sc[...]

    qt, kt = _causal_pairs(nb, by_key=True)
    qmap = lambda h, p, qt_ref, kt_ref: (qt_ref[p], h)
    kmap = lambda h, p, qt_ref, kt_ref: (kt_ref[p], h)
    stat = pl.BlockSpec((1, 1, t), lambda h, p, qt_ref, kt_ref: (h, 0, qt_ref[p]))
    return pl.pallas_call(
        body,
        name="attn_bwd",
        out_shape=[jax.ShapeDtypeStruct((rows, HEADS * QHEAD_W), F32), jax.ShapeDtypeStruct((rows, HEADS * QHEAD_W), F32),
                   jax.ShapeDtypeStruct((rows, HEADS * VDIM), F32)],
        grid_spec=pltpu.PrefetchScalarGridSpec(
            num_scalar_prefetch=2,
            grid=(HEADS, len(qt)),
            in_specs=[pl.BlockSpec((t, QHEAD_W), qmap), pl.BlockSpec((t, QHEAD_W), kmap), pl.BlockSpec((t, VDIM), kmap),
                      pl.BlockSpec((t, VDIM), qmap), stat, stat],
            out_specs=[pl.BlockSpec((rows, QHEAD_W), lambda h, p, qt_ref, kt_ref: (0, h)),
                       pl.BlockSpec((t, QHEAD_W), kmap), pl.BlockSpec((t, VDIM), kmap)],
            scratch_shapes=[pltpu.VMEM((t, QHEAD_W), F32), pltpu.VMEM((t, VDIM), F32)]),
        compiler_params=_params(("parallel", "arbitrary")),
    )(qt, kt, q_att, k_att, v, do, lse_row, delta_row)


C = BLOCK


def _hgrn_prep(hq, hf, hi, lb, c):
    rows = c * C + lax.broadcasted_iota(jnp.int32, (C, C), 0)
    valid = rows >= PAD_LEN
    sg = _sigmoid(hf)
    f = lb + (1.0 - lb) * sg
    g = jnp.where(valid, jnp.log(f), 0.0)
    k = jnp.where(valid, 1.0 - f, 0.0)
    q = _silu(hq)
    r = lax.broadcasted_iota(jnp.int32, (C, C), 0)
    cc = lax.broadcasted_iota(jnp.int32, (C, C), 1)
    tri = jnp.where(cc <= r, 1.0, 0.0).astype(F32)
    b = jnp.dot(tri, g, precision=lax.Precision.HIGHEST, preferred_element_type=F32)
    return q, k, hi, b, f, sg, valid


def _last_row_as_col(b_t):
    lane = lax.broadcasted_iota(jnp.int32, b_t.shape, 1)
    return jnp.sum(jnp.where(lane == C - 1, b_t, 0.0), axis=1, keepdims=True)


def _k_scaled(k, b, bs):
    return (k * jnp.exp(jnp.minimum(bs - b, 0.0))).astype(BF16)


def _hgrn_fwd(proj, lb, rows):
    nc = rows // C

    def body(hq_ref, hf_ref, hi_ref, lb_ref, o_ref, a_ref, s_ref, s_sc, b_sc):
        c = pl.program_id(1)

        @pl.when(c == 0)
        def _():
            s_sc[...] = jnp.zeros_like(s_sc)

        q, k, v, b, _, _, _ = _hgrn_prep(hq_ref[...], hf_ref[...], hi_ref[...], lb_ref[...], c)
        b_sc[...] = b
        s0 = s_sc[...]
        s_ref[0, 0] = s0
        v_bf = v.astype(BF16)
        r16 = lax.broadcasted_iota(jnp.int32, (SUB, C), 0)
        c16 = lax.broadcasted_iota(jnp.int32, (SUB, C), 1)
        slabs = [jnp.zeros((SUB, C), F32)]
        for i in range(1, C // SUB):
            bs = b_sc[SUB * i - 1:SUB * i, :]
            qs = (q[SUB * i:SUB * (i + 1)] * jnp.exp(b[SUB * i:SUB * (i + 1)] - bs)).astype(BF16)
            a_i = lax.dot_general(qs, _k_scaled(k, b, bs), _NT, preferred_element_type=F32)
            slabs.append(jnp.where(c16 <= r16 + (SUB * i - SUB), a_i, 0.0))
        a_off = jnp.concatenate(slabs, axis=0)
        q_t, k_t, b_t = q.T, k.T, b.T
        sub = lax.broadcasted_iota(jnp.int32, (C, C), 0)
        lane = lax.broadcasted_iota(jnp.int32, (C, C), 1)
        lane1 = lax.broadcasted_iota(jnp.int32, (1, C), 1)
        at_band = jnp.zeros((C, C), F32)
        ahead = lane - sub
        for dl in range(SUB):
            k_s = pltpu.roll(k_t, dl, 1) if dl else k_t
            b_s = pltpu.roll(b_t, dl, 1) if dl else b_t
            e = jnp.exp(b_t - b_s)
            band = jnp.sum(q_t * k_s * e, axis=0, keepdims=True)
            band = jnp.where(lane1 >= dl, band, 0.0)
            at_band = at_band + jnp.where(ahead == dl, jnp.broadcast_to(band, (C, C)), 0.0)
        a = (a_off + at_band.T).astype(BF16)
        a_ref[0] = a
        qe = (q * jnp.exp(b)).astype(BF16)
        o_ref[...] = (jnp.dot(a, v_bf, preferred_element_type=F32)
                      + jnp.dot(qe, s0.astype(BF16), preferred_element_type=F32))
        b_last = b_sc[C - 1:C, :]
        kd = (k * jnp.exp(b_last - b)).astype(BF16)
        s_sc[...] = (jnp.exp(_last_row_as_col(b_t)) * s0
                     + lax.dot_general(kd, v_bf, _TN, preferred_element_type=F32))

    seg = lambda base: (lambda h, c: (c, base // C + h))
    return pl.pallas_call(
        body,
        name="hgrn_fwd",
        out_shape=[jax.ShapeDtypeStruct((rows, D_MODEL), F32), jax.ShapeDtypeStruct((HEADS, rows, C), BF16),
                   jax.ShapeDtypeStruct((HEADS, nc, C, C), F32)],
        grid=(HEADS, nc),
        in_specs=[pl.BlockSpec((C, C), seg(SEG_HQ)), pl.BlockSpec((C, C), seg(SEG_HF)), pl.BlockSpec((C, C), seg(SEG_HI)),
                  pl.BlockSpec((1, C), lambda h, c: (0, h))],
        out_specs=[pl.BlockSpec((C, C), lambda h, c: (c, h)), pl.BlockSpec((1, C, C), lambda h, c: (h, c, 0)),
                   pl.BlockSpec((1, 1, C, C), lambda h, c: (h, c, 0, 0))],
        scratch_shapes=[pltpu.VMEM((C, C), F32), pltpu.VMEM((C, C), F32)],
        compiler_params=_params(("parallel", "arbitrary")),
    )(proj, proj, proj, lb)


def _hgrn_bwd(proj, lb, a_mat, s_states, do_h, rows):
    nc = rows // C

    def body(hq_ref, hf_ref, hi_ref, lb_ref, a_ref, s_ref, do_ref, dhq_ref, dhf_ref, dhi_ref, dlb_ref, ds_sc, b_sc):
        step = pl.program_id(1)
        c = nc - 1 - step

        @pl.when(step == 0)
        def _():
            ds_sc[...] = jnp.zeros_like(ds_sc)
            dlb_ref[...] = jnp.zeros_like(dlb_ref)

        hq, hf = hq_ref[...], hf_ref[...]
        lb_row = lb_ref[...]
        q, k, v, b, f, sg, valid = _hgrn_prep(hq, hf, hi_ref[...], lb_row, c)
        b_sc[...] = b
        s0 = s_ref[0, 0]
        ds1 = ds_sc[...]
        s0_bf, ds1_bf = s0.astype(BF16), ds1.astype(BF16)
        do = do_ref[...]
        do_bf, v_bf = do.astype(BF16), v.astype(BF16)
        b_last = b_sc[C - 1:C, :]
        e_last = jnp.exp(b_last - b)
        eb = jnp.exp(b)
        sub = lax.broadcasted_iota(jnp.int32, (C, C), 0)
        lane = lax.broadcasted_iota(jnp.int32, (C, C), 1)
        r16 = lax.broadcasted_iota(jnp.int32, (SUB, C), 0)
        c16 = lax.broadcasted_iota(jnp.int32, (SUB, C), 1)

        dv = (lax.dot_general(a_ref[0], do_bf, _TN, preferred_element_type=F32)
              + jnp.dot((k * e_last).astype(BF16), ds1_bf, preferred_element_type=F32))
        da = jnp.where(lane <= sub, lax.dot_general(do_bf, v_bf, _NT, preferred_element_type=F32), 0.0)
        da_t = jnp.where(sub <= lane, lax.dot_general(v_bf, do_bf, _NT, preferred_element_type=F32), 0.0)

        dq_slabs = [jnp.zeros((SUB, C), F32)]
        for i in range(1, C // SUB):
            bs = b_sc[SUB * i - 1:SUB * i, :]
            da_i = jnp.where(c16 <= r16 + (SUB * i - SUB), da[SUB * i:SUB * (i + 1)], 0.0).astype(BF16)
            dq_slabs.append(jnp.exp(b[SUB * i:SUB * (i + 1)] - bs)
                            * jnp.dot(da_i, _k_scaled(k, b, bs), preferred_element_type=F32))
        dk_slabs = []
        for j in range(C // SUB - 1):
            be = b_sc[SUB * j + SUB - 1:SUB * (j + 1), :]
            qe_j = (q * jnp.exp(jnp.minimum(b - be, 0.0))).astype(BF16)
            da_j = jnp.where(c16 >= r16 + (SUB * j + SUB), da_t[SUB * j:SUB * (j + 1)], 0.0).astype(BF16)
            dk_slabs.append(jnp.exp(be - b[SUB * j:SUB * (j + 1)]) * jnp.dot(da_j, qe_j, preferred_element_type=F32))
        dk_slabs.append(jnp.zeros((SUB, C), F32))

        q_t, k_t, b_t = q.T, k.T, b.T
        lane1 = lax.broadcasted_iota(jnp.int32, (1, C), 1)
        dq_t = jnp.zeros((C, C), F32)
        dk_t = jnp.zeros((C, C), F32)
        ahead = lane - sub
        for dl in range(SUB):
            k_s = pltpu.roll(k_t, dl, 1) if dl else k_t
            b_s = pltpu.roll(b_t, dl, 1) if dl else b_t
            e = jnp.exp(jnp.minimum(b_t - b_s, 0.0))
            dband = jnp.sum(jnp.where(ahead == dl, da_t, 0.0), axis=0, keepdims=True)
            w = jnp.where(lane1 >= dl, dband, 0.0) * e
            dq_t = dq_t + w * k_s
            back = w * q_t
            dk_t = dk_t + (pltpu.roll(back, C - dl, 1) if dl else back)

        dq = eb * lax.dot_general(do_bf, s0_bf, _NT, preferred_element_type=F32) + jnp.concatenate(dq_slabs, axis=0) + dq_t.T
        dk_inter = e_last * lax.dot_general(v_bf, ds1_bf, _NT, preferred_element_type=F32)
        dk = dk_inter + jnp.concatenate(dk_slabs, axis=0) + dk_t.T

        extra = (jnp.exp(b_last) * jnp.sum((s0 * ds1).T, axis=0, keepdims=True)
                 + jnp.sum(k * dk_inter, axis=0, keepdims=True))
        db = q * dq - k * dk + jnp.where(sub == C - 1, jnp.broadcast_to(extra, (C, C)), 0.0)
        tri_t = jnp.where(lane >= sub, 1.0, 0.0).astype(F32)
        dg = jnp.dot(tri_t, db, precision=lax.Precision.HIGHEST, preferred_element_type=F32)
        ds_sc[...] = (jnp.exp(_last_row_as_col(b_t)) * ds1
                      + lax.dot_general((q * eb).astype(BF16), do_bf, _TN, preferred_element_type=F32))

        df = jnp.where(valid, dg / f - dk, 0.0)
        dhf_ref[...] = (df * (1.0 - lb_row) * sg * (1.0 - sg)).astype(BF16)
        dlb_ref[...] += jnp.sum(df * (1.0 - sg), axis=0, keepdims=True)
        dhq_ref[...] = (dq * _dsilu(hq)).astype(BF16)
        dhi_ref[...] = dv.astype(BF16)

    seg = lambda base: (lambda h, s: (nc - 1 - s, base // C + h))
    rmap = lambda h, s: (nc - 1 - s, h)
    return pl.pallas_call(
        body,
        name="hgrn_bwd",
        out_shape=[jax.ShapeDtypeStruct((rows, D_MODEL), BF16)] * 3 + [jax.ShapeDtypeStruct((1, D_MODEL), F32)],
        grid=(HEADS, nc),
        in_specs=[pl.BlockSpec((C, C), seg(SEG_HQ)), pl.BlockSpec((C, C), seg(SEG_HF)), pl.BlockSpec((C, C), seg(SEG_HI)),
                  pl.BlockSpec((1, C), lambda h, s: (0, h)),
                  pl.BlockSpec((1, C, C), lambda h, s: (h, nc - 1 - s, 0)),
                  pl.BlockSpec((1, 1, C, C), lambda h, s: (h, nc - 1 - s, 0, 0)),
                  pl.BlockSpec((C, C), rmap)],
        out_specs=[pl.BlockSpec((C, C), rmap)] * 3 + [pl.BlockSpec((1, C), lambda h, s: (0, h))],
        scratch_shapes=[pltpu.VMEM((C, C), F32), pltpu.VMEM((C, C), F32)],
        compiler_params=_params(("parallel", "arbitrary")),
    )(proj, proj, proj, lb, a_mat, s_states, do_h)


CONV_TC = 512
HALO = 16


def _halo_row(block, k):
    r = lax.broadcasted_iota(jnp.int32, block.shape, 0)
    return jnp.sum(jnp.where(r == k, block, 0.0), axis=0, keepdims=True)


def _conv_taps(i, tm, g_ref, pg_ref):
    shape = g_ref.shape
    r = lax.broadcasted_iota(jnp.int32, shape, 0)
    g = jnp.where(i * tm + r >= PAD_LEN, g_ref[...].astype(F32), 0.0)
    prev = pg_ref[...].astype(F32)
    p1 = jnp.where(i * tm - 1 >= PAD_LEN, _halo_row(prev, HALO - 1), 0.0)
    p2 = jnp.where(i * tm - 2 >= PAD_LEN, _halo_row(prev, HALO - 2), 0.0)
    s1 = jnp.where(r == 0, p1, pltpu.roll(g, 1, 0))
    s2 = jnp.where(r == 0, p2, jnp.where(r == 1, p1, pltpu.roll(g, 2, 0)))
    return g, s1, s2


def _conv_specs(tm, tc, ncb, order):
    gate = pl.BlockSpec((tm, tc), lambda *ids: order(ids))
    halo = pl.BlockSpec((HALO, tc), lambda *ids: (jnp.maximum(order(ids)[0] * (tm // HALO) - 1, 0), order(ids)[1]))
    up = pl.BlockSpec((tm, tc), lambda *ids: (order(ids)[0], ncb + order(ids)[1]))
    return gate, halo, up


def _conv_fwd(ffn, conv_w, conv_b, rows, tm):
    tc = CONV_TC
    ncb = D_FF // tc

    def body(g_ref, pg_ref, up_ref, cw_ref, cb_ref, act_ref):
        i = pl.program_id(0)
        g, s1, s2 = _conv_taps(i, tm, g_ref, pg_ref)
        conv = (cw_ref[0:1, :] * s2 + cw_ref[1:2, :] * s1 + cw_ref[2:3, :] * g) + cb_ref[...]
        act_ref[...] = (_silu(conv) * up_ref[...].astype(F32)).astype(BF16)

    gate, halo, up = _conv_specs(tm, tc, ncb, lambda ids: (ids[0], ids[1]))
    return pl.pallas_call(
        body,
        name="conv_fwd",
        out_shape=jax.ShapeDtypeStruct((rows, D_FF), BF16),
        grid=(rows // tm, ncb),
        in_specs=[gate, halo, up, pl.BlockSpec((3, tc), lambda i, j: (0, j)), pl.BlockSpec((1, tc), lambda i, j: (0, j))],
        out_specs=pl.BlockSpec((tm, tc), lambda i, j: (i, j)),
        compiler_params=_params(("parallel", "parallel")),
    )(ffn, ffn, ffn, conv_w, conv_b)


def _conv_bwd_a(ffn, dact, conv_w, conv_b, rows, tm):
    tc = CONV_TC
    ncb = D_FF // tc

    def body(g_ref, pg_ref, up_ref, da_ref, cw_ref, cb_ref, dc_ref, dffn_ref, w0_ref, w1_ref, w2_ref, db_ref):
        i = pl.program_id(1)
        g, s1, s2 = _conv_taps(i, tm, g_ref, pg_ref)
        conv = (cw_ref[0:1, :] * s2 + cw_ref[1:2, :] * s1 + cw_ref[2:3, :] * g) + cb_ref[...]
        da = da_ref[...].astype(F32)
        act, dact_dconv = _silu_both(conv)
        dffn_ref[...] = (da * act).astype(BF16)
        dc = da * up_ref[...].astype(F32) * dact_dconv
        dc_ref[...] = dc.astype(BF16)
        sums = [jnp.sum(dc * s2, axis=0, keepdims=True), jnp.sum(dc * s1, axis=0, keepdims=True),
                jnp.sum(dc * g, axis=0, keepdims=True), jnp.sum(dc, axis=0, keepdims=True)]
        for ref, val in zip((w0_ref, w1_ref, w2_ref, db_ref), sums):
            @pl.when(i == 0)
            def _(ref=ref, val=val):
                ref[...] = val

            @pl.when(i > 0)
            def _(ref=ref, val=val):
                ref[...] += val

    gate, halo, up = _conv_specs(tm, tc, ncb, lambda ids: (ids[1], ids[0]))
    col = pl.BlockSpec((1, tc), lambda j, i: (0, j))
    return pl.pallas_call(
        body,
        name="conv_bwd_a",
        out_shape=[jax.ShapeDtypeStruct((rows, D_FF), BF16), jax.ShapeDtypeStruct((rows, 2 * D_FF), BF16)]
        + [jax.ShapeDtypeStruct((1, D_FF), F32)] * 4,
        grid=(ncb, rows // tm),
        in_specs=[gate, halo, up, pl.BlockSpec((tm, tc), lambda j, i: (i, j)),
                  pl.BlockSpec((3, tc), lambda j, i: (0, j)), col],
        out_specs=[pl.BlockSpec((tm, tc), lambda j, i: (i, j)), pl.BlockSpec((tm, tc), lambda j, i: (i, ncb + j)),
                   col, col, col, col],
        compiler_params=_params(("parallel", "arbitrary")),
    )(ffn, ffn, ffn, dact, conv_w, conv_b)


def _conv_bwd_b(dconv, conv_w, dffn, rows, tm):
    tc = CONV_TC
    ncb = D_FF // tc
    nrb = rows // tm

    def body(dc_ref, nx_ref, cw_ref, dffn_in, out_ref):
        del dffn_in
        i = pl.program_id(0)
        dc = dc_ref[...].astype(F32)
        r = lax.broadcasted_iota(jnp.int32, dc.shape, 0)
        last = i == nrb - 1
        nxt = nx_ref[...].astype(F32)
        x1 = jnp.where(last, 0.0, _halo_row(nxt, 0))
        x2 = jnp.where(last, 0.0, _halo_row(nxt, 1))
        n1 = jnp.where(r == tm - 1, x1, pltpu.roll(dc, tm - 1, 0))
        n2 = jnp.where(r == tm - 1, x2, jnp.where(r == tm - 2, x1, pltpu.roll(dc, tm - 2, 0)))
        dg = cw_ref[2:3, :] * dc + cw_ref[1:2, :] * n1 + cw_ref[0:1, :] * n2
        out_ref[...] = jnp.where(i * tm + r >= PAD_LEN, dg, 0.0).astype(BF16)

    return pl.pallas_call(
        body,
        name="conv_bwd_b",
        out_shape=jax.ShapeDtypeStruct((rows, 2 * D_FF), BF16),
        grid=(nrb, ncb),
        in_specs=[pl.BlockSpec((tm, tc), lambda i, j: (i, j)),
                  pl.BlockSpec((HALO, tc), lambda i, j: (jnp.minimum((i + 1) * (tm // HALO), rows // HALO - 1), j)),
                  pl.BlockSpec((3, tc), lambda i, j: (0, j)),
                  pl.BlockSpec(memory_space=pl.ANY)],
        out_specs=pl.BlockSpec((tm, tc), lambda i, j: (i, j)),
        input_output_aliases={3: 0},
        compiler_params=_params(("parallel", "parallel")),
    )(dconv, dconv, conv_w, dffn)


def _final_call(h1, y, target, g_final, rows):
    tm = BLOCK

    def fn(i, tm_, h1v, yv, tgt, g):
        h2 = h1v + yv
        out = _rms_fwd(h2, g)
        err = jnp.where(i > 0, out - tgt, 0.0)
        loss = 0.5 * jnp.sum(jnp.mean(err * err, axis=-1, keepdims=True), axis=0, keepdims=True)
        dx, dg = _rms_bwd(h2, g, err * (1.0 / D_MODEL))
        return dx, dx, jnp.broadcast_to(loss, (1, LANE)), dg

    n_in = 4
    in_specs = [pl.BlockSpec((tm, D_MODEL), lambda i: (i, 0)), pl.BlockSpec((tm, D_MODEL), lambda i: (i, 0)),
                pl.BlockSpec((tm, D_MODEL), lambda i: (jnp.maximum(i - 1, 0), 0)),
                pl.BlockSpec((1, D_MODEL), lambda i: (0, 0))]

    def body(*refs):
        i = pl.program_id(0)
        dx, dx2, loss, dg = fn(i, tm, *[r[...] for r in refs[:n_in]])
        refs[4][...] = dx
        refs[5][...] = dx2.astype(BF16)
        for ref, val in ((refs[6], loss), (refs[7], dg)):
            @pl.when(i == 0)
            def _(ref=ref, val=val):
                ref[...] = val

            @pl.when(i > 0)
            def _(ref=ref, val=val):
                ref[...] += val

    return pl.pallas_call(
        body,
        name="final_loss",
        out_shape=[jax.ShapeDtypeStruct((rows, D_MODEL), F32), jax.ShapeDtypeStruct((rows, D_MODEL), BF16),
                   jax.ShapeDtypeStruct((1, LANE), F32), jax.ShapeDtypeStruct((1, D_MODEL), F32)],
        grid=(rows // tm,),
        in_specs=in_specs,
        out_specs=[pl.BlockSpec((tm, D_MODEL), lambda i: (i, 0)), pl.BlockSpec((tm, D_MODEL), lambda i: (i, 0)),
                   pl.BlockSpec((1, LANE), lambda i: (0, 0)), pl.BlockSpec((1, D_MODEL), lambda i: (0, 0))],
        compiler_params=_params(("arbitrary",)),
    )(h1, y, target, g_final)


def _heads_map(fn, *slabs):
    outs = [fn(*[s[:, h * LANE:(h + 1) * LANE] for s in slabs]) for h in range(HEADS)]
    if isinstance(outs[0], tuple):
        return tuple(jnp.concatenate([o[k] for o in outs], axis=1) for k in range(len(outs[0])))
    return jnp.concatenate(outs, axis=1)


def _local_step(x, positions, target, w, p, emit=None):
    kept = {}
    if emit is None:
        def emit(group):
            kept.update(group)
            return None
    s_len = x.shape[0]
    rows = s_len + BLOCK
    tm = _tile(rows, 640, 8)
    row = lambda arr, width, cb=0: ("row", arr, width, cb)

    h0 = jnp.concatenate([jnp.zeros((PAD_LEN, D_MODEL), F32), w["meta_tokens"], x], axis=0)
    pos = jnp.concatenate([jnp.zeros((PAD_LEN,), jnp.int32), jnp.arange(N_META, dtype=jnp.int32),
                           positions.astype(jnp.int32) + N_META])
    inv = 1.0 / (ROPE_THETA ** (jnp.arange(0, ROPE, 2, dtype=F32) / ROPE))
    ang = pos.astype(F32)[:, None] * inv
    zero = jnp.zeros((rows, LANE - ROPE), F32)
    cos_t = jnp.concatenate([jnp.cos(ang), jnp.cos(ang), zero], axis=1)
    sin_t = jnp.concatenate([-jnp.sin(ang), jnp.sin(ang), zero], axis=1)
    lb_r0, lb_r1 = p["lb_raw"][0:1], p["lb_raw"][1:2]

    def lb_fn(i, tm_, r0, r1):
        m = jnp.maximum(r0, r1)
        e0, e1 = jnp.exp(r0 - m), jnp.exp(r1 - m)
        return (e0 / (e0 + e1),)

    (lb,) = _rowwise("lb_fwd", lb_fn, [("bc", lb_r0), ("bc", lb_r1)], [("acc", (1, D_MODEL))], 1, 1)

    (u1,) = _rowwise("mix_norm", lambda i, t, h, g: (_rms_fwd(h, g),),
                     [row(h0, D_MODEL), ("bc", p["g_mix_norm"])], [("row", D_MODEL, BF16)], rows, tm)
    proj = _matmul(u1, w["w_in"], "nn", F32, "mm_proj")
    hint = getattr(w, "hint", lambda name, after: None)
    hint("w_q_up", proj)
    qn, kvn = _rowwise(
        "latent_norm", lambda i, t, ql, kl, gq, gk: (_rms_fwd(ql, gq), _rms_fwd(kl, gk)),
        [row(proj, Q_LORA, 0), row(proj, KV_LORA, SEG_KV_LAT // KV_LORA), ("bc", p["g_q_norm"]), ("bc", p["g_kv_norm"])],
        [("row", Q_LORA, BF16), ("row", KV_LORA, BF16)], rows, tm)
    q_raw = _matmul(qn, w["w_q_up"], "nn", F32, "mm_q_up")
    kv = _matmul(kvn, w["w_kv_up"], "nn", F32, "mm_kv_up")
    q_att, k_att, v_att = _rope_fwd_call(q_raw, kv, proj, cos_t, sin_t, rows, tm)
    o32, o_bf, lse = _attn_fwd(q_att, k_att, v_att, rows)
    hint("w_branch_mla", lse)
    o_h, a_mat, s_states = _hgrn_fwd(proj, lb, rows)

    def hgrn_post(i, t, oh, hg, g):
        return (_heads_map(lambda a, b: _rms_fwd(a, g) * _silu(b), oh, hg),)

    (o_hgrn,) = _rowwise("hgrn_post", hgrn_post,
                         [row(o_h, D_MODEL), row(proj, D_MODEL, SEG_HG // D_MODEL), ("bc", p["g_hgrn_norm"])],
                         [("row", D_MODEL, BF16)], rows, tm)
    br_a = _matmul(o_bf, w["w_branch_mla"], "nn", F32, "mm_branch_mla")
    br_b = _matmul(o_hgrn, w["w_branch_hgrn"], "nn", F32, "mm_branch_hgrn")
    (merged,) = _rowwise(
        "merge", lambda i, t, a, b, ga, gb: (_sigmoid(ga) * a + _sigmoid(gb) * b,),
        [row(br_a, D_MODEL), row(br_b, D_MODEL), row(proj, D_MODEL, SEG_GA // D_MODEL), row(proj, D_MODEL, SEG_GB // D_MODEL)],
        [("row", D_MODEL, BF16)], rows, tm)
    mix_out = _matmul(merged, w["w_out"], "nn", F32, "mm_out")

    def ffn_norm(i, t, h, mo, g):
        h1v = h + mo
        return h1v, _rms_fwd(h1v, g)

    h1, u2 = _rowwise("ffn_norm", ffn_norm, [row(h0, D_MODEL), row(mix_out, D_MODEL), ("bc", p["g_ffn_norm"])],
                      [("row", D_MODEL, F32), ("row", D_MODEL, BF16)], rows, tm)
    ffn = _matmul(u2, w["w_ffn_in"], "nn", BF16, "mm_ffn_in")
    act = _conv_fwd(ffn, w["conv_w"], p["conv_b"], rows, tm)
    y = _matmul(act, w["w_ffn_out"], "nn", F32, "mm_ffn_out")
    dh2, dh2_bf, loss_acc, dg_final = _final_call(h1, y, target, p["g_final_norm"].reshape(1, D_MODEL), rows)

    grads = {"g_final_norm": dg_final.reshape(D_MODEL)}
    dact = _matmul(dh2_bf, w["w_ffn_out"], "nt", BF16, "mm_d_act")
    grads["w_ffn_out"] = _matmul(act, dh2_bf, "tn", BF16, "mm_dw_ffn_out")
    dconv, dffn, dcw0, dcw1, dcw2, dcb = _conv_bwd_a(ffn, dact, w["conv_w"], p["conv_b"], rows, tm)
    dffn = _conv_bwd_b(dconv, w["conv_w"], dffn, rows, tm)
    grads["conv_w"] = jnp.concatenate([dcw0, dcw1, dcw2], axis=0)
    grads["conv_b"] = dcb
    du2 = _matmul(dffn, w["w_ffn_in"], "nt", F32, "mm_d_u2")
    grads["w_ffn_in"] = _matmul(u2, dffn, "tn", BF16, "mm_dw_ffn_in", col_blocks=N_DEV)

    def ffn_norm_bwd(i, t, h, du, dh, g):
        dx, dg = _rms_bwd(h, g, du)
        dh1v = dh + dx
        return dh1v, dh1v, dg

    dh1, dh1_bf, grads["g_ffn_norm"] = _rowwise(
        "ffn_norm_bwd", ffn_norm_bwd, [row(h1, D_MODEL), row(du2, D_MODEL), row(dh2, D_MODEL), ("bc", p["g_ffn_norm"])],
        [("row", D_MODEL, F32), ("row", D_MODEL, BF16), ("acc", (1, D_MODEL))], rows, tm)
    tok = emit({n: grads.pop(n) for n in ("w_ffn_out", "w_ffn_in", "conv_w", "conv_b", "g_final_norm", "g_ffn_norm")})
    dmerged = _matmul(dh1_bf, w["w_out"], "nt", F32, "mm_d_merged", after=tok)
    grads["w_out"] = _matmul(merged, dh1_bf, "tn", BF16, "mm_dw_out")

    def merge_bwd(i, t, dm, a, b, ga, gb):
        sa, sb = _sigmoid(ga), _sigmoid(gb)
        return dm * sa, dm * sb, dm * a * sa * (1.0 - sa), dm * b * sb * (1.0 - sb)

    da_bf, db_bf, dga, dgb = _rowwise(
        "merge_bwd", merge_bwd,
        [row(dmerged, D_MODEL), row(br_a, D_MODEL), row(br_b, D_MODEL),
         row(proj, D_MODEL, SEG_GA // D_MODEL), row(proj, D_MODEL, SEG_GB // D_MODEL)],
        [("row", D_MODEL, BF16)] * 4, rows, tm)
    do_mla = _matmul(da_bf, w["w_branch_mla"], "nt", BF16, "mm_d_o_mla")
    grads["w_branch_mla"] = _matmul(o_bf, da_bf, "tn", BF16, "mm_dw_branch_mla")
    do_hgrn = _matmul(db_bf, w["w_branch_hgrn"], "nt", F32, "mm_d_o_hgrn")
    grads["w_branch_hgrn"] = _matmul(o_hgrn, db_bf, "tn", BF16, "mm_dw_branch_hgrn")

    def hgrn_post_bwd(i, t, dy, oh, hg, g):
        def one(dyh, ohh, hgh):
            gate, dgate = _silu_both(hgh)
            dx, dg = _rms_bwd(ohh, g, dyh * gate)
            return dx, dyh * _rms_fwd(ohh, g) * dgate, dg

        dx, dhg, dg = _heads_map(one, dy, oh, hg)
        dg_sum = dg[:, 0:LANE]
        for h in range(1, HEADS):
            dg_sum = dg_sum + dg[:, h * LANE:(h + 1) * LANE]
        return dx, dhg, dg_sum

    tok = emit({n: grads.pop(n) for n in ("w_out", "w_branch_mla", "w_branch_hgrn")})
    do_h, dhg, grads["g_hgrn_norm"] = _rowwise(
        "hgrn_post_bwd", hgrn_post_bwd,
        [row(do_hgrn, D_MODEL), row(o_h, D_MODEL), row(proj, D_MODEL, SEG_HG // D_MODEL), ("bc", p["g_hgrn_norm"])],
        [("row", D_MODEL, F32), ("row", D_MODEL, BF16), ("acc", (1, LANE))], rows, tm, after=tok)
    dhq, dhf, dhi, dlb = _hgrn_bwd(proj, lb, a_mat, s_states, do_h, rows)

    def lb_bwd(i, tm_, d, l):
        t = d * l * (1.0 - l)
        return t, -t

    dlb0, dlb1 = _rowwise("lb_bwd", lb_bwd, [("bc", dlb), ("bc", lb)], [("acc", (1, D_MODEL))] * 2, 1, 1)
    grads["lb_raw"] = jnp.concatenate([dlb0, dlb1], axis=0)

    delta = _attn_delta(do_mla, o32, rows, tm)
    dq_att, dk_att, dv_att = _attn_bwd(q_att, k_att, v_att, do_mla, lse,
                                       jnp.transpose(delta[:, :HEADS]).reshape(HEADS, 1, rows), rows)
    dq_full, dkv, dkr = _rope_bwd_call(dq_att, dk_att, dv_att, cos_t, sin_t, rows, tm)
    dqn = _matmul(dq_full, w["w_q_up"], "nt", F32, "mm_d_qn")
    grads["w_q_up"] = _matmul(qn, dq_full, "tn", BF16, "mm_dw_q_up")
    dkvn = _matmul(dkv, w["w_kv_up"], "nt", F32, "mm_d_kvn")
    grads["w_kv_up"] = _matmul(kvn, dkv, "tn", BF16, "mm_dw_kv_up")

    def latent_norm_bwd(i, t, ql, kl, dq, dk, gq, gk):
        dql, dgq = _rms_bwd(ql, gq, dq)
        dkl, dgk = _rms_bwd(kl, gk, dk)
        return dql, dkl, dgq, dgk

    dq_lat, dkv_lat, grads["g_q_norm"], grads["g_kv_norm"] = _rowwise(
        "latent_norm_bwd", latent_norm_bwd,
        [row(proj, Q_LORA, 0), row(proj, KV_LORA, SEG_KV_LAT // KV_LORA), row(dqn, Q_LORA), row(dkvn, KV_LORA),
         ("bc", p["g_q_norm"]), ("bc", p["g_kv_norm"])],
        [("row", Q_LORA, BF16), ("row", KV_LORA, BF16), ("acc", (1, Q_LORA)), ("acc", (1, KV_LORA))], rows, tm)
    dproj = jnp.concatenate([dq_lat, dkv_lat, dhq, dhf, dhi, dhg, dga, dgb, dkr], axis=1)
    tok = emit({n: grads.pop(n) for n in ("w_q_up", "w_kv_up", "lb_raw", "g_q_norm", "g_kv_norm", "g_hgrn_norm")})
    grads["w_in"] = _matmul(u1, dproj, "tn", BF16, "mm_dw_in", after=tok)
    tok = emit({"w_in": grads.pop("w_in")})
    du1 = _matmul(dproj, w["w_in"], "nt", F32, "mm_d_u1", after=tok)

    def mix_norm_bwd(i, t, h, du, dh, g):
        dx, dg = _rms_bwd(h, g, du)
        return dh + dx, dh + dx, dg

    grad_x, d_prefix, grads["g_mix_norm"] = _rowwise(
        "mix_norm_bwd", mix_norm_bwd, [row(h0, D_MODEL), row(du1, D_MODEL), row(dh1, D_MODEL), ("bc", p["g_mix_norm"])],
        [("tail", D_MODEL, F32), ("head", D_MODEL, F32), ("acc", (1, D_MODEL))], rows, BLOCK)
    grads["meta_tokens"] = d_prefix[PAD_LEN:BLOCK]
    kept.update(grads)
    return loss_acc[0, 0], grad_x, kept


K_ROPE_AT = Q_LORA + KV_LORA
COL_SHARDED = ("w_in", "w_q_up", "w_kv_up", "w_ffn_in", "conv_w", "meta_tokens")
BIG = ("w_in", "w_q_up", "w_kv_up", "w_branch_mla", "w_branch_hgrn", "w_out", "w_ffn_in", "w_ffn_out")
SMALL = ("conv_b", "g_mix_norm", "g_q_norm", "g_kv_norm", "g_hgrn_norm", "g_ffn_norm", "g_final_norm", "lb_raw")


def _unshard(name, stacked):
    if name in COL_SHARDED:
        return jnp.transpose(stacked, (1, 0, 2)).reshape(stacked.shape[1], N_DEV * stacked.shape[2])
    return stacked.reshape(N_DEV * stacked.shape[1], stacked.shape[2])


def _reshard(name, full):
    if full.ndim == 3:
        return full
    if name in COL_SHARDED:
        r, c = full.shape
        return jnp.transpose(full.reshape(r, N_DEV, c // N_DEV), (1, 0, 2))
    return full.reshape(N_DEV, full.shape[0] // N_DEV, full.shape[1])


def _to_kernel_layout(full):
    out = dict(full)
    if "w_in" in full:
        w_in = full["w_in"]
        pad = jnp.zeros((D_MODEL, KR_W - ROPE), w_in.dtype)
        out["w_in"] = jnp.concatenate(
            [w_in[:, :K_ROPE_AT], w_in[:, K_ROPE_AT + ROPE:], w_in[:, K_ROPE_AT:K_ROPE_AT + ROPE], pad], axis=1)
    if "w_q_up" in full:
        wq = full["w_q_up"].reshape(Q_LORA, HEADS, NOPE + ROPE)
        out["w_q_up"] = jnp.pad(wq, ((0, 0), (0, 0), (0, QHEAD_W - NOPE - ROPE))).reshape(Q_LORA, HEADS * QHEAD_W)
    return out


def _from_kernel_layout(grads):
    out = dict(grads)
    if "w_in" in grads:
        g = grads["w_in"]
        out["w_in"] = jnp.concatenate([g[:, :K_ROPE_AT], g[:, SEG_KR:SEG_KR + ROPE], g[:, K_ROPE_AT:SEG_KR]], axis=1)
    if "w_q_up" in grads:
        g = grads["w_q_up"].reshape(Q_LORA, HEADS, QHEAD_W)
        out["w_q_up"] = g[:, :, :NOPE + ROPE].reshape(Q_LORA, HEADS * (NOPE + ROPE))
    return out


MESH_ID = pl.DeviceIdType.MESH
ANY = pl.BlockSpec(memory_space=pl.ANY)


def _slot(dev):
    return 4 * dev[0] + 2 * dev[1] + dev[2]


def _all_gather(shards):
    n = len(shards)

    def body(*refs):
        ins, outs = refs[:n], refs[n:2 * n]
        send_sems, recv_sems, local_sems = refs[2 * n:]
        x, y, c = lax.axis_index("x"), lax.axis_index("y"), lax.axis_index("c")
        me, sibling = (x, y, c), (x, y, 1 - c)
        chips = [(1 - x, y), (x, 1 - y), (1 - x, 1 - y)]

        def copy(a, k, block, to, src=None):
            dst = outs[a].at[_slot(block)]
            return pltpu.make_async_remote_copy(
                src_ref=dst if src is None else src, dst_ref=dst, send_sem=send_sems.at[a, k],
                recv_sem=recv_sems.at[a, k], device_id=to, device_id_type=MESH_ID)

        mine = [pltpu.make_async_copy(ins[a], outs[a].at[_slot(me)], local_sems.at[a]) for a in range(n)]
        for cp in mine:
            cp.start()
        first = []
        for a in range(n):
            first.append(copy(a, 0, me, sibling, src=ins[a]))
            first += [copy(a, 1 + j, me, (*chip, c), src=ins[a]) for j, chip in enumerate(chips)]
        for cp in first:
            cp.start()
        passed = []
        for a in range(n):
            for j, chip in enumerate(chips):
                copy(a, 1 + j, (*chip, c), me).wait_recv()
                fwd = copy(a, 4 + j, (*chip, c), sibling)
                fwd.start()
                passed.append(fwd)
        for a in range(n):
            copy(a, 0, sibling, me).wait_recv()
            for j, chip in enumerate(chips):
                copy(a, 4 + j, (*chip, 1 - c), me).wait_recv()
        for cp in first + passed:
            cp.wait_send()
        for cp in mine:
            cp.wait()

    return pl.pallas_call(
        body,
        name="gather_weights",
        out_shape=[jax.ShapeDtypeStruct((N_DEV,) + s.shape, s.dtype) for s in shards],
        in_specs=[ANY] * n,
        out_specs=[ANY] * n,
        scratch_shapes=[pltpu.SemaphoreType.DMA((n, 7)), pltpu.SemaphoreType.DMA((n, 7)), pltpu.SemaphoreType.DMA((n,))],
    )(*shards)


def _exchange(blocked, replicated):
    nb, n = len(blocked), len(blocked) + len(replicated)
    arrays = list(blocked) + list(replicated)

    def body(*refs):
        ins, outs = refs[:n], refs[n:2 * n]
        send_sems, recv_sems, local_sems = refs[2 * n:]
        x, y, c = lax.axis_index("x"), lax.axis_index("y"), lax.axis_index("c")
        me = (x, y, c)
        peers = [(x, y, 1 - c), (1 - x, y, c), (x, 1 - y, c), (1 - x, 1 - y, c),
                 (1 - x, y, 1 - c), (x, 1 - y, 1 - c), (1 - x, 1 - y, 1 - c)]

        def src_of(a, dev):
            return ins[a].at[_slot(dev)] if a < nb else ins[a]

        def copy(a, k, frm, to):
            return pltpu.make_async_remote_copy(
                src_ref=src_of(a, to), dst_ref=outs[a].at[_slot(frm)], send_sem=send_sems.at[a, k],
                recv_sem=recv_sems.at[a, k], device_id=to, device_id_type=MESH_ID)

        mine = [pltpu.make_async_copy(src_of(a, me), outs[a].at[_slot(me)], local_sems.at[a]) for a in range(n)]
        for cp in mine:
            cp.start()
        sends = [copy(a, k, me, peer) for a in range(n) for k, peer in enumerate(peers)]
        for cp in sends:
            cp.start()
        for a in range(n):
            for k, peer in enumerate(peers):
                copy(a, k, peer, me).wait_recv()
        for cp in sends:
            cp.wait_send()
        for cp in mine:
            cp.wait()

    return pl.pallas_call(
        body,
        name="exchange_grads",
        out_shape=[jax.ShapeDtypeStruct(s.shape, s.dtype) for s in blocked]
        + [jax.ShapeDtypeStruct((N_DEV,) + s.shape, s.dtype) for s in replicated],
        in_specs=[ANY] * n,
        out_specs=[ANY] * n,
        scratch_shapes=[pltpu.SemaphoreType.DMA((n, 7)), pltpu.SemaphoreType.DMA((n, 7)), pltpu.SemaphoreType.DMA((n,))],
    )(*arrays)


ADAMW_BLOCK_ELEMS = 256 * 1024


def _adamw(name, parts, w, m, v, own=None, me=None):
    r, c = w.shape
    tr = _tile(r, max(16, ADAMW_BLOCK_ELEMS // c), 16)

    def body(*refs):
        if own is None:
            p_ref, w_ref, m_ref, v_ref, g_ref, d_ref, nm_ref, nv_ref = refs
            terms = [p_ref[s].astype(F32) for s in range(N_DEV)]
        else:
            me_ref, p_ref, own_ref, w_ref, m_ref, v_ref, g_ref, d_ref, nm_ref, nv_ref = refs
            mine = own_ref[0].astype(F32)
            terms = [jnp.where(me_ref[0] == s, mine, p_ref[s].astype(F32)) for s in range(N_DEV)]
        g = terms[0]
        for s in range(1, N_DEV):
            g = g + terms[s]
        m_new = ADAM_B1 * m_ref[...] + (1.0 - ADAM_B1) * g
        v_new = ADAM_B2 * v_ref[...] + (1.0 - ADAM_B2) * (g * g)
        m_hat = m_new / (1.0 - ADAM_B1 ** ADAM_STEP)
        v_hat = v_new / (1.0 - ADAM_B2 ** ADAM_STEP)
        g_ref[...] = g
        d_ref[...] = -ADAM_LR * (m_hat / (jnp.sqrt(v_hat) + ADAM_EPS) + ADAM_WD * w_ref[...])
        nm_ref[...] = m_new
        nv_ref[...] = v_new

    if own is None:
        blk = pl.BlockSpec((tr, c), lambda i: (i, 0))
        return pl.pallas_call(
            body,
            name="adamw_" + name,
            out_shape=[jax.ShapeDtypeStruct((r, c), F32)] * 4,
            grid=(r // tr,),
            in_specs=[pl.BlockSpec((N_DEV, tr, c), lambda i: (0, i, 0)), blk, blk, blk],
            out_specs=[blk] * 4,
            compiler_params=_params(("parallel",)),
        )(parts, w, m, v)
    blk = pl.BlockSpec((tr, c), lambda i, me_ref: (i, 0))
    own_at = (lambda i, me_ref: (me_ref[0], i, 0)) if own.shape[0] == N_DEV else (lambda i, me_ref: (0, i, 0))
    return pl.pallas_call(
        body,
        name="adamw_" + name,
        out_shape=[jax.ShapeDtypeStruct((r, c), F32)] * 4,
        grid_spec=pltpu.PrefetchScalarGridSpec(
            num_scalar_prefetch=1,
            grid=(r // tr,),
            in_specs=[pl.BlockSpec((N_DEV, tr, c), lambda i, me_ref: (0, i, 0)), pl.BlockSpec((1, tr, c), own_at),
                      blk, blk, blk],
            out_specs=[blk] * 4),
        compiler_params=_params(("parallel",)),
    )(me, parts, own, w, m, v)


HBM_SPEC = pl.BlockSpec(memory_space=pltpu.HBM)
SEM_SPEC = pl.BlockSpec(memory_space=pltpu.SEMAPHORE)
SIDE_EFFECT = pltpu.SideEffectType.DATAFLOW_SIDE_EFFECTING
N_PEERS = N_DEV - 1


def _peers(x, y, c):
    return [(x, y, 1 - c), (1 - x, y, c), (x, 1 - y, c), (1 - x, 1 - y, c),
            (1 - x, y, 1 - c), (x, 1 - y, 1 - c), (1 - x, 1 - y, 1 - c)]


def _split_copy(srcs, lands, blocked, send_sems, recv_sems, a, k, frm, to):
    src = srcs[a].at[_slot(to)] if blocked[a] else srcs[a]
    return pltpu.make_async_remote_copy(
        src_ref=src, dst_ref=lands[a].at[_slot(frm)], send_sem=send_sems.at[a * N_PEERS + k],
        recv_sem=recv_sems.at[a * N_PEERS + k],
        device_id=to, device_id_type=MESH_ID)


def _exchange_start(name, srcs, lands, blocked, after=()):
    n = len(srcs)
    after = list(after)

    def body(*refs):
        src_refs, land_refs = refs[:n], refs[n:2 * n]
        send_sems, recv_sems = refs[2 * n + len(after)], refs[2 * n + len(after) + 1]
        token = refs[-1]
        x, y, c = lax.axis_index("x"), lax.axis_index("y"), lax.axis_index("c")
        for a in range(n):
            for k, peer in enumerate(_peers(x, y, c)):
                _split_copy(src_refs, land_refs, blocked, send_sems, recv_sems, a, k, (x, y, c), peer).start()
        token[...] = jnp.zeros_like(token)

    thru = [pltpu.HBM(s.shape, s.dtype) for s in list(srcs) + list(lands)]
    res = pl.pallas_call(
        body,
        name=name,
        out_shape=(pltpu.SemaphoreType.DMA((n * N_PEERS,)), pltpu.SemaphoreType.DMA((n * N_PEERS,)), *thru,
                   jax.ShapeDtypeStruct((8, LANE), F32)),
        in_specs=[HBM_SPEC] * (2 * n) + [pl.BlockSpec(memory_space=pl.ANY)] * len(after),
        out_specs=(SEM_SPEC, SEM_SPEC, *([HBM_SPEC] * (2 * n)), pl.BlockSpec(memory_space=pltpu.VMEM)),
        input_output_aliases={i: 2 + i for i in range(2 * n)},
        compiler_params=pltpu.CompilerParams(has_side_effects=SIDE_EFFECT),
    )(*[pltpu.with_memory_space_constraint(s, pltpu.HBM) for s in list(srcs) + list(lands)], *after)
    return res[0], res[1], res[2:2 + n], res[2 + n:2 + 2 * n], res[-1]


def _exchange_wait(name, send_sems, recv_sems, srcs, lands, blocked, after):
    n, n_after = len(srcs), len(after)

    def body(*refs):
        src_refs, land_refs = refs[:n], refs[n:2 * n]
        send, recv = refs[2 * n], refs[2 * n + 1]
        x, y, c = lax.axis_index("x"), lax.axis_index("y"), lax.axis_index("c")
        for a in range(n):
            for k, peer in enumerate(_peers(x, y, c)):
                _split_copy(src_refs, land_refs, blocked, send, recv, a, k, (x, y, c), peer).wait_send()
                _split_copy(src_refs, land_refs, blocked, send, recv, a, k, peer, (x, y, c)).wait_recv()

    res = pl.pallas_call(
        body,
        name=name,
        out_shape=tuple(pltpu.HBM(s.shape, s.dtype) for s in list(srcs) + list(lands)),
        in_specs=[HBM_SPEC] * (2 * n) + [SEM_SPEC, SEM_SPEC] + [pl.BlockSpec(memory_space=pl.ANY)] * n_after,
        out_specs=tuple([HBM_SPEC] * (2 * n)),
        input_output_aliases={i: i for i in range(2 * n)},
        compiler_params=pltpu.CompilerParams(has_side_effects=SIDE_EFFECT),
    )(*srcs, *lands, send_sems, recv_sems, *after)
    return res[:n], res[n:]


class _LazyWeights:
    def __init__(self):
        self.ready, self.groups, self.hints = {}, {}, {}

    def add_group(self, wait_name, names, send, recv, srcs, lands):
        for n in names:
            self.groups[n] = (wait_name, names, send, recv, srcs, lands)

    def hint(self, name, after):
        self.hints[self.groups[name][0]] = after

    def __getitem__(self, name):
        if name not in self.ready:
            wait_name, names, send, recv, srcs, lands = self.groups[name]
            after = [self.hints[wait_name]] if wait_name in self.hints else []
            _, whole = _exchange_wait(wait_name, send, recv, srcs, lands, [False] * len(names), after)
            for n, stacked in zip(names, whole):
                self.ready[n] = _to_kernel_layout({n: _unshard(n, stacked)})[n]
        return self.ready[name]


def kernel(x, positions, meta_tokens, w_in, w_q_up, w_kv_up, w_branch_mla, w_branch_hgrn, w_out, w_ffn_in, w_ffn_out, conv_w, conv_b, g_mix_norm, g_q_norm, g_kv_norm, g_hgrn_norm, g_ffn_norm, g_final_norm, lb_raw, loss_target, m_meta_tokens, m_w_in, m_w_q_up, m_w_kv_up, m_w_branch_mla, m_w_branch_hgrn, m_w_out, m_w_ffn_in, m_w_ffn_out, m_conv_w, m_conv_b, m_g_mix_norm, m_g_q_norm, m_g_kv_norm, m_g_hgrn_norm, m_g_ffn_norm, m_g_final_norm, m_lb_raw, v_meta_tokens, v_w_in, v_w_q_up, v_w_kv_up, v_w_branch_mla, v_w_branch_hgrn, v_w_out, v_w_ffn_in, v_w_ffn_out, v_conv_w, v_conv_b, v_g_mix_norm, v_g_q_norm, v_g_kv_norm, v_g_hgrn_norm, v_g_ffn_norm, v_g_final_norm, v_lb_raw):
    local = dict(zip(
        ("meta_tokens", "w_in", "w_q_up", "w_kv_up", "w_branch_mla", "w_branch_hgrn", "w_out", "w_ffn_in", "w_ffn_out",
         "conv_w", "conv_b", "g_mix_norm", "g_q_norm", "g_kv_norm", "g_hgrn_norm", "g_ffn_norm", "g_final_norm", "lb_raw"),
        (meta_tokens, w_in, w_q_up, w_kv_up, w_branch_mla, w_branch_hgrn, w_out, w_ffn_in, w_ffn_out,
         conv_w, conv_b, g_mix_norm, g_q_norm, g_kv_norm, g_hgrn_norm, g_ffn_norm, g_final_norm, lb_raw)))
    mom_m = dict(zip(local, (m_meta_tokens, m_w_in, m_w_q_up, m_w_kv_up, m_w_branch_mla, m_w_branch_hgrn, m_w_out, m_w_ffn_in,
                             m_w_ffn_out, m_conv_w, m_conv_b, m_g_mix_norm, m_g_q_norm, m_g_kv_norm, m_g_hgrn_norm,
                             m_g_ffn_norm, m_g_final_norm, m_lb_raw)))
    mom_v = dict(zip(local, (v_meta_tokens, v_w_in, v_w_q_up, v_w_kv_up, v_w_branch_mla, v_w_branch_hgrn, v_w_out, v_w_ffn_in,
                             v_w_ffn_out, v_conv_w, v_conv_b, v_g_mix_norm, v_g_q_norm, v_g_kv_norm, v_g_hgrn_norm,
                             v_g_ffn_norm, v_g_final_norm, v_lb_raw)))
    sharded = BIG + ("conv_w", "meta_tokens")

    def shard2d(name, arr):
        return arr.reshape(arr.shape[-2:]) if name != "meta_tokens" else arr

    def as2d(name, arr):
        return arr.reshape(1, -1) if arr.ndim == 1 else shard2d(name, arr)

    me = 4 * lax.axis_index("x") + 2 * lax.axis_index("y") + lax.axis_index("c")

    def landing(own):
        zone = lax.empty((N_DEV,) + own.shape[1:], own.dtype)
        return lax.dynamic_update_slice_in_dim(zone, own, me, 0)

    shards = {n: shard2d(n, local[n]).astype(BF16) for n in BIG}
    shards.update({n: shard2d(n, local[n]) for n in ("conv_w", "meta_tokens")})
    full = _LazyWeights()
    first = ("w_in", "meta_tokens")
    gathered = _all_gather([shards[n] for n in first])
    for n, g in zip(first, gathered):
        full.ready[n] = _to_kernel_layout({n: _unshard(n, g)})[n]
    later = (("w_q_up", "w_kv_up"), ("w_branch_mla", "w_branch_hgrn", "w_out", "w_ffn_in", "w_ffn_out", "conv_w"))
    for k, names in enumerate(later):
        srcs = [shards[n] for n in names]
        send, recv, srcs_thru, lands_thru, _ = _exchange_start(
            f"gather_start_{k}", srcs, [landing(s[None]) for s in srcs], [False] * len(names), after=[gathered[0]])
        full.add_group(f"gather_wait_{k}", names, send, recv, srcs_thru, lands_thru)
    small = {n: local[n] for n in SMALL}

    started = []

    def sources(group):
        group = _from_kernel_layout(group)
        names = list(group)
        blocked = [n in sharded for n in names]
        srcs = [_reshard(n, group[n]) if b else as2d(n, group[n]) for n, b in zip(names, blocked)]
        return names, blocked, srcs

    def emit(group):
        names, blocked, srcs = sources(group)
        lands = [lax.empty((N_DEV,) + (s.shape[1:] if b else s.shape), s.dtype) for s, b in zip(srcs, blocked)]
        k = len(started)
        send, recv, srcs_thru, lands_thru, token = _exchange_start(f"exchange_start_{k}", srcs, lands, blocked)
        started.append((names, blocked, send, recv, srcs_thru, lands_thru))
        return token

    loss, grad_x, last = _local_step(x[0], positions[0], loss_target[0], full, small, emit)

    out = {}

    me_arr = me.astype(jnp.int32).reshape(1)

    def update(names, parts, owns=None):
        for k, (n, part) in enumerate(zip(names, parts)):
            own = None if owns is None else (owns[k] if owns[k].ndim == 3 else owns[k][None])
            res = _adamw(n, part, as2d(n, local[n]), as2d(n, mom_m[n]), as2d(n, mom_v[n]), own,
                         None if owns is None else me_arr)
            out[n] = [r.reshape(local[n].shape) for r in res]

    after = [last["g_mix_norm"]]
    for k, (names, blocked, send, recv, srcs_thru, lands_thru) in enumerate(started):
        srcs_done, parts = _exchange_wait(f"exchange_wait_{k}", send, recv, srcs_thru, lands_thru, blocked, after)
        update(names, parts, srcs_done)
        after = [out[names[0]][0]]
    names, blocked, srcs = sources(last)
    in_blocks = [(n, s) for n, s, b in zip(names, srcs, blocked) if b]
    whole = [(n, s) for n, s, b in zip(names, srcs, blocked) if not b]
    update([n for n, _ in in_blocks + whole], _exchange([s for _, s in in_blocks], [s for _, s in whole]))

    loss = lax.psum(loss, ("x", "y", "c"))
    order = tuple(local)
    return (loss, grad_x[None], *[out[n][0] for n in order], *[out[n][1] for n in order],
            *[out[n][2] for n in order], *[out[n][3] for n in order])
```

```python
import functools

import jax
import jax.numpy as jnp
import numpy as np
from jax import lax
from jax.experimental import pallas as pl
from jax.experimental.pallas import tpu as pltpu

F32 = jnp.float32
BF16 = jnp.bfloat16

D_MODEL = 2048
N_META = 16
BLOCK = 128
PAD_LEN = BLOCK - N_META
HEADS = 16
Q_LORA = 1536
KV_LORA = 512
ROPE = 64
NOPE = 128
VDIM = 128
D_FF = 5632
NORM_EPS = 1e-6
ROPE_THETA = 10000.0
ATTN_SCALE = (NOPE + ROPE) ** -0.5
ADAM_LR = 0.001
ADAM_B1 = 0.9
ADAM_B2 = 0.999
ADAM_EPS = 1e-08
ADAM_WD = 0.01
ADAM_STEP = 10
N_DEV = 8

LANE = 128
SEG_Q_LAT = 0
SEG_KV_LAT = Q_LORA
SEG_HQ = 2048
SEG_HF = SEG_HQ + D_MODEL
SEG_HI = SEG_HF + D_MODEL
SEG_HG = SEG_HI + D_MODEL
SEG_GA = SEG_HG + D_MODEL
SEG_GB = SEG_GA + D_MODEL
SEG_KR = SEG_GB + D_MODEL
KR_W = 256
PROJ_W = SEG_KR + KR_W
QHEAD_W = 256

V7X_VMEM_BYTES = 64 * 1024 * 1024
VMEM_LIMIT = V7X_VMEM_BYTES * 7 // 8
NEG_BIG = -1e30
SUB = 8


def _tile(n, target, mult):
    best = None
    for t in range(mult, min(n, target) + 1, mult):
        if n % t == 0:
            best = t
    return n if best is None else best


def _params(sem):
    return pltpu.CompilerParams(dimension_semantics=sem, vmem_limit_bytes=VMEM_LIMIT)


def _sigmoid(x):
    return 0.5 * jnp.tanh(0.5 * x) + 0.5


MATMUL_WINDOW_BYTES = 40 * 1024 * 1024
_DIMS = {"nn":(((1,), (0,)), ((), ())), "nt": (((1,), (1,)), ((), ())), "tn": (((0,), (0,)), ((), ()))}


def _matmul(a, b, mode, out_dtype, name, after=None, col_blocks=1):
    if mode == "nn":
        (m, k), (_, n) = a.shape, b.shape
    elif mode == "nt":
        (m, k), (n, _) = a.shape, b.shape
    else:
        (k, m), (_, n) = a.shape, b.shape
    tm = _tile(m, 1040, 8) if mode != "tn" else _tile(m, 1024, LANE)
    tn = _tile(n, 1024, LANE) if col_blocks == 1 else n // col_blocks
    if mode == "nn":
        tk = _tile(k, 2816, LANE)
    elif mode == "nt":
        tk = _tile(k, 2816, LANE)
        if 3 * tk < k:
            half = _tile(m, 520, 8)
            out_bytes = 2 * half * tn * jnp.dtype(out_dtype).itemsize + 4 * half * tn
            longer = _tile(k, max(LANE, (MATMUL_WINDOW_BYTES - out_bytes) // (4 * (half + tn))), LANE)
            if 3 * longer >= k:
                tm, tk = half, longer
    else:
        for tm in (_tile(m, 1024, LANE), _tile(m, 512, LANE)):
            out_bytes = 2 * tm * tn * jnp.dtype(out_dtype).itemsize + 4 * tm * tn
            tk = _tile(k, max(8, (MATMUL_WINDOW_BYTES - out_bytes) // (4 * (tm + tn))), 8)
            if 2 * tk >= k:
                break
    nk = k // tk
    if mode == "nn":
        a_spec = pl.BlockSpec((tm, tk), lambda i, j, kk: (i, kk))
        b_spec = pl.BlockSpec((tk, tn), lambda i, j, kk: (kk, j))
    elif mode == "nt":
        a_spec = pl.BlockSpec((tm, tk), lambda i, j, kk: (i, kk))
        b_spec = pl.BlockSpec((tn, tk), lambda i, j, kk: (j, kk))
    else:
        a_spec = pl.BlockSpec((tk, tm), lambda i, j, kk: (kk, i))
        b_spec = pl.BlockSpec((tk, tn), lambda i, j, kk: (kk, j))
    dims = _DIMS[mode]

    n_after = 0 if after is None else 1

    def body(a_ref, b_ref, *rest):
        o_ref, acc = rest[n_after], rest[n_after + 1:]
        part = lax.dot_general(a_ref[...], b_ref[...], dims, preferred_element_type=F32)
        if nk == 1:
            o_ref[...] = part.astype(o_ref.dtype)
            return
        acc_ref, kk = acc[0], pl.program_id(2)

        @pl.when(kk == 0)
        def _():
            acc_ref[...] = part

        @pl.when((kk > 0) & (kk < nk - 1))
        def _():
            acc_ref[...] += part

        @pl.when(kk == nk - 1)
        def _():
            o_ref[...] = (acc_ref[...] + part).astype(o_ref.dtype)

    if col_blocks == 1:
        out_shape = jax.ShapeDtypeStruct((m, n), out_dtype)
        out_spec = pl.BlockSpec((tm, tn), lambda i, j, kk: (i, j))
    else:
        out_shape = jax.ShapeDtypeStruct((col_blocks, m, tn), out_dtype)
        out_spec = pl.BlockSpec((None, tm, tn), lambda i, j, kk: (j, i, 0))
    return pl.pallas_call(
        body,
        name=name,
        out_shape=out_shape,
        grid=(m // tm, n // tn, nk),
        in_specs=[a_spec, b_spec] + [pl.BlockSpec(memory_space=pl.ANY)] * n_after,
        out_specs=out_spec,
        scratch_shapes=[pltpu.VMEM((tm, tn), F32)] if nk > 1 else [],
        compiler_params=_params(("parallel", "parallel", "arbitrary")),
    )(a, b, *([after] * n_after))


ROW_WINDOW_BYTES = 12 * 1024 * 1024


def _rowwise(name, fn, ins, outs, rows, tm, after=None):
    per_row = sum(s[2] * s[1].dtype.itemsize for s in ins if s[0] == "row")
    per_row += sum(s[1] * jnp.dtype(s[2]).itemsize for s in outs if s[0] in ("row", "tail"))
    if per_row:
        tm = _tile(rows, min(tm, max(8, ROW_WINDOW_BYTES // (2 * per_row))), 8)
    n_in = len(ins)
    in_specs, args = [], []
    for spec in ins:
        if spec[0] == "row":
            _, arr, w, cb = spec
            in_specs.append(pl.BlockSpec((tm, w), functools.partial(lambda i, cb: (i, cb), cb=cb)))
        else:
            arr = spec[1]
            in_specs.append(pl.BlockSpec(arr.shape, lambda i: (0, 0)))
        args.append(arr)
    out_shape, out_specs = [], []
    for spec in outs:
        if spec[0] == "row":
            out_shape.append(jax.ShapeDtypeStruct((rows, spec[1]), spec[2]))
            out_specs.append(pl.BlockSpec((tm, spec[1]), lambda i: (i, 0)))
        elif spec[0] == "tail":
            out_shape.append(jax.ShapeDtypeStruct((rows - tm, spec[1]), spec[2]))
            out_specs.append(pl.BlockSpec((tm, spec[1]), lambda i: (jnp.maximum(i - 1, 0), 0)))
        elif spec[0] == "head":
            out_shape.append(jax.ShapeDtypeStruct((tm, spec[1]), spec[2]))
            out_specs.append(pl.BlockSpec((tm, spec[1]), lambda i: (0, 0)))
        else:
            out_shape.append(jax.ShapeDtypeStruct(spec[1], F32))
            out_specs.append(pl.BlockSpec(spec[1], lambda i: (0, 0)))
    has_acc = any(s[0] != "row" for s in outs)
    n_after = 0 if after is None else 1
    in_specs += [pl.BlockSpec(memory_space=pl.ANY)] * n_after
    args += [after] * n_after

    def body(*refs):
        i = pl.program_id(0)
        res = fn(i, tm, *[r[...] for r in refs[:n_in]])
        for spec, ref, val in zip(outs, refs[n_in + n_after:], res):
            if spec[0] in ("row", "tail"):
                ref[...] = val.astype(ref.dtype)
            elif spec[0] == "head":
                @pl.when(i == 0)
                def _(ref=ref, val=val):
                    ref[...] = val.astype(ref.dtype)
            else:
                @pl.when(i == 0)
                def _(ref=ref, val=val):
                    ref[...] = val

                @pl.when(i > 0)
                def _(ref=ref, val=val):
                    ref[...] += val

    return pl.pallas_call(
        body,
        name=name,
        out_shape=out_shape,
        grid=(rows // tm,),
        in_specs=in_specs,
        out_specs=out_specs,
        compiler_params=_params(("arbitrary" if has_acc else "parallel",)),
    )(*args)


def _rms_fwd(x, g):
    r = lax.rsqrt(jnp.mean(x * x, axis=-1, keepdims=True) + NORM_EPS)
    return x * r * g


def _rms_bwd(x, g, dy):
    r = lax.rsqrt(jnp.mean(x * x, axis=-1, keepdims=True) + NORM_EPS)
    xhat = x * r
    dxhat = dy * g
    dx = r * (dxhat - xhat * jnp.mean(dxhat * xhat, axis=-1, keepdims=True))
    return dx, jnp.sum(dy * xhat, axis=0, keepdims=True)


def _silu(x):
    return x * _sigmoid(x)


def _dsilu(x):
    s = _sigmoid(x)
    return s * (1.0 + x * (1.0 - s))


def _silu_both(x):
    s = _sigmoid(x)
    return x * s, s * (1.0 + x * (1.0 - s))


def _rot_src(x):
    lane = lax.broadcasted_iota(jnp.int32, x.shape, 1)
    return jnp.where(lane < ROPE // 2, pltpu.roll(x, LANE - ROPE // 2, 1), pltpu.roll(x, ROPE // 2, 1))


def _rope_fwd_call(q_raw, kv, proj, cos_t, sin_t, rows, tm):
    def fn(i, tm_, q, kvv, kr, c, s):
        kr_rot = kr[:, :LANE]
        kr_rot = kr_rot * c + _rot_src(kr_rot) * s
        qs, ks, vs = [], [], []
        for h in range(HEADS):
            qn = q[:, h * QHEAD_W:h * QHEAD_W + NOPE]
            qr = q[:, h * QHEAD_W + NOPE:(h + 1) * QHEAD_W]
            qs += [qn * SCORE_TO_LOG2, (qr * c + _rot_src(qr) * s) * SCORE_TO_LOG2]
            ks += [kvv[:, h * 2 * NOPE:h * 2 * NOPE + NOPE], kr_rot]
            vs += [kvv[:, h * 2 * NOPE + NOPE:(h + 1) * 2 * NOPE]]
        return jnp.concatenate(qs, axis=1), jnp.concatenate(ks, axis=1), jnp.concatenate(vs, axis=1)

    return _rowwise(
        "rope_fwd", fn,
        [("row", q_raw, HEADS * QHEAD_W, 0), ("row", kv, HEADS * 2 * NOPE, 0), ("row", proj, KR_W, SEG_KR // KR_W),
         ("row", cos_t, LANE, 0), ("row", sin_t, LANE, 0)],
        [("row", HEADS * QHEAD_W, BF16), ("row", HEADS * QHEAD_W, BF16), ("row", HEADS * VDIM, BF16)],
        rows, tm)


def _rope_bwd_call(dq_att, dk_att, dv, cos_t, sin_t, rows, tm):
    def fn(i, tm_, dq, dk, dvv, c, s):
        qs, kvs = [], []
        dkr = jnp.zeros((dq.shape[0], LANE), F32)
        for h in range(HEADS):
            dqr = dq[:, h * QHEAD_W + NOPE:(h + 1) * QHEAD_W] * ATTN_SCALE
            qs += [dq[:, h * QHEAD_W:h * QHEAD_W + NOPE] * ATTN_SCALE, dqr * c - _rot_src(dqr) * s]
            kvs += [dk[:, h * QHEAD_W:h * QHEAD_W + NOPE], dvv[:, h * VDIM:(h + 1) * VDIM]]
            dkr = dkr + dk[:, h * QHEAD_W + NOPE:(h + 1) * QHEAD_W]
        dkr = dkr * c - _rot_src(dkr) * s
        return (jnp.concatenate(qs, axis=1), jnp.concatenate(kvs, axis=1),
                jnp.concatenate([dkr, jnp.zeros_like(dkr)], axis=1))

    return _rowwise(
        "rope_bwd", fn,
        [("row", dq_att, HEADS * QHEAD_W, 0), ("row", dk_att, HEADS * QHEAD_W, 0), ("row", dv, HEADS * VDIM, 0),
         ("row", cos_t, LANE, 0), ("row", sin_t, LANE, 0)],
        [("row", HEADS * QHEAD_W, BF16), ("row", HEADS * 2 * NOPE, BF16), ("row", KR_W, BF16)],
        rows, tm)


def _attn_mask(q_blk, k_blk, t, keys_on_rows=False):
    qa, ka = (1, 0) if keys_on_rows else (0, 1)
    qs = q_blk * t + lax.broadcasted_iota(jnp.int32, (t, t), qa)
    ks = k_blk * t + lax.broadcasted_iota(jnp.int32, (t, t), ka)
    return (ks <= qs) & ((ks >= PAD_LEN) | (ks == qs))


_NT = _DIMS["nt"]
_TN = _DIMS["tn"]
LOG2E = 1.4426950408889634
SCORE_TO_LOG2 = ATTN_SCALE * LOG2E


def _causal_pairs(nb, by_key):
    if by_key:
        pairs = [(qi, kj) for kj in range(nb) for qi in range(kj, nb)]
    else:
        pairs = [(qi, kj) for qi in range(nb) for kj in range(qi + 1)]
    return (jnp.asarray(np.array([p[0] for p in pairs], np.int32)), jnp.asarray(np.array([p[1] for p in pairs], np.int32)))


def _two_parts(t, long_first=True):
    cut = (t // LANE + (1 if long_first else 0)) // 2 * LANE
    return ((0, cut), (cut, t)) if cut < t else ((0, t),)


def _attn_fwd(q_att, k_att, v, rows):
    t = _tile(rows, 640, LANE)
    nb = rows // t

    def body(qt_ref, kt_ref, q_ref, k_ref, v_ref, o32_ref, obf_ref, lse_ref, m_sc, l_sc, acc_sc):
        qi, kj = qt_ref[pl.program_id(1)], kt_ref[pl.program_id(1)]

        @pl.when(kj == 0)
        def _():
            m_sc[...] = jnp.full_like(m_sc, NEG_BIG)
            l_sc[...] = jnp.zeros_like(l_sc)
            acc_sc[...] = jnp.zeros_like(acc_sc)

        def step(masked):
            q = q_ref[...]
            parts = _two_parts(t)
            scores =[lax.dot_general(q, k_ref[lo:hi, :], _NT, preferred_element_type=F32) for lo, hi in parts]
            m, l, acc = m_sc[...], l_sc[...], acc_sc[...]
            for (lo, hi), s in zip(parts, scores):
                if masked:
                    qs = qi * t + lax.broadcasted_iota(jnp.int32, (t, hi - lo), 0)
                    ks = kj * t + lo + lax.broadcasted_iota(jnp.int32, (t, hi - lo), 1)
                    s = jnp.where((ks <= qs) & ((ks >= PAD_LEN) | (ks == qs)), s, NEG_BIG)
                m_new = jnp.maximum(m, jnp.max(s, axis=1, keepdims=True))
                alpha = jnp.exp2(m - m_new)
                p = jnp.exp2(s - jnp.tile(m_new, (1, (hi - lo) // LANE)))
                l = alpha * l + jnp.sum(p, axis=1, keepdims=True)
                acc = alpha * acc + jnp.dot(p.astype(BF16), v_ref[lo:hi, :], preferred_element_type=F32)
                m = m_new
            m_sc[...], l_sc[...], acc_sc[...] = m, l, acc

        pl.when((kj == qi) | (kj == 0))(functools.partial(step, True))
        pl.when((kj < qi) & (kj > 0))(functools.partial(step, False))

        @pl.when(kj == qi)
        def _():
            o = acc_sc[...] / l_sc[...]
            o32_ref[...] = o
            obf_ref[...] = o.astype(BF16)
            lse_ref[0] = jnp.max((m_sc[...] + jnp.log2(l_sc[...])).T, axis=0, keepdims=True)

    qt, kt = _causal_pairs(nb, by_key=False)
    qmap = lambda h, p, qt_ref, kt_ref: (qt_ref[p], h)
    kmap = lambda h, p, qt_ref, kt_ref: (kt_ref[p], h)
    return pl.pallas_call(
        body,
        name="attn_fwd",
        out_shape=[jax.ShapeDtypeStruct((rows, HEADS * VDIM), F32), jax.ShapeDtypeStruct((rows, HEADS * VDIM), BF16),
                   jax.ShapeDtypeStruct((HEADS, 1, rows), F32)],
        grid_spec=pltpu.PrefetchScalarGridSpec(
            num_scalar_prefetch=2,
            grid=(HEADS, len(qt)),
            in_specs=[pl.BlockSpec((t, QHEAD_W), qmap), pl.BlockSpec((t, QHEAD_W), kmap), pl.BlockSpec((t, VDIM), kmap)],
            out_specs=[pl.BlockSpec((t, VDIM), qmap), pl.BlockSpec((t, VDIM), qmap),
                       pl.BlockSpec((1, 1, t), lambda h, p, qt_ref, kt_ref: (h, 0, qt_ref[p]))],
            scratch_shapes=[pltpu.VMEM((t, LANE), F32), pltpu.VMEM((t, LANE), F32), pltpu.VMEM((t, VDIM), F32)]),
        compiler_params=_params(("parallel", "arbitrary")),
    )(qt, kt, q_att, k_att, v)


def _attn_delta(do, o32, rows, tm):
    def fn(i, tm_, dov, ov):
        prod = dov.astype(F32) * ov
        head_of = lax.broadcasted_iota(jnp.int32, (HEADS * VDIM, LANE), 0) // VDIM
        pick = jnp.where(head_of == lax.broadcasted_iota(jnp.int32, (HEADS * VDIM, LANE), 1), 1.0, 0.0).astype(F32)
        return (jnp.dot(prod, pick, precision=lax.Precision.HIGHEST, preferred_element_type=F32),)

    (delta,) = _rowwise("attn_delta", fn, [("row", do, HEADS * VDIM, 0), ("row", o32, HEADS * VDIM, 0)],
                        [("row", LANE, F32)], rows, tm)
    return delta


def _attn_bwd(q_att, k_att, v, do, lse_row, delta_row, rows):
    t = _tile(rows, 640, LANE)
    nb = rows // t

    def body(qt_ref, kt_ref, q_ref, k_ref, v_ref, do_ref, lse_ref, delta_ref, dq_ref, dk_ref, dv_ref, dk_sc, dv_sc):
        qi, kj = qt_ref[pl.program_id(1)], kt_ref[pl.program_id(1)]

        @pl.when(pl.program_id(1) == 0)
        def _():
            dq_ref[...] = jnp.zeros_like(dq_ref)

        @pl.when(qi == kj)
        def _():
            dk_sc[...] = jnp.zeros_like(dk_sc)
            dv_sc[...] = jnp.zeros_like(dv_sc)

        def step(masked):
            k, vv = k_ref[...], v_ref[...]
            parts = _two_parts(t, long_first=False)
            st_all = [lax.dot_general(k, q_ref[lo:hi, :], _NT, preferred_element_type=F32) for lo, hi in parts]
            dpt_all = [lax.dot_general(vv, do_ref[lo:hi, :], _NT, preferred_element_type=F32) for lo, hi in parts]
            dk, dv = dk_sc[...], dv_sc[...]
            for (lo, hi), st, dpt in zip(parts, st_all, dpt_all):
                pt = jnp.exp2(st - lse_ref[0, :, lo:hi])
                if masked:
                    ks = kj * t + lax.broadcasted_iota(jnp.int32, (t, hi - lo), 0)
                    qs = qi * t + lo + lax.broadcasted_iota(jnp.int32, (t, hi - lo), 1)
                    pt = jnp.where((ks <= qs) & ((ks >= PAD_LEN) | (ks == qs)), pt, 0.0)
                dv = dv + jnp.dot(pt.astype(BF16), do_ref[lo:hi, :], preferred_element_type=F32)
                dst = (pt * (dpt - delta_ref[0, :, lo:hi])).astype(BF16)
                dk = dk + jnp.dot(dst, q_ref[lo:hi, :], preferred_element_type=F32)
                q_rows = pl.ds(pl.multiple_of(qi * t + lo, LANE), hi - lo)
                dq_ref[q_rows, :] += lax.dot_general(dst, k, _TN, preferred_element_type=F32)
            dk_sc[...], dv_sc[...] = dk, dv

        pl.when((qi == kj) | (kj == 0))(functools.partial(step, True))
        pl.when((qi > kj) & (kj > 0))(functools.partial(step, False))

        @pl.when(qi == nb - 1)
        def _():
            dk_ref[...] = dk_sc[...] * (1.0 / LOG2E)
            dv_ref[...] = dv_sc[...]

    qt, kt = _causal_pairs(nb, by_key=True)
    qmap = lambda h, p, qt_ref, kt_ref: (qt_ref[p], h)
    kmap = lambda h, p, qt_ref, kt_ref: (kt_ref[p], h)
    stat = pl.BlockSpec((1, 1, t), lambda h, p, qt_ref, kt_ref: (h, 0, qt_ref[p]))
    return pl.pallas_call(
        body,
        name="attn_bwd",
        out_shape=[jax.ShapeDtypeStruct((rows, HEADS * QHEAD_W), F32), jax.ShapeDtypeStruct((rows, HEADS * QHEAD_W), F32),
                   jax.ShapeDtypeStruct((rows, HEADS * VDIM), F32)],
        grid_spec=pltpu.PrefetchScalarGridSpec(
            num_scalar_prefetch=2,
            grid=(HEADS, len(qt)),
            in_specs=[pl.BlockSpec((t, QHEAD_W), qmap), pl.BlockSpec((t, QHEAD_W), kmap), pl.BlockSpec((t, VDIM), kmap),
                      pl.BlockSpec((t, VDIM), qmap), stat, stat],
            out_specs=[pl.BlockSpec((rows, QHEAD_W), lambda h, p, qt_ref, kt_ref: (0, h)),
                       pl.BlockSpec((t, QHEAD_W), kmap), pl.BlockSpec((t, VDIM), kmap)],
            scratch_shapes=[pltpu.VMEM((t, QHEAD_W), F32), pltpu.VMEM((t, VDIM), F32)]),
        compiler_params=_params(("parallel", "arbitrary")),
    )(qt, kt, q_att, k_att, v, do, lse_row, delta_row)


C = BLOCK


def _hgrn_prep(hq, hf, hi, lb, c):
    rows = c * C + lax.broadcasted_iota(jnp.int32, (C, C), 0)
    valid = rows >= PAD_LEN
    sg = _sigmoid(hf)
    f = lb + (1.0 - lb) * sg
    g = jnp.where(valid, jnp.log(f), 0.0)
    k = jnp.where(valid, 1.0 - f, 0.0)
    q = _silu(hq)
    r = lax.broadcasted_iota(jnp.int32, (C, C), 0)
    cc = lax.broadcasted_iota(jnp.int32, (C, C), 1)
    tri = jnp.where(cc <= r, 1.0, 0.0).astype(F32)
    b = jnp.dot(tri, g, precision=lax.Precision.HIGHEST, preferred_element_type=F32)
    return q, k, hi, b, f, sg, valid


def _last_row_as_col(b_t):
    lane = lax.broadcasted_iota(jnp.int32, b_t.shape, 1)
    return jnp.sum(jnp.where(lane == C - 1, b_t, 0.0), axis=1, keepdims=True)


def _k_scaled(k, b, bs):
    return (k * jnp.exp(jnp.minimum(bs - b, 0.0))).astype(BF16)


def _hgrn_fwd(proj, lb, rows):
    nc = rows // C

    def body(hq_ref, hf_ref, hi_ref, lb_ref, o_ref, a_ref, s_ref, s_sc, b_sc):
        c = pl.program_id(1)

        @pl.when(c == 0)
        def _():
            s_sc[...] = jnp.zeros_like(s_sc)

        q, k, v, b, _, _, _ = _hgrn_prep(hq_ref[...], hf_ref[...], hi_ref[...], lb_ref[...], c)
        b_sc[...] = b
        s0 = s_sc[...]
        s_ref[0, 0] = s0
        v_bf = v.astype(BF16)
        r16 = lax.broadcasted_iota(jnp.int32, (SUB, C), 0)
        c16 = lax.broadcasted_iota(jnp.int32, (SUB, C), 1)
        slabs = [jnp.zeros((SUB, C), F32)]
        for i in range(1, C // SUB):
            bs = b_sc[SUB * i - 1:SUB * i, :]
            qs = (q[SUB * i:SUB * (i + 1)] * jnp.exp(b[SUB * i:SUB * (i + 1)] - bs)).astype(BF16)
            a_i = lax.dot_general(qs, _k_scaled(k, b, bs), _NT, preferred_element_type=F32)
            slabs.append(jnp.where(c16 <= r16 + (SUB * i - SUB), a_i, 0.0))
        a_off = jnp.concatenate(slabs, axis=0)
        q_t, k_t, b_t = q.T, k.T, b.T
        sub = lax.broadcasted_iota(jnp.int32, (C, C), 0)
        lane = lax.broadcasted_iota(jnp.int32, (C, C), 1)
        lane1 = lax.broadcasted_iota(jnp.int32, (1, C), 1)
        at_band = jnp.zeros((C, C), F32)
        ahead = lane - sub
        for dl in range(SUB):
            k_s = pltpu.roll(k_t, dl, 1) if dl else k_t
            b_s = pltpu.roll(b_t, dl, 1) if dl else b_t
            e = jnp.exp(b_t - b_s)
            band = jnp.sum(q_t * k_s * e, axis=0, keepdims=True)
            band = jnp.where(lane1 >= dl, band, 0.0)
            at_band = at_band + jnp.where(ahead == dl, jnp.broadcast_to(band, (C, C)), 0.0)
        a = (a_off + at_band.T).astype(BF16)
        a_ref[0] = a
        qe = (q * jnp.exp(b)).astype(BF16)
        o_ref[...] = (jnp.dot(a, v_bf, preferred_element_type=F32)
                      + jnp.dot(qe, s0.astype(BF16), preferred_element_type=F32))
        b_last = b_sc[C - 1:C, :]
        kd = (k * jnp.exp(b_last - b)).astype(BF16)
        s_sc[...] = (jnp.exp(_last_row_as_col(b_t)) * s0
                     + lax.dot_general(kd, v_bf, _TN, preferred_element_type=F32))

    seg = lambda base: (lambda h, c: (c, base // C + h))
    return pl.pallas_call(
        body,
        name="hgrn_fwd",
        out_shape=[jax.ShapeDtypeStruct((rows, D_MODEL), F32), jax.ShapeDtypeStruct((HEADS, rows, C), BF16),
                   jax.ShapeDtypeStruct((HEADS, nc, C, C), F32)],
        grid=(HEADS, nc),
        in_specs=[pl.BlockSpec((C, C), seg(SEG_HQ)), pl.BlockSpec((C, C), seg(SEG_HF)), pl.BlockSpec((C, C), seg(SEG_HI)),
                  pl.BlockSpec((1, C), lambda h, c: (0, h))],
        out_specs=[pl.BlockSpec((C, C), lambda h, c: (c, h)), pl.BlockSpec((1, C, C), lambda h, c: (h, c, 0)),
                   pl.BlockSpec((1, 1, C, C), lambda h, c: (h, c, 0, 0))],
        scratch_shapes=[pltpu.VMEM((C, C), F32), pltpu.VMEM((C, C), F32)],
        compiler_params=_params(("parallel", "arbitrary")),
    )(proj, proj, proj, lb)


def _hgrn_bwd(proj, lb, a_mat, s_states, do_h, rows):
    nc = rows // C

    def body(hq_ref, hf_ref, hi_ref, lb_ref, a_ref, s_ref, do_ref, dhq_ref, dhf_ref, dhi_ref, dlb_ref, ds_sc, b_sc):
        step = pl.program_id(1)
        c = nc - 1 - step

        @pl.when(step == 0)
        def _():
            ds_sc[...] = jnp.zeros_like(ds_sc)
            dlb_ref[...] = jnp.zeros_like(dlb_ref)

        hq, hf = hq_ref[...], hf_ref[...]
        lb_row = lb_ref[...]
        q, k, v, b, f, sg, valid = _hgrn_prep(hq, hf, hi_ref[...], lb_row, c)
        b_sc[...] = b
        s0 = s_ref[0, 0]
        ds1 = ds_sc[...]
        s0_bf, ds1_bf = s0.astype(BF16), ds1.astype(BF16)
        do = do_ref[...]
        do_bf, v_bf = do.astype(BF16), v.astype(BF16)
        b_last = b_sc[C - 1:C, :]
        e_last = jnp.exp(b_last - b)
        eb = jnp.exp(b)
        sub = lax.broadcasted_iota(jnp.int32, (C, C), 0)
        lane = lax.broadcasted_iota(jnp.int32, (C, C), 1)
        r16 = lax.broadcasted_iota(jnp.int32, (SUB, C), 0)
        c16 = lax.broadcasted_iota(jnp.int32, (SUB, C), 1)

        dv = (lax.dot_general(a_ref[0], do_bf, _TN, preferred_element_type=F32)
              + jnp.dot((k * e_last).astype(BF16), ds1_bf, preferred_element_type=F32))
        da = jnp.where(lane <= sub, lax.dot_general(do_bf, v_bf, _NT, preferred_element_type=F32), 0.0)
        da_t = jnp.where(sub <= lane, lax.dot_general(v_bf, do_bf, _NT, preferred_element_type=F32), 0.0)

        dq_slabs = [jnp.zeros((SUB, C), F32)]
        for i in range(1, C // SUB):
            bs = b_sc[SUB * i - 1:SUB * i, :]
            da_i = jnp.where(c16 <= r16 + (SUB * i - SUB), da[SUB * i:SUB * (i + 1)], 0.0).astype(BF16)
            dq_slabs.append(jnp.exp(b[SUB * i:SUB * (i + 1)] - bs)
                            * jnp.dot(da_i, _k_scaled(k, b, bs), preferred_element_type=F32))
        dk_slabs = []
        for j in range(C // SUB - 1):
            be = b_sc[SUB * j + SUB - 1:SUB * (j + 1), :]
            qe_j = (q * jnp.exp(jnp.minimum(b - be, 0.0))).astype(BF16)
            da_j = jnp.where(c16 >= r16 + (SUB * j + SUB), da_t[SUB * j:SUB * (j + 1)], 0.0).astype(BF16)
            dk_slabs.append(jnp.exp(be - b[SUB * j:SUB * (j + 1)]) * jnp.dot(da_j, qe_j, preferred_element_type=F32))
        dk_slabs.append(jnp.zeros((SUB, C), F32))

        q_t, k_t, b_t = q.T, k.T, b.T
        lane1 = lax.broadcasted_iota(jnp.int32, (1, C), 1)
        dq_t = jnp.zeros((C, C), F32)
        dk_t = jnp.zeros((C, C), F32)
        ahead = lane - sub
        for dl in range(SUB):
            k_s = pltpu.roll(k_t, dl, 1) if dl else k_t
            b_s = pltpu.roll(b_t, dl, 1) if dl else b_t
            e = jnp.exp(jnp.minimum(b_t - b_s, 0.0))
            dband = jnp.sum(jnp.where(ahead == dl, da_t, 0.0), axis=0, keepdims=True)
            w = jnp.where(lane1 >= dl, dband, 0.0) * e
            dq_t = dq_t + w * k_s
            back = w * q_t
            dk_t = dk_t + (pltpu.roll(back, C - dl, 1) if dl else back)

        dq = eb * lax.dot_general(do_bf, s0_bf, _NT, preferred_element_type=F32) + jnp.concatenate(dq_slabs, axis=0) + dq_t.T
        dk_inter = e_last * lax.dot_general(v_bf, ds1_bf, _NT, preferred_element_type=F32)
        dk = dk_inter + jnp.concatenate(dk_slabs, axis=0) + dk_t.T

        extra = (jnp.exp(b_last) * jnp.sum((s0 * ds1).T, axis=0, keepdims=True)
                 + jnp.sum(k * dk_inter, axis=0, keepdims=True))
        db = q * dq - k * dk + jnp.where(sub == C - 1, jnp.broadcast_to(extra, (C, C)), 0.0)
        tri_t = jnp.where(lane >= sub, 1.0, 0.0).astype(F32)
        dg = jnp.dot(tri_t, db, precision=lax.Precision.HIGHEST, preferred_element_type=F32)
        ds_sc[...] = (jnp.exp(_last_row_as_col(b_t)) * ds1
                      + lax.dot_general((q * eb).astype(BF16), do_bf, _TN, preferred_element_type=F32))

        df = jnp.where(valid, dg / f - dk, 0.0)
        dhf_ref[...] = (df * (1.0 - lb_row) * sg * (1.0 - sg)).astype(BF16)
        dlb_ref[...] += jnp.sum(df * (1.0 - sg), axis=0, keepdims=True)
        dhq_ref[...] = (dq * _dsilu(hq)).astype(BF16)
        dhi_ref[...] = dv.astype(BF16)

    seg = lambda base: (lambda h, s: (nc - 1 - s, base // C + h))
    rmap = lambda h, s: (nc - 1 - s, h)
    return pl.pallas_call(
        body,
        name="hgrn_bwd",
        out_shape=[jax.ShapeDtypeStruct((rows, D_MODEL), BF16)] * 3 + [jax.ShapeDtypeStruct((1, D_MODEL), F32)],
        grid=(HEADS, nc),
        in_specs=[pl.BlockSpec((C, C), seg(SEG_HQ)), pl.BlockSpec((C, C), seg(SEG_HF)), pl.BlockSpec((C, C), seg(SEG_HI)),
                  pl.BlockSpec((1, C), lambda h, s: (0, h)),
                  pl.BlockSpec((1, C, C), lambda h, s: (h, nc - 1 - s, 0)),
                  pl.BlockSpec((1, 1, C, C), lambda h, s: (h, nc - 1 - s, 0, 0)),
                  pl.BlockSpec((C, C), rmap)],
        out_specs=[pl.BlockSpec((C, C), rmap)] * 3 + [pl.BlockSpec((1, C), lambda h, s: (0, h))],
        scratch_shapes=[pltpu.VMEM((C, C), F32), pltpu.VMEM((C, C), F32)],
        compiler_params=_params(("parallel", "arbitrary")),
    )(proj, proj, proj, lb, a_mat, s_states, do_h)


CONV_TC = 512
HALO = 16


def _halo_row(block, k):
    r = lax.broadcasted_iota(jnp.int32, block.shape, 0)
    return jnp.sum(jnp.where(r == k, block, 0.0), axis=0, keepdims=True)


def _conv_taps(i, tm, g_ref, pg_ref):
    shape = g_ref.shape
    r = lax.broadcasted_iota(jnp.int32, shape, 0)
    g = jnp.where(i * tm + r >= PAD_LEN, g_ref[...].astype(F32), 0.0)
    prev = pg_ref[...].astype(F32)
    p1 = jnp.where(i * tm - 1 >= PAD_LEN, _halo_row(prev, HALO - 1), 0.0)
    p2 = jnp.where(i * tm - 2 >= PAD_LEN, _halo_row(prev, HALO - 2), 0.0)
    s1 = jnp.where(r == 0, p1, pltpu.roll(g, 1, 0))
    s2 = jnp.where(r == 0, p2, jnp.where(r == 1, p1, pltpu.roll(g, 2, 0)))
    return g, s1, s2


def _conv_specs(tm, tc, ncb, order):
    gate = pl.BlockSpec((tm, tc), lambda *ids: order(ids))
    halo = pl.BlockSpec((HALO, tc), lambda *ids: (jnp.maximum(order(ids)[0] * (tm // HALO) - 1, 0), order(ids)[1]))
    up = pl.BlockSpec((tm, tc), lambda *ids: (order(ids)[0], ncb + order(ids)[1]))
    return gate, halo, up


def _conv_fwd(ffn, conv_w, conv_b, rows, tm):
    tc = CONV_TC
    ncb = D_FF // tc

    def body(g_ref, pg_ref, up_ref, cw_ref, cb_ref, act_ref):
        i = pl.program_id(0)
        g, s1, s2 = _conv_taps(i, tm, g_ref, pg_ref)
        conv = (cw_ref[0:1, :] * s2 + cw_ref[1:2, :] * s1 + cw_ref[2:3, :] * g) + cb_ref[...]
        act_ref[...] = (_silu(conv) * up_ref[...].astype(F32)).astype(BF16)

    gate, halo, up = _conv_specs(tm, tc, ncb, lambda ids: (ids[0], ids[1]))
    return pl.pallas_call(
        body,
        name="conv_fwd",
        out_shape=jax.ShapeDtypeStruct((rows, D_FF), BF16),
        grid=(rows // tm, ncb),
        in_specs=[gate, halo, up, pl.BlockSpec((3, tc), lambda i, j: (0, j)), pl.BlockSpec((1, tc), lambda i, j: (0, j))],
        out_specs=pl.BlockSpec((tm, tc), lambda i, j: (i, j)),
        compiler_params=_params(("parallel", "parallel")),
    )(ffn, ffn, ffn, conv_w, conv_b)


def _conv_bwd_a(ffn, dact, conv_w, conv_b, rows, tm):
    tc = CONV_TC
    ncb = D_FF // tc

    def body(g_ref, pg_ref, up_ref, da_ref, cw_ref, cb_ref, dc_ref, dffn_ref, w0_ref, w1_ref, w2_ref, db_ref):
        i = pl.program_id(1)
        g, s1, s2 = _conv_taps(i, tm, g_ref, pg_ref)
        conv = (cw_ref[0:1, :] * s2 + cw_ref[1:2, :] * s1 + cw_ref[2:3, :] * g) + cb_ref[...]
        da = da_ref[...].astype(F32)
        act, dact_dconv = _silu_both(conv)
        dffn_ref[...] = (da * act).astype(BF16)
        dc = da * up_ref[...].astype(F32) * dact_dconv
        dc_ref[...] = dc.astype(BF16)
        sums = [jnp.sum(dc * s2, axis=0, keepdims=True), jnp.sum(dc * s1, axis=0, keepdims=True),
                jnp.sum(dc * g, axis=0, keepdims=True), jnp.sum(dc, axis=0, keepdims=True)]
        for ref, val in zip((w0_ref, w1_ref, w2_ref, db_ref), sums):
            @pl.when(i == 0)
            def _(ref=ref, val=val):
                ref[...] = val

            @pl.when(i > 0)
            def _(ref=ref, val=val):
                ref[...] += val

    gate, halo, up = _conv_specs(tm, tc, ncb, lambda ids: (ids[1], ids[0]))
    col = pl.BlockSpec((1, tc), lambda j, i: (0, j))
    return pl.pallas_call(
        body,
        name="conv_bwd_a",
        out_shape=[jax.ShapeDtypeStruct((rows, D_FF), BF16), jax.ShapeDtypeStruct((rows, 2 * D_FF), BF16)]
        + [jax.ShapeDtypeStruct((1, D_FF), F32)] * 4,
        grid=(ncb, rows // tm),
        in_specs=[gate, halo, up, pl.BlockSpec((tm, tc), lambda j, i: (i, j)),
                  pl.BlockSpec((3, tc), lambda j, i: (0, j)), col],
        out_specs=[pl.BlockSpec((tm, tc), lambda j, i: (i, j)), pl.BlockSpec((tm, tc), lambda j, i: (i, ncb + j)),
                   col, col, col, col],
        compiler_params=_params(("parallel", "arbitrary")),
    )(ffn, ffn, ffn, dact, conv_w, conv_b)


def _conv_bwd_b(dconv, conv_w, dffn, rows, tm):
    tc = CONV_TC
    ncb = D_FF // tc
    nrb = rows // tm

    def body(dc_ref, nx_ref, cw_ref, dffn_in, out_ref):
        del dffn_in
        i = pl.program_id(0)
        dc = dc_ref[...].astype(F32)
        r = lax.broadcasted_iota(jnp.int32, dc.shape, 0)
        last = i == nrb - 1
        nxt = nx_ref[...].astype(F32)
        x1 = jnp.where(last, 0.0, _halo_row(nxt, 0))
        x2 = jnp.where(last, 0.0, _halo_row(nxt, 1))
        n1 = jnp.where(r == tm - 1, x1, pltpu.roll(dc, tm - 1, 0))
        n2 = jnp.where(r == tm - 1, x2, jnp.where(r == tm - 2, x1, pltpu.roll(dc, tm - 2, 0)))
        dg = cw_ref[2:3, :] * dc + cw_ref[1:2, :] * n1 + cw_ref[0:1, :] * n2
        out_ref[...] = jnp.where(i * tm + r >= PAD_LEN, dg, 0.0).astype(BF16)

    return pl.pallas_call(
        body,
        name="conv_bwd_b",
        out_shape=jax.ShapeDtypeStruct((rows, 2 * D_FF), BF16),
        grid=(nrb, ncb),
        in_specs=[pl.BlockSpec((tm, tc), lambda i, j: (i, j)),
                  pl.BlockSpec((HALO, tc), lambda i, j: (jnp.minimum((i + 1) * (tm // HALO), rows // HALO - 1), j)),
                  pl.BlockSpec((3, tc), lambda i, j: (0, j)),
                  pl.BlockSpec(memory_space=pl.ANY)],
        out_specs=pl.BlockSpec((tm, tc), lambda i, j: (i, j)),
        input_output_aliases={3: 0},
        compiler_params=_params(("parallel", "parallel")),
    )(dconv, dconv, conv_w, dffn)


def _final_call(h1, y, target, g_final, rows):
    tm = BLOCK

    def fn(i, tm_, h1v, yv, tgt, g):
        h2 = h1v + yv
        out = _rms_fwd(h2, g)
        err = jnp.where(i > 0, out - tgt, 0.0)
        loss = 0.5 * jnp.sum(jnp.mean(err * err, axis=-1, keepdims=True), axis=0, keepdims=True)
        dx, dg = _rms_bwd(h2, g, err * (1.0 / D_MODEL))
        return dx, dx, jnp.broadcast_to(loss, (1, LANE)), dg

    n_in = 4
    in_specs = [pl.BlockSpec((tm, D_MODEL), lambda i: (i, 0)), pl.BlockSpec((tm, D_MODEL), lambda i: (i, 0)),
                pl.BlockSpec((tm, D_MODEL), lambda i: (jnp.maximum(i - 1, 0), 0)),
                pl.BlockSpec((1, D_MODEL), lambda i: (0, 0))]

    def body(*refs):
        i = pl.program_id(0)
        dx, dx2, loss, dg = fn(i, tm, *[r[...] for r in refs[:n_in]])
        refs[4][...] = dx
        refs[5][...] = dx2.astype(BF16)
        for ref, val in ((refs[6], loss), (refs[7], dg)):
            @pl.when(i == 0)
            def _(ref=ref, val=val):
                ref[...] = val

            @pl.when(i > 0)
            def _(ref=ref, val=val):
                ref[...] += val

    return pl.pallas_call(
        body,
        name="final_loss",
        out_shape=[jax.ShapeDtypeStruct((rows, D_MODEL), F32), jax.ShapeDtypeStruct((rows, D_MODEL), BF16),
                   jax.ShapeDtypeStruct((1, LANE), F32), jax.ShapeDtypeStruct((1, D_MODEL), F32)],
        grid=(rows // tm,),
        in_specs=in_specs,
        out_specs=[pl.BlockSpec((tm, D_MODEL), lambda i: (i, 0)), pl.BlockSpec((tm, D_MODEL), lambda i: (i, 0)),
                   pl.BlockSpec((1, LANE), lambda i: (0, 0)), pl.BlockSpec((1, D_MODEL), lambda i: (0, 0))],
        compiler_params=_params(("arbitrary",)),
    )(h1, y, target, g_final)


def _heads_map(fn, *slabs):
    outs = [fn(*[s[:, h * LANE:(h + 1) * LANE] for s in slabs]) for h in range(HEADS)]
    if isinstance(outs[0], tuple):
        return tuple(jnp.concatenate([o[k] for o in outs], axis=1) for k in range(len(outs[0])))
    return jnp.concatenate(outs, axis=1)


def _local_step(x, positions, target, w, p, emit=None):
    kept = {}
    if emit is None:
        def emit(group):
            kept.update(group)
            return None
    s_len = x.shape[0]
    rows = s_len + BLOCK
    tm = _tile(rows, 640, 8)
    row = lambda arr, width, cb=0: ("row", arr, width, cb)

    h0 = jnp.concatenate([jnp.zeros((PAD_LEN, D_MODEL), F32), w["meta_tokens"], x], axis=0)
    pos = jnp.concatenate([jnp.zeros((PAD_LEN,), jnp.int32), jnp.arange(N_META, dtype=jnp.int32),
                           positions.astype(jnp.int32) + N_META])
    inv = 1.0 / (ROPE_THETA ** (jnp.arange(0, ROPE, 2, dtype=F32) / ROPE))
    ang = pos.astype(F32)[:, None] * inv
    zero = jnp.zeros((rows, LANE - ROPE), F32)
    cos_t = jnp.concatenate([jnp.cos(ang), jnp.cos(ang), zero], axis=1)
    sin_t = jnp.concatenate([-jnp.sin(ang), jnp.sin(ang), zero], axis=1)
    lb_r0, lb_r1 = p["lb_raw"][0:1], p["lb_raw"][1:2]

    def lb_fn(i, tm_, r0, r1):
        m = jnp.maximum(r0, r1)
        e0, e1 = jnp.exp(r0 - m), jnp.exp(r1 - m)
        return (e0 / (e0 + e1),)

    (lb,) = _rowwise("lb_fwd", lb_fn, [("bc", lb_r0), ("bc", lb_r1)], [("acc", (1, D_MODEL))], 1, 1)

    (u1,) = _rowwise("mix_norm", lambda i, t, h, g: (_rms_fwd(h, g),),
                     [row(h0, D_MODEL), ("bc", p["g_mix_norm"])], [("row", D_MODEL, BF16)], rows, tm)
    proj = _matmul(u1, w["w_in"], "nn", F32, "mm_proj")
    hint = getattr(w, "hint", lambda name, after: None)
    hint("w_q_up", proj)
    qn, kvn = _rowwise(
        "latent_norm", lambda i, t, ql, kl, gq, gk: (_rms_fwd(ql, gq), _rms_fwd(kl, gk)),
        [row(proj, Q_LORA, 0), row(proj, KV_LORA, SEG_KV_LAT // KV_LORA), ("bc", p["g_q_norm"]), ("bc", p["g_kv_norm"])],
        [("row", Q_LORA, BF16), ("row", KV_LORA, BF16)], rows, tm)
    q_raw = _matmul(qn, w["w_q_up"], "nn", F32, "mm_q_up")
    kv = _matmul(kvn, w["w_kv_up"], "nn", F32, "mm_kv_up")
    q_att, k_att, v_att = _rope_fwd_call(q_raw, kv, proj, cos_t, sin_t, rows, tm)
    o32, o_bf, lse = _attn_fwd(q_att, k_att, v_att, rows)
    hint("w_branch_mla", lse)
    o_h, a_mat, s_states = _hgrn_fwd(proj, lb, rows)

    def hgrn_post(i, t, oh, hg, g):
        return (_heads_map(lambda a, b: _rms_fwd(a, g) * _silu(b), oh, hg),)

    (o_hgrn,) = _rowwise("hgrn_post", hgrn_post,
                         [row(o_h, D_MODEL), row(proj, D_MODEL, SEG_HG // D_MODEL), ("bc", p["g_hgrn_norm"])],
                         [("row", D_MODEL, BF16)], rows, tm)
    br_a = _matmul(o_bf, w["w_branch_mla"], "nn", F32, "mm_branch_mla")
    br_b = _matmul(o_hgrn, w["w_branch_hgrn"], "nn", F32, "mm_branch_hgrn")
    (merged,) = _rowwise(
        "merge", lambda i, t, a, b, ga, gb: (_sigmoid(ga) * a + _sigmoid(gb) * b,),
        [row(br_a, D_MODEL), row(br_b, D_MODEL), row(proj, D_MODEL, SEG_GA // D_MODEL), row(proj, D_MODEL, SEG_GB // D_MODEL)],
        [("row", D_MODEL, BF16)], rows, tm)
    mix_out = _matmul(merged, w["w_out"], "nn", F32, "mm_out")

    def ffn_norm(i, t, h, mo, g):
        h1v = h + mo
        return h1v, _rms_fwd(h1v, g)

    h1, u2 = _rowwise("ffn_norm", ffn_norm, [row(h0, D_MODEL), row(mix_out, D_MODEL), ("bc", p["g_ffn_norm"])],
                      [("row", D_MODEL, F32), ("row", D_MODEL, BF16)], rows, tm)
    ffn = _matmul(u2, w["w_ffn_in"], "nn", BF16, "mm_ffn_in")
    act = _conv_fwd(ffn, w["conv_w"], p["conv_b"], rows, tm)
    y = _matmul(act, w["w_ffn_out"], "nn", F32, "mm_ffn_out")
    dh2, dh2_bf, loss_acc, dg_final = _final_call(h1, y, target, p["g_final_norm"].reshape(1, D_MODEL), rows)

    grads = {"g_final_norm": dg_final.reshape(D_MODEL)}
    dact = _matmul(dh2_bf, w["w_ffn_out"], "nt", BF16, "mm_d_act")
    grads["w_ffn_out"] = _matmul(act, dh2_bf, "tn", BF16, "mm_dw_ffn_out")
    dconv, dffn, dcw0, dcw1, dcw2, dcb = _conv_bwd_a(ffn, dact, w["conv_w"], p["conv_b"], rows, tm)
    dffn = _conv_bwd_b(dconv, w["conv_w"], dffn, rows, tm)
    grads["conv_w"] = jnp.concatenate([dcw0, dcw1, dcw2], axis=0)
    grads["conv_b"] = dcb
    du2 = _matmul(dffn, w["w_ffn_in"], "nt", F32, "mm_d_u2")
    grads["w_ffn_in"] = _matmul(u2, dffn, "tn", BF16, "mm_dw_ffn_in", col_blocks=N_DEV)

    def ffn_norm_bwd(i, t, h, du, dh, g):
        dx, dg = _rms_bwd(h, g, du)
        dh1v = dh + dx
        return dh1v, dh1v, dg

    dh1, dh1_bf, grads["g_ffn_norm"] = _rowwise(
        "ffn_norm_bwd", ffn_norm_bwd, [row(h1, D_MODEL), row(du2, D_MODEL), row(dh2, D_MODEL), ("bc", p["g_ffn_norm"])],
        [("row", D_MODEL, F32), ("row", D_MODEL, BF16), ("acc", (1, D_MODEL))], rows, tm)
    tok = emit({n: grads.pop(n) for n in ("w_ffn_out", "w_ffn_in", "conv_w", "conv_b", "g_final_norm", "g_ffn_norm")})
    dmerged = _matmul(dh1_bf, w["w_out"], "nt", F32, "mm_d_merged", after=tok)
    grads["w_out"] = _matmul(merged, dh1_bf, "tn", BF16, "mm_dw_out")

    def merge_bwd(i, t, dm, a, b, ga, gb):
        sa, sb = _sigmoid(ga), _sigmoid(gb)
        return dm * sa, dm * sb, dm * a * sa * (1.0 - sa), dm * b * sb * (1.0 - sb)

    da_bf, db_bf, dga, dgb = _rowwise(
        "merge_bwd", merge_bwd,
        [row(dmerged, D_MODEL), row(br_a, D_MODEL), row(br_b, D_MODEL),
         row(proj, D_MODEL, SEG_GA // D_MODEL), row(proj, D_MODEL, SEG_GB // D_MODEL)],
        [("row", D_MODEL, BF16)] * 4, rows, tm)
    do_mla = _matmul(da_bf, w["w_branch_mla"], "nt", BF16, "mm_d_o_mla")
    grads["w_branch_mla"] = _matmul(o_bf, da_bf, "tn", BF16, "mm_dw_branch_mla")
    do_hgrn = _matmul(db_bf, w["w_branch_hgrn"], "nt", F32, "mm_d_o_hgrn")
    grads["w_branch_hgrn"] = _matmul(o_hgrn, db_bf, "tn", BF16, "mm_dw_branch_hgrn")

    def hgrn_post_bwd(i, t, dy, oh, hg, g):
        def one(dyh, ohh, hgh):
            gate, dgate = _silu_both(hgh)
            dx, dg = _rms_bwd(ohh, g, dyh * gate)
            return dx, dyh * _rms_fwd(ohh, g) * dgate, dg

        dx, dhg, dg = _heads_map(one, dy, oh, hg)
        dg_sum = dg[:, 0:LANE]
        for h in range(1, HEADS):
            dg_sum = dg_sum + dg[:, h * LANE:(h + 1) * LANE]
        return dx, dhg, dg_sum

    tok = emit({n: grads.pop(n) for n in ("w_out", "w_branch_mla", "w_branch_hgrn")})
    do_h, dhg, grads["g_hgrn_norm"] = _rowwise(
        "hgrn_post_bwd", hgrn_post_bwd,
        [row(do_hgrn, D_MODEL), row(o_h, D_MODEL), row(proj, D_MODEL, SEG_HG // D_MODEL), ("bc", p["g_hgrn_norm"])],
        [("row", D_MODEL, F32), ("row", D_MODEL, BF16), ("acc", (1, LANE))], rows, tm, after=tok)
    dhq, dhf, dhi, dlb = _hgrn_bwd(proj, lb, a_mat, s_states, do_h, rows)

    def lb_bwd(i, tm_, d, l):
        t = d * l * (1.0 - l)
        return t, -t

    dlb0, dlb1 = _rowwise("lb_bwd", lb_bwd, [("bc", dlb), ("bc", lb)], [("acc", (1, D_MODEL))] * 2, 1, 1)
    grads["lb_raw"] = jnp.concatenate([dlb0, dlb1], axis=0)

    delta = _attn_delta(do_mla, o32, rows, tm)
    dq_att, dk_att, dv_att = _attn_bwd(q_att, k_att, v_att, do_mla, lse,
                                       jnp.transpose(delta[:, :HEADS]).reshape(HEADS, 1, rows), rows)
    dq_full, dkv, dkr = _rope_bwd_call(dq_att, dk_att, dv_att, cos_t, sin_t, rows, tm)
    dqn = _matmul(dq_full, w["w_q_up"], "nt", F32, "mm_d_qn")
    grads["w_q_up"] = _matmul(qn, dq_full, "tn", BF16, "mm_dw_q_up")
    dkvn = _matmul(dkv, w["w_kv_up"], "nt", F32, "mm_d_kvn")
    grads["w_kv_up"] = _matmul(kvn, dkv, "tn", BF16, "mm_dw_kv_up")

    def latent_norm_bwd(i, t, ql, kl, dq, dk, gq, gk):
        dql, dgq = _rms_bwd(ql, gq, dq)
        dkl, dgk = _rms_bwd(kl, gk, dk)
        return dql, dkl, dgq, dgk

    dq_lat, dkv_lat, grads["g_q_norm"], grads["g_kv_norm"] = _rowwise(
        "latent_norm_bwd", latent_norm_bwd,
        [row(proj, Q_LORA, 0), row(proj, KV_LORA, SEG_KV_LAT // KV_LORA), row(dqn, Q_LORA), row(dkvn, KV_LORA),
         ("bc", p["g_q_norm"]), ("bc", p["g_kv_norm"])],
        [("row", Q_LORA, BF16), ("row", KV_LORA, BF16), ("acc", (1, Q_LORA)), ("acc", (1, KV_LORA))], rows, tm)
    dproj = jnp.concatenate([dq_lat, dkv_lat, dhq, dhf, dhi, dhg, dga, dgb, dkr], axis=1)
    tok = emit({n: grads.pop(n) for n in ("w_q_up", "w_kv_up", "lb_raw", "g_q_norm", "g_kv_norm", "g_hgrn_norm")})
    grads["w_in"] = _matmul(u1, dproj, "tn", BF16, "mm_dw_in", after=tok)
    tok = emit({"w_in": grads.pop("w_in")})
    du1 = _matmul(dproj, w["w_in"], "nt", F32, "mm_d_u1", after=tok)

    def mix_norm_bwd(i, t, h, du, dh, g):
        dx, dg = _rms_bwd(h, g, du)
        return dh + dx, dh + dx, dg

    grad_x, d_prefix, grads["g_mix_norm"] = _rowwise(
        "mix_norm_bwd", mix_norm_bwd, [row(h0, D_MODEL), row(du1, D_MODEL), row(dh1, D_MODEL), ("bc", p["g_mix_norm"])],
        [("tail", D_MODEL, F32), ("head", D_MODEL, F32), ("acc", (1, D_MODEL))], rows, BLOCK)
    grads["meta_tokens"] = d_prefix[PAD_LEN:BLOCK]
    kept.update(grads)
    return loss_acc[0, 0], grad_x, kept


K_ROPE_AT = Q_LORA + KV_LORA
COL_SHARDED = ("w_in", "w_q_up", "w_kv_up", "w_ffn_in", "conv_w", "meta_tokens")
BIG = ("w_in", "w_q_up", "w_kv_up", "w_branch_mla", "w_branch_hgrn", "w_out", "w_ffn_in", "w_ffn_out")
SMALL = ("conv_b", "g_mix_norm", "g_q_norm", "g_kv_norm", "g_hgrn_norm", "g_ffn_norm", "g_final_norm", "lb_raw")


def _unshard(name, stacked):
    if name in COL_SHARDED:
        return jnp.transpose(stacked, (1, 0, 2)).reshape(stacked.shape[1], N_DEV * stacked.shape[2])
    return stacked.reshape(N_DEV * stacked.shape[1], stacked.shape[2])


def _reshard(name, full):
    if full.ndim == 3:
        return full
    if name in COL_SHARDED:
        r, c = full.shape
        return jnp.transpose(full.reshape(r, N_DEV, c // N_DEV), (1, 0, 2))
    return full.reshape(N_DEV, full.shape[0] // N_DEV, full.shape[1])


def _to_kernel_layout(full):
    out = dict(full)
    if "w_in" in full:
        w_in = full["w_in"]
        pad = jnp.zeros((D_MODEL, KR_W - ROPE), w_in.dtype)
        out["w_in"] = jnp.concatenate(
            [w_in[:, :K_ROPE_AT], w_in[:, K_ROPE_AT + ROPE:], w_in[:, K_ROPE_AT:K_ROPE_AT + ROPE], pad], axis=1)
    if "w_q_up" in full:
        wq = full["w_q_up"].reshape(Q_LORA, HEADS, NOPE + ROPE)
        out["w_q_up"] = jnp.pad(wq, ((0, 0), (0, 0), (0, QHEAD_W - NOPE - ROPE))).reshape(Q_LORA, HEADS * QHEAD_W)
    return out


def _from_kernel_layout(grads):
    out = dict(grads)
    if "w_in" in grads:
        g = grads["w_in"]
        out["w_in"] = jnp.concatenate([g[:, :K_ROPE_AT], g[:, SEG_KR:SEG_KR + ROPE], g[:, K_ROPE_AT:SEG_KR]], axis=1)
    if "w_q_up" in grads:
        g = grads["w_q_up"].reshape(Q_LORA, HEADS, QHEAD_W)
        out["w_q_up"] = g[:, :, :NOPE + ROPE].reshape(Q_LORA, HEADS * (NOPE + ROPE))
    return out


MESH_ID = pl.DeviceIdType.MESH
ANY = pl.BlockSpec(memory_space=pl.ANY)


def _slot(dev):
    return 4 * dev[0] + 2 * dev[1] + dev[2]


def _all_gather(shards):
    n = len(shards)

    def body(*refs):
        ins, outs = refs[:n], refs[n:2 * n]
        send_sems, recv_sems, local_sems = refs[2 * n:]
        x, y, c = lax.axis_index("x"), lax.axis_index("y"), lax.axis_index("c")
        me, sibling = (x, y, c), (x, y, 1 - c)
        chips = [(1 - x, y), (x, 1 - y), (1 - x, 1 - y)]

        def copy(a, k, block, to, src=None):
            dst = outs[a].at[_slot(block)]
            return pltpu.make_async_remote_copy(
                src_ref=dst if src is None else src, dst_ref=dst, send_sem=send_sems.at[a, k],
                recv_sem=recv_sems.at[a, k], device_id=to, device_id_type=MESH_ID)

        mine = [pltpu.make_async_copy(ins[a], outs[a].at[_slot(me)], local_sems.at[a]) for a in range(n)]
        for cp in mine:
            cp.start()
        first = []
        for a in range(n):
            first.append(copy(a, 0, me, sibling, src=ins[a]))
            first += [copy(a, 1 + j, me, (*chip, c), src=ins[a]) for j, chip in enumerate(chips)]
        for cp in first:
            cp.start()
        passed = []
        for a in range(n):
            for j, chip in enumerate(chips):
                copy(a, 1 + j, (*chip, c), me).wait_recv()
                fwd = copy(a, 4 + j, (*chip, c), sibling)
                fwd.start()
                passed.append(fwd)
        for a in range(n):
            copy(a, 0, sibling, me).wait_recv()
            for j, chip in enumerate(chips):
                copy(a, 4 + j, (*chip, 1 - c), me).wait_recv()
        for cp in first + passed:
            cp.wait_send()
        for cp in mine:
            cp.wait()

    return pl.pallas_call(
        body,
        name="gather_weights",
        out_shape=[jax.ShapeDtypeStruct((N_DEV,) + s.shape, s.dtype) for s in shards],
        in_specs=[ANY] * n,
        out_specs=[ANY] * n,
        scratch_shapes=[pltpu.SemaphoreType.DMA((n, 7)), pltpu.SemaphoreType.DMA((n, 7)), pltpu.SemaphoreType.DMA((n,))],
    )(*shards)


def _exchange(blocked, replicated):
    nb, n = len(blocked), len(blocked) + len(replicated)
    arrays = list(blocked) + list(replicated)

    def body(*refs):
        ins, outs = refs[:n], refs[n:2 * n]
        send_sems, recv_sems, local_sems = refs[2 * n:]
        x, y, c = lax.axis_index("x"), lax.axis_index("y"), lax.axis_index("c")
        me = (x, y, c)
        peers = [(x, y, 1 - c), (1 - x, y, c), (x, 1 - y, c), (1 - x, 1 - y, c),
                 (1 - x, y, 1 - c), (x, 1 - y, 1 - c), (1 - x, 1 - y, 1 - c)]

        def src_of(a, dev):
            return ins[a].at[_slot(dev)] if a < nb else ins[a]

        def copy(a, k, frm, to):
            return pltpu.make_async_remote_copy(
                src_ref=src_of(a, to), dst_ref=outs[a].at[_slot(frm)], send_sem=send_sems.at[a, k],
                recv_sem=recv_sems.at[a, k], device_id=to, device_id_type=MESH_ID)

        mine = [pltpu.make_async_copy(src_of(a, me), outs[a].at[_slot(me)], local_sems.at[a]) for a in range(n)]
        for cp in mine:
            cp.start()
        sends = [copy(a, k, me, peer) for a in range(n) for k, peer in enumerate(peers)]
        for cp in sends:
            cp.start()
        for a in range(n):
            for k, peer in enumerate(peers):
                copy(a, k, peer, me).wait_recv()
        for cp in sends:
            cp.wait_send()
        for cp in mine:
            cp.wait()

    return pl.pallas_call(
        body,
        name="exchange_grads",
        out_shape=[jax.ShapeDtypeStruct(s.shape, s.dtype) for s in blocked]
        + [jax.ShapeDtypeStruct((N_DEV,) + s.shape, s.dtype) for s in replicated],
        in_specs=[ANY] * n,
        out_specs=[ANY] * n,
        scratch_shapes=[pltpu.SemaphoreType.DMA((n, 7)), pltpu.SemaphoreType.DMA((n, 7)), pltpu.SemaphoreType.DMA((n,))],
    )(*arrays)


ADAMW_BLOCK_ELEMS = 256 * 1024


def _adamw(name, parts, w, m, v, own=None, me=None):
    r, c = w.shape
    tr = _tile(r, max(16, ADAMW_BLOCK_ELEMS // c), 16)

    def body(*refs):
        if own is None:
            p_ref, w_ref, m_ref, v_ref, g_ref, d_ref, nm_ref, nv_ref = refs
            terms = [p_ref[s].astype(F32) for s in range(N_DEV)]
        else:
            me_ref, p_ref, own_ref, w_ref, m_ref, v_ref, g_ref, d_ref, nm_ref, nv_ref = refs
            mine = own_ref[0].astype(F32)
            terms = [jnp.where(me_ref[0] == s, mine, p_ref[s].astype(F32)) for s in range(N_DEV)]
        g = terms[0]
        for s in range(1, N_DEV):
            g = g + terms[s]
        m_new = ADAM_B1 * m_ref[...] + (1.0 - ADAM_B1) * g
        v_new = ADAM_B2 * v_ref[...] + (1.0 - ADAM_B2) * (g * g)
        m_hat = m_new / (1.0 - ADAM_B1 ** ADAM_STEP)
        v_hat = v_new / (1.0 - ADAM_B2 ** ADAM_STEP)
        g_ref[...] = g
        d_ref[...] = -ADAM_LR * (m_hat / (jnp.sqrt(v_hat) + ADAM_EPS) + ADAM_WD * w_ref[...])
        nm_ref[...] = m_new
        nv_ref[...] = v_new

    if own is None:
        blk = pl.BlockSpec((tr, c), lambda i: (i, 0))
        return pl.pallas_call(
            body,
            name="adamw_" + name,
            out_shape=[jax.ShapeDtypeStruct((r, c), F32)] * 4,
            grid=(r // tr,),
            in_specs=[pl.BlockSpec((N_DEV, tr, c), lambda i: (0, i, 0)), blk, blk, blk],
            out_specs=[blk] * 4,
            compiler_params=_params(("parallel",)),
        )(parts, w, m, v)
    blk = pl.BlockSpec((tr, c), lambda i, me_ref: (i, 0))
    own_at = (lambda i, me_ref: (me_ref[0], i, 0)) if own.shape[0] == N_DEV else (lambda i, me_ref: (0, i, 0))
    return pl.pallas_call(
        body,
        name="adamw_" + name,
        out_shape=[jax.ShapeDtypeStruct((r, c), F32)] * 4,
        grid_spec=pltpu.PrefetchScalarGridSpec(
            num_scalar_prefetch=1,
            grid=(r // tr,),
            in_specs=[pl.BlockSpec((N_DEV, tr, c), lambda i, me_ref: (0, i, 0)), pl.BlockSpec((1, tr, c), own_at),
                      blk, blk, blk],
            out_specs=[blk] * 4),
        compiler_params=_params(("parallel",)),
    )(me, parts, own, w, m, v)


HBM_SPEC = pl.BlockSpec(memory_space=pltpu.HBM)
SEM_SPEC = pl.BlockSpec(memory_space=pltpu.SEMAPHORE)
SIDE_EFFECT = pltpu.SideEffectType.DATAFLOW_SIDE_EFFECTING
N_PEERS = N_DEV - 1


def _peers(x, y, c):
    return [(x, y, 1 - c), (1 - x, y, c), (x, 1 - y, c), (1 - x, 1 - y, c),
            (1 - x, y, 1 - c), (x, 1 - y, 1 - c), (1 - x, 1 - y, 1 - c)]


def _split_copy(srcs, lands, blocked, send_sems, recv_sems, a, k, frm, to):
    src = srcs[a].at[_slot(to)] if blocked[a] else srcs[a]
    return pltpu.make_async_remote_copy(
        src_ref=src, dst_ref=lands[a].at[_slot(frm)], send_sem=send_sems.at[a * N_PEERS + k],
        recv_sem=recv_sems.at[a * N_PEERS + k],
        device_id=to, device_id_type=MESH_ID)


def _exchange_start(name, srcs, lands, blocked, after=()):
    n = len(srcs)
    after = list(after)

    def body(*refs):
        src_refs, land_refs = refs[:n], refs[n:2 * n]
        send_sems, recv_sems = refs[2 * n + len(after)], refs[2 * n + len(after) + 1]
        token = refs[-1]
        x, y, c = lax.axis_index("x"), lax.axis_index("y"), lax.axis_index("c")
        for a in range(n):
            for k, peer in enumerate(_peers(x, y, c)):
                _split_copy(src_refs, land_refs, blocked, send_sems, recv_sems, a, k, (x, y, c), peer).start()
        token[...] = jnp.zeros_like(token)

    thru = [pltpu.HBM(s.shape, s.dtype) for s in list(srcs) + list(lands)]
    res = pl.pallas_call(
        body,
        name=name,
        out_shape=(pltpu.SemaphoreType.DMA((n * N_PEERS,)), pltpu.SemaphoreType.DMA((n * N_PEERS,)), *thru,
                   jax.ShapeDtypeStruct((8, LANE), F32)),
        in_specs=[HBM_SPEC] * (2 * n) + [pl.BlockSpec(memory_space=pl.ANY)] * len(after),
        out_specs=(SEM_SPEC, SEM_SPEC, *([HBM_SPEC] * (2 * n)), pl.BlockSpec(memory_space=pltpu.VMEM)),
        input_output_aliases={i: 2 + i for i in range(2 * n)},
        compiler_params=pltpu.CompilerParams(has_side_effects=SIDE_EFFECT),
    )(*[pltpu.with_memory_space_constraint(s, pltpu.HBM) for s in list(srcs) + list(lands)], *after)
    return res[0], res[1], res[2:2 + n], res[2 + n:2 + 2 * n], res[-1]


def _exchange_wait(name, send_sems, recv_sems, srcs, lands, blocked, after):
    n, n_after = len(srcs), len(after)

    def body(*refs):
        src_refs, land_refs = refs[:n], refs[n:2 * n]
        send, recv = refs[2 * n], refs[2 * n + 1]
        x, y, c = lax.axis_index("x"), lax.axis_index("y"), lax.axis_index("c")
        for a in range(n):
            for k, peer in enumerate(_peers(x, y, c)):
                _split_copy(src_refs, land_refs, blocked, send, recv, a, k, (x, y, c), peer).wait_send()
                _split_copy(src_refs, land_refs, blocked, send, recv, a, k, peer, (x, y, c)).wait_recv()

    res = pl.pallas_call(
        body,
        name=name,
        out_shape=tuple(pltpu.HBM(s.shape, s.dtype) for s in list(srcs) + list(lands)),
        in_specs=[HBM_SPEC] * (2 * n) + [SEM_SPEC, SEM_SPEC] + [pl.BlockSpec(memory_space=pl.ANY)] * n_after,
        out_specs=tuple([HBM_SPEC] * (2 * n)),
        input_output_aliases={i: i for i in range(2 * n)},
        compiler_params=pltpu.CompilerParams(has_side_effects=SIDE_EFFECT),
    )(*srcs, *lands, send_sems, recv_sems, *after)
    return res[:n], res[n:]


class _LazyWeights:
    def __init__(self):
        self.ready, self.groups, self.hints = {}, {}, {}

    def add_group(self, wait_name, names, send, recv, srcs, lands):
        for n in names:
            self.groups[n] = (wait_name, names, send, recv, srcs, lands)

    def hint(self, name, after):
        self.hints[self.groups[name][0]] = after

    def __getitem__(self, name):
        if name not in self.ready:
            wait_name, names, send, recv, srcs, lands = self.groups[name]
            after = [self.hints[wait_name]] if wait_name in self.hints else []
            _, whole = _exchange_wait(wait_name, send, recv, srcs, lands, [False] * len(names), after)
            for n, stacked in zip(names, whole):
                self.ready[n] = _to_kernel_layout({n: _unshard(n, stacked)})[n]
        return self.ready[name]


def kernel(x, positions, meta_tokens, w_in, w_q_up, w_kv_up, w_branch_mla, w_branch_hgrn, w_out, w_ffn_in, w_ffn_out, conv_w, conv_b, g_mix_norm, g_q_norm, g_kv_norm, g_hgrn_norm, g_ffn_norm, g_final_norm, lb_raw, loss_target, m_meta_tokens, m_w_in, m_w_q_up, m_w_kv_up, m_w_branch_mla, m_w_branch_hgrn, m_w_out, m_w_ffn_in, m_w_ffn_out, m_conv_w, m_conv_b, m_g_mix_norm, m_g_q_norm, m_g_kv_norm, m_g_hgrn_norm, m_g_ffn_norm, m_g_final_norm, m_lb_raw, v_meta_tokens, v_w_in, v_w_q_up, v_w_kv_up, v_w_branch_mla, v_w_branch_hgrn, v_w_out, v_w_ffn_in, v_w_ffn_out, v_conv_w, v_conv_b, v_g_mix_norm, v_g_q_norm, v_g_kv_norm, v_g_hgrn_norm, v_g_ffn_norm, v_g_final_norm, v_lb_raw):
    local = dict(zip(
        ("meta_tokens", "w_in", "w_q_up", "w_kv_up", "w_branch_mla", "w_branch_hgrn", "w_out", "w_ffn_in", "w_ffn_out",
         "conv_w", "conv_b", "g_mix_norm", "g_q_norm", "g_kv_norm", "g_hgrn_norm", "g_ffn_norm", "g_final_norm", "lb_raw"),
        (meta_tokens, w_in, w_q_up, w_kv_up, w_branch_mla, w_branch_hgrn, w_out, w_ffn_in, w_ffn_out,
         conv_w, conv_b, g_mix_norm, g_q_norm, g_kv_norm, g_hgrn_norm, g_ffn_norm, g_final_norm, lb_raw)))
    mom_m = dict(zip(local, (m_meta_tokens, m_w_in, m_w_q_up, m_w_kv_up, m_w_branch_mla, m_w_branch_hgrn, m_w_out, m_w_ffn_in,
                             m_w_ffn_out, m_conv_w, m_conv_b, m_g_mix_norm, m_g_q_norm, m_g_kv_norm, m_g_hgrn_norm,
                             m_g_ffn_norm, m_g_final_norm, m_lb_raw)))
    mom_v = dict(zip(local, (v_meta_tokens, v_w_in, v_w_q_up, v_w_kv_up, v_w_branch_mla, v_w_branch_hgrn, v_w_out, v_w_ffn_in,
                             v_w_ffn_out, v_conv_w, v_conv_b, v_g_mix_norm, v_g_q_norm, v_g_kv_norm, v_g_hgrn_norm,
                             v_g_ffn_norm, v_g_final_norm, v_lb_raw)))
    sharded = BIG + ("conv_w", "meta_tokens")

    def shard2d(name, arr):
        return arr.reshape(arr.shape[-2:]) if name != "meta_tokens" else arr

    def as2d(name, arr):
        return arr.reshape(1, -1) if arr.ndim == 1 else shard2d(name, arr)

    me = 4 * lax.axis_index("x") + 2 * lax.axis_index("y") + lax.axis_index("c")

    def landing(own):
        zone = lax.empty((N_DEV,) + own.shape[1:], own.dtype)
        return lax.dynamic_update_slice_in_dim(zone, own, me, 0)

    shards = {n: shard2d(n, local[n]).astype(BF16) for n in BIG}
    shards.update({n: shard2d(n, local[n]) for n in ("conv_w", "meta_tokens")})
    full = _LazyWeights()
    first = ("w_in", "meta_tokens")
    gathered = _all_gather([shards[n] for n in first])
    for n, g in zip(first, gathered):
        full.ready[n] = _to_kernel_layout({n: _unshard(n, g)})[n]
    later = (("w_q_up", "w_kv_up"), ("w_branch_mla", "w_branch_hgrn", "w_out", "w_ffn_in", "w_ffn_out", "conv_w"))
    for k, names in enumerate(later):
        srcs = [shards[n] for n in names]
        send, recv, srcs_thru, lands_thru, _ = _exchange_start(
            f"gather_start_{k}", srcs, [landing(s[None]) for s in srcs], [False] * len(names), after=[gathered[0]])
        full.add_group(f"gather_wait_{k}", names, send, recv, srcs_thru, lands_thru)
    small = {n: local[n] for n in SMALL}

    started = []

    def sources(group):
        group = _from_kernel_layout(group)
        names = list(group)
        blocked = [n in sharded for n in names]
        srcs = [_reshard(n, group[n]) if b else as2d(n, group[n]) for n, b in zip(names, blocked)]
        return names, blocked, srcs

    def emit(group):
        names, blocked, srcs = sources(group)
        lands = [lax.empty((N_DEV,) + (s.shape[1:] if b else s.shape), s.dtype) for s, b in zip(srcs, blocked)]
        k = len(started)
        send, recv, srcs_thru, lands_thru, token = _exchange_start(f"exchange_start_{k}", srcs, lands, blocked)
        started.append((names, blocked, send, recv, srcs_thru, lands_thru))
        return token

    loss, grad_x, last = _local_step(x[0], positions[0], loss_target[0], full, small, emit)

    out = {}

    me_arr = me.astype(jnp.int32).reshape(1)

    def update(names, parts, owns=None):
        for k, (n, part) in enumerate(zip(names, parts)):
            own = None if owns is None else (owns[k] if owns[k].ndim == 3 else owns[k][None])
            res = _adamw(n, part, as2d(n, local[n]), as2d(n, mom_m[n]), as2d(n, mom_v[n]), own,
                         None if owns is None else me_arr)
            out[n] = [r.reshape(local[n].shape) for r in res]

    after = [last["g_mix_norm"]]
    for k, (names, blocked, send, recv, srcs_thru, lands_thru) in enumerate(started):
        srcs_done, parts = _exchange_wait(f"exchange_wait_{k}", send, recv, srcs_thru, lands_thru, blocked, after)
        update(names, parts, srcs_done)
        after = [out[names[0]][0]]
    names, blocked, srcs = sources(last)
    in_blocks = [(n, s) for n, s, b in zip(names, srcs, blocked) if b]
    whole = [(n, s) for n, s, b in zip(names, srcs, blocked) if not b]
    update([n for n, _ in in_blocks + whole], _exchange([s for _, s in in_blocks], [s for _, s in whole]))

    loss = lax.psum(loss, ("x", "y", "c"))
    order = tuple(local)
    return (loss, grad_x[None], *[out[n][0] for n in order], *[out[n][1] for n in order],
            *[out[n][2] for n in order], *[out[n][3] for n in order])
```
